```python
import jax
import jax.numpy as jnp
from jax import lax
import numpy as np

D_MODEL = 2048
BATCH = 2
SEQ = 4096
DEPTH = 1

HEAD_DIM = 128
FOX_HEADS = 6
DSA_HEADS = 6
CROSS_HEADS = 4
N_MEM = 256
FOX_WIDTH = FOX_HEADS * HEAD_DIM
DSA_WIDTH = DSA_HEADS * HEAD_DIM
CROSS_WIDTH = CROSS_HEADS * HEAD_DIM
N_BRANCHES = 3
ROPE_THETA = 500000.0
ROT_DIM = HEAD_DIM // 4
IDX_HEADS = 16
IDX_DIM = 64
IDX_ROT_DIM = IDX_DIM // 4
INDEX_TOPK = 256
Q_BLOCK = 128
N_GROUPS = 4
EXPERTS_PER_GROUP = 8
N_EXPERTS = N_GROUPS * EXPERTS_PER_GROUP
MOE_TOP_K = 2
EXPERT_FF = 1024
MOE_BLOCK = 128
EPS = 1e-6
IN_SPLITS = (FOX_WIDTH, FOX_WIDTH, FOX_WIDTH, FOX_HEADS,
             DSA_WIDTH, DSA_WIDTH, DSA_WIDTH,
             IDX_HEADS * IDX_DIM, IDX_HEADS, IDX_DIM,
             CROSS_WIDTH, N_BRANCHES * D_MODEL)
IN_WIDTH = sum(IN_SPLITS)

kernel_name = 'hybrid_fox_dsa_xattn_hmoe'


def rms_norm(x, g):
    xf = x.astype(jnp.float32)
    y = xf * lax.rsqrt(jnp.mean(xf * xf, axis=-1, keepdims=True) + EPS)
    return (y * g.astype(jnp.float32)).astype(x.dtype)


def partial_rope(x, positions, rot_dim):
    half = rot_dim // 2
    inv_freq = jnp.power(jnp.float32(ROPE_THETA), -jnp.arange(half, dtype=jnp.float32) * 2.0 / rot_dim)
    ang = positions.astype(jnp.float32)[..., None] * inv_freq
    cos = jnp.cos(ang)[:, :, None, :]
    sin = jnp.sin(ang)[:, :, None, :]
    xf = x.astype(jnp.float32)
    x1, x2, rest = xf[..., :half], xf[..., half:rot_dim], xf[..., rot_dim:]
    out = jnp.concatenate([x1 * cos - x2 * sin, x2 * cos + x1 * sin, rest], axis=-1)
    return out.astype(x.dtype)


def to_blocks(a):
    b, s = a.shape[:2]
    return jnp.moveaxis(a.reshape((b, s // Q_BLOCK, Q_BLOCK) + a.shape[2:]), 1, 0)


def from_blocks(a):
    a = jnp.moveaxis(a, 0, 1)
    return a.reshape((a.shape[0], a.shape[1] * a.shape[2]) + a.shape[3:])


def fox_attention(q, k, v, log_f):
    s_len, dh = q.shape[1], q.shape[3]
    cum = jnp.cumsum(log_f.astype(jnp.float32), axis=1)
    cum_k = jnp.moveaxis(cum, 1, 2)
    kpos = jnp.arange(s_len)
    qpos = kpos.reshape(-1, Q_BLOCK)
    scale = dh ** -0.5

    def block(args):
        qb, cqb, qp = args
        s = jnp.einsum('bqhd,bkhd->bhqk', qb, k, preferred_element_type=jnp.float32) * scale
        s = s + jnp.moveaxis(cqb, 1, 2)[..., None] - cum_k[:, :, None, :]
        s = jnp.where(kpos[None, :] <= qp[:, None], s, -jnp.inf)
        p = jax.nn.softmax(s, axis=-1)
        return jnp.einsum('bhqk,bkhd->bqhd', p.astype(v.dtype), v)

    return from_blocks(lax.map(block, (to_blocks(q), to_blocks(cum), qpos)))


def dsa_attention(q, k, v, q_idx, k_idx, w_idx):
    s_len, dh = q.shape[1], q.shape[3]
    topk = min(INDEX_TOPK, s_len // 4)
    kpos = jnp.arange(s_len)
    qpos = kpos.reshape(-1, Q_BLOCK)
    scale = dh ** -0.5
    gather = jax.vmap(lambda kb, ib: kb[ib])

    def block(args):
        qb, qib, wb, qp = args
        logits = jnp.einsum('bqjd,bkd->bqjk', qib, k_idx, preferred_element_type=jnp.float32) * IDX_DIM ** -0.5
        score = jnp.einsum('bqj,bqjk->bqk', wb.astype(jnp.float32), jax.nn.relu(logits))
        score = jnp.where((kpos[None, :] <= qp[:, None])[None], score, -jnp.inf)
        _, idx = lax.top_k(score, topk)
        valid = idx <= qp[None, :, None]
        kg = gather(k, idx)
        vg = gather(v, idx)
        s = jnp.einsum('bqhd,bqkhd->bhqk', qb, kg, preferred_element_type=jnp.float32) * scale
        s = jnp.where(valid[:, None], s, -jnp.inf)
        p = jax.nn.softmax(s, axis=-1)
        return jnp.einsum('bhqk,bqkhd->bqhd', p.astype(vg.dtype), vg)

    blocks = (to_blocks(q), to_blocks(q_idx), to_blocks(w_idx), qpos)
    return from_blocks(lax.map(block, blocks))


def cross_attention(q, k_mem, v_mem):
    s = jnp.einsum('bqhd,bmhd->bhqm', q, k_mem, preferred_element_type=jnp.float32) * q.shape[-1] ** -0.5
    p = jax.nn.softmax(s, axis=-1)
    return jnp.einsum('bhqm,bmhd->bqhd', p.astype(v_mem.dtype), v_mem)


def hier_moe(h, wg, bg, we, be, w_gate, w_up, w_down):
    b, s, d = h.shape
    n = b * s
    hf = h.reshape(n, d)
    g_logits = jnp.einsum('nd,dg->ng', hf, wg, preferred_element_type=jnp.float32) + bg.astype(jnp.float32)
    p_group = jax.nn.softmax(g_logits, axis=-1)
    grp = jnp.argmax(g_logits, axis=-1).astype(jnp.int32)
    p_sel = jnp.take_along_axis(p_group, grp[:, None], axis=1)
    e_logits = jnp.einsum('nd,de->ne', hf, we, preferred_element_type=jnp.float32) + be.astype(jnp.float32)
    e_in = jnp.take_along_axis(e_logits.reshape(n, N_GROUPS, EXPERTS_PER_GROUP), grp[:, None, None], axis=1)[:, 0]
    top_l, top_j = lax.top_k(e_in, MOE_TOP_K)
    gate = p_sel * jax.nn.softmax(top_l, axis=-1)
    eid = (grp[:, None] * EXPERTS_PER_GROUP + top_j).reshape(-1).astype(jnp.int32)
    tok = jnp.repeat(jnp.arange(n, dtype=jnp.int32), MOE_TOP_K)
    gw = gate.reshape(-1)
    a = n * MOE_TOP_K
    n_slots = -(-(a + N_EXPERTS * (MOE_BLOCK - 1)) // MOE_BLOCK) * MOE_BLOCK
    n_blocks = n_slots // MOE_BLOCK
    counts = jnp.zeros((N_EXPERTS,), jnp.int32).at[eid].add(1)
    padded = (counts + MOE_BLOCK - 1) // MOE_BLOCK * MOE_BLOCK
    pad_end = jnp.cumsum(padded)
    pad_start = pad_end - padded
    seg_start = jnp.cumsum(counts) - counts
    order = jnp.argsort(eid, stable=True)
    se = eid[order]
    dest = pad_start[se] + jnp.arange(a, dtype=jnp.int32) - seg_start[se]
    slot_tok = jnp.full((n_slots,), n, jnp.int32).at[dest].set(tok[order])
    slot_gate = jnp.zeros((n_slots,), jnp.float32).at[dest].set(gw[order])
    block_e = jnp.minimum(jnp.searchsorted(pad_end, jnp.arange(n_blocks, dtype=jnp.int32) * MOE_BLOCK, side='right'), N_EXPERTS - 1)
    h_pad = jnp.concatenate([hf, jnp.zeros((1, d), hf.dtype)], axis=0)
    xs = h_pad[slot_tok].reshape(n_blocks, MOE_BLOCK, d)

    def expert_block(args):
        xb, e = args
        hid = jax.nn.silu(xb @ w_gate[e]) * (xb @ w_up[e])
        return hid @ w_down[e]

    ys = lax.map(expert_block, (xs, block_e)).reshape(n_slots, d)
    out = jax.ops.segment_sum(ys * slot_gate[:, None].astype(ys.dtype), slot_tok, num_segments=n + 1)
    return out[:n].reshape(b, s, d)


def hybrid_layer(x, mem, positions, attn_norm_g, mem_norm_g, w_in, fox_forget_b,
                 fox_q_norm_g, fox_k_norm_g, dsa_q_norm_g, dsa_k_norm_g, idx_k_norm_g,
                 cross_q_norm_g, cross_k_norm_g, w_mem_kv, w_branch_fox, w_branch_dsa,
                 w_branch_cross, w_out, ffn_norm_g, router_group_w, router_group_b,
                 router_expert_w, router_expert_b, expert_w_gate, expert_w_up, expert_w_down):
    b, s, _ = x.shape
    h = rms_norm(x, attn_norm_g)
    proj = jnp.einsum('bsd,de->bse', h, w_in)
    offsets = [int(o) for o in np.cumsum(IN_SPLITS)[:-1]]
    fq, fk, fv, f_logit, dq, dk, dv, iq, iw, ik, cq, gates = jnp.split(proj, offsets, axis=-1)

    def heads(t, n_heads, dim):
        return t.reshape(t.shape[0], t.shape[1], n_heads, dim)

    fq = rms_norm(heads(fq, FOX_HEADS, HEAD_DIM), fox_q_norm_g)
    fk = rms_norm(heads(fk, FOX_HEADS, HEAD_DIM), fox_k_norm_g)
    log_f = jax.nn.log_sigmoid(f_logit.astype(jnp.float32) + fox_forget_b.astype(jnp.float32))
    o_fox = fox_attention(fq, fk, heads(fv, FOX_HEADS, HEAD_DIM), log_f).reshape(b, s, FOX_WIDTH)

    dq = partial_rope(rms_norm(heads(dq, DSA_HEADS, HEAD_DIM), dsa_q_norm_g), positions, ROT_DIM)
    dk = partial_rope(rms_norm(heads(dk, DSA_HEADS, HEAD_DIM), dsa_k_norm_g), positions, ROT_DIM)
    iq = partial_rope(heads(iq, IDX_HEADS, IDX_DIM), positions, IDX_ROT_DIM)
    ik = partial_rope(rms_norm(ik, idx_k_norm_g)[:, :, None, :], positions, IDX_ROT_DIM)[:, :, 0]
    iw = iw * IDX_HEADS ** -0.5
    o_dsa = dsa_attention(dq, dk, heads(dv, DSA_HEADS, HEAD_DIM), iq, ik, iw).reshape(b, s, DSA_WIDTH)

    m = rms_norm(mem, mem_norm_g)
    mk, mv = jnp.split(jnp.einsum('bmd,de->bme', m, w_mem_kv), 2, axis=-1)
    mk = rms_norm(heads(mk, CROSS_HEADS, HEAD_DIM), cross_k_norm_g)
    cq = rms_norm(heads(cq, CROSS_HEADS, HEAD_DIM), cross_q_norm_g)
    o_cross = cross_attention(cq, mk, heads(mv, CROSS_HEADS, HEAD_DIM)).reshape(b, s, CROSS_WIDTH)

    g_fox, g_dsa, g_cross = jnp.split(jax.nn.sigmoid(gates), N_BRANCHES, axis=-1)
    merged = (g_fox * (o_fox @ w_branch_fox)
              + g_dsa * (o_dsa @ w_branch_dsa)
              + g_cross * (o_cross @ w_branch_cross))
    x = x + merged @ w_out

    x = x + hier_moe(rms_norm(x, ffn_norm_g), router_group_w, router_group_b, router_expert_w,
                     router_expert_b, expert_w_gate, expert_w_up, expert_w_down)
    return x


def setup_inputs(seed: int = 0) -> dict:
    key = jax.random.key(seed)
    ks = jax.random.split(key, 32)
    f32 = jnp.float32

    def w(k, shape, fan_in):
        return jax.random.normal(k, shape, f32) * fan_in ** -0.5

    def gain(k, shape):
        return 1.0 + 0.05 * jax.random.normal(k, shape, f32)

    L, D = DEPTH, D_MODEL
    offset = jax.random.randint(ks[2], (BATCH, 1), 0, 1024, dtype=jnp.int32)
    return {
        'x': jax.random.normal(ks[0], (BATCH, SEQ, D), f32),
        'mem': jax.random.normal(ks[1], (BATCH, N_MEM, D), f32),
        'positions': offset + jnp.arange(SEQ, dtype=jnp.int32)[None, :],
        'attn_norm_g': gain(ks[3], (L, D)),
        'mem_norm_g': gain(ks[4], (L, D)),
        'w_in': w(ks[5], (L, D, IN_WIDTH), D),
        'fox_forget_b': jax.random.uniform(ks[6], (L, FOX_HEADS), f32, 1.0, 6.0),
        'fox_q_norm_g': gain(ks[7], (L, HEAD_DIM)),
        'fox_k_norm_g': gain(ks[8], (L, HEAD_DIM)),
        'dsa_q_norm_g': gain(ks[9], (L, HEAD_DIM)),
        'dsa_k_norm_g': gain(ks[10], (L, HEAD_DIM)),
        'idx_k_norm_g': gain(ks[11], (L, IDX_DIM)),
        'cross_q_norm_g': gain(ks[12], (L, HEAD_DIM)),
        'cross_k_norm_g': gain(ks[13], (L, HEAD_DIM)),
        'w_mem_kv': w(ks[14], (L, D, 2 * CROSS_WIDTH), D),
        'w_branch_fox': w(ks[15], (L, FOX_WIDTH, D), FOX_WIDTH),
        'w_branch_dsa': w(ks[16], (L, DSA_WIDTH, D), DSA_WIDTH),
        'w_branch_cross': w(ks[17], (L, CROSS_WIDTH, D), CROSS_WIDTH),
        'w_out': w(ks[18], (L, D, D), D),
        'ffn_norm_g': gain(ks[19], (L, D)),
        'router_group_w': w(ks[20], (L, D, N_GROUPS), D),
        'router_group_b': 0.01 * jax.random.normal(ks[21], (L, N_GROUPS), f32),
        'router_expert_w': w(ks[22], (L, D, N_EXPERTS), D),
        'router_expert_b': 0.01 * jax.random.normal(ks[23], (L, N_EXPERTS), f32),
        'expert_w_gate': w(ks[24], (L, N_EXPERTS, D, EXPERT_FF), D),
        'expert_w_up': w(ks[25], (L, N_EXPERTS, D, EXPERT_FF), D),
        'expert_w_down': w(ks[26], (L, N_EXPERTS, EXPERT_FF, D), EXPERT_FF),
    }


def reference(x, mem, positions, attn_norm_g, mem_norm_g, w_in, fox_forget_b,
              fox_q_norm_g, fox_k_norm_g, dsa_q_norm_g, dsa_k_norm_g, idx_k_norm_g,
              cross_q_norm_g, cross_k_norm_g, w_mem_kv, w_branch_fox, w_branch_dsa,
              w_branch_cross, w_out, ffn_norm_g, router_group_w, router_group_b,
              router_expert_w, router_expert_b, expert_w_gate, expert_w_up, expert_w_down):
    for layer in range(DEPTH):
        x = hybrid_layer(
            x, mem, positions, attn_norm_g[layer], mem_norm_g[layer], w_in[layer],
            fox_forget_b[layer], fox_q_norm_g[layer], fox_k_norm_g[layer],
            dsa_q_norm_g[layer], dsa_k_norm_g[layer], idx_k_norm_g[layer],
            cross_q_norm_g[layer], cross_k_norm_g[layer], w_mem_kv[layer],
            w_branch_fox[layer], w_branch_dsa[layer], w_branch_cross[layer], w_out[layer],
            ffn_norm_g[layer], router_group_w[layer], router_group_b[layer],
            router_expert_w[layer], router_expert_b[layer], expert_w_gate[layer],
            expert_w_up[layer], expert_w_down[layer])
    return x
```

```python
import functools

import jax
import jax.numpy as jnp
from jax import lax
from jax.experimental import pallas as pl
from jax.experimental.pallas import tpu as pltpu

F32 = jnp.float32
BF16 = jnp.bfloat16

LANES = 128
HEAD_DIM = 128
FOX_HEADS = 6
DSA_HEADS = 6
CROSS_HEADS = 4
IDX_HEADS = 16
IDX_DIM = 64
ROPE_THETA = 500000.0
ROT_DIM = HEAD_DIM // 4
IDX_ROT_DIM = IDX_DIM // 4
INDEX_TOPK = 256
N_GROUPS = 4
EXPERTS_PER_GROUP = 8
N_EXPERTS = N_GROUPS * EXPERTS_PER_GROUP
MOE_TOP_K = 2
EPS = 1e-6

NEG = -1e30
INT_MIN = -(2 ** 31)
VMEM_LIMIT = 56 * 1024 * 1024

ATT_TQ = 256
ATT_TK = 256
MOE_BM = 256
MOE_GROUP_BLOCKS = 4
MOE_FC = 256


def _cparams(*sem):
    return pltpu.CompilerParams(dimension_semantics=sem, vmem_limit_bytes=VMEM_LIMIT)


def _rmsnorm_body(x_ref, g_ref, o_ref):
    x = x_ref[...]
    r = lax.rsqrt(jnp.mean(x * x, axis=-1, keepdims=True) + EPS)
    o_ref[...] = ((x * r) * g_ref[...]).astype(o_ref.dtype)


def _rmsnorm(x2d, g, out_dtype, tm):
    m, d = x2d.shape
    return pl.pallas_call(
        _rmsnorm_body,
        grid=(m // tm,),
        in_specs=[pl.BlockSpec((tm, d), lambda i: (i, 0)),
                  pl.BlockSpec((1, d), lambda i: (0, 0))],
        out_specs=pl.BlockSpec((tm, d), lambda i: (i, 0)),
        out_shape=jax.ShapeDtypeStruct((m, d), out_dtype),
        compiler_params=_cparams("parallel"),
        name="rmsnorm",
    )(x2d, g.reshape(1, d))


def _rope_table_body(pos_ref, f_ref, sgn_ref, c_ref, s_ref):
    ang = pos_ref[...] * f_ref[...]
    c_ref[...] = jnp.cos(ang)
    s_ref[...] = jnp.sin(ang) * sgn_ref[...]


def _rope_tables(pos_col, freq_lane, sign_lane, tm):
    n = pos_col.shape[0]
    row = pl.BlockSpec((1, LANES), lambda i: (0, 0))
    tab = pl.BlockSpec((tm, LANES), lambda i: (i, 0))
    return pl.pallas_call(
        _rope_table_body,
        grid=(n // tm,),
        in_specs=[pl.BlockSpec((tm, 1), lambda i: (i, 0)), row, row],
        out_specs=[tab, tab],
        out_shape=[jax.ShapeDtypeStruct((n, LANES), F32)] * 2,
        compiler_params=_cparams("parallel"),
        name="rope_tables",
    )(pos_col, freq_lane.reshape(1, LANES), sign_lane.reshape(1, LANES))


def _rope_lane_patterns(rot_dim, period):
    half = rot_dim // 2
    inv_freq = jnp.power(jnp.float32(ROPE_THETA), -jnp.arange(half, dtype=F32) * 2.0 / rot_dim)
    lane = jnp.arange(LANES) % period
    freq = jnp.where(lane < rot_dim, inv_freq[lane % half], 0.0).astype(F32)
    sign = jnp.where(lane < half, -1.0, jnp.where(lane < rot_dim, 1.0, 0.0)).astype(F32)
    return freq, sign


def _apply_rope(a, c, s, half, period):
    lane = lax.broadcasted_iota(jnp.int32, a.shape, 1)
    first = (lane & (period - 1)) < half
    partner = jnp.where(first, pltpu.roll(a, LANES - half, 1), pltpu.roll(a, half, 1))
    return a * c + partner * s


def _proj_body(*refs, mode, n_lane_groups):
    h_ref, w_ref = refs[0], refs[1]
    o_ref = refs[-1]
    acc = jnp.dot(h_ref[...], w_ref[...], preferred_element_type=F32)
    if mode == "raw":
        o_ref[...] = acc.astype(o_ref.dtype)
        return
    for g in range(n_lane_groups):
        sl = slice(g * LANES, (g + 1) * LANES)
        a = acc[:, sl]
        if mode in ("norm", "norm_rope"):
            g_ref = refs[2]
            a = (a * lax.rsqrt(jnp.mean(a * a, axis=-1, keepdims=True) + EPS)) * g_ref[:, sl]
        if mode == "norm_rope":
            a = _apply_rope(a, refs[3][...], refs[4][...], ROT_DIM // 2, HEAD_DIM)
        if mode == "rope_idx":
            a = _apply_rope(a, refs[2][...], refs[3][...], IDX_ROT_DIM // 2, IDX_DIM)
        o_ref[:, sl] = a.astype(o_ref.dtype)


def _proj(h, w, mode, out_dtype, tm, tn, gains=None, tables=None):
    m, k = h.shape
    n = w.shape[1]
    in_specs = [pl.BlockSpec((tm, k), lambda j, i: (i, 0)),
                pl.BlockSpec((k, tn), lambda j, i: (0, j))]
    args = [h, w]
    if gains is not None:
        in_specs.append(pl.BlockSpec((1, tn), lambda j, i: (0, j)))
        args.append(gains.reshape(1, n))
    if tables is not None:
        for t in tables:
            in_specs.append(pl.BlockSpec((tm, LANES), lambda j, i: (i, 0)))
            args.append(t)
    return pl.pallas_call(
        functools.partial(_proj_body, mode=mode, n_lane_groups=tn // LANES),
        grid=(n // tn, m // tm),
        in_specs=in_specs,
        out_specs=pl.BlockSpec((tm, tn), lambda j, i: (i, j)),
        out_shape=jax.ShapeDtypeStruct((m, n), out_dtype),
        compiler_params=_cparams("parallel", "parallel"),
        name="proj_" + mode,
    )(*args)


SMALL_IK = 0
SMALL_IW = 64
SMALL_F = 80


def _small_body(x_ref, gk_ref, fb_ref, c_ref, s_ref, ika_ref, ikb_ref, wl_ref, lf_ref):
    x = x_ref[...]
    lane = lax.broadcasted_iota(jnp.int32, x.shape, 1)
    is_ik = lane < IDX_DIM
    ik = jnp.where(is_ik, x, 0.0)
    ms = jnp.sum(ik * ik, axis=-1, keepdims=True) * (1.0 / IDX_DIM)
    ik = (ik * lax.rsqrt(ms + EPS)) * gk_ref[...]
    ik = _apply_rope(ik, c_ref[...], s_ref[...], IDX_ROT_DIM // 2, IDX_DIM)
    ik = jnp.where(is_ik, ik, 0.0)
    ika_ref[...] = ik.astype(BF16)
    ikb_ref[...] = pltpu.roll(ik, IDX_DIM, 1).astype(BF16)
    wl_ref[...] = pltpu.roll(x, LANES - SMALL_IW, 1) * (2.0 ** -5)
    z = x + fb_ref[...]
    log_f = jnp.minimum(z, 0.0) - jnp.log1p(jnp.exp(-jnp.abs(z)))
    lf_ref[...] = pltpu.roll(log_f, LANES - SMALL_F, 1)


def _small(x, idx_k_gain_lane, forget_b_lane, ci, si, tm):
    n = x.shape[0]
    row = pl.BlockSpec((1, LANES), lambda i: (0, 0))
    tab = pl.BlockSpec((tm, LANES), lambda i: (i, 0))
    return pl.pallas_call(
        _small_body,
        grid=(n // tm,),
        in_specs=[tab, row, row, tab, tab],
        out_specs=[tab, tab, tab, tab],
        out_shape=[jax.ShapeDtypeStruct((n, LANES), BF16), jax.ShapeDtypeStruct((n, LANES), BF16),
                   jax.ShapeDtypeStruct((n, LANES), F32), jax.ShapeDtypeStruct((n, LANES), F32)],
        compiler_params=_cparams("parallel"),
        name="small_cols",
    )(x, idx_k_gain_lane.reshape(1, LANES), forget_b_lane.reshape(1, LANES), ci, si)


def _cumsum_body(col_ref, row_ref, ccol_ref, crow_ref, *, seq):
    c = col_ref[...]
    r = row_ref[...]
    ci = lax.broadcasted_iota(jnp.int32, c.shape, 0)
    ri = lax.broadcasted_iota(jnp.int32, r.shape, 1)
    sh = 1
    while sh < seq:
        c = c + jnp.where(ci >= sh, pltpu.roll(c, sh, 0), 0.0)
        r = r + jnp.where(ri >= sh, pltpu.roll(r, sh, 1), 0.0)
        sh *= 2
    ccol_ref[...] = c
    crow_ref[...] = r


def _forget_cumsum(lf_col, lf_row):
    b, s, _ = lf_col.shape
    return pl.pallas_call(
        functools.partial(_cumsum_body, seq=s),
        grid=(b,),
        in_specs=[pl.BlockSpec((None, s, LANES), lambda i: (i, 0, 0)),
                  pl.BlockSpec((None, 8, s), lambda i: (i, 0, 0))],
        out_specs=[pl.BlockSpec((None, s, LANES), lambda i: (i, 0, 0)),
                   pl.BlockSpec((None, 8, s), lambda i: (i, 0, 0))],
        out_shape=[jax.ShapeDtypeStruct((b, s, LANES), F32), jax.ShapeDtypeStruct((b, 8, s), F32)],
        compiler_params=_cparams("parallel"),
        name="forget_cumsum",
    )(lf_col, lf_row)


def _attn_body(q_ref, k_ref, v_ref, *rest, tq, tk, n_heads, mode):
    o_ref = rest[-1]
    qi = pl.program_id(1)
    scale = HEAD_DIM ** -0.5
    row = qi * tq + lax.broadcasted_iota(jnp.int32, (tq, tk), 0)
    col0 = lax.broadcasted_iota(jnp.int32, (tq, tk), 1)
    for h in range(n_heads):
        hs = slice(h * HEAD_DIM, (h + 1) * HEAD_DIM)
        q = q_ref[:, hs]
        cq = rest[0][:, h:h + 1] if mode == "fox" else None

        def step(j, carry, masked, q=q, hs=hs, h=h, cq=cq):
            m, l, acc = carry
            off = pl.multiple_of(j * tk, tk)
            k = k_ref[pl.ds(off, tk), hs]
            v = v_ref[pl.ds(off, tk), hs]
            s = lax.dot_general(q, k, (((1,), (1,)), ((), ())), preferred_element_type=F32) * scale
            if mode == "fox":
                ck = rest[1][j][h:h + 1, :]
                s = s + (cq - ck)
                if masked:
                    s = jnp.where(col0 + off <= row, s, NEG)
            else:
                s = s + rest[0][j]
            m_new = jnp.maximum(m, jnp.max(s, axis=-1, keepdims=True))
            alpha = jnp.exp(m - m_new)
            p = jnp.exp(s - m_new)
            l = alpha * l + jnp.sum(p, axis=-1, keepdims=True)
            acc = alpha * acc + jnp.dot(p.astype(BF16), v, preferred_element_type=F32)
            return m_new, l, acc

        carry = (jnp.full((tq, 1), NEG, F32), jnp.zeros((tq, 1), F32), jnp.zeros((tq, HEAD_DIM), F32))
        if mode == "fox":
            carry = lax.fori_loop(0, qi, functools.partial(step, masked=False), carry)
            carry = step(qi, carry, True)
        else:
            carry = lax.fori_loop(0, qi + 1, functools.partial(step, masked=False), carry)
        _, l, acc = carry
        o_ref[:, hs] = (acc / l).astype(o_ref.dtype)


def _attention(q_arr, q_blk, k_arr, k_blk, v_arr, v_blk, extras, mode, n_heads, tq, tk):
    b, s, _ = q_arr.shape
    w = n_heads * HEAD_DIM
    in_specs = [pl.BlockSpec((None, tq, w), lambda bi, qi: (bi, qi, q_blk)),
                pl.BlockSpec((None, s, w), lambda bi, qi: (bi, 0, k_blk)),
                pl.BlockSpec((None, s, w), lambda bi, qi: (bi, 0, v_blk))]
    if mode == "fox":
        in_specs += [pl.BlockSpec((None, tq, LANES), lambda bi, qi: (bi, qi, 0)),
                     pl.BlockSpec((None, s // tk, 8, tk), lambda bi, qi: (bi, 0, 0, 0))]
    else:
        in_specs += [pl.BlockSpec((None, None, s // tk, tq, tk), lambda bi, qi: (bi, qi, 0, 0, 0))]
    return pl.pallas_call(
        functools.partial(_attn_body, tq=tq, tk=tk, n_heads=n_heads, mode=mode),
        grid=(b, s // tq),
        in_specs=in_specs,
        out_specs=pl.BlockSpec((None, tq, w), lambda bi, qi: (bi, qi, 0)),
        out_shape=jax.ShapeDtypeStruct((b, s, w), BF16),
        compiler_params=_cparams("parallel", "parallel"),
        name="attn_" + mode,
    )(q_arr, k_arr, v_arr, *extras)


def _select_body(iq_ref, ika_ref, ikb_ref, wl_ref, o_ref, keys_ref, wb_ref, *, tq, tk, n_chunks, topk):
    qi = pl.program_id(1)
    used = qi + 1
    rep = tk // LANES
    for j in range(IDX_HEADS):
        wb_ref[j] = jnp.broadcast_to(wl_ref[:, j:j + 1], (tq, LANES))
    row = qi * tq + lax.broadcasted_iota(jnp.int32, (tq, tk), 0)
    col0 = lax.broadcasted_iota(jnp.int32, (tq, tk), 1)

    def score_chunk(c, _):
        off = pl.multiple_of(c * tk, tk)
        ka = ika_ref[pl.ds(off, tk), :]
        kb = ikb_ref[pl.ds(off, tk), :]
        acc = jnp.zeros((tq, tk), F32)
        for j in range(IDX_HEADS):
            pair = iq_ref[:, (j // 2) * LANES:(j // 2 + 1) * LANES]
            logit = lax.dot_general(pair, ka if j % 2 == 0 else kb, (((1,), (1,)), ((), ())),
                                    preferred_element_type=F32)
            w = jnp.concatenate([wb_ref[j]] * rep, axis=1)
            acc = acc + w * jnp.maximum(logit, 0.0)
        bits = lax.bitcast_convert_type(acc, jnp.int32)
        key = jnp.where(bits < 0, bits ^ jnp.int32(0x7FFFFFFF), bits)
        keys_ref[c] = jnp.where(col0 + off <= row, key, jnp.int32(INT_MIN))
        return 0

    lax.fori_loop(0, used, score_chunk, 0)

    ones = jnp.ones((tk, LANES), BF16)

    def bit_step(i, prefix):
        cand_u = prefix | jnp.left_shift(jnp.int32(1), 31 - i)
        cand = jnp.concatenate([cand_u ^ jnp.int32(INT_MIN)] * rep, axis=1)

        def count_chunk(c, cnt):
            ind = jnp.where(keys_ref[c] >= cand, 1.0, 0.0).astype(BF16)
            return cnt + jnp.dot(ind, ones, preferred_element_type=F32)

        cnt = lax.fori_loop(0, used, count_chunk, jnp.zeros((tq, LANES), F32))
        return jnp.where(cnt >= topk, cand_u, prefix)

    prefix = lax.fori_loop(0, 32, bit_step, jnp.zeros((tq, LANES), jnp.int32))
    thr = jnp.maximum(prefix ^ jnp.int32(INT_MIN), jnp.int32(INT_MIN + 1))
    thr = jnp.concatenate([thr] * rep, axis=1)

    def write_chunk(c, _):
        o_ref[c] = jnp.where(keys_ref[c] >= thr, 0.0, NEG)
        return 0

    def fill_chunk(c, _):
        o_ref[c] = jnp.full((tq, tk), NEG, F32)
        return 0

    lax.fori_loop(0, used, write_chunk, 0)
    lax.fori_loop(used, n_chunks, fill_chunk, 0)


def _select_bias(iq, ika, ikb, wl, topk, tq, tk):
    b, s, _ = iq.shape
    n_chunks = s // tk
    return pl.pallas_call(
        functools.partial(_select_body, tq=tq, tk=tk, n_chunks=n_chunks, topk=topk),
        grid=(b, s // tq),
        in_specs=[pl.BlockSpec((None, tq, IDX_HEADS * IDX_DIM), lambda bi, qi: (bi, qi, 0)),
                  pl.BlockSpec((None, s, LANES), lambda bi, qi: (bi, 0, 0)),
                  pl.BlockSpec((None, s, LANES), lambda bi, qi: (bi, 0, 0)),
                  pl.BlockSpec((None, tq, LANES), lambda bi, qi: (bi, qi, 0))],
        out_specs=pl.BlockSpec((None, None, n_chunks, tq, tk), lambda bi, qi: (bi, qi, 0, 0, 0)),
        out_shape=jax.ShapeDtypeStruct((b, s // tq, n_chunks, tq, tk), F32),
        scratch_shapes=[pltpu.VMEM((n_chunks, tq, tk), jnp.int32),
                        pltpu.VMEM((IDX_HEADS, tq, LANES), F32)],
        compiler_params=_cparams("parallel", "parallel"),
        name="index_select",
    )(iq, ika, ikb, wl)


def _cross_body(q_ref, k_ref, v_ref, o_ref, *, n_heads):
    scale = HEAD_DIM ** -0.5
    for h in range(n_heads):
        hs = slice(h * HEAD_DIM, (h + 1) * HEAD_DIM)
        s = lax.dot_general(q_ref[:, hs], k_ref[:, hs], (((1,), (1,)), ((), ())),
                            preferred_element_type=F32) * scale
        p = jnp.exp(s - jnp.max(s, axis=-1, keepdims=True))
        l = jnp.sum(p, axis=-1, keepdims=True)
        o = jnp.dot(p.astype(BF16), v_ref[:, hs], preferred_element_type=F32)
        o_ref[:, hs] = (o / l).astype(o_ref.dtype)


def _cross_attention(q_arr, q_blk, mk, mv, tq):
    b, s, _ = q_arr.shape
    n_mem = mk.shape[1]
    w = CROSS_HEADS * HEAD_DIM
    return pl.pallas_call(
        functools.partial(_cross_body, n_heads=CROSS_HEADS),
        grid=(b, s // tq),
        in_specs=[pl.BlockSpec((None, tq, w), lambda bi, qi: (bi, qi, q_blk)),
                  pl.BlockSpec((None, n_mem, w), lambda bi, qi: (bi, 0, 0)),
                  pl.BlockSpec((None, n_mem, w), lambda bi, qi: (bi, 0, 0))],
        out_specs=pl.BlockSpec((None, tq, w), lambda bi, qi: (bi, qi, 0)),
        out_shape=jax.ShapeDtypeStruct((b, s, w), BF16),
        compiler_params=_cparams("parallel", "parallel"),
        name="cross_attn",
    )(q_arr, mk, mv)


def _sigmoid(x):
    return 1.0 / (1.0 + jnp.exp(-x))


def _merge_body(h_ref, of_ref, od_ref, oc_ref, x_ref, wgf_ref, wgd_ref, wgc_ref,
                wbf_ref, wbd_ref, wbc_ref, wo_ref, o_ref):
    j = pl.program_id(1)
    h = h_ref[...]

    def branch(wg_ref, ob_ref, wb_ref):
        gate = _sigmoid(jnp.dot(h, wg_ref[...], preferred_element_type=F32))
        return gate * jnp.dot(ob_ref[...], wb_ref[...], preferred_element_type=F32)

    merged = branch(wgf_ref, of_ref, wbf_ref) + branch(wgd_ref, od_ref, wbd_ref)
    merged = merged + branch(wgc_ref, oc_ref, wbc_ref)
    part = jnp.dot(merged.astype(BF16), wo_ref[...], preferred_element_type=F32)

    @pl.when(j == 0)
    def _():
        o_ref[...] = x_ref[...] + part

    @pl.when(j != 0)
    def _():
        o_ref[...] += part


def _merge(h, o_fox, o_dsa, o_cross, x, w_gates, wbf, wbd, wbc, w_out, tm, tn):
    m, d = x.shape
    nj = d // tn
    row = lambda i, j: (i, 0)
    in_specs = [pl.BlockSpec((tm, d), row),
                pl.BlockSpec((tm, o_fox.shape[1]), row),
                pl.BlockSpec((tm, o_dsa.shape[1]), row),
                pl.BlockSpec((tm, o_cross.shape[1]), row),
                pl.BlockSpec((tm, d), row),
                pl.BlockSpec((d, tn), lambda i, j: (0, j)),
                pl.BlockSpec((d, tn), lambda i, j: (0, nj + j)),
                pl.BlockSpec((d, tn), lambda i, j: (0, 2 * nj + j)),
                pl.BlockSpec((wbf.shape[0], tn), lambda i, j: (0, j)),
                pl.BlockSpec((wbd.shape[0], tn), lambda i, j: (0, j)),
                pl.BlockSpec((wbc.shape[0], tn), lambda i, j: (0, j)),
                pl.BlockSpec((tn, d), lambda i, j: (j, 0))]
    return pl.pallas_call(
        _merge_body,
        grid=(m // tm, nj),
        in_specs=in_specs,
        out_specs=pl.BlockSpec((tm, d), row),
        out_shape=jax.ShapeDtypeStruct((m, d), F32),
        compiler_params=_cparams("parallel", "arbitrary"),
        name="gated_merge",
    )(h, o_fox, o_dsa, o_cross, x, w_gates, w_gates, w_gates, wbf, wbd, wbc, w_out)


def _router_body(x_ref, g_ref, w_ref, b_ref, h_ref, eid_ref, gate_ref):
    x = x_ref[...]
    r = lax.rsqrt(jnp.mean(x * x, axis=-1, keepdims=True) + EPS)
    h = (x * r) * g_ref[...]
    h_ref[...] = h
    logits = jnp.dot(h.astype(BF16), w_ref[...], preferred_element_type=F32) + b_ref[...]
    lane = lax.broadcasted_iota(jnp.int32, logits.shape, 1).astype(F32)

    def masked_max(mask):
        return jnp.max(jnp.where(mask, logits, -jnp.inf), axis=-1, keepdims=True)

    def first_lane(mask):
        return jnp.min(jnp.where(mask, lane, float(LANES)), axis=-1, keepdims=True)

    is_group = lane < N_GROUPS
    gmax = masked_max(is_group)
    grp = first_lane(is_group & (logits == gmax))
    denom = jnp.sum(jnp.where(is_group, jnp.exp(logits - gmax), 0.0), axis=-1, keepdims=True)
    p_sel = 1.0 / denom
    lo = N_GROUPS + grp * EXPERTS_PER_GROUP
    in_grp = (lane >= lo) & (lane < lo + EXPERTS_PER_GROUP)
    e1 = masked_max(in_grp)
    j1 = first_lane(in_grp & (logits == e1))
    rest = in_grp & (lane != j1)
    e2 = masked_max(rest)
    j2 = first_lane(rest & (logits == e2))
    t = jnp.exp(e2 - e1)
    g1 = p_sel * (1.0 / (1.0 + t))
    g2 = p_sel * (t / (1.0 + t))
    eid = jnp.where(lane == 0.0, j1 - N_GROUPS, jnp.where(lane == 1.0, j2 - N_GROUPS, 0.0))
    eid_ref[...] = eid.astype(jnp.int32)
    gate_ref[...] = jnp.where(lane == 0.0, g1, jnp.where(lane == 1.0, g2, 0.0))


def _router(x, g, w_router, b_router, tm):
    m, d = x.shape
    return pl.pallas_call(
        _router_body,
        grid=(m // tm,),
        in_specs=[pl.BlockSpec((tm, d), lambda i: (i, 0)),
                  pl.BlockSpec((1, d), lambda i: (0, 0)),
                  pl.BlockSpec((d, LANES), lambda i: (0, 0)),
                  pl.BlockSpec((1, LANES), lambda i: (0, 0))],
        out_specs=[pl.BlockSpec((tm, d), lambda i: (i, 0)),
                   pl.BlockSpec((tm, LANES), lambda i: (i, 0)),
                   pl.BlockSpec((tm, LANES), lambda i: (i, 0))],
        out_shape=[jax.ShapeDtypeStruct((m, d), F32),
                   jax.ShapeDtypeStruct((m, LANES), jnp.int32),
                   jax.ShapeDtypeStruct((m, LANES), F32)],
        compiler_params=_cparams("parallel"),
        name="moe_router",
    )(x, g.reshape(1, d), w_router, b_router.reshape(1, LANES))


def _expert_body(ge_ref, gb0_ref, gnb_ref, used_ref, tok_ref, h_hbm, wg_ref, wu_ref, wd_ref, ys_hbm,
                 xs_ref, xb_ref, acc_ref, wgb_ref, wub_ref, wdb_ref, sem_in, sem_out,
                 *, bm, n_f, n_blocks):
    g = pl.program_id(0)
    f = pl.program_id(1)
    nb = gnb_ref[g]
    blk0 = gb0_ref[g]

    @pl.when(jnp.logical_and(g == pl.num_programs(0) - 1, f == n_f - 1))
    def _():
        xs_ref[pl.ds(0, bm), :] = jnp.zeros((bm, xs_ref.shape[1]), F32)

        def tail_copy(i):
            return pltpu.make_async_copy(xs_ref.at[pl.ds(0, bm)], ys_hbm.at[pl.ds(i * bm, bm)], sem_out)

        def start_tail(i, _):
            tail_copy(i).start()
            return 0

        lax.fori_loop(used_ref[0], n_blocks, start_tail, 0)

        def wait_tail(i, _):
            tail_copy(i).wait()
            return 0

        lax.fori_loop(used_ref[0], n_blocks, wait_tail, 0)

    def row_copy(r):
        return pltpu.make_async_copy(h_hbm.at[pl.ds(tok_ref[blk0 * bm + r], 1)],
                                     xs_ref.at[pl.ds(r, 1)], sem_in)

    def block_copy_out(i):
        return pltpu.make_async_copy(acc_ref.at[pl.ds(i * bm, bm)],
                                     ys_hbm.at[pl.ds((blk0 + i) * bm, bm)], sem_out)

    @pl.when(jnp.logical_and(nb > 0, f == 0))
    def _():
        def issue(r, _):
            row_copy(r).start()
            return 0

        lax.fori_loop(0, nb * bm, issue, 0)

        def wait(r, _):
            row_copy(r).wait()
            return 0

        lax.fori_loop(0, nb * bm, wait, 0)

        def cast(i, _):
            rows = pl.ds(pl.multiple_of(i * bm, bm), bm)
            xb_ref[rows, :] = xs_ref[rows, :].astype(BF16)
            return 0

        lax.fori_loop(0, nb, cast, 0)

    @pl.when(nb > 0)
    def _():
        wgb_ref[...] = wg_ref[...].astype(BF16)
        wub_ref[...] = wu_ref[...].astype(BF16)
        wdb_ref[...] = wd_ref[...].astype(BF16)

        def block(i, _):
            rows = pl.ds(pl.multiple_of(i * bm, bm), bm)
            xb = xb_ref[rows, :]
            a = jnp.dot(xb, wgb_ref[...], preferred_element_type=F32)
            u = jnp.dot(xb, wub_ref[...], preferred_element_type=F32)
            hid = (a * _sigmoid(a)) * u
            y = jnp.dot(hid.astype(BF16), wdb_ref[...], preferred_element_type=F32)

            @pl.when(f == 0)
            def _():
                acc_ref[rows, :] = y

            @pl.when(f != 0)
            def _():
                acc_ref[rows, :] += y

            return 0

        lax.fori_loop(0, nb, block, 0)

    @pl.when(jnp.logical_and(nb > 0, f == n_f - 1))
    def _():
        def start_out(i, _):
            block_copy_out(i).start()
            return 0

        lax.fori_loop(0, nb, start_out, 0)

        def wait_out(i, _):
            block_copy_out(i).wait()
            return 0

        lax.fori_loop(0, nb, wait_out, 0)


def _experts(h2, slot_tok, grp_e, grp_blk0, grp_nb, n_used, w_gate, w_up, w_down, n_slots):
    d = h2.shape[1]
    ff = w_gate.shape[2]
    n_f = ff // MOE_FC
    n_groups = grp_e.shape[0]
    rmax = MOE_GROUP_BLOCKS * MOE_BM

    def f_idx(g, f, gnb):
        return jnp.where(gnb[g] > 0, f, n_f - 1)

    grid_spec = pltpu.PrefetchScalarGridSpec(
        num_scalar_prefetch=5,
        grid=(n_groups, n_f),
        in_specs=[
            pl.BlockSpec(memory_space=pl.ANY),
            pl.BlockSpec((None, d, MOE_FC), lambda g, f, ge, gb0, gnb, *_: (ge[g], 0, f_idx(g, f, gnb))),
            pl.BlockSpec((None, d, MOE_FC), lambda g, f, ge, gb0, gnb, *_: (ge[g], 0, f_idx(g, f, gnb))),
            pl.BlockSpec((None, MOE_FC, d), lambda g, f, ge, gb0, gnb, *_: (ge[g], f_idx(g, f, gnb), 0)),
        ],
        out_specs=pl.BlockSpec(memory_space=pl.ANY),
        scratch_shapes=[pltpu.VMEM((rmax, d), F32),
                        pltpu.VMEM((rmax, d), BF16),
                        pltpu.VMEM((rmax, d), F32),
                        pltpu.VMEM((d, MOE_FC), BF16),
                        pltpu.VMEM((d, MOE_FC), BF16),
                        pltpu.VMEM((MOE_FC, d), BF16),
                        pltpu.SemaphoreType.DMA(()),
                        pltpu.SemaphoreType.DMA(())],
    )
    return pl.pallas_call(
        functools.partial(_expert_body, bm=MOE_BM, n_f=n_f, n_blocks=n_slots // MOE_BM),
        grid_spec=grid_spec,
        out_shape=jax.ShapeDtypeStruct((n_slots, d), F32),
        compiler_params=_cparams("arbitrary", "arbitrary"),
        name="moe_experts",
    )(grp_e, grp_blk0, grp_nb, n_used, slot_tok, h2, w_gate, w_up, w_down)


def _combine_body(pos_ref, x_ref, gate_ref, ys_hbm, o_ref, buf_ref, sem, *, tm):
    i = pl.program_id(0)

    def row_copy(r, k):
        return pltpu.make_async_copy(ys_hbm.at[pl.ds(pos_ref[(i * tm + r) * MOE_TOP_K + k], 1)],
                                     buf_ref.at[k, pl.ds(r, 1)], sem)

    def issue(r, _):
        row_copy(r, 0).start()
        row_copy(r, 1).start()
        return 0

    lax.fori_loop(0, tm, issue, 0)

    def wait(r, _):
        row_copy(r, 0).wait()
        row_copy(r, 1).wait()
        return 0

    lax.fori_loop(0, tm, wait, 0)
    o_ref[...] = x_ref[...] + (buf_ref[0] * gate_ref[:, 0:1] + buf_ref[1] * gate_ref[:, 1:2])


def _combine(x, gate, ys, pos, tm):
    m, d = x.shape
    grid_spec = pltpu.PrefetchScalarGridSpec(
        num_scalar_prefetch=1,
        grid=(m // tm,),
        in_specs=[pl.BlockSpec((tm, d), lambda i, pos: (i, 0)),
                  pl.BlockSpec((tm, LANES), lambda i, pos: (i, 0)),
                  pl.BlockSpec(memory_space=pl.ANY)],
        out_specs=pl.BlockSpec((tm, d), lambda i, pos: (i, 0)),
        scratch_shapes=[pltpu.VMEM((MOE_TOP_K, tm, d), F32), pltpu.SemaphoreType.DMA(())],
    )
    return pl.pallas_call(
        functools.partial(_combine_body, tm=tm),
        grid_spec=grid_spec,
        out_shape=jax.ShapeDtypeStruct((m, d), F32),
        compiler_params=_cparams("arbitrary"),
        name="moe_combine",
    )(pos, x, gate, ys)


def _moe_plan(eid, n_tokens):
    a = n_tokens * MOE_TOP_K
    eid = eid.reshape(a)
    n_blocks = (a + N_EXPERTS * (MOE_BM - 1)) // MOE_BM
    n_slots = n_blocks * MOE_BM
    n_groups = n_blocks // MOE_GROUP_BLOCKS + N_EXPERTS
    onehot = (eid[:, None] == jnp.arange(N_EXPERTS, dtype=jnp.int32)[None, :]).astype(jnp.int32)
    rank = jnp.take_along_axis(jnp.cumsum(onehot, axis=0), eid[:, None], axis=1)[:, 0] - 1
    counts = jnp.sum(onehot, axis=0)
    blocks_e = (counts + MOE_BM - 1) // MOE_BM
    blk_start = jnp.cumsum(blocks_e) - blocks_e
    pos = (blk_start[eid] * MOE_BM + rank).astype(jnp.int32)
    tok = jnp.arange(a, dtype=jnp.int32) // MOE_TOP_K
    slot_tok = jnp.zeros((n_slots,), jnp.int32).at[pos].set(tok)
    groups_e = (blocks_e + MOE_GROUP_BLOCKS - 1) // MOE_GROUP_BLOCKS
    grp_end = jnp.cumsum(groups_e)
    grp_start = grp_end - groups_e
    gidx = jnp.arange(n_groups, dtype=jnp.int32)
    ge = jnp.minimum(jnp.searchsorted(grp_end, gidx, side="right"), N_EXPERTS - 1).astype(jnp.int32)
    local = gidx - grp_start[ge]
    active = gidx < grp_end[-1]
    gb0 = jnp.where(active, blk_start[ge] + local * MOE_GROUP_BLOCKS, 0).astype(jnp.int32)
    gnb = jnp.where(active, jnp.minimum(blocks_e[ge] - local * MOE_GROUP_BLOCKS, MOE_GROUP_BLOCKS),
                    0).astype(jnp.int32)
    last_e = ge[jnp.maximum(grp_end[-1] - 1, 0)]
    ge = jnp.where(active, ge, last_e).astype(jnp.int32)
    n_used = jnp.sum(blocks_e).astype(jnp.int32).reshape(1)
    return pos, slot_tok, ge, gb0, gnb, n_used, n_slots


def _layer(x, mem, positions, attn_norm_g, mem_norm_g, w_in, fox_forget_b, fox_q_norm_g,
           fox_k_norm_g, dsa_q_norm_g, dsa_k_norm_g, idx_k_norm_g, cross_q_norm_g, cross_k_norm_g,
           w_mem_kv, w_branch_fox, w_branch_dsa, w_branch_cross, w_out, ffn_norm_g,
           router_group_w, router_group_b, router_expert_w, router_expert_b,
           expert_w_gate, expert_w_up, expert_w_down):
    b, s, d = x.shape
    n = b * s
    n_mem = mem.shape[1]
    fw, dw, cw = FOX_HEADS * HEAD_DIM, DSA_HEADS * HEAD_DIM, CROSS_HEADS * HEAD_DIM
    iqw = IDX_HEADS * IDX_DIM
    tm = min(512, n)
    x2d = x.reshape(n, d)

    sizes = (fw, fw, fw, FOX_HEADS, dw, dw, dw, iqw, IDX_HEADS, IDX_DIM, cw, 3 * d)
    offs = [0]
    for sz in sizes:
        offs.append(offs[-1] + sz)
    col = lambda i: w_in[:, offs[i]:offs[i + 1]]
    (c_fq, c_fk, c_fv, c_fl, c_dq, c_dk, c_dv, c_iq, c_iw, c_ik, c_cq, c_g) = range(12)
    w_qk1 = jnp.concatenate([col(c_fq), col(c_fk), col(c_cq)], axis=1).astype(BF16)
    w_qk2 = jnp.concatenate([col(c_dq), col(c_dk)], axis=1).astype(BF16)
    w_vv = jnp.concatenate([col(c_fv), col(c_dv)], axis=1).astype(BF16)
    w_iq = col(c_iq).astype(BF16)
    w_small = jnp.concatenate(
        [col(c_ik), col(c_iw), col(c_fl),
         jnp.zeros((d, LANES - IDX_DIM - IDX_HEADS - FOX_HEADS), F32)], axis=1).astype(BF16)
    w_gates = col(c_g).astype(BF16)
    g_qk1 = jnp.concatenate([jnp.tile(fox_q_norm_g, FOX_HEADS), jnp.tile(fox_k_norm_g, FOX_HEADS),
                             jnp.tile(cross_q_norm_g, CROSS_HEADS)])
    g_qk2 = jnp.concatenate([jnp.tile(dsa_q_norm_g, DSA_HEADS), jnp.tile(dsa_k_norm_g, DSA_HEADS)])
    g_ik = jnp.concatenate([idx_k_norm_g, jnp.zeros((LANES - IDX_DIM,), F32)])
    fb_lane = jnp.zeros((LANES,), F32).at[SMALL_F:SMALL_F + FOX_HEADS].set(fox_forget_b)

    pos_col = positions.reshape(n, 1).astype(F32)
    cd, sd = _rope_tables(pos_col, *_rope_lane_patterns(ROT_DIM, HEAD_DIM), tm=tm)
    ci, si = _rope_tables(pos_col, *_rope_lane_patterns(IDX_ROT_DIM, IDX_DIM), tm=tm)

    h = _rmsnorm(x2d, attn_norm_g, BF16, tm)
    qk1 = _proj(h, w_qk1, "norm", BF16, tm, 1024, gains=g_qk1)
    qk2 = _proj(h, w_qk2, "norm_rope", BF16, tm, 768, gains=g_qk2, tables=(cd, sd))
    vv = _proj(h, w_vv, "raw", BF16, tm, 768)
    iq = _proj(h, w_iq, "rope_idx", BF16, tm, 1024, tables=(ci, si))
    small = _proj(h, w_small, "raw", F32, tm, LANES)
    ika, ikb, wl, lf_col = _small(small, g_ik, fb_lane, ci, si, tm)

    lf_col = lf_col.reshape(b, s, LANES)
    lf_row = jnp.transpose(lf_col[:, :, :8], (0, 2, 1))
    cum_col, cum_row = _forget_cumsum(lf_col, lf_row)
    tq, tk = min(ATT_TQ, s), min(ATT_TK, s)
    cum_row = jnp.transpose(cum_row.reshape(b, 8, s // tk, tk), (0, 2, 1, 3))

    qk1 = qk1.reshape(b, s, 2 * fw + cw)
    qk2 = qk2.reshape(b, s, 2 * dw)
    vv = vv.reshape(b, s, fw + dw)
    o_fox = _attention(qk1, 0, qk1, 1, vv, 0, (cum_col, cum_row), "fox", FOX_HEADS, tq, tk)

    topk = min(INDEX_TOPK, s // 4)
    bias = _select_bias(iq.reshape(b, s, iqw), ika.reshape(b, s, LANES), ikb.reshape(b, s, LANES),
                        wl.reshape(b, s, LANES), topk, tq, tk)
    o_dsa = _attention(qk2, 0, qk2, 1, vv, 1, (bias,), "dsa", DSA_HEADS, tq, tk)

    tmm = min(512, b * n_mem)
    m_n = _rmsnorm(mem.reshape(b * n_mem, d), mem_norm_g, BF16, tmm)
    w_kv = w_mem_kv.astype(BF16)
    mk = _proj(m_n, w_kv[:, :cw], "norm", BF16, tmm, cw, gains=jnp.tile(cross_k_norm_g, CROSS_HEADS))
    mv = _proj(m_n, w_kv[:, cw:], "raw", BF16, tmm, cw)
    o_cross = _cross_attention(qk1, (2 * fw) // cw, mk.reshape(b, n_mem, cw), mv.reshape(b, n_mem, cw),
                               min(512, s))

    x_mid = _merge(h, o_fox.reshape(n, fw), o_dsa.reshape(n, dw), o_cross.reshape(n, cw), x2d,
                   w_gates, w_branch_fox.astype(BF16), w_branch_dsa.astype(BF16),
                   w_branch_cross.astype(BF16), w_out.astype(BF16), tm, 256)

    w_router = jnp.concatenate(
        [router_group_w, router_expert_w, jnp.zeros((d, LANES - N_GROUPS - N_EXPERTS), F32)],
        axis=1).astype(BF16)
    b_router = jnp.concatenate(
        [router_group_b, router_expert_b, jnp.zeros((LANES - N_GROUPS - N_EXPERTS,), F32)])
    h2, eid, gate = _router(x_mid, ffn_norm_g, w_router, b_router, tm)
    pos, slot_tok, ge, gb0, gnb, n_used, n_slots = _moe_plan(eid[:, :MOE_TOP_K], n)
    ys = _experts(h2, slot_tok, ge, gb0, gnb, n_used, expert_w_gate, expert_w_up, expert_w_down,
                  n_slots)
    out = _combine(x_mid, gate, ys, pos, min(256, n))
    return out.reshape(b, s, d)


def kernel(x, mem, positions, attn_norm_g, mem_norm_g, w_in, fox_forget_b, fox_q_norm_g, fox_k_norm_g, dsa_q_norm_g, dsa_k_norm_g, idx_k_norm_g, cross_q_norm_g, cross_k_norm_g, w_mem_kv, w_branch_fox, w_branch_dsa, w_branch_cross, w_out, ffn_norm_g, router_group_w, router_group_b, router_expert_w, router_expert_b, expert_w_gate, expert_w_up, expert_w_down):
    depth = w_in.shape[0]
    for layer in range(depth):
        x = _layer(
            x, mem, positions, attn_norm_g[layer], mem_norm_g[layer], w_in[layer],
            fox_forget_b[layer], fox_q_norm_g[layer], fox_k_norm_g[layer],
            dsa_q_norm_g[layer], dsa_k_norm_g[layer], idx_k_norm_g[layer],
            cross_q_norm_g[layer], cross_k_norm_g[layer], w_mem_kv[layer],
            w_branch_fox[layer], w_branch_dsa[layer], w_branch_cross[layer], w_out[layer],
            ffn_norm_g[layer], router_group_w[layer], router_group_b[layer],
            router_expert_w[layer], router_expert_b[layer], expert_w_gate[layer],
            expert_w_up[layer], expert_w_down[layer])
    return x
```

```python
import functools

import jax
import jax.numpy as jnp
from jax import lax
from jax.experimental import pallas as pl
from jax.experimental.pallas import tpu as pltpu

F32 = jnp.float32
BF16 = jnp.bfloat16

LANES = 128
HEAD_DIM = 128
FOX_HEADS = 6
DSA_HEADS = 6
CROSS_HEADS = 4
IDX_HEADS = 16
IDX_DIM = 64
ROPE_THETA = 500000.0
ROT_DIM = HEAD_DIM // 4
IDX_ROT_DIM = IDX_DIM // 4
INDEX_TOPK = 256
N_GROUPS = 4
EXPERTS_PER_GROUP = 8
N_EXPERTS = N_GROUPS * EXPERTS_PER_GROUP
MOE_TOP_K = 2
EPS = 1e-6

NEG = -1e30
INT_MIN = -(2 ** 31)
VMEM_LIMIT = 56 * 1024 * 1024

ATT_TQ = 256
ATT_TK = 256
ATT_HEADS_PER_LOOP = 6
MOE_BM = 256
MOE_GROUP_BLOCKS = 4
MOE_FC = 256


def _cparams(*sem):
    return pltpu.CompilerParams(dimension_semantics=sem, vmem_limit_bytes=VMEM_LIMIT)


def _rmsnorm_body(x_ref, g_ref, o_ref):
    x = x_ref[...]
    r = lax.rsqrt(jnp.mean(x * x, axis=-1, keepdims=True) + EPS)
    o_ref[...] = ((x * r) * g_ref[...]).astype(o_ref.dtype)


def _rmsnorm(x2d, g, out_dtype, tm):
    m, d = x2d.shape
    return pl.pallas_call(
        _rmsnorm_body,
        grid=(m // tm,),
        in_specs=[pl.BlockSpec((tm, d), lambda i: (i, 0)),
                  pl.BlockSpec((1, d), lambda i: (0, 0))],
        out_specs=pl.BlockSpec((tm, d), lambda i: (i, 0)),
        out_shape=jax.ShapeDtypeStruct((m, d), out_dtype),
        compiler_params=_cparams("parallel"),
        name="rmsnorm",
    )(x2d, g.reshape(1, d))


def _rope_table_body(pos_ref, f_ref, sgn_ref, c_ref, s_ref):
    ang = pos_ref[...] * f_ref[...]
    c_ref[...] = jnp.cos(ang)
    s_ref[...] = jnp.sin(ang) * sgn_ref[...]


def _rope_tables(pos_col, freq_lane, sign_lane, tm):
    n = pos_col.shape[0]
    row = pl.BlockSpec((1, LANES), lambda i: (0, 0))
    tab = pl.BlockSpec((tm, LANES), lambda i: (i, 0))
    return pl.pallas_call(
        _rope_table_body,
        grid=(n // tm,),
        in_specs=[pl.BlockSpec((tm, 1), lambda i: (i, 0)), row, row],
        out_specs=[tab, tab],
        out_shape=[jax.ShapeDtypeStruct((n, LANES), F32)] * 2,
        compiler_params=_cparams("parallel"),
        name="rope_tables",
    )(pos_col, freq_lane.reshape(1, LANES), sign_lane.reshape(1, LANES))


def _rope_lane_patterns(rot_dim, period):
    half = rot_dim // 2
    inv_freq = jnp.power(jnp.float32(ROPE_THETA), -jnp.arange(half, dtype=F32) * 2.0 / rot_dim)
    lane = jnp.arange(LANES) % period
    freq = jnp.where(lane < rot_dim, inv_freq[lane % half], 0.0).astype(F32)
    sign = jnp.where(lane < half, -1.0, jnp.where(lane < rot_dim, 1.0, 0.0)).astype(F32)
    return freq, sign


def _apply_rope(a, c, s, half, period):
    lane = lax.broadcasted_iota(jnp.int32, a.shape, 1)
    first = (lane & (period - 1)) < half
    partner = jnp.where(first, pltpu.roll(a, LANES - half, 1), pltpu.roll(a, half, 1))
    return a * c + partner * s


def _proj_body(*refs, mode, n_lane_groups):
    h_ref, w_ref = refs[0], refs[1]
    o_ref = refs[-1]
    acc = jnp.dot(h_ref[...], w_ref[...], preferred_element_type=F32)
    if mode == "raw":
        o_ref[...] = acc.astype(o_ref.dtype)
        return
    for g in range(n_lane_groups):
        sl = slice(g * LANES, (g + 1) * LANES)
        a = acc[:, sl]
        if mode in ("norm", "norm_rope"):
            g_ref = refs[2]
            a = (a * lax.rsqrt(jnp.mean(a * a, axis=-1, keepdims=True) + EPS)) * g_ref[:, sl]
        if mode == "norm_rope":
            a = _apply_rope(a, refs[3][...], refs[4][...], ROT_DIM // 2, HEAD_DIM)
        if mode == "rope_idx":
            a = _apply_rope(a, refs[2][...], refs[3][...], IDX_ROT_DIM // 2, IDX_DIM)
        o_ref[:, sl] = a.astype(o_ref.dtype)


def _proj(h, w, mode, out_dtype, tm, tn, gains=None, tables=None):
    m, k = h.shape
    n = w.shape[1]
    in_specs = [pl.BlockSpec((tm, k), lambda j, i: (i, 0)),
                pl.BlockSpec((k, tn), lambda j, i: (0, j))]
    args = [h, w]
    if gains is not None:
        in_specs.append(pl.BlockSpec((1, tn), lambda j, i: (0, j)))
        args.append(gains.reshape(1, n))
    if tables is not None:
        for t in tables:
            in_specs.append(pl.BlockSpec((tm, LANES), lambda j, i: (i, 0)))
            args.append(t)
    return pl.pallas_call(
        functools.partial(_proj_body, mode=mode, n_lane_groups=tn // LANES),
        grid=(n // tn, m // tm),
        in_specs=in_specs,
        out_specs=pl.BlockSpec((tm, tn), lambda j, i: (i, j)),
        out_shape=jax.ShapeDtypeStruct((m, n), out_dtype),
        compiler_params=_cparams("parallel", "parallel"),
        name="proj_" + mode,
    )(*args)


SMALL_IK = 0
SMALL_IW = 64
SMALL_F = 80


def _small_body(x_ref, gk_ref, fb_ref, c_ref, s_ref, ika_ref, ikb_ref, wl_ref, lf_ref):
    x = x_ref[...]
    lane = lax.broadcasted_iota(jnp.int32, x.shape, 1)
    is_ik = lane < IDX_DIM
    ik = jnp.where(is_ik, x, 0.0)
    ms = jnp.sum(ik * ik, axis=-1, keepdims=True) * (1.0 / IDX_DIM)
    ik = (ik * lax.rsqrt(ms + EPS)) * gk_ref[...]
    ik = _apply_rope(ik, c_ref[...], s_ref[...], IDX_ROT_DIM // 2, IDX_DIM)
    ik = jnp.where(is_ik, ik, 0.0)
    ika_ref[...] = ik.astype(BF16)
    ikb_ref[...] = pltpu.roll(ik, IDX_DIM, 1).astype(BF16)
    wl_ref[...] = pltpu.roll(x, LANES - SMALL_IW, 1) * (2.0 ** -5)
    z = x + fb_ref[...]
    log_f = jnp.minimum(z, 0.0) - jnp.log1p(jnp.exp(-jnp.abs(z)))
    lf_ref[...] = pltpu.roll(log_f, LANES - SMALL_F, 1)


def _small(x, idx_k_gain_lane, forget_b_lane, ci, si, tm):
    n = x.shape[0]
    row = pl.BlockSpec((1, LANES), lambda i: (0, 0))
    tab = pl.BlockSpec((tm, LANES), lambda i: (i, 0))
    return pl.pallas_call(
        _small_body,
        grid=(n // tm,),
        in_specs=[tab, row, row, tab, tab],
        out_specs=[tab, tab, tab, tab],
        out_shape=[jax.ShapeDtypeStruct((n, LANES), BF16), jax.ShapeDtypeStruct((n, LANES), BF16),
                   jax.ShapeDtypeStruct((n, LANES), F32), jax.ShapeDtypeStruct((n, LANES), F32)],
        compiler_params=_cparams("parallel"),
        name="small_cols",
    )(x, idx_k_gain_lane.reshape(1, LANES), forget_b_lane.reshape(1, LANES), ci, si)


def _cumsum_body(col_ref, ccol_ref, *, seq):
    c = col_ref[...]
    ci = lax.broadcasted_iota(jnp.int32, c.shape, 0)
    sh = 1
    while sh < seq:
        c = c + jnp.where(ci >= sh, pltpu.roll(c, sh, 0), 0.0)
        sh *= 2
    ccol_ref[...] = c


def _forget_cumsum(lf_col):
    b, s, _ = lf_col.shape
    return pl.pallas_call(
        functools.partial(_cumsum_body, seq=s),
        grid=(b,),
        in_specs=[pl.BlockSpec((None, s, LANES), lambda i: (i, 0, 0))],
        out_specs=pl.BlockSpec((None, s, LANES), lambda i: (i, 0, 0)),
        out_shape=jax.ShapeDtypeStruct((b, s, LANES), F32),
        compiler_params=_cparams("parallel"),
        name="forget_cumsum",
    )(lf_col)


AUG_DIM = 2 * HEAD_DIM
LOG2E = 1.4426950408889634


def _split3(x):
    hi = x.astype(BF16).astype(F32)
    r1 = x - hi
    mid = r1.astype(BF16).astype(F32)
    lo = (r1 - mid).astype(BF16).astype(F32)
    return hi, mid, lo


def _fox_pack_body(q_ref, k_ref, cum_ref, qa_ref, ka_ref, *, n_heads):
    rows = q_ref.shape[0]
    lane = lax.broadcasted_iota(jnp.int32, (rows, LANES), 1)
    for h in range(n_heads):
        c = jnp.broadcast_to(cum_ref[:, h:h + 1], (rows, LANES)) * LOG2E
        qh, qm, ql = _split3(c)
        kh, km, kl = _split3(-c)
        q_aug = jnp.where(lane == 0, qh, jnp.where(lane == 1, qm, jnp.where(
            lane == 2, ql, jnp.where(lane < 6, 1.0, 0.0))))
        k_aug = jnp.where(lane < 3, 1.0, jnp.where(lane == 3, kh, jnp.where(
            lane == 4, km, jnp.where(lane == 5, kl, 0.0))))
        base = h * AUG_DIM
        qa_ref[:, base:base + HEAD_DIM] = q_ref[:, h * HEAD_DIM:(h + 1) * HEAD_DIM]
        ka_ref[:, base:base + HEAD_DIM] = k_ref[:, h * HEAD_DIM:(h + 1) * HEAD_DIM]
        qa_ref[:, base + HEAD_DIM:base + AUG_DIM] = q_aug.astype(BF16)
        ka_ref[:, base + HEAD_DIM:base + AUG_DIM] = k_aug.astype(BF16)


def _fox_pack(qk, cum_col, n_heads, ts):
    b, s, _ = qk.shape
    w = n_heads * HEAD_DIM
    wa = n_heads * AUG_DIM
    return pl.pallas_call(
        functools.partial(_fox_pack_body, n_heads=n_heads),
        grid=(b, s // ts),
        in_specs=[pl.BlockSpec((None, ts, w), lambda bi, i: (bi, i, 0)),
                  pl.BlockSpec((None, ts, w), lambda bi, i: (bi, i, 1)),
                  pl.BlockSpec((None, ts, LANES), lambda bi, i: (bi, i, 0))],
        out_specs=[pl.BlockSpec((None, ts, wa), lambda bi, i: (bi, i, 0)),
                   pl.BlockSpec((None, ts, wa), lambda bi, i: (bi, i, 0))],
        out_shape=[jax.ShapeDtypeStruct((b, s, wa), BF16)] * 2,
        compiler_params=_cparams("parallel", "parallel"),
        name="fox_pack",
    )(qk, qk, cum_col)


def _attn_body(q_ref, k_ref, vt_ref, *rest, tq, tk, n_heads, kdim, mode, heads_per_loop):
    o_ref = rest[-1]
    qi = pl.program_id(1)
    key0 = lax.broadcasted_iota(jnp.int32, (tk, tq), 0)
    qry = qi * tq + lax.broadcasted_iota(jnp.int32, (tk, tq), 1)
    def scores(h, j):
        off = pl.multiple_of(j * tk, tk)
        q = q_ref[:, h * kdim:(h + 1) * kdim]
        k = k_ref[pl.ds(off, tk), h * kdim:(h + 1) * kdim]
        return lax.dot_general(k, q, (((1,), (1,)), ((), ())), preferred_element_type=F32)

    def update(h, j, s, m, l, acc, masked):
        if mode == "dsa":
            s = s + rest[0][j]
        if masked:
            s = jnp.where(key0 + j * tk <= qry, s, NEG)
        m_new = jnp.maximum(m, jnp.max(s, axis=0, keepdims=True))
        alpha = jnp.exp2(m - m_new)
        p = jnp.exp2(s - m_new)
        l = alpha * l + jnp.sum(p, axis=0, keepdims=True)
        vt = vt_ref[j, h * HEAD_DIM:(h + 1) * HEAD_DIM, :]
        acc = alpha * acc + jnp.dot(vt, p.astype(BF16), preferred_element_type=F32)
        return m_new, l, acc

    last = qi
    for h0 in range(0, n_heads, heads_per_loop):
        heads = range(h0, h0 + heads_per_loop)

        def step(j, carries, heads=heads):
            nxt = jnp.minimum(j + 1, last)
            out = []
            for h, (s, m, l, acc) in zip(heads, carries):
                s_next = scores(h, nxt)
                m, l, acc = update(h, j, s, m, l, acc, False)
                out.append((s_next, m, l, acc))
            return tuple(out)

        carries = tuple(
            (scores(h, 0), jnp.full((1, tq), NEG, F32), jnp.zeros((1, tq), F32),
             jnp.zeros((HEAD_DIM, tq), F32)) for h in heads)
        n_loop = last if mode == "fox" else last + 1
        carries = lax.fori_loop(0, n_loop, step, carries)
        for h, (s, m, l, acc) in zip(heads, carries):
            if mode == "fox":
                m, l, acc = update(h, last, s, m, l, acc, True)
            o_ref[:, h * HEAD_DIM:(h + 1) * HEAD_DIM] = jnp.transpose(acc / l).astype(o_ref.dtype)


def _attention(q_arr, q_blk, k_arr, k_blk, vt_arr, v_blk, extras, mode, n_heads, kdim, tq, tk):
    b, s, _ = q_arr.shape
    w = n_heads * HEAD_DIM
    wk = n_heads * kdim
    in_specs = [pl.BlockSpec((None, tq, wk), lambda bi, qi: (bi, qi, q_blk)),
                pl.BlockSpec((None, s, wk), lambda bi, qi: (bi, 0, k_blk)),
                pl.BlockSpec((None, s // tk, w, tk), lambda bi, qi: (bi, 0, v_blk, 0))]
    if mode == "dsa":
        in_specs += [pl.BlockSpec((None, None, s // tk, tk, tq), lambda bi, qi: (bi, qi, 0, 0, 0))]
    return pl.pallas_call(
        functools.partial(_attn_body, tq=tq, tk=tk, n_heads=n_heads, kdim=kdim, mode=mode,
                          heads_per_loop=ATT_HEADS_PER_LOOP),
        grid=(b, s // tq),
        in_specs=in_specs,
        out_specs=pl.BlockSpec((None, tq, w), lambda bi, qi: (bi, qi, 0)),
        out_shape=jax.ShapeDtypeStruct((b, s, w), BF16),
        compiler_params=_cparams("parallel", "parallel"),
        name="attn_" + mode,
    )(q_arr, k_arr, vt_arr, *extras)


def _select_body(iq_ref, ika_ref, ikb_ref, wt_ref, o_ref, keys_ref, *, tq, tk, n_chunks, topk):
    qi = pl.program_id(1)
    used = qi + 1
    key0 = lax.broadcasted_iota(jnp.int32, (tk, tq), 0)
    qry = qi * tq + lax.broadcasted_iota(jnp.int32, (tk, tq), 1)

    def score_chunk(c, _):
        off = pl.multiple_of(c * tk, tk)
        ka = ika_ref[pl.ds(off, tk), :]
        kb = ikb_ref[pl.ds(off, tk), :]
        acc = jnp.zeros((tk, tq), F32)
        for j in range(IDX_HEADS):
            pair = iq_ref[:, (j // 2) * LANES:(j // 2 + 1) * LANES]
            logit = lax.dot_general(ka if j % 2 == 0 else kb, pair, (((1,), (1,)), ((), ())),
                                    preferred_element_type=F32)
            acc = acc + wt_ref[j:j + 1, :] * jnp.maximum(logit, 0.0)
        bits = lax.bitcast_convert_type(acc, jnp.int32)
        key = jnp.where(bits < 0, bits ^ jnp.int32(0x7FFFFFFF), bits)
        keys_ref[c] = jnp.where(key0 + off <= qry, key, jnp.int32(INT_MIN))
        return 0

    lax.fori_loop(0, used, score_chunk, 0)

    def bit_step(i, prefix):
        cand_u = prefix | jnp.left_shift(jnp.int32(1), 31 - i)
        cand = cand_u ^ jnp.int32(INT_MIN)

        def count_chunk(c, cnt):
            ge = jnp.where(keys_ref[c] >= cand, 1.0, 0.0)
            return cnt + jnp.sum(ge.reshape(tk // 8, 8, tq), axis=0)

        cnt = lax.fori_loop(0, used, count_chunk, jnp.zeros((8, tq), F32))
        cnt = jnp.sum(cnt, axis=0, keepdims=True)
        return jnp.where(cnt >= topk, cand_u, prefix)

    prefix = lax.fori_loop(0, 32, bit_step, jnp.zeros((1, tq), jnp.int32))
    thr = jnp.maximum(prefix ^ jnp.int32(INT_MIN), jnp.int32(INT_MIN + 1))

    def write_chunk(c, _):
        o_ref[c] = jnp.where(keys_ref[c] >= thr, 0.0, NEG)
        return 0

    def fill_chunk(c, _):
        o_ref[c] = jnp.full((tk, tq), NEG, F32)
        return 0

    lax.fori_loop(0, used, write_chunk, 0)
    lax.fori_loop(used, n_chunks, fill_chunk, 0)


def _select_bias(iq, ika, ikb, wt, topk, tq, tk):
    b, s, _ = iq.shape
    n_chunks = s // tk
    return pl.pallas_call(
        functools.partial(_select_body, tq=tq, tk=tk, n_chunks=n_chunks, topk=topk),
        grid=(b, s // tq),
        in_specs=[pl.BlockSpec((None, tq, IDX_HEADS * IDX_DIM), lambda bi, qi: (bi, qi, 0)),
                  pl.BlockSpec((None, s, LANES), lambda bi, qi: (bi, 0, 0)),
                  pl.BlockSpec((None, s, LANES), lambda bi, qi: (bi, 0, 0)),
                  pl.BlockSpec((None, IDX_HEADS, tq), lambda bi, qi: (bi, 0, qi))],
        out_specs=pl.BlockSpec((None, None, n_chunks, tk, tq), lambda bi, qi: (bi, qi, 0, 0, 0)),
        out_shape=jax.ShapeDtypeStruct((b, s // tq, n_chunks, tk, tq), F32),
        scratch_shapes=[pltpu.VMEM((n_chunks, tk, tq), jnp.int32)],
        compiler_params=_cparams("parallel", "parallel"),
        name="index_select",
    )(iq, ika, ikb, wt)


def _cross_body(q_ref, k_ref, v_ref, o_ref, *, n_heads):
    for h in range(n_heads):
        hs = slice(h * HEAD_DIM, (h + 1) * HEAD_DIM)
        s = lax.dot_general(q_ref[:, hs], k_ref[:, hs], (((1,), (1,)), ((), ())),
                            preferred_element_type=F32)
        p = jnp.exp2(s - jnp.max(s, axis=-1, keepdims=True))
        l = jnp.sum(p, axis=-1, keepdims=True)
        o = jnp.dot(p.astype(BF16), v_ref[:, hs], preferred_element_type=F32)
        o_ref[:, hs] = (o / l).astype(o_ref.dtype)


def _cross_attention(q_arr, q_blk, mk, mv, tq):
    b, s, _ = q_arr.shape
    n_mem = mk.shape[1]
    w = CROSS_HEADS * HEAD_DIM
    return pl.pallas_call(
        functools.partial(_cross_body, n_heads=CROSS_HEADS),
        grid=(b, s // tq),
        in_specs=[pl.BlockSpec((None, tq, w), lambda bi, qi: (bi, qi, q_blk)),
                  pl.BlockSpec((None, n_mem, w), lambda bi, qi: (bi, 0, 0)),
                  pl.BlockSpec((None, n_mem, w), lambda bi, qi: (bi, 0, 0))],
        out_specs=pl.BlockSpec((None, tq, w), lambda bi, qi: (bi, qi, 0)),
        out_shape=jax.ShapeDtypeStruct((b, s, w), BF16),
        compiler_params=_cparams("parallel", "parallel"),
        name="cross_attn",
    )(q_arr, mk, mv)


def _sigmoid(x):
    return 1.0 / (1.0 + jnp.exp(-x))


def _merge_body(h_ref, of_ref, od_ref, oc_ref, x_ref, wgf_ref, wgd_ref, wgc_ref,
                wbf_ref, wbd_ref, wbc_ref, wo_ref, o_ref):
    j = pl.program_id(1)
    h = h_ref[...]

    def branch(wg_ref, ob_ref, wb_ref):
        gate = _sigmoid(jnp.dot(h, wg_ref[...], preferred_element_type=F32))
        return gate * jnp.dot(ob_ref[...], wb_ref[...], preferred_element_type=F32)

    merged = branch(wgf_ref, of_ref, wbf_ref) + branch(wgd_ref, od_ref, wbd_ref)
    merged = merged + branch(wgc_ref, oc_ref, wbc_ref)
    part = jnp.dot(merged.astype(BF16), wo_ref[...], preferred_element_type=F32)

    @pl.when(j == 0)
    def _():
        o_ref[...] = x_ref[...] + part

    @pl.when(j != 0)
    def _():
        o_ref[...] += part


def _merge(h, o_fox, o_dsa, o_cross, x, w_gates, wbf, wbd, wbc, w_out, tm, tn):
    m, d = x.shape
    nj = d // tn
    row = lambda i, j: (i, 0)
    in_specs = [pl.BlockSpec((tm, d), row),
                pl.BlockSpec((tm, o_fox.shape[1]), row),
                pl.BlockSpec((tm, o_dsa.shape[1]), row),
                pl.BlockSpec((tm, o_cross.shape[1]), row),
                pl.BlockSpec((tm, d), row),
                pl.BlockSpec((d, tn), lambda i, j: (0, j)),
                pl.BlockSpec((d, tn), lambda i, j: (0, nj + j)),
                pl.BlockSpec((d, tn), lambda i, j: (0, 2 * nj + j)),
                pl.BlockSpec((wbf.shape[0], tn), lambda i, j: (0, j)),
                pl.BlockSpec((wbd.shape[0], tn), lambda i, j: (0, j)),
                pl.BlockSpec((wbc.shape[0], tn), lambda i, j: (0, j)),
                pl.BlockSpec((tn, d), lambda i, j: (j, 0))]
    return pl.pallas_call(
        _merge_body,
        grid=(m // tm, nj),
        in_specs=in_specs,
        out_specs=pl.BlockSpec((tm, d), row),
        out_shape=jax.ShapeDtypeStruct((m, d), F32),
        compiler_params=_cparams("parallel", "arbitrary"),
        name="gated_merge",
    )(h, o_fox, o_dsa, o_cross, x, w_gates, w_gates, w_gates, wbf, wbd, wbc, w_out)


def _router_body(x_ref, g_ref, w_ref, b_ref, h_ref, eid_ref, gate_ref):
    x = x_ref[...]
    r = lax.rsqrt(jnp.mean(x * x, axis=-1, keepdims=True) + EPS)
    h = (x * r) * g_ref[...]
    h_ref[...] = h
    logits = jnp.dot(h.astype(BF16), w_ref[...], preferred_element_type=F32) + b_ref[...]
    lane = lax.broadcasted_iota(jnp.int32, logits.shape, 1).astype(F32)

    def masked_max(mask):
        return jnp.max(jnp.where(mask, logits, -jnp.inf), axis=-1, keepdims=True)

    def first_lane(mask):
        return jnp.min(jnp.where(mask, lane, float(LANES)), axis=-1, keepdims=True)

    is_group = lane < N_GROUPS
    gmax = masked_max(is_group)
    grp = first_lane(is_group & (logits == gmax))
    denom = jnp.sum(jnp.where(is_group, jnp.exp(logits - gmax), 0.0), axis=-1, keepdims=True)
    p_sel = 1.0 / denom
    lo = N_GROUPS + grp * EXPERTS_PER_GROUP
    in_grp = (lane >= lo) & (lane < lo + EXPERTS_PER_GROUP)
    e1 = masked_max(in_grp)
    j1 = first_lane(in_grp & (logits == e1))
    rest = in_grp & (lane != j1)
    e2 = masked_max(rest)
    j2 = first_lane(rest & (logits == e2))
    t = jnp.exp(e2 - e1)
    g1 = p_sel * (1.0 / (1.0 + t))
    g2 = p_sel * (t / (1.0 + t))
    eid = jnp.where(lane == 0.0, j1 - N_GROUPS, jnp.where(lane == 1.0, j2 - N_GROUPS, 0.0))
    eid_ref[...] = eid.astype(jnp.int32)
    gate_ref[...] = jnp.where(lane == 0.0, g1, jnp.where(lane == 1.0, g2, 0.0))


def _router(x, g, w_router, b_router, tm):
    m, d = x.shape
    return pl.pallas_call(
        _router_body,
        grid=(m // tm,),
        in_specs=[pl.BlockSpec((tm, d), lambda i: (i, 0)),
                  pl.BlockSpec((1, d), lambda i: (0, 0)),
                  pl.BlockSpec((d, LANES), lambda i: (0, 0)),
                  pl.BlockSpec((1, LANES), lambda i: (0, 0))],
        out_specs=[pl.BlockSpec((tm, d), lambda i: (i, 0)),
                   pl.BlockSpec((tm, LANES), lambda i: (i, 0)),
                   pl.BlockSpec((tm, LANES), lambda i: (i, 0))],
        out_shape=[jax.ShapeDtypeStruct((m, d), F32),
                   jax.ShapeDtypeStruct((m, LANES), jnp.int32),
                   jax.ShapeDtypeStruct((m, LANES), F32)],
        compiler_params=_cparams("parallel"),
        name="moe_router",
    )(x, g.reshape(1, d), w_router, b_router.reshape(1, LANES))


def _expert_body(ge_ref, gb0_ref, gnb_ref, used_ref, tok_ref, h_hbm, wg_ref, wu_ref, wd_ref, ys_hbm,
                 xs_ref, xb_ref, acc_ref, wgb_ref, wub_ref, wdb_ref, sem_in, sem_out,
                 *, bm, n_f, n_blocks):
    g = pl.program_id(0)
    f = pl.program_id(1)
    nb = gnb_ref[g]
    blk0 = gb0_ref[g]

    @pl.when(jnp.logical_and(g == pl.num_programs(0) - 1, f == n_f - 1))
    def _():
        xs_ref[pl.ds(0, bm), :] = jnp.zeros((bm, xs_ref.shape[1]), F32)

        def tail_copy(i):
            return pltpu.make_async_copy(xs_ref.at[pl.ds(0, bm)], ys_hbm.at[pl.ds(i * bm, bm)], sem_out)

        def start_tail(i, _):
            tail_copy(i).start()
            return 0

        lax.fori_loop(used_ref[0], n_blocks, start_tail, 0)

        def wait_tail(i, _):
            tail_copy(i).wait()
            return 0

        lax.fori_loop(used_ref[0], n_blocks, wait_tail, 0)

    def row_copy(r):
        return pltpu.make_async_copy(h_hbm.at[pl.ds(tok_ref[blk0 * bm + r], 1)],
                                     xs_ref.at[pl.ds(r, 1)], sem_in)

    def block_copy_out(i):
        return pltpu.make_async_copy(acc_ref.at[pl.ds(i * bm, bm)],
                                     ys_hbm.at[pl.ds((blk0 + i) * bm, bm)], sem_out)

    @pl.when(jnp.logical_and(nb > 0, f == 0))
    def _():
        def issue(r, _):
            row_copy(r).start()
            return 0

        lax.fori_loop(0, nb * bm, issue, 0)

        def wait(r, _):
            row_copy(r).wait()
            return 0

        lax.fori_loop(0, nb * bm, wait, 0)

        def cast(i, _):
            rows = pl.ds(pl.multiple_of(i * bm, bm), bm)
            xb_ref[rows, :] = xs_ref[rows, :].astype(BF16)
            return 0

        lax.fori_loop(0, nb, cast, 0)

    @pl.when(nb > 0)
    def _():
        wgb_ref[...] = wg_ref[...].astype(BF16)
        wub_ref[...] = wu_ref[...].astype(BF16)
        wdb_ref[...] = wd_ref[...].astype(BF16)

        def block(i, _):
            rows = pl.ds(pl.multiple_of(i * bm, bm), bm)
            xb = xb_ref[rows, :]
            a = jnp.dot(xb, wgb_ref[...], preferred_element_type=F32)
            u = jnp.dot(xb, wub_ref[...], preferred_element_type=F32)
            hid = (a * _sigmoid(a)) * u
            y = jnp.dot(hid.astype(BF16), wdb_ref[...], preferred_element_type=F32)

            @pl.when(f == 0)
            def _():
                acc_ref[rows, :] = y

            @pl.when(f != 0)
            def _():
                acc_ref[rows, :] += y

            return 0

        lax.fori_loop(0, nb, block, 0)

    @pl.when(jnp.logical_and(nb > 0, f == n_f - 1))
    def _():
        def start_out(i, _):
            block_copy_out(i).start()
            return 0

        lax.fori_loop(0, nb, start_out, 0)

        def wait_out(i, _):
            block_copy_out(i).wait()
            return 0

        lax.fori_loop(0, nb, wait_out, 0)


def _experts(h2, slot_tok, grp_e, grp_blk0, grp_nb, n_used, w_gate, w_up, w_down, n_slots):
    d = h2.shape[1]
    ff = w_gate.shape[2]
    n_f = ff // MOE_FC
    n_groups = grp_e.shape[0]
    rmax = MOE_GROUP_BLOCKS * MOE_BM

    def f_idx(g, f, gnb):
        return jnp.where(gnb[g] > 0, f, n_f - 1)

    grid_spec = pltpu.PrefetchScalarGridSpec(
        num_scalar_prefetch=5,
        grid=(n_groups, n_f),
        in_specs=[
            pl.BlockSpec(memory_space=pl.ANY),
            pl.BlockSpec((None, d, MOE_FC), lambda g, f, ge, gb0, gnb, *_: (ge[g], 0, f_idx(g, f, gnb))),
            pl.BlockSpec((None, d, MOE_FC), lambda g, f, ge, gb0, gnb, *_: (ge[g], 0, f_idx(g, f, gnb))),
            pl.BlockSpec((None, MOE_FC, d), lambda g, f, ge, gb0, gnb, *_: (ge[g], f_idx(g, f, gnb), 0)),
        ],
        out_specs=pl.BlockSpec(memory_space=pl.ANY),
        scratch_shapes=[pltpu.VMEM((rmax, d), F32),
                        pltpu.VMEM((rmax, d), BF16),
                        pltpu.VMEM((rmax, d), F32),
                        pltpu.VMEM((d, MOE_FC), BF16),
                        pltpu.VMEM((d, MOE_FC), BF16),
                        pltpu.VMEM((MOE_FC, d), BF16),
                        pltpu.SemaphoreType.DMA(()),
                        pltpu.SemaphoreType.DMA(())],
    )
    return pl.pallas_call(
        functools.partial(_expert_body, bm=MOE_BM, n_f=n_f, n_blocks=n_slots // MOE_BM),
        grid_spec=grid_spec,
        out_shape=jax.ShapeDtypeStruct((n_slots, d), F32),
        compiler_params=_cparams("arbitrary", "arbitrary"),
        name="moe_experts",
    )(grp_e, grp_blk0, grp_nb, n_used, slot_tok, h2, w_gate, w_up, w_down)


def _combine_body(pos_ref, x_ref, gate_ref, ys_hbm, o_ref, buf_ref, sem, *, tm):
    i = pl.program_id(0)

    def row_copy(r, k):
        return pltpu.make_async_copy(ys_hbm.at[pl.ds(pos_ref[(i * tm + r) * MOE_TOP_K + k], 1)],
                                     buf_ref.at[k, pl.ds(r, 1)], sem)

    def issue(r, _):
        row_copy(r, 0).start()
        row_copy(r, 1).start()
        return 0

    lax.fori_loop(0, tm, issue, 0)

    def wait(r, _):
        row_copy(r, 0).wait()
        row_copy(r, 1).wait()
        return 0

    lax.fori_loop(0, tm, wait, 0)
    o_ref[...] = x_ref[...] + (buf_ref[0] * gate_ref[:, 0:1] + buf_ref[1] * gate_ref[:, 1:2])


def _combine(x, gate, ys, pos, tm):
    m, d = x.shape
    grid_spec = pltpu.PrefetchScalarGridSpec(
        num_scalar_prefetch=1,
        grid=(m // tm,),
        in_specs=[pl.BlockSpec((tm, d), lambda i, pos: (i, 0)),
                  pl.BlockSpec((tm, LANES), lambda i, pos: (i, 0)),
                  pl.BlockSpec(memory_space=pl.ANY)],
        out_specs=pl.BlockSpec((tm, d), lambda i, pos: (i, 0)),
        scratch_shapes=[pltpu.VMEM((MOE_TOP_K, tm, d), F32), pltpu.SemaphoreType.DMA(())],
    )
    return pl.pallas_call(
        functools.partial(_combine_body, tm=tm),
        grid_spec=grid_spec,
        out_shape=jax.ShapeDtypeStruct((m, d), F32),
        compiler_params=_cparams("arbitrary"),
        name="moe_combine",
    )(pos, x, gate, ys)


def _moe_plan(eid, n_tokens):
    a = n_tokens * MOE_TOP_K
    eid = eid.reshape(a)
    n_blocks = (a + N_EXPERTS * (MOE_BM - 1)) // MOE_BM
    n_slots = n_blocks * MOE_BM
    n_groups = n_blocks // MOE_GROUP_BLOCKS + N_EXPERTS
    onehot = (eid[:, None] == jnp.arange(N_EXPERTS, dtype=jnp.int32)[None, :]).astype(jnp.int32)
    rank = jnp.take_along_axis(jnp.cumsum(onehot, axis=0), eid[:, None], axis=1)[:, 0] - 1
    counts = jnp.sum(onehot, axis=0)
    blocks_e = (counts + MOE_BM - 1) // MOE_BM
    blk_start = jnp.cumsum(blocks_e) - blocks_e
    pos = (blk_start[eid] * MOE_BM + rank).astype(jnp.int32)
    tok = jnp.arange(a, dtype=jnp.int32) // MOE_TOP_K
    slot_tok = jnp.zeros((n_slots,), jnp.int32).at[pos].set(tok)
    groups_e = (blocks_e + MOE_GROUP_BLOCKS - 1) // MOE_GROUP_BLOCKS
    grp_end = jnp.cumsum(groups_e)
    grp_start = grp_end - groups_e
    gidx = jnp.arange(n_groups, dtype=jnp.int32)
    ge = jnp.minimum(jnp.searchsorted(grp_end, gidx, side="right"), N_EXPERTS - 1).astype(jnp.int32)
    local = gidx - grp_start[ge]
    active = gidx < grp_end[-1]
    gb0 = jnp.where(active, blk_start[ge] + local * MOE_GROUP_BLOCKS, 0).astype(jnp.int32)
    gnb = jnp.where(active, jnp.minimum(blocks_e[ge] - local * MOE_GROUP_BLOCKS, MOE_GROUP_BLOCKS),
                    0).astype(jnp.int32)
    last_e = ge[jnp.maximum(grp_end[-1] - 1, 0)]
    ge = jnp.where(active, ge, last_e).astype(jnp.int32)
    n_used = jnp.sum(blocks_e).astype(jnp.int32).reshape(1)
    return pos, slot_tok, ge, gb0, gnb, n_used, n_slots


def _layer(x, mem, positions, attn_norm_g, mem_norm_g, w_in, fox_forget_b, fox_q_norm_g,
           fox_k_norm_g, dsa_q_norm_g, dsa_k_norm_g, idx_k_norm_g, cross_q_norm_g, cross_k_norm_g,
           w_mem_kv, w_branch_fox, w_branch_dsa, w_branch_cross, w_out, ffn_norm_g,
           router_group_w, router_group_b, router_expert_w, router_expert_b,
           expert_w_gate, expert_w_up, expert_w_down):
    b, s, d = x.shape
    n = b * s
    n_mem = mem.shape[1]
    fw, dw, cw = FOX_HEADS * HEAD_DIM, DSA_HEADS * HEAD_DIM, CROSS_HEADS * HEAD_DIM
    iqw = IDX_HEADS * IDX_DIM
    tm = min(512, n)
    x2d = x.reshape(n, d)

    sizes = (fw, fw, fw, FOX_HEADS, dw, dw, dw, iqw, IDX_HEADS, IDX_DIM, cw, 3 * d)
    offs = [0]
    for sz in sizes:
        offs.append(offs[-1] + sz)
    col = lambda i: w_in[:, offs[i]:offs[i + 1]]
    (c_fq, c_fk, c_fv, c_fl, c_dq, c_dk, c_dv, c_iq, c_iw, c_ik, c_cq, c_g) = range(12)
    w_qk1 = jnp.concatenate([col(c_fq), col(c_fk), col(c_cq)], axis=1).astype(BF16)
    w_qk2 = jnp.concatenate([col(c_dq), col(c_dk)], axis=1).astype(BF16)
    w_vv = jnp.concatenate([col(c_fv), col(c_dv)], axis=1).astype(BF16)
    w_iq = col(c_iq).astype(BF16)
    w_small = jnp.concatenate(
        [col(c_ik), col(c_iw), col(c_fl),
         jnp.zeros((d, LANES - IDX_DIM - IDX_HEADS - FOX_HEADS), F32)], axis=1).astype(BF16)
    w_gates = col(c_g).astype(BF16)
    q_unit = (HEAD_DIM ** -0.5) * LOG2E
    g_qk1 = jnp.concatenate([jnp.tile(fox_q_norm_g * q_unit, FOX_HEADS),
                             jnp.tile(fox_k_norm_g, FOX_HEADS),
                             jnp.tile(cross_q_norm_g * q_unit, CROSS_HEADS)])
    g_qk2 = jnp.concatenate([jnp.tile(dsa_q_norm_g * q_unit, DSA_HEADS),
                             jnp.tile(dsa_k_norm_g, DSA_HEADS)])
    g_ik = jnp.concatenate([idx_k_norm_g, jnp.zeros((LANES - IDX_DIM,), F32)])
    fb_lane = jnp.zeros((LANES,), F32).at[SMALL_F:SMALL_F + FOX_HEADS].set(fox_forget_b)

    pos_col = positions.reshape(n, 1).astype(F32)
    cd, sd = _rope_tables(pos_col, *_rope_lane_patterns(ROT_DIM, HEAD_DIM), tm=tm)
    ci, si = _rope_tables(pos_col, *_rope_lane_patterns(IDX_ROT_DIM, IDX_DIM), tm=tm)

    h = _rmsnorm(x2d, attn_norm_g, BF16, tm)
    qk1 = _proj(h, w_qk1, "norm", BF16, tm, 1024, gains=g_qk1)
    qk2 = _proj(h, w_qk2, "norm_rope", BF16, tm, 768, gains=g_qk2, tables=(cd, sd))
    vv = _proj(h, w_vv, "raw", BF16, tm, 768)
    iq = _proj(h, w_iq, "rope_idx", BF16, tm, 1024, tables=(ci, si))
    small = _proj(h, w_small, "raw", F32, tm, LANES)
    ika, ikb, wl, lf_col = _small(small, g_ik, fb_lane, ci, si, tm)

    tq, tk = min(ATT_TQ, s), min(ATT_TK, s)
    assert tq == tk and s % tq == 0
    cum_col = _forget_cumsum(lf_col.reshape(b, s, LANES))
    qk1 = qk1.reshape(b, s, 2 * fw + cw)
    qk2 = qk2.reshape(b, s, 2 * dw)
    q_aug, k_aug = _fox_pack(qk1, cum_col, FOX_HEADS, min(512, s))
    vt = jnp.transpose(vv.reshape(b, s // tk, tk, fw + dw), (0, 1, 3, 2))
    o_fox = _attention(q_aug, 0, k_aug, 0, vt, 0, (), "fox", FOX_HEADS, AUG_DIM, tq, tk)

    topk = min(INDEX_TOPK, s // 4)
    wt = jnp.transpose(wl.reshape(b, s, LANES)[:, :, :IDX_HEADS], (0, 2, 1))
    bias = _select_bias(iq.reshape(b, s, iqw), ika.reshape(b, s, LANES), ikb.reshape(b, s, LANES),
                        wt, topk, tq, tk)
    o_dsa = _attention(qk2, 0, qk2, 1, vt, 1, (bias,), "dsa", DSA_HEADS, HEAD_DIM, tq, tk)

    tmm = min(512, b * n_mem)
    m_n = _rmsnorm(mem.reshape(b * n_mem, d), mem_norm_g, BF16, tmm)
    w_kv = w_mem_kv.astype(BF16)
    mk = _proj(m_n, w_kv[:, :cw], "norm", BF16, tmm, cw, gains=jnp.tile(cross_k_norm_g, CROSS_HEADS))
    mv = _proj(m_n, w_kv[:, cw:], "raw", BF16, tmm, cw)
    o_cross = _cross_attention(qk1, (2 * fw) // cw, mk.reshape(b, n_mem, cw), mv.reshape(b, n_mem, cw),
                               min(512, s))

    x_mid = _merge(h, o_fox.reshape(n, fw), o_dsa.reshape(n, dw), o_cross.reshape(n, cw), x2d,
                   w_gates, w_branch_fox.astype(BF16), w_branch_dsa.astype(BF16),
                   w_branch_cross.astype(BF16), w_out.astype(BF16), tm, 256)

    w_router = jnp.concatenate(
        [router_group_w, router_expert_w, jnp.zeros((d, LANES - N_GROUPS - N_EXPERTS), F32)],
        axis=1).astype(BF16)
    b_router = jnp.concatenate(
        [router_group_b, router_expert_b, jnp.zeros((LANES - N_GROUPS - N_EXPERTS,), F32)])
    h2, eid, gate = _router(x_mid, ffn_norm_g, w_router, b_router, tm)
    pos, slot_tok, ge, gb0, gnb, n_used, n_slots = _moe_plan(eid[:, :MOE_TOP_K], n)
    ys = _experts(h2, slot_tok, ge, gb0, gnb, n_used, expert_w_gate, expert_w_up, expert_w_down,
                  n_slots)
    out = _combine(x_mid, gate, ys, pos, min(256, n))
    return out.reshape(b, s, d)


def kernel(x, mem, positions, attn_norm_g, mem_norm_g, w_in, fox_forget_b, fox_q_norm_g, fox_k_norm_g, dsa_q_norm_g, dsa_k_norm_g, idx_k_norm_g, cross_q_norm_g, cross_k_norm_g, w_mem_kv, w_branch_fox, w_branch_dsa, w_branch_cross, w_out, ffn_norm_g, router_group_w, router_group_b, router_expert_w, router_expert_b, expert_w_gate, expert_w_up, expert_w_down):
    depth = w_in.shape[0]
    for layer in range(depth):
        x = _layer(
            x, mem, positions, attn_norm_g[layer], mem_norm_g[layer], w_in[layer],
            fox_forget_b[layer], fox_q_norm_g[layer], fox_k_norm_g[layer],
            dsa_q_norm_g[layer], dsa_k_norm_g[layer], idx_k_norm_g[layer],
            cross_q_norm_g[layer], cross_k_norm_g[layer], w_mem_kv[layer],
            w_branch_fox[layer], w_branch_dsa[layer], w_branch_cross[layer], w_out[layer],
            ffn_norm_g[layer], router_group_w[layer], router_group_b[layer],
            router_expert_w[layer], router_expert_b[layer], expert_w_gate[layer],
            expert_w_up[layer], expert_w_down[layer])
    return x
```

```python
import functools

import jax
import jax.numpy as jnp
from jax import lax
from jax.experimental import pallas as pl
from jax.experimental.pallas import tpu as pltpu

F32 = jnp.float32
BF16 = jnp.bfloat16

LANES = 128
HEAD_DIM = 128
FOX_HEADS = 6
DSA_HEADS = 6
CROSS_HEADS = 4
IDX_HEADS = 16
IDX_DIM = 64
ROPE_THETA = 500000.0
ROT_DIM = HEAD_DIM // 4
IDX_ROT_DIM = IDX_DIM // 4
INDEX_TOPK = 256
N_GROUPS = 4
EXPERTS_PER_GROUP = 8
N_EXPERTS = N_GROUPS * EXPERTS_PER_GROUP
MOE_TOP_K = 2
EPS = 1e-6

NEG = -1e30
INT_MIN = -(2 ** 31)
VMEM_LIMIT = 56 * 1024 * 1024

ATT_TQ = 256
ATT_TK = 256
ATT_HEADS_PER_LOOP = 6
MERGE_TN = 512
MERGE_SUB = 256
MOE_BM = 256
MOE_GROUP_BLOCKS = 4
MOE_FC = 256


def _cparams(*sem):
    return pltpu.CompilerParams(dimension_semantics=sem, vmem_limit_bytes=VMEM_LIMIT)


def _rmsnorm_body(x_ref, g_ref, o_ref):
    x = x_ref[...]
    r = lax.rsqrt(jnp.mean(x * x, axis=-1, keepdims=True) + EPS)
    o_ref[...] = ((x * r) * g_ref[...]).astype(o_ref.dtype)


def _rmsnorm(x2d, g, out_dtype, tm):
    m, d = x2d.shape
    return pl.pallas_call(
        _rmsnorm_body,
        grid=(m // tm,),
        in_specs=[pl.BlockSpec((tm, d), lambda i: (i, 0)),
                  pl.BlockSpec((1, d), lambda i: (0, 0))],
        out_specs=pl.BlockSpec((tm, d), lambda i: (i, 0)),
        out_shape=jax.ShapeDtypeStruct((m, d), out_dtype),
        compiler_params=_cparams("parallel"),
        name="rmsnorm",
    )(x2d, g.reshape(1, d))


def _w_prep_body(w_ref, *o_refs, plan):
    for o_ref, pieces in zip(o_refs, plan):
        end = 0
        for dst, src, width in pieces:
            o_ref[:, dst:dst + width] = w_ref[:, src:src + width].astype(o_ref.dtype)
            end = dst + width
        if end < o_ref.shape[1]:
            o_ref[:, end:] = jnp.zeros((o_ref.shape[0], o_ref.shape[1] - end), o_ref.dtype)


def _w_prep(w, plan, widths, tr):
    k, n = w.shape
    return pl.pallas_call(
        functools.partial(_w_prep_body, plan=plan),
        grid=(k // tr,),
        in_specs=[pl.BlockSpec((tr, n), lambda i: (i, 0))],
        out_specs=[pl.BlockSpec((tr, wd), lambda i: (i, 0)) for wd in widths],
        out_shape=[jax.ShapeDtypeStruct((k, wd), BF16) for wd in widths],
        compiler_params=_cparams("parallel"),
        name="w_prep",
    )(w)


def _rope_table_body(pos_ref, f_ref, sgn_ref, c_ref, s_ref):
    ang = pos_ref[...] * f_ref[...]
    c_ref[...] = jnp.cos(ang)
    s_ref[...] = jnp.sin(ang) * sgn_ref[...]


def _rope_tables(pos_col, freq_lane, sign_lane, tm):
    n = pos_col.shape[0]
    row = pl.BlockSpec((1, LANES), lambda i: (0, 0))
    tab = pl.BlockSpec((tm, LANES), lambda i: (i, 0))
    return pl.pallas_call(
        _rope_table_body,
        grid=(n // tm,),
        in_specs=[pl.BlockSpec((tm, 1), lambda i: (i, 0)), row, row],
        out_specs=[tab, tab],
        out_shape=[jax.ShapeDtypeStruct((n, LANES), F32)] * 2,
        compiler_params=_cparams("parallel"),
        name="rope_tables",
    )(pos_col, freq_lane.reshape(1, LANES), sign_lane.reshape(1, LANES))


def _rope_lane_patterns(rot_dim, period):
    half = rot_dim // 2
    inv_freq = jnp.power(jnp.float32(ROPE_THETA), -jnp.arange(half, dtype=F32) * 2.0 / rot_dim)
    lane = jnp.arange(LANES) % period
    freq = jnp.where(lane < rot_dim, inv_freq[lane % half], 0.0).astype(F32)
    sign = jnp.where(lane < half, -1.0, jnp.where(lane < rot_dim, 1.0, 0.0)).astype(F32)
    return freq, sign


def _apply_rope(a, c, s, half, period):
    lane = lax.broadcasted_iota(jnp.int32, a.shape, 1)
    first = (lane & (period - 1)) < half
    partner = jnp.where(first, pltpu.roll(a, LANES - half, 1), pltpu.roll(a, half, 1))
    return a * c + partner * s


def _proj_body(*refs, mode, n_lane_groups):
    h_ref, w_ref = refs[0], refs[1]
    o_ref = refs[-1]
    acc = jnp.dot(h_ref[...], w_ref[...], preferred_element_type=F32)
    if mode == "raw":
        o_ref[...] = acc.astype(o_ref.dtype)
        return
    for g in range(n_lane_groups):
        sl = slice(g * LANES, (g + 1) * LANES)
        a = acc[:, sl]
        if mode in ("norm", "norm_rope"):
            g_ref = refs[2]
            a = (a * lax.rsqrt(jnp.mean(a * a, axis=-1, keepdims=True) + EPS)) * g_ref[:, sl]
        if mode == "norm_rope":
            a = _apply_rope(a, refs[3][...], refs[4][...], ROT_DIM // 2, HEAD_DIM)
        if mode == "rope_idx":
            a = _apply_rope(a, refs[2][...], refs[3][...], IDX_ROT_DIM // 2, IDX_DIM)
        o_ref[:, sl] = a.astype(o_ref.dtype)


def _proj(h, w, mode, out_dtype, tm, tn, gains=None, tables=None):
    m, k = h.shape
    n = w.shape[1]
    in_specs = [pl.BlockSpec((tm, k), lambda j, i: (i, 0)),
                pl.BlockSpec((k, tn), lambda j, i: (0, j))]
    args = [h, w]
    if gains is not None:
        in_specs.append(pl.BlockSpec((1, tn), lambda j, i: (0, j)))
        args.append(gains.reshape(1, n))
    if tables is not None:
        for t in tables:
            in_specs.append(pl.BlockSpec((tm, LANES), lambda j, i: (i, 0)))
            args.append(t)
    return pl.pallas_call(
        functools.partial(_proj_body, mode=mode, n_lane_groups=tn // LANES),
        grid=(n // tn, m // tm),
        in_specs=in_specs,
        out_specs=pl.BlockSpec((tm, tn), lambda j, i: (i, j)),
        out_shape=jax.ShapeDtypeStruct((m, n), out_dtype),
        compiler_params=_cparams("parallel", "parallel"),
        name="proj_" + mode,
    )(*args)


SMALL_IK = 0
SMALL_IW = 64
SMALL_F = 80


def _small_body(x_ref, gk_ref, fb_ref, c_ref, s_ref, ika_ref, ikb_ref, wl_ref, lf_ref):
    x = x_ref[...]
    lane = lax.broadcasted_iota(jnp.int32, x.shape, 1)
    is_ik = lane < IDX_DIM
    ik = jnp.where(is_ik, x, 0.0)
    ms = jnp.sum(ik * ik, axis=-1, keepdims=True) * (1.0 / IDX_DIM)
    ik = (ik * lax.rsqrt(ms + EPS)) * gk_ref[...]
    ik = _apply_rope(ik, c_ref[...], s_ref[...], IDX_ROT_DIM // 2, IDX_DIM)
    ik = jnp.where(is_ik, ik, 0.0)
    ika_ref[...] = ik.astype(BF16)
    ikb_ref[...] = pltpu.roll(ik, IDX_DIM, 1).astype(BF16)
    wl_ref[...] = pltpu.roll(x, LANES - SMALL_IW, 1) * (2.0 ** -5)
    z = x + fb_ref[...]
    log_f = jnp.minimum(z, 0.0) - jnp.log1p(jnp.exp(-jnp.abs(z)))
    lf_ref[...] = pltpu.roll(log_f, LANES - SMALL_F, 1)


def _small(x, idx_k_gain_lane, forget_b_lane, ci, si, tm):
    n = x.shape[0]
    row = pl.BlockSpec((1, LANES), lambda i: (0, 0))
    tab = pl.BlockSpec((tm, LANES), lambda i: (i, 0))
    return pl.pallas_call(
        _small_body,
        grid=(n // tm,),
        in_specs=[tab, row, row, tab, tab],
        out_specs=[tab, tab, tab, tab],
        out_shape=[jax.ShapeDtypeStruct((n, LANES), BF16), jax.ShapeDtypeStruct((n, LANES), BF16),
                   jax.ShapeDtypeStruct((n, LANES), F32), jax.ShapeDtypeStruct((n, LANES), F32)],
        compiler_params=_cparams("parallel"),
        name="small_cols",
    )(x, idx_k_gain_lane.reshape(1, LANES), forget_b_lane.reshape(1, LANES), ci, si)


def _cumsum_body(col_ref, ccol_ref, *, seq):
    c = col_ref[...]
    ci = lax.broadcasted_iota(jnp.int32, c.shape, 0)
    sh = 1
    while sh < seq:
        c = c + jnp.where(ci >= sh, pltpu.roll(c, sh, 0), 0.0)
        sh *= 2
    ccol_ref[...] = c


def _forget_cumsum(lf_col):
    b, s, _ = lf_col.shape
    return pl.pallas_call(
        functools.partial(_cumsum_body, seq=s),
        grid=(b,),
        in_specs=[pl.BlockSpec((None, s, LANES), lambda i: (i, 0, 0))],
        out_specs=pl.BlockSpec((None, s, LANES), lambda i: (i, 0, 0)),
        out_shape=jax.ShapeDtypeStruct((b, s, LANES), F32),
        compiler_params=_cparams("parallel"),
        name="forget_cumsum",
    )(lf_col)


AUG_DIM = 2 * HEAD_DIM
LOG2E = 1.4426950408889634


def _split3(x):
    hi = x.astype(BF16).astype(F32)
    r1 = x - hi
    mid = r1.astype(BF16).astype(F32)
    lo = (r1 - mid).astype(BF16).astype(F32)
    return hi, mid, lo


def _fox_pack_body(q_ref, k_ref, cum_ref, qa_ref, ka_ref, *, n_heads):
    rows = q_ref.shape[0]
    lane = lax.broadcasted_iota(jnp.int32, (rows, LANES), 1)
    for h in range(n_heads):
        c = jnp.broadcast_to(cum_ref[:, h:h + 1], (rows, LANES)) * LOG2E
        qh, qm, ql = _split3(c)
        kh, km, kl = _split3(-c)
        q_aug = jnp.where(lane == 0, qh, jnp.where(lane == 1, qm, jnp.where(
            lane == 2, ql, jnp.where(lane < 6, 1.0, 0.0))))
        k_aug = jnp.where(lane < 3, 1.0, jnp.where(lane == 3, kh, jnp.where(
            lane == 4, km, jnp.where(lane == 5, kl, 0.0))))
        base = h * AUG_DIM
        qa_ref[:, base:base + HEAD_DIM] = q_ref[:, h * HEAD_DIM:(h + 1) * HEAD_DIM]
        ka_ref[:, base:base + HEAD_DIM] = k_ref[:, h * HEAD_DIM:(h + 1) * HEAD_DIM]
        qa_ref[:, base + HEAD_DIM:base + AUG_DIM] = q_aug.astype(BF16)
        ka_ref[:, base + HEAD_DIM:base + AUG_DIM] = k_aug.astype(BF16)


def _fox_pack(qk, cum_col, n_heads, ts):
    b, s, _ = qk.shape
    w = n_heads * HEAD_DIM
    wa = n_heads * AUG_DIM
    return pl.pallas_call(
        functools.partial(_fox_pack_body, n_heads=n_heads),
        grid=(b, s // ts),
        in_specs=[pl.BlockSpec((None, ts, w), lambda bi, i: (bi, i, 0)),
                  pl.BlockSpec((None, ts, w), lambda bi, i: (bi, i, 1)),
                  pl.BlockSpec((None, ts, LANES), lambda bi, i: (bi, i, 0))],
        out_specs=[pl.BlockSpec((None, ts, wa), lambda bi, i: (bi, i, 0)),
                   pl.BlockSpec((None, ts, wa), lambda bi, i: (bi, i, 0))],
        out_shape=[jax.ShapeDtypeStruct((b, s, wa), BF16)] * 2,
        compiler_params=_cparams("parallel", "parallel"),
        name="fox_pack",
    )(qk, qk, cum_col)


def _attn_body(q_ref, k_ref, vt_ref, *rest, tq, tk, n_heads, kdim, mode, heads_per_loop):
    o_ref = rest[-1]
    qi = pl.program_id(1)
    key0 = lax.broadcasted_iota(jnp.int32, (tk, tq), 0)
    qry = qi * tq + lax.broadcasted_iota(jnp.int32, (tk, tq), 1)
    def scores(h, j):
        off = pl.multiple_of(j * tk, tk)
        q = q_ref[:, h * kdim:(h + 1) * kdim]
        k = k_ref[pl.ds(off, tk), h * kdim:(h + 1) * kdim]
        return lax.dot_general(k, q, (((1,), (1,)), ((), ())), preferred_element_type=F32)

    def update(h, j, s, m, l, acc, masked):
        if mode == "dsa":
            s = s + rest[0][j]
        if masked:
            s = jnp.where(key0 + j * tk <= qry, s, NEG)
        m_new = jnp.maximum(m, jnp.max(s, axis=0, keepdims=True))
        alpha = jnp.exp2(m - m_new)
        p = jnp.exp2(s - m_new)
        l = alpha * l + jnp.sum(p, axis=0, keepdims=True)
        vt = vt_ref[j, h * HEAD_DIM:(h + 1) * HEAD_DIM, :]
        acc = alpha * acc + jnp.dot(vt, p.astype(BF16), preferred_element_type=F32)
        return m_new, l, acc

    last = qi
    for h0 in range(0, n_heads, heads_per_loop):
        heads = range(h0, h0 + heads_per_loop)

        def step(j, carries, heads=heads):
            nxt = jnp.minimum(j + 1, last)
            out = []
            for h, (s, m, l, acc) in zip(heads, carries):
                s_next = scores(h, nxt)
                m, l, acc = update(h, j, s, m, l, acc, False)
                out.append((s_next, m, l, acc))
            return tuple(out)

        carries = tuple(
            (scores(h, 0), jnp.full((1, tq), NEG, F32), jnp.zeros((1, tq), F32),
             jnp.zeros((HEAD_DIM, tq), F32)) for h in heads)
        n_loop = last if mode == "fox" else last + 1
        carries = lax.fori_loop(0, n_loop, step, carries)
        for h, (s, m, l, acc) in zip(heads, carries):
            if mode == "fox":
                m, l, acc = update(h, last, s, m, l, acc, True)
            o_ref[:, h * HEAD_DIM:(h + 1) * HEAD_DIM] = jnp.transpose(acc / l).astype(o_ref.dtype)


def _attention(q_arr, q_blk, k_arr, k_blk, vt_arr, v_blk, extras, mode, n_heads, kdim, tq, tk):
    b, s, _ = q_arr.shape
    w = n_heads * HEAD_DIM
    wk = n_heads * kdim
    in_specs = [pl.BlockSpec((None, tq, wk), lambda bi, qi: (bi, qi, q_blk)),
                pl.BlockSpec((None, s, wk), lambda bi, qi: (bi, 0, k_blk)),
                pl.BlockSpec((None, s // tk, w, tk), lambda bi, qi: (bi, 0, v_blk, 0))]
    if mode == "dsa":
        in_specs += [pl.BlockSpec((None, None, s // tk, tk, tq), lambda bi, qi: (bi, qi, 0, 0, 0))]
    return pl.pallas_call(
        functools.partial(_attn_body, tq=tq, tk=tk, n_heads=n_heads, kdim=kdim, mode=mode,
                          heads_per_loop=ATT_HEADS_PER_LOOP),
        grid=(b, s // tq),
        in_specs=in_specs,
        out_specs=pl.BlockSpec((None, tq, w), lambda bi, qi: (bi, qi, 0)),
        out_shape=jax.ShapeDtypeStruct((b, s, w), BF16),
        compiler_params=_cparams("parallel", "parallel"),
        name="attn_" + mode,
    )(q_arr, k_arr, vt_arr, *extras)


def _select_body(iq_ref, ika_ref, ikb_ref, wt_ref, o_ref, keys_ref, *, tq, tk, n_chunks, topk):
    qi = pl.program_id(1)
    used = qi + 1
    key0 = lax.broadcasted_iota(jnp.int32, (tk, tq), 0)
    qry = qi * tq + lax.broadcasted_iota(jnp.int32, (tk, tq), 1)

    def score_chunk(c, _):
        off = pl.multiple_of(c * tk, tk)
        ka = ika_ref[pl.ds(off, tk), :]
        kb = ikb_ref[pl.ds(off, tk), :]
        acc = jnp.zeros((tk, tq), F32)
        for j in range(IDX_HEADS):
            pair = iq_ref[:, (j // 2) * LANES:(j // 2 + 1) * LANES]
            logit = lax.dot_general(ka if j % 2 == 0 else kb, pair, (((1,), (1,)), ((), ())),
                                    preferred_element_type=F32)
            acc = acc + wt_ref[j:j + 1, :] * jnp.maximum(logit, 0.0)
        bits = lax.bitcast_convert_type(acc, jnp.int32)
        key = jnp.where(bits < 0, bits ^ jnp.int32(0x7FFFFFFF), bits)
        keys_ref[c] = jnp.where(key0 + off <= qry, key, jnp.int32(INT_MIN))
        return 0

    lax.fori_loop(0, used, score_chunk, 0)

    def bit_step(i, prefix):
        cand_u = prefix | jnp.left_shift(jnp.int32(1), 31 - i)
        cand = cand_u ^ jnp.int32(INT_MIN)

        def count_chunk(c, cnt):
            ge = jnp.where(keys_ref[c] >= cand, 1.0, 0.0)
            return cnt + jnp.sum(ge.reshape(tk // 8, 8, tq), axis=0)

        cnt = lax.fori_loop(0, used, count_chunk, jnp.zeros((8, tq), F32))
        cnt = jnp.sum(cnt, axis=0, keepdims=True)
        return jnp.where(cnt >= topk, cand_u, prefix)

    prefix = lax.fori_loop(0, 32, bit_step, jnp.zeros((1, tq), jnp.int32))
    thr = jnp.maximum(prefix ^ jnp.int32(INT_MIN), jnp.int32(INT_MIN + 1))

    def write_chunk(c, _):
        o_ref[c] = jnp.where(keys_ref[c] >= thr, 0.0, NEG)
        return 0

    def fill_chunk(c, _):
        o_ref[c] = jnp.full((tk, tq), NEG, F32)
        return 0

    lax.fori_loop(0, used, write_chunk, 0)
    lax.fori_loop(used, n_chunks, fill_chunk, 0)


def _select_bias(iq, ika, ikb, wt, topk, tq, tk):
    b, s, _ = iq.shape
    n_chunks = s // tk
    return pl.pallas_call(
        functools.partial(_select_body, tq=tq, tk=tk, n_chunks=n_chunks, topk=topk),
        grid=(b, s // tq),
        in_specs=[pl.BlockSpec((None, tq, IDX_HEADS * IDX_DIM), lambda bi, qi: (bi, qi, 0)),
                  pl.BlockSpec((None, s, LANES), lambda bi, qi: (bi, 0, 0)),
                  pl.BlockSpec((None, s, LANES), lambda bi, qi: (bi, 0, 0)),
                  pl.BlockSpec((None, IDX_HEADS, tq), lambda bi, qi: (bi, 0, qi))],
        out_specs=pl.BlockSpec((None, None, n_chunks, tk, tq), lambda bi, qi: (bi, qi, 0, 0, 0)),
        out_shape=jax.ShapeDtypeStruct((b, s // tq, n_chunks, tk, tq), F32),
        scratch_shapes=[pltpu.VMEM((n_chunks, tk, tq), jnp.int32)],
        compiler_params=_cparams("parallel", "parallel"),
        name="index_select",
    )(iq, ika, ikb, wt)


def _cross_body(q_ref, k_ref, v_ref, o_ref, *, n_heads):
    for h in range(n_heads):
        hs = slice(h * HEAD_DIM, (h + 1) * HEAD_DIM)
        s = lax.dot_general(q_ref[:, hs], k_ref[:, hs], (((1,), (1,)), ((), ())),
                            preferred_element_type=F32)
        p = jnp.exp2(s - jnp.max(s, axis=-1, keepdims=True))
        l = jnp.sum(p, axis=-1, keepdims=True)
        o = jnp.dot(p.astype(BF16), v_ref[:, hs], preferred_element_type=F32)
        o_ref[:, hs] = (o / l).astype(o_ref.dtype)


def _cross_attention(q_arr, q_blk, mk, mv, tq):
    b, s, _ = q_arr.shape
    n_mem = mk.shape[1]
    w = CROSS_HEADS * HEAD_DIM
    return pl.pallas_call(
        functools.partial(_cross_body, n_heads=CROSS_HEADS),
        grid=(b, s // tq),
        in_specs=[pl.BlockSpec((None, tq, w), lambda bi, qi: (bi, qi, q_blk)),
                  pl.BlockSpec((None, n_mem, w), lambda bi, qi: (bi, 0, 0)),
                  pl.BlockSpec((None, n_mem, w), lambda bi, qi: (bi, 0, 0))],
        out_specs=pl.BlockSpec((None, tq, w), lambda bi, qi: (bi, qi, 0)),
        out_shape=jax.ShapeDtypeStruct((b, s, w), BF16),
        compiler_params=_cparams("parallel", "parallel"),
        name="cross_attn",
    )(q_arr, mk, mv)


def _sigmoid(x):
    return 1.0 / (1.0 + jnp.exp(-x))


def _merge_body(h_ref, of_ref, od_ref, oc_ref, x_ref, wgf_ref, wgd_ref, wgc_ref,
                wbf_ref, wbd_ref, wbc_ref, wo_ref, o_ref):
    j = pl.program_id(1)
    h = h_ref[...]
    tn = wo_ref.shape[0]
    sub = min(tn, MERGE_SUB)

    def branch(wg_ref, ob_ref, wb_ref, cs):
        gate = _sigmoid(jnp.dot(h, wg_ref[:, cs], preferred_element_type=F32))
        return gate * jnp.dot(ob_ref[...], wb_ref[:, cs], preferred_element_type=F32)

    part = None
    for c in range(tn // sub):
        cs = slice(c * sub, (c + 1) * sub)
        merged = branch(wgf_ref, of_ref, wbf_ref, cs) + branch(wgd_ref, od_ref, wbd_ref, cs)
        merged = merged + branch(wgc_ref, oc_ref, wbc_ref, cs)
        p = jnp.dot(merged.astype(BF16), wo_ref[cs, :], preferred_element_type=F32)
        part = p if part is None else part + p

    @pl.when(j == 0)
    def _():
        o_ref[...] = x_ref[...] + part

    @pl.when(j != 0)
    def _():
        o_ref[...] += part


def _merge(h, o_fox, o_dsa, o_cross, x, w_gates, wbf, wbd, wbc, w_out, tm, tn):
    m, d = x.shape
    nj = d // tn
    row = lambda i, j: (i, 0)
    in_specs = [pl.BlockSpec((tm, d), row),
                pl.BlockSpec((tm, o_fox.shape[1]), row),
                pl.BlockSpec((tm, o_dsa.shape[1]), row),
                pl.BlockSpec((tm, o_cross.shape[1]), row),
                pl.BlockSpec((tm, d), row),
                pl.BlockSpec((d, tn), lambda i, j: (0, j)),
                pl.BlockSpec((d, tn), lambda i, j: (0, nj + j)),
                pl.BlockSpec((d, tn), lambda i, j: (0, 2 * nj + j)),
                pl.BlockSpec((wbf.shape[0], tn), lambda i, j: (0, j)),
                pl.BlockSpec((wbd.shape[0], tn), lambda i, j: (0, j)),
                pl.BlockSpec((wbc.shape[0], tn), lambda i, j: (0, j)),
                pl.BlockSpec((tn, d), lambda i, j: (j, 0))]
    return pl.pallas_call(
        _merge_body,
        grid=(m // tm, nj),
        in_specs=in_specs,
        out_specs=pl.BlockSpec((tm, d), row),
        out_shape=jax.ShapeDtypeStruct((m, d), F32),
        compiler_params=_cparams("parallel", "arbitrary"),
        name="gated_merge",
    )(h, o_fox, o_dsa, o_cross, x, w_gates, w_gates, w_gates, wbf, wbd, wbc, w_out)


def _router_body(x_ref, g_ref, w_ref, b_ref, h_ref, eid_ref, gate_ref):
    x = x_ref[...]
    r = lax.rsqrt(jnp.mean(x * x, axis=-1, keepdims=True) + EPS)
    h = (x * r) * g_ref[...]
    h_ref[...] = h
    logits = jnp.dot(h.astype(BF16), w_ref[...], preferred_element_type=F32) + b_ref[...]
    lane = lax.broadcasted_iota(jnp.int32, logits.shape, 1).astype(F32)

    def masked_max(mask):
        return jnp.max(jnp.where(mask, logits, -jnp.inf), axis=-1, keepdims=True)

    def first_lane(mask):
        return jnp.min(jnp.where(mask, lane, float(LANES)), axis=-1, keepdims=True)

    is_group = lane < N_GROUPS
    gmax = masked_max(is_group)
    grp = first_lane(is_group & (logits == gmax))
    denom = jnp.sum(jnp.where(is_group, jnp.exp(logits - gmax), 0.0), axis=-1, keepdims=True)
    p_sel = 1.0 / denom
    lo = N_GROUPS + grp * EXPERTS_PER_GROUP
    in_grp = (lane >= lo) & (lane < lo + EXPERTS_PER_GROUP)
    e1 = masked_max(in_grp)
    j1 = first_lane(in_grp & (logits == e1))
    rest = in_grp & (lane != j1)
    e2 = masked_max(rest)
    j2 = first_lane(rest & (logits == e2))
    t = jnp.exp(e2 - e1)
    g1 = p_sel * (1.0 / (1.0 + t))
    g2 = p_sel * (t / (1.0 + t))
    eid = jnp.where(lane == 0.0, j1 - N_GROUPS, jnp.where(lane == 1.0, j2 - N_GROUPS, 0.0))
    eid_ref[...] = eid.astype(jnp.int32)
    gate_ref[...] = jnp.where(lane == 0.0, g1, jnp.where(lane == 1.0, g2, 0.0))


def _router(x, g, w_router, b_router, tm):
    m, d = x.shape
    return pl.pallas_call(
        _router_body,
        grid=(m // tm,),
        in_specs=[pl.BlockSpec((tm, d), lambda i: (i, 0)),
                  pl.BlockSpec((1, d), lambda i: (0, 0)),
                  pl.BlockSpec((d, LANES), lambda i: (0, 0)),
                  pl.BlockSpec((1, LANES), lambda i: (0, 0))],
        out_specs=[pl.BlockSpec((tm, d), lambda i: (i, 0)),
                   pl.BlockSpec((tm, LANES), lambda i: (i, 0)),
                   pl.BlockSpec((tm, LANES), lambda i: (i, 0))],
        out_shape=[jax.ShapeDtypeStruct((m, d), F32),
                   jax.ShapeDtypeStruct((m, LANES), jnp.int32),
                   jax.ShapeDtypeStruct((m, LANES), F32)],
        compiler_params=_cparams("parallel"),
        name="moe_router",
    )(x, g.reshape(1, d), w_router, b_router.reshape(1, LANES))


def _expert_body(ge_ref, gb0_ref, gnb_ref, used_ref, tok_ref, h_hbm, wg_ref, wu_ref, wd_ref, ys_hbm,
                 xs_ref, xb_ref, acc_ref, wgb_ref, wub_ref, wdb_ref, sem_in, sem_out,
                 *, bm, n_f, n_blocks):
    g = pl.program_id(0)
    f = pl.program_id(1)
    nb = gnb_ref[g]
    blk0 = gb0_ref[g]

    @pl.when(jnp.logical_and(g == pl.num_programs(0) - 1, f == n_f - 1))
    def _():
        xs_ref[0, pl.ds(0, bm), :] = jnp.zeros((bm, xs_ref.shape[2]), F32)

        def tail_copy(i):
            return pltpu.make_async_copy(xs_ref.at[0, pl.ds(0, bm)], ys_hbm.at[pl.ds(i * bm, bm)],
                                         sem_out)

        def start_tail(i, _):
            tail_copy(i).start()
            return 0

        lax.fori_loop(used_ref[0], n_blocks, start_tail, 0)

        def wait_tail(i, _):
            tail_copy(i).wait()
            return 0

        lax.fori_loop(used_ref[0], n_blocks, wait_tail, 0)

    def gather(grp):
        slot = grp % 2
        base = gb0_ref[grp] * bm

        def issue(r, _):
            pltpu.make_async_copy(h_hbm.at[pl.ds(tok_ref[base + r], 1)],
                                  xs_ref.at[slot, pl.ds(r, 1)], sem_in.at[slot]).start()
            return 0

        lax.fori_loop(0, gnb_ref[grp] * bm, issue, 0)

    def block_copy_out(i):
        return pltpu.make_async_copy(acc_ref.at[pl.ds(i * bm, bm)],
                                     ys_hbm.at[pl.ds((blk0 + i) * bm, bm)], sem_out)

    @pl.when(jnp.logical_and(g == 0, f == 0))
    def _():
        gather(g)

    @pl.when(jnp.logical_and(nb > 0, f == 0))
    def _():
        slot = g % 2

        def land(i, _):
            rows = pl.ds(pl.multiple_of(i * bm, bm), bm)
            pltpu.make_async_copy(h_hbm.at[pl.ds(0, bm)], xs_ref.at[slot, rows], sem_in.at[slot]).wait()
            return 0

        lax.fori_loop(0, nb, land, 0)

        def cast(i, _):
            rows = pl.ds(pl.multiple_of(i * bm, bm), bm)
            xb_ref[rows, :] = xs_ref[slot, rows, :].astype(BF16)
            return 0

        lax.fori_loop(0, nb, cast, 0)

    @pl.when(jnp.logical_and(f == 0, g + 1 < pl.num_programs(0)))
    def _():
        gather(g + 1)

    @pl.when(nb > 0)
    def _():
        wgb_ref[...] = wg_ref[...].astype(BF16)
        wub_ref[...] = wu_ref[...].astype(BF16)
        wdb_ref[...] = wd_ref[...].astype(BF16)

        def block(i, _):
            rows = pl.ds(pl.multiple_of(i * bm, bm), bm)
            xb = xb_ref[rows, :]
            a = jnp.dot(xb, wgb_ref[...], preferred_element_type=F32)
            u = jnp.dot(xb, wub_ref[...], preferred_element_type=F32)
            hid = (a * _sigmoid(a)) * u
            y = jnp.dot(hid.astype(BF16), wdb_ref[...], preferred_element_type=F32)

            @pl.when(f == 0)
            def _():
                acc_ref[rows, :] = y

            @pl.when(f != 0)
            def _():
                acc_ref[rows, :] += y

            return 0

        lax.fori_loop(0, nb, block, 0)

    @pl.when(jnp.logical_and(nb > 0, f == n_f - 1))
    def _():
        def start_out(i, _):
            block_copy_out(i).start()
            return 0

        lax.fori_loop(0, nb, start_out, 0)

        def wait_out(i, _):
            block_copy_out(i).wait()
            return 0

        lax.fori_loop(0, nb, wait_out, 0)


def _experts(h2, slot_tok, grp_e, grp_blk0, grp_nb, n_used, w_gate, w_up, w_down, n_slots):
    d = h2.shape[1]
    ff = w_gate.shape[2]
    n_f = ff // MOE_FC
    n_groups = grp_e.shape[0]
    rmax = MOE_GROUP_BLOCKS * MOE_BM

    def f_idx(g, f, gnb):
        return jnp.where(gnb[g] > 0, f, n_f - 1)

    grid_spec = pltpu.PrefetchScalarGridSpec(
        num_scalar_prefetch=5,
        grid=(n_groups, n_f),
        in_specs=[
            pl.BlockSpec(memory_space=pl.ANY),
            pl.BlockSpec((None, d, MOE_FC), lambda g, f, ge, gb0, gnb, *_: (ge[g], 0, f_idx(g, f, gnb))),
            pl.BlockSpec((None, d, MOE_FC), lambda g, f, ge, gb0, gnb, *_: (ge[g], 0, f_idx(g, f, gnb))),
            pl.BlockSpec((None, MOE_FC, d), lambda g, f, ge, gb0, gnb, *_: (ge[g], f_idx(g, f, gnb), 0)),
        ],
        out_specs=pl.BlockSpec(memory_space=pl.ANY),
        scratch_shapes=[pltpu.VMEM((2, rmax, d), F32),
                        pltpu.VMEM((rmax, d), BF16),
                        pltpu.VMEM((rmax, d), F32),
                        pltpu.VMEM((d, MOE_FC), BF16),
                        pltpu.VMEM((d, MOE_FC), BF16),
                        pltpu.VMEM((MOE_FC, d), BF16),
                        pltpu.SemaphoreType.DMA((2,)),
                        pltpu.SemaphoreType.DMA(())],
    )
    return pl.pallas_call(
        functools.partial(_expert_body, bm=MOE_BM, n_f=n_f, n_blocks=n_slots // MOE_BM),
        grid_spec=grid_spec,
        out_shape=jax.ShapeDtypeStruct((n_slots, d), F32),
        compiler_params=_cparams("arbitrary", "arbitrary"),
        name="moe_experts",
    )(grp_e, grp_blk0, grp_nb, n_used, slot_tok, h2, w_gate, w_up, w_down)


def _combine_body(pos_ref, x_ref, gate_ref, ys_hbm, o_ref, buf_ref, sem, *, tm):
    i = pl.program_id(0)
    n_steps = pl.num_programs(0)

    def issue(step):
        slot = step % 2

        def body(r, _):
            for k in range(MOE_TOP_K):
                src = pos_ref[(step * tm + r) * MOE_TOP_K + k]
                pltpu.make_async_copy(ys_hbm.at[pl.ds(src, 1)], buf_ref.at[slot, k, pl.ds(r, 1)],
                                      sem.at[slot]).start()
            return 0

        lax.fori_loop(0, tm, body, 0)

    @pl.when(i == 0)
    def _():
        issue(i)

    @pl.when(i + 1 < n_steps)
    def _():
        issue(i + 1)

    slot = i % 2
    for k in range(MOE_TOP_K):
        pltpu.make_async_copy(ys_hbm.at[pl.ds(0, tm)], buf_ref.at[slot, k], sem.at[slot]).wait()
    o_ref[...] = x_ref[...] + (buf_ref[slot, 0] * gate_ref[:, 0:1] + buf_ref[slot, 1] * gate_ref[:, 1:2])


def _combine(x, gate, ys, pos, tm):
    m, d = x.shape
    grid_spec = pltpu.PrefetchScalarGridSpec(
        num_scalar_prefetch=1,
        grid=(m // tm,),
        in_specs=[pl.BlockSpec((tm, d), lambda i, pos: (i, 0)),
                  pl.BlockSpec((tm, LANES), lambda i, pos: (i, 0)),
                  pl.BlockSpec(memory_space=pl.ANY)],
        out_specs=pl.BlockSpec((tm, d), lambda i, pos: (i, 0)),
        scratch_shapes=[pltpu.VMEM((2, MOE_TOP_K, tm, d), F32), pltpu.SemaphoreType.DMA((2,))],
    )
    return pl.pallas_call(
        functools.partial(_combine_body, tm=tm),
        grid_spec=grid_spec,
        out_shape=jax.ShapeDtypeStruct((m, d), F32),
        compiler_params=_cparams("arbitrary"),
        name="moe_combine",
    )(pos, x, gate, ys)


def _moe_plan(eid, n_tokens):
    a = n_tokens * MOE_TOP_K
    eid = eid.reshape(a)
    n_blocks = (a + N_EXPERTS * (MOE_BM - 1)) // MOE_BM
    n_slots = n_blocks * MOE_BM
    n_groups = n_blocks // MOE_GROUP_BLOCKS + N_EXPERTS
    onehot = (eid[:, None] == jnp.arange(N_EXPERTS, dtype=jnp.int32)[None, :]).astype(jnp.int32)
    rank = jnp.take_along_axis(jnp.cumsum(onehot, axis=0), eid[:, None], axis=1)[:, 0] - 1
    counts = jnp.sum(onehot, axis=0)
    blocks_e = (counts + MOE_BM - 1) // MOE_BM
    blk_start = jnp.cumsum(blocks_e) - blocks_e
    pos = (blk_start[eid] * MOE_BM + rank).astype(jnp.int32)
    tok = jnp.arange(a, dtype=jnp.int32) // MOE_TOP_K
    slot_tok = jnp.zeros((n_slots,), jnp.int32).at[pos].set(tok)
    groups_e = (blocks_e + MOE_GROUP_BLOCKS - 1) // MOE_GROUP_BLOCKS
    grp_end = jnp.cumsum(groups_e)
    grp_start = grp_end - groups_e
    gidx = jnp.arange(n_groups, dtype=jnp.int32)
    ge = jnp.minimum(jnp.searchsorted(grp_end, gidx, side="right"), N_EXPERTS - 1).astype(jnp.int32)
    local = gidx - grp_start[ge]
    active = gidx < grp_end[-1]
    gb0 = jnp.where(active, blk_start[ge] + local * MOE_GROUP_BLOCKS, 0).astype(jnp.int32)
    gnb = jnp.where(active, jnp.minimum(blocks_e[ge] - local * MOE_GROUP_BLOCKS, MOE_GROUP_BLOCKS),
                    0).astype(jnp.int32)
    last_e = ge[jnp.maximum(grp_end[-1] - 1, 0)]
    ge = jnp.where(active, ge, last_e).astype(jnp.int32)
    n_used = jnp.sum(blocks_e).astype(jnp.int32).reshape(1)
    return pos, slot_tok, ge, gb0, gnb, n_used, n_slots


def _layer(x, mem, positions, attn_norm_g, mem_norm_g, w_in, fox_forget_b, fox_q_norm_g,
           fox_k_norm_g, dsa_q_norm_g, dsa_k_norm_g, idx_k_norm_g, cross_q_norm_g, cross_k_norm_g,
           w_mem_kv, w_branch_fox, w_branch_dsa, w_branch_cross, w_out, ffn_norm_g,
           router_group_w, router_group_b, router_expert_w, router_expert_b,
           expert_w_gate, expert_w_up, expert_w_down):
    b, s, d = x.shape
    n = b * s
    n_mem = mem.shape[1]
    fw, dw, cw = FOX_HEADS * HEAD_DIM, DSA_HEADS * HEAD_DIM, CROSS_HEADS * HEAD_DIM
    iqw = IDX_HEADS * IDX_DIM
    tm = min(512, n)
    x2d = x.reshape(n, d)

    sizes = (fw, fw, fw, FOX_HEADS, dw, dw, dw, iqw, IDX_HEADS, IDX_DIM, cw, 3 * d)
    offs = [0]
    for sz in sizes:
        offs.append(offs[-1] + sz)
    (c_fq, c_fk, c_fv, c_fl, c_dq, c_dk, c_dv, c_iq, c_iw, c_ik, c_cq, c_g) = range(12)

    def group(*cols):
        pieces, dst = [], 0
        for c in cols:
            pieces.append((dst, offs[c], sizes[c]))
            dst += sizes[c]
        return tuple(pieces)

    plan = (group(c_fq, c_fk, c_cq), group(c_dq, c_dk), group(c_fv, c_dv), group(c_iq),
            group(c_ik, c_iw, c_fl), group(c_g))
    w_qk1, w_qk2, w_vv, w_iq, w_small, w_gates = _w_prep(
        w_in, plan, (2 * fw + cw, 2 * dw, fw + dw, iqw, LANES, 3 * d), min(256, d))
    q_unit = (HEAD_DIM ** -0.5) * LOG2E
    g_qk1 = jnp.concatenate([jnp.tile(fox_q_norm_g * q_unit, FOX_HEADS),
                             jnp.tile(fox_k_norm_g, FOX_HEADS),
                             jnp.tile(cross_q_norm_g * q_unit, CROSS_HEADS)])
    g_qk2 = jnp.concatenate([jnp.tile(dsa_q_norm_g * q_unit, DSA_HEADS),
                             jnp.tile(dsa_k_norm_g, DSA_HEADS)])
    g_ik = jnp.concatenate([idx_k_norm_g, jnp.zeros((LANES - IDX_DIM,), F32)])
    fb_lane = jnp.zeros((LANES,), F32).at[SMALL_F:SMALL_F + FOX_HEADS].set(fox_forget_b)

    pos_col = positions.reshape(n, 1).astype(F32)
    cd, sd = _rope_tables(pos_col, *_rope_lane_patterns(ROT_DIM, HEAD_DIM), tm=tm)
    ci, si = _rope_tables(pos_col, *_rope_lane_patterns(IDX_ROT_DIM, IDX_DIM), tm=tm)

    h = _rmsnorm(x2d, attn_norm_g, BF16, tm)
    qk1 = _proj(h, w_qk1, "norm", BF16, tm, 1024, gains=g_qk1)
    qk2 = _proj(h, w_qk2, "norm_rope", BF16, tm, 768, gains=g_qk2, tables=(cd, sd))
    vv = _proj(h, w_vv, "raw", BF16, tm, 768)
    iq = _proj(h, w_iq, "rope_idx", BF16, tm, 1024, tables=(ci, si))
    small = _proj(h, w_small, "raw", F32, tm, LANES)
    ika, ikb, wl, lf_col = _small(small, g_ik, fb_lane, ci, si, tm)

    tq, tk = min(ATT_TQ, s), min(ATT_TK, s)
    assert tq == tk and s % tq == 0
    cum_col = _forget_cumsum(lf_col.reshape(b, s, LANES))
    qk1 = qk1.reshape(b, s, 2 * fw + cw)
    qk2 = qk2.reshape(b, s, 2 * dw)
    q_aug, k_aug = _fox_pack(qk1, cum_col, FOX_HEADS, min(512, s))
    vt = jnp.transpose(vv.reshape(b, s // tk, tk, fw + dw), (0, 1, 3, 2))
    o_fox = _attention(q_aug, 0, k_aug, 0, vt, 0, (), "fox", FOX_HEADS, AUG_DIM, tq, tk)

    topk = min(INDEX_TOPK, s // 4)
    wt = jnp.transpose(wl.reshape(b, s, LANES)[:, :, :IDX_HEADS], (0, 2, 1))
    bias = _select_bias(iq.reshape(b, s, iqw), ika.reshape(b, s, LANES), ikb.reshape(b, s, LANES),
                        wt, topk, tq, tk)
    o_dsa = _attention(qk2, 0, qk2, 1, vt, 1, (bias,), "dsa", DSA_HEADS, HEAD_DIM, tq, tk)

    tmm = min(512, b * n_mem)
    m_n = _rmsnorm(mem.reshape(b * n_mem, d), mem_norm_g, BF16, tmm)
    w_kv = w_mem_kv.astype(BF16)
    mk = _proj(m_n, w_kv[:, :cw], "norm", BF16, tmm, cw, gains=jnp.tile(cross_k_norm_g, CROSS_HEADS))
    mv = _proj(m_n, w_kv[:, cw:], "raw", BF16, tmm, cw)
    o_cross = _cross_attention(qk1, (2 * fw) // cw, mk.reshape(b, n_mem, cw), mv.reshape(b, n_mem, cw),
                               min(512, s))

    x_mid = _merge(h, o_fox.reshape(n, fw), o_dsa.reshape(n, dw), o_cross.reshape(n, cw), x2d,
                   w_gates, w_branch_fox.astype(BF16), w_branch_dsa.astype(BF16),
                   w_branch_cross.astype(BF16), w_out.astype(BF16), tm, MERGE_TN)

    w_router = jnp.concatenate(
        [router_group_w, router_expert_w, jnp.zeros((d, LANES - N_GROUPS - N_EXPERTS), F32)],
        axis=1).astype(BF16)
    b_router = jnp.concatenate(
        [router_group_b, router_expert_b, jnp.zeros((LANES - N_GROUPS - N_EXPERTS,), F32)])
    h2, eid, gate = _router(x_mid, ffn_norm_g, w_router, b_router, tm)
    pos, slot_tok, ge, gb0, gnb, n_used, n_slots = _moe_plan(eid[:, :MOE_TOP_K], n)
    ys = _experts(h2, slot_tok, ge, gb0, gnb, n_used, expert_w_gate, expert_w_up, expert_w_down,
                  n_slots)
    out = _combine(x_mid, gate, ys, pos, min(256, n))
    return out.reshape(b, s, d)


def kernel(x, mem, positions, attn_norm_g, mem_norm_g, w_in, fox_forget_b, fox_q_norm_g, fox_k_norm_g, dsa_q_norm_g, dsa_k_norm_g, idx_k_norm_g, cross_q_norm_g, cross_k_norm_g, w_mem_kv, w_branch_fox, w_branch_dsa, w_branch_cross, w_out, ffn_norm_g, router_group_w, router_group_b, router_expert_w, router_expert_b, expert_w_gate, expert_w_up, expert_w_down):
    depth = w_in.shape[0]
    for layer in range(depth):
        x = _layer(
            x, mem, positions, attn_norm_g[layer], mem_norm_g[layer], w_in[layer],
            fox_forget_b[layer], fox_q_norm_g[layer], fox_k_norm_g[layer],
            dsa_q_norm_g[layer], dsa_k_norm_g[layer], idx_k_norm_g[layer],
            cross_q_norm_g[layer], cross_k_norm_g[layer], w_mem_kv[layer],
            w_branch_fox[layer], w_branch_dsa[layer], w_branch_cross[layer], w_out[layer],
            ffn_norm_g[layer], router_group_w[layer], router_group_b[layer],
            router_expert_w[layer], router_expert_b[layer], expert_w_gate[layer],
            expert_w_up[layer], expert_w_down[layer])
    return x
```

```python
import functools

import jax
import jax.numpy as jnp
from jax import lax
from jax.experimental import pallas as pl
from jax.experimental.pallas import tpu as pltpu

F32 = jnp.float32
BF16 = jnp.bfloat16

LANES = 128
HEAD_DIM = 128
FOX_HEADS = 6
DSA_HEADS = 6
CROSS_HEADS = 4
IDX_HEADS = 16
IDX_DIM = 64
ROPE_THETA = 500000.0
ROT_DIM = HEAD_DIM // 4
IDX_ROT_DIM = IDX_DIM // 4
INDEX_TOPK = 256
N_GROUPS = 4
EXPERTS_PER_GROUP = 8
N_EXPERTS = N_GROUPS * EXPERTS_PER_GROUP
MOE_TOP_K = 2
EPS = 1e-6

NEG = -1e30
INT_MIN = -(2 ** 31)
VMEM_LIMIT = 56 * 1024 * 1024

ATT_TQ = 256
ATT_TK = 256
ATT_HEADS_PER_LOOP = 6
MERGE_TN = 512
MERGE_SUB = 256
MOE_BM = 256
MOE_GROUP_BLOCKS = 4
MOE_FC = 256


def _cparams(*sem):
    return pltpu.CompilerParams(dimension_semantics=sem, vmem_limit_bytes=VMEM_LIMIT)


def _rmsnorm_body(x_ref, g_ref, o_ref):
    x = x_ref[...]
    r = lax.rsqrt(jnp.mean(x * x, axis=-1, keepdims=True) + EPS)
    o_ref[...] = ((x * r) * g_ref[...]).astype(o_ref.dtype)


def _rmsnorm(x2d, g, out_dtype, tm):
    m, d = x2d.shape
    return pl.pallas_call(
        _rmsnorm_body,
        grid=(m // tm,),
        in_specs=[pl.BlockSpec((tm, d), lambda i: (i, 0)),
                  pl.BlockSpec((1, d), lambda i: (0, 0))],
        out_specs=pl.BlockSpec((tm, d), lambda i: (i, 0)),
        out_shape=jax.ShapeDtypeStruct((m, d), out_dtype),
        compiler_params=_cparams("parallel"),
        name="rmsnorm",
    )(x2d, g.reshape(1, d))


W_PREP_ROWS = 128


def _w_prep_body(src_ref, w_ref, o_ref):
    del src_ref
    o_ref[...] = w_ref[:, 0, :].astype(o_ref.dtype)


def _w_prep(w_t, src_rows):
    _, _, k = w_t.shape
    n_chunks = src_rows.shape[0]
    grid_spec = pltpu.PrefetchScalarGridSpec(
        num_scalar_prefetch=1,
        grid=(n_chunks,),
        in_specs=[pl.BlockSpec((pl.Element(W_PREP_ROWS), pl.Element(1), pl.Element(k)),
                               lambda c, src: (src[c], 0, 0))],
        out_specs=pl.BlockSpec((W_PREP_ROWS, k), lambda c, src: (c, 0)),
    )
    return pl.pallas_call(
        _w_prep_body,
        grid_spec=grid_spec,
        out_shape=jax.ShapeDtypeStruct((n_chunks * W_PREP_ROWS, k), BF16),
        compiler_params=_cparams("parallel"),
        name="w_prep",
    )(src_rows, w_t)


def _rope_table_body(pos_ref, f_ref, sgn_ref, c_ref, s_ref):
    ang = pos_ref[...] * f_ref[...]
    c_ref[...] = jnp.cos(ang)
    s_ref[...] = jnp.sin(ang) * sgn_ref[...]


def _rope_tables(pos_col, freq_lane, sign_lane, tm):
    n = pos_col.shape[0]
    row = pl.BlockSpec((1, LANES), lambda i: (0, 0))
    tab = pl.BlockSpec((tm, LANES), lambda i: (i, 0))
    return pl.pallas_call(
        _rope_table_body,
        grid=(n // tm,),
        in_specs=[pl.BlockSpec((tm, 1), lambda i: (i, 0)), row, row],
        out_specs=[tab, tab],
        out_shape=[jax.ShapeDtypeStruct((n, LANES), F32)] * 2,
        compiler_params=_cparams("parallel"),
        name="rope_tables",
    )(pos_col, freq_lane.reshape(1, LANES), sign_lane.reshape(1, LANES))


def _rope_lane_patterns(rot_dim, period):
    half = rot_dim // 2
    inv_freq = jnp.power(jnp.float32(ROPE_THETA), -jnp.arange(half, dtype=F32) * 2.0 / rot_dim)
    lane = jnp.arange(LANES) % period
    freq = jnp.where(lane < rot_dim, inv_freq[lane % half], 0.0).astype(F32)
    sign = jnp.where(lane < half, -1.0, jnp.where(lane < rot_dim, 1.0, 0.0)).astype(F32)
    return freq, sign


def _apply_rope(a, c, s, half, period):
    lane = lax.broadcasted_iota(jnp.int32, a.shape, 1)
    first = (lane & (period - 1)) < half
    partner = jnp.where(first, pltpu.roll(a, LANES - half, 1), pltpu.roll(a, half, 1))
    return a * c + partner * s


def _proj_body(*refs, mode, n_lane_groups, w_rows):
    h_ref, w_ref = refs[0], refs[1]
    o_ref = refs[-1]
    if w_rows:
        acc = lax.dot_general(h_ref[...], w_ref[...], (((1,), (1,)), ((), ())),
                              preferred_element_type=F32)
    else:
        acc = jnp.dot(h_ref[...], w_ref[...], preferred_element_type=F32)
    if mode == "raw":
        o_ref[...] = acc.astype(o_ref.dtype)
        return
    for g in range(n_lane_groups):
        sl = slice(g * LANES, (g + 1) * LANES)
        a = acc[:, sl]
        if mode in ("norm", "norm_rope"):
            g_ref = refs[2]
            a = (a * lax.rsqrt(jnp.mean(a * a, axis=-1, keepdims=True) + EPS)) * g_ref[:, sl]
        if mode == "norm_rope":
            a = _apply_rope(a, refs[3][...], refs[4][...], ROT_DIM // 2, HEAD_DIM)
        if mode == "rope_idx":
            a = _apply_rope(a, refs[2][...], refs[3][...], IDX_ROT_DIM // 2, IDX_DIM)
        o_ref[:, sl] = a.astype(o_ref.dtype)


def _proj(h, w, mode, out_dtype, tm, tn, gains=None, tables=None, w_rows=None):
    m, k = h.shape
    if w_rows is None:
        n = w.shape[1]
        w_spec = pl.BlockSpec((k, tn), lambda j, i: (0, j))
    else:
        row0, n = w_rows
        assert row0 % tn == 0 and n % tn == 0
        w_spec = pl.BlockSpec((tn, k), lambda j, i: (row0 // tn + j, 0))
    in_specs = [pl.BlockSpec((tm, k), lambda j, i: (i, 0)), w_spec]
    args = [h, w]
    if gains is not None:
        in_specs.append(pl.BlockSpec((1, tn), lambda j, i: (0, j)))
        args.append(gains.reshape(1, n))
    if tables is not None:
        for t in tables:
            in_specs.append(pl.BlockSpec((tm, LANES), lambda j, i: (i, 0)))
            args.append(t)
    return pl.pallas_call(
        functools.partial(_proj_body, mode=mode, n_lane_groups=tn // LANES,
                          w_rows=w_rows is not None),
        grid=(n // tn, m // tm),
        in_specs=in_specs,
        out_specs=pl.BlockSpec((tm, tn), lambda j, i: (i, j)),
        out_shape=jax.ShapeDtypeStruct((m, n), out_dtype),
        compiler_params=_cparams("parallel", "parallel"),
        name="proj_" + mode,
    )(*args)


SMALL_IK = 0
SMALL_IW = 64
SMALL_F = 80


def _small_body(x_ref, gk_ref, fb_ref, c_ref, s_ref, ika_ref, ikb_ref, wl_ref, lf_ref):
    x = x_ref[...]
    lane = lax.broadcasted_iota(jnp.int32, x.shape, 1)
    is_ik = lane < IDX_DIM
    ik = jnp.where(is_ik, x, 0.0)
    ms = jnp.sum(ik * ik, axis=-1, keepdims=True) * (1.0 / IDX_DIM)
    ik = (ik * lax.rsqrt(ms + EPS)) * gk_ref[...]
    ik = _apply_rope(ik, c_ref[...], s_ref[...], IDX_ROT_DIM // 2, IDX_DIM)
    ik = jnp.where(is_ik, ik, 0.0)
    ika_ref[...] = ik.astype(BF16)
    ikb_ref[...] = pltpu.roll(ik, IDX_DIM, 1).astype(BF16)
    wl_ref[...] = pltpu.roll(x, LANES - SMALL_IW, 1) * (2.0 ** -5)
    z = x + fb_ref[...]
    log_f = jnp.minimum(z, 0.0) - jnp.log1p(jnp.exp(-jnp.abs(z)))
    lf_ref[...] = pltpu.roll(log_f, LANES - SMALL_F, 1)


def _small(x, idx_k_gain_lane, forget_b_lane, ci, si, tm):
    n = x.shape[0]
    row = pl.BlockSpec((1, LANES), lambda i: (0, 0))
    tab = pl.BlockSpec((tm, LANES), lambda i: (i, 0))
    return pl.pallas_call(
        _small_body,
        grid=(n // tm,),
        in_specs=[tab, row, row, tab, tab],
        out_specs=[tab, tab, tab, tab],
        out_shape=[jax.ShapeDtypeStruct((n, LANES), BF16), jax.ShapeDtypeStruct((n, LANES), BF16),
                   jax.ShapeDtypeStruct((n, LANES), F32), jax.ShapeDtypeStruct((n, LANES), F32)],
        compiler_params=_cparams("parallel"),
        name="small_cols",
    )(x, idx_k_gain_lane.reshape(1, LANES), forget_b_lane.reshape(1, LANES), ci, si)


def _cumsum_body(col_ref, ccol_ref, *, seq):
    c = col_ref[...]
    ci = lax.broadcasted_iota(jnp.int32, c.shape, 0)
    sh = 1
    while sh < seq:
        c = c + jnp.where(ci >= sh, pltpu.roll(c, sh, 0), 0.0)
        sh *= 2
    ccol_ref[...] = c


def _forget_cumsum(lf_col):
    b, s, _ = lf_col.shape
    return pl.pallas_call(
        functools.partial(_cumsum_body, seq=s),
        grid=(b,),
        in_specs=[pl.BlockSpec((None, s, LANES), lambda i: (i, 0, 0))],
        out_specs=pl.BlockSpec((None, s, LANES), lambda i: (i, 0, 0)),
        out_shape=jax.ShapeDtypeStruct((b, s, LANES), F32),
        compiler_params=_cparams("parallel"),
        name="forget_cumsum",
    )(lf_col)


AUG_DIM = 2 * HEAD_DIM
LOG2E = 1.4426950408889634


def _split3(x):
    hi = x.astype(BF16).astype(F32)
    r1 = x - hi
    mid = r1.astype(BF16).astype(F32)
    lo = (r1 - mid).astype(BF16).astype(F32)
    return hi, mid, lo


def _fox_pack_body(q_ref, k_ref, cum_ref, qa_ref, ka_ref, *, n_heads):
    rows = q_ref.shape[0]
    lane = lax.broadcasted_iota(jnp.int32, (rows, LANES), 1)
    for h in range(n_heads):
        c = jnp.broadcast_to(cum_ref[:, h:h + 1], (rows, LANES)) * LOG2E
        qh, qm, ql = _split3(c)
        kh, km, kl = _split3(-c)
        q_aug = jnp.where(lane == 0, qh, jnp.where(lane == 1, qm, jnp.where(
            lane == 2, ql, jnp.where(lane < 6, 1.0, 0.0))))
        k_aug = jnp.where(lane < 3, 1.0, jnp.where(lane == 3, kh, jnp.where(
            lane == 4, km, jnp.where(lane == 5, kl, 0.0))))
        base = h * AUG_DIM
        qa_ref[:, base:base + HEAD_DIM] = q_ref[:, h * HEAD_DIM:(h + 1) * HEAD_DIM]
        ka_ref[:, base:base + HEAD_DIM] = k_ref[:, h * HEAD_DIM:(h + 1) * HEAD_DIM]
        qa_ref[:, base + HEAD_DIM:base + AUG_DIM] = q_aug.astype(BF16)
        ka_ref[:, base + HEAD_DIM:base + AUG_DIM] = k_aug.astype(BF16)


def _fox_pack(qk, cum_col, n_heads, ts):
    b, s, _ = qk.shape
    w = n_heads * HEAD_DIM
    wa = n_heads * AUG_DIM
    return pl.pallas_call(
        functools.partial(_fox_pack_body, n_heads=n_heads),
        grid=(b, s // ts),
        in_specs=[pl.BlockSpec((None, ts, w), lambda bi, i: (bi, i, 0)),
                  pl.BlockSpec((None, ts, w), lambda bi, i: (bi, i, 1)),
                  pl.BlockSpec((None, ts, LANES), lambda bi, i: (bi, i, 0))],
        out_specs=[pl.BlockSpec((None, ts, wa), lambda bi, i: (bi, i, 0)),
                   pl.BlockSpec((None, ts, wa), lambda bi, i: (bi, i, 0))],
        out_shape=[jax.ShapeDtypeStruct((b, s, wa), BF16)] * 2,
        compiler_params=_cparams("parallel", "parallel"),
        name="fox_pack",
    )(qk, qk, cum_col)


def _attn_body(q_ref, k_ref, vt_ref, *rest, tq, tk, n_heads, kdim, mode, heads_per_loop):
    o_ref = rest[-1]
    qi = pl.program_id(1)
    key0 = lax.broadcasted_iota(jnp.int32, (tk, tq), 0)
    qry = qi * tq + lax.broadcasted_iota(jnp.int32, (tk, tq), 1)
    def scores(h, j):
        off = pl.multiple_of(j * tk, tk)
        q = q_ref[:, h * kdim:(h + 1) * kdim]
        k = k_ref[pl.ds(off, tk), h * kdim:(h + 1) * kdim]
        return lax.dot_general(k, q, (((1,), (1,)), ((), ())), preferred_element_type=F32)

    def update(h, j, s, m, l, acc, masked):
        if mode == "dsa":
            s = s + rest[0][j]
        if masked:
            s = jnp.where(key0 + j * tk <= qry, s, NEG)
        m_new = jnp.maximum(m, jnp.max(s, axis=0, keepdims=True))
        alpha = jnp.exp2(m - m_new)
        p = jnp.exp2(s - m_new)
        l = alpha * l + jnp.sum(p, axis=0, keepdims=True)
        vt = vt_ref[j, h * HEAD_DIM:(h + 1) * HEAD_DIM, :]
        acc = alpha * acc + jnp.dot(vt, p.astype(BF16), preferred_element_type=F32)
        return m_new, l, acc

    last = qi
    for h0 in range(0, n_heads, heads_per_loop):
        heads = range(h0, h0 + heads_per_loop)

        def step(j, carries, heads=heads):
            nxt = jnp.minimum(j + 1, last)
            out = []
            for h, (s, m, l, acc) in zip(heads, carries):
                s_next = scores(h, nxt)
                m, l, acc = update(h, j, s, m, l, acc, False)
                out.append((s_next, m, l, acc))
            return tuple(out)

        carries = tuple(
            (scores(h, 0), jnp.full((1, tq), NEG, F32), jnp.zeros((1, tq), F32),
             jnp.zeros((HEAD_DIM, tq), F32)) for h in heads)
        n_loop = last if mode == "fox" else last + 1
        carries = lax.fori_loop(0, n_loop, step, carries)
        for h, (s, m, l, acc) in zip(heads, carries):
            if mode == "fox":
                m, l, acc = update(h, last, s, m, l, acc, True)
            o_ref[:, h * HEAD_DIM:(h + 1) * HEAD_DIM] = jnp.transpose(acc / l).astype(o_ref.dtype)


def _attention(q_arr, q_blk, k_arr, k_blk, vt_arr, v_blk, extras, mode, n_heads, kdim, tq, tk):
    b, s, _ = q_arr.shape
    w = n_heads * HEAD_DIM
    wk = n_heads * kdim
    in_specs = [pl.BlockSpec((None, tq, wk), lambda bi, qi: (bi, qi, q_blk)),
                pl.BlockSpec((None, s, wk), lambda bi, qi: (bi, 0, k_blk)),
                pl.BlockSpec((None, s // tk, w, tk), lambda bi, qi: (bi, 0, v_blk, 0))]
    if mode == "dsa":
        in_specs += [pl.BlockSpec((None, None, s // tk, tk, tq), lambda bi, qi: (bi, qi, 0, 0, 0))]
    return pl.pallas_call(
        functools.partial(_attn_body, tq=tq, tk=tk, n_heads=n_heads, kdim=kdim, mode=mode,
                          heads_per_loop=ATT_HEADS_PER_LOOP),
        grid=(b, s // tq),
        in_specs=in_specs,
        out_specs=pl.BlockSpec((None, tq, w), lambda bi, qi: (bi, qi, 0)),
        out_shape=jax.ShapeDtypeStruct((b, s, w), BF16),
        compiler_params=_cparams("parallel", "parallel"),
        name="attn_" + mode,
    )(q_arr, k_arr, vt_arr, *extras)


def _select_body(iq_ref, ika_ref, ikb_ref, wt_ref, o_ref, keys_ref, *, tq, tk, n_chunks, topk):
    qi = pl.program_id(1)
    used = qi + 1
    key0 = lax.broadcasted_iota(jnp.int32, (tk, tq), 0)
    qry = qi * tq + lax.broadcasted_iota(jnp.int32, (tk, tq), 1)

    def score_chunk(c, _):
        off = pl.multiple_of(c * tk, tk)
        ka = ika_ref[pl.ds(off, tk), :]
        kb = ikb_ref[pl.ds(off, tk), :]
        acc = jnp.zeros((tk, tq), F32)
        for j in range(IDX_HEADS):
            pair = iq_ref[:, (j // 2) * LANES:(j // 2 + 1) * LANES]
            logit = lax.dot_general(ka if j % 2 == 0 else kb, pair, (((1,), (1,)), ((), ())),
                                    preferred_element_type=F32)
            acc = acc + wt_ref[j:j + 1, :] * jnp.maximum(logit, 0.0)
        bits = lax.bitcast_convert_type(acc, jnp.int32)
        key = jnp.where(bits < 0, bits ^ jnp.int32(0x7FFFFFFF), bits)
        keys_ref[c] = jnp.where(key0 + off <= qry, key, jnp.int32(INT_MIN))
        return 0

    lax.fori_loop(0, used, score_chunk, 0)

    def bit_step(i, prefix):
        cand_u = prefix | jnp.left_shift(jnp.int32(1), 31 - i)
        cand = cand_u ^ jnp.int32(INT_MIN)

        def count_chunk(c, cnt):
            ge = jnp.where(keys_ref[c] >= cand, 1.0, 0.0)
            return cnt + jnp.sum(ge.reshape(tk // 8, 8, tq), axis=0)

        cnt = lax.fori_loop(0, used, count_chunk, jnp.zeros((8, tq), F32))
        cnt = jnp.sum(cnt, axis=0, keepdims=True)
        return jnp.where(cnt >= topk, cand_u, prefix)

    prefix = lax.fori_loop(0, 32, bit_step, jnp.zeros((1, tq), jnp.int32))
    thr = jnp.maximum(prefix ^ jnp.int32(INT_MIN), jnp.int32(INT_MIN + 1))

    def write_chunk(c, _):
        o_ref[c] = jnp.where(keys_ref[c] >= thr, 0.0, NEG)
        return 0

    def fill_chunk(c, _):
        o_ref[c] = jnp.full((tk, tq), NEG, F32)
        return 0

    lax.fori_loop(0, used, write_chunk, 0)
    lax.fori_loop(used, n_chunks, fill_chunk, 0)


def _select_bias(iq, ika, ikb, wt, topk, tq, tk):
    b, s, _ = iq.shape
    n_chunks = s // tk
    return pl.pallas_call(
        functools.partial(_select_body, tq=tq, tk=tk, n_chunks=n_chunks, topk=topk),
        grid=(b, s // tq),
        in_specs=[pl.BlockSpec((None, tq, IDX_HEADS * IDX_DIM), lambda bi, qi: (bi, qi, 0)),
                  pl.BlockSpec((None, s, LANES), lambda bi, qi: (bi, 0, 0)),
                  pl.BlockSpec((None, s, LANES), lambda bi, qi: (bi, 0, 0)),
                  pl.BlockSpec((None, IDX_HEADS, tq), lambda bi, qi: (bi, 0, qi))],
        out_specs=pl.BlockSpec((None, None, n_chunks, tk, tq), lambda bi, qi: (bi, qi, 0, 0, 0)),
        out_shape=jax.ShapeDtypeStruct((b, s // tq, n_chunks, tk, tq), F32),
        scratch_shapes=[pltpu.VMEM((n_chunks, tk, tq), jnp.int32)],
        compiler_params=_cparams("parallel", "parallel"),
        name="index_select",
    )(iq, ika, ikb, wt)


def _cross_body(q_ref, k_ref, v_ref, o_ref, *, n_heads):
    for h in range(n_heads):
        hs = slice(h * HEAD_DIM, (h + 1) * HEAD_DIM)
        s = lax.dot_general(q_ref[:, hs], k_ref[:, hs], (((1,), (1,)), ((), ())),
                            preferred_element_type=F32)
        p = jnp.exp2(s - jnp.max(s, axis=-1, keepdims=True))
        l = jnp.sum(p, axis=-1, keepdims=True)
        o = jnp.dot(p.astype(BF16), v_ref[:, hs], preferred_element_type=F32)
        o_ref[:, hs] = (o / l).astype(o_ref.dtype)


def _cross_attention(q_arr, q_blk, mk, mv, tq):
    b, s, _ = q_arr.shape
    n_mem = mk.shape[1]
    w = CROSS_HEADS * HEAD_DIM
    return pl.pallas_call(
        functools.partial(_cross_body, n_heads=CROSS_HEADS),
        grid=(b, s // tq),
        in_specs=[pl.BlockSpec((None, tq, w), lambda bi, qi: (bi, qi, q_blk)),
                  pl.BlockSpec((None, n_mem, w), lambda bi, qi: (bi, 0, 0)),
                  pl.BlockSpec((None, n_mem, w), lambda bi, qi: (bi, 0, 0))],
        out_specs=pl.BlockSpec((None, tq, w), lambda bi, qi: (bi, qi, 0)),
        out_shape=jax.ShapeDtypeStruct((b, s, w), BF16),
        compiler_params=_cparams("parallel", "parallel"),
        name="cross_attn",
    )(q_arr, mk, mv)


def _sigmoid(x):
    return 1.0 / (1.0 + jnp.exp(-x))


def _merge_body(h_ref, of_ref, od_ref, oc_ref, x_ref, wgf_ref, wgd_ref, wgc_ref,
                wbf_ref, wbd_ref, wbc_ref, wo_ref, o_ref):
    j = pl.program_id(1)
    h = h_ref[...]
    tn = wo_ref.shape[0]
    sub = min(tn, MERGE_SUB)

    def branch(wg_ref, ob_ref, wb_ref, cs):
        gate = _sigmoid(lax.dot_general(h, wg_ref[cs, :], (((1,), (1,)), ((), ())),
                                        preferred_element_type=F32))
        return gate * jnp.dot(ob_ref[...], wb_ref[:, cs], preferred_element_type=F32)

    part = None
    for c in range(tn // sub):
        cs = slice(c * sub, (c + 1) * sub)
        merged = branch(wgf_ref, of_ref, wbf_ref, cs) + branch(wgd_ref, od_ref, wbd_ref, cs)
        merged = merged + branch(wgc_ref, oc_ref, wbc_ref, cs)
        p = jnp.dot(merged.astype(BF16), wo_ref[cs, :], preferred_element_type=F32)
        part = p if part is None else part + p

    @pl.when(j == 0)
    def _():
        o_ref[...] = x_ref[...] + part

    @pl.when(j != 0)
    def _():
        o_ref[...] += part


def _merge(h, o_fox, o_dsa, o_cross, x, w_gates, gate_row0, wbf, wbd, wbc, w_out, tm, tn):
    m, d = x.shape
    nj = d // tn
    assert gate_row0 % tn == 0
    g0 = gate_row0 // tn
    row = lambda i, j: (i, 0)
    in_specs = [pl.BlockSpec((tm, d), row),
                pl.BlockSpec((tm, o_fox.shape[1]), row),
                pl.BlockSpec((tm, o_dsa.shape[1]), row),
                pl.BlockSpec((tm, o_cross.shape[1]), row),
                pl.BlockSpec((tm, d), row),
                pl.BlockSpec((tn, d), lambda i, j: (g0 + j, 0)),
                pl.BlockSpec((tn, d), lambda i, j: (g0 + nj + j, 0)),
                pl.BlockSpec((tn, d), lambda i, j: (g0 + 2 * nj + j, 0)),
                pl.BlockSpec((wbf.shape[0], tn), lambda i, j: (0, j)),
                pl.BlockSpec((wbd.shape[0], tn), lambda i, j: (0, j)),
                pl.BlockSpec((wbc.shape[0], tn), lambda i, j: (0, j)),
                pl.BlockSpec((tn, d), lambda i, j: (j, 0))]
    return pl.pallas_call(
        _merge_body,
        grid=(m // tm, nj),
        in_specs=in_specs,
        out_specs=pl.BlockSpec((tm, d), row),
        out_shape=jax.ShapeDtypeStruct((m, d), F32),
        compiler_params=_cparams("parallel", "arbitrary"),
        name="gated_merge",
    )(h, o_fox, o_dsa, o_cross, x, w_gates, w_gates, w_gates, wbf, wbd, wbc, w_out)


def _router_body(x_ref, g_ref, w_ref, b_ref, h_ref, eid_ref, gate_ref):
    x = x_ref[...]
    r = lax.rsqrt(jnp.mean(x * x, axis=-1, keepdims=True) + EPS)
    h = (x * r) * g_ref[...]
    h_ref[...] = h
    logits = jnp.dot(h.astype(BF16), w_ref[...], preferred_element_type=F32) + b_ref[...]
    lane = lax.broadcasted_iota(jnp.int32, logits.shape, 1).astype(F32)

    def masked_max(mask):
        return jnp.max(jnp.where(mask, logits, -jnp.inf), axis=-1, keepdims=True)

    def first_lane(mask):
        return jnp.min(jnp.where(mask, lane, float(LANES)), axis=-1, keepdims=True)

    is_group = lane < N_GROUPS
    gmax = masked_max(is_group)
    grp = first_lane(is_group & (logits == gmax))
    denom = jnp.sum(jnp.where(is_group, jnp.exp(logits - gmax), 0.0), axis=-1, keepdims=True)
    p_sel = 1.0 / denom
    lo = N_GROUPS + grp * EXPERTS_PER_GROUP
    in_grp = (lane >= lo) & (lane < lo + EXPERTS_PER_GROUP)
    e1 = masked_max(in_grp)
    j1 = first_lane(in_grp & (logits == e1))
    rest = in_grp & (lane != j1)
    e2 = masked_max(rest)
    j2 = first_lane(rest & (logits == e2))
    t = jnp.exp(e2 - e1)
    g1 = p_sel * (1.0 / (1.0 + t))
    g2 = p_sel * (t / (1.0 + t))
    eid = jnp.where(lane == 0.0, j1 - N_GROUPS, jnp.where(lane == 1.0, j2 - N_GROUPS, 0.0))
    eid_ref[...] = eid.astype(jnp.int32)
    gate_ref[...] = jnp.where(lane == 0.0, g1, jnp.where(lane == 1.0, g2, 0.0))


def _router(x, g, w_router, b_router, tm):
    m, d = x.shape
    return pl.pallas_call(
        _router_body,
        grid=(m // tm,),
        in_specs=[pl.BlockSpec((tm, d), lambda i: (i, 0)),
                  pl.BlockSpec((1, d), lambda i: (0, 0)),
                  pl.BlockSpec((d, LANES), lambda i: (0, 0)),
                  pl.BlockSpec((1, LANES), lambda i: (0, 0))],
        out_specs=[pl.BlockSpec((tm, d), lambda i: (i, 0)),
                   pl.BlockSpec((tm, LANES), lambda i: (i, 0)),
                   pl.BlockSpec((tm, LANES), lambda i: (i, 0))],
        out_shape=[jax.ShapeDtypeStruct((m, d), F32),
                   jax.ShapeDtypeStruct((m, LANES), jnp.int32),
                   jax.ShapeDtypeStruct((m, LANES), F32)],
        compiler_params=_cparams("parallel"),
        name="moe_router",
    )(x, g.reshape(1, d), w_router, b_router.reshape(1, LANES))


def _expert_body(ge_ref, gb0_ref, gnb_ref, used_ref, tok_ref, h_hbm, wg_ref, wu_ref, wd_ref, ys_hbm,
                 xs_ref, xb_ref, acc_ref, wgb_ref, wub_ref, wdb_ref, sem_in, sem_out,
                 *, bm, n_f, n_blocks):
    g = pl.program_id(0)
    f = pl.program_id(1)
    nb = gnb_ref[g]
    blk0 = gb0_ref[g]

    @pl.when(jnp.logical_and(g == pl.num_programs(0) - 1, f == n_f - 1))
    def _():
        xs_ref[0, pl.ds(0, bm), :] = jnp.zeros((bm, xs_ref.shape[2]), F32)

        def tail_copy(i):
            return pltpu.make_async_copy(xs_ref.at[0, pl.ds(0, bm)], ys_hbm.at[pl.ds(i * bm, bm)],
                                         sem_out)

        def start_tail(i, _):
            tail_copy(i).start()
            return 0

        lax.fori_loop(used_ref[0], n_blocks, start_tail, 0)

        def wait_tail(i, _):
            tail_copy(i).wait()
            return 0

        lax.fori_loop(used_ref[0], n_blocks, wait_tail, 0)

    def gather(grp):
        slot = grp % 2
        base = gb0_ref[grp] * bm

        def issue(r, _):
            pltpu.make_async_copy(h_hbm.at[pl.ds(tok_ref[base + r], 1)],
                                  xs_ref.at[slot, pl.ds(r, 1)], sem_in.at[slot]).start()
            return 0

        lax.fori_loop(0, gnb_ref[grp] * bm, issue, 0)

    def block_copy_out(i):
        return pltpu.make_async_copy(acc_ref.at[pl.ds(i * bm, bm)],
                                     ys_hbm.at[pl.ds((blk0 + i) * bm, bm)], sem_out)

    @pl.when(jnp.logical_and(g == 0, f == 0))
    def _():
        gather(g)

    @pl.when(jnp.logical_and(nb > 0, f == 0))
    def _():
        slot = g % 2

        def land(i, _):
            rows = pl.ds(pl.multiple_of(i * bm, bm), bm)
            pltpu.make_async_copy(h_hbm.at[pl.ds(0, bm)], xs_ref.at[slot, rows], sem_in.at[slot]).wait()
            return 0

        lax.fori_loop(0, nb, land, 0)

        def cast(i, _):
            rows = pl.ds(pl.multiple_of(i * bm, bm), bm)
            xb_ref[rows, :] = xs_ref[slot, rows, :].astype(BF16)
            return 0

        lax.fori_loop(0, nb, cast, 0)

    @pl.when(jnp.logical_and(f == 0, g + 1 < pl.num_programs(0)))
    def _():
        gather(g + 1)

    @pl.when(nb > 0)
    def _():
        wgb_ref[...] = wg_ref[...].astype(BF16)
        wub_ref[...] = wu_ref[...].astype(BF16)
        wdb_ref[...] = wd_ref[...].astype(BF16)

        def block(i, _):
            rows = pl.ds(pl.multiple_of(i * bm, bm), bm)
            xb = xb_ref[rows, :]
            a = jnp.dot(xb, wgb_ref[...], preferred_element_type=F32)
            u = jnp.dot(xb, wub_ref[...], preferred_element_type=F32)
            hid = (a * _sigmoid(a)) * u
            y = jnp.dot(hid.astype(BF16), wdb_ref[...], preferred_element_type=F32)

            @pl.when(f == 0)
            def _():
                acc_ref[rows, :] = y

            @pl.when(f != 0)
            def _():
                acc_ref[rows, :] += y

            return 0

        lax.fori_loop(0, nb, block, 0)

    @pl.when(jnp.logical_and(nb > 0, f == n_f - 1))
    def _():
        def start_out(i, _):
            block_copy_out(i).start()
            return 0

        lax.fori_loop(0, nb, start_out, 0)

        def wait_out(i, _):
            block_copy_out(i).wait()
            return 0

        lax.fori_loop(0, nb, wait_out, 0)


def _experts(h2, slot_tok, grp_e, grp_blk0, grp_nb, n_used, w_gate, w_up, w_down, n_slots):
    d = h2.shape[1]
    ff = w_gate.shape[2]
    n_f = ff // MOE_FC
    n_groups = grp_e.shape[0]
    rmax = MOE_GROUP_BLOCKS * MOE_BM

    def f_idx(g, f, gnb):
        return jnp.where(gnb[g] > 0, f, n_f - 1)

    grid_spec = pltpu.PrefetchScalarGridSpec(
        num_scalar_prefetch=5,
        grid=(n_groups, n_f),
        in_specs=[
            pl.BlockSpec(memory_space=pl.ANY),
            pl.BlockSpec((None, d, MOE_FC), lambda g, f, ge, gb0, gnb, *_: (ge[g], 0, f_idx(g, f, gnb))),
            pl.BlockSpec((None, d, MOE_FC), lambda g, f, ge, gb0, gnb, *_: (ge[g], 0, f_idx(g, f, gnb))),
            pl.BlockSpec((None, MOE_FC, d), lambda g, f, ge, gb0, gnb, *_: (ge[g], f_idx(g, f, gnb), 0)),
        ],
        out_specs=pl.BlockSpec(memory_space=pl.ANY),
        scratch_shapes=[pltpu.VMEM((2, rmax, d), F32),
                        pltpu.VMEM((rmax, d), BF16),
                        pltpu.VMEM((rmax, d), F32),
                        pltpu.VMEM((d, MOE_FC), BF16),
                        pltpu.VMEM((d, MOE_FC), BF16),
                        pltpu.VMEM((MOE_FC, d), BF16),
                        pltpu.SemaphoreType.DMA((2,)),
                        pltpu.SemaphoreType.DMA(())],
    )
    return pl.pallas_call(
        functools.partial(_expert_body, bm=MOE_BM, n_f=n_f, n_blocks=n_slots // MOE_BM),
        grid_spec=grid_spec,
        out_shape=jax.ShapeDtypeStruct((n_slots, d), F32),
        compiler_params=_cparams("arbitrary", "arbitrary"),
        name="moe_experts",
    )(grp_e, grp_blk0, grp_nb, n_used, slot_tok, h2, w_gate, w_up, w_down)


def _combine_body(pos_ref, x_ref, gate_ref, ys_hbm, o_ref, buf_ref, sem, *, tm):
    i = pl.program_id(0)
    n_steps = pl.num_programs(0)

    def issue(step):
        slot = step % 2

        def body(r, _):
            for k in range(MOE_TOP_K):
                src = pos_ref[(step * tm + r) * MOE_TOP_K + k]
                pltpu.make_async_copy(ys_hbm.at[pl.ds(src, 1)], buf_ref.at[slot, k, pl.ds(r, 1)],
                                      sem.at[slot]).start()
            return 0

        lax.fori_loop(0, tm, body, 0)

    @pl.when(i == 0)
    def _():
        issue(i)

    @pl.when(i + 1 < n_steps)
    def _():
        issue(i + 1)

    slot = i % 2
    for k in range(MOE_TOP_K):
        pltpu.make_async_copy(ys_hbm.at[pl.ds(0, tm)], buf_ref.at[slot, k], sem.at[slot]).wait()
    o_ref[...] = x_ref[...] + (buf_ref[slot, 0] * gate_ref[:, 0:1] + buf_ref[slot, 1] * gate_ref[:, 1:2])


def _combine(x, gate, ys, pos, tm):
    m, d = x.shape
    grid_spec = pltpu.PrefetchScalarGridSpec(
        num_scalar_prefetch=1,
        grid=(m // tm,),
        in_specs=[pl.BlockSpec((tm, d), lambda i, pos: (i, 0)),
                  pl.BlockSpec((tm, LANES), lambda i, pos: (i, 0)),
                  pl.BlockSpec(memory_space=pl.ANY)],
        out_specs=pl.BlockSpec((tm, d), lambda i, pos: (i, 0)),
        scratch_shapes=[pltpu.VMEM((2, MOE_TOP_K, tm, d), F32), pltpu.SemaphoreType.DMA((2,))],
    )
    return pl.pallas_call(
        functools.partial(_combine_body, tm=tm),
        grid_spec=grid_spec,
        out_shape=jax.ShapeDtypeStruct((m, d), F32),
        compiler_params=_cparams("arbitrary"),
        name="moe_combine",
    )(pos, x, gate, ys)


def _moe_plan(eid, n_tokens):
    a = n_tokens * MOE_TOP_K
    eid = eid.reshape(a)
    n_blocks = (a + N_EXPERTS * (MOE_BM - 1)) // MOE_BM
    n_slots = n_blocks * MOE_BM
    n_groups = n_blocks // MOE_GROUP_BLOCKS + N_EXPERTS
    onehot = (eid[:, None] == jnp.arange(N_EXPERTS, dtype=jnp.int32)[None, :]).astype(jnp.int32)
    rank = jnp.take_along_axis(jnp.cumsum(onehot, axis=0), eid[:, None], axis=1)[:, 0] - 1
    counts = jnp.sum(onehot, axis=0)
    blocks_e = (counts + MOE_BM - 1) // MOE_BM
    blk_start = jnp.cumsum(blocks_e) - blocks_e
    pos = (blk_start[eid] * MOE_BM + rank).astype(jnp.int32)
    tok = jnp.arange(a, dtype=jnp.int32) // MOE_TOP_K
    slot_tok = jnp.zeros((n_slots,), jnp.int32).at[pos].set(tok)
    groups_e = (blocks_e + MOE_GROUP_BLOCKS - 1) // MOE_GROUP_BLOCKS
    grp_end = jnp.cumsum(groups_e)
    grp_start = grp_end - groups_e
    gidx = jnp.arange(n_groups, dtype=jnp.int32)
    ge = jnp.minimum(jnp.searchsorted(grp_end, gidx, side="right"), N_EXPERTS - 1).astype(jnp.int32)
    local = gidx - grp_start[ge]
    active = gidx < grp_end[-1]
    gb0 = jnp.where(active, blk_start[ge] + local * MOE_GROUP_BLOCKS, 0).astype(jnp.int32)
    gnb = jnp.where(active, jnp.minimum(blocks_e[ge] - local * MOE_GROUP_BLOCKS, MOE_GROUP_BLOCKS),
                    0).astype(jnp.int32)
    last_e = ge[jnp.maximum(grp_end[-1] - 1, 0)]
    ge = jnp.where(active, ge, last_e).astype(jnp.int32)
    n_used = jnp.sum(blocks_e).astype(jnp.int32).reshape(1)
    return pos, slot_tok, ge, gb0, gnb, n_used, n_slots


def _layer(x, mem, positions, attn_norm_g, mem_norm_g, w_in, fox_forget_b, fox_q_norm_g,
           fox_k_norm_g, dsa_q_norm_g, dsa_k_norm_g, idx_k_norm_g, cross_q_norm_g, cross_k_norm_g,
           w_mem_kv, w_branch_fox, w_branch_dsa, w_branch_cross, w_out, ffn_norm_g,
           router_group_w, router_group_b, router_expert_w, router_expert_b,
           expert_w_gate, expert_w_up, expert_w_down):
    b, s, d = x.shape
    n = b * s
    n_mem = mem.shape[1]
    fw, dw, cw = FOX_HEADS * HEAD_DIM, DSA_HEADS * HEAD_DIM, CROSS_HEADS * HEAD_DIM
    iqw = IDX_HEADS * IDX_DIM
    tm = min(512, n)
    x2d = x.reshape(n, d)

    sizes = (fw, fw, fw, FOX_HEADS, dw, dw, dw, iqw, IDX_HEADS, IDX_DIM, cw, 3 * d)
    offs = [0]
    for sz in sizes:
        offs.append(offs[-1] + sz)
    (c_fq, c_fk, c_fv, c_fl, c_dq, c_dk, c_dv, c_iq, c_iw, c_ik, c_cq, c_g) = range(12)

    w_t = jnp.transpose(w_in, (2, 0, 1)).reshape(offs[-1], d)
    order = (c_fq, c_fk, c_cq, c_dq, c_dk, c_fv, c_dv, c_iq, c_g)
    row0, r = {}, 0
    for c in order:
        row0[c] = r
        r += sizes[c]
    w_all = jnp.concatenate([w_t[offs[c]:offs[c + 1]] for c in order], axis=0).astype(BF16)
    small_rows = [w_t[offs[c]:offs[c + 1]] for c in (c_ik, c_iw, c_fl)]
    small_rows.append(jnp.zeros((LANES - IDX_DIM - IDX_HEADS - FOX_HEADS, d), F32))
    w_small = jnp.concatenate(small_rows, axis=0).astype(BF16)
    q_unit = (HEAD_DIM ** -0.5) * LOG2E
    g_qk1 = jnp.concatenate([jnp.tile(fox_q_norm_g * q_unit, FOX_HEADS),
                             jnp.tile(fox_k_norm_g, FOX_HEADS),
                             jnp.tile(cross_q_norm_g * q_unit, CROSS_HEADS)])
    g_qk2 = jnp.concatenate([jnp.tile(dsa_q_norm_g * q_unit, DSA_HEADS),
                             jnp.tile(dsa_k_norm_g, DSA_HEADS)])
    g_ik = jnp.concatenate([idx_k_norm_g, jnp.zeros((LANES - IDX_DIM,), F32)])
    fb_lane = jnp.zeros((LANES,), F32).at[SMALL_F:SMALL_F + FOX_HEADS].set(fox_forget_b)

    pos_col = positions.reshape(n, 1).astype(F32)
    cd, sd = _rope_tables(pos_col, *_rope_lane_patterns(ROT_DIM, HEAD_DIM), tm=tm)
    ci, si = _rope_tables(pos_col, *_rope_lane_patterns(IDX_ROT_DIM, IDX_DIM), tm=tm)

    h = _rmsnorm(x2d, attn_norm_g, BF16, tm)
    qk1 = _proj(h, w_all, "norm", BF16, tm, 1024, gains=g_qk1, w_rows=(row0[c_fq], 2 * fw + cw))
    qk2 = _proj(h, w_all, "norm_rope", BF16, tm, 512, gains=g_qk2, tables=(cd, sd),
                w_rows=(row0[c_dq], 2 * dw))
    vv = _proj(h, w_all, "raw", BF16, tm, 512, w_rows=(row0[c_fv], fw + dw))
    iq = _proj(h, w_all, "rope_idx", BF16, tm, 1024, tables=(ci, si), w_rows=(row0[c_iq], iqw))
    small = _proj(h, w_small, "raw", F32, tm, LANES, w_rows=(0, LANES))
    ika, ikb, wl, lf_col = _small(small, g_ik, fb_lane, ci, si, tm)

    tq, tk = min(ATT_TQ, s), min(ATT_TK, s)
    assert tq == tk and s % tq == 0
    cum_col = _forget_cumsum(lf_col.reshape(b, s, LANES))
    qk1 = qk1.reshape(b, s, 2 * fw + cw)
    qk2 = qk2.reshape(b, s, 2 * dw)
    q_aug, k_aug = _fox_pack(qk1, cum_col, FOX_HEADS, min(512, s))
    vt = jnp.transpose(vv.reshape(b, s // tk, tk, fw + dw), (0, 1, 3, 2))
    o_fox = _attention(q_aug, 0, k_aug, 0, vt, 0, (), "fox", FOX_HEADS, AUG_DIM, tq, tk)

    topk = min(INDEX_TOPK, s // 4)
    wt = jnp.transpose(wl.reshape(b, s, LANES)[:, :, :IDX_HEADS], (0, 2, 1))
    bias = _select_bias(iq.reshape(b, s, iqw), ika.reshape(b, s, LANES), ikb.reshape(b, s, LANES),
                        wt, topk, tq, tk)
    o_dsa = _attention(qk2, 0, qk2, 1, vt, 1, (bias,), "dsa", DSA_HEADS, HEAD_DIM, tq, tk)

    tmm = min(512, b * n_mem)
    m_n = _rmsnorm(mem.reshape(b * n_mem, d), mem_norm_g, BF16, tmm)
    w_kv = w_mem_kv.astype(BF16)
    mk = _proj(m_n, w_kv[:, :cw], "norm", BF16, tmm, cw, gains=jnp.tile(cross_k_norm_g, CROSS_HEADS))
    mv = _proj(m_n, w_kv[:, cw:], "raw", BF16, tmm, cw)
    o_cross = _cross_attention(qk1, (2 * fw) // cw, mk.reshape(b, n_mem, cw), mv.reshape(b, n_mem, cw),
                               min(512, s))

    x_mid = _merge(h, o_fox.reshape(n, fw), o_dsa.reshape(n, dw), o_cross.reshape(n, cw), x2d,
                   w_all, row0[c_g], w_branch_fox.astype(BF16), w_branch_dsa.astype(BF16),
                   w_branch_cross.astype(BF16), w_out.astype(BF16), tm, MERGE_TN)

    w_router = jnp.concatenate(
        [router_group_w, router_expert_w, jnp.zeros((d, LANES - N_GROUPS - N_EXPERTS), F32)],
        axis=1).astype(BF16)
    b_router = jnp.concatenate(
        [router_group_b, router_expert_b, jnp.zeros((LANES - N_GROUPS - N_EXPERTS,), F32)])
    h2, eid, gate = _router(x_mid, ffn_norm_g, w_router, b_router, tm)
    pos, slot_tok, ge, gb0, gnb, n_used, n_slots = _moe_plan(eid[:, :MOE_TOP_K], n)
    ys = _experts(h2, slot_tok, ge, gb0, gnb, n_used, expert_w_gate, expert_w_up, expert_w_down,
                  n_slots)
    out = _combine(x_mid, gate, ys, pos, min(256, n))
    return out.reshape(b, s, d)


def kernel(x, mem, positions, attn_norm_g, mem_norm_g, w_in, fox_forget_b, fox_q_norm_g, fox_k_norm_g, dsa_q_norm_g, dsa_k_norm_g, idx_k_norm_g, cross_q_norm_g, cross_k_norm_g, w_mem_kv, w_branch_fox, w_branch_dsa, w_branch_cross, w_out, ffn_norm_g, router_group_w, router_group_b, router_expert_w, router_expert_b, expert_w_gate, expert_w_up, expert_w_down):
    depth = w_in.shape[0]
    for layer in range(depth):
        x = _layer(
            x, mem, positions, attn_norm_g[layer], mem_norm_g[layer], w_in[layer:layer + 1],
            fox_forget_b[layer], fox_q_norm_g[layer], fox_k_norm_g[layer],
            dsa_q_norm_g[layer], dsa_k_norm_g[layer], idx_k_norm_g[layer],
            cross_q_norm_g[layer], cross_k_norm_g[layer], w_mem_kv[layer],
            w_branch_fox[layer], w_branch_dsa[layer], w_branch_cross[layer], w_out[layer],
            ffn_norm_g[layer], router_group_w[layer], router_group_b[layer],
            router_expert_w[layer], router_expert_b[layer], expert_w_gate[layer],
            expert_w_up[layer], expert_w_down[layer])
    return x
```

```python
import functools

import jax
import jax.numpy as jnp
from jax import lax
from jax.experimental import pallas as pl
from jax.experimental.pallas import tpu as pltpu

F32 = jnp.float32
BF16 = jnp.bfloat16

LANES = 128
HEAD_DIM = 128
FOX_HEADS = 6
DSA_HEADS = 6
CROSS_HEADS = 4
IDX_HEADS = 16
IDX_DIM = 64
ROPE_THETA = 500000.0
ROT_DIM = HEAD_DIM // 4
IDX_ROT_DIM = IDX_DIM // 4
INDEX_TOPK = 256
N_GROUPS = 4
EXPERTS_PER_GROUP = 8
N_EXPERTS = N_GROUPS * EXPERTS_PER_GROUP
MOE_TOP_K = 2
EPS = 1e-6

NEG = -1e30
INT_MIN = -(2 ** 31)
VMEM_LIMIT = 56 * 1024 * 1024

ATT_TQ = 256
ATT_TK = 256
ATT_HEADS_PER_LOOP = 6
MERGE_TN = 512
MERGE_SUB = 256
MOE_BM = 256
MOE_GROUP_BLOCKS = 4
MOE_FC = 512


def _cparams(*sem):
    return pltpu.CompilerParams(dimension_semantics=sem, vmem_limit_bytes=VMEM_LIMIT)


def _rmsnorm_body(x_ref, g_ref, o_ref):
    x = x_ref[...]
    r = lax.rsqrt(jnp.mean(x * x, axis=-1, keepdims=True) + EPS)
    o_ref[...] = ((x * r) * g_ref[...]).astype(o_ref.dtype)


def _rmsnorm(x2d, g, out_dtype, tm):
    m, d = x2d.shape
    return pl.pallas_call(
        _rmsnorm_body,
        grid=(m // tm,),
        in_specs=[pl.BlockSpec((tm, d), lambda i: (i, 0)),
                  pl.BlockSpec((1, d), lambda i: (0, 0))],
        out_specs=pl.BlockSpec((tm, d), lambda i: (i, 0)),
        out_shape=jax.ShapeDtypeStruct((m, d), out_dtype),
        compiler_params=_cparams("parallel"),
        name="rmsnorm",
    )(x2d, g.reshape(1, d))


W_PREP_ROWS = 128


def _w_prep_body(src_ref, w_ref, o_ref):
    del src_ref
    o_ref[...] = w_ref[:, 0, :].astype(o_ref.dtype)


def _w_prep(w_t, src_rows):
    _, _, k = w_t.shape
    n_chunks = src_rows.shape[0]
    grid_spec = pltpu.PrefetchScalarGridSpec(
        num_scalar_prefetch=1,
        grid=(n_chunks,),
        in_specs=[pl.BlockSpec((pl.Element(W_PREP_ROWS), pl.Element(1), pl.Element(k)),
                               lambda c, src: (src[c], 0, 0))],
        out_specs=pl.BlockSpec((W_PREP_ROWS, k), lambda c, src: (c, 0)),
    )
    return pl.pallas_call(
        _w_prep_body,
        grid_spec=grid_spec,
        out_shape=jax.ShapeDtypeStruct((n_chunks * W_PREP_ROWS, k), BF16),
        compiler_params=_cparams("parallel"),
        name="w_prep",
    )(src_rows, w_t)


def _rope_table_body(pos_ref, f_ref, sgn_ref, c_ref, s_ref):
    ang = pos_ref[...] * f_ref[...]
    c_ref[...] = jnp.cos(ang)
    s_ref[...] = jnp.sin(ang) * sgn_ref[...]


def _rope_tables(pos_col, freq_lane, sign_lane, tm):
    n = pos_col.shape[0]
    row = pl.BlockSpec((1, LANES), lambda i: (0, 0))
    tab = pl.BlockSpec((tm, LANES), lambda i: (i, 0))
    return pl.pallas_call(
        _rope_table_body,
        grid=(n // tm,),
        in_specs=[pl.BlockSpec((tm, 1), lambda i: (i, 0)), row, row],
        out_specs=[tab, tab],
        out_shape=[jax.ShapeDtypeStruct((n, LANES), F32)] * 2,
        compiler_params=_cparams("parallel"),
        name="rope_tables",
    )(pos_col, freq_lane.reshape(1, LANES), sign_lane.reshape(1, LANES))


def _rope_lane_patterns(rot_dim, period):
    half = rot_dim // 2
    inv_freq = jnp.power(jnp.float32(ROPE_THETA), -jnp.arange(half, dtype=F32) * 2.0 / rot_dim)
    lane = jnp.arange(LANES) % period
    freq = jnp.where(lane < rot_dim, inv_freq[lane % half], 0.0).astype(F32)
    sign = jnp.where(lane < half, -1.0, jnp.where(lane < rot_dim, 1.0, 0.0)).astype(F32)
    return freq, sign


def _apply_rope(a, c, s, half, period):
    lane = lax.broadcasted_iota(jnp.int32, a.shape, 1)
    first = (lane & (period - 1)) < half
    partner = jnp.where(first, pltpu.roll(a, LANES - half, 1), pltpu.roll(a, half, 1))
    return a * c + partner * s


def _proj_body(*refs, mode, n_lane_groups, w_rows):
    h_ref, w_ref = refs[0], refs[1]
    o_ref = refs[-1]
    if w_rows:
        acc = lax.dot_general(h_ref[...], w_ref[...], (((1,), (1,)), ((), ())),
                              preferred_element_type=F32)
    else:
        acc = jnp.dot(h_ref[...], w_ref[...], preferred_element_type=F32)
    if mode == "raw":
        o_ref[...] = acc.astype(o_ref.dtype)
        return
    for g in range(n_lane_groups):
        sl = slice(g * LANES, (g + 1) * LANES)
        a = acc[:, sl]
        if mode in ("norm", "norm_rope"):
            g_ref = refs[2]
            a = (a * lax.rsqrt(jnp.mean(a * a, axis=-1, keepdims=True) + EPS)) * g_ref[:, sl]
        if mode == "norm_rope":
            a = _apply_rope(a, refs[3][...], refs[4][...], ROT_DIM // 2, HEAD_DIM)
        if mode == "rope_idx":
            a = _apply_rope(a, refs[2][...], refs[3][...], IDX_ROT_DIM // 2, IDX_DIM)
        o_ref[:, sl] = a.astype(o_ref.dtype)


def _proj(h, w, mode, out_dtype, tm, tn, gains=None, tables=None, w_rows=None):
    m, k = h.shape
    if w_rows is None:
        n = w.shape[1]
        w_spec = pl.BlockSpec((k, tn), lambda j, i: (0, j))
    else:
        row0, n = w_rows
        assert row0 % tn == 0 and n % tn == 0
        w_spec = pl.BlockSpec((tn, k), lambda j, i: (row0 // tn + j, 0))
    in_specs = [pl.BlockSpec((tm, k), lambda j, i: (i, 0)), w_spec]
    args = [h, w]
    if gains is not None:
        in_specs.append(pl.BlockSpec((1, tn), lambda j, i: (0, j)))
        args.append(gains.reshape(1, n))
    if tables is not None:
        for t in tables:
            in_specs.append(pl.BlockSpec((tm, LANES), lambda j, i: (i, 0)))
            args.append(t)
    return pl.pallas_call(
        functools.partial(_proj_body, mode=mode, n_lane_groups=tn // LANES,
                          w_rows=w_rows is not None),
        grid=(n // tn, m // tm),
        in_specs=in_specs,
        out_specs=pl.BlockSpec((tm, tn), lambda j, i: (i, j)),
        out_shape=jax.ShapeDtypeStruct((m, n), out_dtype),
        compiler_params=_cparams("parallel", "parallel"),
        name="proj_" + mode,
    )(*args)


SMALL_IK = 0
SMALL_IW = 64
SMALL_F = 80


def _small_body(x_ref, gk_ref, fb_ref, c_ref, s_ref, ika_ref, ikb_ref, wl_ref, lf_ref):
    x = x_ref[...]
    lane = lax.broadcasted_iota(jnp.int32, x.shape, 1)
    is_ik = lane < IDX_DIM
    ik = jnp.where(is_ik, x, 0.0)
    ms = jnp.sum(ik * ik, axis=-1, keepdims=True) * (1.0 / IDX_DIM)
    ik = (ik * lax.rsqrt(ms + EPS)) * gk_ref[...]
    ik = _apply_rope(ik, c_ref[...], s_ref[...], IDX_ROT_DIM // 2, IDX_DIM)
    ik = jnp.where(is_ik, ik, 0.0)
    ika_ref[...] = ik.astype(BF16)
    ikb_ref[...] = pltpu.roll(ik, IDX_DIM, 1).astype(BF16)
    wl_ref[...] = pltpu.roll(x, LANES - SMALL_IW, 1) * (2.0 ** -5)
    z = x + fb_ref[...]
    log_f = jnp.minimum(z, 0.0) - jnp.log1p(jnp.exp(-jnp.abs(z)))
    lf_ref[...] = pltpu.roll(log_f, LANES - SMALL_F, 1)


def _small(x, idx_k_gain_lane, forget_b_lane, ci, si, tm):
    n = x.shape[0]
    row = pl.BlockSpec((1, LANES), lambda i: (0, 0))
    tab = pl.BlockSpec((tm, LANES), lambda i: (i, 0))
    return pl.pallas_call(
        _small_body,
        grid=(n // tm,),
        in_specs=[tab, row, row, tab, tab],
        out_specs=[tab, tab, tab, tab],
        out_shape=[jax.ShapeDtypeStruct((n, LANES), BF16), jax.ShapeDtypeStruct((n, LANES), BF16),
                   jax.ShapeDtypeStruct((n, LANES), F32), jax.ShapeDtypeStruct((n, LANES), F32)],
        compiler_params=_cparams("parallel"),
        name="small_cols",
    )(x, idx_k_gain_lane.reshape(1, LANES), forget_b_lane.reshape(1, LANES), ci, si)


def _cumsum_body(col_ref, ccol_ref, *, seq):
    c = col_ref[...]
    ci = lax.broadcasted_iota(jnp.int32, c.shape, 0)
    sh = 1
    while sh < seq:
        c = c + jnp.where(ci >= sh, pltpu.roll(c, sh, 0), 0.0)
        sh *= 2
    ccol_ref[...] = c


def _forget_cumsum(lf_col):
    b, s, _ = lf_col.shape
    return pl.pallas_call(
        functools.partial(_cumsum_body, seq=s),
        grid=(b,),
        in_specs=[pl.BlockSpec((None, s, LANES), lambda i: (i, 0, 0))],
        out_specs=pl.BlockSpec((None, s, LANES), lambda i: (i, 0, 0)),
        out_shape=jax.ShapeDtypeStruct((b, s, LANES), F32),
        compiler_params=_cparams("parallel"),
        name="forget_cumsum",
    )(lf_col)


AUG_DIM = 2 * HEAD_DIM
LOG2E = 1.4426950408889634


def _split3(x):
    hi = x.astype(BF16).astype(F32)
    r1 = x - hi
    mid = r1.astype(BF16).astype(F32)
    lo = (r1 - mid).astype(BF16).astype(F32)
    return hi, mid, lo


def _fox_pack_body(q_ref, k_ref, cum_ref, qa_ref, ka_ref, *, n_heads):
    rows = q_ref.shape[0]
    lane = lax.broadcasted_iota(jnp.int32, (rows, LANES), 1)
    for h in range(n_heads):
        c = jnp.broadcast_to(cum_ref[:, h:h + 1], (rows, LANES)) * LOG2E
        qh, qm, ql = _split3(c)
        kh, km, kl = _split3(-c)
        q_aug = jnp.where(lane == 0, qh, jnp.where(lane == 1, qm, jnp.where(
            lane == 2, ql, jnp.where(lane < 6, 1.0, 0.0))))
        k_aug = jnp.where(lane < 3, 1.0, jnp.where(lane == 3, kh, jnp.where(
            lane == 4, km, jnp.where(lane == 5, kl, 0.0))))
        base = h * AUG_DIM
        qa_ref[:, base:base + HEAD_DIM] = q_ref[:, h * HEAD_DIM:(h + 1) * HEAD_DIM]
        ka_ref[:, base:base + HEAD_DIM] = k_ref[:, h * HEAD_DIM:(h + 1) * HEAD_DIM]
        qa_ref[:, base + HEAD_DIM:base + AUG_DIM] = q_aug.astype(BF16)
        ka_ref[:, base + HEAD_DIM:base + AUG_DIM] = k_aug.astype(BF16)


def _fox_pack(qk, cum_col, n_heads, ts):
    b, s, _ = qk.shape
    w = n_heads * HEAD_DIM
    wa = n_heads * AUG_DIM
    return pl.pallas_call(
        functools.partial(_fox_pack_body, n_heads=n_heads),
        grid=(b, s // ts),
        in_specs=[pl.BlockSpec((None, ts, w), lambda bi, i: (bi, i, 0)),
                  pl.BlockSpec((None, ts, w), lambda bi, i: (bi, i, 1)),
                  pl.BlockSpec((None, ts, LANES), lambda bi, i: (bi, i, 0))],
        out_specs=[pl.BlockSpec((None, ts, wa), lambda bi, i: (bi, i, 0)),
                   pl.BlockSpec((None, ts, wa), lambda bi, i: (bi, i, 0))],
        out_shape=[jax.ShapeDtypeStruct((b, s, wa), BF16)] * 2,
        compiler_params=_cparams("parallel", "parallel"),
        name="fox_pack",
    )(qk, qk, cum_col)


def _attn_body(q_ref, k_ref, vt_ref, *rest, tq, tk, n_heads, kdim, mode, heads_per_loop):
    o_ref = rest[-1]
    qi = pl.program_id(1)
    key0 = lax.broadcasted_iota(jnp.int32, (tk, tq), 0)
    qry = qi * tq + lax.broadcasted_iota(jnp.int32, (tk, tq), 1)
    def scores(h, j):
        off = pl.multiple_of(j * tk, tk)
        q = q_ref[:, h * kdim:(h + 1) * kdim]
        k = k_ref[pl.ds(off, tk), h * kdim:(h + 1) * kdim]
        return lax.dot_general(k, q, (((1,), (1,)), ((), ())), preferred_element_type=F32)

    def update(h, j, s, m, l, acc, masked):
        if mode == "dsa":
            s = s + rest[0][j]
        if masked:
            s = jnp.where(key0 + j * tk <= qry, s, NEG)
        m_new = jnp.maximum(m, jnp.max(s, axis=0, keepdims=True))
        alpha = jnp.exp2(m - m_new)
        p = jnp.exp2(s - m_new)
        l = alpha * l + jnp.sum(p, axis=0, keepdims=True)
        vt = vt_ref[j, h * HEAD_DIM:(h + 1) * HEAD_DIM, :]
        acc = alpha * acc + jnp.dot(vt, p.astype(BF16), preferred_element_type=F32)
        return m_new, l, acc

    last = qi
    for h0 in range(0, n_heads, heads_per_loop):
        heads = range(h0, h0 + heads_per_loop)

        def step(j, carries, heads=heads):
            nxt = jnp.minimum(j + 1, last)
            out = []
            for h, (s, m, l, acc) in zip(heads, carries):
                s_next = scores(h, nxt)
                m, l, acc = update(h, j, s, m, l, acc, False)
                out.append((s_next, m, l, acc))
            return tuple(out)

        carries = tuple(
            (scores(h, 0), jnp.full((1, tq), NEG, F32), jnp.zeros((1, tq), F32),
             jnp.zeros((HEAD_DIM, tq), F32)) for h in heads)
        n_loop = last if mode == "fox" else last + 1
        carries = lax.fori_loop(0, n_loop, step, carries)
        for h, (s, m, l, acc) in zip(heads, carries):
            if mode == "fox":
                m, l, acc = update(h, last, s, m, l, acc, True)
            o_ref[:, h * HEAD_DIM:(h + 1) * HEAD_DIM] = jnp.transpose(acc / l).astype(o_ref.dtype)


def _attention(q_arr, q_blk, k_arr, k_blk, vt_arr, v_blk, extras, mode, n_heads, kdim, tq, tk):
    b, s, _ = q_arr.shape
    w = n_heads * HEAD_DIM
    wk = n_heads * kdim
    in_specs = [pl.BlockSpec((None, tq, wk), lambda bi, qi: (bi, qi, q_blk)),
                pl.BlockSpec((None, s, wk), lambda bi, qi: (bi, 0, k_blk)),
                pl.BlockSpec((None, s // tk, w, tk), lambda bi, qi: (bi, 0, v_blk, 0))]
    if mode == "dsa":
        in_specs += [pl.BlockSpec((None, None, s // tk, tk, tq), lambda bi, qi: (bi, qi, 0, 0, 0))]
    return pl.pallas_call(
        functools.partial(_attn_body, tq=tq, tk=tk, n_heads=n_heads, kdim=kdim, mode=mode,
                          heads_per_loop=ATT_HEADS_PER_LOOP),
        grid=(b, s // tq),
        in_specs=in_specs,
        out_specs=pl.BlockSpec((None, tq, w), lambda bi, qi: (bi, qi, 0)),
        out_shape=jax.ShapeDtypeStruct((b, s, w), BF16),
        compiler_params=_cparams("parallel", "parallel"),
        name="attn_" + mode,
    )(q_arr, k_arr, vt_arr, *extras)


def _select_body(iq_ref, ika_ref, ikb_ref, wt_ref, o_ref, keys_ref, *, tq, tk, n_chunks, topk):
    qi = pl.program_id(1)
    used = qi + 1
    key0 = lax.broadcasted_iota(jnp.int32, (tk, tq), 0)
    qry = qi * tq + lax.broadcasted_iota(jnp.int32, (tk, tq), 1)

    def score_chunk(c, _):
        off = pl.multiple_of(c * tk, tk)
        ka = ika_ref[pl.ds(off, tk), :]
        kb = ikb_ref[pl.ds(off, tk), :]
        acc = jnp.zeros((tk, tq), F32)
        for j in range(IDX_HEADS):
            pair = iq_ref[:, (j // 2) * LANES:(j // 2 + 1) * LANES]
            logit = lax.dot_general(ka if j % 2 == 0 else kb, pair, (((1,), (1,)), ((), ())),
                                    preferred_element_type=F32)
            acc = acc + wt_ref[j:j + 1, :] * jnp.maximum(logit, 0.0)
        bits = lax.bitcast_convert_type(acc, jnp.int32)
        key = jnp.where(bits < 0, bits ^ jnp.int32(0x7FFFFFFF), bits)
        keys_ref[c] = jnp.where(key0 + off <= qry, key, jnp.int32(INT_MIN))
        return 0

    lax.fori_loop(0, used, score_chunk, 0)

    def bit_step(i, prefix):
        cand_u = prefix | jnp.left_shift(jnp.int32(1), 31 - i)
        cand = cand_u ^ jnp.int32(INT_MIN)

        def count_chunk(c, cnt):
            ge = jnp.where(keys_ref[c] >= cand, 1.0, 0.0)
            return cnt + jnp.sum(ge.reshape(tk // 8, 8, tq), axis=0)

        cnt = lax.fori_loop(0, used, count_chunk, jnp.zeros((8, tq), F32))
        cnt = jnp.sum(cnt, axis=0, keepdims=True)
        return jnp.where(cnt >= topk, cand_u, prefix)

    prefix = lax.fori_loop(0, 32, bit_step, jnp.zeros((1, tq), jnp.int32))
    thr = jnp.maximum(prefix ^ jnp.int32(INT_MIN), jnp.int32(INT_MIN + 1))

    def write_chunk(c, _):
        o_ref[c] = jnp.where(keys_ref[c] >= thr, 0.0, NEG)
        return 0

    def fill_chunk(c, _):
        o_ref[c] = jnp.full((tk, tq), NEG, F32)
        return 0

    lax.fori_loop(0, used, write_chunk, 0)
    lax.fori_loop(used, n_chunks, fill_chunk, 0)


def _select_bias(iq, ika, ikb, wt, topk, tq, tk):
    b, s, _ = iq.shape
    n_chunks = s // tk
    return pl.pallas_call(
        functools.partial(_select_body, tq=tq, tk=tk, n_chunks=n_chunks, topk=topk),
        grid=(b, s // tq),
        in_specs=[pl.BlockSpec((None, tq, IDX_HEADS * IDX_DIM), lambda bi, qi: (bi, qi, 0)),
                  pl.BlockSpec((None, s, LANES), lambda bi, qi: (bi, 0, 0)),
                  pl.BlockSpec((None, s, LANES), lambda bi, qi: (bi, 0, 0)),
                  pl.BlockSpec((None, IDX_HEADS, tq), lambda bi, qi: (bi, 0, qi))],
        out_specs=pl.BlockSpec((None, None, n_chunks, tk, tq), lambda bi, qi: (bi, qi, 0, 0, 0)),
        out_shape=jax.ShapeDtypeStruct((b, s // tq, n_chunks, tk, tq), F32),
        scratch_shapes=[pltpu.VMEM((n_chunks, tk, tq), jnp.int32)],
        compiler_params=_cparams("parallel", "parallel"),
        name="index_select",
    )(iq, ika, ikb, wt)


def _cross_body(q_ref, k_ref, v_ref, o_ref, *, n_heads):
    for h in range(n_heads):
        hs = slice(h * HEAD_DIM, (h + 1) * HEAD_DIM)
        s = lax.dot_general(q_ref[:, hs], k_ref[:, hs], (((1,), (1,)), ((), ())),
                            preferred_element_type=F32)
        p = jnp.exp2(s - jnp.max(s, axis=-1, keepdims=True))
        l = jnp.sum(p, axis=-1, keepdims=True)
        o = jnp.dot(p.astype(BF16), v_ref[:, hs], preferred_element_type=F32)
        o_ref[:, hs] = (o / l).astype(o_ref.dtype)


def _cross_attention(q_arr, q_blk, mk, mv, tq):
    b, s, _ = q_arr.shape
    n_mem = mk.shape[1]
    w = CROSS_HEADS * HEAD_DIM
    return pl.pallas_call(
        functools.partial(_cross_body, n_heads=CROSS_HEADS),
        grid=(b, s // tq),
        in_specs=[pl.BlockSpec((None, tq, w), lambda bi, qi: (bi, qi, q_blk)),
                  pl.BlockSpec((None, n_mem, w), lambda bi, qi: (bi, 0, 0)),
                  pl.BlockSpec((None, n_mem, w), lambda bi, qi: (bi, 0, 0))],
        out_specs=pl.BlockSpec((None, tq, w), lambda bi, qi: (bi, qi, 0)),
        out_shape=jax.ShapeDtypeStruct((b, s, w), BF16),
        compiler_params=_cparams("parallel", "parallel"),
        name="cross_attn",
    )(q_arr, mk, mv)


def _sigmoid(x):
    return 1.0 / (1.0 + jnp.exp(-x))


def _merge_body(h_ref, of_ref, od_ref, oc_ref, x_ref, wgf_ref, wgd_ref, wgc_ref,
                wbf_ref, wbd_ref, wbc_ref, wo_ref, o_ref):
    j = pl.program_id(1)
    h = h_ref[...]
    tn = wo_ref.shape[0]
    sub = min(tn, MERGE_SUB)

    def branch(wg_ref, ob_ref, wb_ref, cs):
        gate = _sigmoid(lax.dot_general(h, wg_ref[cs, :], (((1,), (1,)), ((), ())),
                                        preferred_element_type=F32))
        return gate * jnp.dot(ob_ref[...], wb_ref[:, cs], preferred_element_type=F32)

    part = None
    for c in range(tn // sub):
        cs = slice(c * sub, (c + 1) * sub)
        merged = branch(wgf_ref, of_ref, wbf_ref, cs) + branch(wgd_ref, od_ref, wbd_ref, cs)
        merged = merged + branch(wgc_ref, oc_ref, wbc_ref, cs)
        p = jnp.dot(merged.astype(BF16), wo_ref[cs, :], preferred_element_type=F32)
        part = p if part is None else part + p

    @pl.when(j == 0)
    def _():
        o_ref[...] = x_ref[...] + part

    @pl.when(j != 0)
    def _():
        o_ref[...] += part


def _merge(h, o_fox, o_dsa, o_cross, x, w_gates, gate_row0, wbf, wbd, wbc, w_out, tm, tn):
    m, d = x.shape
    nj = d // tn
    assert gate_row0 % tn == 0
    g0 = gate_row0 // tn
    row = lambda i, j: (i, 0)
    in_specs = [pl.BlockSpec((tm, d), row),
                pl.BlockSpec((tm, o_fox.shape[1]), row),
                pl.BlockSpec((tm, o_dsa.shape[1]), row),
                pl.BlockSpec((tm, o_cross.shape[1]), row),
                pl.BlockSpec((tm, d), row),
                pl.BlockSpec((tn, d), lambda i, j: (g0 + j, 0)),
                pl.BlockSpec((tn, d), lambda i, j: (g0 + nj + j, 0)),
                pl.BlockSpec((tn, d), lambda i, j: (g0 + 2 * nj + j, 0)),
                pl.BlockSpec((wbf.shape[0], tn), lambda i, j: (0, j)),
                pl.BlockSpec((wbd.shape[0], tn), lambda i, j: (0, j)),
                pl.BlockSpec((wbc.shape[0], tn), lambda i, j: (0, j)),
                pl.BlockSpec((tn, d), lambda i, j: (j, 0))]
    return pl.pallas_call(
        _merge_body,
        grid=(m // tm, nj),
        in_specs=in_specs,
        out_specs=pl.BlockSpec((tm, d), row),
        out_shape=jax.ShapeDtypeStruct((m, d), F32),
        compiler_params=_cparams("parallel", "arbitrary"),
        name="gated_merge",
    )(h, o_fox, o_dsa, o_cross, x, w_gates, w_gates, w_gates, wbf, wbd, wbc, w_out)


def _router_body(x_ref, g_ref, w_ref, b_ref, h_ref, eid_ref, gate_ref):
    x = x_ref[...]
    r = lax.rsqrt(jnp.mean(x * x, axis=-1, keepdims=True) + EPS)
    h = (x * r) * g_ref[...]
    hb = h.astype(BF16)
    bits = lax.bitcast_convert_type(hb.astype(F32), jnp.int32)
    half = bits.shape[1] // 2
    h_ref[...] = lax.shift_right_logical(bits[:, :half], 16) | (bits[:, half:] & jnp.int32(-65536))
    logits = jnp.dot(hb, w_ref[...], preferred_element_type=F32) + b_ref[...]
    lane = lax.broadcasted_iota(jnp.int32, logits.shape, 1).astype(F32)

    def masked_max(mask):
        return jnp.max(jnp.where(mask, logits, -jnp.inf), axis=-1, keepdims=True)

    def first_lane(mask):
        return jnp.min(jnp.where(mask, lane, float(LANES)), axis=-1, keepdims=True)

    is_group = lane < N_GROUPS
    gmax = masked_max(is_group)
    grp = first_lane(is_group & (logits == gmax))
    denom = jnp.sum(jnp.where(is_group, jnp.exp(logits - gmax), 0.0), axis=-1, keepdims=True)
    p_sel = 1.0 / denom
    lo = N_GROUPS + grp * EXPERTS_PER_GROUP
    in_grp = (lane >= lo) & (lane < lo + EXPERTS_PER_GROUP)
    e1 = masked_max(in_grp)
    j1 = first_lane(in_grp & (logits == e1))
    rest = in_grp & (lane != j1)
    e2 = masked_max(rest)
    j2 = first_lane(rest & (logits == e2))
    t = jnp.exp(e2 - e1)
    g1 = p_sel * (1.0 / (1.0 + t))
    g2 = p_sel * (t / (1.0 + t))
    eid = jnp.where(lane == 0.0, j1 - N_GROUPS, jnp.where(lane == 1.0, j2 - N_GROUPS, 0.0))
    eid_ref[...] = eid.astype(jnp.int32)
    gate_ref[...] = jnp.where(lane == 0.0, g1, jnp.where(lane == 1.0, g2, 0.0))


def _router(x, g, w_router, b_router, tm):
    m, d = x.shape
    return pl.pallas_call(
        _router_body,
        grid=(m // tm,),
        in_specs=[pl.BlockSpec((tm, d), lambda i: (i, 0)),
                  pl.BlockSpec((1, d), lambda i: (0, 0)),
                  pl.BlockSpec((d, LANES), lambda i: (0, 0)),
                  pl.BlockSpec((1, LANES), lambda i: (0, 0))],
        out_specs=[pl.BlockSpec((tm, d // 2), lambda i: (i, 0)),
                   pl.BlockSpec((tm, LANES), lambda i: (i, 0)),
                   pl.BlockSpec((tm, LANES), lambda i: (i, 0))],
        out_shape=[jax.ShapeDtypeStruct((m, d // 2), jnp.int32),
                   jax.ShapeDtypeStruct((m, LANES), jnp.int32),
                   jax.ShapeDtypeStruct((m, LANES), F32)],
        compiler_params=_cparams("parallel"),
        name="moe_router",
    )(x, g.reshape(1, d), w_router, b_router.reshape(1, LANES))


def _expert_body(ge_ref, gb0_ref, gnb_ref, used_ref, tok_ref, h_hbm, wg_ref, wu_ref, wd_ref, ys_hbm,
                 xs_ref, xb_ref, acc_ref, wgb_ref, wub_ref, wdb_ref, sem_in, sem_out,
                 *, bm, n_f, n_blocks):
    g = pl.program_id(0)
    f = pl.program_id(1)
    nb = gnb_ref[g]
    blk0 = gb0_ref[g]

    @pl.when(jnp.logical_and(g == 0, f == 0))
    def _():
        acc_ref[pl.ds(0, bm)] = jnp.zeros((bm,) + acc_ref.shape[1:], F32)

        def tail_copy(i):
            return pltpu.make_async_copy(acc_ref.at[pl.ds(0, bm)], ys_hbm.at[pl.ds(i * bm, bm)],
                                         sem_out)

        def start_tail(i, _):
            tail_copy(i).start()
            return 0

        lax.fori_loop(used_ref[0], n_blocks, start_tail, 0)

        def wait_tail(i, _):
            tail_copy(i).wait()
            return 0

        lax.fori_loop(used_ref[0], n_blocks, wait_tail, 0)

    def gather(grp):
        slot = grp % 2
        base = gb0_ref[grp] * bm

        def issue(r, _):
            pltpu.make_async_copy(h_hbm.at[pl.ds(tok_ref[base + r], 1)],
                                  xs_ref.at[slot, pl.ds(r, 1)], sem_in.at[slot]).start()
            return 0

        lax.fori_loop(0, gnb_ref[grp] * bm, issue, 0)

    def block_copy_out(i):
        return pltpu.make_async_copy(acc_ref.at[pl.ds(i * bm, bm)],
                                     ys_hbm.at[pl.ds((blk0 + i) * bm, bm)], sem_out)

    @pl.when(jnp.logical_and(g == 0, f == 0))
    def _():
        gather(g)

    @pl.when(jnp.logical_and(nb > 0, f == 0))
    def _():
        slot = g % 2

        def land(i, _):
            rows = pl.ds(pl.multiple_of(i * bm, bm), bm)
            pltpu.make_async_copy(h_hbm.at[pl.ds(0, bm)], xs_ref.at[slot, rows], sem_in.at[slot]).wait()
            return 0

        lax.fori_loop(0, nb, land, 0)

        half = xb_ref.shape[1] // 2

        def unpack(i, _):
            rows = pl.ds(pl.multiple_of(i * bm, bm), bm)
            w = xs_ref[slot, rows, :]
            lo = lax.bitcast_convert_type(lax.shift_left(w, 16), F32)
            hi = lax.bitcast_convert_type(w & jnp.int32(-65536), F32)
            xb_ref[rows, :half] = lo.astype(BF16)
            xb_ref[rows, half:] = hi.astype(BF16)
            return 0

        lax.fori_loop(0, nb, unpack, 0)

    @pl.when(jnp.logical_and(f == 0, g + 1 < pl.num_programs(0)))
    def _():
        gather(g + 1)

    @pl.when(nb > 0)
    def _():
        wgb_ref[...] = wg_ref[...].astype(BF16)
        wub_ref[...] = wu_ref[...].astype(BF16)
        wdb_ref[...] = wd_ref[...].astype(BF16)

        def block(i, _):
            rows = pl.ds(pl.multiple_of(i * bm, bm), bm)
            xb = xb_ref[rows, :]
            a = jnp.dot(xb, wgb_ref[...], preferred_element_type=F32)
            u = jnp.dot(xb, wub_ref[...], preferred_element_type=F32)
            hid = (a * _sigmoid(a)) * u
            y = jnp.dot(hid.astype(BF16), wdb_ref[...], preferred_element_type=F32)

            @pl.when(f == 0)
            def _():
                acc_ref[rows, :] = y

            @pl.when(f != 0)
            def _():
                acc_ref[rows, :] += y

            return 0

        lax.fori_loop(0, nb, block, 0)

    @pl.when(jnp.logical_and(nb > 0, f == n_f - 1))
    def _():
        def start_out(i, _):
            block_copy_out(i).start()
            return 0

        lax.fori_loop(0, nb, start_out, 0)

        def wait_out(i, _):
            block_copy_out(i).wait()
            return 0

        lax.fori_loop(0, nb, wait_out, 0)


def _experts(h2, slot_tok, grp_e, grp_blk0, grp_nb, n_used, w_gate, w_up, w_down, n_slots):
    d = w_gate.shape[1]
    ff = w_gate.shape[2]
    n_f = ff // MOE_FC
    n_groups = grp_e.shape[0]
    rmax = MOE_GROUP_BLOCKS * MOE_BM

    def f_idx(g, f, gnb):
        return jnp.where(gnb[g] > 0, f, n_f - 1)

    grid_spec = pltpu.PrefetchScalarGridSpec(
        num_scalar_prefetch=5,
        grid=(n_groups, n_f),
        in_specs=[
            pl.BlockSpec(memory_space=pl.ANY),
            pl.BlockSpec((None, d, MOE_FC), lambda g, f, ge, gb0, gnb, *_: (ge[g], 0, f_idx(g, f, gnb))),
            pl.BlockSpec((None, d, MOE_FC), lambda g, f, ge, gb0, gnb, *_: (ge[g], 0, f_idx(g, f, gnb))),
            pl.BlockSpec((None, MOE_FC, d), lambda g, f, ge, gb0, gnb, *_: (ge[g], f_idx(g, f, gnb), 0)),
        ],
        out_specs=pl.BlockSpec(memory_space=pl.ANY),
        scratch_shapes=[pltpu.VMEM((2, rmax, d // 2), jnp.int32),
                        pltpu.VMEM((rmax, d), BF16),
                        pltpu.VMEM((rmax, d), F32),
                        pltpu.VMEM((d, MOE_FC), BF16),
                        pltpu.VMEM((d, MOE_FC), BF16),
                        pltpu.VMEM((MOE_FC, d), BF16),
                        pltpu.SemaphoreType.DMA((2,)),
                        pltpu.SemaphoreType.DMA(())],
    )
    return pl.pallas_call(
        functools.partial(_expert_body, bm=MOE_BM, n_f=n_f, n_blocks=n_slots // MOE_BM),
        grid_spec=grid_spec,
        out_shape=jax.ShapeDtypeStruct((n_slots, d), F32),
        compiler_params=_cparams("arbitrary", "arbitrary"),
        name="moe_experts",
    )(grp_e, grp_blk0, grp_nb, n_used, slot_tok, h2, w_gate, w_up, w_down)


def _combine_body(pos_ref, x_ref, gate_ref, ys_hbm, o_ref, buf_ref, sem, *, tm):
    i = pl.program_id(0)
    n_steps = pl.num_programs(0)

    def issue(step):
        slot = step % 2

        def body(r, _):
            for k in range(MOE_TOP_K):
                src = pos_ref[(step * tm + r) * MOE_TOP_K + k]
                pltpu.make_async_copy(ys_hbm.at[pl.ds(src, 1)], buf_ref.at[slot, k, pl.ds(r, 1)],
                                      sem.at[slot]).start()
            return 0

        lax.fori_loop(0, tm, body, 0)

    @pl.when(i == 0)
    def _():
        issue(i)

    @pl.when(i + 1 < n_steps)
    def _():
        issue(i + 1)

    slot = i % 2
    for k in range(MOE_TOP_K):
        pltpu.make_async_copy(ys_hbm.at[pl.ds(0, tm)], buf_ref.at[slot, k], sem.at[slot]).wait()
    o_ref[...] = x_ref[...] + (buf_ref[slot, 0] * gate_ref[:, 0:1] + buf_ref[slot, 1] * gate_ref[:, 1:2])


def _combine(x, gate, ys, pos, tm):
    m, d = x.shape
    grid_spec = pltpu.PrefetchScalarGridSpec(
        num_scalar_prefetch=1,
        grid=(m // tm,),
        in_specs=[pl.BlockSpec((tm, d), lambda i, pos: (i, 0)),
                  pl.BlockSpec((tm, LANES), lambda i, pos: (i, 0)),
                  pl.BlockSpec(memory_space=pl.ANY)],
        out_specs=pl.BlockSpec((tm, d), lambda i, pos: (i, 0)),
        scratch_shapes=[pltpu.VMEM((2, MOE_TOP_K, tm, d), F32), pltpu.SemaphoreType.DMA((2,))],
    )
    return pl.pallas_call(
        functools.partial(_combine_body, tm=tm),
        grid_spec=grid_spec,
        out_shape=jax.ShapeDtypeStruct((m, d), F32),
        compiler_params=_cparams("arbitrary"),
        name="moe_combine",
    )(pos, x, gate, ys)


def _moe_plan(eid, n_tokens):
    a = n_tokens * MOE_TOP_K
    eid = eid.reshape(a)
    n_blocks = (a + N_EXPERTS * (MOE_BM - 1)) // MOE_BM
    n_slots = n_blocks * MOE_BM
    n_groups = n_blocks // MOE_GROUP_BLOCKS + N_EXPERTS
    onehot = (eid[:, None] == jnp.arange(N_EXPERTS, dtype=jnp.int32)[None, :]).astype(jnp.int32)
    rank = jnp.take_along_axis(jnp.cumsum(onehot, axis=0), eid[:, None], axis=1)[:, 0] - 1
    counts = jnp.sum(onehot, axis=0)
    blocks_e = (counts + MOE_BM - 1) // MOE_BM
    blk_start = jnp.cumsum(blocks_e) - blocks_e
    pos = (blk_start[eid] * MOE_BM + rank).astype(jnp.int32)
    tok = jnp.arange(a, dtype=jnp.int32) // MOE_TOP_K
    slot_tok = jnp.zeros((n_slots,), jnp.int32).at[pos].set(tok)
    groups_e = (blocks_e + MOE_GROUP_BLOCKS - 1) // MOE_GROUP_BLOCKS
    grp_end = jnp.cumsum(groups_e)
    grp_start = grp_end - groups_e
    gidx = jnp.arange(n_groups, dtype=jnp.int32)
    ge = jnp.minimum(jnp.searchsorted(grp_end, gidx, side="right"), N_EXPERTS - 1).astype(jnp.int32)
    local = gidx - grp_start[ge]
    active = gidx < grp_end[-1]
    gb0 = jnp.where(active, blk_start[ge] + local * MOE_GROUP_BLOCKS, 0).astype(jnp.int32)
    gnb = jnp.where(active, jnp.minimum(blocks_e[ge] - local * MOE_GROUP_BLOCKS, MOE_GROUP_BLOCKS),
                    0).astype(jnp.int32)
    last_e = ge[jnp.maximum(grp_end[-1] - 1, 0)]
    ge = jnp.where(active, ge, last_e).astype(jnp.int32)
    n_used = jnp.sum(blocks_e).astype(jnp.int32).reshape(1)
    return pos, slot_tok, ge, gb0, gnb, n_used, n_slots


def _layer(x, mem, positions, attn_norm_g, mem_norm_g, w_in, fox_forget_b, fox_q_norm_g,
           fox_k_norm_g, dsa_q_norm_g, dsa_k_norm_g, idx_k_norm_g, cross_q_norm_g, cross_k_norm_g,
           w_mem_kv, w_branch_fox, w_branch_dsa, w_branch_cross, w_out, ffn_norm_g,
           router_group_w, router_group_b, router_expert_w, router_expert_b,
           expert_w_gate, expert_w_up, expert_w_down):
    b, s, d = x.shape
    n = b * s
    n_mem = mem.shape[1]
    fw, dw, cw = FOX_HEADS * HEAD_DIM, DSA_HEADS * HEAD_DIM, CROSS_HEADS * HEAD_DIM
    iqw = IDX_HEADS * IDX_DIM
    tm = min(512, n)
    x2d = x.reshape(n, d)

    sizes = (fw, fw, fw, FOX_HEADS, dw, dw, dw, iqw, IDX_HEADS, IDX_DIM, cw, 3 * d)
    offs = [0]
    for sz in sizes:
        offs.append(offs[-1] + sz)
    (c_fq, c_fk, c_fv, c_fl, c_dq, c_dk, c_dv, c_iq, c_iw, c_ik, c_cq, c_g) = range(12)

    w_t = jnp.transpose(w_in, (2, 0, 1)).reshape(offs[-1], d)
    order = (c_dq, c_dk, c_fv, c_dv, c_fq, c_fk, c_cq, c_iq, c_g)
    row0, r = {}, 0
    for c in order:
        row0[c] = r
        r += sizes[c]
    w_all = jnp.concatenate([w_t[offs[c]:offs[c + 1]] for c in order], axis=0).astype(BF16)
    small_rows = [w_t[offs[c]:offs[c + 1]] for c in (c_ik, c_iw, c_fl)]
    small_rows.append(jnp.zeros((LANES - IDX_DIM - IDX_HEADS - FOX_HEADS, d), F32))
    w_small = jnp.concatenate(small_rows, axis=0).astype(BF16)
    q_unit = (HEAD_DIM ** -0.5) * LOG2E
    g_qk1 = jnp.concatenate([jnp.tile(fox_q_norm_g * q_unit, FOX_HEADS),
                             jnp.tile(fox_k_norm_g, FOX_HEADS),
                             jnp.tile(cross_q_norm_g * q_unit, CROSS_HEADS)])
    g_qk2 = jnp.concatenate([jnp.tile(dsa_q_norm_g * q_unit, DSA_HEADS),
                             jnp.tile(dsa_k_norm_g, DSA_HEADS)])
    g_ik = jnp.concatenate([idx_k_norm_g, jnp.zeros((LANES - IDX_DIM,), F32)])
    fb_lane = jnp.zeros((LANES,), F32).at[SMALL_F:SMALL_F + FOX_HEADS].set(fox_forget_b)

    pos_col = positions.reshape(n, 1).astype(F32)
    cd, sd = _rope_tables(pos_col, *_rope_lane_patterns(ROT_DIM, HEAD_DIM), tm=tm)
    ci, si = _rope_tables(pos_col, *_rope_lane_patterns(IDX_ROT_DIM, IDX_DIM), tm=tm)

    h = _rmsnorm(x2d, attn_norm_g, BF16, tm)
    qk1 = _proj(h, w_all, "norm", BF16, tm, 1024, gains=g_qk1, w_rows=(row0[c_fq], 2 * fw + cw))
    qk2 = _proj(h, w_all, "norm_rope", BF16, tm, 768, gains=g_qk2, tables=(cd, sd),
                w_rows=(row0[c_dq], 2 * dw))
    vv = _proj(h, w_all, "raw", BF16, tm, 768, w_rows=(row0[c_fv], fw + dw))
    iq = _proj(h, w_all, "rope_idx", BF16, tm, 1024, tables=(ci, si), w_rows=(row0[c_iq], iqw))
    small = _proj(h, w_small, "raw", F32, tm, LANES, w_rows=(0, LANES))
    ika, ikb, wl, lf_col = _small(small, g_ik, fb_lane, ci, si, tm)

    tq, tk = min(ATT_TQ, s), min(ATT_TK, s)
    assert tq == tk and s % tq == 0
    cum_col = _forget_cumsum(lf_col.reshape(b, s, LANES))
    qk1 = qk1.reshape(b, s, 2 * fw + cw)
    qk2 = qk2.reshape(b, s, 2 * dw)
    q_aug, k_aug = _fox_pack(qk1, cum_col, FOX_HEADS, min(512, s))
    vt = jnp.transpose(vv.reshape(b, s // tk, tk, fw + dw), (0, 1, 3, 2))
    o_fox = _attention(q_aug, 0, k_aug, 0, vt, 0, (), "fox", FOX_HEADS, AUG_DIM, tq, tk)

    topk = min(INDEX_TOPK, s // 4)
    wt = jnp.transpose(wl.reshape(b, s, LANES)[:, :, :IDX_HEADS], (0, 2, 1))
    bias = _select_bias(iq.reshape(b, s, iqw), ika.reshape(b, s, LANES), ikb.reshape(b, s, LANES),
                        wt, topk, tq, tk)
    o_dsa = _attention(qk2, 0, qk2, 1, vt, 1, (bias,), "dsa", DSA_HEADS, HEAD_DIM, tq, tk)

    tmm = min(512, b * n_mem)
    m_n = _rmsnorm(mem.reshape(b * n_mem, d), mem_norm_g, BF16, tmm)
    w_kv = w_mem_kv.astype(BF16)
    mk = _proj(m_n, w_kv[:, :cw], "norm", BF16, tmm, cw, gains=jnp.tile(cross_k_norm_g, CROSS_HEADS))
    mv = _proj(m_n, w_kv[:, cw:], "raw", BF16, tmm, cw)
    o_cross = _cross_attention(qk1, (2 * fw) // cw, mk.reshape(b, n_mem, cw), mv.reshape(b, n_mem, cw),
                               min(512, s))

    x_mid = _merge(h, o_fox.reshape(n, fw), o_dsa.reshape(n, dw), o_cross.reshape(n, cw), x2d,
                   w_all, row0[c_g], w_branch_fox.astype(BF16), w_branch_dsa.astype(BF16),
                   w_branch_cross.astype(BF16), w_out.astype(BF16), tm, MERGE_TN)

    w_router = jnp.concatenate(
        [router_group_w, router_expert_w, jnp.zeros((d, LANES - N_GROUPS - N_EXPERTS), F32)],
        axis=1).astype(BF16)
    b_router = jnp.concatenate(
        [router_group_b, router_expert_b, jnp.zeros((LANES - N_GROUPS - N_EXPERTS,), F32)])
    h2, eid, gate = _router(x_mid, ffn_norm_g, w_router, b_router, tm)
    pos, slot_tok, ge, gb0, gnb, n_used, n_slots = _moe_plan(eid[:, :MOE_TOP_K], n)
    ys = _experts(h2, slot_tok, ge, gb0, gnb, n_used, expert_w_gate, expert_w_up, expert_w_down,
                  n_slots)
    out = _combine(x_mid, gate, ys, pos, min(256, n))
    return out.reshape(b, s, d)


def kernel(x, mem, positions, attn_norm_g, mem_norm_g, w_in, fox_forget_b, fox_q_norm_g, fox_k_norm_g, dsa_q_norm_g, dsa_k_norm_g, idx_k_norm_g, cross_q_norm_g, cross_k_norm_g, w_mem_kv, w_branch_fox, w_branch_dsa, w_branch_cross, w_out, ffn_norm_g, router_group_w, router_group_b, router_expert_w, router_expert_b, expert_w_gate, expert_w_up, expert_w_down):
    depth = w_in.shape[0]
    for layer in range(depth):
        x = _layer(
            x, mem, positions, attn_norm_g[layer], mem_norm_g[layer], w_in[layer:layer + 1],
            fox_forget_b[layer], fox_q_norm_g[layer], fox_k_norm_g[layer],
            dsa_q_norm_g[layer], dsa_k_norm_g[layer], idx_k_norm_g[layer],
            cross_q_norm_g[layer], cross_k_norm_g[layer], w_mem_kv[layer],
            w_branch_fox[layer], w_branch_dsa[layer], w_branch_cross[layer], w_out[layer],
            ffn_norm_g[layer], router_group_w[layer], router_group_b[layer],
            router_expert_w[layer], router_expert_b[layer], expert_w_gate[layer],
            expert_w_up[layer], expert_w_down[layer])
    return x
```

```python
import functools

import jax
import jax.numpy as jnp
from jax import lax
from jax.experimental import pallas as pl
from jax.experimental.pallas import tpu as pltpu

F32 = jnp.float32
BF16 = jnp.bfloat16

LANES = 128
HEAD_DIM = 128
FOX_HEADS = 6
DSA_HEADS = 6
CROSS_HEADS = 4
IDX_HEADS = 16
IDX_DIM = 64
ROPE_THETA = 500000.0
ROT_DIM = HEAD_DIM // 4
IDX_ROT_DIM = IDX_DIM // 4
INDEX_TOPK = 256
N_GROUPS = 4
EXPERTS_PER_GROUP = 8
N_EXPERTS = N_GROUPS * EXPERTS_PER_GROUP
MOE_TOP_K = 2
EPS = 1e-6

NEG = -1e30
INT_MIN = -(2 ** 31)
VMEM_LIMIT = 56 * 1024 * 1024

ATT_TQ = 256
ATT_TK = 256
ATT_HEADS_PER_LOOP = 6
MERGE_TN = 512
MERGE_SUB = 256
MOE_BM = 256
MOE_GROUP_BLOCKS = 4
MOE_FC = 512
DMA_ISSUE_UNROLL = 8


def _cparams(*sem):
    return pltpu.CompilerParams(dimension_semantics=sem, vmem_limit_bytes=VMEM_LIMIT)


def _rmsnorm_body(x_ref, g_ref, o_ref):
    x = x_ref[...]
    r = lax.rsqrt(jnp.mean(x * x, axis=-1, keepdims=True) + EPS)
    o_ref[...] = ((x * r) * g_ref[...]).astype(o_ref.dtype)


def _rmsnorm(x2d, g, out_dtype, tm):
    m, d = x2d.shape
    return pl.pallas_call(
        _rmsnorm_body,
        grid=(m // tm,),
        in_specs=[pl.BlockSpec((tm, d), lambda i: (i, 0)),
                  pl.BlockSpec((1, d), lambda i: (0, 0))],
        out_specs=pl.BlockSpec((tm, d), lambda i: (i, 0)),
        out_shape=jax.ShapeDtypeStruct((m, d), out_dtype),
        compiler_params=_cparams("parallel"),
        name="rmsnorm",
    )(x2d, g.reshape(1, d))


W_PREP_ROWS = 128


def _w_prep_body(src_ref, w_ref, o_ref):
    del src_ref
    o_ref[...] = w_ref[:, 0, :].astype(o_ref.dtype)


def _w_prep(w_t, src_rows):
    _, _, k = w_t.shape
    n_chunks = src_rows.shape[0]
    grid_spec = pltpu.PrefetchScalarGridSpec(
        num_scalar_prefetch=1,
        grid=(n_chunks,),
        in_specs=[pl.BlockSpec((pl.Element(W_PREP_ROWS), pl.Element(1), pl.Element(k)),
                               lambda c, src: (src[c], 0, 0))],
        out_specs=pl.BlockSpec((W_PREP_ROWS, k), lambda c, src: (c, 0)),
    )
    return pl.pallas_call(
        _w_prep_body,
        grid_spec=grid_spec,
        out_shape=jax.ShapeDtypeStruct((n_chunks * W_PREP_ROWS, k), BF16),
        compiler_params=_cparams("parallel"),
        name="w_prep",
    )(src_rows, w_t)


def _rope_table_body(pos_ref, f_ref, sgn_ref, c_ref, s_ref):
    ang = pos_ref[...] * f_ref[...]
    c_ref[...] = jnp.cos(ang)
    s_ref[...] = jnp.sin(ang) * sgn_ref[...]


def _rope_tables(pos_col, freq_lane, sign_lane, tm):
    n = pos_col.shape[0]
    row = pl.BlockSpec((1, LANES), lambda i: (0, 0))
    tab = pl.BlockSpec((tm, LANES), lambda i: (i, 0))
    return pl.pallas_call(
        _rope_table_body,
        grid=(n // tm,),
        in_specs=[pl.BlockSpec((tm, 1), lambda i: (i, 0)), row, row],
        out_specs=[tab, tab],
        out_shape=[jax.ShapeDtypeStruct((n, LANES), F32)] * 2,
        compiler_params=_cparams("parallel"),
        name="rope_tables",
    )(pos_col, freq_lane.reshape(1, LANES), sign_lane.reshape(1, LANES))


def _rope_lane_patterns(rot_dim, period):
    half = rot_dim // 2
    inv_freq = jnp.power(jnp.float32(ROPE_THETA), -jnp.arange(half, dtype=F32) * 2.0 / rot_dim)
    lane = jnp.arange(LANES) % period
    freq = jnp.where(lane < rot_dim, inv_freq[lane % half], 0.0).astype(F32)
    sign = jnp.where(lane < half, -1.0, jnp.where(lane < rot_dim, 1.0, 0.0)).astype(F32)
    return freq, sign


def _apply_rope(a, c, s, half, period):
    lane = lax.broadcasted_iota(jnp.int32, a.shape, 1)
    first = (lane & (period - 1)) < half
    partner = jnp.where(first, pltpu.roll(a, LANES - half, 1), pltpu.roll(a, half, 1))
    return a * c + partner * s


def _proj_body(*refs, mode, n_lane_groups, w_rows):
    h_ref, w_ref = refs[0], refs[1]
    o_ref = refs[-1]
    if w_rows:
        acc = lax.dot_general(h_ref[...], w_ref[...], (((1,), (1,)), ((), ())),
                              preferred_element_type=F32)
    else:
        acc = jnp.dot(h_ref[...], w_ref[...], preferred_element_type=F32)
    if mode == "raw":
        o_ref[...] = acc.astype(o_ref.dtype)
        return
    for g in range(n_lane_groups):
        sl = slice(g * LANES, (g + 1) * LANES)
        a = acc[:, sl]
        if mode in ("norm", "norm_rope"):
            g_ref = refs[2]
            a = (a * lax.rsqrt(jnp.mean(a * a, axis=-1, keepdims=True) + EPS)) * g_ref[:, sl]
        if mode == "norm_rope":
            a = _apply_rope(a, refs[3][...], refs[4][...], ROT_DIM // 2, HEAD_DIM)
        if mode == "rope_idx":
            a = _apply_rope(a, refs[2][...], refs[3][...], IDX_ROT_DIM // 2, IDX_DIM)
        o_ref[:, sl] = a.astype(o_ref.dtype)


def _proj(h, w, mode, out_dtype, tm, tn, gains=None, tables=None, w_rows=None):
    m, k = h.shape
    if w_rows is None:
        n = w.shape[1]
        w_spec = pl.BlockSpec((k, tn), lambda j, i: (0, j))
    else:
        row0, n = w_rows
        assert row0 % tn == 0 and n % tn == 0
        w_spec = pl.BlockSpec((tn, k), lambda j, i: (row0 // tn + j, 0))
    in_specs = [pl.BlockSpec((tm, k), lambda j, i: (i, 0)), w_spec]
    args = [h, w]
    if gains is not None:
        in_specs.append(pl.BlockSpec((1, tn), lambda j, i: (0, j)))
        args.append(gains.reshape(1, n))
    if tables is not None:
        for t in tables:
            in_specs.append(pl.BlockSpec((tm, LANES), lambda j, i: (i, 0)))
            args.append(t)
    return pl.pallas_call(
        functools.partial(_proj_body, mode=mode, n_lane_groups=tn // LANES,
                          w_rows=w_rows is not None),
        grid=(n // tn, m // tm),
        in_specs=in_specs,
        out_specs=pl.BlockSpec((tm, tn), lambda j, i: (i, j)),
        out_shape=jax.ShapeDtypeStruct((m, n), out_dtype),
        compiler_params=_cparams("parallel", "parallel"),
        name="proj_" + mode,
    )(*args)


SMALL_IK = 0
SMALL_IW = 64
SMALL_F = 80


def _small_body(x_ref, gk_ref, fb_ref, c_ref, s_ref, ika_ref, ikb_ref, wl_ref, lf_ref):
    x = x_ref[...]
    lane = lax.broadcasted_iota(jnp.int32, x.shape, 1)
    is_ik = lane < IDX_DIM
    ik = jnp.where(is_ik, x, 0.0)
    ms = jnp.sum(ik * ik, axis=-1, keepdims=True) * (1.0 / IDX_DIM)
    ik = (ik * lax.rsqrt(ms + EPS)) * gk_ref[...]
    ik = _apply_rope(ik, c_ref[...], s_ref[...], IDX_ROT_DIM // 2, IDX_DIM)
    ik = jnp.where(is_ik, ik, 0.0)
    ika_ref[...] = ik.astype(BF16)
    ikb_ref[...] = pltpu.roll(ik, IDX_DIM, 1).astype(BF16)
    wl_ref[...] = pltpu.roll(x, LANES - SMALL_IW, 1) * (2.0 ** -5)
    z = x + fb_ref[...]
    log_f = jnp.minimum(z, 0.0) - jnp.log1p(jnp.exp(-jnp.abs(z)))
    lf_ref[...] = pltpu.roll(log_f, LANES - SMALL_F, 1)


def _small(x, idx_k_gain_lane, forget_b_lane, ci, si, tm):
    n = x.shape[0]
    row = pl.BlockSpec((1, LANES), lambda i: (0, 0))
    tab = pl.BlockSpec((tm, LANES), lambda i: (i, 0))
    return pl.pallas_call(
        _small_body,
        grid=(n // tm,),
        in_specs=[tab, row, row, tab, tab],
        out_specs=[tab, tab, tab, tab],
        out_shape=[jax.ShapeDtypeStruct((n, LANES), BF16), jax.ShapeDtypeStruct((n, LANES), BF16),
                   jax.ShapeDtypeStruct((n, LANES), F32), jax.ShapeDtypeStruct((n, LANES), F32)],
        compiler_params=_cparams("parallel"),
        name="small_cols",
    )(x, idx_k_gain_lane.reshape(1, LANES), forget_b_lane.reshape(1, LANES), ci, si)


def _cumsum_body(col_ref, ccol_ref, *, seq):
    c = col_ref[...]
    ci = lax.broadcasted_iota(jnp.int32, c.shape, 0)
    sh = 1
    while sh < seq:
        c = c + jnp.where(ci >= sh, pltpu.roll(c, sh, 0), 0.0)
        sh *= 2
    ccol_ref[...] = c


def _forget_cumsum(lf_col):
    b, s, _ = lf_col.shape
    return pl.pallas_call(
        functools.partial(_cumsum_body, seq=s),
        grid=(b,),
        in_specs=[pl.BlockSpec((None, s, LANES), lambda i: (i, 0, 0))],
        out_specs=pl.BlockSpec((None, s, LANES), lambda i: (i, 0, 0)),
        out_shape=jax.ShapeDtypeStruct((b, s, LANES), F32),
        compiler_params=_cparams("parallel"),
        name="forget_cumsum",
    )(lf_col)


AUG_DIM = 2 * HEAD_DIM
LOG2E = 1.4426950408889634


def _split3(x):
    hi = x.astype(BF16).astype(F32)
    r1 = x - hi
    mid = r1.astype(BF16).astype(F32)
    lo = (r1 - mid).astype(BF16).astype(F32)
    return hi, mid, lo


def _fox_pack_body(q_ref, k_ref, cum_ref, qa_ref, ka_ref, *, n_heads):
    rows = q_ref.shape[0]
    lane = lax.broadcasted_iota(jnp.int32, (rows, LANES), 1)
    for h in range(n_heads):
        c = jnp.broadcast_to(cum_ref[:, h:h + 1], (rows, LANES)) * LOG2E
        qh, qm, ql = _split3(c)
        kh, km, kl = _split3(-c)
        q_aug = jnp.where(lane == 0, qh, jnp.where(lane == 1, qm, jnp.where(
            lane == 2, ql, jnp.where(lane < 6, 1.0, 0.0))))
        k_aug = jnp.where(lane < 3, 1.0, jnp.where(lane == 3, kh, jnp.where(
            lane == 4, km, jnp.where(lane == 5, kl, 0.0))))
        base = h * AUG_DIM
        qa_ref[:, base:base + HEAD_DIM] = q_ref[:, h * HEAD_DIM:(h + 1) * HEAD_DIM]
        ka_ref[:, base:base + HEAD_DIM] = k_ref[:, h * HEAD_DIM:(h + 1) * HEAD_DIM]
        qa_ref[:, base + HEAD_DIM:base + AUG_DIM] = q_aug.astype(BF16)
        ka_ref[:, base + HEAD_DIM:base + AUG_DIM] = k_aug.astype(BF16)


def _fox_pack(qk, cum_col, n_heads, ts):
    b, s, _ = qk.shape
    w = n_heads * HEAD_DIM
    wa = n_heads * AUG_DIM
    return pl.pallas_call(
        functools.partial(_fox_pack_body, n_heads=n_heads),
        grid=(b, s // ts),
        in_specs=[pl.BlockSpec((None, ts, w), lambda bi, i: (bi, i, 0)),
                  pl.BlockSpec((None, ts, w), lambda bi, i: (bi, i, 1)),
                  pl.BlockSpec((None, ts, LANES), lambda bi, i: (bi, i, 0))],
        out_specs=[pl.BlockSpec((None, ts, wa), lambda bi, i: (bi, i, 0)),
                   pl.BlockSpec((None, ts, wa), lambda bi, i: (bi, i, 0))],
        out_shape=[jax.ShapeDtypeStruct((b, s, wa), BF16)] * 2,
        compiler_params=_cparams("parallel", "parallel"),
        name="fox_pack",
    )(qk, qk, cum_col)


def _attn_body(q_ref, k_ref, vt_ref, *rest, tq, tk, n_heads, kdim, mode, heads_per_loop):
    o_ref = rest[-1]
    qi = pl.program_id(1)
    key0 = lax.broadcasted_iota(jnp.int32, (tk, tq), 0)
    qry = qi * tq + lax.broadcasted_iota(jnp.int32, (tk, tq), 1)
    def scores(h, j):
        off = pl.multiple_of(j * tk, tk)
        q = q_ref[:, h * kdim:(h + 1) * kdim]
        k = k_ref[pl.ds(off, tk), h * kdim:(h + 1) * kdim]
        return lax.dot_general(k, q, (((1,), (1,)), ((), ())), preferred_element_type=F32)

    def update(h, j, s, m, l, acc, masked):
        if mode == "dsa":
            s = s + rest[0][j]
        if masked:
            s = jnp.where(key0 + j * tk <= qry, s, NEG)
        m_new = jnp.maximum(m, jnp.max(s, axis=0, keepdims=True))
        alpha = jnp.exp2(m - m_new)
        p = jnp.exp2(s - m_new)
        l = alpha * l + jnp.sum(p, axis=0, keepdims=True)
        vt = vt_ref[j, h * HEAD_DIM:(h + 1) * HEAD_DIM, :]
        acc = alpha * acc + jnp.dot(vt, p.astype(BF16), preferred_element_type=F32)
        return m_new, l, acc

    last = qi
    for h0 in range(0, n_heads, heads_per_loop):
        heads = range(h0, h0 + heads_per_loop)

        def step(j, carries, heads=heads):
            nxt = jnp.minimum(j + 1, last)
            out = []
            for h, (s, m, l, acc) in zip(heads, carries):
                s_next = scores(h, nxt)
                m, l, acc = update(h, j, s, m, l, acc, False)
                out.append((s_next, m, l, acc))
            return tuple(out)

        carries = tuple(
            (scores(h, 0), jnp.full((1, tq), NEG, F32), jnp.zeros((1, tq), F32),
             jnp.zeros((HEAD_DIM, tq), F32)) for h in heads)
        n_loop = last if mode == "fox" else last + 1
        carries = lax.fori_loop(0, n_loop, step, carries)
        for h, (s, m, l, acc) in zip(heads, carries):
            if mode == "fox":
                m, l, acc = update(h, last, s, m, l, acc, True)
            o_ref[:, h * HEAD_DIM:(h + 1) * HEAD_DIM] = jnp.transpose(acc / l).astype(o_ref.dtype)


def _attention(q_arr, q_blk, k_arr, k_blk, vt_arr, v_blk, extras, mode, n_heads, kdim, tq, tk):
    b, s, _ = q_arr.shape
    w = n_heads * HEAD_DIM
    wk = n_heads * kdim
    in_specs = [pl.BlockSpec((None, tq, wk), lambda bi, qi: (bi, qi, q_blk)),
                pl.BlockSpec((None, s, wk), lambda bi, qi: (bi, 0, k_blk)),
                pl.BlockSpec((None, s // tk, w, tk), lambda bi, qi: (bi, 0, v_blk, 0))]
    if mode == "dsa":
        in_specs += [pl.BlockSpec((None, None, s // tk, tk, tq), lambda bi, qi: (bi, qi, 0, 0, 0))]
    return pl.pallas_call(
        functools.partial(_attn_body, tq=tq, tk=tk, n_heads=n_heads, kdim=kdim, mode=mode,
                          heads_per_loop=ATT_HEADS_PER_LOOP),
        grid=(b, s // tq),
        in_specs=in_specs,
        out_specs=pl.BlockSpec((None, tq, w), lambda bi, qi: (bi, qi, 0)),
        out_shape=jax.ShapeDtypeStruct((b, s, w), BF16),
        compiler_params=_cparams("parallel", "parallel"),
        name="attn_" + mode,
    )(q_arr, k_arr, vt_arr, *extras)


def _select_body(iq_ref, ika_ref, ikb_ref, wt_ref, o_ref, keys_ref, half_ref, *, tq, tk, n_chunks, topk):
    qi = pl.program_id(1)
    used = qi + 1
    key0 = lax.broadcasted_iota(jnp.int32, (tk, tq), 0)
    qry = qi * tq + lax.broadcasted_iota(jnp.int32, (tk, tq), 1)

    def score_chunk(c, _):
        off = pl.multiple_of(c * tk, tk)
        ka = ika_ref[pl.ds(off, tk), :]
        kb = ikb_ref[pl.ds(off, tk), :]
        acc = jnp.zeros((tk, tq), F32)
        for j in range(IDX_HEADS):
            pair = iq_ref[:, (j // 2) * LANES:(j // 2 + 1) * LANES]
            logit = lax.dot_general(ka if j % 2 == 0 else kb, pair, (((1,), (1,)), ((), ())),
                                    preferred_element_type=F32)
            acc = acc + wt_ref[j:j + 1, :] * jnp.maximum(logit, 0.0)
        bits = lax.bitcast_convert_type(acc, jnp.int32)
        key = jnp.where(bits < 0, bits ^ jnp.int32(0x7FFFFFFF), bits)
        key = jnp.where(key0 + off <= qry, key, jnp.int32(INT_MIN))
        keys_ref[c] = key
        half_ref[c] = lax.shift_right_arithmetic(key, 16).astype(jnp.int16)
        return 0

    lax.fori_loop(0, used, score_chunk, 0)

    one16 = jnp.int16(1)
    zero16 = jnp.int16(0)

    def count_ge(cand16):
        def count_chunk(c, cnt):
            ge = jnp.where(half_ref[c] >= cand16, one16, zero16)
            parts = [ge[r:r + 16] for r in range(0, tk, 16)]
            while len(parts) > 1:
                parts = [a + b for a, b in zip(parts[::2], parts[1::2])]
            return cnt + parts[0]

        cnt = lax.fori_loop(0, used, count_chunk, jnp.zeros((16, tq), jnp.int16))
        return jnp.sum(cnt.astype(jnp.int32), axis=0, keepdims=True)

    def search16(need):
        def bit_step(i, prefix):
            cand_u = prefix | jnp.left_shift(jnp.int32(1), 15 - i)
            cnt = count_ge((cand_u - 32768).astype(jnp.int16))
            return jnp.where(cnt >= need, cand_u, prefix)

        return lax.fori_loop(0, 16, bit_step, jnp.zeros((1, tq), jnp.int32)) - 32768

    hi = search16(topk)
    above = count_ge(jnp.minimum(hi + 1, 32767).astype(jnp.int16))
    above = jnp.where(hi >= 32767, 0, above)
    hi16 = hi.astype(jnp.int16)

    def low_chunk(c, _):
        low = (keys_ref[c] & jnp.int32(0xFFFF)) - 32768
        half_ref[c] = jnp.where(half_ref[c] == hi16, low.astype(jnp.int16), jnp.int16(-32768))
        return 0

    lax.fori_loop(0, used, low_chunk, 0)
    lo = search16(topk - above)
    thr = lax.shift_left(hi, 16) | (lo + 32768)
    thr = jnp.maximum(thr, jnp.int32(INT_MIN + 1))

    def write_chunk(c, _):
        o_ref[c] = jnp.where(keys_ref[c] >= thr, 0.0, NEG)
        return 0

    def fill_chunk(c, _):
        o_ref[c] = jnp.full((tk, tq), NEG, F32)
        return 0

    lax.fori_loop(0, used, write_chunk, 0)
    lax.fori_loop(used, n_chunks, fill_chunk, 0)


def _select_bias(iq, ika, ikb, wt, topk, tq, tk):
    b, s, _ = iq.shape
    n_chunks = s // tk
    return pl.pallas_call(
        functools.partial(_select_body, tq=tq, tk=tk, n_chunks=n_chunks, topk=topk),
        grid=(b, s // tq),
        in_specs=[pl.BlockSpec((None, tq, IDX_HEADS * IDX_DIM), lambda bi, qi: (bi, qi, 0)),
                  pl.BlockSpec((None, s, LANES), lambda bi, qi: (bi, 0, 0)),
                  pl.BlockSpec((None, s, LANES), lambda bi, qi: (bi, 0, 0)),
                  pl.BlockSpec((None, IDX_HEADS, tq), lambda bi, qi: (bi, 0, qi))],
        out_specs=pl.BlockSpec((None, None, n_chunks, tk, tq), lambda bi, qi: (bi, qi, 0, 0, 0)),
        out_shape=jax.ShapeDtypeStruct((b, s // tq, n_chunks, tk, tq), F32),
        scratch_shapes=[pltpu.VMEM((n_chunks, tk, tq), jnp.int32),
                        pltpu.VMEM((n_chunks, tk, tq), jnp.int16)],
        compiler_params=_cparams("parallel", "parallel"),
        name="index_select",
    )(iq, ika, ikb, wt)


def _cross_body(q_ref, k_ref, v_ref, o_ref, *, n_heads):
    for h in range(n_heads):
        hs = slice(h * HEAD_DIM, (h + 1) * HEAD_DIM)
        s = lax.dot_general(q_ref[:, hs], k_ref[:, hs], (((1,), (1,)), ((), ())),
                            preferred_element_type=F32)
        p = jnp.exp2(s - jnp.max(s, axis=-1, keepdims=True))
        l = jnp.sum(p, axis=-1, keepdims=True)
        o = jnp.dot(p.astype(BF16), v_ref[:, hs], preferred_element_type=F32)
        o_ref[:, hs] = (o / l).astype(o_ref.dtype)


def _cross_attention(q_arr, q_blk, mk, mv, tq):
    b, s, _ = q_arr.shape
    n_mem = mk.shape[1]
    w = CROSS_HEADS * HEAD_DIM
    return pl.pallas_call(
        functools.partial(_cross_body, n_heads=CROSS_HEADS),
        grid=(b, s // tq),
        in_specs=[pl.BlockSpec((None, tq, w), lambda bi, qi: (bi, qi, q_blk)),
                  pl.BlockSpec((None, n_mem, w), lambda bi, qi: (bi, 0, 0)),
                  pl.BlockSpec((None, n_mem, w), lambda bi, qi: (bi, 0, 0))],
        out_specs=pl.BlockSpec((None, tq, w), lambda bi, qi: (bi, qi, 0)),
        out_shape=jax.ShapeDtypeStruct((b, s, w), BF16),
        compiler_params=_cparams("parallel", "parallel"),
        name="cross_attn",
    )(q_arr, mk, mv)


def _sigmoid(x):
    return 1.0 / (1.0 + jnp.exp(-x))


def _merge_body(h_ref, of_ref, od_ref, oc_ref, x_ref, wgf_ref, wgd_ref, wgc_ref,
                wbf_ref, wbd_ref, wbc_ref, wo_ref, o_ref):
    j = pl.program_id(1)
    h = h_ref[...]
    tn = wo_ref.shape[0]
    sub = min(tn, MERGE_SUB)

    def branch(wg_ref, ob_ref, wb_ref, cs):
        gate = _sigmoid(lax.dot_general(h, wg_ref[cs, :], (((1,), (1,)), ((), ())),
                                        preferred_element_type=F32))
        return gate * jnp.dot(ob_ref[...], wb_ref[:, cs], preferred_element_type=F32)

    part = None
    for c in range(tn // sub):
        cs = slice(c * sub, (c + 1) * sub)
        merged = branch(wgf_ref, of_ref, wbf_ref, cs) + branch(wgd_ref, od_ref, wbd_ref, cs)
        merged = merged + branch(wgc_ref, oc_ref, wbc_ref, cs)
        p = jnp.dot(merged.astype(BF16), wo_ref[cs, :], preferred_element_type=F32)
        part = p if part is None else part + p

    @pl.when(j == 0)
    def _():
        o_ref[...] = x_ref[...] + part

    @pl.when(j != 0)
    def _():
        o_ref[...] += part


def _merge(h, o_fox, o_dsa, o_cross, x, w_gates, gate_row0, wbf, wbd, wbc, w_out, tm, tn):
    m, d = x.shape
    nj = d // tn
    assert gate_row0 % tn == 0
    g0 = gate_row0 // tn
    row = lambda i, j: (i, 0)
    in_specs = [pl.BlockSpec((tm, d), row),
                pl.BlockSpec((tm, o_fox.shape[1]), row),
                pl.BlockSpec((tm, o_dsa.shape[1]), row),
                pl.BlockSpec((tm, o_cross.shape[1]), row),
                pl.BlockSpec((tm, d), row),
                pl.BlockSpec((tn, d), lambda i, j: (g0 + j, 0)),
                pl.BlockSpec((tn, d), lambda i, j: (g0 + nj + j, 0)),
                pl.BlockSpec((tn, d), lambda i, j: (g0 + 2 * nj + j, 0)),
                pl.BlockSpec((wbf.shape[0], tn), lambda i, j: (0, j)),
                pl.BlockSpec((wbd.shape[0], tn), lambda i, j: (0, j)),
                pl.BlockSpec((wbc.shape[0], tn), lambda i, j: (0, j)),
                pl.BlockSpec((tn, d), lambda i, j: (j, 0))]
    return pl.pallas_call(
        _merge_body,
        grid=(m // tm, nj),
        in_specs=in_specs,
        out_specs=pl.BlockSpec((tm, d), row),
        out_shape=jax.ShapeDtypeStruct((m, d), F32),
        compiler_params=_cparams("parallel", "arbitrary"),
        name="gated_merge",
    )(h, o_fox, o_dsa, o_cross, x, w_gates, w_gates, w_gates, wbf, wbd, wbc, w_out)


def _router_body(x_ref, g_ref, w_ref, b_ref, h_ref, eid_ref, gate_ref):
    x = x_ref[...]
    r = lax.rsqrt(jnp.mean(x * x, axis=-1, keepdims=True) + EPS)
    h = (x * r) * g_ref[...]
    hb = h.astype(BF16)
    bits = lax.bitcast_convert_type(hb.astype(F32), jnp.int32)
    half = bits.shape[1] // 2
    h_ref[...] = lax.shift_right_logical(bits[:, :half], 16) | (bits[:, half:] & jnp.int32(-65536))
    logits = jnp.dot(hb, w_ref[...], preferred_element_type=F32) + b_ref[...]
    lane = lax.broadcasted_iota(jnp.int32, logits.shape, 1).astype(F32)

    def masked_max(mask):
        return jnp.max(jnp.where(mask, logits, -jnp.inf), axis=-1, keepdims=True)

    def first_lane(mask):
        return jnp.min(jnp.where(mask, lane, float(LANES)), axis=-1, keepdims=True)

    is_group = lane < N_GROUPS
    gmax = masked_max(is_group)
    grp = first_lane(is_group & (logits == gmax))
    denom = jnp.sum(jnp.where(is_group, jnp.exp(logits - gmax), 0.0), axis=-1, keepdims=True)
    p_sel = 1.0 / denom
    lo = N_GROUPS + grp * EXPERTS_PER_GROUP
    in_grp = (lane >= lo) & (lane < lo + EXPERTS_PER_GROUP)
    e1 = masked_max(in_grp)
    j1 = first_lane(in_grp & (logits == e1))
    rest = in_grp & (lane != j1)
    e2 = masked_max(rest)
    j2 = first_lane(rest & (logits == e2))
    t = jnp.exp(e2 - e1)
    g1 = p_sel * (1.0 / (1.0 + t))
    g2 = p_sel * (t / (1.0 + t))
    eid = jnp.where(lane == 0.0, j1 - N_GROUPS, jnp.where(lane == 1.0, j2 - N_GROUPS, 0.0))
    eid_ref[...] = eid.astype(jnp.int32)
    gate_ref[...] = jnp.where(lane == 0.0, g1, jnp.where(lane == 1.0, g2, 0.0))


def _router(x, g, w_router, b_router, tm):
    m, d = x.shape
    return pl.pallas_call(
        _router_body,
        grid=(m // tm,),
        in_specs=[pl.BlockSpec((tm, d), lambda i: (i, 0)),
                  pl.BlockSpec((1, d), lambda i: (0, 0)),
                  pl.BlockSpec((d, LANES), lambda i: (0, 0)),
                  pl.BlockSpec((1, LANES), lambda i: (0, 0))],
        out_specs=[pl.BlockSpec((tm, d // 2), lambda i: (i, 0)),
                   pl.BlockSpec((tm, LANES), lambda i: (i, 0)),
                   pl.BlockSpec((tm, LANES), lambda i: (i, 0))],
        out_shape=[jax.ShapeDtypeStruct((m, d // 2), jnp.int32),
                   jax.ShapeDtypeStruct((m, LANES), jnp.int32),
                   jax.ShapeDtypeStruct((m, LANES), F32)],
        compiler_params=_cparams("parallel"),
        name="moe_router",
    )(x, g.reshape(1, d), w_router, b_router.reshape(1, LANES))


def _expert_body(ge_ref, gb0_ref, gnb_ref, used_ref, tok_ref, h_hbm, wg_ref, wu_ref, wd_ref, ys_hbm,
                 xs_ref, xb_ref, acc_ref, wgb_ref, wub_ref, wdb_ref, sem_in, sem_out,
                 *, bm, n_f, n_blocks):
    g = pl.program_id(0)
    f = pl.program_id(1)
    nb = gnb_ref[g]
    blk0 = gb0_ref[g]

    @pl.when(jnp.logical_and(g == 0, f == 0))
    def _():
        acc_ref[pl.ds(0, bm)] = jnp.zeros((bm,) + acc_ref.shape[1:], F32)

        def tail_copy(i):
            return pltpu.make_async_copy(acc_ref.at[pl.ds(0, bm)], ys_hbm.at[pl.ds(i * bm, bm)],
                                         sem_out)

        def start_tail(i, _):
            tail_copy(i).start()
            return 0

        lax.fori_loop(used_ref[0], n_blocks, start_tail, 0)

        def wait_tail(i, _):
            tail_copy(i).wait()
            return 0

        lax.fori_loop(used_ref[0], n_blocks, wait_tail, 0)

    def gather(grp):
        slot = grp % 2
        base = gb0_ref[grp] * bm

        def issue(r0, _):
            for u in range(DMA_ISSUE_UNROLL):
                r = r0 * DMA_ISSUE_UNROLL + u
                pltpu.make_async_copy(h_hbm.at[pl.ds(tok_ref[base + r], 1)],
                                      xs_ref.at[slot, pl.ds(r, 1)], sem_in.at[slot]).start()
            return 0

        lax.fori_loop(0, gnb_ref[grp] * (bm // DMA_ISSUE_UNROLL), issue, 0)

    def block_copy_out(i):
        return pltpu.make_async_copy(acc_ref.at[pl.ds(i * bm, bm)],
                                     ys_hbm.at[pl.ds((blk0 + i) * bm, bm)], sem_out)

    @pl.when(jnp.logical_and(g == 0, f == 0))
    def _():
        gather(g)

    @pl.when(jnp.logical_and(nb > 0, f == 0))
    def _():
        slot = g % 2

        def land(i, _):
            rows = pl.ds(pl.multiple_of(i * bm, bm), bm)
            pltpu.make_async_copy(h_hbm.at[pl.ds(0, bm)], xs_ref.at[slot, rows], sem_in.at[slot]).wait()
            return 0

        lax.fori_loop(0, nb, land, 0)

        half = xb_ref.shape[1] // 2

        def unpack(i, _):
            rows = pl.ds(pl.multiple_of(i * bm, bm), bm)
            w = xs_ref[slot, rows, :]
            lo = lax.bitcast_convert_type(lax.shift_left(w, 16), F32)
            hi = lax.bitcast_convert_type(w & jnp.int32(-65536), F32)
            xb_ref[rows, :half] = lo.astype(BF16)
            xb_ref[rows, half:] = hi.astype(BF16)
            return 0

        lax.fori_loop(0, nb, unpack, 0)

    @pl.when(jnp.logical_and(f == 0, g + 1 < pl.num_programs(0)))
    def _():
        gather(g + 1)

    @pl.when(nb > 0)
    def _():
        wgb_ref[...] = wg_ref[...].astype(BF16)
        wub_ref[...] = wu_ref[...].astype(BF16)
        wdb_ref[...] = wd_ref[...].astype(BF16)

        def block(i, _):
            rows = pl.ds(pl.multiple_of(i * bm, bm), bm)
            xb = xb_ref[rows, :]
            a = jnp.dot(xb, wgb_ref[...], preferred_element_type=F32)
            u = jnp.dot(xb, wub_ref[...], preferred_element_type=F32)
            hid = (a * _sigmoid(a)) * u
            y = jnp.dot(hid.astype(BF16), wdb_ref[...], preferred_element_type=F32)

            @pl.when(f == 0)
            def _():
                acc_ref[rows, :] = y

            @pl.when(f != 0)
            def _():
                acc_ref[rows, :] += y

            return 0

        lax.fori_loop(0, nb, block, 0)

    @pl.when(jnp.logical_and(nb > 0, f == n_f - 1))
    def _():
        def start_out(i, _):
            block_copy_out(i).start()
            return 0

        lax.fori_loop(0, nb, start_out, 0)

        def wait_out(i, _):
            block_copy_out(i).wait()
            return 0

        lax.fori_loop(0, nb, wait_out, 0)


def _experts(h2, slot_tok, grp_e, grp_blk0, grp_nb, n_used, w_gate, w_up, w_down, n_slots):
    d = w_gate.shape[1]
    ff = w_gate.shape[2]
    n_f = ff // MOE_FC
    n_groups = grp_e.shape[0]
    rmax = MOE_GROUP_BLOCKS * MOE_BM

    def f_idx(g, f, gnb):
        return jnp.where(gnb[g] > 0, f, n_f - 1)

    grid_spec = pltpu.PrefetchScalarGridSpec(
        num_scalar_prefetch=5,
        grid=(n_groups, n_f),
        in_specs=[
            pl.BlockSpec(memory_space=pl.ANY),
            pl.BlockSpec((None, d, MOE_FC), lambda g, f, ge, gb0, gnb, *_: (ge[g], 0, f_idx(g, f, gnb))),
            pl.BlockSpec((None, d, MOE_FC), lambda g, f, ge, gb0, gnb, *_: (ge[g], 0, f_idx(g, f, gnb))),
            pl.BlockSpec((None, MOE_FC, d), lambda g, f, ge, gb0, gnb, *_: (ge[g], f_idx(g, f, gnb), 0)),
        ],
        out_specs=pl.BlockSpec(memory_space=pl.ANY),
        scratch_shapes=[pltpu.VMEM((2, rmax, d // 2), jnp.int32),
                        pltpu.VMEM((rmax, d), BF16),
                        pltpu.VMEM((rmax, d), F32),
                        pltpu.VMEM((d, MOE_FC), BF16),
                        pltpu.VMEM((d, MOE_FC), BF16),
                        pltpu.VMEM((MOE_FC, d), BF16),
                        pltpu.SemaphoreType.DMA((2,)),
                        pltpu.SemaphoreType.DMA(())],
    )
    return pl.pallas_call(
        functools.partial(_expert_body, bm=MOE_BM, n_f=n_f, n_blocks=n_slots // MOE_BM),
        grid_spec=grid_spec,
        out_shape=jax.ShapeDtypeStruct((n_slots, d), F32),
        compiler_params=_cparams("arbitrary", "arbitrary"),
        name="moe_experts",
    )(grp_e, grp_blk0, grp_nb, n_used, slot_tok, h2, w_gate, w_up, w_down)


def _combine_body(pos_ref, x_ref, gate_ref, ys_hbm, o_ref, buf_ref, sem, *, tm):
    i = pl.program_id(0)
    n_steps = pl.num_programs(0)

    def issue(step):
        slot = step % 2

        def body(r0, _):
            for u in range(DMA_ISSUE_UNROLL):
                r = r0 * DMA_ISSUE_UNROLL + u
                for k in range(MOE_TOP_K):
                    src = pos_ref[(step * tm + r) * MOE_TOP_K + k]
                    pltpu.make_async_copy(ys_hbm.at[pl.ds(src, 1)], buf_ref.at[slot, k, pl.ds(r, 1)],
                                          sem.at[slot]).start()
            return 0

        lax.fori_loop(0, tm // DMA_ISSUE_UNROLL, body, 0)

    @pl.when(i == 0)
    def _():
        issue(i)

    @pl.when(i + 1 < n_steps)
    def _():
        issue(i + 1)

    slot = i % 2
    for k in range(MOE_TOP_K):
        pltpu.make_async_copy(ys_hbm.at[pl.ds(0, tm)], buf_ref.at[slot, k], sem.at[slot]).wait()
    o_ref[...] = x_ref[...] + (buf_ref[slot, 0] * gate_ref[:, 0:1] + buf_ref[slot, 1] * gate_ref[:, 1:2])


def _combine(x, gate, ys, pos, tm):
    m, d = x.shape
    grid_spec = pltpu.PrefetchScalarGridSpec(
        num_scalar_prefetch=1,
        grid=(m // tm,),
        in_specs=[pl.BlockSpec((tm, d), lambda i, pos: (i, 0)),
                  pl.BlockSpec((tm, LANES), lambda i, pos: (i, 0)),
                  pl.BlockSpec(memory_space=pl.ANY)],
        out_specs=pl.BlockSpec((tm, d), lambda i, pos: (i, 0)),
        scratch_shapes=[pltpu.VMEM((2, MOE_TOP_K, tm, d), F32), pltpu.SemaphoreType.DMA((2,))],
    )
    return pl.pallas_call(
        functools.partial(_combine_body, tm=tm),
        grid_spec=grid_spec,
        out_shape=jax.ShapeDtypeStruct((m, d), F32),
        compiler_params=_cparams("arbitrary"),
        name="moe_combine",
    )(pos, x, gate, ys)


def _moe_plan(eid, n_tokens):
    a = n_tokens * MOE_TOP_K
    eid = eid.reshape(a)
    n_blocks = (a + N_EXPERTS * (MOE_BM - 1)) // MOE_BM
    n_slots = n_blocks * MOE_BM
    n_groups = n_blocks // MOE_GROUP_BLOCKS + N_EXPERTS
    onehot = (eid[:, None] == jnp.arange(N_EXPERTS, dtype=jnp.int32)[None, :]).astype(jnp.int32)
    rank = jnp.take_along_axis(jnp.cumsum(onehot, axis=0), eid[:, None], axis=1)[:, 0] - 1
    counts = jnp.sum(onehot, axis=0)
    blocks_e = (counts + MOE_BM - 1) // MOE_BM
    blk_start = jnp.cumsum(blocks_e) - blocks_e
    pos = (blk_start[eid] * MOE_BM + rank).astype(jnp.int32)
    tok = jnp.arange(a, dtype=jnp.int32) // MOE_TOP_K
    slot_tok = jnp.zeros((n_slots,), jnp.int32).at[pos].set(tok)
    groups_e = (blocks_e + MOE_GROUP_BLOCKS - 1) // MOE_GROUP_BLOCKS
    grp_end = jnp.cumsum(groups_e)
    grp_start = grp_end - groups_e
    gidx = jnp.arange(n_groups, dtype=jnp.int32)
    ge = jnp.minimum(jnp.searchsorted(grp_end, gidx, side="right"), N_EXPERTS - 1).astype(jnp.int32)
    local = gidx - grp_start[ge]
    active = gidx < grp_end[-1]
    gb0 = jnp.where(active, blk_start[ge] + local * MOE_GROUP_BLOCKS, 0).astype(jnp.int32)
    gnb = jnp.where(active, jnp.minimum(blocks_e[ge] - local * MOE_GROUP_BLOCKS, MOE_GROUP_BLOCKS),
                    0).astype(jnp.int32)
    last_e = ge[jnp.maximum(grp_end[-1] - 1, 0)]
    ge = jnp.where(active, ge, last_e).astype(jnp.int32)
    n_used = jnp.sum(blocks_e).astype(jnp.int32).reshape(1)
    return pos, slot_tok, ge, gb0, gnb, n_used, n_slots


def _layer(x, mem, positions, attn_norm_g, mem_norm_g, w_in, fox_forget_b, fox_q_norm_g,
           fox_k_norm_g, dsa_q_norm_g, dsa_k_norm_g, idx_k_norm_g, cross_q_norm_g, cross_k_norm_g,
           w_mem_kv, w_branch_fox, w_branch_dsa, w_branch_cross, w_out, ffn_norm_g,
           router_group_w, router_group_b, router_expert_w, router_expert_b,
           expert_w_gate, expert_w_up, expert_w_down):
    b, s, d = x.shape
    n = b * s
    n_mem = mem.shape[1]
    fw, dw, cw = FOX_HEADS * HEAD_DIM, DSA_HEADS * HEAD_DIM, CROSS_HEADS * HEAD_DIM
    iqw = IDX_HEADS * IDX_DIM
    tm = min(512, n)
    x2d = x.reshape(n, d)

    sizes = (fw, fw, fw, FOX_HEADS, dw, dw, dw, iqw, IDX_HEADS, IDX_DIM, cw, 3 * d)
    offs = [0]
    for sz in sizes:
        offs.append(offs[-1] + sz)
    (c_fq, c_fk, c_fv, c_fl, c_dq, c_dk, c_dv, c_iq, c_iw, c_ik, c_cq, c_g) = range(12)

    w_t = jnp.transpose(w_in, (2, 0, 1)).reshape(offs[-1], d)
    order = (c_dq, c_dk, c_fv, c_dv, c_fq, c_fk, c_cq, c_iq, c_g)
    row0, r = {}, 0
    for c in order:
        row0[c] = r
        r += sizes[c]
    w_all = jnp.concatenate([w_t[offs[c]:offs[c + 1]] for c in order], axis=0).astype(BF16)
    small_rows = [w_t[offs[c]:offs[c + 1]] for c in (c_ik, c_iw, c_fl)]
    small_rows.append(jnp.zeros((LANES - IDX_DIM - IDX_HEADS - FOX_HEADS, d), F32))
    w_small = jnp.concatenate(small_rows, axis=0).astype(BF16)
    q_unit = (HEAD_DIM ** -0.5) * LOG2E
    g_qk1 = jnp.concatenate([jnp.tile(fox_q_norm_g * q_unit, FOX_HEADS),
                             jnp.tile(fox_k_norm_g, FOX_HEADS),
                             jnp.tile(cross_q_norm_g * q_unit, CROSS_HEADS)])
    g_qk2 = jnp.concatenate([jnp.tile(dsa_q_norm_g * q_unit, DSA_HEADS),
                             jnp.tile(dsa_k_norm_g, DSA_HEADS)])
    g_ik = jnp.concatenate([idx_k_norm_g, jnp.zeros((LANES - IDX_DIM,), F32)])
    fb_lane = jnp.zeros((LANES,), F32).at[SMALL_F:SMALL_F + FOX_HEADS].set(fox_forget_b)

    pos_col = positions.reshape(n, 1).astype(F32)
    cd, sd = _rope_tables(pos_col, *_rope_lane_patterns(ROT_DIM, HEAD_DIM), tm=tm)
    ci, si = _rope_tables(pos_col, *_rope_lane_patterns(IDX_ROT_DIM, IDX_DIM), tm=tm)

    h = _rmsnorm(x2d, attn_norm_g, BF16, tm)
    qk1 = _proj(h, w_all, "norm", BF16, tm, 1024, gains=g_qk1, w_rows=(row0[c_fq], 2 * fw + cw))
    qk2 = _proj(h, w_all, "norm_rope", BF16, tm, 768, gains=g_qk2, tables=(cd, sd),
                w_rows=(row0[c_dq], 2 * dw))
    vv = _proj(h, w_all, "raw", BF16, tm, 768, w_rows=(row0[c_fv], fw + dw))
    iq = _proj(h, w_all, "rope_idx", BF16, tm, 1024, tables=(ci, si), w_rows=(row0[c_iq], iqw))
    small = _proj(h, w_small, "raw", F32, tm, LANES, w_rows=(0, LANES))
    ika, ikb, wl, lf_col = _small(small, g_ik, fb_lane, ci, si, tm)

    tq, tk = min(ATT_TQ, s), min(ATT_TK, s)
    assert tq == tk and s % tq == 0
    cum_col = _forget_cumsum(lf_col.reshape(b, s, LANES))
    qk1 = qk1.reshape(b, s, 2 * fw + cw)
    qk2 = qk2.reshape(b, s, 2 * dw)
    q_aug, k_aug = _fox_pack(qk1, cum_col, FOX_HEADS, min(512, s))
    vt = jnp.transpose(vv.reshape(b, s // tk, tk, fw + dw), (0, 1, 3, 2))
    o_fox = _attention(q_aug, 0, k_aug, 0, vt, 0, (), "fox", FOX_HEADS, AUG_DIM, tq, tk)

    topk = min(INDEX_TOPK, s // 4)
    wt = jnp.transpose(wl.reshape(b, s, LANES)[:, :, :IDX_HEADS], (0, 2, 1))
    bias = _select_bias(iq.reshape(b, s, iqw), ika.reshape(b, s, LANES), ikb.reshape(b, s, LANES),
                        wt, topk, tq, tk)
    o_dsa = _attention(qk2, 0, qk2, 1, vt, 1, (bias,), "dsa", DSA_HEADS, HEAD_DIM, tq, tk)

    tmm = min(512, b * n_mem)
    m_n = _rmsnorm(mem.reshape(b * n_mem, d), mem_norm_g, BF16, tmm)
    w_kv = w_mem_kv.astype(BF16)
    mk = _proj(m_n, w_kv[:, :cw], "norm", BF16, tmm, cw, gains=jnp.tile(cross_k_norm_g, CROSS_HEADS))
    mv = _proj(m_n, w_kv[:, cw:], "raw", BF16, tmm, cw)
    o_cross = _cross_attention(qk1, (2 * fw) // cw, mk.reshape(b, n_mem, cw), mv.reshape(b, n_mem, cw),
                               min(512, s))

    x_mid = _merge(h, o_fox.reshape(n, fw), o_dsa.reshape(n, dw), o_cross.reshape(n, cw), x2d,
                   w_all, row0[c_g], w_branch_fox.astype(BF16), w_branch_dsa.astype(BF16),
                   w_branch_cross.astype(BF16), w_out.astype(BF16), tm, MERGE_TN)

    w_router = jnp.concatenate(
        [router_group_w, router_expert_w, jnp.zeros((d, LANES - N_GROUPS - N_EXPERTS), F32)],
        axis=1).astype(BF16)
    b_router = jnp.concatenate(
        [router_group_b, router_expert_b, jnp.zeros((LANES - N_GROUPS - N_EXPERTS,), F32)])
    h2, eid, gate = _router(x_mid, ffn_norm_g, w_router, b_router, tm)
    pos, slot_tok, ge, gb0, gnb, n_used, n_slots = _moe_plan(eid[:, :MOE_TOP_K], n)
    ys = _experts(h2, slot_tok, ge, gb0, gnb, n_used, expert_w_gate, expert_w_up, expert_w_down,
                  n_slots)
    out = _combine(x_mid, gate, ys, pos, min(256, n))
    return out.reshape(b, s, d)


def kernel(x, mem, positions, attn_norm_g, mem_norm_g, w_in, fox_forget_b, fox_q_norm_g, fox_k_norm_g, dsa_q_norm_g, dsa_k_norm_g, idx_k_norm_g, cross_q_norm_g, cross_k_norm_g, w_mem_kv, w_branch_fox, w_branch_dsa, w_branch_cross, w_out, ffn_norm_g, router_group_w, router_group_b, router_expert_w, router_expert_b, expert_w_gate, expert_w_up, expert_w_down):
    depth = w_in.shape[0]
    for layer in range(depth):
        x = _layer(
            x, mem, positions, attn_norm_g[layer], mem_norm_g[layer], w_in[layer:layer + 1],
            fox_forget_b[layer], fox_q_norm_g[layer], fox_k_norm_g[layer],
            dsa_q_norm_g[layer], dsa_k_norm_g[layer], idx_k_norm_g[layer],
            cross_q_norm_g[layer], cross_k_norm_g[layer], w_mem_kv[layer],
            w_branch_fox[layer], w_branch_dsa[layer], w_branch_cross[layer], w_out[layer],
            ffn_norm_g[layer], router_group_w[layer], router_group_b[layer],
            router_expert_w[layer], router_expert_b[layer], expert_w_gate[layer],
            expert_w_up[layer], expert_w_down[layer])
    return x
```

```python
import functools

import jax
import jax.numpy as jnp
from jax import lax
from jax.experimental import pallas as pl
from jax.experimental.pallas import tpu as pltpu

F32 = jnp.float32
BF16 = jnp.bfloat16

LANES = 128
HEAD_DIM = 128
FOX_HEADS = 6
DSA_HEADS = 6
CROSS_HEADS = 4
IDX_HEADS = 16
IDX_DIM = 64
ROPE_THETA = 500000.0
ROT_DIM = HEAD_DIM // 4
IDX_ROT_DIM = IDX_DIM // 4
INDEX_TOPK = 256
N_GROUPS = 4
EXPERTS_PER_GROUP = 8
N_EXPERTS = N_GROUPS * EXPERTS_PER_GROUP
MOE_TOP_K = 2
EPS = 1e-6

NEG = -1e30
INT_MIN = -(2 ** 31)
VMEM_LIMIT = 56 * 1024 * 1024

ATT_TQ = 256
ATT_TK = 256
ATT_HEADS_PER_LOOP = 6
MERGE_TN = 512
MERGE_SUB = 256
MOE_BM = 256
MOE_GROUP_BLOCKS = 4
MOE_FC = 512
DMA_ISSUE_UNROLL = 8


def _cparams(*sem):
    return pltpu.CompilerParams(dimension_semantics=sem, vmem_limit_bytes=VMEM_LIMIT)


def _rmsnorm_body(x_ref, g_ref, o_ref):
    x = x_ref[...]
    r = lax.rsqrt(jnp.mean(x * x, axis=-1, keepdims=True) + EPS)
    o_ref[...] = ((x * r) * g_ref[...]).astype(o_ref.dtype)


def _rmsnorm(x2d, g, out_dtype, tm):
    m, d = x2d.shape
    return pl.pallas_call(
        _rmsnorm_body,
        grid=(m // tm,),
        in_specs=[pl.BlockSpec((tm, d), lambda i: (i, 0)),
                  pl.BlockSpec((1, d), lambda i: (0, 0))],
        out_specs=pl.BlockSpec((tm, d), lambda i: (i, 0)),
        out_shape=jax.ShapeDtypeStruct((m, d), out_dtype),
        compiler_params=_cparams("parallel"),
        name="rmsnorm",
    )(x2d, g.reshape(1, d))


W_PREP_ROWS = 128


def _w_prep_body(src_ref, w_ref, o_ref):
    del src_ref
    o_ref[...] = w_ref[:, 0, :].astype(o_ref.dtype)


def _w_prep(w_t, src_rows):
    _, _, k = w_t.shape
    n_chunks = src_rows.shape[0]
    grid_spec = pltpu.PrefetchScalarGridSpec(
        num_scalar_prefetch=1,
        grid=(n_chunks,),
        in_specs=[pl.BlockSpec((pl.Element(W_PREP_ROWS), pl.Element(1), pl.Element(k)),
                               lambda c, src: (src[c], 0, 0))],
        out_specs=pl.BlockSpec((W_PREP_ROWS, k), lambda c, src: (c, 0)),
    )
    return pl.pallas_call(
        _w_prep_body,
        grid_spec=grid_spec,
        out_shape=jax.ShapeDtypeStruct((n_chunks * W_PREP_ROWS, k), BF16),
        compiler_params=_cparams("parallel"),
        name="w_prep",
    )(src_rows, w_t)


def _rope_table_body(pos_ref, f_ref, sgn_ref, c_ref, s_ref):
    ang = pos_ref[...] * f_ref[...]
    c_ref[...] = jnp.cos(ang)
    s_ref[...] = jnp.sin(ang) * sgn_ref[...]


def _rope_tables(pos_col, freq_lane, sign_lane, tm):
    n = pos_col.shape[0]
    row = pl.BlockSpec((1, LANES), lambda i: (0, 0))
    tab = pl.BlockSpec((tm, LANES), lambda i: (i, 0))
    return pl.pallas_call(
        _rope_table_body,
        grid=(n // tm,),
        in_specs=[pl.BlockSpec((tm, 1), lambda i: (i, 0)), row, row],
        out_specs=[tab, tab],
        out_shape=[jax.ShapeDtypeStruct((n, LANES), F32)] * 2,
        compiler_params=_cparams("parallel"),
        name="rope_tables",
    )(pos_col, freq_lane.reshape(1, LANES), sign_lane.reshape(1, LANES))


def _rope_lane_patterns(rot_dim, period):
    half = rot_dim // 2
    inv_freq = jnp.power(jnp.float32(ROPE_THETA), -jnp.arange(half, dtype=F32) * 2.0 / rot_dim)
    lane = jnp.arange(LANES) % period
    freq = jnp.where(lane < rot_dim, inv_freq[lane % half], 0.0).astype(F32)
    sign = jnp.where(lane < half, -1.0, jnp.where(lane < rot_dim, 1.0, 0.0)).astype(F32)
    return freq, sign


def _apply_rope(a, c, s, half, period):
    lane = lax.broadcasted_iota(jnp.int32, a.shape, 1)
    first = (lane & (period - 1)) < half
    partner = jnp.where(first, pltpu.roll(a, LANES - half, 1), pltpu.roll(a, half, 1))
    return a * c + partner * s


def _proj_body(*refs, mode, n_lane_groups, w_rows):
    h_ref, w_ref = refs[0], refs[1]
    o_ref = refs[-1]
    if w_rows:
        acc = lax.dot_general(h_ref[...], w_ref[...], (((1,), (1,)), ((), ())),
                              preferred_element_type=F32)
    else:
        acc = jnp.dot(h_ref[...], w_ref[...], preferred_element_type=F32)
    if mode == "raw":
        o_ref[...] = acc.astype(o_ref.dtype)
        return
    for g in range(n_lane_groups):
        sl = slice(g * LANES, (g + 1) * LANES)
        a = acc[:, sl]
        if mode in ("norm", "norm_rope"):
            g_ref = refs[2]
            a = (a * lax.rsqrt(jnp.mean(a * a, axis=-1, keepdims=True) + EPS)) * g_ref[:, sl]
        if mode == "norm_rope":
            a = _apply_rope(a, refs[3][...], refs[4][...], ROT_DIM // 2, HEAD_DIM)
        if mode == "rope_idx":
            a = _apply_rope(a, refs[2][...], refs[3][...], IDX_ROT_DIM // 2, IDX_DIM)
        o_ref[:, sl] = a.astype(o_ref.dtype)


def _proj(h, w, mode, out_dtype, tm, tn, gains=None, tables=None, w_rows=None):
    m, k = h.shape
    if w_rows is None:
        n = w.shape[1]
        w_spec = pl.BlockSpec((k, tn), lambda j, i: (0, j))
    else:
        row0, n = w_rows
        assert row0 % tn == 0 and n % tn == 0
        w_spec = pl.BlockSpec((tn, k), lambda j, i: (row0 // tn + j, 0))
    in_specs = [pl.BlockSpec((tm, k), lambda j, i: (i, 0)), w_spec]
    args = [h, w]
    if gains is not None:
        in_specs.append(pl.BlockSpec((1, tn), lambda j, i: (0, j)))
        args.append(gains.reshape(1, n))
    if tables is not None:
        for t in tables:
            in_specs.append(pl.BlockSpec((tm, LANES), lambda j, i: (i, 0)))
            args.append(t)
    return pl.pallas_call(
        functools.partial(_proj_body, mode=mode, n_lane_groups=tn // LANES,
                          w_rows=w_rows is not None),
        grid=(n // tn, m // tm),
        in_specs=in_specs,
        out_specs=pl.BlockSpec((tm, tn), lambda j, i: (i, j)),
        out_shape=jax.ShapeDtypeStruct((m, n), out_dtype),
        compiler_params=_cparams("parallel", "parallel"),
        name="proj_" + mode,
    )(*args)


SMALL_IK = 0
SMALL_IW = 64
SMALL_F = 80


def _small_body(x_ref, gk_ref, fb_ref, c_ref, s_ref, ika_ref, ikb_ref, wl_ref, lf_ref):
    x = x_ref[...]
    lane = lax.broadcasted_iota(jnp.int32, x.shape, 1)
    is_ik = lane < IDX_DIM
    ik = jnp.where(is_ik, x, 0.0)
    ms = jnp.sum(ik * ik, axis=-1, keepdims=True) * (1.0 / IDX_DIM)
    ik = (ik * lax.rsqrt(ms + EPS)) * gk_ref[...]
    ik = _apply_rope(ik, c_ref[...], s_ref[...], IDX_ROT_DIM // 2, IDX_DIM)
    ik = jnp.where(is_ik, ik, 0.0)
    ika_ref[...] = ik.astype(BF16)
    ikb_ref[...] = pltpu.roll(ik, IDX_DIM, 1).astype(BF16)
    wl_ref[...] = pltpu.roll(x, LANES - SMALL_IW, 1) * (2.0 ** -5)
    z = x + fb_ref[...]
    log_f = jnp.minimum(z, 0.0) - jnp.log1p(jnp.exp(-jnp.abs(z)))
    lf_ref[...] = pltpu.roll(log_f, LANES - SMALL_F, 1)


def _small(x, idx_k_gain_lane, forget_b_lane, ci, si, tm):
    n = x.shape[0]
    row = pl.BlockSpec((1, LANES), lambda i: (0, 0))
    tab = pl.BlockSpec((tm, LANES), lambda i: (i, 0))
    return pl.pallas_call(
        _small_body,
        grid=(n // tm,),
        in_specs=[tab, row, row, tab, tab],
        out_specs=[tab, tab, tab, tab],
        out_shape=[jax.ShapeDtypeStruct((n, LANES), BF16), jax.ShapeDtypeStruct((n, LANES), BF16),
                   jax.ShapeDtypeStruct((n, LANES), F32), jax.ShapeDtypeStruct((n, LANES), F32)],
        compiler_params=_cparams("parallel"),
        name="small_cols",
    )(x, idx_k_gain_lane.reshape(1, LANES), forget_b_lane.reshape(1, LANES), ci, si)


def _cumsum_body(col_ref, ccol_ref, *, seq):
    c = col_ref[...]
    ci = lax.broadcasted_iota(jnp.int32, c.shape, 0)
    sh = 1
    while sh < seq:
        c = c + jnp.where(ci >= sh, pltpu.roll(c, sh, 0), 0.0)
        sh *= 2
    ccol_ref[...] = c


def _forget_cumsum(lf_col):
    b, s, _ = lf_col.shape
    return pl.pallas_call(
        functools.partial(_cumsum_body, seq=s),
        grid=(b,),
        in_specs=[pl.BlockSpec((None, s, LANES), lambda i: (i, 0, 0))],
        out_specs=pl.BlockSpec((None, s, LANES), lambda i: (i, 0, 0)),
        out_shape=jax.ShapeDtypeStruct((b, s, LANES), F32),
        compiler_params=_cparams("parallel"),
        name="forget_cumsum",
    )(lf_col)


AUG_DIM = 2 * HEAD_DIM
LOG2E = 1.4426950408889634


def _split3(x):
    hi = x.astype(BF16).astype(F32)
    r1 = x - hi
    mid = r1.astype(BF16).astype(F32)
    lo = (r1 - mid).astype(BF16).astype(F32)
    return hi, mid, lo


def _fox_pack_body(q_ref, k_ref, cum_ref, qa_ref, ka_ref, *, n_heads):
    rows = q_ref.shape[0]
    lane = lax.broadcasted_iota(jnp.int32, (rows, LANES), 1)
    for h in range(n_heads):
        c = jnp.broadcast_to(cum_ref[:, h:h + 1], (rows, LANES)) * LOG2E
        qh, qm, ql = _split3(c)
        kh, km, kl = _split3(-c)
        q_aug = jnp.where(lane == 0, qh, jnp.where(lane == 1, qm, jnp.where(
            lane == 2, ql, jnp.where(lane < 6, 1.0, 0.0))))
        k_aug = jnp.where(lane < 3, 1.0, jnp.where(lane == 3, kh, jnp.where(
            lane == 4, km, jnp.where(lane == 5, kl, 0.0))))
        base = h * AUG_DIM
        qa_ref[:, base:base + HEAD_DIM] = q_ref[:, h * HEAD_DIM:(h + 1) * HEAD_DIM]
        ka_ref[:, base:base + HEAD_DIM] = k_ref[:, h * HEAD_DIM:(h + 1) * HEAD_DIM]
        qa_ref[:, base + HEAD_DIM:base + AUG_DIM] = q_aug.astype(BF16)
        ka_ref[:, base + HEAD_DIM:base + AUG_DIM] = k_aug.astype(BF16)


def _fox_pack(qk, cum_col, n_heads, ts):
    b, s, _ = qk.shape
    w = n_heads * HEAD_DIM
    wa = n_heads * AUG_DIM
    return pl.pallas_call(
        functools.partial(_fox_pack_body, n_heads=n_heads),
        grid=(b, s // ts),
        in_specs=[pl.BlockSpec((None, ts, w), lambda bi, i: (bi, i, 0)),
                  pl.BlockSpec((None, ts, w), lambda bi, i: (bi, i, 1)),
                  pl.BlockSpec((None, ts, LANES), lambda bi, i: (bi, i, 0))],
        out_specs=[pl.BlockSpec((None, ts, wa), lambda bi, i: (bi, i, 0)),
                   pl.BlockSpec((None, ts, wa), lambda bi, i: (bi, i, 0))],
        out_shape=[jax.ShapeDtypeStruct((b, s, wa), BF16)] * 2,
        compiler_params=_cparams("parallel", "parallel"),
        name="fox_pack",
    )(qk, qk, cum_col)


def _attn_body(q_ref, k_ref, vt_ref, *rest, tq, tk, n_heads, kdim, mode, heads_per_loop):
    o_ref = rest[-1]
    qi = pl.program_id(1)
    key0 = lax.broadcasted_iota(jnp.int32, (tk, tq), 0)
    qry = qi * tq + lax.broadcasted_iota(jnp.int32, (tk, tq), 1)
    def scores(h, j):
        off = pl.multiple_of(j * tk, tk)
        q = q_ref[:, h * kdim:(h + 1) * kdim]
        k = k_ref[pl.ds(off, tk), h * kdim:(h + 1) * kdim]
        return lax.dot_general(k, q, (((1,), (1,)), ((), ())), preferred_element_type=F32)

    def flush(h, j, p, alpha, acc):
        vt = vt_ref[j, h * HEAD_DIM:(h + 1) * HEAD_DIM, :]
        return alpha * acc + jnp.dot(vt, p, preferred_element_type=F32)

    def softmax_step(h, j, m, l, masked):
        s = scores(h, j)
        if mode == "dsa":
            s = s + rest[0][j]
        if masked:
            s = jnp.where(key0 + j * tk <= qry, s, NEG)
        m_new = jnp.maximum(m, jnp.max(s, axis=0, keepdims=True))
        alpha = jnp.exp2(m - m_new)
        p = jnp.exp2(s - m_new)
        l = alpha * l + jnp.sum(p, axis=0, keepdims=True)
        return p.astype(BF16), alpha, m_new, l

    last = qi
    for h0 in range(0, n_heads, heads_per_loop):
        heads = range(h0, h0 + heads_per_loop)

        def step(j, carries, masked=False, heads=heads):
            prev = jnp.maximum(j - 1, 0)
            out = []
            for h, (p, alpha, m, l, acc) in zip(heads, carries):
                acc = flush(h, prev, p, alpha, acc)
                p, alpha, m, l = softmax_step(h, j, m, l, masked)
                out.append((p, alpha, m, l, acc))
            return tuple(out)

        carries = tuple(
            (jnp.zeros((tk, tq), BF16), jnp.ones((1, tq), F32), jnp.full((1, tq), NEG, F32),
             jnp.zeros((1, tq), F32), jnp.zeros((HEAD_DIM, tq), F32)) for h in heads)
        n_loop = last if mode == "fox" else last + 1
        carries = lax.fori_loop(0, n_loop, step, carries)
        if mode == "fox":
            carries = step(last, carries, masked=True)
        for h, (p, alpha, m, l, acc) in zip(heads, carries):
            acc = flush(h, last, p, alpha, acc)
            o_ref[:, h * HEAD_DIM:(h + 1) * HEAD_DIM] = jnp.transpose(acc / l).astype(o_ref.dtype)


def _attention(q_arr, q_blk, k_arr, k_blk, vt_arr, v_blk, extras, mode, n_heads, kdim, tq, tk):
    b, s, _ = q_arr.shape
    w = n_heads * HEAD_DIM
    wk = n_heads * kdim
    in_specs = [pl.BlockSpec((None, tq, wk), lambda bi, qi: (bi, qi, q_blk)),
                pl.BlockSpec((None, s, wk), lambda bi, qi: (bi, 0, k_blk)),
                pl.BlockSpec((None, s // tk, w, tk), lambda bi, qi: (bi, 0, v_blk, 0))]
    if mode == "dsa":
        in_specs += [pl.BlockSpec((None, None, s // tk, tk, tq), lambda bi, qi: (bi, qi, 0, 0, 0))]
    return pl.pallas_call(
        functools.partial(_attn_body, tq=tq, tk=tk, n_heads=n_heads, kdim=kdim, mode=mode,
                          heads_per_loop=ATT_HEADS_PER_LOOP),
        grid=(b, s // tq),
        in_specs=in_specs,
        out_specs=pl.BlockSpec((None, tq, w), lambda bi, qi: (bi, qi, 0)),
        out_shape=jax.ShapeDtypeStruct((b, s, w), BF16),
        compiler_params=_cparams("parallel", "parallel"),
        name="attn_" + mode,
    )(q_arr, k_arr, vt_arr, *extras)


def _select_body(iq_ref, ika_ref, ikb_ref, wt_ref, o_ref, keys_ref, half_ref, *, tq, tk, n_chunks, topk):
    qi = pl.program_id(1)
    used = qi + 1
    key0 = lax.broadcasted_iota(jnp.int32, (tk, tq), 0)
    qry = qi * tq + lax.broadcasted_iota(jnp.int32, (tk, tq), 1)

    def score_chunk(c, _):
        off = pl.multiple_of(c * tk, tk)
        ka = ika_ref[pl.ds(off, tk), :]
        kb = ikb_ref[pl.ds(off, tk), :]
        acc = jnp.zeros((tk, tq), F32)
        for j in range(IDX_HEADS):
            pair = iq_ref[:, (j // 2) * LANES:(j // 2 + 1) * LANES]
            logit = lax.dot_general(ka if j % 2 == 0 else kb, pair, (((1,), (1,)), ((), ())),
                                    preferred_element_type=F32)
            acc = acc + wt_ref[j:j + 1, :] * jnp.maximum(logit, 0.0)
        bits = lax.bitcast_convert_type(acc, jnp.int32)
        key = jnp.where(bits < 0, bits ^ jnp.int32(0x7FFFFFFF), bits)
        key = jnp.where(key0 + off <= qry, key, jnp.int32(INT_MIN))
        keys_ref[c] = key
        half_ref[c] = lax.shift_right_arithmetic(key, 16).astype(jnp.int16)
        return 0

    lax.fori_loop(0, used, score_chunk, 0)

    one16 = jnp.int16(1)
    zero16 = jnp.int16(0)

    def count_ge(cand16):
        def count_chunk(c, cnt):
            ge = jnp.where(half_ref[c] >= cand16, one16, zero16)
            parts = [ge[r:r + 16] for r in range(0, tk, 16)]
            while len(parts) > 1:
                parts = [a + b for a, b in zip(parts[::2], parts[1::2])]
            return cnt + parts[0]

        cnt = lax.fori_loop(0, used, count_chunk, jnp.zeros((16, tq), jnp.int16))
        return jnp.sum(cnt.astype(jnp.int32), axis=0, keepdims=True)

    def search16(need):
        def bit_step(i, prefix):
            cand_u = prefix | jnp.left_shift(jnp.int32(1), 15 - i)
            cnt = count_ge((cand_u - 32768).astype(jnp.int16))
            return jnp.where(cnt >= need, cand_u, prefix)

        return lax.fori_loop(0, 16, bit_step, jnp.zeros((1, tq), jnp.int32)) - 32768

    hi = search16(topk)
    above = count_ge(jnp.minimum(hi + 1, 32767).astype(jnp.int16))
    above = jnp.where(hi >= 32767, 0, above)
    hi16 = hi.astype(jnp.int16)

    def low_chunk(c, _):
        low = (keys_ref[c] & jnp.int32(0xFFFF)) - 32768
        half_ref[c] = jnp.where(half_ref[c] == hi16, low.astype(jnp.int16), jnp.int16(-32768))
        return 0

    lax.fori_loop(0, used, low_chunk, 0)
    lo = search16(topk - above)
    thr = lax.shift_left(hi, 16) | (lo + 32768)
    thr = jnp.maximum(thr, jnp.int32(INT_MIN + 1))

    def write_chunk(c, _):
        o_ref[c] = jnp.where(keys_ref[c] >= thr, 0.0, NEG)
        return 0

    def fill_chunk(c, _):
        o_ref[c] = jnp.full((tk, tq), NEG, F32)
        return 0

    lax.fori_loop(0, used, write_chunk, 0)
    lax.fori_loop(used, n_chunks, fill_chunk, 0)


def _select_bias(iq, ika, ikb, wt, topk, tq, tk):
    b, s, _ = iq.shape
    n_chunks = s // tk
    return pl.pallas_call(
        functools.partial(_select_body, tq=tq, tk=tk, n_chunks=n_chunks, topk=topk),
        grid=(b, s // tq),
        in_specs=[pl.BlockSpec((None, tq, IDX_HEADS * IDX_DIM), lambda bi, qi: (bi, qi, 0)),
                  pl.BlockSpec((None, s, LANES), lambda bi, qi: (bi, 0, 0)),
                  pl.BlockSpec((None, s, LANES), lambda bi, qi: (bi, 0, 0)),
                  pl.BlockSpec((None, IDX_HEADS, tq), lambda bi, qi: (bi, 0, qi))],
        out_specs=pl.BlockSpec((None, None, n_chunks, tk, tq), lambda bi, qi: (bi, qi, 0, 0, 0)),
        out_shape=jax.ShapeDtypeStruct((b, s // tq, n_chunks, tk, tq), F32),
        scratch_shapes=[pltpu.VMEM((n_chunks, tk, tq), jnp.int32),
                        pltpu.VMEM((n_chunks, tk, tq), jnp.int16)],
        compiler_params=_cparams("parallel", "parallel"),
        name="index_select",
    )(iq, ika, ikb, wt)


def _cross_body(q_ref, k_ref, v_ref, o_ref, *, n_heads):
    for h in range(n_heads):
        hs = slice(h * HEAD_DIM, (h + 1) * HEAD_DIM)
        s = lax.dot_general(q_ref[:, hs], k_ref[:, hs], (((1,), (1,)), ((), ())),
                            preferred_element_type=F32)
        p = jnp.exp2(s - jnp.max(s, axis=-1, keepdims=True))
        l = jnp.sum(p, axis=-1, keepdims=True)
        o = jnp.dot(p.astype(BF16), v_ref[:, hs], preferred_element_type=F32)
        o_ref[:, hs] = (o / l).astype(o_ref.dtype)


def _cross_attention(q_arr, q_blk, mk, mv, tq):
    b, s, _ = q_arr.shape
    n_mem = mk.shape[1]
    w = CROSS_HEADS * HEAD_DIM
    return pl.pallas_call(
        functools.partial(_cross_body, n_heads=CROSS_HEADS),
        grid=(b, s // tq),
        in_specs=[pl.BlockSpec((None, tq, w), lambda bi, qi: (bi, qi, q_blk)),
                  pl.BlockSpec((None, n_mem, w), lambda bi, qi: (bi, 0, 0)),
                  pl.BlockSpec((None, n_mem, w), lambda bi, qi: (bi, 0, 0))],
        out_specs=pl.BlockSpec((None, tq, w), lambda bi, qi: (bi, qi, 0)),
        out_shape=jax.ShapeDtypeStruct((b, s, w), BF16),
        compiler_params=_cparams("parallel", "parallel"),
        name="cross_attn",
    )(q_arr, mk, mv)


def _sigmoid(x):
    return 1.0 / (1.0 + jnp.exp(-x))


def _merge_body(h_ref, of_ref, od_ref, oc_ref, x_ref, wgf_ref, wgd_ref, wgc_ref,
                wbf_ref, wbd_ref, wbc_ref, wo_ref, o_ref):
    j = pl.program_id(1)
    h = h_ref[...]
    tn = wo_ref.shape[0]
    sub = min(tn, MERGE_SUB)

    def gate_logits(wg_ref, cs):
        return lax.dot_general(h, wg_ref[cs, :], (((1,), (1,)), ((), ())), preferred_element_type=F32)

    part = None
    for c in range(tn // sub):
        cs = slice(c * sub, (c + 1) * sub)
        zf, zd, zc = gate_logits(wgf_ref, cs), gate_logits(wgd_ref, cs), gate_logits(wgc_ref, cs)
        bf = jnp.dot(of_ref[...], wbf_ref[:, cs], preferred_element_type=F32)
        bd = jnp.dot(od_ref[...], wbd_ref[:, cs], preferred_element_type=F32)
        bc = jnp.dot(oc_ref[...], wbc_ref[:, cs], preferred_element_type=F32)
        merged = (_sigmoid(zf) * bf + _sigmoid(zd) * bd) + _sigmoid(zc) * bc
        p = jnp.dot(merged.astype(BF16), wo_ref[cs, :], preferred_element_type=F32)
        part = p if part is None else part + p

    @pl.when(j == 0)
    def _():
        o_ref[...] = x_ref[...] + part

    @pl.when(j != 0)
    def _():
        o_ref[...] += part


def _merge(h, o_fox, o_dsa, o_cross, x, w_gates, gate_row0, wbf, wbd, wbc, w_out, tm, tn):
    m, d = x.shape
    nj = d // tn
    assert gate_row0 % tn == 0
    g0 = gate_row0 // tn
    row = lambda i, j: (i, 0)
    in_specs = [pl.BlockSpec((tm, d), row),
                pl.BlockSpec((tm, o_fox.shape[1]), row),
                pl.BlockSpec((tm, o_dsa.shape[1]), row),
                pl.BlockSpec((tm, o_cross.shape[1]), row),
                pl.BlockSpec((tm, d), row),
                pl.BlockSpec((tn, d), lambda i, j: (g0 + j, 0)),
                pl.BlockSpec((tn, d), lambda i, j: (g0 + nj + j, 0)),
                pl.BlockSpec((tn, d), lambda i, j: (g0 + 2 * nj + j, 0)),
                pl.BlockSpec((wbf.shape[0], tn), lambda i, j: (0, j)),
                pl.BlockSpec((wbd.shape[0], tn), lambda i, j: (0, j)),
                pl.BlockSpec((wbc.shape[0], tn), lambda i, j: (0, j)),
                pl.BlockSpec((tn, d), lambda i, j: (j, 0))]
    return pl.pallas_call(
        _merge_body,
        grid=(m // tm, nj),
        in_specs=in_specs,
        out_specs=pl.BlockSpec((tm, d), row),
        out_shape=jax.ShapeDtypeStruct((m, d), F32),
        compiler_params=_cparams("parallel", "arbitrary"),
        name="gated_merge",
    )(h, o_fox, o_dsa, o_cross, x, w_gates, w_gates, w_gates, wbf, wbd, wbc, w_out)


def _router_body(x_ref, g_ref, w_ref, b_ref, h_ref, eid_ref, gate_ref):
    x = x_ref[...]
    r = lax.rsqrt(jnp.mean(x * x, axis=-1, keepdims=True) + EPS)
    h = (x * r) * g_ref[...]
    hb = h.astype(BF16)
    bits = lax.bitcast_convert_type(hb.astype(F32), jnp.int32)
    half = bits.shape[1] // 2
    h_ref[...] = lax.shift_right_logical(bits[:, :half], 16) | (bits[:, half:] & jnp.int32(-65536))
    logits = jnp.dot(hb, w_ref[...], preferred_element_type=F32) + b_ref[...]
    lane = lax.broadcasted_iota(jnp.int32, logits.shape, 1).astype(F32)

    def masked_max(mask):
        return jnp.max(jnp.where(mask, logits, -jnp.inf), axis=-1, keepdims=True)

    def first_lane(mask):
        return jnp.min(jnp.where(mask, lane, float(LANES)), axis=-1, keepdims=True)

    is_group = lane < N_GROUPS
    gmax = masked_max(is_group)
    grp = first_lane(is_group & (logits == gmax))
    denom = jnp.sum(jnp.where(is_group, jnp.exp(logits - gmax), 0.0), axis=-1, keepdims=True)
    p_sel = 1.0 / denom
    lo = N_GROUPS + grp * EXPERTS_PER_GROUP
    in_grp = (lane >= lo) & (lane < lo + EXPERTS_PER_GROUP)
    e1 = masked_max(in_grp)
    j1 = first_lane(in_grp & (logits == e1))
    rest = in_grp & (lane != j1)
    e2 = masked_max(rest)
    j2 = first_lane(rest & (logits == e2))
    t = jnp.exp(e2 - e1)
    g1 = p_sel * (1.0 / (1.0 + t))
    g2 = p_sel * (t / (1.0 + t))
    eid = jnp.where(lane == 0.0, j1 - N_GROUPS, jnp.where(lane == 1.0, j2 - N_GROUPS, 0.0))
    eid_ref[...] = eid.astype(jnp.int32)
    gate_ref[...] = jnp.where(lane == 0.0, g1, jnp.where(lane == 1.0, g2, 0.0))


def _router(x, g, w_router, b_router, tm):
    m, d = x.shape
    return pl.pallas_call(
        _router_body,
        grid=(m // tm,),
        in_specs=[pl.BlockSpec((tm, d), lambda i: (i, 0)),
                  pl.BlockSpec((1, d), lambda i: (0, 0)),
                  pl.BlockSpec((d, LANES), lambda i: (0, 0)),
                  pl.BlockSpec((1, LANES), lambda i: (0, 0))],
        out_specs=[pl.BlockSpec((tm, d // 2), lambda i: (i, 0)),
                   pl.BlockSpec((tm, LANES), lambda i: (i, 0)),
                   pl.BlockSpec((tm, LANES), lambda i: (i, 0))],
        out_shape=[jax.ShapeDtypeStruct((m, d // 2), jnp.int32),
                   jax.ShapeDtypeStruct((m, LANES), jnp.int32),
                   jax.ShapeDtypeStruct((m, LANES), F32)],
        compiler_params=_cparams("parallel"),
        name="moe_router",
    )(x, g.reshape(1, d), w_router, b_router.reshape(1, LANES))


def _expert_body(ge_ref, gb0_ref, gnb_ref, used_ref, tok_ref, h_hbm, wg_ref, wu_ref, wd_ref, ys_hbm,
                 xs_ref, xb_ref, acc_ref, wgb_ref, wub_ref, wdb_ref, sem_in, sem_out,
                 *, bm, n_f, n_blocks):
    g = pl.program_id(0)
    f = pl.program_id(1)
    nb = gnb_ref[g]
    blk0 = gb0_ref[g]

    @pl.when(jnp.logical_and(g == 0, f == 0))
    def _():
        acc_ref[pl.ds(0, bm)] = jnp.zeros((bm,) + acc_ref.shape[1:], F32)

        def tail_copy(i):
            return pltpu.make_async_copy(acc_ref.at[pl.ds(0, bm)], ys_hbm.at[pl.ds(i * bm, bm)],
                                         sem_out)

        def start_tail(i, _):
            tail_copy(i).start()
            return 0

        lax.fori_loop(used_ref[0], n_blocks, start_tail, 0)

        def wait_tail(i, _):
            tail_copy(i).wait()
            return 0

        lax.fori_loop(used_ref[0], n_blocks, wait_tail, 0)

    def gather(grp):
        slot = grp % 2
        base = gb0_ref[grp] * bm

        def issue(r0, _):
            for u in range(DMA_ISSUE_UNROLL):
                r = r0 * DMA_ISSUE_UNROLL + u
                pltpu.make_async_copy(h_hbm.at[pl.ds(tok_ref[base + r], 1)],
                                      xs_ref.at[slot, pl.ds(r, 1)], sem_in.at[slot]).start()
            return 0

        lax.fori_loop(0, gnb_ref[grp] * (bm // DMA_ISSUE_UNROLL), issue, 0)

    def block_copy_out(i):
        return pltpu.make_async_copy(acc_ref.at[pl.ds(i * bm, bm)],
                                     ys_hbm.at[pl.ds((blk0 + i) * bm, bm)], sem_out)

    @pl.when(jnp.logical_and(g == 0, f == 0))
    def _():
        gather(g)

    @pl.when(jnp.logical_and(nb > 0, f == 0))
    def _():
        slot = g % 2

        def land(i, _):
            rows = pl.ds(pl.multiple_of(i * bm, bm), bm)
            pltpu.make_async_copy(h_hbm.at[pl.ds(0, bm)], xs_ref.at[slot, rows], sem_in.at[slot]).wait()
            return 0

        lax.fori_loop(0, nb, land, 0)

        half = xb_ref.shape[1] // 2

        def unpack(i, _):
            rows = pl.ds(pl.multiple_of(i * bm, bm), bm)
            w = xs_ref[slot, rows, :]
            lo = lax.bitcast_convert_type(lax.shift_left(w, 16), F32)
            hi = lax.bitcast_convert_type(w & jnp.int32(-65536), F32)
            xb_ref[rows, :half] = lo.astype(BF16)
            xb_ref[rows, half:] = hi.astype(BF16)
            return 0

        lax.fori_loop(0, nb, unpack, 0)

    @pl.when(jnp.logical_and(f == 0, g + 1 < pl.num_programs(0)))
    def _():
        gather(g + 1)

    def wait_out(n):
        def body(i, _):
            pltpu.make_async_copy(acc_ref.at[pl.ds(0, bm)], ys_hbm.at[pl.ds(0, bm)], sem_out).wait()
            return 0

        lax.fori_loop(0, n, body, 0)

    @pl.when(jnp.logical_and(f == 0, g > 0))
    def _():
        wait_out(gnb_ref[jnp.maximum(g - 1, 0)])

    @pl.when(nb > 0)
    def _():
        wgb_ref[...] = wg_ref[...].astype(BF16)
        wub_ref[...] = wu_ref[...].astype(BF16)
        wdb_ref[...] = wd_ref[...].astype(BF16)

        def block(i, _):
            rows = pl.ds(pl.multiple_of(i * bm, bm), bm)
            xb = xb_ref[rows, :]
            a = jnp.dot(xb, wgb_ref[...], preferred_element_type=F32)
            u = jnp.dot(xb, wub_ref[...], preferred_element_type=F32)
            hid = (a * _sigmoid(a)) * u
            y = jnp.dot(hid.astype(BF16), wdb_ref[...], preferred_element_type=F32)

            @pl.when(f == 0)
            def _():
                acc_ref[rows, :] = y

            @pl.when(f != 0)
            def _():
                acc_ref[rows, :] += y

            return 0

        lax.fori_loop(0, nb, block, 0)

    @pl.when(jnp.logical_and(nb > 0, f == n_f - 1))
    def _():
        def start_out(i, _):
            block_copy_out(i).start()
            return 0

        lax.fori_loop(0, nb, start_out, 0)

        @pl.when(g == pl.num_programs(0) - 1)
        def _():
            wait_out(nb)


def _experts(h2, slot_tok, grp_e, grp_blk0, grp_nb, n_used, w_gate, w_up, w_down, n_slots):
    d = w_gate.shape[1]
    ff = w_gate.shape[2]
    n_f = ff // MOE_FC
    n_groups = grp_e.shape[0]
    rmax = MOE_GROUP_BLOCKS * MOE_BM

    def f_idx(g, f, gnb):
        return jnp.where(gnb[g] > 0, f, n_f - 1)

    grid_spec = pltpu.PrefetchScalarGridSpec(
        num_scalar_prefetch=5,
        grid=(n_groups, n_f),
        in_specs=[
            pl.BlockSpec(memory_space=pl.ANY),
            pl.BlockSpec((None, d, MOE_FC), lambda g, f, ge, gb0, gnb, *_: (ge[g], 0, f_idx(g, f, gnb))),
            pl.BlockSpec((None, d, MOE_FC), lambda g, f, ge, gb0, gnb, *_: (ge[g], 0, f_idx(g, f, gnb))),
            pl.BlockSpec((None, MOE_FC, d), lambda g, f, ge, gb0, gnb, *_: (ge[g], f_idx(g, f, gnb), 0)),
        ],
        out_specs=pl.BlockSpec(memory_space=pl.ANY),
        scratch_shapes=[pltpu.VMEM((2, rmax, d // 2), jnp.int32),
                        pltpu.VMEM((rmax, d), BF16),
                        pltpu.VMEM((rmax, d), F32),
                        pltpu.VMEM((d, MOE_FC), BF16),
                        pltpu.VMEM((d, MOE_FC), BF16),
                        pltpu.VMEM((MOE_FC, d), BF16),
                        pltpu.SemaphoreType.DMA((2,)),
                        pltpu.SemaphoreType.DMA(())],
    )
    return pl.pallas_call(
        functools.partial(_expert_body, bm=MOE_BM, n_f=n_f, n_blocks=n_slots // MOE_BM),
        grid_spec=grid_spec,
        out_shape=jax.ShapeDtypeStruct((n_slots, d), F32),
        compiler_params=_cparams("arbitrary", "arbitrary"),
        name="moe_experts",
    )(grp_e, grp_blk0, grp_nb, n_used, slot_tok, h2, w_gate, w_up, w_down)


def _combine_body(pos_ref, x_ref, gate_ref, ys_hbm, o_ref, buf_ref, sem, *, tm):
    i = pl.program_id(0)
    n_steps = pl.num_programs(0)

    def issue(step):
        slot = step % 2

        def body(r0, _):
            for u in range(DMA_ISSUE_UNROLL):
                r = r0 * DMA_ISSUE_UNROLL + u
                for k in range(MOE_TOP_K):
                    src = pos_ref[(step * tm + r) * MOE_TOP_K + k]
                    pltpu.make_async_copy(ys_hbm.at[pl.ds(src, 1)], buf_ref.at[slot, k, pl.ds(r, 1)],
                                          sem.at[slot]).start()
            return 0

        lax.fori_loop(0, tm // DMA_ISSUE_UNROLL, body, 0)

    @pl.when(i == 0)
    def _():
        issue(i)

    @pl.when(i + 1 < n_steps)
    def _():
        issue(i + 1)

    slot = i % 2
    for k in range(MOE_TOP_K):
        pltpu.make_async_copy(ys_hbm.at[pl.ds(0, tm)], buf_ref.at[slot, k], sem.at[slot]).wait()
    o_ref[...] = x_ref[...] + (buf_ref[slot, 0] * gate_ref[:, 0:1] + buf_ref[slot, 1] * gate_ref[:, 1:2])


def _combine(x, gate, ys, pos, tm):
    m, d = x.shape
    grid_spec = pltpu.PrefetchScalarGridSpec(
        num_scalar_prefetch=1,
        grid=(m // tm,),
        in_specs=[pl.BlockSpec((tm, d), lambda i, pos: (i, 0)),
                  pl.BlockSpec((tm, LANES), lambda i, pos: (i, 0)),
                  pl.BlockSpec(memory_space=pl.ANY)],
        out_specs=pl.BlockSpec((tm, d), lambda i, pos: (i, 0)),
        scratch_shapes=[pltpu.VMEM((2, MOE_TOP_K, tm, d), F32), pltpu.SemaphoreType.DMA((2,))],
    )
    return pl.pallas_call(
        functools.partial(_combine_body, tm=tm),
        grid_spec=grid_spec,
        out_shape=jax.ShapeDtypeStruct((m, d), F32),
        compiler_params=_cparams("arbitrary"),
        name="moe_combine",
    )(pos, x, gate, ys)


def _moe_plan(eid, n_tokens):
    a = n_tokens * MOE_TOP_K
    eid = eid.reshape(a)
    n_blocks = (a + N_EXPERTS * (MOE_BM - 1)) // MOE_BM
    n_slots = n_blocks * MOE_BM
    n_groups = n_blocks // MOE_GROUP_BLOCKS + N_EXPERTS
    onehot = (eid[:, None] == jnp.arange(N_EXPERTS, dtype=jnp.int32)[None, :]).astype(jnp.int32)
    rank = jnp.take_along_axis(jnp.cumsum(onehot, axis=0), eid[:, None], axis=1)[:, 0] - 1
    counts = jnp.sum(onehot, axis=0)
    blocks_e = (counts + MOE_BM - 1) // MOE_BM
    blk_start = jnp.cumsum(blocks_e) - blocks_e
    pos = (blk_start[eid] * MOE_BM + rank).astype(jnp.int32)
    tok = jnp.arange(a, dtype=jnp.int32) // MOE_TOP_K
    slot_tok = jnp.zeros((n_slots,), jnp.int32).at[pos].set(tok, unique_indices=True)
    groups_e = (blocks_e + MOE_GROUP_BLOCKS - 1) // MOE_GROUP_BLOCKS
    grp_end = jnp.cumsum(groups_e)
    grp_start = grp_end - groups_e
    gidx = jnp.arange(n_groups, dtype=jnp.int32)
    ge = jnp.minimum(jnp.searchsorted(grp_end, gidx, side="right"), N_EXPERTS - 1).astype(jnp.int32)
    local = gidx - grp_start[ge]
    active = gidx < grp_end[-1]
    gb0 = jnp.where(active, blk_start[ge] + local * MOE_GROUP_BLOCKS, 0).astype(jnp.int32)
    gnb = jnp.where(active, jnp.minimum(blocks_e[ge] - local * MOE_GROUP_BLOCKS, MOE_GROUP_BLOCKS),
                    0).astype(jnp.int32)
    last_e = ge[jnp.maximum(grp_end[-1] - 1, 0)]
    ge = jnp.where(active, ge, last_e).astype(jnp.int32)
    n_used = jnp.sum(blocks_e).astype(jnp.int32).reshape(1)
    return pos, slot_tok, ge, gb0, gnb, n_used, n_slots


def _layer(x, mem, positions, attn_norm_g, mem_norm_g, w_in, fox_forget_b, fox_q_norm_g,
           fox_k_norm_g, dsa_q_norm_g, dsa_k_norm_g, idx_k_norm_g, cross_q_norm_g, cross_k_norm_g,
           w_mem_kv, w_branch_fox, w_branch_dsa, w_branch_cross, w_out, ffn_norm_g,
           router_group_w, router_group_b, router_expert_w, router_expert_b,
           expert_w_gate, expert_w_up, expert_w_down):
    b, s, d = x.shape
    n = b * s
    n_mem = mem.shape[1]
    fw, dw, cw = FOX_HEADS * HEAD_DIM, DSA_HEADS * HEAD_DIM, CROSS_HEADS * HEAD_DIM
    iqw = IDX_HEADS * IDX_DIM
    tm = min(512, n)
    x2d = x.reshape(n, d)

    sizes = (fw, fw, fw, FOX_HEADS, dw, dw, dw, iqw, IDX_HEADS, IDX_DIM, cw, 3 * d)
    offs = [0]
    for sz in sizes:
        offs.append(offs[-1] + sz)
    (c_fq, c_fk, c_fv, c_fl, c_dq, c_dk, c_dv, c_iq, c_iw, c_ik, c_cq, c_g) = range(12)

    w_t = jnp.transpose(w_in, (2, 0, 1)).reshape(offs[-1], d)
    order = (c_dq, c_dk, c_fv, c_dv, c_fq, c_fk, c_cq, c_iq, c_g)
    row0, r = {}, 0
    for c in order:
        row0[c] = r
        r += sizes[c]
    w_all = jnp.concatenate([w_t[offs[c]:offs[c + 1]] for c in order], axis=0).astype(BF16)
    small_rows = [w_t[offs[c]:offs[c + 1]] for c in (c_ik, c_iw, c_fl)]
    small_rows.append(jnp.zeros((LANES - IDX_DIM - IDX_HEADS - FOX_HEADS, d), F32))
    w_small = jnp.concatenate(small_rows, axis=0).astype(BF16)
    q_unit = (HEAD_DIM ** -0.5) * LOG2E
    g_qk1 = jnp.concatenate([jnp.tile(fox_q_norm_g * q_unit, FOX_HEADS),
                             jnp.tile(fox_k_norm_g, FOX_HEADS),
                             jnp.tile(cross_q_norm_g * q_unit, CROSS_HEADS)])
    g_qk2 = jnp.concatenate([jnp.tile(dsa_q_norm_g * q_unit, DSA_HEADS),
                             jnp.tile(dsa_k_norm_g, DSA_HEADS)])
    g_ik = jnp.concatenate([idx_k_norm_g, jnp.zeros((LANES - IDX_DIM,), F32)])
    fb_lane = jnp.zeros((LANES,), F32).at[SMALL_F:SMALL_F + FOX_HEADS].set(fox_forget_b)

    pos_col = positions.reshape(n, 1).astype(F32)
    cd, sd = _rope_tables(pos_col, *_rope_lane_patterns(ROT_DIM, HEAD_DIM), tm=tm)
    ci, si = _rope_tables(pos_col, *_rope_lane_patterns(IDX_ROT_DIM, IDX_DIM), tm=tm)

    h = _rmsnorm(x2d, attn_norm_g, BF16, tm)
    qk1 = _proj(h, w_all, "norm", BF16, tm, 1024, gains=g_qk1, w_rows=(row0[c_fq], 2 * fw + cw))
    qk2 = _proj(h, w_all, "norm_rope", BF16, tm, 768, gains=g_qk2, tables=(cd, sd),
                w_rows=(row0[c_dq], 2 * dw))
    vv = _proj(h, w_all, "raw", BF16, tm, 768, w_rows=(row0[c_fv], fw + dw))
    iq = _proj(h, w_all, "rope_idx", BF16, tm, 1024, tables=(ci, si), w_rows=(row0[c_iq], iqw))
    small = _proj(h, w_small, "raw", F32, tm, LANES, w_rows=(0, LANES))
    ika, ikb, wl, lf_col = _small(small, g_ik, fb_lane, ci, si, tm)

    tq, tk = min(ATT_TQ, s), min(ATT_TK, s)
    assert tq == tk and s % tq == 0
    cum_col = _forget_cumsum(lf_col.reshape(b, s, LANES))
    qk1 = qk1.reshape(b, s, 2 * fw + cw)
    qk2 = qk2.reshape(b, s, 2 * dw)
    q_aug, k_aug = _fox_pack(qk1, cum_col, FOX_HEADS, min(512, s))
    vt = jnp.transpose(vv.reshape(b, s // tk, tk, fw + dw), (0, 1, 3, 2))
    o_fox = _attention(q_aug, 0, k_aug, 0, vt, 0, (), "fox", FOX_HEADS, AUG_DIM, tq, tk)

    topk = min(INDEX_TOPK, s // 4)
    wt = jnp.transpose(wl.reshape(b, s, LANES)[:, :, :IDX_HEADS], (0, 2, 1))
    bias = _select_bias(iq.reshape(b, s, iqw), ika.reshape(b, s, LANES), ikb.reshape(b, s, LANES),
                        wt, topk, tq, tk)
    o_dsa = _attention(qk2, 0, qk2, 1, vt, 1, (bias,), "dsa", DSA_HEADS, HEAD_DIM, tq, tk)

    tmm = min(512, b * n_mem)
    m_n = _rmsnorm(mem.reshape(b * n_mem, d), mem_norm_g, BF16, tmm)
    w_kv = w_mem_kv.astype(BF16)
    mk = _proj(m_n, w_kv[:, :cw], "norm", BF16, tmm, cw, gains=jnp.tile(cross_k_norm_g, CROSS_HEADS))
    mv = _proj(m_n, w_kv[:, cw:], "raw", BF16, tmm, cw)
    o_cross = _cross_attention(qk1, (2 * fw) // cw, mk.reshape(b, n_mem, cw), mv.reshape(b, n_mem, cw),
                               min(512, s))

    x_mid = _merge(h, o_fox.reshape(n, fw), o_dsa.reshape(n, dw), o_cross.reshape(n, cw), x2d,
                   w_all, row0[c_g], w_branch_fox.astype(BF16), w_branch_dsa.astype(BF16),
                   w_branch_cross.astype(BF16), w_out.astype(BF16), tm, MERGE_TN)

    w_router = jnp.concatenate(
        [router_group_w, router_expert_w, jnp.zeros((d, LANES - N_GROUPS - N_EXPERTS), F32)],
        axis=1).astype(BF16)
    b_router = jnp.concatenate(
        [router_group_b, router_expert_b, jnp.zeros((LANES - N_GROUPS - N_EXPERTS,), F32)])
    h2, eid, gate = _router(x_mid, ffn_norm_g, w_router, b_router, tm)
    pos, slot_tok, ge, gb0, gnb, n_used, n_slots = _moe_plan(eid[:, :MOE_TOP_K], n)
    ys = _experts(h2, slot_tok, ge, gb0, gnb, n_used, expert_w_gate, expert_w_up, expert_w_down,
                  n_slots)
    out = _combine(x_mid, gate, ys, pos, min(256, n))
    return out.reshape(b, s, d)


def kernel(x, mem, positions, attn_norm_g, mem_norm_g, w_in, fox_forget_b, fox_q_norm_g, fox_k_norm_g, dsa_q_norm_g, dsa_k_norm_g, idx_k_norm_g, cross_q_norm_g, cross_k_norm_g, w_mem_kv, w_branch_fox, w_branch_dsa, w_branch_cross, w_out, ffn_norm_g, router_group_w, router_group_b, router_expert_w, router_expert_b, expert_w_gate, expert_w_up, expert_w_down):
    depth = w_in.shape[0]
    for layer in range(depth):
        x = _layer(
            x, mem, positions, attn_norm_g[layer], mem_norm_g[layer], w_in[layer:layer + 1],
            fox_forget_b[layer], fox_q_norm_g[layer], fox_k_norm_g[layer],
            dsa_q_norm_g[layer], dsa_k_norm_g[layer], idx_k_norm_g[layer],
            cross_q_norm_g[layer], cross_k_norm_g[layer], w_mem_kv[layer],
            w_branch_fox[layer], w_branch_dsa[layer], w_branch_cross[layer], w_out[layer],
            ffn_norm_g[layer], router_group_w[layer], router_group_b[layer],
            router_expert_w[layer], router_expert_b[layer], expert_w_gate[layer],
            expert_w_up[layer], expert_w_down[layer])
    return x
```

```python
import functools

import jax
import jax.numpy as jnp
from jax import lax
from jax.experimental import pallas as pl
from jax.experimental.pallas import tpu as pltpu

F32 = jnp.float32
BF16 = jnp.bfloat16

LANES = 128
HEAD_DIM = 128
FOX_HEADS = 6
DSA_HEADS = 6
CROSS_HEADS = 4
IDX_HEADS = 16
IDX_DIM = 64
ROPE_THETA = 500000.0
ROT_DIM = HEAD_DIM // 4
IDX_ROT_DIM = IDX_DIM // 4
INDEX_TOPK = 256
N_GROUPS = 4
EXPERTS_PER_GROUP = 8
N_EXPERTS = N_GROUPS * EXPERTS_PER_GROUP
MOE_TOP_K = 2
EPS = 1e-6

NEG = -1e30
INT_MIN = -(2 ** 31)
VMEM_LIMIT = 56 * 1024 * 1024

ATT_TQ = 256
ATT_TK = 256
ATT_HEADS_PER_LOOP = 6
MERGE_TN = 512
MERGE_SUB = 256
MOE_BM = 256
MOE_GROUP_BLOCKS = 4
MOE_FC = 512
DMA_ISSUE_UNROLL = 8


def _cparams(*sem):
    return pltpu.CompilerParams(dimension_semantics=sem, vmem_limit_bytes=VMEM_LIMIT)


def _rmsnorm_body(x_ref, g_ref, o_ref):
    x = x_ref[...]
    r = lax.rsqrt(jnp.mean(x * x, axis=-1, keepdims=True) + EPS)
    o_ref[...] = ((x * r) * g_ref[...]).astype(o_ref.dtype)


def _rmsnorm(x2d, g, out_dtype, tm):
    m, d = x2d.shape
    return pl.pallas_call(
        _rmsnorm_body,
        grid=(m // tm,),
        in_specs=[pl.BlockSpec((tm, d), lambda i: (i, 0)),
                  pl.BlockSpec((1, d), lambda i: (0, 0))],
        out_specs=pl.BlockSpec((tm, d), lambda i: (i, 0)),
        out_shape=jax.ShapeDtypeStruct((m, d), out_dtype),
        compiler_params=_cparams("parallel"),
        name="rmsnorm",
    )(x2d, g.reshape(1, d))


W_PREP_ROWS = 128


def _w_prep_body(src_ref, w_ref, o_ref):
    del src_ref
    o_ref[...] = w_ref[:, 0, :].astype(o_ref.dtype)


def _w_prep(w_t, src_rows):
    _, _, k = w_t.shape
    n_chunks = src_rows.shape[0]
    grid_spec = pltpu.PrefetchScalarGridSpec(
        num_scalar_prefetch=1,
        grid=(n_chunks,),
        in_specs=[pl.BlockSpec((pl.Element(W_PREP_ROWS), pl.Element(1), pl.Element(k)),
                               lambda c, src: (src[c], 0, 0))],
        out_specs=pl.BlockSpec((W_PREP_ROWS, k), lambda c, src: (c, 0)),
    )
    return pl.pallas_call(
        _w_prep_body,
        grid_spec=grid_spec,
        out_shape=jax.ShapeDtypeStruct((n_chunks * W_PREP_ROWS, k), BF16),
        compiler_params=_cparams("parallel"),
        name="w_prep",
    )(src_rows, w_t)


IDX_FREQ_LANE = 32


def _rope_table_body(pos_ref, f_ref, cd_ref, sd_ref, ci_ref, si_ref):
    ang = pos_ref[...] * f_ref[...]
    c = jnp.cos(ang)
    s = jnp.sin(ang)
    lane = lax.broadcasted_iota(jnp.int32, c.shape, 1)
    hd = ROT_DIM // 2
    cd_ref[...] = jnp.where(lane < hd, c, jnp.where(lane < ROT_DIM, pltpu.roll(c, hd, 1), 1.0))
    sd_ref[...] = jnp.where(lane < hd, -s, jnp.where(lane < ROT_DIM, pltpu.roll(s, hd, 1), 0.0))
    m = lane & (IDX_DIM - 1)
    hi = IDX_ROT_DIM // 2
    ci_src = pltpu.roll(c, LANES - IDX_FREQ_LANE, 1)
    si_src = pltpu.roll(s, LANES - IDX_FREQ_LANE, 1)
    ci_ref[...] = jnp.where(m < IDX_ROT_DIM, ci_src, 1.0)
    si_ref[...] = jnp.where(m < hi, -si_src, jnp.where(m < IDX_ROT_DIM, si_src, 0.0))


def _rope_tables(pos_col, tm):
    n = pos_col.shape[0]

    def inv_freq(rot_dim):
        half = rot_dim // 2
        return jnp.power(jnp.float32(ROPE_THETA), -jnp.arange(half, dtype=F32) * 2.0 / rot_dim)

    fi = jnp.tile(inv_freq(IDX_ROT_DIM), 2)
    freq = jnp.zeros((LANES,), F32).at[:ROT_DIM // 2].set(inv_freq(ROT_DIM))
    for base in (IDX_FREQ_LANE, IDX_FREQ_LANE + IDX_DIM):
        freq = freq.at[base:base + IDX_ROT_DIM].set(fi)
    tab = pl.BlockSpec((tm, LANES), lambda i: (i, 0))
    return pl.pallas_call(
        _rope_table_body,
        grid=(n // tm,),
        in_specs=[pl.BlockSpec((tm, 1), lambda i: (i, 0)), pl.BlockSpec((1, LANES), lambda i: (0, 0))],
        out_specs=[tab] * 4,
        out_shape=[jax.ShapeDtypeStruct((n, LANES), F32)] * 4,
        compiler_params=_cparams("parallel"),
        name="rope_tables",
    )(pos_col, freq.reshape(1, LANES))


def _apply_rope(a, c, s, half, period):
    lane = lax.broadcasted_iota(jnp.int32, a.shape, 1)
    first = (lane & (period - 1)) < half
    partner = jnp.where(first, pltpu.roll(a, LANES - half, 1), pltpu.roll(a, half, 1))
    return a * c + partner * s


def _proj_body(*refs, mode, n_lane_groups, w_rows):
    h_ref, w_ref = refs[0], refs[1]
    o_ref = refs[-1]

    if w_rows:
        acc = lax.dot_general(h_ref[...], w_ref[...], (((1,), (1,)), ((), ())),
                              preferred_element_type=F32)
    else:
        acc = jnp.dot(h_ref[...], w_ref[...], preferred_element_type=F32)
    if mode == "raw":
        o_ref[...] = acc.astype(o_ref.dtype)
        return
    for g in range(n_lane_groups):
        sl = slice(g * LANES, (g + 1) * LANES)
        a = acc[:, sl]
        if mode in ("norm", "norm_rope"):
            g_ref = refs[2]
            a = (a * lax.rsqrt(jnp.mean(a * a, axis=-1, keepdims=True) + EPS)) * g_ref[:, sl]
        if mode == "norm_rope":
            a = _apply_rope(a, refs[3][...], refs[4][...], ROT_DIM // 2, HEAD_DIM)
        if mode == "rope_idx":
            a = _apply_rope(a, refs[2][...], refs[3][...], IDX_ROT_DIM // 2, IDX_DIM)
        o_ref[:, sl] = a.astype(o_ref.dtype)


def _proj(h, w, mode, out_dtype, tm, tn, gains=None, tables=None, w_rows=None):
    m, k = h.shape
    if w_rows is None:
        n = w.shape[1]
        w_spec = pl.BlockSpec((k, tn), lambda j, i: (0, j))
    else:
        row0, n = w_rows
        assert row0 % tn == 0 and n % tn == 0
        w_spec = pl.BlockSpec((tn, k), lambda j, i: (row0 // tn + j, 0))
    in_specs = [pl.BlockSpec((tm, k), lambda j, i: (i, 0)), w_spec]
    args = [h, w]
    if gains is not None:
        in_specs.append(pl.BlockSpec((1, tn), lambda j, i: (0, j)))
        args.append(gains.reshape(1, n))
    if tables is not None:
        for t in tables:
            in_specs.append(pl.BlockSpec((tm, LANES), lambda j, i: (i, 0)))
            args.append(t)
    return pl.pallas_call(
        functools.partial(_proj_body, mode=mode, n_lane_groups=tn // LANES,
                          w_rows=w_rows is not None),
        grid=(n // tn, m // tm),
        in_specs=in_specs,
        out_specs=pl.BlockSpec((tm, tn), lambda j, i: (i, j)),
        out_shape=jax.ShapeDtypeStruct((m, n), out_dtype),
        compiler_params=_cparams("parallel", "parallel"),
        name="proj_" + mode,
    )(*args)


SMALL_IK = 0
SMALL_IW = 64
SMALL_F = 80


def _small_body(x_ref, gk_ref, fb_ref, c_ref, s_ref, ika_ref, ikb_ref, wl_ref, lf_ref):
    x = x_ref[...]
    lane = lax.broadcasted_iota(jnp.int32, x.shape, 1)
    is_ik = lane < IDX_DIM
    ik = jnp.where(is_ik, x, 0.0)
    ms = jnp.sum(ik * ik, axis=-1, keepdims=True) * (1.0 / IDX_DIM)
    ik = (ik * lax.rsqrt(ms + EPS)) * gk_ref[...]
    ik = _apply_rope(ik, c_ref[...], s_ref[...], IDX_ROT_DIM // 2, IDX_DIM)
    ik = jnp.where(is_ik, ik, 0.0)
    ika_ref[...] = ik.astype(BF16)
    ikb_ref[...] = pltpu.roll(ik, IDX_DIM, 1).astype(BF16)
    wl_ref[...] = pltpu.roll(x, LANES - SMALL_IW, 1) * (2.0 ** -5)
    z = x + fb_ref[...]
    log_f = jnp.minimum(z, 0.0) - jnp.log1p(jnp.exp(-jnp.abs(z)))
    lf_ref[...] = pltpu.roll(log_f, LANES - SMALL_F, 1)


def _small(x, idx_k_gain_lane, forget_b_lane, ci, si, tm):
    n = x.shape[0]
    row = pl.BlockSpec((1, LANES), lambda i: (0, 0))
    tab = pl.BlockSpec((tm, LANES), lambda i: (i, 0))
    return pl.pallas_call(
        _small_body,
        grid=(n // tm,),
        in_specs=[tab, row, row, tab, tab],
        out_specs=[tab, tab, tab, tab],
        out_shape=[jax.ShapeDtypeStruct((n, LANES), BF16), jax.ShapeDtypeStruct((n, LANES), BF16),
                   jax.ShapeDtypeStruct((n, LANES), F32), jax.ShapeDtypeStruct((n, LANES), F32)],
        compiler_params=_cparams("parallel"),
        name="small_cols",
    )(x, idx_k_gain_lane.reshape(1, LANES), forget_b_lane.reshape(1, LANES), ci, si)


def _cumsum_body(col_ref, ccol_ref, *, seq):
    c = col_ref[...]
    ci = lax.broadcasted_iota(jnp.int32, c.shape, 0)
    sh = 1
    while sh < seq:
        c = c + jnp.where(ci >= sh, pltpu.roll(c, sh, 0), 0.0)
        sh *= 2
    ccol_ref[...] = c


def _forget_cumsum(lf_col):
    b, s, _ = lf_col.shape
    return pl.pallas_call(
        functools.partial(_cumsum_body, seq=s),
        grid=(b,),
        in_specs=[pl.BlockSpec((None, s, LANES), lambda i: (i, 0, 0))],
        out_specs=pl.BlockSpec((None, s, LANES), lambda i: (i, 0, 0)),
        out_shape=jax.ShapeDtypeStruct((b, s, LANES), F32),
        compiler_params=_cparams("parallel"),
        name="forget_cumsum",
    )(lf_col)


AUG_DIM = 2 * HEAD_DIM
LOG2E = 1.4426950408889634


def _split3(x):
    hi = x.astype(BF16).astype(F32)
    r1 = x - hi
    mid = r1.astype(BF16).astype(F32)
    lo = (r1 - mid).astype(BF16).astype(F32)
    return hi, mid, lo


def _fox_pack_body(q_ref, k_ref, cum_ref, qa_ref, ka_ref, *, n_heads):
    rows = q_ref.shape[0]
    lane = lax.broadcasted_iota(jnp.int32, (rows, LANES), 1)
    for h in range(n_heads):
        c = jnp.broadcast_to(cum_ref[:, h:h + 1], (rows, LANES)) * LOG2E
        qh, qm, ql = _split3(c)
        kh, km, kl = _split3(-c)
        q_aug = jnp.where(lane == 0, qh, jnp.where(lane == 1, qm, jnp.where(
            lane == 2, ql, jnp.where(lane < 6, 1.0, 0.0))))
        k_aug = jnp.where(lane < 3, 1.0, jnp.where(lane == 3, kh, jnp.where(
            lane == 4, km, jnp.where(lane == 5, kl, 0.0))))
        base = h * AUG_DIM
        qa_ref[:, base:base + HEAD_DIM] = q_ref[:, h * HEAD_DIM:(h + 1) * HEAD_DIM]
        ka_ref[:, base:base + HEAD_DIM] = k_ref[:, h * HEAD_DIM:(h + 1) * HEAD_DIM]
        qa_ref[:, base + HEAD_DIM:base + AUG_DIM] = q_aug.astype(BF16)
        ka_ref[:, base + HEAD_DIM:base + AUG_DIM] = k_aug.astype(BF16)


def _fox_pack(qk, cum_col, n_heads, ts):
    b, s, _ = qk.shape
    w = n_heads * HEAD_DIM
    wa = n_heads * AUG_DIM
    return pl.pallas_call(
        functools.partial(_fox_pack_body, n_heads=n_heads),
        grid=(b, s // ts),
        in_specs=[pl.BlockSpec((None, ts, w), lambda bi, i: (bi, i, 0)),
                  pl.BlockSpec((None, ts, w), lambda bi, i: (bi, i, 1)),
                  pl.BlockSpec((None, ts, LANES), lambda bi, i: (bi, i, 0))],
        out_specs=[pl.BlockSpec((None, ts, wa), lambda bi, i: (bi, i, 0)),
                   pl.BlockSpec((None, ts, wa), lambda bi, i: (bi, i, 0))],
        out_shape=[jax.ShapeDtypeStruct((b, s, wa), BF16)] * 2,
        compiler_params=_cparams("parallel", "parallel"),
        name="fox_pack",
    )(qk, qk, cum_col)


def _attn_body(q_ref, k_ref, vt_ref, *rest, tq, tk, n_heads, kdim, mode, heads_per_loop):
    o_ref = rest[-1]
    qi = pl.program_id(1)
    key0 = lax.broadcasted_iota(jnp.int32, (tk, tq), 0)
    qry = qi * tq + lax.broadcasted_iota(jnp.int32, (tk, tq), 1)
    def scores(h, j):
        off = pl.multiple_of(j * tk, tk)
        q = q_ref[:, h * kdim:(h + 1) * kdim]
        k = k_ref[pl.ds(off, tk), h * kdim:(h + 1) * kdim]
        return lax.dot_general(k, q, (((1,), (1,)), ((), ())), preferred_element_type=F32)

    def flush(h, j, p, alpha, acc):
        vt = vt_ref[j, h * HEAD_DIM:(h + 1) * HEAD_DIM, :]
        return alpha * acc + jnp.dot(vt, p, preferred_element_type=F32)

    def softmax_step(h, j, m, l, masked):
        s = scores(h, j)
        if mode == "dsa":
            s = s + rest[0][j]
        if masked:
            s = jnp.where(key0 + j * tk <= qry, s, NEG)
        m_new = jnp.maximum(m, jnp.max(s, axis=0, keepdims=True))
        alpha = jnp.exp2(m - m_new)
        p = jnp.exp2(s - m_new)
        l = alpha * l + jnp.sum(p, axis=0, keepdims=True)
        return p.astype(BF16), alpha, m_new, l

    last = qi
    for h0 in range(0, n_heads, heads_per_loop):
        heads = range(h0, h0 + heads_per_loop)

        def step(j, carries, masked=False, heads=heads):
            prev = jnp.maximum(j - 1, 0)
            out = []
            for h, (p, alpha, m, l, acc) in zip(heads, carries):
                acc = flush(h, prev, p, alpha, acc)
                p, alpha, m, l = softmax_step(h, j, m, l, masked)
                out.append((p, alpha, m, l, acc))
            return tuple(out)

        carries = tuple(
            (jnp.zeros((tk, tq), BF16), jnp.ones((1, tq), F32), jnp.full((1, tq), NEG, F32),
             jnp.zeros((1, tq), F32), jnp.zeros((HEAD_DIM, tq), F32)) for h in heads)
        n_loop = last if mode == "fox" else last + 1
        carries = lax.fori_loop(0, n_loop, step, carries)
        if mode == "fox":
            carries = step(last, carries, masked=True)
        for h, (p, alpha, m, l, acc) in zip(heads, carries):
            acc = flush(h, last, p, alpha, acc)
            o_ref[:, h * HEAD_DIM:(h + 1) * HEAD_DIM] = jnp.transpose(acc / l).astype(o_ref.dtype)


def _attention(q_arr, q_blk, k_arr, k_blk, vt_arr, v_blk, extras, mode, n_heads, kdim, tq, tk):
    b, s, _ = q_arr.shape
    w = n_heads * HEAD_DIM
    wk = n_heads * kdim
    in_specs = [pl.BlockSpec((None, tq, wk), lambda bi, qi: (bi, qi, q_blk)),
                pl.BlockSpec((None, s, wk), lambda bi, qi: (bi, 0, k_blk)),
                pl.BlockSpec((None, s // tk, w, tk), lambda bi, qi: (bi, 0, v_blk, 0))]
    if mode == "dsa":
        in_specs += [pl.BlockSpec((None, None, s // tk, tk, tq), lambda bi, qi: (bi, qi, 0, 0, 0))]
    return pl.pallas_call(
        functools.partial(_attn_body, tq=tq, tk=tk, n_heads=n_heads, kdim=kdim, mode=mode,
                          heads_per_loop=ATT_HEADS_PER_LOOP),
        grid=(b, s // tq),
        in_specs=in_specs,
        out_specs=pl.BlockSpec((None, tq, w), lambda bi, qi: (bi, qi, 0)),
        out_shape=jax.ShapeDtypeStruct((b, s, w), BF16),
        compiler_params=_cparams("parallel", "parallel"),
        name="attn_" + mode,
    )(q_arr, k_arr, vt_arr, *extras)


def _select_body(iq_ref, ika_ref, ikb_ref, wt_ref, o_ref, keys_ref, half_ref, *, tq, tk, n_chunks, topk):
    qi = pl.program_id(1)
    used = qi + 1
    key0 = lax.broadcasted_iota(jnp.int32, (tk, tq), 0)
    qry = qi * tq + lax.broadcasted_iota(jnp.int32, (tk, tq), 1)

    def score_chunk(c, _):
        off = pl.multiple_of(c * tk, tk)
        ka = ika_ref[pl.ds(off, tk), :]
        kb = ikb_ref[pl.ds(off, tk), :]
        acc = jnp.zeros((tk, tq), F32)
        for j in range(IDX_HEADS):
            pair = iq_ref[:, (j // 2) * LANES:(j // 2 + 1) * LANES]
            logit = lax.dot_general(ka if j % 2 == 0 else kb, pair, (((1,), (1,)), ((), ())),
                                    preferred_element_type=F32)
            acc = acc + wt_ref[j:j + 1, :] * jnp.maximum(logit, 0.0)
        bits = lax.bitcast_convert_type(acc, jnp.int32)
        key = jnp.where(bits < 0, bits ^ jnp.int32(0x7FFFFFFF), bits)
        key = jnp.where(key0 + off <= qry, key, jnp.int32(INT_MIN))
        keys_ref[c] = key
        half_ref[c] = lax.shift_right_arithmetic(key, 16).astype(jnp.int16)
        return 0

    lax.fori_loop(0, used, score_chunk, 0)

    one16 = jnp.int16(1)
    zero16 = jnp.int16(0)

    def count_ge(cand16):
        def count_chunk(c, cnt):
            ge = jnp.where(half_ref[c] >= cand16, one16, zero16)
            parts = [ge[r:r + 16] for r in range(0, tk, 16)]
            while len(parts) > 1:
                parts = [a + b for a, b in zip(parts[::2], parts[1::2])]
            return cnt + parts[0]

        cnt = lax.fori_loop(0, used, count_chunk, jnp.zeros((16, tq), jnp.int16))
        return jnp.sum(cnt.astype(jnp.int32), axis=0, keepdims=True)

    def search16(need):
        def bit_step(i, prefix):
            cand_u = prefix | jnp.left_shift(jnp.int32(1), 15 - i)
            cnt = count_ge((cand_u - 32768).astype(jnp.int16))
            return jnp.where(cnt >= need, cand_u, prefix)

        return lax.fori_loop(0, 16, bit_step, jnp.zeros((1, tq), jnp.int32)) - 32768

    hi = search16(topk)
    above = count_ge(jnp.minimum(hi + 1, 32767).astype(jnp.int16))
    above = jnp.where(hi >= 32767, 0, above)
    hi16 = hi.astype(jnp.int16)

    def low_chunk(c, _):
        low = (keys_ref[c] & jnp.int32(0xFFFF)) - 32768
        half_ref[c] = jnp.where(half_ref[c] == hi16, low.astype(jnp.int16), jnp.int16(-32768))
        return 0

    lax.fori_loop(0, used, low_chunk, 0)
    lo = search16(topk - above)
    thr = lax.shift_left(hi, 16) | (lo + 32768)
    thr = jnp.maximum(thr, jnp.int32(INT_MIN + 1))

    def count_keys(pred):
        def chunk(c, cnt):
            hit = jnp.where(pred(keys_ref[c]), 1.0, 0.0)
            return cnt + jnp.sum(hit.reshape(tk // 8, 8, tq), axis=0)

        cnt = lax.fori_loop(0, used, chunk, jnp.zeros((8, tq), F32))
        return jnp.sum(cnt, axis=0, keepdims=True)

    tied = jnp.max(count_keys(lambda k: k >= thr)) > topk

    @pl.when(jnp.logical_not(tied))
    def _():
        def write_chunk(c, _):
            o_ref[c] = jnp.where(keys_ref[c] >= thr, 0.0, NEG)
            return 0

        lax.fori_loop(0, used, write_chunk, 0)

    @pl.when(tied)
    def _():
        need = topk - count_keys(lambda k: k > thr)
        earlier = jnp.where(lax.broadcasted_iota(jnp.int32, (tk, tk), 1)
                            < lax.broadcasted_iota(jnp.int32, (tk, tk), 0), 1.0, 0.0).astype(BF16)

        def write_chunk(c, seen):
            keys = keys_ref[c]
            tie = jnp.where(keys == thr, 1.0, 0.0)
            rank = jnp.dot(earlier, tie.astype(BF16), preferred_element_type=F32) + seen
            o_ref[c] = jnp.where(keys > thr, 0.0,
                                 jnp.where(keys == thr, jnp.where(rank < need, 0.0, NEG), NEG))
            return seen + jnp.sum(tie, axis=0, keepdims=True)

        lax.fori_loop(0, used, write_chunk, jnp.zeros((1, tq), F32))

    def fill_chunk(c, _):
        o_ref[c] = jnp.full((tk, tq), NEG, F32)
        return 0

    lax.fori_loop(used, n_chunks, fill_chunk, 0)


def _select_bias(iq, ika, ikb, wt, topk, tq, tk):
    b, s, _ = iq.shape
    n_chunks = s // tk
    return pl.pallas_call(
        functools.partial(_select_body, tq=tq, tk=tk, n_chunks=n_chunks, topk=topk),
        grid=(b, s // tq),
        in_specs=[pl.BlockSpec((None, tq, IDX_HEADS * IDX_DIM), lambda bi, qi: (bi, qi, 0)),
                  pl.BlockSpec((None, s, LANES), lambda bi, qi: (bi, 0, 0)),
                  pl.BlockSpec((None, s, LANES), lambda bi, qi: (bi, 0, 0)),
                  pl.BlockSpec((None, IDX_HEADS, tq), lambda bi, qi: (bi, 0, qi))],
        out_specs=pl.BlockSpec((None, None, n_chunks, tk, tq), lambda bi, qi: (bi, qi, 0, 0, 0)),
        out_shape=jax.ShapeDtypeStruct((b, s // tq, n_chunks, tk, tq), F32),
        scratch_shapes=[pltpu.VMEM((n_chunks, tk, tq), jnp.int32),
                        pltpu.VMEM((n_chunks, tk, tq), jnp.int16)],
        compiler_params=_cparams("parallel", "parallel"),
        name="index_select",
    )(iq, ika, ikb, wt)


def _cross_body(q_ref, k_ref, v_ref, o_ref, *, n_heads):
    for h in range(n_heads):
        hs = slice(h * HEAD_DIM, (h + 1) * HEAD_DIM)
        s = lax.dot_general(q_ref[:, hs], k_ref[:, hs], (((1,), (1,)), ((), ())),
                            preferred_element_type=F32)
        p = jnp.exp2(s - jnp.max(s, axis=-1, keepdims=True))
        l = jnp.sum(p, axis=-1, keepdims=True)
        o = jnp.dot(p.astype(BF16), v_ref[:, hs], preferred_element_type=F32)
        o_ref[:, hs] = (o / l).astype(o_ref.dtype)


def _cross_attention(q_arr, q_blk, mk, mv, tq):
    b, s, _ = q_arr.shape
    n_mem = mk.shape[1]
    w = CROSS_HEADS * HEAD_DIM
    return pl.pallas_call(
        functools.partial(_cross_body, n_heads=CROSS_HEADS),
        grid=(b, s // tq),
        in_specs=[pl.BlockSpec((None, tq, w), lambda bi, qi: (bi, qi, q_blk)),
                  pl.BlockSpec((None, n_mem, w), lambda bi, qi: (bi, 0, 0)),
                  pl.BlockSpec((None, n_mem, w), lambda bi, qi: (bi, 0, 0))],
        out_specs=pl.BlockSpec((None, tq, w), lambda bi, qi: (bi, qi, 0)),
        out_shape=jax.ShapeDtypeStruct((b, s, w), BF16),
        compiler_params=_cparams("parallel", "parallel"),
        name="cross_attn",
    )(q_arr, mk, mv)


def _sigmoid(x):
    return 1.0 / (1.0 + jnp.exp(-x))


def _merge_body(h_ref, of_ref, od_ref, oc_ref, x_ref, wgf_ref, wgd_ref, wgc_ref,
                wbf_ref, wbd_ref, wbc_ref, wo_ref, o_ref):
    j = pl.program_id(1)
    h = h_ref[...]
    tn = wo_ref.shape[0]
    sub = min(tn, MERGE_SUB)

    def gate_logits(wg_ref, cs):
        return lax.dot_general(h, wg_ref[cs, :], (((1,), (1,)), ((), ())), preferred_element_type=F32)

    part = None
    for c in range(tn // sub):
        cs = slice(c * sub, (c + 1) * sub)
        zf, zd, zc = gate_logits(wgf_ref, cs), gate_logits(wgd_ref, cs), gate_logits(wgc_ref, cs)
        bf = jnp.dot(of_ref[...], wbf_ref[:, cs], preferred_element_type=F32)
        bd = jnp.dot(od_ref[...], wbd_ref[:, cs], preferred_element_type=F32)
        bc = jnp.dot(oc_ref[...], wbc_ref[:, cs], preferred_element_type=F32)
        merged = (_sigmoid(zf) * bf + _sigmoid(zd) * bd) + _sigmoid(zc) * bc
        p = jnp.dot(merged.astype(BF16), wo_ref[cs, :], preferred_element_type=F32)
        part = p if part is None else part + p

    @pl.when(j == 0)
    def _():
        o_ref[...] = x_ref[...] + part

    @pl.when(j != 0)
    def _():
        o_ref[...] += part


def _merge(h, o_fox, o_dsa, o_cross, x, w_gates, gate_row0, wbf, wbd, wbc, w_out, tm, tn):
    m, d = x.shape
    nj = d // tn
    assert gate_row0 % tn == 0
    g0 = gate_row0 // tn
    row = lambda i, j: (i, 0)
    in_specs = [pl.BlockSpec((tm, d), row),
                pl.BlockSpec((tm, o_fox.shape[1]), row),
                pl.BlockSpec((tm, o_dsa.shape[1]), row),
                pl.BlockSpec((tm, o_cross.shape[1]), row),
                pl.BlockSpec((tm, d), row),
                pl.BlockSpec((tn, d), lambda i, j: (g0 + j, 0)),
                pl.BlockSpec((tn, d), lambda i, j: (g0 + nj + j, 0)),
                pl.BlockSpec((tn, d), lambda i, j: (g0 + 2 * nj + j, 0)),
                pl.BlockSpec((wbf.shape[0], tn), lambda i, j: (0, j)),
                pl.BlockSpec((wbd.shape[0], tn), lambda i, j: (0, j)),
                pl.BlockSpec((wbc.shape[0], tn), lambda i, j: (0, j)),
                pl.BlockSpec((tn, d), lambda i, j: (j, 0))]
    return pl.pallas_call(
        _merge_body,
        grid=(m // tm, nj),
        in_specs=in_specs,
        out_specs=pl.BlockSpec((tm, d), row),
        out_shape=jax.ShapeDtypeStruct((m, d), F32),
        compiler_params=_cparams("parallel", "arbitrary"),
        name="gated_merge",
    )(h, o_fox, o_dsa, o_cross, x, w_gates, w_gates, w_gates, wbf, wbd, wbc, w_out)


def _router_body(x_ref, g_ref, w_ref, b_ref, h_ref, eid_ref, gate_ref):
    x = x_ref[...]
    r = lax.rsqrt(jnp.mean(x * x, axis=-1, keepdims=True) + EPS)
    h = (x * r) * g_ref[...]
    hb = h.astype(BF16)
    bits = lax.bitcast_convert_type(hb.astype(F32), jnp.int32)
    half = bits.shape[1] // 2
    h_ref[...] = lax.shift_right_logical(bits[:, :half], 16) | (bits[:, half:] & jnp.int32(-65536))
    logits = jnp.dot(hb, w_ref[...], preferred_element_type=F32) + b_ref[...]
    lane = lax.broadcasted_iota(jnp.int32, logits.shape, 1).astype(F32)

    def masked_max(mask):
        return jnp.max(jnp.where(mask, logits, -jnp.inf), axis=-1, keepdims=True)

    def first_lane(mask):
        return jnp.min(jnp.where(mask, lane, float(LANES)), axis=-1, keepdims=True)

    is_group = lane < N_GROUPS
    gmax = masked_max(is_group)
    grp = first_lane(is_group & (logits == gmax))
    denom = jnp.sum(jnp.where(is_group, jnp.exp(logits - gmax), 0.0), axis=-1, keepdims=True)
    p_sel = 1.0 / denom
    lo = N_GROUPS + grp * EXPERTS_PER_GROUP
    in_grp = (lane >= lo) & (lane < lo + EXPERTS_PER_GROUP)
    e1 = masked_max(in_grp)
    j1 = first_lane(in_grp & (logits == e1))
    rest = in_grp & (lane != j1)
    e2 = masked_max(rest)
    j2 = first_lane(rest & (logits == e2))
    t = jnp.exp(e2 - e1)
    g1 = p_sel * (1.0 / (1.0 + t))
    g2 = p_sel * (t / (1.0 + t))
    eid = jnp.where(lane == 0.0, j1 - N_GROUPS, jnp.where(lane == 1.0, j2 - N_GROUPS, 0.0))
    eid_ref[...] = eid.astype(jnp.int32)
    gate_ref[...] = jnp.where(lane == 0.0, g1, jnp.where(lane == 1.0, g2, 0.0))


def _router(x, g, w_router, b_router, tm):
    m, d = x.shape
    return pl.pallas_call(
        _router_body,
        grid=(m // tm,),
        in_specs=[pl.BlockSpec((tm, d), lambda i: (i, 0)),
                  pl.BlockSpec((1, d), lambda i: (0, 0)),
                  pl.BlockSpec((d, LANES), lambda i: (0, 0)),
                  pl.BlockSpec((1, LANES), lambda i: (0, 0))],
        out_specs=[pl.BlockSpec((tm, d // 2), lambda i: (i, 0)),
                   pl.BlockSpec((tm, LANES), lambda i: (i, 0)),
                   pl.BlockSpec((tm, LANES), lambda i: (i, 0))],
        out_shape=[jax.ShapeDtypeStruct((m, d // 2), jnp.int32),
                   jax.ShapeDtypeStruct((m, LANES), jnp.int32),
                   jax.ShapeDtypeStruct((m, LANES), F32)],
        compiler_params=_cparams("parallel"),
        name="moe_router",
    )(x, g.reshape(1, d), w_router, b_router.reshape(1, LANES))


def _expert_body(ge_ref, gb0_ref, gnb_ref, used_ref, tok_ref, h_hbm, wg_ref, wu_ref, wd_ref, ys_hbm,
                 xs_ref, xb_ref, acc_ref, wgb_ref, wub_ref, wdb_ref, sem_in, sem_out,
                 *, bm, n_f, n_blocks):
    g = pl.program_id(0)
    f = pl.program_id(1)
    nb = gnb_ref[g]
    blk0 = gb0_ref[g]

    @pl.when(jnp.logical_and(g == 0, f == 0))
    def _():
        acc_ref[pl.ds(0, bm)] = jnp.zeros((bm,) + acc_ref.shape[1:], F32)

        def tail_copy(i):
            return pltpu.make_async_copy(acc_ref.at[pl.ds(0, bm)], ys_hbm.at[pl.ds(i * bm, bm)],
                                         sem_out)

        def start_tail(i, _):
            tail_copy(i).start()
            return 0

        lax.fori_loop(used_ref[0], n_blocks, start_tail, 0)

        def wait_tail(i, _):
            tail_copy(i).wait()
            return 0

        lax.fori_loop(used_ref[0], n_blocks, wait_tail, 0)

    def gather(grp):
        slot = grp % 2
        base = gb0_ref[grp] * bm

        def issue(r0, _):
            for u in range(DMA_ISSUE_UNROLL):
                r = r0 * DMA_ISSUE_UNROLL + u
                pltpu.make_async_copy(h_hbm.at[pl.ds(tok_ref[base + r], 1)],
                                      xs_ref.at[slot, pl.ds(r, 1)], sem_in.at[slot]).start()
            return 0

        lax.fori_loop(0, gnb_ref[grp] * (bm // DMA_ISSUE_UNROLL), issue, 0)

    def block_copy_out(i):
        return pltpu.make_async_copy(acc_ref.at[pl.ds(i * bm, bm)],
                                     ys_hbm.at[pl.ds((blk0 + i) * bm, bm)], sem_out)

    @pl.when(jnp.logical_and(g == 0, f == 0))
    def _():
        gather(g)

    @pl.when(jnp.logical_and(nb > 0, f == 0))
    def _():
        slot = g % 2

        def land(i, _):
            rows = pl.ds(pl.multiple_of(i * bm, bm), bm)
            pltpu.make_async_copy(h_hbm.at[pl.ds(0, bm)], xs_ref.at[slot, rows], sem_in.at[slot]).wait()
            return 0

        lax.fori_loop(0, nb, land, 0)

        half = xb_ref.shape[1] // 2

        def unpack(i, _):
            rows = pl.ds(pl.multiple_of(i * bm, bm), bm)
            w = xs_ref[slot, rows, :]
            lo = lax.bitcast_convert_type(lax.shift_left(w, 16), F32)
            hi = lax.bitcast_convert_type(w & jnp.int32(-65536), F32)
            xb_ref[rows, :half] = lo.astype(BF16)
            xb_ref[rows, half:] = hi.astype(BF16)
            return 0

        lax.fori_loop(0, nb, unpack, 0)

    @pl.when(jnp.logical_and(f == 0, g + 1 < pl.num_programs(0)))
    def _():
        gather(g + 1)

    def wait_out(n):
        def body(i, _):
            pltpu.make_async_copy(acc_ref.at[pl.ds(0, bm)], ys_hbm.at[pl.ds(0, bm)], sem_out).wait()
            return 0

        lax.fori_loop(0, n, body, 0)

    @pl.when(jnp.logical_and(f == 0, g > 0))
    def _():
        wait_out(gnb_ref[jnp.maximum(g - 1, 0)])

    @pl.when(nb > 0)
    def _():
        wgb_ref[...] = wg_ref[...].astype(BF16)
        wub_ref[...] = wu_ref[...].astype(BF16)
        wdb_ref[...] = wd_ref[...].astype(BF16)

        def block(i, _):
            rows = pl.ds(pl.multiple_of(i * bm, bm), bm)
            xb = xb_ref[rows, :]
            a = jnp.dot(xb, wgb_ref[...], preferred_element_type=F32)
            u = jnp.dot(xb, wub_ref[...], preferred_element_type=F32)
            hid = (a * _sigmoid(a)) * u
            y = jnp.dot(hid.astype(BF16), wdb_ref[...], preferred_element_type=F32)

            @pl.when(f == 0)
            def _():
                acc_ref[rows, :] = y

            @pl.when(f != 0)
            def _():
                acc_ref[rows, :] += y

            return 0

        lax.fori_loop(0, nb, block, 0)

    @pl.when(jnp.logical_and(nb > 0, f == n_f - 1))
    def _():
        def start_out(i, _):
            block_copy_out(i).start()
            return 0

        lax.fori_loop(0, nb, start_out, 0)

        @pl.when(g == pl.num_programs(0) - 1)
        def _():
            wait_out(nb)


def _experts(h2, slot_tok, grp_e, grp_blk0, grp_nb, n_used, w_gate, w_up, w_down, n_slots):
    d = w_gate.shape[1]
    ff = w_gate.shape[2]
    n_f = ff // MOE_FC
    n_groups = grp_e.shape[0]
    rmax = MOE_GROUP_BLOCKS * MOE_BM

    def f_idx(g, f, gnb):
        return jnp.where(gnb[g] > 0, f, n_f - 1)

    grid_spec = pltpu.PrefetchScalarGridSpec(
        num_scalar_prefetch=5,
        grid=(n_groups, n_f),
        in_specs=[
            pl.BlockSpec(memory_space=pl.ANY),
            pl.BlockSpec((None, d, MOE_FC), lambda g, f, ge, gb0, gnb, *_: (ge[g], 0, f_idx(g, f, gnb))),
            pl.BlockSpec((None, d, MOE_FC), lambda g, f, ge, gb0, gnb, *_: (ge[g], 0, f_idx(g, f, gnb))),
            pl.BlockSpec((None, MOE_FC, d), lambda g, f, ge, gb0, gnb, *_: (ge[g], f_idx(g, f, gnb), 0)),
        ],
        out_specs=pl.BlockSpec(memory_space=pl.ANY),
        scratch_shapes=[pltpu.VMEM((2, rmax, d // 2), jnp.int32),
                        pltpu.VMEM((rmax, d), BF16),
                        pltpu.VMEM((rmax, d), F32),
                        pltpu.VMEM((d, MOE_FC), BF16),
                        pltpu.VMEM((d, MOE_FC), BF16),
                        pltpu.VMEM((MOE_FC, d), BF16),
                        pltpu.SemaphoreType.DMA((2,)),
                        pltpu.SemaphoreType.DMA(())],
    )
    return pl.pallas_call(
        functools.partial(_expert_body, bm=MOE_BM, n_f=n_f, n_blocks=n_slots // MOE_BM),
        grid_spec=grid_spec,
        out_shape=jax.ShapeDtypeStruct((n_slots, d), F32),
        compiler_params=_cparams("arbitrary", "arbitrary"),
        name="moe_experts",
    )(grp_e, grp_blk0, grp_nb, n_used, slot_tok, h2, w_gate, w_up, w_down)


def _combine_body(pos_ref, x_ref, gate_ref, ys_hbm, o_ref, buf_ref, sem, *, tm):
    i = pl.program_id(0)
    n_steps = pl.num_programs(0)

    def issue(step):
        slot = step % 2

        def body(r0, _):
            for u in range(DMA_ISSUE_UNROLL):
                r = r0 * DMA_ISSUE_UNROLL + u
                for k in range(MOE_TOP_K):
                    src = pos_ref[(step * tm + r) * MOE_TOP_K + k]
                    pltpu.make_async_copy(ys_hbm.at[pl.ds(src, 1)], buf_ref.at[slot, k, pl.ds(r, 1)],
                                          sem.at[slot]).start()
            return 0

        lax.fori_loop(0, tm // DMA_ISSUE_UNROLL, body, 0)

    @pl.when(i == 0)
    def _():
        issue(i)

    @pl.when(i + 1 < n_steps)
    def _():
        issue(i + 1)

    slot = i % 2
    for k in range(MOE_TOP_K):
        pltpu.make_async_copy(ys_hbm.at[pl.ds(0, tm)], buf_ref.at[slot, k], sem.at[slot]).wait()
    o_ref[...] = x_ref[...] + (buf_ref[slot, 0] * gate_ref[:, 0:1] + buf_ref[slot, 1] * gate_ref[:, 1:2])


def _combine(x, gate, ys, pos, tm):
    m, d = x.shape
    grid_spec = pltpu.PrefetchScalarGridSpec(
        num_scalar_prefetch=1,
        grid=(m // tm,),
        in_specs=[pl.BlockSpec((tm, d), lambda i, pos: (i, 0)),
                  pl.BlockSpec((tm, LANES), lambda i, pos: (i, 0)),
                  pl.BlockSpec(memory_space=pl.ANY)],
        out_specs=pl.BlockSpec((tm, d), lambda i, pos: (i, 0)),
        scratch_shapes=[pltpu.VMEM((2, MOE_TOP_K, tm, d), F32), pltpu.SemaphoreType.DMA((2,))],
    )
    return pl.pallas_call(
        functools.partial(_combine_body, tm=tm),
        grid_spec=grid_spec,
        out_shape=jax.ShapeDtypeStruct((m, d), F32),
        compiler_params=_cparams("arbitrary"),
        name="moe_combine",
    )(pos, x, gate, ys)


def _moe_plan(eid, n_tokens):
    a = n_tokens * MOE_TOP_K
    eid = eid.reshape(a)
    n_blocks = (a + N_EXPERTS * (MOE_BM - 1)) // MOE_BM
    n_slots = n_blocks * MOE_BM
    n_groups = n_blocks // MOE_GROUP_BLOCKS + N_EXPERTS
    onehot = (eid[:, None] == jnp.arange(N_EXPERTS, dtype=jnp.int32)[None, :]).astype(jnp.int32)
    rank = jnp.take_along_axis(jnp.cumsum(onehot, axis=0), eid[:, None], axis=1)[:, 0] - 1
    counts = jnp.sum(onehot, axis=0)
    blocks_e = (counts + MOE_BM - 1) // MOE_BM
    blk_start = jnp.cumsum(blocks_e) - blocks_e
    pos = (blk_start[eid] * MOE_BM + rank).astype(jnp.int32)
    tok = jnp.arange(a, dtype=jnp.int32) // MOE_TOP_K
    slot_tok = jnp.zeros((n_slots,), jnp.int32).at[pos].set(tok, unique_indices=True)
    groups_e = (blocks_e + MOE_GROUP_BLOCKS - 1) // MOE_GROUP_BLOCKS
    grp_end = jnp.cumsum(groups_e)
    grp_start = grp_end - groups_e
    gidx = jnp.arange(n_groups, dtype=jnp.int32)
    ge = jnp.minimum(jnp.searchsorted(grp_end, gidx, side="right"), N_EXPERTS - 1).astype(jnp.int32)
    local = gidx - grp_start[ge]
    active = gidx < grp_end[-1]
    gb0 = jnp.where(active, blk_start[ge] + local * MOE_GROUP_BLOCKS, 0).astype(jnp.int32)
    gnb = jnp.where(active, jnp.minimum(blocks_e[ge] - local * MOE_GROUP_BLOCKS, MOE_GROUP_BLOCKS),
                    0).astype(jnp.int32)
    last_e = ge[jnp.maximum(grp_end[-1] - 1, 0)]
    ge = jnp.where(active, ge, last_e).astype(jnp.int32)
    n_used = jnp.sum(blocks_e).astype(jnp.int32).reshape(1)
    return pos, slot_tok, ge, gb0, gnb, n_used, n_slots


def _layer(x, mem, positions, attn_norm_g, mem_norm_g, w_in, fox_forget_b, fox_q_norm_g,
           fox_k_norm_g, dsa_q_norm_g, dsa_k_norm_g, idx_k_norm_g, cross_q_norm_g, cross_k_norm_g,
           w_mem_kv, w_branch_fox, w_branch_dsa, w_branch_cross, w_out, ffn_norm_g,
           router_group_w, router_group_b, router_expert_w, router_expert_b,
           expert_w_gate, expert_w_up, expert_w_down):
    b, s, d = x.shape
    n = b * s
    n_mem = mem.shape[1]
    fw, dw, cw = FOX_HEADS * HEAD_DIM, DSA_HEADS * HEAD_DIM, CROSS_HEADS * HEAD_DIM
    iqw = IDX_HEADS * IDX_DIM
    tm = min(512, n)
    x2d = x.reshape(n, d)

    sizes = (fw, fw, fw, FOX_HEADS, dw, dw, dw, iqw, IDX_HEADS, IDX_DIM, cw, 3 * d)
    offs = [0]
    for sz in sizes:
        offs.append(offs[-1] + sz)
    (c_fq, c_fk, c_fv, c_fl, c_dq, c_dk, c_dv, c_iq, c_iw, c_ik, c_cq, c_g) = range(12)

    w_t = jnp.transpose(w_in, (2, 0, 1)).reshape(offs[-1], d)
    order = (c_dq, c_dk, c_fv, c_dv, c_fq, c_fk, c_cq, c_iq, c_g)
    row0, r = {}, 0
    for c in order:
        row0[c] = r
        r += sizes[c]
    w_all = jnp.concatenate([w_t[offs[c]:offs[c + 1]] for c in order], axis=0).astype(BF16)
    small_rows = [w_t[offs[c]:offs[c + 1]] for c in (c_ik, c_iw, c_fl)]
    small_rows.append(jnp.zeros((LANES - IDX_DIM - IDX_HEADS - FOX_HEADS, d), F32))
    w_small = jnp.concatenate(small_rows, axis=0).astype(BF16)
    q_unit = (HEAD_DIM ** -0.5) * LOG2E
    g_qk1 = jnp.concatenate([jnp.tile(fox_q_norm_g * q_unit, FOX_HEADS),
                             jnp.tile(fox_k_norm_g, FOX_HEADS),
                             jnp.tile(cross_q_norm_g * q_unit, CROSS_HEADS)])
    g_qk2 = jnp.concatenate([jnp.tile(dsa_q_norm_g * q_unit, DSA_HEADS),
                             jnp.tile(dsa_k_norm_g, DSA_HEADS)])
    g_ik = jnp.concatenate([idx_k_norm_g, jnp.zeros((LANES - IDX_DIM,), F32)])
    fb_lane = jnp.zeros((LANES,), F32).at[SMALL_F:SMALL_F + FOX_HEADS].set(fox_forget_b)

    pos_col = positions.reshape(n, 1).astype(F32)
    cd, sd, ci, si = _rope_tables(pos_col, tm)

    h = _rmsnorm(x2d, attn_norm_g, BF16, tm)
    qk1 = _proj(h, w_all, "norm", BF16, tm, 1024, gains=g_qk1, w_rows=(row0[c_fq], 2 * fw + cw))
    qk2 = _proj(h, w_all, "norm_rope", BF16, tm, 768, gains=g_qk2, tables=(cd, sd),
                w_rows=(row0[c_dq], 2 * dw))
    vv = _proj(h, w_all, "raw", BF16, tm, 768, w_rows=(row0[c_fv], fw + dw))
    iq = _proj(h, w_all, "rope_idx", BF16, tm, 1024, tables=(ci, si), w_rows=(row0[c_iq], iqw))
    small = _proj(h, w_small, "raw", F32, tm, LANES, w_rows=(0, LANES))
    ika, ikb, wl, lf_col = _small(small, g_ik, fb_lane, ci, si, tm)

    tq, tk = min(ATT_TQ, s), min(ATT_TK, s)
    assert tq == tk and s % tq == 0
    cum_col = _forget_cumsum(lf_col.reshape(b, s, LANES))
    qk1 = qk1.reshape(b, s, 2 * fw + cw)
    qk2 = qk2.reshape(b, s, 2 * dw)
    q_aug, k_aug = _fox_pack(qk1, cum_col, FOX_HEADS, min(512, s))
    vt = jnp.transpose(vv.reshape(b, s // tk, tk, fw + dw), (0, 1, 3, 2))
    o_fox = _attention(q_aug, 0, k_aug, 0, vt, 0, (), "fox", FOX_HEADS, AUG_DIM, tq, tk)

    topk = min(INDEX_TOPK, s // 4)
    wt = jnp.transpose(wl.reshape(b, s, LANES)[:, :, :IDX_HEADS], (0, 2, 1))
    bias = _select_bias(iq.reshape(b, s, iqw), ika.reshape(b, s, LANES), ikb.reshape(b, s, LANES),
                        wt, topk, tq, tk)
    o_dsa = _attention(qk2, 0, qk2, 1, vt, 1, (bias,), "dsa", DSA_HEADS, HEAD_DIM, tq, tk)

    tmm = min(512, b * n_mem)
    m_n = _rmsnorm(mem.reshape(b * n_mem, d), mem_norm_g, BF16, tmm)
    w_kv = w_mem_kv.astype(BF16)
    mk = _proj(m_n, w_kv[:, :cw], "norm", BF16, tmm, cw, gains=jnp.tile(cross_k_norm_g, CROSS_HEADS))
    mv = _proj(m_n, w_kv[:, cw:], "raw", BF16, tmm, cw)
    o_cross = _cross_attention(qk1, (2 * fw) // cw, mk.reshape(b, n_mem, cw), mv.reshape(b, n_mem, cw),
                               min(512, s))

    x_mid = _merge(h, o_fox.reshape(n, fw), o_dsa.reshape(n, dw), o_cross.reshape(n, cw), x2d,
                   w_all, row0[c_g], w_branch_fox.astype(BF16), w_branch_dsa.astype(BF16),
                   w_branch_cross.astype(BF16), w_out.astype(BF16), tm, MERGE_TN)

    w_router = jnp.concatenate(
        [router_group_w, router_expert_w, jnp.zeros((d, LANES - N_GROUPS - N_EXPERTS), F32)],
        axis=1).astype(BF16)
    b_router = jnp.concatenate(
        [router_group_b, router_expert_b, jnp.zeros((LANES - N_GROUPS - N_EXPERTS,), F32)])
    h2, eid, gate = _router(x_mid, ffn_norm_g, w_router, b_router, tm)
    pos, slot_tok, ge, gb0, gnb, n_used, n_slots = _moe_plan(eid[:, :MOE_TOP_K], n)
    ys = _experts(h2, slot_tok, ge, gb0, gnb, n_used, expert_w_gate, expert_w_up, expert_w_down,
                  n_slots)
    out = _combine(x_mid, gate, ys, pos, min(256, n))
    return out.reshape(b, s, d)


def kernel(x, mem, positions, attn_norm_g, mem_norm_g, w_in, fox_forget_b, fox_q_norm_g, fox_k_norm_g, dsa_q_norm_g, dsa_k_norm_g, idx_k_norm_g, cross_q_norm_g, cross_k_norm_g, w_mem_kv, w_branch_fox, w_branch_dsa, w_branch_cross, w_out, ffn_norm_g, router_group_w, router_group_b, router_expert_w, router_expert_b, expert_w_gate, expert_w_up, expert_w_down):
    depth = w_in.shape[0]
    for layer in range(depth):
        x = _layer(
            x, mem, positions, attn_norm_g[layer], mem_norm_g[layer], w_in[layer:layer + 1],
            fox_forget_b[layer], fox_q_norm_g[layer], fox_k_norm_g[layer],
            dsa_q_norm_g[layer], dsa_k_norm_g[layer], idx_k_norm_g[layer],
            cross_q_norm_g[layer], cross_k_norm_g[layer], w_mem_kv[layer],
            w_branch_fox[layer], w_branch_dsa[layer], w_branch_cross[layer], w_out[layer],
            ffn_norm_g[layer], router_group_w[layer], router_group_b[layer],
            router_expert_w[layer], router_expert_b[layer], expert_w_gate[layer],
            expert_w_up[layer], expert_w_down[layer])
    return x
```

```python
import functools

import jax
import jax.numpy as jnp
from jax import lax
from jax.experimental import pallas as pl
from jax.experimental.pallas import tpu as pltpu

F32 = jnp.float32
BF16 = jnp.bfloat16

LANES = 128
HEAD_DIM = 128
FOX_HEADS = 6
DSA_HEADS = 6
CROSS_HEADS = 4
IDX_HEADS = 16
IDX_DIM = 64
ROPE_THETA = 500000.0
ROT_DIM = HEAD_DIM // 4
IDX_ROT_DIM = IDX_DIM // 4
INDEX_TOPK = 256
N_GROUPS = 4
EXPERTS_PER_GROUP = 8
N_EXPERTS = N_GROUPS * EXPERTS_PER_GROUP
MOE_TOP_K = 2
EPS = 1e-6

NEG = -1e30
INT_MIN = -(2 ** 31)
VMEM_LIMIT = 56 * 1024 * 1024

ATT_TQ = 256
ATT_TK = 256
ATT_HEADS_PER_LOOP = 6
PROJ_TM = 1024
MERGE_TN = 512
MERGE_SUB = 256
MOE_BM = 256
MOE_GROUP_BLOCKS = 4
MOE_FC = 512
DMA_ISSUE_UNROLL = 8


def _cparams(*sem):
    return pltpu.CompilerParams(dimension_semantics=sem, vmem_limit_bytes=VMEM_LIMIT)


def _rmsnorm_body(x_ref, g_ref, o_ref):
    x = x_ref[...]
    r = lax.rsqrt(jnp.mean(x * x, axis=-1, keepdims=True) + EPS)
    o_ref[...] = ((x * r) * g_ref[...]).astype(o_ref.dtype)


def _rmsnorm(x2d, g, out_dtype, tm):
    m, d = x2d.shape
    return pl.pallas_call(
        _rmsnorm_body,
        grid=(m // tm,),
        in_specs=[pl.BlockSpec((tm, d), lambda i: (i, 0)),
                  pl.BlockSpec((1, d), lambda i: (0, 0))],
        out_specs=pl.BlockSpec((tm, d), lambda i: (i, 0)),
        out_shape=jax.ShapeDtypeStruct((m, d), out_dtype),
        compiler_params=_cparams("parallel"),
        name="rmsnorm",
    )(x2d, g.reshape(1, d))


W_PREP_ROWS = 128


def _w_prep_body(src_ref, w_ref, o_ref):
    del src_ref
    o_ref[...] = w_ref[:, 0, :].astype(o_ref.dtype)


def _w_prep(w_t, src_rows):
    _, _, k = w_t.shape
    n_chunks = src_rows.shape[0]
    grid_spec = pltpu.PrefetchScalarGridSpec(
        num_scalar_prefetch=1,
        grid=(n_chunks,),
        in_specs=[pl.BlockSpec((pl.Element(W_PREP_ROWS), pl.Element(1), pl.Element(k)),
                               lambda c, src: (src[c], 0, 0))],
        out_specs=pl.BlockSpec((W_PREP_ROWS, k), lambda c, src: (c, 0)),
    )
    return pl.pallas_call(
        _w_prep_body,
        grid_spec=grid_spec,
        out_shape=jax.ShapeDtypeStruct((n_chunks * W_PREP_ROWS, k), BF16),
        compiler_params=_cparams("parallel"),
        name="w_prep",
    )(src_rows, w_t)


IDX_FREQ_LANE = 32


def _rope_table_body(pos_ref, f_ref, cd_ref, sd_ref, ci_ref, si_ref):
    ang = pos_ref[...] * f_ref[...]
    c = jnp.cos(ang)
    s = jnp.sin(ang)
    lane = lax.broadcasted_iota(jnp.int32, c.shape, 1)
    hd = ROT_DIM // 2
    cd_ref[...] = jnp.where(lane < hd, c, jnp.where(lane < ROT_DIM, pltpu.roll(c, hd, 1), 1.0))
    sd_ref[...] = jnp.where(lane < hd, -s, jnp.where(lane < ROT_DIM, pltpu.roll(s, hd, 1), 0.0))
    m = lane & (IDX_DIM - 1)
    hi = IDX_ROT_DIM // 2
    ci_src = pltpu.roll(c, LANES - IDX_FREQ_LANE, 1)
    si_src = pltpu.roll(s, LANES - IDX_FREQ_LANE, 1)
    ci_ref[...] = jnp.where(m < IDX_ROT_DIM, ci_src, 1.0)
    si_ref[...] = jnp.where(m < hi, -si_src, jnp.where(m < IDX_ROT_DIM, si_src, 0.0))


def _rope_tables(pos_col, tm):
    n = pos_col.shape[0]

    def inv_freq(rot_dim):
        half = rot_dim // 2
        return jnp.power(jnp.float32(ROPE_THETA), -jnp.arange(half, dtype=F32) * 2.0 / rot_dim)

    fi = jnp.tile(inv_freq(IDX_ROT_DIM), 2)
    freq = jnp.zeros((LANES,), F32).at[:ROT_DIM // 2].set(inv_freq(ROT_DIM))
    for base in (IDX_FREQ_LANE, IDX_FREQ_LANE + IDX_DIM):
        freq = freq.at[base:base + IDX_ROT_DIM].set(fi)
    tab = pl.BlockSpec((tm, LANES), lambda i: (i, 0))
    return pl.pallas_call(
        _rope_table_body,
        grid=(n // tm,),
        in_specs=[pl.BlockSpec((tm, 1), lambda i: (i, 0)), pl.BlockSpec((1, LANES), lambda i: (0, 0))],
        out_specs=[tab] * 4,
        out_shape=[jax.ShapeDtypeStruct((n, LANES), F32)] * 4,
        compiler_params=_cparams("parallel"),
        name="rope_tables",
    )(pos_col, freq.reshape(1, LANES))


def _apply_rope(a, c, s, half, period):
    lane = lax.broadcasted_iota(jnp.int32, a.shape, 1)
    first = (lane & (period - 1)) < half
    partner = jnp.where(first, pltpu.roll(a, LANES - half, 1), pltpu.roll(a, half, 1))
    return a * c + partner * s


def _proj_body(*refs, mode, n_lane_groups, w_rows):
    h_ref, w_ref = refs[0], refs[1]
    o_ref = refs[-1]

    if w_rows:
        acc = lax.dot_general(h_ref[...], w_ref[...], (((1,), (1,)), ((), ())),
                              preferred_element_type=F32)
    else:
        acc = jnp.dot(h_ref[...], w_ref[...], preferred_element_type=F32)
    if mode == "raw":
        o_ref[...] = acc.astype(o_ref.dtype)
        return
    for g in range(n_lane_groups):
        sl = slice(g * LANES, (g + 1) * LANES)
        a = acc[:, sl]
        if mode in ("norm", "norm_rope"):
            g_ref = refs[2]
            a = (a * lax.rsqrt(jnp.mean(a * a, axis=-1, keepdims=True) + EPS)) * g_ref[:, sl]
        if mode == "norm_rope":
            a = _apply_rope(a, refs[3][...], refs[4][...], ROT_DIM // 2, HEAD_DIM)
        if mode == "rope_idx":
            a = _apply_rope(a, refs[2][...], refs[3][...], IDX_ROT_DIM // 2, IDX_DIM)
        o_ref[:, sl] = a.astype(o_ref.dtype)


def _proj_t_body(h_ref, w_ref, o_ref, *, tk):
    acc = lax.dot_general(w_ref[...], h_ref[...], (((1,), (1,)), ((), ())),
                          preferred_element_type=F32)
    for q in range(o_ref.shape[0]):
        o_ref[q] = acc[:, q * tk:(q + 1) * tk].astype(o_ref.dtype)


def _proj_t(h, w, w_rows, tm, tn, tk):
    m, k = h.shape
    row0, n = w_rows
    assert row0 % tn == 0 and n % tn == 0 and tm % tk == 0
    return pl.pallas_call(
        functools.partial(_proj_t_body, tk=tk),
        grid=(n // tn, m // tm),
        in_specs=[pl.BlockSpec((tm, k), lambda j, i: (i, 0)),
                  pl.BlockSpec((tn, k), lambda j, i: (row0 // tn + j, 0))],
        out_specs=pl.BlockSpec((tm // tk, tn, tk), lambda j, i: (i, j, 0)),
        out_shape=jax.ShapeDtypeStruct((m // tk, n, tk), BF16),
        compiler_params=_cparams("parallel", "parallel"),
        name="proj_t",
    )(h, w)


def _proj(h, w, mode, out_dtype, tm, tn, gains=None, tables=None, w_rows=None):
    m, k = h.shape
    if w_rows is None:
        n = w.shape[1]
        w_spec = pl.BlockSpec((k, tn), lambda j, i: (0, j))
    else:
        row0, n = w_rows
        assert row0 % tn == 0 and n % tn == 0
        w_spec = pl.BlockSpec((tn, k), lambda j, i: (row0 // tn + j, 0))
    in_specs = [pl.BlockSpec((tm, k), lambda j, i: (i, 0)), w_spec]
    args = [h, w]
    if gains is not None:
        in_specs.append(pl.BlockSpec((1, tn), lambda j, i: (0, j)))
        args.append(gains.reshape(1, n))
    if tables is not None:
        for t in tables:
            in_specs.append(pl.BlockSpec((tm, LANES), lambda j, i: (i, 0)))
            args.append(t)
    return pl.pallas_call(
        functools.partial(_proj_body, mode=mode, n_lane_groups=tn // LANES,
                          w_rows=w_rows is not None),
        grid=(n // tn, m // tm),
        in_specs=in_specs,
        out_specs=pl.BlockSpec((tm, tn), lambda j, i: (i, j)),
        out_shape=jax.ShapeDtypeStruct((m, n), out_dtype),
        compiler_params=_cparams("parallel", "parallel"),
        name="proj_" + mode,
    )(*args)


SMALL_IK = 0
SMALL_IW = 64
SMALL_F = 80


def _small_body(x_ref, gk_ref, fb_ref, c_ref, s_ref, ika_ref, ikb_ref, wl_ref, lf_ref):
    x = x_ref[...]
    lane = lax.broadcasted_iota(jnp.int32, x.shape, 1)
    is_ik = lane < IDX_DIM
    ik = jnp.where(is_ik, x, 0.0)
    ms = jnp.sum(ik * ik, axis=-1, keepdims=True) * (1.0 / IDX_DIM)
    ik = (ik * lax.rsqrt(ms + EPS)) * gk_ref[...]
    ik = _apply_rope(ik, c_ref[...], s_ref[...], IDX_ROT_DIM // 2, IDX_DIM)
    ik = jnp.where(is_ik, ik, 0.0)
    ika_ref[...] = ik.astype(BF16)
    ikb_ref[...] = pltpu.roll(ik, IDX_DIM, 1).astype(BF16)
    wl_ref[...] = pltpu.roll(x, LANES - SMALL_IW, 1) * (2.0 ** -5)
    z = x + fb_ref[...]
    log_f = jnp.minimum(z, 0.0) - jnp.log1p(jnp.exp(-jnp.abs(z)))
    lf_ref[...] = pltpu.roll(log_f, LANES - SMALL_F, 1)


def _small(x, idx_k_gain_lane, forget_b_lane, ci, si, tm):
    n = x.shape[0]
    row = pl.BlockSpec((1, LANES), lambda i: (0, 0))
    tab = pl.BlockSpec((tm, LANES), lambda i: (i, 0))
    return pl.pallas_call(
        _small_body,
        grid=(n // tm,),
        in_specs=[tab, row, row, tab, tab],
        out_specs=[tab, tab, tab, tab],
        out_shape=[jax.ShapeDtypeStruct((n, LANES), BF16), jax.ShapeDtypeStruct((n, LANES), BF16),
                   jax.ShapeDtypeStruct((n, LANES), F32), jax.ShapeDtypeStruct((n, LANES), F32)],
        compiler_params=_cparams("parallel"),
        name="small_cols",
    )(x, idx_k_gain_lane.reshape(1, LANES), forget_b_lane.reshape(1, LANES), ci, si)


def _cumsum_body(col_ref, ccol_ref, *, seq):
    c = col_ref[...]
    ci = lax.broadcasted_iota(jnp.int32, c.shape, 0)
    sh = 1
    while sh < seq:
        c = c + jnp.where(ci >= sh, pltpu.roll(c, sh, 0), 0.0)
        sh *= 2
    ccol_ref[...] = c


def _forget_cumsum(lf_col):
    b, s, _ = lf_col.shape
    return pl.pallas_call(
        functools.partial(_cumsum_body, seq=s),
        grid=(b,),
        in_specs=[pl.BlockSpec((None, s, LANES), lambda i: (i, 0, 0))],
        out_specs=pl.BlockSpec((None, s, LANES), lambda i: (i, 0, 0)),
        out_shape=jax.ShapeDtypeStruct((b, s, LANES), F32),
        compiler_params=_cparams("parallel"),
        name="forget_cumsum",
    )(lf_col)


AUG_DIM = 2 * HEAD_DIM
LOG2E = 1.4426950408889634


def _split3(x):
    hi = x.astype(BF16).astype(F32)
    r1 = x - hi
    mid = r1.astype(BF16).astype(F32)
    lo = (r1 - mid).astype(BF16).astype(F32)
    return hi, mid, lo


def _fox_pack_body(q_ref, k_ref, cum_ref, qa_ref, ka_ref, *, n_heads):
    rows = q_ref.shape[0]
    lane = lax.broadcasted_iota(jnp.int32, (rows, LANES), 1)
    for h in range(n_heads):
        c = jnp.broadcast_to(cum_ref[:, h:h + 1], (rows, LANES)) * LOG2E
        qh, qm, ql = _split3(c)
        kh, km, kl = _split3(-c)
        q_aug = jnp.where(lane == 0, qh, jnp.where(lane == 1, qm, jnp.where(
            lane == 2, ql, jnp.where(lane < 6, 1.0, 0.0))))
        k_aug = jnp.where(lane < 3, 1.0, jnp.where(lane == 3, kh, jnp.where(
            lane == 4, km, jnp.where(lane == 5, kl, 0.0))))
        base = h * AUG_DIM
        qa_ref[:, base:base + HEAD_DIM] = q_ref[:, h * HEAD_DIM:(h + 1) * HEAD_DIM]
        ka_ref[:, base:base + HEAD_DIM] = k_ref[:, h * HEAD_DIM:(h + 1) * HEAD_DIM]
        qa_ref[:, base + HEAD_DIM:base + AUG_DIM] = q_aug.astype(BF16)
        ka_ref[:, base + HEAD_DIM:base + AUG_DIM] = k_aug.astype(BF16)


def _fox_pack(qk, cum_col, n_heads, ts):
    b, s, _ = qk.shape
    w = n_heads * HEAD_DIM
    wa = n_heads * AUG_DIM
    return pl.pallas_call(
        functools.partial(_fox_pack_body, n_heads=n_heads),
        grid=(b, s // ts),
        in_specs=[pl.BlockSpec((None, ts, w), lambda bi, i: (bi, i, 0)),
                  pl.BlockSpec((None, ts, w), lambda bi, i: (bi, i, 1)),
                  pl.BlockSpec((None, ts, LANES), lambda bi, i: (bi, i, 0))],
        out_specs=[pl.BlockSpec((None, ts, wa), lambda bi, i: (bi, i, 0)),
                   pl.BlockSpec((None, ts, wa), lambda bi, i: (bi, i, 0))],
        out_shape=[jax.ShapeDtypeStruct((b, s, wa), BF16)] * 2,
        compiler_params=_cparams("parallel", "parallel"),
        name="fox_pack",
    )(qk, qk, cum_col)


def _attn_body(q_ref, k_ref, vt_ref, *rest, tq, tk, n_heads, kdim, mode, heads_per_loop):
    o_ref = rest[-1]
    qi = pl.program_id(1)
    key0 = lax.broadcasted_iota(jnp.int32, (tk, tq), 0)
    qry = qi * tq + lax.broadcasted_iota(jnp.int32, (tk, tq), 1)
    def scores(h, j):
        off = pl.multiple_of(j * tk, tk)
        q = q_ref[:, h * kdim:(h + 1) * kdim]
        k = k_ref[pl.ds(off, tk), h * kdim:(h + 1) * kdim]
        return lax.dot_general(k, q, (((1,), (1,)), ((), ())), preferred_element_type=F32)

    def flush(h, j, p, alpha, acc):
        vt = vt_ref[j, h * HEAD_DIM:(h + 1) * HEAD_DIM, :]
        return alpha * acc + jnp.dot(vt, p, preferred_element_type=F32)

    def softmax_step(h, j, m, l, masked):
        s = scores(h, j)
        if mode == "dsa":
            s = s + rest[0][j]
        if masked:
            s = jnp.where(key0 + j * tk <= qry, s, NEG)
        m_new = jnp.maximum(m, jnp.max(s, axis=0, keepdims=True))
        alpha = jnp.exp2(m - m_new)
        p = jnp.exp2(s - m_new)
        l = alpha * l + jnp.sum(p, axis=0, keepdims=True)
        return p.astype(BF16), alpha, m_new, l

    last = qi
    for h0 in range(0, n_heads, heads_per_loop):
        heads = range(h0, h0 + heads_per_loop)

        def step(j, carries, masked=False, heads=heads):
            prev = jnp.maximum(j - 1, 0)
            out = []
            for h, (p, alpha, m, l, acc) in zip(heads, carries):
                acc = flush(h, prev, p, alpha, acc)
                p, alpha, m, l = softmax_step(h, j, m, l, masked)
                out.append((p, alpha, m, l, acc))
            return tuple(out)

        carries = tuple(
            (jnp.zeros((tk, tq), BF16), jnp.ones((1, tq), F32), jnp.full((1, tq), NEG, F32),
             jnp.zeros((1, tq), F32), jnp.zeros((HEAD_DIM, tq), F32)) for h in heads)
        n_loop = last if mode == "fox" else last + 1
        carries = lax.fori_loop(0, n_loop, step, carries)
        if mode == "fox":
            carries = step(last, carries, masked=True)
        for h, (p, alpha, m, l, acc) in zip(heads, carries):
            acc = flush(h, last, p, alpha, acc)
            o_ref[:, h * HEAD_DIM:(h + 1) * HEAD_DIM] = jnp.transpose(acc / l).astype(o_ref.dtype)


def _attention(q_arr, q_blk, k_arr, k_blk, vt_arr, v_blk, extras, mode, n_heads, kdim, tq, tk):
    b, s, _ = q_arr.shape
    w = n_heads * HEAD_DIM
    wk = n_heads * kdim
    in_specs = [pl.BlockSpec((None, tq, wk), lambda bi, qi: (bi, qi, q_blk)),
                pl.BlockSpec((None, s, wk), lambda bi, qi: (bi, 0, k_blk)),
                pl.BlockSpec((None, s // tk, w, tk), lambda bi, qi: (bi, 0, v_blk, 0))]
    if mode == "dsa":
        in_specs += [pl.BlockSpec((None, None, s // tk, tk, tq), lambda bi, qi: (bi, qi, 0, 0, 0))]
    return pl.pallas_call(
        functools.partial(_attn_body, tq=tq, tk=tk, n_heads=n_heads, kdim=kdim, mode=mode,
                          heads_per_loop=ATT_HEADS_PER_LOOP),
        grid=(b, s // tq),
        in_specs=in_specs,
        out_specs=pl.BlockSpec((None, tq, w), lambda bi, qi: (bi, qi, 0)),
        out_shape=jax.ShapeDtypeStruct((b, s, w), BF16),
        compiler_params=_cparams("parallel", "parallel"),
        name="attn_" + mode,
    )(q_arr, k_arr, vt_arr, *extras)


def _select_body(iq_ref, ika_ref, ikb_ref, wt_ref, o_ref, keys_ref, half_ref, *, tq, tk, n_chunks, topk):
    qi = pl.program_id(1)
    used = qi + 1
    key0 = lax.broadcasted_iota(jnp.int32, (tk, tq), 0)
    qry = qi * tq + lax.broadcasted_iota(jnp.int32, (tk, tq), 1)

    def score_chunk(c, _):
        off = pl.multiple_of(c * tk, tk)
        ka = ika_ref[pl.ds(off, tk), :]
        kb = ikb_ref[pl.ds(off, tk), :]
        acc = jnp.zeros((tk, tq), F32)
        for j in range(IDX_HEADS):
            pair = iq_ref[:, (j // 2) * LANES:(j // 2 + 1) * LANES]
            logit = lax.dot_general(ka if j % 2 == 0 else kb, pair, (((1,), (1,)), ((), ())),
                                    preferred_element_type=F32)
            acc = acc + wt_ref[j:j + 1, :] * jnp.maximum(logit, 0.0)
        bits = lax.bitcast_convert_type(acc, jnp.int32)
        key = jnp.where(bits < 0, bits ^ jnp.int32(0x7FFFFFFF), bits)
        key = jnp.where(key0 + off <= qry, key, jnp.int32(INT_MIN))
        keys_ref[c] = key
        half_ref[c] = lax.shift_right_arithmetic(key, 16).astype(jnp.int16)
        return 0

    lax.fori_loop(0, used, score_chunk, 0)

    one16 = jnp.int16(1)
    zero16 = jnp.int16(0)

    def count_ge(cand16):
        def count_chunk(c, cnt):
            ge = jnp.where(half_ref[c] >= cand16, one16, zero16)
            parts = [ge[r:r + 16] for r in range(0, tk, 16)]
            while len(parts) > 1:
                parts = [a + b for a, b in zip(parts[::2], parts[1::2])]
            return cnt + parts[0]

        cnt = lax.fori_loop(0, used, count_chunk, jnp.zeros((16, tq), jnp.int16))
        return jnp.sum(cnt.astype(jnp.int32), axis=0, keepdims=True)

    def search16(need):
        def bit_step(i, prefix):
            cand_u = prefix | jnp.left_shift(jnp.int32(1), 15 - i)
            cnt = count_ge((cand_u - 32768).astype(jnp.int16))
            return jnp.where(cnt >= need, cand_u, prefix)

        return lax.fori_loop(0, 16, bit_step, jnp.zeros((1, tq), jnp.int32)) - 32768

    hi = search16(topk)
    above = count_ge(jnp.minimum(hi + 1, 32767).astype(jnp.int16))
    above = jnp.where(hi >= 32767, 0, above)
    hi16 = hi.astype(jnp.int16)

    def low_chunk(c, _):
        low = (keys_ref[c] & jnp.int32(0xFFFF)) - 32768
        half_ref[c] = jnp.where(half_ref[c] == hi16, low.astype(jnp.int16), jnp.int16(-32768))
        return 0

    lax.fori_loop(0, used, low_chunk, 0)
    lo = search16(topk - above)
    thr = lax.shift_left(hi, 16) | (lo + 32768)
    thr = jnp.maximum(thr, jnp.int32(INT_MIN + 1))

    def count_keys(pred):
        def chunk(c, cnt):
            hit = jnp.where(pred(keys_ref[c]), 1.0, 0.0)
            return cnt + jnp.sum(hit.reshape(tk // 8, 8, tq), axis=0)

        cnt = lax.fori_loop(0, used, chunk, jnp.zeros((8, tq), F32))
        return jnp.sum(cnt, axis=0, keepdims=True)

    tied = jnp.max(count_keys(lambda k: k >= thr)) > topk

    @pl.when(jnp.logical_not(tied))
    def _():
        def write_chunk(c, _):
            o_ref[c] = jnp.where(keys_ref[c] >= thr, 0.0, NEG)
            return 0

        lax.fori_loop(0, used, write_chunk, 0)

    @pl.when(tied)
    def _():
        need = topk - count_keys(lambda k: k > thr)
        earlier = jnp.where(lax.broadcasted_iota(jnp.int32, (tk, tk), 1)
                            < lax.broadcasted_iota(jnp.int32, (tk, tk), 0), 1.0, 0.0).astype(BF16)

        def write_chunk(c, seen):
            keys = keys_ref[c]
            tie = jnp.where(keys == thr, 1.0, 0.0)
            rank = jnp.dot(earlier, tie.astype(BF16), preferred_element_type=F32) + seen
            o_ref[c] = jnp.where(keys > thr, 0.0,
                                 jnp.where(keys == thr, jnp.where(rank < need, 0.0, NEG), NEG))
            return seen + jnp.sum(tie, axis=0, keepdims=True)

        lax.fori_loop(0, used, write_chunk, jnp.zeros((1, tq), F32))

    def fill_chunk(c, _):
        o_ref[c] = jnp.full((tk, tq), NEG, F32)
        return 0

    lax.fori_loop(used, n_chunks, fill_chunk, 0)


def _select_bias(iq, ika, ikb, wt, topk, tq, tk):
    b, s, _ = iq.shape
    n_chunks = s // tk
    return pl.pallas_call(
        functools.partial(_select_body, tq=tq, tk=tk, n_chunks=n_chunks, topk=topk),
        grid=(b, s // tq),
        in_specs=[pl.BlockSpec((None, tq, IDX_HEADS * IDX_DIM), lambda bi, qi: (bi, qi, 0)),
                  pl.BlockSpec((None, s, LANES), lambda bi, qi: (bi, 0, 0)),
                  pl.BlockSpec((None, s, LANES), lambda bi, qi: (bi, 0, 0)),
                  pl.BlockSpec((None, IDX_HEADS, tq), lambda bi, qi: (bi, 0, qi))],
        out_specs=pl.BlockSpec((None, None, n_chunks, tk, tq), lambda bi, qi: (bi, qi, 0, 0, 0)),
        out_shape=jax.ShapeDtypeStruct((b, s // tq, n_chunks, tk, tq), F32),
        scratch_shapes=[pltpu.VMEM((n_chunks, tk, tq), jnp.int32),
                        pltpu.VMEM((n_chunks, tk, tq), jnp.int16)],
        compiler_params=_cparams("parallel", "parallel"),
        name="index_select",
    )(iq, ika, ikb, wt)


def _cross_body(q_ref, k_ref, v_ref, o_ref, *, n_heads):
    for h in range(n_heads):
        hs = slice(h * HEAD_DIM, (h + 1) * HEAD_DIM)
        s = lax.dot_general(q_ref[:, hs], k_ref[:, hs], (((1,), (1,)), ((), ())),
                            preferred_element_type=F32)
        p = jnp.exp2(s - jnp.max(s, axis=-1, keepdims=True))
        l = jnp.sum(p, axis=-1, keepdims=True)
        o = jnp.dot(p.astype(BF16), v_ref[:, hs], preferred_element_type=F32)
        o_ref[:, hs] = (o / l).astype(o_ref.dtype)


def _cross_attention(q_arr, q_blk, mk, mv, tq):
    b, s, _ = q_arr.shape
    n_mem = mk.shape[1]
    w = CROSS_HEADS * HEAD_DIM
    return pl.pallas_call(
        functools.partial(_cross_body, n_heads=CROSS_HEADS),
        grid=(b, s // tq),
        in_specs=[pl.BlockSpec((None, tq, w), lambda bi, qi: (bi, qi, q_blk)),
                  pl.BlockSpec((None, n_mem, w), lambda bi, qi: (bi, 0, 0)),
                  pl.BlockSpec((None, n_mem, w), lambda bi, qi: (bi, 0, 0))],
        out_specs=pl.BlockSpec((None, tq, w), lambda bi, qi: (bi, qi, 0)),
        out_shape=jax.ShapeDtypeStruct((b, s, w), BF16),
        compiler_params=_cparams("parallel", "parallel"),
        name="cross_attn",
    )(q_arr, mk, mv)


def _sigmoid(x):
    return 1.0 / (1.0 + jnp.exp(-x))


def _merge_body(h_ref, of_ref, od_ref, oc_ref, x_ref, wgf_ref, wgd_ref, wgc_ref,
                wbf_ref, wbd_ref, wbc_ref, wo_ref, o_ref):
    j = pl.program_id(1)
    h = h_ref[...]
    tn = wo_ref.shape[0]
    sub = min(tn, MERGE_SUB)

    def gate_logits(wg_ref, cs):
        return lax.dot_general(h, wg_ref[cs, :], (((1,), (1,)), ((), ())), preferred_element_type=F32)

    part = None
    for c in range(tn // sub):
        cs = slice(c * sub, (c + 1) * sub)
        zf, zd, zc = gate_logits(wgf_ref, cs), gate_logits(wgd_ref, cs), gate_logits(wgc_ref, cs)
        bf = jnp.dot(of_ref[...], wbf_ref[:, cs], preferred_element_type=F32)
        bd = jnp.dot(od_ref[...], wbd_ref[:, cs], preferred_element_type=F32)
        bc = jnp.dot(oc_ref[...], wbc_ref[:, cs], preferred_element_type=F32)
        merged = (_sigmoid(zf) * bf + _sigmoid(zd) * bd) + _sigmoid(zc) * bc
        p = jnp.dot(merged.astype(BF16), wo_ref[cs, :], preferred_element_type=F32)
        part = p if part is None else part + p

    @pl.when(j == 0)
    def _():
        o_ref[...] = x_ref[...] + part

    @pl.when(j != 0)
    def _():
        o_ref[...] += part


def _merge(h, o_fox, o_dsa, o_cross, x, w_gates, gate_row0, wbf, wbd, wbc, w_out, tm, tn):
    m, d = x.shape
    nj = d // tn
    assert gate_row0 % tn == 0
    g0 = gate_row0 // tn
    row = lambda i, j: (i, 0)
    in_specs = [pl.BlockSpec((tm, d), row),
                pl.BlockSpec((tm, o_fox.shape[1]), row),
                pl.BlockSpec((tm, o_dsa.shape[1]), row),
                pl.BlockSpec((tm, o_cross.shape[1]), row),
                pl.BlockSpec((tm, d), row),
                pl.BlockSpec((tn, d), lambda i, j: (g0 + j, 0)),
                pl.BlockSpec((tn, d), lambda i, j: (g0 + nj + j, 0)),
                pl.BlockSpec((tn, d), lambda i, j: (g0 + 2 * nj + j, 0)),
                pl.BlockSpec((wbf.shape[0], tn), lambda i, j: (0, j)),
                pl.BlockSpec((wbd.shape[0], tn), lambda i, j: (0, j)),
                pl.BlockSpec((wbc.shape[0], tn), lambda i, j: (0, j)),
                pl.BlockSpec((tn, d), lambda i, j: (j, 0))]
    return pl.pallas_call(
        _merge_body,
        grid=(m // tm, nj),
        in_specs=in_specs,
        out_specs=pl.BlockSpec((tm, d), row),
        out_shape=jax.ShapeDtypeStruct((m, d), F32),
        compiler_params=_cparams("parallel", "arbitrary"),
        name="gated_merge",
    )(h, o_fox, o_dsa, o_cross, x, w_gates, w_gates, w_gates, wbf, wbd, wbc, w_out)


def _router_body(x_ref, g_ref, w_ref, b_ref, h_ref, eid_ref, gate_ref):
    x = x_ref[...]
    r = lax.rsqrt(jnp.mean(x * x, axis=-1, keepdims=True) + EPS)
    h = (x * r) * g_ref[...]
    hb = h.astype(BF16)
    bits = lax.bitcast_convert_type(hb.astype(F32), jnp.int32)
    half = bits.shape[1] // 2
    h_ref[...] = lax.shift_right_logical(bits[:, :half], 16) | (bits[:, half:] & jnp.int32(-65536))
    logits = jnp.dot(hb, w_ref[...], preferred_element_type=F32) + b_ref[...]
    lane = lax.broadcasted_iota(jnp.int32, logits.shape, 1).astype(F32)

    def masked_max(mask):
        return jnp.max(jnp.where(mask, logits, -jnp.inf), axis=-1, keepdims=True)

    def first_lane(mask):
        return jnp.min(jnp.where(mask, lane, float(LANES)), axis=-1, keepdims=True)

    is_group = lane < N_GROUPS
    gmax = masked_max(is_group)
    grp = first_lane(is_group & (logits == gmax))
    denom = jnp.sum(jnp.where(is_group, jnp.exp(logits - gmax), 0.0), axis=-1, keepdims=True)
    p_sel = 1.0 / denom
    lo = N_GROUPS + grp * EXPERTS_PER_GROUP
    in_grp = (lane >= lo) & (lane < lo + EXPERTS_PER_GROUP)
    e1 = masked_max(in_grp)
    j1 = first_lane(in_grp & (logits == e1))
    rest = in_grp & (lane != j1)
    e2 = masked_max(rest)
    j2 = first_lane(rest & (logits == e2))
    t = jnp.exp(e2 - e1)
    g1 = p_sel * (1.0 / (1.0 + t))
    g2 = p_sel * (t / (1.0 + t))
    eid = jnp.where(lane == 0.0, j1 - N_GROUPS, jnp.where(lane == 1.0, j2 - N_GROUPS, 0.0))
    eid_ref[...] = eid.astype(jnp.int32)
    gate_ref[...] = jnp.where(lane == 0.0, g1, jnp.where(lane == 1.0, g2, 0.0))


def _router(x, g, w_router, b_router, tm):
    m, d = x.shape
    return pl.pallas_call(
        _router_body,
        grid=(m // tm,),
        in_specs=[pl.BlockSpec((tm, d), lambda i: (i, 0)),
                  pl.BlockSpec((1, d), lambda i: (0, 0)),
                  pl.BlockSpec((d, LANES), lambda i: (0, 0)),
                  pl.BlockSpec((1, LANES), lambda i: (0, 0))],
        out_specs=[pl.BlockSpec((tm, d // 2), lambda i: (i, 0)),
                   pl.BlockSpec((tm, LANES), lambda i: (i, 0)),
                   pl.BlockSpec((tm, LANES), lambda i: (i, 0))],
        out_shape=[jax.ShapeDtypeStruct((m, d // 2), jnp.int32),
                   jax.ShapeDtypeStruct((m, LANES), jnp.int32),
                   jax.ShapeDtypeStruct((m, LANES), F32)],
        compiler_params=_cparams("parallel"),
        name="moe_router",
    )(x, g.reshape(1, d), w_router, b_router.reshape(1, LANES))


def _expert_body(ge_ref, gb0_ref, gnb_ref, used_ref, tok_ref, h_hbm, wg_ref, wu_ref, wd_ref, ys_hbm,
                 xs_ref, xb_ref, acc_ref, wgb_ref, wub_ref, wdb_ref, sem_in, sem_out,
                 *, bm, n_f, n_blocks):
    g = pl.program_id(0)
    f = pl.program_id(1)
    nb = gnb_ref[g]
    blk0 = gb0_ref[g]

    @pl.when(jnp.logical_and(g == 0, f == 0))
    def _():
        acc_ref[pl.ds(0, bm)] = jnp.zeros((bm,) + acc_ref.shape[1:], F32)

        def tail_copy(i):
            return pltpu.make_async_copy(acc_ref.at[pl.ds(0, bm)], ys_hbm.at[pl.ds(i * bm, bm)],
                                         sem_out)

        def start_tail(i, _):
            tail_copy(i).start()
            return 0

        lax.fori_loop(used_ref[0], n_blocks, start_tail, 0)

        def wait_tail(i, _):
            tail_copy(i).wait()
            return 0

        lax.fori_loop(used_ref[0], n_blocks, wait_tail, 0)

    def gather(grp):
        slot = grp % 2
        base = gb0_ref[grp] * bm

        def issue(r0, _):
            for u in range(DMA_ISSUE_UNROLL):
                r = r0 * DMA_ISSUE_UNROLL + u
                pltpu.make_async_copy(h_hbm.at[pl.ds(tok_ref[base + r], 1)],
                                      xs_ref.at[slot, pl.ds(r, 1)], sem_in.at[slot]).start()
            return 0

        lax.fori_loop(0, gnb_ref[grp] * (bm // DMA_ISSUE_UNROLL), issue, 0)

    def block_copy_out(i):
        return pltpu.make_async_copy(acc_ref.at[pl.ds(i * bm, bm)],
                                     ys_hbm.at[pl.ds((blk0 + i) * bm, bm)], sem_out)

    @pl.when(jnp.logical_and(g == 0, f == 0))
    def _():
        gather(g)

    @pl.when(jnp.logical_and(nb > 0, f == 0))
    def _():
        slot = g % 2

        def land(i, _):
            rows = pl.ds(pl.multiple_of(i * bm, bm), bm)
            pltpu.make_async_copy(h_hbm.at[pl.ds(0, bm)], xs_ref.at[slot, rows], sem_in.at[slot]).wait()
            return 0

        lax.fori_loop(0, nb, land, 0)

        half = xb_ref.shape[1] // 2

        def unpack(i, _):
            rows = pl.ds(pl.multiple_of(i * bm, bm), bm)
            w = xs_ref[slot, rows, :]
            lo = lax.bitcast_convert_type(lax.shift_left(w, 16), F32)
            hi = lax.bitcast_convert_type(w & jnp.int32(-65536), F32)
            xb_ref[rows, :half] = lo.astype(BF16)
            xb_ref[rows, half:] = hi.astype(BF16)
            return 0

        lax.fori_loop(0, nb, unpack, 0)

    @pl.when(jnp.logical_and(f == 0, g + 1 < pl.num_programs(0)))
    def _():
        gather(g + 1)

    def wait_out(n):
        def body(i, _):
            pltpu.make_async_copy(acc_ref.at[pl.ds(0, bm)], ys_hbm.at[pl.ds(0, bm)], sem_out).wait()
            return 0

        lax.fori_loop(0, n, body, 0)

    @pl.when(jnp.logical_and(f == 0, g > 0))
    def _():
        wait_out(gnb_ref[jnp.maximum(g - 1, 0)])

    @pl.when(nb > 0)
    def _():
        wgb_ref[...] = wg_ref[...].astype(BF16)
        wub_ref[...] = wu_ref[...].astype(BF16)
        wdb_ref[...] = wd_ref[...].astype(BF16)

        def block(i, _):
            rows = pl.ds(pl.multiple_of(i * bm, bm), bm)
            xb = xb_ref[rows, :]
            a = jnp.dot(xb, wgb_ref[...], preferred_element_type=F32)
            u = jnp.dot(xb, wub_ref[...], preferred_element_type=F32)
            hid = (a * _sigmoid(a)) * u
            y = jnp.dot(hid.astype(BF16), wdb_ref[...], preferred_element_type=F32)

            @pl.when(f == 0)
            def _():
                acc_ref[rows, :] = y

            @pl.when(f != 0)
            def _():
                acc_ref[rows, :] += y

            return 0

        lax.fori_loop(0, nb, block, 0)

    @pl.when(jnp.logical_and(nb > 0, f == n_f - 1))
    def _():
        def start_out(i, _):
            block_copy_out(i).start()
            return 0

        lax.fori_loop(0, nb, start_out, 0)

        @pl.when(g == pl.num_programs(0) - 1)
        def _():
            wait_out(nb)


def _experts(h2, slot_tok, grp_e, grp_blk0, grp_nb, n_used, w_gate, w_up, w_down, n_slots):
    d = w_gate.shape[1]
    ff = w_gate.shape[2]
    n_f = ff // MOE_FC
    n_groups = grp_e.shape[0]
    rmax = MOE_GROUP_BLOCKS * MOE_BM

    def f_idx(g, f, gnb):
        return jnp.where(gnb[g] > 0, f, n_f - 1)

    grid_spec = pltpu.PrefetchScalarGridSpec(
        num_scalar_prefetch=5,
        grid=(n_groups, n_f),
        in_specs=[
            pl.BlockSpec(memory_space=pl.ANY),
            pl.BlockSpec((None, d, MOE_FC), lambda g, f, ge, gb0, gnb, *_: (ge[g], 0, f_idx(g, f, gnb))),
            pl.BlockSpec((None, d, MOE_FC), lambda g, f, ge, gb0, gnb, *_: (ge[g], 0, f_idx(g, f, gnb))),
            pl.BlockSpec((None, MOE_FC, d), lambda g, f, ge, gb0, gnb, *_: (ge[g], f_idx(g, f, gnb), 0)),
        ],
        out_specs=pl.BlockSpec(memory_space=pl.ANY),
        scratch_shapes=[pltpu.VMEM((2, rmax, d // 2), jnp.int32),
                        pltpu.VMEM((rmax, d), BF16),
                        pltpu.VMEM((rmax, d), F32),
                        pltpu.VMEM((d, MOE_FC), BF16),
                        pltpu.VMEM((d, MOE_FC), BF16),
                        pltpu.VMEM((MOE_FC, d), BF16),
                        pltpu.SemaphoreType.DMA((2,)),
                        pltpu.SemaphoreType.DMA(())],
    )
    return pl.pallas_call(
        functools.partial(_expert_body, bm=MOE_BM, n_f=n_f, n_blocks=n_slots // MOE_BM),
        grid_spec=grid_spec,
        out_shape=jax.ShapeDtypeStruct((n_slots, d), F32),
        compiler_params=_cparams("arbitrary", "arbitrary"),
        name="moe_experts",
    )(grp_e, grp_blk0, grp_nb, n_used, slot_tok, h2, w_gate, w_up, w_down)


def _combine_body(pos_ref, x_ref, gate_ref, ys_hbm, o_ref, buf_ref, sem, *, tm):
    i = pl.program_id(0)
    n_steps = pl.num_programs(0)

    def issue(step):
        slot = step % 2

        def body(r0, _):
            for u in range(DMA_ISSUE_UNROLL):
                r = r0 * DMA_ISSUE_UNROLL + u
                for k in range(MOE_TOP_K):
                    src = pos_ref[(step * tm + r) * MOE_TOP_K + k]
                    pltpu.make_async_copy(ys_hbm.at[pl.ds(src, 1)], buf_ref.at[slot, k, pl.ds(r, 1)],
                                          sem.at[slot]).start()
            return 0

        lax.fori_loop(0, tm // DMA_ISSUE_UNROLL, body, 0)

    @pl.when(i == 0)
    def _():
        issue(i)

    @pl.when(i + 1 < n_steps)
    def _():
        issue(i + 1)

    slot = i % 2
    for k in range(MOE_TOP_K):
        pltpu.make_async_copy(ys_hbm.at[pl.ds(0, tm)], buf_ref.at[slot, k], sem.at[slot]).wait()
    o_ref[...] = x_ref[...] + (buf_ref[slot, 0] * gate_ref[:, 0:1] + buf_ref[slot, 1] * gate_ref[:, 1:2])


def _combine(x, gate, ys, pos, tm):
    m, d = x.shape
    grid_spec = pltpu.PrefetchScalarGridSpec(
        num_scalar_prefetch=1,
        grid=(m // tm,),
        in_specs=[pl.BlockSpec((tm, d), lambda i, pos: (i, 0)),
                  pl.BlockSpec((tm, LANES), lambda i, pos: (i, 0)),
                  pl.BlockSpec(memory_space=pl.ANY)],
        out_specs=pl.BlockSpec((tm, d), lambda i, pos: (i, 0)),
        scratch_shapes=[pltpu.VMEM((2, MOE_TOP_K, tm, d), F32), pltpu.SemaphoreType.DMA((2,))],
    )
    return pl.pallas_call(
        functools.partial(_combine_body, tm=tm),
        grid_spec=grid_spec,
        out_shape=jax.ShapeDtypeStruct((m, d), F32),
        compiler_params=_cparams("arbitrary"),
        name="moe_combine",
    )(pos, x, gate, ys)


def _moe_plan(eid, n_tokens):
    a = n_tokens * MOE_TOP_K
    eid = eid.reshape(a)
    n_blocks = (a + N_EXPERTS * (MOE_BM - 1)) // MOE_BM
    n_slots = n_blocks * MOE_BM
    n_groups = n_blocks // MOE_GROUP_BLOCKS + N_EXPERTS
    onehot = (eid[:, None] == jnp.arange(N_EXPERTS, dtype=jnp.int32)[None, :]).astype(jnp.int32)
    rank = jnp.take_along_axis(jnp.cumsum(onehot, axis=0), eid[:, None], axis=1)[:, 0] - 1
    counts = jnp.sum(onehot, axis=0)
    blocks_e = (counts + MOE_BM - 1) // MOE_BM
    blk_start = jnp.cumsum(blocks_e) - blocks_e
    pos = (blk_start[eid] * MOE_BM + rank).astype(jnp.int32)
    tok = jnp.arange(a, dtype=jnp.int32) // MOE_TOP_K
    slot_tok = jnp.zeros((n_slots,), jnp.int32).at[pos].set(tok, unique_indices=True)
    groups_e = (blocks_e + MOE_GROUP_BLOCKS - 1) // MOE_GROUP_BLOCKS
    grp_end = jnp.cumsum(groups_e)
    grp_start = grp_end - groups_e
    gidx = jnp.arange(n_groups, dtype=jnp.int32)
    ge = jnp.minimum(jnp.searchsorted(grp_end, gidx, side="right"), N_EXPERTS - 1).astype(jnp.int32)
    local = gidx - grp_start[ge]
    active = gidx < grp_end[-1]
    gb0 = jnp.where(active, blk_start[ge] + local * MOE_GROUP_BLOCKS, 0).astype(jnp.int32)
    gnb = jnp.where(active, jnp.minimum(blocks_e[ge] - local * MOE_GROUP_BLOCKS, MOE_GROUP_BLOCKS),
                    0).astype(jnp.int32)
    last_e = ge[jnp.maximum(grp_end[-1] - 1, 0)]
    ge = jnp.where(active, ge, last_e).astype(jnp.int32)
    n_used = jnp.sum(blocks_e).astype(jnp.int32).reshape(1)
    return pos, slot_tok, ge, gb0, gnb, n_used, n_slots


def _layer(x, mem, positions, attn_norm_g, mem_norm_g, w_in, fox_forget_b, fox_q_norm_g,
           fox_k_norm_g, dsa_q_norm_g, dsa_k_norm_g, idx_k_norm_g, cross_q_norm_g, cross_k_norm_g,
           w_mem_kv, w_branch_fox, w_branch_dsa, w_branch_cross, w_out, ffn_norm_g,
           router_group_w, router_group_b, router_expert_w, router_expert_b,
           expert_w_gate, expert_w_up, expert_w_down):
    b, s, d = x.shape
    n = b * s
    n_mem = mem.shape[1]
    fw, dw, cw = FOX_HEADS * HEAD_DIM, DSA_HEADS * HEAD_DIM, CROSS_HEADS * HEAD_DIM
    iqw = IDX_HEADS * IDX_DIM
    tm = min(512, n)
    x2d = x.reshape(n, d)

    sizes = (fw, fw, fw, FOX_HEADS, dw, dw, dw, iqw, IDX_HEADS, IDX_DIM, cw, 3 * d)
    offs = [0]
    for sz in sizes:
        offs.append(offs[-1] + sz)
    (c_fq, c_fk, c_fv, c_fl, c_dq, c_dk, c_dv, c_iq, c_iw, c_ik, c_cq, c_g) = range(12)

    w_t = jnp.transpose(w_in, (2, 0, 1)).reshape(offs[-1], d)
    order = (c_dq, c_dk, c_fv, c_dv, c_fq, c_fk, c_cq, c_iq, c_g)
    row0, r = {}, 0
    for c in order:
        row0[c] = r
        r += sizes[c]
    w_all = jnp.concatenate([w_t[offs[c]:offs[c + 1]] for c in order], axis=0).astype(BF16)
    small_rows = [w_t[offs[c]:offs[c + 1]] for c in (c_ik, c_iw, c_fl)]
    small_rows.append(jnp.zeros((LANES - IDX_DIM - IDX_HEADS - FOX_HEADS, d), F32))
    w_small = jnp.concatenate(small_rows, axis=0).astype(BF16)
    q_unit = (HEAD_DIM ** -0.5) * LOG2E
    g_qk1 = jnp.concatenate([jnp.tile(fox_q_norm_g * q_unit, FOX_HEADS),
                             jnp.tile(fox_k_norm_g, FOX_HEADS),
                             jnp.tile(cross_q_norm_g * q_unit, CROSS_HEADS)])
    g_qk2 = jnp.concatenate([jnp.tile(dsa_q_norm_g * q_unit, DSA_HEADS),
                             jnp.tile(dsa_k_norm_g, DSA_HEADS)])
    g_ik = jnp.concatenate([idx_k_norm_g, jnp.zeros((LANES - IDX_DIM,), F32)])
    fb_lane = jnp.zeros((LANES,), F32).at[SMALL_F:SMALL_F + FOX_HEADS].set(fox_forget_b)

    pos_col = positions.reshape(n, 1).astype(F32)
    cd, sd, ci, si = _rope_tables(pos_col, tm)

    h = _rmsnorm(x2d, attn_norm_g, BF16, tm)
    tmp = min(PROJ_TM, n)
    qk1 = _proj(h, w_all, "norm", BF16, tmp, 1024, gains=g_qk1, w_rows=(row0[c_fq], 2 * fw + cw))
    qk2 = _proj(h, w_all, "norm_rope", BF16, tm, 768, gains=g_qk2, tables=(cd, sd),
                w_rows=(row0[c_dq], 2 * dw))
    iq = _proj(h, w_all, "rope_idx", BF16, tmp, 1024, tables=(ci, si), w_rows=(row0[c_iq], iqw))
    small = _proj(h, w_small, "raw", F32, tm, LANES, w_rows=(0, LANES))
    ika, ikb, wl, lf_col = _small(small, g_ik, fb_lane, ci, si, tm)

    tq, tk = min(ATT_TQ, s), min(ATT_TK, s)
    assert tq == tk and s % tq == 0
    cum_col = _forget_cumsum(lf_col.reshape(b, s, LANES))
    qk1 = qk1.reshape(b, s, 2 * fw + cw)
    qk2 = qk2.reshape(b, s, 2 * dw)
    q_aug, k_aug = _fox_pack(qk1, cum_col, FOX_HEADS, min(512, s))
    vt = _proj_t(h, w_all, (row0[c_fv], fw + dw), tmp, 768, tk).reshape(b, s // tk, fw + dw, tk)
    o_fox = _attention(q_aug, 0, k_aug, 0, vt, 0, (), "fox", FOX_HEADS, AUG_DIM, tq, tk)

    topk = min(INDEX_TOPK, s // 4)
    wt = jnp.transpose(wl.reshape(b, s, LANES)[:, :, :IDX_HEADS], (0, 2, 1))
    bias = _select_bias(iq.reshape(b, s, iqw), ika.reshape(b, s, LANES), ikb.reshape(b, s, LANES),
                        wt, topk, tq, tk)
    o_dsa = _attention(qk2, 0, qk2, 1, vt, 1, (bias,), "dsa", DSA_HEADS, HEAD_DIM, tq, tk)

    tmm = min(512, b * n_mem)
    m_n = _rmsnorm(mem.reshape(b * n_mem, d), mem_norm_g, BF16, tmm)
    w_kv = w_mem_kv.astype(BF16)
    mk = _proj(m_n, w_kv[:, :cw], "norm", BF16, tmm, cw, gains=jnp.tile(cross_k_norm_g, CROSS_HEADS))
    mv = _proj(m_n, w_kv[:, cw:], "raw", BF16, tmm, cw)
    o_cross = _cross_attention(qk1, (2 * fw) // cw, mk.reshape(b, n_mem, cw), mv.reshape(b, n_mem, cw),
                               min(512, s))

    x_mid = _merge(h, o_fox.reshape(n, fw), o_dsa.reshape(n, dw), o_cross.reshape(n, cw), x2d,
                   w_all, row0[c_g], w_branch_fox.astype(BF16), w_branch_dsa.astype(BF16),
                   w_branch_cross.astype(BF16), w_out.astype(BF16), tm, MERGE_TN)

    w_router = jnp.concatenate(
        [router_group_w, router_expert_w, jnp.zeros((d, LANES - N_GROUPS - N_EXPERTS), F32)],
        axis=1).astype(BF16)
    b_router = jnp.concatenate(
        [router_group_b, router_expert_b, jnp.zeros((LANES - N_GROUPS - N_EXPERTS,), F32)])
    h2, eid, gate = _router(x_mid, ffn_norm_g, w_router, b_router, tm)
    pos, slot_tok, ge, gb0, gnb, n_used, n_slots = _moe_plan(eid[:, :MOE_TOP_K], n)
    ys = _experts(h2, slot_tok, ge, gb0, gnb, n_used, expert_w_gate, expert_w_up, expert_w_down,
                  n_slots)
    out = _combine(x_mid, gate, ys, pos, min(256, n))
    return out.reshape(b, s, d)


def kernel(x, mem, positions, attn_norm_g, mem_norm_g, w_in, fox_forget_b, fox_q_norm_g, fox_k_norm_g, dsa_q_norm_g, dsa_k_norm_g, idx_k_norm_g, cross_q_norm_g, cross_k_norm_g, w_mem_kv, w_branch_fox, w_branch_dsa, w_branch_cross, w_out, ffn_norm_g, router_group_w, router_group_b, router_expert_w, router_expert_b, expert_w_gate, expert_w_up, expert_w_down):
    depth = w_in.shape[0]
    for layer in range(depth):
        x = _layer(
            x, mem, positions, attn_norm_g[layer], mem_norm_g[layer], w_in[layer:layer + 1],
            fox_forget_b[layer], fox_q_norm_g[layer], fox_k_norm_g[layer],
            dsa_q_norm_g[layer], dsa_k_norm_g[layer], idx_k_norm_g[layer],
            cross_q_norm_g[layer], cross_k_norm_g[layer], w_mem_kv[layer],
            w_branch_fox[layer], w_branch_dsa[layer], w_branch_cross[layer], w_out[layer],
            ffn_norm_g[layer], router_group_w[layer], router_group_b[layer],
            router_expert_w[layer], router_expert_b[layer], expert_w_gate[layer],
            expert_w_up[layer], expert_w_down[layer])
    return x
```

```python
import functools

import jax
import jax.numpy as jnp
from jax import lax
from jax.experimental import pallas as pl
from jax.experimental.pallas import tpu as pltpu

F32 = jnp.float32
BF16 = jnp.bfloat16

LANES = 128
HEAD_DIM = 128
FOX_HEADS = 6
DSA_HEADS = 6
CROSS_HEADS = 4
IDX_HEADS = 16
IDX_DIM = 64
ROPE_THETA = 500000.0
ROT_DIM = HEAD_DIM // 4
IDX_ROT_DIM = IDX_DIM // 4
INDEX_TOPK = 256
N_GROUPS = 4
EXPERTS_PER_GROUP = 8
N_EXPERTS = N_GROUPS * EXPERTS_PER_GROUP
MOE_TOP_K = 2
EPS = 1e-6

NEG = -1e30
INT_MIN = -(2 ** 31)
VMEM_LIMIT = 56 * 1024 * 1024

ATT_TQ = 256
ATT_TK = 256
ATT_HEADS_PER_LOOP = 6
PROJ_TM = 1024
MERGE_TN = 512
MERGE_SUB = 256
MOE_BM = 128
MOE_GROUP_BLOCKS = 8
MOE_FC = 512
DMA_ISSUE_UNROLL = 8


def _cparams(*sem):
    return pltpu.CompilerParams(dimension_semantics=sem, vmem_limit_bytes=VMEM_LIMIT)


def _rmsnorm_body(x_ref, g_ref, o_ref):
    x = x_ref[...]
    r = lax.rsqrt(jnp.mean(x * x, axis=-1, keepdims=True) + EPS)
    o_ref[...] = ((x * r) * g_ref[...]).astype(o_ref.dtype)


def _rmsnorm(x2d, g, out_dtype, tm):
    m, d = x2d.shape
    return pl.pallas_call(
        _rmsnorm_body,
        grid=(m // tm,),
        in_specs=[pl.BlockSpec((tm, d), lambda i: (i, 0)),
                  pl.BlockSpec((1, d), lambda i: (0, 0))],
        out_specs=pl.BlockSpec((tm, d), lambda i: (i, 0)),
        out_shape=jax.ShapeDtypeStruct((m, d), out_dtype),
        compiler_params=_cparams("parallel"),
        name="rmsnorm",
    )(x2d, g.reshape(1, d))


W_PREP_ROWS = 128


def _w_prep_body(src_ref, w_ref, o_ref):
    del src_ref
    o_ref[...] = w_ref[:, 0, :].astype(o_ref.dtype)


def _w_prep(w_t, src_rows):
    _, _, k = w_t.shape
    n_chunks = src_rows.shape[0]
    grid_spec = pltpu.PrefetchScalarGridSpec(
        num_scalar_prefetch=1,
        grid=(n_chunks,),
        in_specs=[pl.BlockSpec((pl.Element(W_PREP_ROWS), pl.Element(1), pl.Element(k)),
                               lambda c, src: (src[c], 0, 0))],
        out_specs=pl.BlockSpec((W_PREP_ROWS, k), lambda c, src: (c, 0)),
    )
    return pl.pallas_call(
        _w_prep_body,
        grid_spec=grid_spec,
        out_shape=jax.ShapeDtypeStruct((n_chunks * W_PREP_ROWS, k), BF16),
        compiler_params=_cparams("parallel"),
        name="w_prep",
    )(src_rows, w_t)


IDX_FREQ_LANE = 32


def _rope_table_body(pos_ref, f_ref, cd_ref, sd_ref, ci_ref, si_ref):
    ang = pos_ref[...] * f_ref[...]
    c = jnp.cos(ang)
    s = jnp.sin(ang)
    lane = lax.broadcasted_iota(jnp.int32, c.shape, 1)
    hd = ROT_DIM // 2
    cd_ref[...] = jnp.where(lane < hd, c, jnp.where(lane < ROT_DIM, pltpu.roll(c, hd, 1), 1.0))
    sd_ref[...] = jnp.where(lane < hd, -s, jnp.where(lane < ROT_DIM, pltpu.roll(s, hd, 1), 0.0))
    m = lane & (IDX_DIM - 1)
    hi = IDX_ROT_DIM // 2
    ci_src = pltpu.roll(c, LANES - IDX_FREQ_LANE, 1)
    si_src = pltpu.roll(s, LANES - IDX_FREQ_LANE, 1)
    ci_ref[...] = jnp.where(m < IDX_ROT_DIM, ci_src, 1.0)
    si_ref[...] = jnp.where(m < hi, -si_src, jnp.where(m < IDX_ROT_DIM, si_src, 0.0))


def _rope_tables(pos_col, tm):
    n = pos_col.shape[0]

    def inv_freq(rot_dim):
        half = rot_dim // 2
        return jnp.power(jnp.float32(ROPE_THETA), -jnp.arange(half, dtype=F32) * 2.0 / rot_dim)

    fi = jnp.tile(inv_freq(IDX_ROT_DIM), 2)
    freq = jnp.zeros((LANES,), F32).at[:ROT_DIM // 2].set(inv_freq(ROT_DIM))
    for base in (IDX_FREQ_LANE, IDX_FREQ_LANE + IDX_DIM):
        freq = freq.at[base:base + IDX_ROT_DIM].set(fi)
    tab = pl.BlockSpec((tm, LANES), lambda i: (i, 0))
    return pl.pallas_call(
        _rope_table_body,
        grid=(n // tm,),
        in_specs=[pl.BlockSpec((tm, 1), lambda i: (i, 0)), pl.BlockSpec((1, LANES), lambda i: (0, 0))],
        out_specs=[tab] * 4,
        out_shape=[jax.ShapeDtypeStruct((n, LANES), F32)] * 4,
        compiler_params=_cparams("parallel"),
        name="rope_tables",
    )(pos_col, freq.reshape(1, LANES))


def _apply_rope(a, c, s, half, period):
    lane = lax.broadcasted_iota(jnp.int32, a.shape, 1)
    first = (lane & (period - 1)) < half
    partner = jnp.where(first, pltpu.roll(a, LANES - half, 1), pltpu.roll(a, half, 1))
    return a * c + partner * s


def _proj_body(*refs, mode, n_lane_groups, w_rows):
    h_ref, w_ref = refs[0], refs[1]
    o_ref = refs[-1]

    if w_rows:
        acc = lax.dot_general(h_ref[...], w_ref[...], (((1,), (1,)), ((), ())),
                              preferred_element_type=F32)
    else:
        acc = jnp.dot(h_ref[...], w_ref[...], preferred_element_type=F32)
    if mode == "raw":
        o_ref[...] = acc.astype(o_ref.dtype)
        return
    for g in range(n_lane_groups):
        sl = slice(g * LANES, (g + 1) * LANES)
        a = acc[:, sl]
        if mode in ("norm", "norm_rope"):
            g_ref = refs[2]
            a = (a * lax.rsqrt(jnp.mean(a * a, axis=-1, keepdims=True) + EPS)) * g_ref[:, sl]
        if mode == "norm_rope":
            a = _apply_rope(a, refs[3][...], refs[4][...], ROT_DIM // 2, HEAD_DIM)
        if mode == "rope_idx":
            a = _apply_rope(a, refs[2][...], refs[3][...], IDX_ROT_DIM // 2, IDX_DIM)
        o_ref[:, sl] = a.astype(o_ref.dtype)


def _proj_t_body(h_ref, w_ref, o_ref, *, tk):
    acc = lax.dot_general(w_ref[...], h_ref[...], (((1,), (1,)), ((), ())),
                          preferred_element_type=F32)
    for q in range(o_ref.shape[0]):
        o_ref[q] = acc[:, q * tk:(q + 1) * tk].astype(o_ref.dtype)


def _proj_t(h, w, w_rows, tm, tn, tk):
    m, k = h.shape
    row0, n = w_rows
    assert row0 % tn == 0 and n % tn == 0 and tm % tk == 0
    return pl.pallas_call(
        functools.partial(_proj_t_body, tk=tk),
        grid=(n // tn, m // tm),
        in_specs=[pl.BlockSpec((tm, k), lambda j, i: (i, 0)),
                  pl.BlockSpec((tn, k), lambda j, i: (row0 // tn + j, 0))],
        out_specs=pl.BlockSpec((tm // tk, tn, tk), lambda j, i: (i, j, 0)),
        out_shape=jax.ShapeDtypeStruct((m // tk, n, tk), BF16),
        compiler_params=_cparams("parallel", "parallel"),
        name="proj_t",
    )(h, w)


def _proj(h, w, mode, out_dtype, tm, tn, gains=None, tables=None, w_rows=None):
    m, k = h.shape
    if w_rows is None:
        n = w.shape[1]
        w_spec = pl.BlockSpec((k, tn), lambda j, i: (0, j))
    else:
        row0, n = w_rows
        assert row0 % tn == 0 and n % tn == 0
        w_spec = pl.BlockSpec((tn, k), lambda j, i: (row0 // tn + j, 0))
    in_specs = [pl.BlockSpec((tm, k), lambda j, i: (i, 0)), w_spec]
    args = [h, w]
    if gains is not None:
        in_specs.append(pl.BlockSpec((1, tn), lambda j, i: (0, j)))
        args.append(gains.reshape(1, n))
    if tables is not None:
        for t in tables:
            in_specs.append(pl.BlockSpec((tm, LANES), lambda j, i: (i, 0)))
            args.append(t)
    return pl.pallas_call(
        functools.partial(_proj_body, mode=mode, n_lane_groups=tn // LANES,
                          w_rows=w_rows is not None),
        grid=(n // tn, m // tm),
        in_specs=in_specs,
        out_specs=pl.BlockSpec((tm, tn), lambda j, i: (i, j)),
        out_shape=jax.ShapeDtypeStruct((m, n), out_dtype),
        compiler_params=_cparams("parallel", "parallel"),
        name="proj_" + mode,
    )(*args)


SMALL_IK = 0
SMALL_IW = 64
SMALL_F = 80


def _small_body(x_ref, gk_ref, fb_ref, c_ref, s_ref, ika_ref, ikb_ref, wl_ref, lf_ref):
    x = x_ref[...]
    lane = lax.broadcasted_iota(jnp.int32, x.shape, 1)
    is_ik = lane < IDX_DIM
    ik = jnp.where(is_ik, x, 0.0)
    ms = jnp.sum(ik * ik, axis=-1, keepdims=True) * (1.0 / IDX_DIM)
    ik = (ik * lax.rsqrt(ms + EPS)) * gk_ref[...]
    ik = _apply_rope(ik, c_ref[...], s_ref[...], IDX_ROT_DIM // 2, IDX_DIM)
    ik = jnp.where(is_ik, ik, 0.0)
    ika_ref[...] = ik.astype(BF16)
    ikb_ref[...] = pltpu.roll(ik, IDX_DIM, 1).astype(BF16)
    wl_ref[...] = pltpu.roll(x, LANES - SMALL_IW, 1) * (2.0 ** -5)
    z = x + fb_ref[...]
    log_f = jnp.minimum(z, 0.0) - jnp.log1p(jnp.exp(-jnp.abs(z)))
    lf_ref[...] = pltpu.roll(log_f, LANES - SMALL_F, 1)


def _small(x, idx_k_gain_lane, forget_b_lane, ci, si, tm):
    n = x.shape[0]
    row = pl.BlockSpec((1, LANES), lambda i: (0, 0))
    tab = pl.BlockSpec((tm, LANES), lambda i: (i, 0))
    return pl.pallas_call(
        _small_body,
        grid=(n // tm,),
        in_specs=[tab, row, row, tab, tab],
        out_specs=[tab, tab, tab, tab],
        out_shape=[jax.ShapeDtypeStruct((n, LANES), BF16), jax.ShapeDtypeStruct((n, LANES), BF16),
                   jax.ShapeDtypeStruct((n, LANES), F32), jax.ShapeDtypeStruct((n, LANES), F32)],
        compiler_params=_cparams("parallel"),
        name="small_cols",
    )(x, idx_k_gain_lane.reshape(1, LANES), forget_b_lane.reshape(1, LANES), ci, si)


def _cumsum_body(col_ref, ccol_ref, *, seq):
    c = col_ref[...]
    ci = lax.broadcasted_iota(jnp.int32, c.shape, 0)
    sh = 1
    while sh < seq:
        c = c + jnp.where(ci >= sh, pltpu.roll(c, sh, 0), 0.0)
        sh *= 2
    ccol_ref[...] = c


def _forget_cumsum(lf_col):
    b, s, _ = lf_col.shape
    return pl.pallas_call(
        functools.partial(_cumsum_body, seq=s),
        grid=(b,),
        in_specs=[pl.BlockSpec((None, s, LANES), lambda i: (i, 0, 0))],
        out_specs=pl.BlockSpec((None, s, LANES), lambda i: (i, 0, 0)),
        out_shape=jax.ShapeDtypeStruct((b, s, LANES), F32),
        compiler_params=_cparams("parallel"),
        name="forget_cumsum",
    )(lf_col)


AUG_DIM = 2 * HEAD_DIM
LOG2E = 1.4426950408889634


def _split3(x):
    hi = x.astype(BF16).astype(F32)
    r1 = x - hi
    mid = r1.astype(BF16).astype(F32)
    lo = (r1 - mid).astype(BF16).astype(F32)
    return hi, mid, lo


def _fox_pack_body(q_ref, k_ref, cum_ref, qa_ref, ka_ref, *, n_heads):
    rows = q_ref.shape[0]
    lane = lax.broadcasted_iota(jnp.int32, (rows, LANES), 1)
    for h in range(n_heads):
        c = jnp.broadcast_to(cum_ref[:, h:h + 1], (rows, LANES)) * LOG2E
        qh, qm, ql = _split3(c)
        kh, km, kl = _split3(-c)
        q_aug = jnp.where(lane == 0, qh, jnp.where(lane == 1, qm, jnp.where(
            lane == 2, ql, jnp.where(lane < 6, 1.0, 0.0))))
        k_aug = jnp.where(lane < 3, 1.0, jnp.where(lane == 3, kh, jnp.where(
            lane == 4, km, jnp.where(lane == 5, kl, 0.0))))
        base = h * AUG_DIM
        qa_ref[:, base:base + HEAD_DIM] = q_ref[:, h * HEAD_DIM:(h + 1) * HEAD_DIM]
        ka_ref[:, base:base + HEAD_DIM] = k_ref[:, h * HEAD_DIM:(h + 1) * HEAD_DIM]
        qa_ref[:, base + HEAD_DIM:base + AUG_DIM] = q_aug.astype(BF16)
        ka_ref[:, base + HEAD_DIM:base + AUG_DIM] = k_aug.astype(BF16)


def _fox_pack(qk, cum_col, n_heads, ts):
    b, s, _ = qk.shape
    w = n_heads * HEAD_DIM
    wa = n_heads * AUG_DIM
    return pl.pallas_call(
        functools.partial(_fox_pack_body, n_heads=n_heads),
        grid=(b, s // ts),
        in_specs=[pl.BlockSpec((None, ts, w), lambda bi, i: (bi, i, 0)),
                  pl.BlockSpec((None, ts, w), lambda bi, i: (bi, i, 1)),
                  pl.BlockSpec((None, ts, LANES), lambda bi, i: (bi, i, 0))],
        out_specs=[pl.BlockSpec((None, ts, wa), lambda bi, i: (bi, i, 0)),
                   pl.BlockSpec((None, ts, wa), lambda bi, i: (bi, i, 0))],
        out_shape=[jax.ShapeDtypeStruct((b, s, wa), BF16)] * 2,
        compiler_params=_cparams("parallel", "parallel"),
        name="fox_pack",
    )(qk, qk, cum_col)


def _attn_body(q_ref, k_ref, vt_ref, *rest, tq, tk, n_heads, kdim, mode, heads_per_loop):
    o_ref = rest[-1]
    qi = pl.program_id(1)
    key0 = lax.broadcasted_iota(jnp.int32, (tk, tq), 0)
    qry = qi * tq + lax.broadcasted_iota(jnp.int32, (tk, tq), 1)
    def scores(h, j):
        off = pl.multiple_of(j * tk, tk)
        q = q_ref[:, h * kdim:(h + 1) * kdim]
        k = k_ref[pl.ds(off, tk), h * kdim:(h + 1) * kdim]
        return lax.dot_general(k, q, (((1,), (1,)), ((), ())), preferred_element_type=F32)

    def flush(h, j, p, alpha, acc):
        vt = vt_ref[j, h * HEAD_DIM:(h + 1) * HEAD_DIM, :]
        return alpha * acc + jnp.dot(vt, p, preferred_element_type=F32)

    def softmax_step(h, j, m, l, masked):
        s = scores(h, j)
        if mode == "dsa":
            s = s + rest[0][j]
        if masked:
            s = jnp.where(key0 + j * tk <= qry, s, NEG)
        m_new = jnp.maximum(m, jnp.max(s, axis=0, keepdims=True))
        alpha = jnp.exp2(m - m_new)
        p = jnp.exp2(s - m_new)
        l = alpha * l + jnp.sum(p, axis=0, keepdims=True)
        return p.astype(BF16), alpha, m_new, l

    last = qi
    for h0 in range(0, n_heads, heads_per_loop):
        heads = range(h0, h0 + heads_per_loop)

        def step(j, carries, masked=False, heads=heads):
            prev = jnp.maximum(j - 1, 0)
            out = []
            for h, (p, alpha, m, l, acc) in zip(heads, carries):
                acc = flush(h, prev, p, alpha, acc)
                p, alpha, m, l = softmax_step(h, j, m, l, masked)
                out.append((p, alpha, m, l, acc))
            return tuple(out)

        carries = tuple(
            (jnp.zeros((tk, tq), BF16), jnp.ones((1, tq), F32), jnp.full((1, tq), NEG, F32),
             jnp.zeros((1, tq), F32), jnp.zeros((HEAD_DIM, tq), F32)) for h in heads)
        n_loop = last if mode == "fox" else last + 1
        carries = lax.fori_loop(0, n_loop, step, carries)
        if mode == "fox":
            carries = step(last, carries, masked=True)
        for h, (p, alpha, m, l, acc) in zip(heads, carries):
            acc = flush(h, last, p, alpha, acc)
            o_ref[:, h * HEAD_DIM:(h + 1) * HEAD_DIM] = jnp.transpose(acc / l).astype(o_ref.dtype)


def _attention(q_arr, q_blk, k_arr, k_blk, vt_arr, v_blk, extras, mode, n_heads, kdim, tq, tk):
    b, s, _ = q_arr.shape
    w = n_heads * HEAD_DIM
    wk = n_heads * kdim
    in_specs = [pl.BlockSpec((None, tq, wk), lambda bi, qi: (bi, qi, q_blk)),
                pl.BlockSpec((None, s, wk), lambda bi, qi: (bi, 0, k_blk)),
                pl.BlockSpec((None, s // tk, w, tk), lambda bi, qi: (bi, 0, v_blk, 0))]
    if mode == "dsa":
        in_specs += [pl.BlockSpec((None, None, s // tk, tk, tq), lambda bi, qi: (bi, qi, 0, 0, 0))]
    return pl.pallas_call(
        functools.partial(_attn_body, tq=tq, tk=tk, n_heads=n_heads, kdim=kdim, mode=mode,
                          heads_per_loop=ATT_HEADS_PER_LOOP),
        grid=(b, s // tq),
        in_specs=in_specs,
        out_specs=pl.BlockSpec((None, tq, w), lambda bi, qi: (bi, qi, 0)),
        out_shape=jax.ShapeDtypeStruct((b, s, w), BF16),
        compiler_params=_cparams("parallel", "parallel"),
        name="attn_" + mode,
    )(q_arr, k_arr, vt_arr, *extras)


def _select_body(iq_ref, ika_ref, ikb_ref, wt_ref, o_ref, keys_ref, half_ref, *, tq, tk, n_chunks, topk):
    qi = pl.program_id(1)
    used = qi + 1
    key0 = lax.broadcasted_iota(jnp.int32, (tk, tq), 0)
    qry = qi * tq + lax.broadcasted_iota(jnp.int32, (tk, tq), 1)

    def score_chunk(c, _):
        off = pl.multiple_of(c * tk, tk)
        ka = ika_ref[pl.ds(off, tk), :]
        kb = ikb_ref[pl.ds(off, tk), :]
        acc = jnp.zeros((tk, tq), F32)
        for j in range(IDX_HEADS):
            pair = iq_ref[:, (j // 2) * LANES:(j // 2 + 1) * LANES]
            logit = lax.dot_general(ka if j % 2 == 0 else kb, pair, (((1,), (1,)), ((), ())),
                                    preferred_element_type=F32)
            acc = acc + wt_ref[j:j + 1, :] * jnp.maximum(logit, 0.0)
        bits = lax.bitcast_convert_type(acc, jnp.int32)
        key = jnp.where(bits < 0, bits ^ jnp.int32(0x7FFFFFFF), bits)
        key = jnp.where(key0 + off <= qry, key, jnp.int32(INT_MIN))
        keys_ref[c] = key
        half_ref[c] = lax.shift_right_arithmetic(key, 16).astype(jnp.int16)
        return 0

    lax.fori_loop(0, used, score_chunk, 0)

    one16 = jnp.int16(1)
    zero16 = jnp.int16(0)

    def count_ge(cand16):
        def count_chunk(c, cnt):
            ge = jnp.where(half_ref[c] >= cand16, one16, zero16)
            parts = [ge[r:r + 16] for r in range(0, tk, 16)]
            while len(parts) > 1:
                parts = [a + b for a, b in zip(parts[::2], parts[1::2])]
            return cnt + parts[0]

        cnt = lax.fori_loop(0, used, count_chunk, jnp.zeros((16, tq), jnp.int16))
        return jnp.sum(cnt.astype(jnp.int32), axis=0, keepdims=True)

    def search16(need):
        def bit_step(i, prefix):
            cand_u = prefix | jnp.left_shift(jnp.int32(1), 15 - i)
            cnt = count_ge((cand_u - 32768).astype(jnp.int16))
            return jnp.where(cnt >= need, cand_u, prefix)

        return lax.fori_loop(0, 16, bit_step, jnp.zeros((1, tq), jnp.int32)) - 32768

    hi = search16(topk)
    above = count_ge(jnp.minimum(hi + 1, 32767).astype(jnp.int16))
    above = jnp.where(hi >= 32767, 0, above)
    hi16 = hi.astype(jnp.int16)

    def low_chunk(c, _):
        low = (keys_ref[c] & jnp.int32(0xFFFF)) - 32768
        half_ref[c] = jnp.where(half_ref[c] == hi16, low.astype(jnp.int16), jnp.int16(-32768))
        return 0

    lax.fori_loop(0, used, low_chunk, 0)
    lo = search16(topk - above)
    thr = lax.shift_left(hi, 16) | (lo + 32768)
    thr = jnp.maximum(thr, jnp.int32(INT_MIN + 1))

    def count_keys(pred):
        def chunk(c, cnt):
            hit = jnp.where(pred(keys_ref[c]), 1.0, 0.0)
            return cnt + jnp.sum(hit.reshape(tk // 8, 8, tq), axis=0)

        cnt = lax.fori_loop(0, used, chunk, jnp.zeros((8, tq), F32))
        return jnp.sum(cnt, axis=0, keepdims=True)

    tied = jnp.max(count_keys(lambda k: k >= thr)) > topk

    @pl.when(jnp.logical_not(tied))
    def _():
        def write_chunk(c, _):
            o_ref[c] = jnp.where(keys_ref[c] >= thr, 0.0, NEG)
            return 0

        lax.fori_loop(0, used, write_chunk, 0)

    @pl.when(tied)
    def _():
        need = topk - count_keys(lambda k: k > thr)
        earlier = jnp.where(lax.broadcasted_iota(jnp.int32, (tk, tk), 1)
                            < lax.broadcasted_iota(jnp.int32, (tk, tk), 0), 1.0, 0.0).astype(BF16)

        def write_chunk(c, seen):
            keys = keys_ref[c]
            tie = jnp.where(keys == thr, 1.0, 0.0)
            rank = jnp.dot(earlier, tie.astype(BF16), preferred_element_type=F32) + seen
            o_ref[c] = jnp.where(keys > thr, 0.0,
                                 jnp.where(keys == thr, jnp.where(rank < need, 0.0, NEG), NEG))
            return seen + jnp.sum(tie, axis=0, keepdims=True)

        lax.fori_loop(0, used, write_chunk, jnp.zeros((1, tq), F32))

    def fill_chunk(c, _):
        o_ref[c] = jnp.full((tk, tq), NEG, F32)
        return 0

    lax.fori_loop(used, n_chunks, fill_chunk, 0)


def _select_bias(iq, ika, ikb, wt, topk, tq, tk):
    b, s, _ = iq.shape
    n_chunks = s // tk
    return pl.pallas_call(
        functools.partial(_select_body, tq=tq, tk=tk, n_chunks=n_chunks, topk=topk),
        grid=(b, s // tq),
        in_specs=[pl.BlockSpec((None, tq, IDX_HEADS * IDX_DIM), lambda bi, qi: (bi, qi, 0)),
                  pl.BlockSpec((None, s, LANES), lambda bi, qi: (bi, 0, 0)),
                  pl.BlockSpec((None, s, LANES), lambda bi, qi: (bi, 0, 0)),
                  pl.BlockSpec((None, IDX_HEADS, tq), lambda bi, qi: (bi, 0, qi))],
        out_specs=pl.BlockSpec((None, None, n_chunks, tk, tq), lambda bi, qi: (bi, qi, 0, 0, 0)),
        out_shape=jax.ShapeDtypeStruct((b, s // tq, n_chunks, tk, tq), F32),
        scratch_shapes=[pltpu.VMEM((n_chunks, tk, tq), jnp.int32),
                        pltpu.VMEM((n_chunks, tk, tq), jnp.int16)],
        compiler_params=_cparams("parallel", "parallel"),
        name="index_select",
    )(iq, ika, ikb, wt)


def _cross_body(q_ref, k_ref, v_ref, o_ref, *, n_heads):
    for h in range(n_heads):
        hs = slice(h * HEAD_DIM, (h + 1) * HEAD_DIM)
        s = lax.dot_general(q_ref[:, hs], k_ref[:, hs], (((1,), (1,)), ((), ())),
                            preferred_element_type=F32)
        p = jnp.exp2(s - jnp.max(s, axis=-1, keepdims=True))
        l = jnp.sum(p, axis=-1, keepdims=True)
        o = jnp.dot(p.astype(BF16), v_ref[:, hs], preferred_element_type=F32)
        o_ref[:, hs] = (o / l).astype(o_ref.dtype)


def _cross_attention(q_arr, q_blk, mk, mv, tq):
    b, s, _ = q_arr.shape
    n_mem = mk.shape[1]
    w = CROSS_HEADS * HEAD_DIM
    return pl.pallas_call(
        functools.partial(_cross_body, n_heads=CROSS_HEADS),
        grid=(b, s // tq),
        in_specs=[pl.BlockSpec((None, tq, w), lambda bi, qi: (bi, qi, q_blk)),
                  pl.BlockSpec((None, n_mem, w), lambda bi, qi: (bi, 0, 0)),
                  pl.BlockSpec((None, n_mem, w), lambda bi, qi: (bi, 0, 0))],
        out_specs=pl.BlockSpec((None, tq, w), lambda bi, qi: (bi, qi, 0)),
        out_shape=jax.ShapeDtypeStruct((b, s, w), BF16),
        compiler_params=_cparams("parallel", "parallel"),
        name="cross_attn",
    )(q_arr, mk, mv)


def _sigmoid(x):
    return 1.0 / (1.0 + jnp.exp(-x))


def _merge_body(h_ref, of_ref, od_ref, oc_ref, x_ref, wgf_ref, wgd_ref, wgc_ref,
                wbf_ref, wbd_ref, wbc_ref, wo_ref, o_ref):
    j = pl.program_id(1)
    h = h_ref[...]
    tn = wo_ref.shape[0]
    sub = min(tn, MERGE_SUB)

    def gate_logits(wg_ref, cs):
        return lax.dot_general(h, wg_ref[cs, :], (((1,), (1,)), ((), ())), preferred_element_type=F32)

    part = None
    for c in range(tn // sub):
        cs = slice(c * sub, (c + 1) * sub)
        zf, zd, zc = gate_logits(wgf_ref, cs), gate_logits(wgd_ref, cs), gate_logits(wgc_ref, cs)
        bf = jnp.dot(of_ref[...], wbf_ref[:, cs], preferred_element_type=F32)
        bd = jnp.dot(od_ref[...], wbd_ref[:, cs], preferred_element_type=F32)
        bc = jnp.dot(oc_ref[...], wbc_ref[:, cs], preferred_element_type=F32)
        merged = (_sigmoid(zf) * bf + _sigmoid(zd) * bd) + _sigmoid(zc) * bc
        p = jnp.dot(merged.astype(BF16), wo_ref[cs, :], preferred_element_type=F32)
        part = p if part is None else part + p

    @pl.when(j == 0)
    def _():
        o_ref[...] = x_ref[...] + part

    @pl.when(j != 0)
    def _():
        o_ref[...] += part


def _merge(h, o_fox, o_dsa, o_cross, x, w_gates, gate_row0, wbf, wbd, wbc, w_out, tm, tn):
    m, d = x.shape
    nj = d // tn
    assert gate_row0 % tn == 0
    g0 = gate_row0 // tn
    row = lambda i, j: (i, 0)
    in_specs = [pl.BlockSpec((tm, d), row),
                pl.BlockSpec((tm, o_fox.shape[1]), row),
                pl.BlockSpec((tm, o_dsa.shape[1]), row),
                pl.BlockSpec((tm, o_cross.shape[1]), row),
                pl.BlockSpec((tm, d), row),
                pl.BlockSpec((tn, d), lambda i, j: (g0 + j, 0)),
                pl.BlockSpec((tn, d), lambda i, j: (g0 + nj + j, 0)),
                pl.BlockSpec((tn, d), lambda i, j: (g0 + 2 * nj + j, 0)),
                pl.BlockSpec((wbf.shape[0], tn), lambda i, j: (0, j)),
                pl.BlockSpec((wbd.shape[0], tn), lambda i, j: (0, j)),
                pl.BlockSpec((wbc.shape[0], tn), lambda i, j: (0, j)),
                pl.BlockSpec((tn, d), lambda i, j: (j, 0))]
    return pl.pallas_call(
        _merge_body,
        grid=(m // tm, nj),
        in_specs=in_specs,
        out_specs=pl.BlockSpec((tm, d), row),
        out_shape=jax.ShapeDtypeStruct((m, d), F32),
        compiler_params=_cparams("parallel", "arbitrary"),
        name="gated_merge",
    )(h, o_fox, o_dsa, o_cross, x, w_gates, w_gates, w_gates, wbf, wbd, wbc, w_out)


def _router_body(x_ref, g_ref, w_ref, b_ref, h_ref, eid_ref, gate_ref):
    x = x_ref[...]
    r = lax.rsqrt(jnp.mean(x * x, axis=-1, keepdims=True) + EPS)
    h = (x * r) * g_ref[...]
    hb = h.astype(BF16)
    bits = lax.bitcast_convert_type(hb.astype(F32), jnp.int32)
    half = bits.shape[1] // 2
    h_ref[...] = lax.shift_right_logical(bits[:, :half], 16) | (bits[:, half:] & jnp.int32(-65536))
    logits = jnp.dot(hb, w_ref[...], preferred_element_type=F32) + b_ref[...]
    lane = lax.broadcasted_iota(jnp.int32, logits.shape, 1).astype(F32)

    def masked_max(mask):
        return jnp.max(jnp.where(mask, logits, -jnp.inf), axis=-1, keepdims=True)

    def first_lane(mask):
        return jnp.min(jnp.where(mask, lane, float(LANES)), axis=-1, keepdims=True)

    is_group = lane < N_GROUPS
    gmax = masked_max(is_group)
    grp = first_lane(is_group & (logits == gmax))
    denom = jnp.sum(jnp.where(is_group, jnp.exp(logits - gmax), 0.0), axis=-1, keepdims=True)
    p_sel = 1.0 / denom
    lo = N_GROUPS + grp * EXPERTS_PER_GROUP
    in_grp = (lane >= lo) & (lane < lo + EXPERTS_PER_GROUP)
    e1 = masked_max(in_grp)
    j1 = first_lane(in_grp & (logits == e1))
    rest = in_grp & (lane != j1)
    e2 = masked_max(rest)
    j2 = first_lane(rest & (logits == e2))
    t = jnp.exp(e2 - e1)
    g1 = p_sel * (1.0 / (1.0 + t))
    g2 = p_sel * (t / (1.0 + t))
    eid = jnp.where(lane == 0.0, j1 - N_GROUPS, jnp.where(lane == 1.0, j2 - N_GROUPS, 0.0))
    eid_ref[...] = eid.astype(jnp.int32)
    gate_ref[...] = jnp.where(lane == 0.0, g1, jnp.where(lane == 1.0, g2, 0.0))


def _router(x, g, w_router, b_router, tm):
    m, d = x.shape
    return pl.pallas_call(
        _router_body,
        grid=(m // tm,),
        in_specs=[pl.BlockSpec((tm, d), lambda i: (i, 0)),
                  pl.BlockSpec((1, d), lambda i: (0, 0)),
                  pl.BlockSpec((d, LANES), lambda i: (0, 0)),
                  pl.BlockSpec((1, LANES), lambda i: (0, 0))],
        out_specs=[pl.BlockSpec((tm, d // 2), lambda i: (i, 0)),
                   pl.BlockSpec((tm, LANES), lambda i: (i, 0)),
                   pl.BlockSpec((tm, LANES), lambda i: (i, 0))],
        out_shape=[jax.ShapeDtypeStruct((m, d // 2), jnp.int32),
                   jax.ShapeDtypeStruct((m, LANES), jnp.int32),
                   jax.ShapeDtypeStruct((m, LANES), F32)],
        compiler_params=_cparams("parallel"),
        name="moe_router",
    )(x, g.reshape(1, d), w_router, b_router.reshape(1, LANES))


def _expert_body(ge_ref, gb0_ref, gnb_ref, used_ref, tok_ref, h_hbm, wg_ref, wu_ref, wd_ref, ys_hbm,
                 xs_ref, xb_ref, acc_ref, wgb_ref, wub_ref, wdb_ref, sem_in, sem_out,
                 *, bm, n_f, n_blocks):
    g = pl.program_id(0)
    f = pl.program_id(1)
    nb = gnb_ref[g]
    blk0 = gb0_ref[g]

    @pl.when(jnp.logical_and(g == 0, f == 0))
    def _():
        acc_ref[pl.ds(0, bm)] = jnp.zeros((bm,) + acc_ref.shape[1:], F32)

        def tail_copy(i):
            return pltpu.make_async_copy(acc_ref.at[pl.ds(0, bm)], ys_hbm.at[pl.ds(i * bm, bm)],
                                         sem_out)

        def start_tail(i, _):
            tail_copy(i).start()
            return 0

        lax.fori_loop(used_ref[0], n_blocks, start_tail, 0)

        def wait_tail(i, _):
            tail_copy(i).wait()
            return 0

        lax.fori_loop(used_ref[0], n_blocks, wait_tail, 0)

    def gather(grp):
        slot = grp % 2
        base = gb0_ref[grp] * bm

        def issue(r0, _):
            for u in range(DMA_ISSUE_UNROLL):
                r = r0 * DMA_ISSUE_UNROLL + u
                pltpu.make_async_copy(h_hbm.at[pl.ds(tok_ref[base + r], 1)],
                                      xs_ref.at[slot, pl.ds(r, 1)], sem_in.at[slot]).start()
            return 0

        lax.fori_loop(0, gnb_ref[grp] * (bm // DMA_ISSUE_UNROLL), issue, 0)

    def block_copy_out(i):
        return pltpu.make_async_copy(acc_ref.at[pl.ds(i * bm, bm)],
                                     ys_hbm.at[pl.ds((blk0 + i) * bm, bm)], sem_out)

    @pl.when(jnp.logical_and(g == 0, f == 0))
    def _():
        gather(g)

    @pl.when(jnp.logical_and(nb > 0, f == 0))
    def _():
        slot = g % 2

        def land(i, _):
            rows = pl.ds(pl.multiple_of(i * bm, bm), bm)
            pltpu.make_async_copy(h_hbm.at[pl.ds(0, bm)], xs_ref.at[slot, rows], sem_in.at[slot]).wait()
            return 0

        lax.fori_loop(0, nb, land, 0)

        half = xb_ref.shape[1] // 2

        def unpack(i, _):
            rows = pl.ds(pl.multiple_of(i * bm, bm), bm)
            w = xs_ref[slot, rows, :]
            lo = lax.bitcast_convert_type(lax.shift_left(w, 16), F32)
            hi = lax.bitcast_convert_type(w & jnp.int32(-65536), F32)
            xb_ref[rows, :half] = lo.astype(BF16)
            xb_ref[rows, half:] = hi.astype(BF16)
            return 0

        lax.fori_loop(0, nb, unpack, 0)

    @pl.when(jnp.logical_and(f == 0, g + 1 < pl.num_programs(0)))
    def _():
        gather(g + 1)

    def wait_out(n):
        def body(i, _):
            pltpu.make_async_copy(acc_ref.at[pl.ds(0, bm)], ys_hbm.at[pl.ds(0, bm)], sem_out).wait()
            return 0

        lax.fori_loop(0, n, body, 0)

    @pl.when(jnp.logical_and(f == 0, g > 0))
    def _():
        wait_out(gnb_ref[jnp.maximum(g - 1, 0)])

    @pl.when(nb > 0)
    def _():
        wgb_ref[...] = wg_ref[...].astype(BF16)
        wub_ref[...] = wu_ref[...].astype(BF16)
        wdb_ref[...] = wd_ref[...].astype(BF16)

        def swiglu(rows):
            xb = xb_ref[rows, :]
            a = jnp.dot(xb, wgb_ref[...], preferred_element_type=F32)
            u = jnp.dot(xb, wub_ref[...], preferred_element_type=F32)
            hid = (a * _sigmoid(a)) * u
            y = jnp.dot(hid.astype(BF16), wdb_ref[...], preferred_element_type=F32)

            @pl.when(f == 0)
            def _():
                acc_ref[rows, :] = y

            @pl.when(f != 0)
            def _():
                acc_ref[rows, :] += y

        def block_pair(i, _):
            swiglu(pl.ds(pl.multiple_of(i * 2 * bm, 2 * bm), 2 * bm))
            return 0

        lax.fori_loop(0, nb // 2, block_pair, 0)

        @pl.when(nb % 2 == 1)
        def _():
            swiglu(pl.ds(pl.multiple_of((nb - 1) * bm, bm), bm))

    @pl.when(jnp.logical_and(nb > 0, f == n_f - 1))
    def _():
        def start_out(i, _):
            block_copy_out(i).start()
            return 0

        lax.fori_loop(0, nb, start_out, 0)

        @pl.when(g == pl.num_programs(0) - 1)
        def _():
            wait_out(nb)


def _experts(h2, slot_tok, grp_e, grp_blk0, grp_nb, n_used, w_gate, w_up, w_down, n_slots):
    d = w_gate.shape[1]
    ff = w_gate.shape[2]
    n_f = ff // MOE_FC
    n_groups = grp_e.shape[0]
    rmax = MOE_GROUP_BLOCKS * MOE_BM

    def f_idx(g, f, gnb):
        return jnp.where(gnb[g] > 0, f, n_f - 1)

    grid_spec = pltpu.PrefetchScalarGridSpec(
        num_scalar_prefetch=5,
        grid=(n_groups, n_f),
        in_specs=[
            pl.BlockSpec(memory_space=pl.ANY),
            pl.BlockSpec((None, d, MOE_FC), lambda g, f, ge, gb0, gnb, *_: (ge[g], 0, f_idx(g, f, gnb))),
            pl.BlockSpec((None, d, MOE_FC), lambda g, f, ge, gb0, gnb, *_: (ge[g], 0, f_idx(g, f, gnb))),
            pl.BlockSpec((None, MOE_FC, d), lambda g, f, ge, gb0, gnb, *_: (ge[g], f_idx(g, f, gnb), 0)),
        ],
        out_specs=pl.BlockSpec(memory_space=pl.ANY),
        scratch_shapes=[pltpu.VMEM((2, rmax, d // 2), jnp.int32),
                        pltpu.VMEM((rmax, d), BF16),
                        pltpu.VMEM((rmax, d), F32),
                        pltpu.VMEM((d, MOE_FC), BF16),
                        pltpu.VMEM((d, MOE_FC), BF16),
                        pltpu.VMEM((MOE_FC, d), BF16),
                        pltpu.SemaphoreType.DMA((2,)),
                        pltpu.SemaphoreType.DMA(())],
    )
    return pl.pallas_call(
        functools.partial(_expert_body, bm=MOE_BM, n_f=n_f, n_blocks=n_slots // MOE_BM),
        grid_spec=grid_spec,
        out_shape=jax.ShapeDtypeStruct((n_slots, d), F32),
        compiler_params=_cparams("arbitrary", "arbitrary"),
        name="moe_experts",
    )(grp_e, grp_blk0, grp_nb, n_used, slot_tok, h2, w_gate, w_up, w_down)


def _combine_body(pos_ref, x_ref, gate_ref, ys_hbm, o_ref, buf_ref, sem, *, tm):
    i = pl.program_id(0)
    n_steps = pl.num_programs(0)

    def issue(step):
        slot = step % 2

        def body(r0, _):
            for u in range(DMA_ISSUE_UNROLL):
                r = r0 * DMA_ISSUE_UNROLL + u
                for k in range(MOE_TOP_K):
                    src = pos_ref[(step * tm + r) * MOE_TOP_K + k]
                    pltpu.make_async_copy(ys_hbm.at[pl.ds(src, 1)], buf_ref.at[slot, k, pl.ds(r, 1)],
                                          sem.at[slot]).start()
            return 0

        lax.fori_loop(0, tm // DMA_ISSUE_UNROLL, body, 0)

    @pl.when(i == 0)
    def _():
        issue(i)

    @pl.when(i + 1 < n_steps)
    def _():
        issue(i + 1)

    slot = i % 2
    for k in range(MOE_TOP_K):
        pltpu.make_async_copy(ys_hbm.at[pl.ds(0, tm)], buf_ref.at[slot, k], sem.at[slot]).wait()
    o_ref[...] = x_ref[...] + (buf_ref[slot, 0] * gate_ref[:, 0:1] + buf_ref[slot, 1] * gate_ref[:, 1:2])


def _combine(x, gate, ys, pos, tm):
    m, d = x.shape
    grid_spec = pltpu.PrefetchScalarGridSpec(
        num_scalar_prefetch=1,
        grid=(m // tm,),
        in_specs=[pl.BlockSpec((tm, d), lambda i, pos: (i, 0)),
                  pl.BlockSpec((tm, LANES), lambda i, pos: (i, 0)),
                  pl.BlockSpec(memory_space=pl.ANY)],
        out_specs=pl.BlockSpec((tm, d), lambda i, pos: (i, 0)),
        scratch_shapes=[pltpu.VMEM((2, MOE_TOP_K, tm, d), F32), pltpu.SemaphoreType.DMA((2,))],
    )
    return pl.pallas_call(
        functools.partial(_combine_body, tm=tm),
        grid_spec=grid_spec,
        out_shape=jax.ShapeDtypeStruct((m, d), F32),
        compiler_params=_cparams("arbitrary"),
        name="moe_combine",
    )(pos, x, gate, ys)


def _moe_plan(eid, n_tokens):
    a = n_tokens * MOE_TOP_K
    eid = eid.reshape(a)
    n_blocks = (a + N_EXPERTS * (MOE_BM - 1)) // MOE_BM
    n_slots = n_blocks * MOE_BM
    n_groups = n_blocks // MOE_GROUP_BLOCKS + N_EXPERTS
    onehot = (eid[:, None] == jnp.arange(N_EXPERTS, dtype=jnp.int32)[None, :]).astype(jnp.int32)
    rank = jnp.take_along_axis(jnp.cumsum(onehot, axis=0), eid[:, None], axis=1)[:, 0] - 1
    counts = jnp.sum(onehot, axis=0)
    blocks_e = (counts + MOE_BM - 1) // MOE_BM
    blk_start = jnp.cumsum(blocks_e) - blocks_e
    pos = (blk_start[eid] * MOE_BM + rank).astype(jnp.int32)
    tok = jnp.arange(a, dtype=jnp.int32) // MOE_TOP_K
    slot_tok = jnp.zeros((n_slots,), jnp.int32).at[pos].set(tok, unique_indices=True)
    groups_e = (blocks_e + MOE_GROUP_BLOCKS - 1) // MOE_GROUP_BLOCKS
    grp_end = jnp.cumsum(groups_e)
    grp_start = grp_end - groups_e
    gidx = jnp.arange(n_groups, dtype=jnp.int32)
    ge = jnp.minimum(jnp.searchsorted(grp_end, gidx, side="right"), N_EXPERTS - 1).astype(jnp.int32)
    local = gidx - grp_start[ge]
    active = gidx < grp_end[-1]
    gb0 = jnp.where(active, blk_start[ge] + local * MOE_GROUP_BLOCKS, 0).astype(jnp.int32)
    gnb = jnp.where(active, jnp.minimum(blocks_e[ge] - local * MOE_GROUP_BLOCKS, MOE_GROUP_BLOCKS),
                    0).astype(jnp.int32)
    last_e = ge[jnp.maximum(grp_end[-1] - 1, 0)]
    ge = jnp.where(active, ge, last_e).astype(jnp.int32)
    n_used = jnp.sum(blocks_e).astype(jnp.int32).reshape(1)
    return pos, slot_tok, ge, gb0, gnb, n_used, n_slots


def _layer(x, mem, positions, attn_norm_g, mem_norm_g, w_in, fox_forget_b, fox_q_norm_g,
           fox_k_norm_g, dsa_q_norm_g, dsa_k_norm_g, idx_k_norm_g, cross_q_norm_g, cross_k_norm_g,
           w_mem_kv, w_branch_fox, w_branch_dsa, w_branch_cross, w_out, ffn_norm_g,
           router_group_w, router_group_b, router_expert_w, router_expert_b,
           expert_w_gate, expert_w_up, expert_w_down):
    b, s, d = x.shape
    n = b * s
    n_mem = mem.shape[1]
    fw, dw, cw = FOX_HEADS * HEAD_DIM, DSA_HEADS * HEAD_DIM, CROSS_HEADS * HEAD_DIM
    iqw = IDX_HEADS * IDX_DIM
    tm = min(512, n)
    x2d = x.reshape(n, d)

    sizes = (fw, fw, fw, FOX_HEADS, dw, dw, dw, iqw, IDX_HEADS, IDX_DIM, cw, 3 * d)
    offs = [0]
    for sz in sizes:
        offs.append(offs[-1] + sz)
    (c_fq, c_fk, c_fv, c_fl, c_dq, c_dk, c_dv, c_iq, c_iw, c_ik, c_cq, c_g) = range(12)

    w_t = jnp.transpose(w_in, (2, 0, 1)).reshape(offs[-1], d)
    order = (c_dq, c_dk, c_fv, c_dv, c_fq, c_fk, c_cq, c_iq, c_g)
    row0, r = {}, 0
    for c in order:
        row0[c] = r
        r += sizes[c]
    w_all = jnp.concatenate([w_t[offs[c]:offs[c + 1]] for c in order], axis=0).astype(BF16)
    small_rows = [w_t[offs[c]:offs[c + 1]] for c in (c_ik, c_iw, c_fl)]
    small_rows.append(jnp.zeros((LANES - IDX_DIM - IDX_HEADS - FOX_HEADS, d), F32))
    w_small = jnp.concatenate(small_rows, axis=0).astype(BF16)
    q_unit = (HEAD_DIM ** -0.5) * LOG2E
    g_qk1 = jnp.concatenate([jnp.tile(fox_q_norm_g * q_unit, FOX_HEADS),
                             jnp.tile(fox_k_norm_g, FOX_HEADS),
                             jnp.tile(cross_q_norm_g * q_unit, CROSS_HEADS)])
    g_qk2 = jnp.concatenate([jnp.tile(dsa_q_norm_g * q_unit, DSA_HEADS),
                             jnp.tile(dsa_k_norm_g, DSA_HEADS)])
    g_ik = jnp.concatenate([idx_k_norm_g, jnp.zeros((LANES - IDX_DIM,), F32)])
    fb_lane = jnp.zeros((LANES,), F32).at[SMALL_F:SMALL_F + FOX_HEADS].set(fox_forget_b)

    pos_col = positions.reshape(n, 1).astype(F32)
    cd, sd, ci, si = _rope_tables(pos_col, tm)

    h = _rmsnorm(x2d, attn_norm_g, BF16, tm)
    tmp = min(PROJ_TM, n)
    qk1 = _proj(h, w_all, "norm", BF16, tmp, 1024, gains=g_qk1, w_rows=(row0[c_fq], 2 * fw + cw))
    qk2 = _proj(h, w_all, "norm_rope", BF16, tm, 768, gains=g_qk2, tables=(cd, sd),
                w_rows=(row0[c_dq], 2 * dw))
    iq = _proj(h, w_all, "rope_idx", BF16, tmp, 1024, tables=(ci, si), w_rows=(row0[c_iq], iqw))
    small = _proj(h, w_small, "raw", F32, tm, LANES, w_rows=(0, LANES))
    ika, ikb, wl, lf_col = _small(small, g_ik, fb_lane, ci, si, tm)

    tq, tk = min(ATT_TQ, s), min(ATT_TK, s)
    assert tq == tk and s % tq == 0
    cum_col = _forget_cumsum(lf_col.reshape(b, s, LANES))
    qk1 = qk1.reshape(b, s, 2 * fw + cw)
    qk2 = qk2.reshape(b, s, 2 * dw)
    q_aug, k_aug = _fox_pack(qk1, cum_col, FOX_HEADS, min(512, s))
    vt = _proj_t(h, w_all, (row0[c_fv], fw + dw), tmp, 768, tk).reshape(b, s // tk, fw + dw, tk)
    o_fox = _attention(q_aug, 0, k_aug, 0, vt, 0, (), "fox", FOX_HEADS, AUG_DIM, tq, tk)

    topk = min(INDEX_TOPK, s // 4)
    wt = jnp.transpose(wl.reshape(b, s, LANES)[:, :, :IDX_HEADS], (0, 2, 1))
    bias = _select_bias(iq.reshape(b, s, iqw), ika.reshape(b, s, LANES), ikb.reshape(b, s, LANES),
                        wt, topk, tq, tk)
    o_dsa = _attention(qk2, 0, qk2, 1, vt, 1, (bias,), "dsa", DSA_HEADS, HEAD_DIM, tq, tk)

    tmm = min(512, b * n_mem)
    m_n = _rmsnorm(mem.reshape(b * n_mem, d), mem_norm_g, BF16, tmm)
    w_kv = w_mem_kv.astype(BF16)
    mk = _proj(m_n, w_kv[:, :cw], "norm", BF16, tmm, cw, gains=jnp.tile(cross_k_norm_g, CROSS_HEADS))
    mv = _proj(m_n, w_kv[:, cw:], "raw", BF16, tmm, cw)
    o_cross = _cross_attention(qk1, (2 * fw) // cw, mk.reshape(b, n_mem, cw), mv.reshape(b, n_mem, cw),
                               min(512, s))

    x_mid = _merge(h, o_fox.reshape(n, fw), o_dsa.reshape(n, dw), o_cross.reshape(n, cw), x2d,
                   w_all, row0[c_g], w_branch_fox.astype(BF16), w_branch_dsa.astype(BF16),
                   w_branch_cross.astype(BF16), w_out.astype(BF16), tm, MERGE_TN)

    w_router = jnp.concatenate(
        [router_group_w, router_expert_w, jnp.zeros((d, LANES - N_GROUPS - N_EXPERTS), F32)],
        axis=1).astype(BF16)
    b_router = jnp.concatenate(
        [router_group_b, router_expert_b, jnp.zeros((LANES - N_GROUPS - N_EXPERTS,), F32)])
    h2, eid, gate = _router(x_mid, ffn_norm_g, w_router, b_router, tm)
    pos, slot_tok, ge, gb0, gnb, n_used, n_slots = _moe_plan(eid[:, :MOE_TOP_K], n)
    ys = _experts(h2, slot_tok, ge, gb0, gnb, n_used, expert_w_gate, expert_w_up, expert_w_down,
                  n_slots)
    out = _combine(x_mid, gate, ys, pos, min(256, n))
    return out.reshape(b, s, d)


def kernel(x, mem, positions, attn_norm_g, mem_norm_g, w_in, fox_forget_b, fox_q_norm_g, fox_k_norm_g, dsa_q_norm_g, dsa_k_norm_g, idx_k_norm_g, cross_q_norm_g, cross_k_norm_g, w_mem_kv, w_branch_fox, w_branch_dsa, w_branch_cross, w_out, ffn_norm_g, router_group_w, router_group_b, router_expert_w, router_expert_b, expert_w_gate, expert_w_up, expert_w_down):
    depth = w_in.shape[0]
    for layer in range(depth):
        x = _layer(
            x, mem, positions, attn_norm_g[layer], mem_norm_g[layer], w_in[layer:layer + 1],
            fox_forget_b[layer], fox_q_norm_g[layer], fox_k_norm_g[layer],
            dsa_q_norm_g[layer], dsa_k_norm_g[layer], idx_k_norm_g[layer],
            cross_q_norm_g[layer], cross_k_norm_g[layer], w_mem_kv[layer],
            w_branch_fox[layer], w_branch_dsa[layer], w_branch_cross[layer], w_out[layer],
            ffn_norm_g[layer], router_group_w[layer], router_group_b[layer],
            router_expert_w[layer], router_expert_b[layer], expert_w_gate[layer],
            expert_w_up[layer], expert_w_down[layer])
    return x
```

```python
import functools

import jax
import jax.numpy as jnp
from jax import lax
from jax.experimental import pallas as pl
from jax.experimental.pallas import tpu as pltpu

F32 = jnp.float32
BF16 = jnp.bfloat16

LANES = 128
HEAD_DIM = 128
FOX_HEADS = 6
DSA_HEADS = 6
CROSS_HEADS = 4
IDX_HEADS = 16
IDX_DIM = 64
ROPE_THETA = 500000.0
ROT_DIM = HEAD_DIM // 4
IDX_ROT_DIM = IDX_DIM // 4
INDEX_TOPK = 256
N_GROUPS = 4
EXPERTS_PER_GROUP = 8
N_EXPERTS = N_GROUPS * EXPERTS_PER_GROUP
MOE_TOP_K = 2
EPS = 1e-6

NEG = -1e30
INT_MIN = -(2 ** 31)
VMEM_LIMIT = 56 * 1024 * 1024

ATT_TQ = 256
ATT_TK = 256
ATT_HEADS_PER_LOOP = 6
PROJ_TM = 1024
MERGE_TN = 512
MERGE_SUB = 256
MOE_BM = 128
MOE_GROUP_BLOCKS = 8
MOE_MM_ROWS = 256
MOE_FC = 512
DMA_ISSUE_UNROLL = 8


def _cparams(*sem):
    return pltpu.CompilerParams(dimension_semantics=sem, vmem_limit_bytes=VMEM_LIMIT)


def _rmsnorm_body(x_ref, g_ref, o_ref):
    x = x_ref[...]
    r = lax.rsqrt(jnp.mean(x * x, axis=-1, keepdims=True) + EPS)
    o_ref[...] = ((x * r) * g_ref[...]).astype(o_ref.dtype)


def _rmsnorm(x2d, g, out_dtype, tm):
    m, d = x2d.shape
    return pl.pallas_call(
        _rmsnorm_body,
        grid=(m // tm,),
        in_specs=[pl.BlockSpec((tm, d), lambda i: (i, 0)),
                  pl.BlockSpec((1, d), lambda i: (0, 0))],
        out_specs=pl.BlockSpec((tm, d), lambda i: (i, 0)),
        out_shape=jax.ShapeDtypeStruct((m, d), out_dtype),
        compiler_params=_cparams("parallel"),
        name="rmsnorm",
    )(x2d, g.reshape(1, d))


W_PREP_ROWS = 128


def _w_prep_body(src_ref, w_ref, o_ref):
    del src_ref
    o_ref[...] = w_ref[:, 0, :].astype(o_ref.dtype)


def _w_prep(w_t, src_rows):
    _, _, k = w_t.shape
    n_chunks = src_rows.shape[0]
    grid_spec = pltpu.PrefetchScalarGridSpec(
        num_scalar_prefetch=1,
        grid=(n_chunks,),
        in_specs=[pl.BlockSpec((pl.Element(W_PREP_ROWS), pl.Element(1), pl.Element(k)),
                               lambda c, src: (src[c], 0, 0))],
        out_specs=pl.BlockSpec((W_PREP_ROWS, k), lambda c, src: (c, 0)),
    )
    return pl.pallas_call(
        _w_prep_body,
        grid_spec=grid_spec,
        out_shape=jax.ShapeDtypeStruct((n_chunks * W_PREP_ROWS, k), BF16),
        compiler_params=_cparams("parallel"),
        name="w_prep",
    )(src_rows, w_t)


IDX_FREQ_LANE = 32


def _rope_table_body(pos_ref, f_ref, cd_ref, sd_ref, ci_ref, si_ref):
    ang = pos_ref[...] * f_ref[...]
    c = jnp.cos(ang)
    s = jnp.sin(ang)
    lane = lax.broadcasted_iota(jnp.int32, c.shape, 1)
    hd = ROT_DIM // 2
    cd_ref[...] = jnp.where(lane < hd, c, jnp.where(lane < ROT_DIM, pltpu.roll(c, hd, 1), 1.0))
    sd_ref[...] = jnp.where(lane < hd, -s, jnp.where(lane < ROT_DIM, pltpu.roll(s, hd, 1), 0.0))
    m = lane & (IDX_DIM - 1)
    hi = IDX_ROT_DIM // 2
    ci_src = pltpu.roll(c, LANES - IDX_FREQ_LANE, 1)
    si_src = pltpu.roll(s, LANES - IDX_FREQ_LANE, 1)
    ci_ref[...] = jnp.where(m < IDX_ROT_DIM, ci_src, 1.0)
    si_ref[...] = jnp.where(m < hi, -si_src, jnp.where(m < IDX_ROT_DIM, si_src, 0.0))


def _rope_tables(pos_col, tm):
    n = pos_col.shape[0]

    def inv_freq(rot_dim):
        half = rot_dim // 2
        return jnp.power(jnp.float32(ROPE_THETA), -jnp.arange(half, dtype=F32) * 2.0 / rot_dim)

    fi = jnp.tile(inv_freq(IDX_ROT_DIM), 2)
    freq = jnp.zeros((LANES,), F32).at[:ROT_DIM // 2].set(inv_freq(ROT_DIM))
    for base in (IDX_FREQ_LANE, IDX_FREQ_LANE + IDX_DIM):
        freq = freq.at[base:base + IDX_ROT_DIM].set(fi)
    tab = pl.BlockSpec((tm, LANES), lambda i: (i, 0))
    return pl.pallas_call(
        _rope_table_body,
        grid=(n // tm,),
        in_specs=[pl.BlockSpec((tm, 1), lambda i: (i, 0)), pl.BlockSpec((1, LANES), lambda i: (0, 0))],
        out_specs=[tab] * 4,
        out_shape=[jax.ShapeDtypeStruct((n, LANES), F32)] * 4,
        compiler_params=_cparams("parallel"),
        name="rope_tables",
    )(pos_col, freq.reshape(1, LANES))


def _apply_rope(a, c, s, half, period):
    lane = lax.broadcasted_iota(jnp.int32, a.shape, 1)
    first = (lane & (period - 1)) < half
    partner = jnp.where(first, pltpu.roll(a, LANES - half, 1), pltpu.roll(a, half, 1))
    return a * c + partner * s


def _proj_body(*refs, mode, n_lane_groups, w_rows):
    h_ref, w_ref = refs[0], refs[1]
    o_ref = refs[-1]

    if w_rows:
        acc = lax.dot_general(h_ref[...], w_ref[...], (((1,), (1,)), ((), ())),
                              preferred_element_type=F32)
    else:
        acc = jnp.dot(h_ref[...], w_ref[...], preferred_element_type=F32)
    if mode == "raw":
        o_ref[...] = acc.astype(o_ref.dtype)
        return
    for g in range(n_lane_groups):
        sl = slice(g * LANES, (g + 1) * LANES)
        a = acc[:, sl]
        if mode in ("norm", "norm_rope"):
            g_ref = refs[2]
            a = (a * lax.rsqrt(jnp.mean(a * a, axis=-1, keepdims=True) + EPS)) * g_ref[:, sl]
        if mode == "norm_rope":
            a = _apply_rope(a, refs[3][...], refs[4][...], ROT_DIM // 2, HEAD_DIM)
        if mode == "rope_idx":
            a = _apply_rope(a, refs[2][...], refs[3][...], IDX_ROT_DIM // 2, IDX_DIM)
        o_ref[:, sl] = a.astype(o_ref.dtype)


def _proj_t_body(h_ref, w_ref, o_ref, *, tk):
    acc = lax.dot_general(w_ref[...], h_ref[...], (((1,), (1,)), ((), ())),
                          preferred_element_type=F32)
    for q in range(o_ref.shape[0]):
        o_ref[q] = acc[:, q * tk:(q + 1) * tk].astype(o_ref.dtype)


def _proj_t(h, w, w_rows, tm, tn, tk):
    m, k = h.shape
    row0, n = w_rows
    assert row0 % tn == 0 and n % tn == 0 and tm % tk == 0
    return pl.pallas_call(
        functools.partial(_proj_t_body, tk=tk),
        grid=(n // tn, m // tm),
        in_specs=[pl.BlockSpec((tm, k), lambda j, i: (i, 0)),
                  pl.BlockSpec((tn, k), lambda j, i: (row0 // tn + j, 0))],
        out_specs=pl.BlockSpec((tm // tk, tn, tk), lambda j, i: (i, j, 0)),
        out_shape=jax.ShapeDtypeStruct((m // tk, n, tk), BF16),
        compiler_params=_cparams("parallel", "parallel"),
        name="proj_t",
    )(h, w)


def _proj(h, w, mode, out_dtype, tm, tn, gains=None, tables=None, w_rows=None):
    m, k = h.shape
    if w_rows is None:
        n = w.shape[1]
        w_spec = pl.BlockSpec((k, tn), lambda j, i: (0, j))
    else:
        row0, n = w_rows
        assert row0 % tn == 0 and n % tn == 0
        w_spec = pl.BlockSpec((tn, k), lambda j, i: (row0 // tn + j, 0))
    in_specs = [pl.BlockSpec((tm, k), lambda j, i: (i, 0)), w_spec]
    args = [h, w]
    if gains is not None:
        in_specs.append(pl.BlockSpec((1, tn), lambda j, i: (0, j)))
        args.append(gains.reshape(1, n))
    if tables is not None:
        for t in tables:
            in_specs.append(pl.BlockSpec((tm, LANES), lambda j, i: (i, 0)))
            args.append(t)
    return pl.pallas_call(
        functools.partial(_proj_body, mode=mode, n_lane_groups=tn // LANES,
                          w_rows=w_rows is not None),
        grid=(n // tn, m // tm),
        in_specs=in_specs,
        out_specs=pl.BlockSpec((tm, tn), lambda j, i: (i, j)),
        out_shape=jax.ShapeDtypeStruct((m, n), out_dtype),
        compiler_params=_cparams("parallel", "parallel"),
        name="proj_" + mode,
    )(*args)


SMALL_IK = 0
SMALL_IW = 64
SMALL_F = 80


def _small_body(x_ref, gk_ref, fb_ref, c_ref, s_ref, ika_ref, ikb_ref, wl_ref, lf_ref):
    x = x_ref[...]
    lane = lax.broadcasted_iota(jnp.int32, x.shape, 1)
    is_ik = lane < IDX_DIM
    ik = jnp.where(is_ik, x, 0.0)
    ms = jnp.sum(ik * ik, axis=-1, keepdims=True) * (1.0 / IDX_DIM)
    ik = (ik * lax.rsqrt(ms + EPS)) * gk_ref[...]
    ik = _apply_rope(ik, c_ref[...], s_ref[...], IDX_ROT_DIM // 2, IDX_DIM)
    ik = jnp.where(is_ik, ik, 0.0)
    ika_ref[...] = ik.astype(BF16)
    ikb_ref[...] = pltpu.roll(ik, IDX_DIM, 1).astype(BF16)
    wl_ref[...] = pltpu.roll(x, LANES - SMALL_IW, 1) * (2.0 ** -5)
    z = x + fb_ref[...]
    log_f = jnp.minimum(z, 0.0) - jnp.log1p(jnp.exp(-jnp.abs(z)))
    lf_ref[...] = pltpu.roll(log_f, LANES - SMALL_F, 1)


def _small(x, idx_k_gain_lane, forget_b_lane, ci, si, tm):
    n = x.shape[0]
    row = pl.BlockSpec((1, LANES), lambda i: (0, 0))
    tab = pl.BlockSpec((tm, LANES), lambda i: (i, 0))
    return pl.pallas_call(
        _small_body,
        grid=(n // tm,),
        in_specs=[tab, row, row, tab, tab],
        out_specs=[tab, tab, tab, tab],
        out_shape=[jax.ShapeDtypeStruct((n, LANES), BF16), jax.ShapeDtypeStruct((n, LANES), BF16),
                   jax.ShapeDtypeStruct((n, LANES), F32), jax.ShapeDtypeStruct((n, LANES), F32)],
        compiler_params=_cparams("parallel"),
        name="small_cols",
    )(x, idx_k_gain_lane.reshape(1, LANES), forget_b_lane.reshape(1, LANES), ci, si)


def _cumsum_body(col_ref, ccol_ref, *, seq):
    c = col_ref[...]
    ci = lax.broadcasted_iota(jnp.int32, c.shape, 0)
    sh = 1
    while sh < seq:
        c = c + jnp.where(ci >= sh, pltpu.roll(c, sh, 0), 0.0)
        sh *= 2
    ccol_ref[...] = c


def _forget_cumsum(lf_col):
    b, s, _ = lf_col.shape
    return pl.pallas_call(
        functools.partial(_cumsum_body, seq=s),
        grid=(b,),
        in_specs=[pl.BlockSpec((None, s, LANES), lambda i: (i, 0, 0))],
        out_specs=pl.BlockSpec((None, s, LANES), lambda i: (i, 0, 0)),
        out_shape=jax.ShapeDtypeStruct((b, s, LANES), F32),
        compiler_params=_cparams("parallel"),
        name="forget_cumsum",
    )(lf_col)


AUG_DIM = 2 * HEAD_DIM
LOG2E = 1.4426950408889634


def _split3(x):
    hi = x.astype(BF16).astype(F32)
    r1 = x - hi
    mid = r1.astype(BF16).astype(F32)
    lo = (r1 - mid).astype(BF16).astype(F32)
    return hi, mid, lo


def _fox_pack_body(q_ref, k_ref, cum_ref, qa_ref, ka_ref, *, n_heads):
    rows = q_ref.shape[0]
    lane = lax.broadcasted_iota(jnp.int32, (rows, LANES), 1)
    for h in range(n_heads):
        c = jnp.broadcast_to(cum_ref[:, h:h + 1], (rows, LANES)) * LOG2E
        qh, qm, ql = _split3(c)
        kh, km, kl = _split3(-c)
        q_aug = jnp.where(lane == 0, qh, jnp.where(lane == 1, qm, jnp.where(
            lane == 2, ql, jnp.where(lane < 6, 1.0, 0.0))))
        k_aug = jnp.where(lane < 3, 1.0, jnp.where(lane == 3, kh, jnp.where(
            lane == 4, km, jnp.where(lane == 5, kl, 0.0))))
        base = h * AUG_DIM
        qa_ref[:, base:base + HEAD_DIM] = q_ref[:, h * HEAD_DIM:(h + 1) * HEAD_DIM]
        ka_ref[:, base:base + HEAD_DIM] = k_ref[:, h * HEAD_DIM:(h + 1) * HEAD_DIM]
        qa_ref[:, base + HEAD_DIM:base + AUG_DIM] = q_aug.astype(BF16)
        ka_ref[:, base + HEAD_DIM:base + AUG_DIM] = k_aug.astype(BF16)


def _fox_pack(qk, cum_col, n_heads, ts):
    b, s, _ = qk.shape
    w = n_heads * HEAD_DIM
    wa = n_heads * AUG_DIM
    return pl.pallas_call(
        functools.partial(_fox_pack_body, n_heads=n_heads),
        grid=(b, s // ts),
        in_specs=[pl.BlockSpec((None, ts, w), lambda bi, i: (bi, i, 0)),
                  pl.BlockSpec((None, ts, w), lambda bi, i: (bi, i, 1)),
                  pl.BlockSpec((None, ts, LANES), lambda bi, i: (bi, i, 0))],
        out_specs=[pl.BlockSpec((None, ts, wa), lambda bi, i: (bi, i, 0)),
                   pl.BlockSpec((None, ts, wa), lambda bi, i: (bi, i, 0))],
        out_shape=[jax.ShapeDtypeStruct((b, s, wa), BF16)] * 2,
        compiler_params=_cparams("parallel", "parallel"),
        name="fox_pack",
    )(qk, qk, cum_col)


def _attn_body(q_ref, k_ref, vt_ref, *rest, tq, tk, n_heads, kdim, mode, heads_per_loop):
    o_ref = rest[-1]
    qi = pl.program_id(1)
    key0 = lax.broadcasted_iota(jnp.int32, (tk, tq), 0)
    qry = qi * tq + lax.broadcasted_iota(jnp.int32, (tk, tq), 1)
    def scores(h, j):
        off = pl.multiple_of(j * tk, tk)
        q = q_ref[:, h * kdim:(h + 1) * kdim]
        k = k_ref[pl.ds(off, tk), h * kdim:(h + 1) * kdim]
        return lax.dot_general(k, q, (((1,), (1,)), ((), ())), preferred_element_type=F32)

    def flush(h, j, p, alpha, acc):
        vt = vt_ref[j, h * HEAD_DIM:(h + 1) * HEAD_DIM, :]
        return alpha * acc + jnp.dot(vt, p, preferred_element_type=F32)

    def softmax_step(h, j, m, l, masked):
        s = scores(h, j)
        if mode == "dsa":
            s = s + rest[0][j]
        if masked:
            s = jnp.where(key0 + j * tk <= qry, s, NEG)
        m_new = jnp.maximum(m, jnp.max(s, axis=0, keepdims=True))
        alpha = jnp.exp2(m - m_new)
        p = jnp.exp2(s - m_new)
        l = alpha * l + jnp.sum(p, axis=0, keepdims=True)
        return p.astype(BF16), alpha, m_new, l

    last = qi
    for h0 in range(0, n_heads, heads_per_loop):
        heads = range(h0, h0 + heads_per_loop)

        def step(j, carries, masked=False, heads=heads):
            prev = jnp.maximum(j - 1, 0)
            out = []
            for h, (p, alpha, m, l, acc) in zip(heads, carries):
                acc = flush(h, prev, p, alpha, acc)
                p, alpha, m, l = softmax_step(h, j, m, l, masked)
                out.append((p, alpha, m, l, acc))
            return tuple(out)

        carries = tuple(
            (jnp.zeros((tk, tq), BF16), jnp.ones((1, tq), F32), jnp.full((1, tq), NEG, F32),
             jnp.zeros((1, tq), F32), jnp.zeros((HEAD_DIM, tq), F32)) for h in heads)
        n_loop = last if mode == "fox" else last + 1
        carries = lax.fori_loop(0, n_loop, step, carries)
        if mode == "fox":
            carries = step(last, carries, masked=True)
        for h, (p, alpha, m, l, acc) in zip(heads, carries):
            acc = flush(h, last, p, alpha, acc)
            o_ref[:, h * HEAD_DIM:(h + 1) * HEAD_DIM] = jnp.transpose(acc / l).astype(o_ref.dtype)


def _attention(q_arr, q_blk, k_arr, k_blk, vt_arr, v_blk, extras, mode, n_heads, kdim, tq, tk):
    b, s, _ = q_arr.shape
    w = n_heads * HEAD_DIM
    wk = n_heads * kdim
    in_specs = [pl.BlockSpec((None, tq, wk), lambda bi, qi: (bi, qi, q_blk)),
                pl.BlockSpec((None, s, wk), lambda bi, qi: (bi, 0, k_blk)),
                pl.BlockSpec((None, s // tk, w, tk), lambda bi, qi: (bi, 0, v_blk, 0))]
    if mode == "dsa":
        in_specs += [pl.BlockSpec((None, None, s // tk, tk, tq), lambda bi, qi: (bi, qi, 0, 0, 0))]
    return pl.pallas_call(
        functools.partial(_attn_body, tq=tq, tk=tk, n_heads=n_heads, kdim=kdim, mode=mode,
                          heads_per_loop=ATT_HEADS_PER_LOOP),
        grid=(b, s // tq),
        in_specs=in_specs,
        out_specs=pl.BlockSpec((None, tq, w), lambda bi, qi: (bi, qi, 0)),
        out_shape=jax.ShapeDtypeStruct((b, s, w), BF16),
        compiler_params=_cparams("parallel", "parallel"),
        name="attn_" + mode,
    )(q_arr, k_arr, vt_arr, *extras)


def _select_body(iq_ref, ika_ref, ikb_ref, wt_ref, o_ref, keys_ref, half_ref, *, tq, tk, n_chunks, topk):
    qi = pl.program_id(1)
    used = qi + 1
    key0 = lax.broadcasted_iota(jnp.int32, (tk, tq), 0)
    qry = qi * tq + lax.broadcasted_iota(jnp.int32, (tk, tq), 1)

    def score_chunk(c, _):
        off = pl.multiple_of(c * tk, tk)
        ka = ika_ref[pl.ds(off, tk), :]
        kb = ikb_ref[pl.ds(off, tk), :]
        acc = jnp.zeros((tk, tq), F32)
        for j in range(IDX_HEADS):
            pair = iq_ref[:, (j // 2) * LANES:(j // 2 + 1) * LANES]
            logit = lax.dot_general(ka if j % 2 == 0 else kb, pair, (((1,), (1,)), ((), ())),
                                    preferred_element_type=F32)
            acc = acc + wt_ref[j:j + 1, :] * jnp.maximum(logit, 0.0)
        bits = lax.bitcast_convert_type(acc, jnp.int32)
        key = jnp.where(bits < 0, bits ^ jnp.int32(0x7FFFFFFF), bits)
        key = jnp.where(key0 + off <= qry, key, jnp.int32(INT_MIN))
        keys_ref[c] = key
        half_ref[c] = lax.shift_right_arithmetic(key, 16).astype(jnp.int16)
        return 0

    lax.fori_loop(0, used, score_chunk, 0)

    one16 = jnp.int16(1)
    zero16 = jnp.int16(0)

    def count_ge(cand16):
        def count_chunk(c, cnt):
            ge = jnp.where(half_ref[c] >= cand16, one16, zero16)
            parts = [ge[r:r + 16] for r in range(0, tk, 16)]
            while len(parts) > 1:
                parts = [a + b for a, b in zip(parts[::2], parts[1::2])]
            return cnt + parts[0]

        cnt = lax.fori_loop(0, used, count_chunk, jnp.zeros((16, tq), jnp.int16))
        return jnp.sum(cnt.astype(jnp.int32), axis=0, keepdims=True)

    def search16(need):
        def bit_step(i, carry):
            prefix, kept = carry
            cand_u = prefix | jnp.left_shift(jnp.int32(1), 15 - i)
            cnt = count_ge((cand_u - 32768).astype(jnp.int16))
            ok = cnt >= need
            return jnp.where(ok, cand_u, prefix), jnp.where(ok, cnt, kept)

        init = (jnp.zeros((1, tq), jnp.int32), jnp.full((1, tq), tk, jnp.int32) * used)
        prefix, kept = lax.fori_loop(0, 16, bit_step, init)
        return prefix - 32768, kept

    hi, _ = search16(topk)
    above = count_ge(jnp.minimum(hi + 1, 32767).astype(jnp.int16))
    above = jnp.where(hi >= 32767, 0, above)
    hi16 = hi.astype(jnp.int16)

    def low_chunk(c, _):
        low = (keys_ref[c] & jnp.int32(0xFFFF)) - 32768
        half_ref[c] = jnp.where(half_ref[c] == hi16, low.astype(jnp.int16), jnp.int16(-32768))
        return 0

    lax.fori_loop(0, used, low_chunk, 0)
    lo, at_or_above_lo = search16(topk - above)
    thr = lax.shift_left(hi, 16) | (lo + 32768)
    thr = jnp.maximum(thr, jnp.int32(INT_MIN + 1))
    tied = jnp.max(above + at_or_above_lo) > topk

    def count_keys(pred):
        def chunk(c, cnt):
            hit = jnp.where(pred(keys_ref[c]), 1.0, 0.0)
            return cnt + jnp.sum(hit.reshape(tk // 8, 8, tq), axis=0)

        cnt = lax.fori_loop(0, used, chunk, jnp.zeros((8, tq), F32))
        return jnp.sum(cnt, axis=0, keepdims=True)

    @pl.when(jnp.logical_not(tied))
    def _():
        def write_chunk(c, _):
            o_ref[c] = jnp.where(keys_ref[c] >= thr, 0.0, NEG)
            return 0

        lax.fori_loop(0, used, write_chunk, 0)

    @pl.when(tied)
    def _():
        need = topk - count_keys(lambda k: k > thr)
        earlier = jnp.where(lax.broadcasted_iota(jnp.int32, (tk, tk), 1)
                            < lax.broadcasted_iota(jnp.int32, (tk, tk), 0), 1.0, 0.0).astype(BF16)

        def write_chunk(c, seen):
            keys = keys_ref[c]
            tie = jnp.where(keys == thr, 1.0, 0.0)
            rank = jnp.dot(earlier, tie.astype(BF16), preferred_element_type=F32) + seen
            o_ref[c] = jnp.where(keys > thr, 0.0,
                                 jnp.where(keys == thr, jnp.where(rank < need, 0.0, NEG), NEG))
            return seen + jnp.sum(tie, axis=0, keepdims=True)

        lax.fori_loop(0, used, write_chunk, jnp.zeros((1, tq), F32))

    def fill_chunk(c, _):
        o_ref[c] = jnp.full((tk, tq), NEG, F32)
        return 0

    lax.fori_loop(used, n_chunks, fill_chunk, 0)


def _select_bias(iq, ika, ikb, wt, topk, tq, tk):
    b, s, _ = iq.shape
    n_chunks = s // tk
    return pl.pallas_call(
        functools.partial(_select_body, tq=tq, tk=tk, n_chunks=n_chunks, topk=topk),
        grid=(b, s // tq),
        in_specs=[pl.BlockSpec((None, tq, IDX_HEADS * IDX_DIM), lambda bi, qi: (bi, qi, 0)),
                  pl.BlockSpec((None, s, LANES), lambda bi, qi: (bi, 0, 0)),
                  pl.BlockSpec((None, s, LANES), lambda bi, qi: (bi, 0, 0)),
                  pl.BlockSpec((None, IDX_HEADS, tq), lambda bi, qi: (bi, 0, qi))],
        out_specs=pl.BlockSpec((None, None, n_chunks, tk, tq), lambda bi, qi: (bi, qi, 0, 0, 0)),
        out_shape=jax.ShapeDtypeStruct((b, s // tq, n_chunks, tk, tq), F32),
        scratch_shapes=[pltpu.VMEM((n_chunks, tk, tq), jnp.int32),
                        pltpu.VMEM((n_chunks, tk, tq), jnp.int16)],
        compiler_params=_cparams("parallel", "parallel"),
        name="index_select",
    )(iq, ika, ikb, wt)


def _cross_body(q_ref, k_ref, v_ref, o_ref, *, n_heads):
    for h in range(n_heads):
        hs = slice(h * HEAD_DIM, (h + 1) * HEAD_DIM)
        s = lax.dot_general(q_ref[:, hs], k_ref[:, hs], (((1,), (1,)), ((), ())),
                            preferred_element_type=F32)
        p = jnp.exp2(s - jnp.max(s, axis=-1, keepdims=True))
        l = jnp.sum(p, axis=-1, keepdims=True)
        o = jnp.dot(p.astype(BF16), v_ref[:, hs], preferred_element_type=F32)
        o_ref[:, hs] = (o / l).astype(o_ref.dtype)


def _cross_attention(q_arr, q_blk, mk, mv, tq):
    b, s, _ = q_arr.shape
    n_mem = mk.shape[1]
    w = CROSS_HEADS * HEAD_DIM
    return pl.pallas_call(
        functools.partial(_cross_body, n_heads=CROSS_HEADS),
        grid=(b, s // tq),
        in_specs=[pl.BlockSpec((None, tq, w), lambda bi, qi: (bi, qi, q_blk)),
                  pl.BlockSpec((None, n_mem, w), lambda bi, qi: (bi, 0, 0)),
                  pl.BlockSpec((None, n_mem, w), lambda bi, qi: (bi, 0, 0))],
        out_specs=pl.BlockSpec((None, tq, w), lambda bi, qi: (bi, qi, 0)),
        out_shape=jax.ShapeDtypeStruct((b, s, w), BF16),
        compiler_params=_cparams("parallel", "parallel"),
        name="cross_attn",
    )(q_arr, mk, mv)


def _sigmoid(x):
    return 1.0 / (1.0 + jnp.exp(-x))


def _merge_body(h_ref, of_ref, od_ref, oc_ref, x_ref, wgf_ref, wgd_ref, wgc_ref,
                wbf_ref, wbd_ref, wbc_ref, wo_ref, o_ref):
    j = pl.program_id(1)
    h = h_ref[...]
    tn = wo_ref.shape[0]
    sub = min(tn, MERGE_SUB)

    def gate_logits(wg_ref, cs):
        return lax.dot_general(h, wg_ref[cs, :], (((1,), (1,)), ((), ())), preferred_element_type=F32)

    part = None
    for c in range(tn // sub):
        cs = slice(c * sub, (c + 1) * sub)
        zf, zd, zc = gate_logits(wgf_ref, cs), gate_logits(wgd_ref, cs), gate_logits(wgc_ref, cs)
        bf = jnp.dot(of_ref[...], wbf_ref[:, cs], preferred_element_type=F32)
        bd = jnp.dot(od_ref[...], wbd_ref[:, cs], preferred_element_type=F32)
        bc = jnp.dot(oc_ref[...], wbc_ref[:, cs], preferred_element_type=F32)
        merged = (_sigmoid(zf) * bf + _sigmoid(zd) * bd) + _sigmoid(zc) * bc
        p = jnp.dot(merged.astype(BF16), wo_ref[cs, :], preferred_element_type=F32)
        part = p if part is None else part + p

    @pl.when(j == 0)
    def _():
        o_ref[...] = x_ref[...] + part

    @pl.when(j != 0)
    def _():
        o_ref[...] += part


def _merge(h, o_fox, o_dsa, o_cross, x, w_gates, gate_row0, wbf, wbd, wbc, w_out, tm, tn):
    m, d = x.shape
    nj = d // tn
    assert gate_row0 % tn == 0
    g0 = gate_row0 // tn
    row = lambda i, j: (i, 0)
    in_specs = [pl.BlockSpec((tm, d), row),
                pl.BlockSpec((tm, o_fox.shape[1]), row),
                pl.BlockSpec((tm, o_dsa.shape[1]), row),
                pl.BlockSpec((tm, o_cross.shape[1]), row),
                pl.BlockSpec((tm, d), row),
                pl.BlockSpec((tn, d), lambda i, j: (g0 + j, 0)),
                pl.BlockSpec((tn, d), lambda i, j: (g0 + nj + j, 0)),
                pl.BlockSpec((tn, d), lambda i, j: (g0 + 2 * nj + j, 0)),
                pl.BlockSpec((wbf.shape[0], tn), lambda i, j: (0, j)),
                pl.BlockSpec((wbd.shape[0], tn), lambda i, j: (0, j)),
                pl.BlockSpec((wbc.shape[0], tn), lambda i, j: (0, j)),
                pl.BlockSpec((tn, d), lambda i, j: (j, 0))]
    return pl.pallas_call(
        _merge_body,
        grid=(m // tm, nj),
        in_specs=in_specs,
        out_specs=pl.BlockSpec((tm, d), row),
        out_shape=jax.ShapeDtypeStruct((m, d), F32),
        compiler_params=_cparams("parallel", "arbitrary"),
        name="gated_merge",
    )(h, o_fox, o_dsa, o_cross, x, w_gates, w_gates, w_gates, wbf, wbd, wbc, w_out)


def _router_body(x_ref, g_ref, w_ref, b_ref, h_ref, eid_ref, gate_ref):
    x = x_ref[...]
    r = lax.rsqrt(jnp.mean(x * x, axis=-1, keepdims=True) + EPS)
    h = (x * r) * g_ref[...]
    hb = h.astype(BF16)
    bits = lax.bitcast_convert_type(hb.astype(F32), jnp.int32)
    half = bits.shape[1] // 2
    h_ref[...] = lax.shift_right_logical(bits[:, :half], 16) | (bits[:, half:] & jnp.int32(-65536))
    logits = jnp.dot(hb, w_ref[...], preferred_element_type=F32) + b_ref[...]
    lane = lax.broadcasted_iota(jnp.int32, logits.shape, 1).astype(F32)

    def masked_max(mask):
        return jnp.max(jnp.where(mask, logits, -jnp.inf), axis=-1, keepdims=True)

    def first_lane(mask):
        return jnp.min(jnp.where(mask, lane, float(LANES)), axis=-1, keepdims=True)

    is_group = lane < N_GROUPS
    gmax = masked_max(is_group)
    grp = first_lane(is_group & (logits == gmax))
    denom = jnp.sum(jnp.where(is_group, jnp.exp(logits - gmax), 0.0), axis=-1, keepdims=True)
    p_sel = 1.0 / denom
    lo = N_GROUPS + grp * EXPERTS_PER_GROUP
    in_grp = (lane >= lo) & (lane < lo + EXPERTS_PER_GROUP)
    e1 = masked_max(in_grp)
    j1 = first_lane(in_grp & (logits == e1))
    rest = in_grp & (lane != j1)
    e2 = masked_max(rest)
    j2 = first_lane(rest & (logits == e2))
    t = jnp.exp(e2 - e1)
    g1 = p_sel * (1.0 / (1.0 + t))
    g2 = p_sel * (t / (1.0 + t))
    eid = jnp.where(lane == 0.0, j1 - N_GROUPS, jnp.where(lane == 1.0, j2 - N_GROUPS, 0.0))
    eid_ref[...] = eid.astype(jnp.int32)
    gate_ref[...] = jnp.where(lane == 0.0, g1, jnp.where(lane == 1.0, g2, 0.0))


def _router(x, g, w_router, b_router, tm):
    m, d = x.shape
    return pl.pallas_call(
        _router_body,
        grid=(m // tm,),
        in_specs=[pl.BlockSpec((tm, d), lambda i: (i, 0)),
                  pl.BlockSpec((1, d), lambda i: (0, 0)),
                  pl.BlockSpec((d, LANES), lambda i: (0, 0)),
                  pl.BlockSpec((1, LANES), lambda i: (0, 0))],
        out_specs=[pl.BlockSpec((tm, d // 2), lambda i: (i, 0)),
                   pl.BlockSpec((tm, LANES), lambda i: (i, 0)),
                   pl.BlockSpec((tm, LANES), lambda i: (i, 0))],
        out_shape=[jax.ShapeDtypeStruct((m, d // 2), jnp.int32),
                   jax.ShapeDtypeStruct((m, LANES), jnp.int32),
                   jax.ShapeDtypeStruct((m, LANES), F32)],
        compiler_params=_cparams("parallel"),
        name="moe_router",
    )(x, g.reshape(1, d), w_router, b_router.reshape(1, LANES))


def _expert_body(ge_ref, gb0_ref, gnb_ref, used_ref, tok_ref, h_hbm, wg_ref, wu_ref, wd_ref, ys_hbm,
                 xs_ref, xb_ref, acc_ref, wgb_ref, wub_ref, wdb_ref, sem_in, sem_out,
                 *, bm, n_f, n_blocks):
    g = pl.program_id(0)
    f = pl.program_id(1)
    nb = gnb_ref[g]
    blk0 = gb0_ref[g]

    @pl.when(jnp.logical_and(g == 0, f == 0))
    def _():
        acc_ref[pl.ds(0, bm)] = jnp.zeros((bm,) + acc_ref.shape[1:], F32)

        def tail_copy(i):
            return pltpu.make_async_copy(acc_ref.at[pl.ds(0, bm)], ys_hbm.at[pl.ds(i * bm, bm)],
                                         sem_out)

        def start_tail(i, _):
            tail_copy(i).start()
            return 0

        lax.fori_loop(used_ref[0], n_blocks, start_tail, 0)

        def wait_tail(i, _):
            tail_copy(i).wait()
            return 0

        lax.fori_loop(used_ref[0], n_blocks, wait_tail, 0)

    def gather(grp):
        slot = grp % 2
        base = gb0_ref[grp] * bm

        def issue(r0, _):
            for u in range(DMA_ISSUE_UNROLL):
                r = r0 * DMA_ISSUE_UNROLL + u
                pltpu.make_async_copy(h_hbm.at[pl.ds(tok_ref[base + r], 1)],
                                      xs_ref.at[slot, pl.ds(r, 1)], sem_in.at[slot]).start()
            return 0

        lax.fori_loop(0, gnb_ref[grp] * (bm // DMA_ISSUE_UNROLL), issue, 0)

    def block_copy_out(i):
        return pltpu.make_async_copy(acc_ref.at[pl.ds(i * bm, bm)],
                                     ys_hbm.at[pl.ds((blk0 + i) * bm, bm)], sem_out)

    @pl.when(jnp.logical_and(g == 0, f == 0))
    def _():
        gather(g)

    @pl.when(jnp.logical_and(nb > 0, f == 0))
    def _():
        slot = g % 2

        def land(i, _):
            rows = pl.ds(pl.multiple_of(i * bm, bm), bm)
            pltpu.make_async_copy(h_hbm.at[pl.ds(0, bm)], xs_ref.at[slot, rows], sem_in.at[slot]).wait()
            return 0

        lax.fori_loop(0, nb, land, 0)

        half = xb_ref.shape[1] // 2

        def unpack(i, _):
            rows = pl.ds(pl.multiple_of(i * bm, bm), bm)
            w = xs_ref[slot, rows, :]
            lo = lax.bitcast_convert_type(lax.shift_left(w, 16), F32)
            hi = lax.bitcast_convert_type(w & jnp.int32(-65536), F32)
            xb_ref[rows, :half] = lo.astype(BF16)
            xb_ref[rows, half:] = hi.astype(BF16)
            return 0

        lax.fori_loop(0, nb, unpack, 0)

    @pl.when(jnp.logical_and(f == 0, g + 1 < pl.num_programs(0)))
    def _():
        gather(g + 1)

    def wait_out(n):
        def body(i, _):
            pltpu.make_async_copy(acc_ref.at[pl.ds(0, bm)], ys_hbm.at[pl.ds(0, bm)], sem_out).wait()
            return 0

        lax.fori_loop(0, n, body, 0)

    @pl.when(jnp.logical_and(f == 0, g > 0))
    def _():
        wait_out(gnb_ref[jnp.maximum(g - 1, 0)])

    @pl.when(nb > 0)
    def _():
        wgb_ref[...] = wg_ref[...].astype(BF16)
        wub_ref[...] = wu_ref[...].astype(BF16)
        wdb_ref[...] = wd_ref[...].astype(BF16)

        def swiglu(rows):
            xb = xb_ref[rows, :]
            a = jnp.dot(xb, wgb_ref[...], preferred_element_type=F32)
            u = jnp.dot(xb, wub_ref[...], preferred_element_type=F32)
            hid = (a * _sigmoid(a)) * u
            y = jnp.dot(hid.astype(BF16), wdb_ref[...], preferred_element_type=F32)

            @pl.when(f == 0)
            def _():
                acc_ref[rows, :] = y

            @pl.when(f != 0)
            def _():
                acc_ref[rows, :] += y

        per = MOE_MM_ROWS // bm

        def full_tile(i, _):
            swiglu(pl.ds(pl.multiple_of(i * MOE_MM_ROWS, MOE_MM_ROWS), MOE_MM_ROWS))
            return 0

        lax.fori_loop(0, nb // per, full_tile, 0)
        piece = per // 2
        while piece >= 1:
            done = (nb // (2 * piece)) * (2 * piece)

            @pl.when((nb // piece) % 2 == 1)
            def _(done=done, piece=piece):
                swiglu(pl.ds(pl.multiple_of(done * bm, piece * bm), piece * bm))

            piece //= 2

    @pl.when(jnp.logical_and(nb > 0, f == n_f - 1))
    def _():
        def start_out(i, _):
            block_copy_out(i).start()
            return 0

        lax.fori_loop(0, nb, start_out, 0)

        @pl.when(g == pl.num_programs(0) - 1)
        def _():
            wait_out(nb)


def _experts(h2, slot_tok, grp_e, grp_blk0, grp_nb, n_used, w_gate, w_up, w_down, n_slots):
    d = w_gate.shape[1]
    ff = w_gate.shape[2]
    n_f = ff // MOE_FC
    n_groups = grp_e.shape[0]
    rmax = MOE_GROUP_BLOCKS * MOE_BM

    def f_idx(g, f, gnb):
        return jnp.where(gnb[g] > 0, f, n_f - 1)

    grid_spec = pltpu.PrefetchScalarGridSpec(
        num_scalar_prefetch=5,
        grid=(n_groups, n_f),
        in_specs=[
            pl.BlockSpec(memory_space=pl.ANY),
            pl.BlockSpec((None, d, MOE_FC), lambda g, f, ge, gb0, gnb, *_: (ge[g], 0, f_idx(g, f, gnb))),
            pl.BlockSpec((None, d, MOE_FC), lambda g, f, ge, gb0, gnb, *_: (ge[g], 0, f_idx(g, f, gnb))),
            pl.BlockSpec((None, MOE_FC, d), lambda g, f, ge, gb0, gnb, *_: (ge[g], f_idx(g, f, gnb), 0)),
        ],
        out_specs=pl.BlockSpec(memory_space=pl.ANY),
        scratch_shapes=[pltpu.VMEM((2, rmax, d // 2), jnp.int32),
                        pltpu.VMEM((rmax, d), BF16),
                        pltpu.VMEM((rmax, d), F32),
                        pltpu.VMEM((d, MOE_FC), BF16),
                        pltpu.VMEM((d, MOE_FC), BF16),
                        pltpu.VMEM((MOE_FC, d), BF16),
                        pltpu.SemaphoreType.DMA((2,)),
                        pltpu.SemaphoreType.DMA(())],
    )
    return pl.pallas_call(
        functools.partial(_expert_body, bm=MOE_BM, n_f=n_f, n_blocks=n_slots // MOE_BM),
        grid_spec=grid_spec,
        out_shape=jax.ShapeDtypeStruct((n_slots, d), F32),
        compiler_params=_cparams("arbitrary", "arbitrary"),
        name="moe_experts",
    )(grp_e, grp_blk0, grp_nb, n_used, slot_tok, h2, w_gate, w_up, w_down)


def _combine_body(pos_ref, x_ref, gate_ref, ys_hbm, o_ref, buf_ref, sem, *, tm):
    i = pl.program_id(0)
    n_steps = pl.num_programs(0)

    def issue(step):
        slot = step % 2

        def body(r0, _):
            for u in range(DMA_ISSUE_UNROLL):
                r = r0 * DMA_ISSUE_UNROLL + u
                for k in range(MOE_TOP_K):
                    src = pos_ref[(step * tm + r) * MOE_TOP_K + k]
                    pltpu.make_async_copy(ys_hbm.at[pl.ds(src, 1)], buf_ref.at[slot, k, pl.ds(r, 1)],
                                          sem.at[slot]).start()
            return 0

        lax.fori_loop(0, tm // DMA_ISSUE_UNROLL, body, 0)

    @pl.when(i == 0)
    def _():
        issue(i)

    @pl.when(i + 1 < n_steps)
    def _():
        issue(i + 1)

    slot = i % 2
    for k in range(MOE_TOP_K):
        pltpu.make_async_copy(ys_hbm.at[pl.ds(0, tm)], buf_ref.at[slot, k], sem.at[slot]).wait()
    o_ref[...] = x_ref[...] + (buf_ref[slot, 0] * gate_ref[:, 0:1] + buf_ref[slot, 1] * gate_ref[:, 1:2])


def _combine(x, gate, ys, pos, tm):
    m, d = x.shape
    grid_spec = pltpu.PrefetchScalarGridSpec(
        num_scalar_prefetch=1,
        grid=(m // tm,),
        in_specs=[pl.BlockSpec((tm, d), lambda i, pos: (i, 0)),
                  pl.BlockSpec((tm, LANES), lambda i, pos: (i, 0)),
                  pl.BlockSpec(memory_space=pl.ANY)],
        out_specs=pl.BlockSpec((tm, d), lambda i, pos: (i, 0)),
        scratch_shapes=[pltpu.VMEM((2, MOE_TOP_K, tm, d), F32), pltpu.SemaphoreType.DMA((2,))],
    )
    return pl.pallas_call(
        functools.partial(_combine_body, tm=tm),
        grid_spec=grid_spec,
        out_shape=jax.ShapeDtypeStruct((m, d), F32),
        compiler_params=_cparams("arbitrary"),
        name="moe_combine",
    )(pos, x, gate, ys)


def _moe_plan(eid, n_tokens):
    a = n_tokens * MOE_TOP_K
    eid = eid.reshape(a)
    n_blocks = (a + N_EXPERTS * (MOE_BM - 1)) // MOE_BM
    n_slots = n_blocks * MOE_BM
    n_groups = n_blocks // MOE_GROUP_BLOCKS + N_EXPERTS
    onehot = (eid[:, None] == jnp.arange(N_EXPERTS, dtype=jnp.int32)[None, :]).astype(jnp.int32)
    rank = jnp.take_along_axis(jnp.cumsum(onehot, axis=0), eid[:, None], axis=1)[:, 0] - 1
    counts = jnp.sum(onehot, axis=0)
    blocks_e = (counts + MOE_BM - 1) // MOE_BM
    blk_start = jnp.cumsum(blocks_e) - blocks_e
    pos = (blk_start[eid] * MOE_BM + rank).astype(jnp.int32)
    tok = jnp.arange(a, dtype=jnp.int32) // MOE_TOP_K
    slot_tok = jnp.zeros((n_slots,), jnp.int32).at[pos].set(tok, unique_indices=True)
    groups_e = (blocks_e + MOE_GROUP_BLOCKS - 1) // MOE_GROUP_BLOCKS
    grp_end = jnp.cumsum(groups_e)
    grp_start = grp_end - groups_e
    gidx = jnp.arange(n_groups, dtype=jnp.int32)
    ge = jnp.minimum(jnp.searchsorted(grp_end, gidx, side="right"), N_EXPERTS - 1).astype(jnp.int32)
    local = gidx - grp_start[ge]
    active = gidx < grp_end[-1]
    gb0 = jnp.where(active, blk_start[ge] + local * MOE_GROUP_BLOCKS, 0).astype(jnp.int32)
    gnb = jnp.where(active, jnp.minimum(blocks_e[ge] - local * MOE_GROUP_BLOCKS, MOE_GROUP_BLOCKS),
                    0).astype(jnp.int32)
    last_e = ge[jnp.maximum(grp_end[-1] - 1, 0)]
    ge = jnp.where(active, ge, last_e).astype(jnp.int32)
    n_used = jnp.sum(blocks_e).astype(jnp.int32).reshape(1)
    return pos, slot_tok, ge, gb0, gnb, n_used, n_slots


def _layer(x, mem, positions, attn_norm_g, mem_norm_g, w_in, fox_forget_b, fox_q_norm_g,
           fox_k_norm_g, dsa_q_norm_g, dsa_k_norm_g, idx_k_norm_g, cross_q_norm_g, cross_k_norm_g,
           w_mem_kv, w_branch_fox, w_branch_dsa, w_branch_cross, w_out, ffn_norm_g,
           router_group_w, router_group_b, router_expert_w, router_expert_b,
           expert_w_gate, expert_w_up, expert_w_down):
    b, s, d = x.shape
    n = b * s
    n_mem = mem.shape[1]
    fw, dw, cw = FOX_HEADS * HEAD_DIM, DSA_HEADS * HEAD_DIM, CROSS_HEADS * HEAD_DIM
    iqw = IDX_HEADS * IDX_DIM
    tm = min(512, n)
    x2d = x.reshape(n, d)

    sizes = (fw, fw, fw, FOX_HEADS, dw, dw, dw, iqw, IDX_HEADS, IDX_DIM, cw, 3 * d)
    offs = [0]
    for sz in sizes:
        offs.append(offs[-1] + sz)
    (c_fq, c_fk, c_fv, c_fl, c_dq, c_dk, c_dv, c_iq, c_iw, c_ik, c_cq, c_g) = range(12)

    w_t = jnp.transpose(w_in, (2, 0, 1)).reshape(offs[-1], d)
    order = (c_dq, c_dk, c_fv, c_dv, c_fq, c_fk, c_cq, c_iq, c_g)
    row0, r = {}, 0
    for c in order:
        row0[c] = r
        r += sizes[c]
    w_all = jnp.concatenate([w_t[offs[c]:offs[c + 1]] for c in order], axis=0).astype(BF16)
    small_rows = [w_t[offs[c]:offs[c + 1]] for c in (c_ik, c_iw, c_fl)]
    small_rows.append(jnp.zeros((LANES - IDX_DIM - IDX_HEADS - FOX_HEADS, d), F32))
    w_small = jnp.concatenate(small_rows, axis=0).astype(BF16)
    q_unit = (HEAD_DIM ** -0.5) * LOG2E
    g_qk1 = jnp.concatenate([jnp.tile(fox_q_norm_g * q_unit, FOX_HEADS),
                             jnp.tile(fox_k_norm_g, FOX_HEADS),
                             jnp.tile(cross_q_norm_g * q_unit, CROSS_HEADS)])
    g_qk2 = jnp.concatenate([jnp.tile(dsa_q_norm_g * q_unit, DSA_HEADS),
                             jnp.tile(dsa_k_norm_g, DSA_HEADS)])
    g_ik = jnp.concatenate([idx_k_norm_g, jnp.zeros((LANES - IDX_DIM,), F32)])
    fb_lane = jnp.zeros((LANES,), F32).at[SMALL_F:SMALL_F + FOX_HEADS].set(fox_forget_b)

    pos_col = positions.reshape(n, 1).astype(F32)
    cd, sd, ci, si = _rope_tables(pos_col, tm)

    h = _rmsnorm(x2d, attn_norm_g, BF16, tm)
    tmp = min(PROJ_TM, n)
    qk1 = _proj(h, w_all, "norm", BF16, tmp, 1024, gains=g_qk1, w_rows=(row0[c_fq], 2 * fw + cw))
    qk2 = _proj(h, w_all, "norm_rope", BF16, tm, 768, gains=g_qk2, tables=(cd, sd),
                w_rows=(row0[c_dq], 2 * dw))
    iq = _proj(h, w_all, "rope_idx", BF16, tmp, 1024, tables=(ci, si), w_rows=(row0[c_iq], iqw))
    small = _proj(h, w_small, "raw", F32, tm, LANES, w_rows=(0, LANES))
    ika, ikb, wl, lf_col = _small(small, g_ik, fb_lane, ci, si, tm)

    tq, tk = min(ATT_TQ, s), min(ATT_TK, s)
    assert tq == tk and s % tq == 0
    cum_col = _forget_cumsum(lf_col.reshape(b, s, LANES))
    qk1 = qk1.reshape(b, s, 2 * fw + cw)
    qk2 = qk2.reshape(b, s, 2 * dw)
    q_aug, k_aug = _fox_pack(qk1, cum_col, FOX_HEADS, min(512, s))
    vt = _proj_t(h, w_all, (row0[c_fv], fw + dw), tmp, 768, tk).reshape(b, s // tk, fw + dw, tk)
    o_fox = _attention(q_aug, 0, k_aug, 0, vt, 0, (), "fox", FOX_HEADS, AUG_DIM, tq, tk)

    topk = min(INDEX_TOPK, s // 4)
    wt = jnp.transpose(wl.reshape(b, s, LANES)[:, :, :IDX_HEADS], (0, 2, 1))
    bias = _select_bias(iq.reshape(b, s, iqw), ika.reshape(b, s, LANES), ikb.reshape(b, s, LANES),
                        wt, topk, tq, tk)
    o_dsa = _attention(qk2, 0, qk2, 1, vt, 1, (bias,), "dsa", DSA_HEADS, HEAD_DIM, tq, tk)

    tmm = min(512, b * n_mem)
    m_n = _rmsnorm(mem.reshape(b * n_mem, d), mem_norm_g, BF16, tmm)
    w_kv = w_mem_kv.astype(BF16)
    mk = _proj(m_n, w_kv[:, :cw], "norm", BF16, tmm, cw, gains=jnp.tile(cross_k_norm_g, CROSS_HEADS))
    mv = _proj(m_n, w_kv[:, cw:], "raw", BF16, tmm, cw)
    o_cross = _cross_attention(qk1, (2 * fw) // cw, mk.reshape(b, n_mem, cw), mv.reshape(b, n_mem, cw),
                               min(512, s))

    x_mid = _merge(h, o_fox.reshape(n, fw), o_dsa.reshape(n, dw), o_cross.reshape(n, cw), x2d,
                   w_all, row0[c_g], w_branch_fox.astype(BF16), w_branch_dsa.astype(BF16),
                   w_branch_cross.astype(BF16), w_out.astype(BF16), tm, MERGE_TN)

    w_router = jnp.concatenate(
        [router_group_w, router_expert_w, jnp.zeros((d, LANES - N_GROUPS - N_EXPERTS), F32)],
        axis=1).astype(BF16)
    b_router = jnp.concatenate(
        [router_group_b, router_expert_b, jnp.zeros((LANES - N_GROUPS - N_EXPERTS,), F32)])
    h2, eid, gate = _router(x_mid, ffn_norm_g, w_router, b_router, tm)
    pos, slot_tok, ge, gb0, gnb, n_used, n_slots = _moe_plan(eid[:, :MOE_TOP_K], n)
    ys = _experts(h2, slot_tok, ge, gb0, gnb, n_used, expert_w_gate, expert_w_up, expert_w_down,
                  n_slots)
    out = _combine(x_mid, gate, ys, pos, min(256, n))
    return out.reshape(b, s, d)


def kernel(x, mem, positions, attn_norm_g, mem_norm_g, w_in, fox_forget_b, fox_q_norm_g, fox_k_norm_g, dsa_q_norm_g, dsa_k_norm_g, idx_k_norm_g, cross_q_norm_g, cross_k_norm_g, w_mem_kv, w_branch_fox, w_branch_dsa, w_branch_cross, w_out, ffn_norm_g, router_group_w, router_group_b, router_expert_w, router_expert_b, expert_w_gate, expert_w_up, expert_w_down):
    depth = w_in.shape[0]
    for layer in range(depth):
        x = _layer(
            x, mem, positions, attn_norm_g[layer], mem_norm_g[layer], w_in[layer:layer + 1],
            fox_forget_b[layer], fox_q_norm_g[layer], fox_k_norm_g[layer],
            dsa_q_norm_g[layer], dsa_k_norm_g[layer], idx_k_norm_g[layer],
            cross_q_norm_g[layer], cross_k_norm_g[layer], w_mem_kv[layer],
            w_branch_fox[layer], w_branch_dsa[layer], w_branch_cross[layer], w_out[layer],
            ffn_norm_g[layer], router_group_w[layer], router_group_b[layer],
            router_expert_w[layer], router_expert_b[layer], expert_w_gate[layer],
            expert_w_up[layer], expert_w_down[layer])
    return x
```

```python
import functools

import jax
import jax.numpy as jnp
from jax import lax
from jax.experimental import pallas as pl
from jax.experimental.pallas import tpu as pltpu

F32 = jnp.float32
BF16 = jnp.bfloat16

LANES = 128
HEAD_DIM = 128
FOX_HEADS = 6
DSA_HEADS = 6
CROSS_HEADS = 4
IDX_HEADS = 16
IDX_DIM = 64
ROPE_THETA = 500000.0
ROT_DIM = HEAD_DIM // 4
IDX_ROT_DIM = IDX_DIM // 4
INDEX_TOPK = 256
N_GROUPS = 4
EXPERTS_PER_GROUP = 8
N_EXPERTS = N_GROUPS * EXPERTS_PER_GROUP
MOE_TOP_K = 2
EPS = 1e-6

NEG = -1e30
INT_MIN = -(2 ** 31)
VMEM_LIMIT = 56 * 1024 * 1024

ATT_TQ = 512
ATT_TK = 512
ATT_HEADS_PER_LOOP = 6
PROJ_TM = 1024
MERGE_TN = 512
MERGE_SUB = 256
MOE_BM = 128
MOE_GROUP_BLOCKS = 8
MOE_MM_ROWS = 256
MOE_FC = 512
DMA_ISSUE_UNROLL = 8


def _cparams(*sem):
    return pltpu.CompilerParams(dimension_semantics=sem, vmem_limit_bytes=VMEM_LIMIT)


def _rmsnorm_body(x_ref, g_ref, o_ref):
    x = x_ref[...]
    r = lax.rsqrt(jnp.mean(x * x, axis=-1, keepdims=True) + EPS)
    o_ref[...] = ((x * r) * g_ref[...]).astype(o_ref.dtype)


def _rmsnorm(x2d, g, out_dtype, tm):
    m, d = x2d.shape
    return pl.pallas_call(
        _rmsnorm_body,
        grid=(m // tm,),
        in_specs=[pl.BlockSpec((tm, d), lambda i: (i, 0)),
                  pl.BlockSpec((1, d), lambda i: (0, 0))],
        out_specs=pl.BlockSpec((tm, d), lambda i: (i, 0)),
        out_shape=jax.ShapeDtypeStruct((m, d), out_dtype),
        compiler_params=_cparams("parallel"),
        name="rmsnorm",
    )(x2d, g.reshape(1, d))


W_PREP_ROWS = 128


def _w_prep_body(src_ref, w_ref, o_ref):
    del src_ref
    o_ref[...] = w_ref[:, 0, :].astype(o_ref.dtype)


def _w_prep(w_t, src_rows):
    _, _, k = w_t.shape
    n_chunks = src_rows.shape[0]
    grid_spec = pltpu.PrefetchScalarGridSpec(
        num_scalar_prefetch=1,
        grid=(n_chunks,),
        in_specs=[pl.BlockSpec((pl.Element(W_PREP_ROWS), pl.Element(1), pl.Element(k)),
                               lambda c, src: (src[c], 0, 0))],
        out_specs=pl.BlockSpec((W_PREP_ROWS, k), lambda c, src: (c, 0)),
    )
    return pl.pallas_call(
        _w_prep_body,
        grid_spec=grid_spec,
        out_shape=jax.ShapeDtypeStruct((n_chunks * W_PREP_ROWS, k), BF16),
        compiler_params=_cparams("parallel"),
        name="w_prep",
    )(src_rows, w_t)


IDX_FREQ_LANE = 32


def _rope_table_body(pos_ref, f_ref, cd_ref, sd_ref, ci_ref, si_ref):
    ang = pos_ref[...] * f_ref[...]
    c = jnp.cos(ang)
    s = jnp.sin(ang)
    lane = lax.broadcasted_iota(jnp.int32, c.shape, 1)
    hd = ROT_DIM // 2
    cd_ref[...] = jnp.where(lane < hd, c, jnp.where(lane < ROT_DIM, pltpu.roll(c, hd, 1), 1.0))
    sd_ref[...] = jnp.where(lane < hd, -s, jnp.where(lane < ROT_DIM, pltpu.roll(s, hd, 1), 0.0))
    m = lane & (IDX_DIM - 1)
    hi = IDX_ROT_DIM // 2
    ci_src = pltpu.roll(c, LANES - IDX_FREQ_LANE, 1)
    si_src = pltpu.roll(s, LANES - IDX_FREQ_LANE, 1)
    ci_ref[...] = jnp.where(m < IDX_ROT_DIM, ci_src, 1.0)
    si_ref[...] = jnp.where(m < hi, -si_src, jnp.where(m < IDX_ROT_DIM, si_src, 0.0))


def _rope_tables(pos_col, tm):
    n = pos_col.shape[0]

    def inv_freq(rot_dim):
        half = rot_dim // 2
        return jnp.power(jnp.float32(ROPE_THETA), -jnp.arange(half, dtype=F32) * 2.0 / rot_dim)

    fi = jnp.tile(inv_freq(IDX_ROT_DIM), 2)
    freq = jnp.zeros((LANES,), F32).at[:ROT_DIM // 2].set(inv_freq(ROT_DIM))
    for base in (IDX_FREQ_LANE, IDX_FREQ_LANE + IDX_DIM):
        freq = freq.at[base:base + IDX_ROT_DIM].set(fi)
    tab = pl.BlockSpec((tm, LANES), lambda i: (i, 0))
    return pl.pallas_call(
        _rope_table_body,
        grid=(n // tm,),
        in_specs=[pl.BlockSpec((tm, 1), lambda i: (i, 0)), pl.BlockSpec((1, LANES), lambda i: (0, 0))],
        out_specs=[tab] * 4,
        out_shape=[jax.ShapeDtypeStruct((n, LANES), F32)] * 4,
        compiler_params=_cparams("parallel"),
        name="rope_tables",
    )(pos_col, freq.reshape(1, LANES))


def _apply_rope(a, c, s, half, period):
    lane = lax.broadcasted_iota(jnp.int32, a.shape, 1)
    first = (lane & (period - 1)) < half
    partner = jnp.where(first, pltpu.roll(a, LANES - half, 1), pltpu.roll(a, half, 1))
    return a * c + partner * s


def _proj_body(*refs, mode, n_lane_groups, w_rows):
    h_ref, w_ref = refs[0], refs[1]
    o_ref = refs[-1]

    if w_rows:
        acc = lax.dot_general(h_ref[...], w_ref[...], (((1,), (1,)), ((), ())),
                              preferred_element_type=F32)
    else:
        acc = jnp.dot(h_ref[...], w_ref[...], preferred_element_type=F32)
    if mode == "raw":
        o_ref[...] = acc.astype(o_ref.dtype)
        return
    for g in range(n_lane_groups):
        sl = slice(g * LANES, (g + 1) * LANES)
        a = acc[:, sl]
        if mode in ("norm", "norm_rope"):
            g_ref = refs[2]
            a = (a * lax.rsqrt(jnp.mean(a * a, axis=-1, keepdims=True) + EPS)) * g_ref[:, sl]
        if mode == "norm_rope":
            a = _apply_rope(a, refs[3][...], refs[4][...], ROT_DIM // 2, HEAD_DIM)
        if mode == "rope_idx":
            a = _apply_rope(a, refs[2][...], refs[3][...], IDX_ROT_DIM // 2, IDX_DIM)
        o_ref[:, sl] = a.astype(o_ref.dtype)


def _proj_t_body(h_ref, w_ref, o_ref, *, tk):
    acc = lax.dot_general(w_ref[...], h_ref[...], (((1,), (1,)), ((), ())),
                          preferred_element_type=F32)
    for q in range(o_ref.shape[0]):
        o_ref[q] = acc[:, q * tk:(q + 1) * tk].astype(o_ref.dtype)


def _proj_t(h, w, w_rows, tm, tn, tk):
    m, k = h.shape
    row0, n = w_rows
    assert row0 % tn == 0 and n % tn == 0 and tm % tk == 0
    return pl.pallas_call(
        functools.partial(_proj_t_body, tk=tk),
        grid=(n // tn, m // tm),
        in_specs=[pl.BlockSpec((tm, k), lambda j, i: (i, 0)),
                  pl.BlockSpec((tn, k), lambda j, i: (row0 // tn + j, 0))],
        out_specs=pl.BlockSpec((tm // tk, tn, tk), lambda j, i: (i, j, 0)),
        out_shape=jax.ShapeDtypeStruct((m // tk, n, tk), BF16),
        compiler_params=_cparams("parallel", "parallel"),
        name="proj_t",
    )(h, w)


def _proj(h, w, mode, out_dtype, tm, tn, gains=None, tables=None, w_rows=None):
    m, k = h.shape
    if w_rows is None:
        n = w.shape[1]
        w_spec = pl.BlockSpec((k, tn), lambda j, i: (0, j))
    else:
        row0, n = w_rows
        assert row0 % tn == 0 and n % tn == 0
        w_spec = pl.BlockSpec((tn, k), lambda j, i: (row0 // tn + j, 0))
    in_specs = [pl.BlockSpec((tm, k), lambda j, i: (i, 0)), w_spec]
    args = [h, w]
    if gains is not None:
        in_specs.append(pl.BlockSpec((1, tn), lambda j, i: (0, j)))
        args.append(gains.reshape(1, n))
    if tables is not None:
        for t in tables:
            in_specs.append(pl.BlockSpec((tm, LANES), lambda j, i: (i, 0)))
            args.append(t)
    return pl.pallas_call(
        functools.partial(_proj_body, mode=mode, n_lane_groups=tn // LANES,
                          w_rows=w_rows is not None),
        grid=(n // tn, m // tm),
        in_specs=in_specs,
        out_specs=pl.BlockSpec((tm, tn), lambda j, i: (i, j)),
        out_shape=jax.ShapeDtypeStruct((m, n), out_dtype),
        compiler_params=_cparams("parallel", "parallel"),
        name="proj_" + mode,
    )(*args)


SMALL_IK = 0
SMALL_IW = 64
SMALL_F = 80


def _small_body(x_ref, gk_ref, fb_ref, c_ref, s_ref, ika_ref, ikb_ref, wl_ref, lf_ref):
    x = x_ref[...]
    lane = lax.broadcasted_iota(jnp.int32, x.shape, 1)
    is_ik = lane < IDX_DIM
    ik = jnp.where(is_ik, x, 0.0)
    ms = jnp.sum(ik * ik, axis=-1, keepdims=True) * (1.0 / IDX_DIM)
    ik = (ik * lax.rsqrt(ms + EPS)) * gk_ref[...]
    ik = _apply_rope(ik, c_ref[...], s_ref[...], IDX_ROT_DIM // 2, IDX_DIM)
    ik = jnp.where(is_ik, ik, 0.0)
    ika_ref[...] = ik.astype(BF16)
    ikb_ref[...] = pltpu.roll(ik, IDX_DIM, 1).astype(BF16)
    wl_ref[...] = pltpu.roll(x, LANES - SMALL_IW, 1) * (2.0 ** -5)
    z = x + fb_ref[...]
    log_f = jnp.minimum(z, 0.0) - jnp.log1p(jnp.exp(-jnp.abs(z)))
    lf_ref[...] = pltpu.roll(log_f, LANES - SMALL_F, 1)


def _small(x, idx_k_gain_lane, forget_b_lane, ci, si, tm):
    n = x.shape[0]
    row = pl.BlockSpec((1, LANES), lambda i: (0, 0))
    tab = pl.BlockSpec((tm, LANES), lambda i: (i, 0))
    return pl.pallas_call(
        _small_body,
        grid=(n // tm,),
        in_specs=[tab, row, row, tab, tab],
        out_specs=[tab, tab, tab, tab],
        out_shape=[jax.ShapeDtypeStruct((n, LANES), BF16), jax.ShapeDtypeStruct((n, LANES), BF16),
                   jax.ShapeDtypeStruct((n, LANES), F32), jax.ShapeDtypeStruct((n, LANES), F32)],
        compiler_params=_cparams("parallel"),
        name="small_cols",
    )(x, idx_k_gain_lane.reshape(1, LANES), forget_b_lane.reshape(1, LANES), ci, si)


def _cumsum_body(col_ref, ccol_ref, *, seq):
    c = col_ref[...]
    ci = lax.broadcasted_iota(jnp.int32, c.shape, 0)
    sh = 1
    while sh < seq:
        c = c + jnp.where(ci >= sh, pltpu.roll(c, sh, 0), 0.0)
        sh *= 2
    ccol_ref[...] = c


def _forget_cumsum(lf_col):
    b, s, _ = lf_col.shape
    return pl.pallas_call(
        functools.partial(_cumsum_body, seq=s),
        grid=(b,),
        in_specs=[pl.BlockSpec((None, s, LANES), lambda i: (i, 0, 0))],
        out_specs=pl.BlockSpec((None, s, LANES), lambda i: (i, 0, 0)),
        out_shape=jax.ShapeDtypeStruct((b, s, LANES), F32),
        compiler_params=_cparams("parallel"),
        name="forget_cumsum",
    )(lf_col)


AUG_DIM = 2 * HEAD_DIM
LOG2E = 1.4426950408889634


def _split3(x):
    hi = x.astype(BF16).astype(F32)
    r1 = x - hi
    mid = r1.astype(BF16).astype(F32)
    lo = (r1 - mid).astype(BF16).astype(F32)
    return hi, mid, lo


def _fox_pack_body(q_ref, k_ref, cum_ref, qa_ref, ka_ref, *, n_heads):
    rows = q_ref.shape[0]
    lane = lax.broadcasted_iota(jnp.int32, (rows, LANES), 1)
    for h in range(n_heads):
        c = jnp.broadcast_to(cum_ref[:, h:h + 1], (rows, LANES)) * LOG2E
        qh, qm, ql = _split3(c)
        kh, km, kl = _split3(-c)
        q_aug = jnp.where(lane == 0, qh, jnp.where(lane == 1, qm, jnp.where(
            lane == 2, ql, jnp.where(lane < 6, 1.0, 0.0))))
        k_aug = jnp.where(lane < 3, 1.0, jnp.where(lane == 3, kh, jnp.where(
            lane == 4, km, jnp.where(lane == 5, kl, 0.0))))
        base = h * AUG_DIM
        qa_ref[:, base:base + HEAD_DIM] = q_ref[:, h * HEAD_DIM:(h + 1) * HEAD_DIM]
        ka_ref[:, base:base + HEAD_DIM] = k_ref[:, h * HEAD_DIM:(h + 1) * HEAD_DIM]
        qa_ref[:, base + HEAD_DIM:base + AUG_DIM] = q_aug.astype(BF16)
        ka_ref[:, base + HEAD_DIM:base + AUG_DIM] = k_aug.astype(BF16)


def _fox_pack(qk, cum_col, n_heads, ts):
    b, s, _ = qk.shape
    w = n_heads * HEAD_DIM
    wa = n_heads * AUG_DIM
    return pl.pallas_call(
        functools.partial(_fox_pack_body, n_heads=n_heads),
        grid=(b, s // ts),
        in_specs=[pl.BlockSpec((None, ts, w), lambda bi, i: (bi, i, 0)),
                  pl.BlockSpec((None, ts, w), lambda bi, i: (bi, i, 1)),
                  pl.BlockSpec((None, ts, LANES), lambda bi, i: (bi, i, 0))],
        out_specs=[pl.BlockSpec((None, ts, wa), lambda bi, i: (bi, i, 0)),
                   pl.BlockSpec((None, ts, wa), lambda bi, i: (bi, i, 0))],
        out_shape=[jax.ShapeDtypeStruct((b, s, wa), BF16)] * 2,
        compiler_params=_cparams("parallel", "parallel"),
        name="fox_pack",
    )(qk, qk, cum_col)


def _attn_body(q_ref, k_ref, vt_ref, *rest, tq, tk, n_heads, kdim, mode, heads_per_loop):
    o_ref = rest[-1]
    qi = pl.program_id(1)
    key0 = lax.broadcasted_iota(jnp.int32, (tk, tq), 0)
    qry = qi * tq + lax.broadcasted_iota(jnp.int32, (tk, tq), 1)
    def scores(h, j):
        off = pl.multiple_of(j * tk, tk)
        q = q_ref[:, h * kdim:(h + 1) * kdim]
        k = k_ref[pl.ds(off, tk), h * kdim:(h + 1) * kdim]
        return lax.dot_general(k, q, (((1,), (1,)), ((), ())), preferred_element_type=F32)

    def flush(h, j, p, alpha, acc):
        vt = vt_ref[j, h * HEAD_DIM:(h + 1) * HEAD_DIM, :]
        return alpha * acc + jnp.dot(vt, p, preferred_element_type=F32)

    def softmax_step(h, j, m, l, masked):
        s = scores(h, j)
        if mode == "dsa":
            s = s + rest[0][j]
        if masked:
            s = jnp.where(key0 + j * tk <= qry, s, NEG)
        m_new = jnp.maximum(m, jnp.max(s, axis=0, keepdims=True))
        alpha = jnp.exp2(m - m_new)
        p = jnp.exp2(s - m_new)
        l = alpha * l + jnp.sum(p, axis=0, keepdims=True)
        return p.astype(BF16), alpha, m_new, l

    last = qi
    for h0 in range(0, n_heads, heads_per_loop):
        heads = range(h0, h0 + heads_per_loop)

        def step(j, carries, masked=False, heads=heads):
            prev = jnp.maximum(j - 1, 0)
            out = []
            for h, (p, alpha, m, l, acc) in zip(heads, carries):
                acc = flush(h, prev, p, alpha, acc)
                p, alpha, m, l = softmax_step(h, j, m, l, masked)
                out.append((p, alpha, m, l, acc))
            return tuple(out)

        carries = tuple(
            (jnp.zeros((tk, tq), BF16), jnp.ones((1, tq), F32), jnp.full((1, tq), NEG, F32),
             jnp.zeros((1, tq), F32), jnp.zeros((HEAD_DIM, tq), F32)) for h in heads)
        n_loop = last if mode == "fox" else last + 1
        carries = lax.fori_loop(0, n_loop, step, carries)
        if mode == "fox":
            carries = step(last, carries, masked=True)
        for h, (p, alpha, m, l, acc) in zip(heads, carries):
            acc = flush(h, last, p, alpha, acc)
            o_ref[:, h * HEAD_DIM:(h + 1) * HEAD_DIM] = jnp.transpose(acc / l).astype(o_ref.dtype)


def _attention(q_arr, q_blk, k_arr, k_blk, vt_arr, v_blk, extras, mode, n_heads, kdim, tq, tk):
    b, s, _ = q_arr.shape
    w = n_heads * HEAD_DIM
    wk = n_heads * kdim
    in_specs = [pl.BlockSpec((None, tq, wk), lambda bi, qi: (bi, qi, q_blk)),
                pl.BlockSpec((None, s, wk), lambda bi, qi: (bi, 0, k_blk)),
                pl.BlockSpec((None, s // tk, w, tk), lambda bi, qi: (bi, 0, v_blk, 0))]
    if mode == "dsa":
        in_specs += [pl.BlockSpec((None, None, s // tk, tk, tq), lambda bi, qi: (bi, qi, 0, 0, 0))]
    return pl.pallas_call(
        functools.partial(_attn_body, tq=tq, tk=tk, n_heads=n_heads, kdim=kdim, mode=mode,
                          heads_per_loop=ATT_HEADS_PER_LOOP),
        grid=(b, s // tq),
        in_specs=in_specs,
        out_specs=pl.BlockSpec((None, tq, w), lambda bi, qi: (bi, qi, 0)),
        out_shape=jax.ShapeDtypeStruct((b, s, w), BF16),
        compiler_params=_cparams("parallel", "parallel"),
        name="attn_" + mode,
    )(q_arr, k_arr, vt_arr, *extras)


def _select_body(iq_ref, ika_ref, ikb_ref, wt_ref, o_ref, keys_ref, half_ref, *, tq, tk, n_chunks, topk):
    qi = pl.program_id(1)
    used = qi + 1
    key0 = lax.broadcasted_iota(jnp.int32, (tk, tq), 0)
    qry = qi * tq + lax.broadcasted_iota(jnp.int32, (tk, tq), 1)

    def score_chunk(c, _):
        off = pl.multiple_of(c * tk, tk)
        ka = ika_ref[pl.ds(off, tk), :]
        kb = ikb_ref[pl.ds(off, tk), :]
        acc = jnp.zeros((tk, tq), F32)
        for j in range(IDX_HEADS):
            pair = iq_ref[:, (j // 2) * LANES:(j // 2 + 1) * LANES]
            logit = lax.dot_general(ka if j % 2 == 0 else kb, pair, (((1,), (1,)), ((), ())),
                                    preferred_element_type=F32)
            acc = acc + wt_ref[j:j + 1, :] * jnp.maximum(logit, 0.0)
        bits = lax.bitcast_convert_type(acc, jnp.int32)
        key = jnp.where(bits < 0, bits ^ jnp.int32(0x7FFFFFFF), bits)
        key = jnp.where(key0 + off <= qry, key, jnp.int32(INT_MIN))
        keys_ref[c] = key
        half_ref[c] = lax.shift_right_arithmetic(key, 16).astype(jnp.int16)
        return 0

    lax.fori_loop(0, used, score_chunk, 0)

    one16 = jnp.int16(1)
    zero16 = jnp.int16(0)

    def count_ge(cand16):
        def count_chunk(c, cnt):
            ge = jnp.where(half_ref[c] >= cand16, one16, zero16)
            parts = [ge[r:r + 16] for r in range(0, tk, 16)]
            while len(parts) > 1:
                parts = [a + b for a, b in zip(parts[::2], parts[1::2])]
            return cnt + parts[0]

        cnt = lax.fori_loop(0, used, count_chunk, jnp.zeros((16, tq), jnp.int16))
        return jnp.sum(cnt.astype(jnp.int32), axis=0, keepdims=True)

    def search16(need):
        def bit_step(i, carry):
            prefix, kept = carry
            cand_u = prefix | jnp.left_shift(jnp.int32(1), 15 - i)
            cnt = count_ge((cand_u - 32768).astype(jnp.int16))
            ok = cnt >= need
            return jnp.where(ok, cand_u, prefix), jnp.where(ok, cnt, kept)

        init = (jnp.zeros((1, tq), jnp.int32), jnp.full((1, tq), tk, jnp.int32) * used)
        prefix, kept = lax.fori_loop(0, 16, bit_step, init)
        return prefix - 32768, kept

    hi, _ = search16(topk)
    above = count_ge(jnp.minimum(hi + 1, 32767).astype(jnp.int16))
    above = jnp.where(hi >= 32767, 0, above)
    hi16 = hi.astype(jnp.int16)

    def low_chunk(c, _):
        low = (keys_ref[c] & jnp.int32(0xFFFF)) - 32768
        half_ref[c] = jnp.where(half_ref[c] == hi16, low.astype(jnp.int16), jnp.int16(-32768))
        return 0

    lax.fori_loop(0, used, low_chunk, 0)
    lo, at_or_above_lo = search16(topk - above)
    thr = lax.shift_left(hi, 16) | (lo + 32768)
    thr = jnp.maximum(thr, jnp.int32(INT_MIN + 1))
    tied = jnp.max(above + at_or_above_lo) > topk

    def count_keys(pred):
        def chunk(c, cnt):
            hit = jnp.where(pred(keys_ref[c]), 1.0, 0.0)
            return cnt + jnp.sum(hit.reshape(tk // 8, 8, tq), axis=0)

        cnt = lax.fori_loop(0, used, chunk, jnp.zeros((8, tq), F32))
        return jnp.sum(cnt, axis=0, keepdims=True)

    @pl.when(jnp.logical_not(tied))
    def _():
        def write_chunk(c, _):
            o_ref[c] = jnp.where(keys_ref[c] >= thr, 0.0, NEG)
            return 0

        lax.fori_loop(0, used, write_chunk, 0)

    @pl.when(tied)
    def _():
        need = topk - count_keys(lambda k: k > thr)
        earlier = jnp.where(lax.broadcasted_iota(jnp.int32, (tk, tk), 1)
                            < lax.broadcasted_iota(jnp.int32, (tk, tk), 0), 1.0, 0.0).astype(BF16)

        def write_chunk(c, seen):
            keys = keys_ref[c]
            tie = jnp.where(keys == thr, 1.0, 0.0)
            rank = jnp.dot(earlier, tie.astype(BF16), preferred_element_type=F32) + seen
            o_ref[c] = jnp.where(keys > thr, 0.0,
                                 jnp.where(keys == thr, jnp.where(rank < need, 0.0, NEG), NEG))
            return seen + jnp.sum(tie, axis=0, keepdims=True)

        lax.fori_loop(0, used, write_chunk, jnp.zeros((1, tq), F32))

    def fill_chunk(c, _):
        o_ref[c] = jnp.full((tk, tq), NEG, F32)
        return 0

    lax.fori_loop(used, n_chunks, fill_chunk, 0)


def _select_bias(iq, ika, ikb, wt, topk, tq, tk):
    b, s, _ = iq.shape
    n_chunks = s // tk
    return pl.pallas_call(
        functools.partial(_select_body, tq=tq, tk=tk, n_chunks=n_chunks, topk=topk),
        grid=(b, s // tq),
        in_specs=[pl.BlockSpec((None, tq, IDX_HEADS * IDX_DIM), lambda bi, qi: (bi, qi, 0)),
                  pl.BlockSpec((None, s, LANES), lambda bi, qi: (bi, 0, 0)),
                  pl.BlockSpec((None, s, LANES), lambda bi, qi: (bi, 0, 0)),
                  pl.BlockSpec((None, IDX_HEADS, tq), lambda bi, qi: (bi, 0, qi))],
        out_specs=pl.BlockSpec((None, None, n_chunks, tk, tq), lambda bi, qi: (bi, qi, 0, 0, 0)),
        out_shape=jax.ShapeDtypeStruct((b, s // tq, n_chunks, tk, tq), F32),
        scratch_shapes=[pltpu.VMEM((n_chunks, tk, tq), jnp.int32),
                        pltpu.VMEM((n_chunks, tk, tq), jnp.int16)],
        compiler_params=_cparams("parallel", "parallel"),
        name="index_select",
    )(iq, ika, ikb, wt)


def _cross_body(q_ref, k_ref, v_ref, o_ref, *, n_heads):
    for h in range(n_heads):
        hs = slice(h * HEAD_DIM, (h + 1) * HEAD_DIM)
        s = lax.dot_general(q_ref[:, hs], k_ref[:, hs], (((1,), (1,)), ((), ())),
                            preferred_element_type=F32)
        p = jnp.exp2(s - jnp.max(s, axis=-1, keepdims=True))
        l = jnp.sum(p, axis=-1, keepdims=True)
        o = jnp.dot(p.astype(BF16), v_ref[:, hs], preferred_element_type=F32)
        o_ref[:, hs] = (o / l).astype(o_ref.dtype)


def _cross_attention(q_arr, q_blk, mk, mv, tq):
    b, s, _ = q_arr.shape
    n_mem = mk.shape[1]
    w = CROSS_HEADS * HEAD_DIM
    return pl.pallas_call(
        functools.partial(_cross_body, n_heads=CROSS_HEADS),
        grid=(b, s // tq),
        in_specs=[pl.BlockSpec((None, tq, w), lambda bi, qi: (bi, qi, q_blk)),
                  pl.BlockSpec((None, n_mem, w), lambda bi, qi: (bi, 0, 0)),
                  pl.BlockSpec((None, n_mem, w), lambda bi, qi: (bi, 0, 0))],
        out_specs=pl.BlockSpec((None, tq, w), lambda bi, qi: (bi, qi, 0)),
        out_shape=jax.ShapeDtypeStruct((b, s, w), BF16),
        compiler_params=_cparams("parallel", "parallel"),
        name="cross_attn",
    )(q_arr, mk, mv)


def _sigmoid(x):
    return 1.0 / (1.0 + jnp.exp(-x))


def _merge_body(h_ref, of_ref, od_ref, oc_ref, x_ref, wgf_ref, wgd_ref, wgc_ref,
                wbf_ref, wbd_ref, wbc_ref, wo_ref, o_ref):
    j = pl.program_id(1)
    h = h_ref[...]
    tn = wo_ref.shape[0]
    sub = min(tn, MERGE_SUB)

    def gate_logits(wg_ref, cs):
        return lax.dot_general(h, wg_ref[cs, :], (((1,), (1,)), ((), ())), preferred_element_type=F32)

    part = None
    for c in range(tn // sub):
        cs = slice(c * sub, (c + 1) * sub)
        zf, zd, zc = gate_logits(wgf_ref, cs), gate_logits(wgd_ref, cs), gate_logits(wgc_ref, cs)
        bf = jnp.dot(of_ref[...], wbf_ref[:, cs], preferred_element_type=F32)
        bd = jnp.dot(od_ref[...], wbd_ref[:, cs], preferred_element_type=F32)
        bc = jnp.dot(oc_ref[...], wbc_ref[:, cs], preferred_element_type=F32)
        merged = (_sigmoid(zf) * bf + _sigmoid(zd) * bd) + _sigmoid(zc) * bc
        p = jnp.dot(merged.astype(BF16), wo_ref[cs, :], preferred_element_type=F32)
        part = p if part is None else part + p

    @pl.when(j == 0)
    def _():
        o_ref[...] = x_ref[...] + part

    @pl.when(j != 0)
    def _():
        o_ref[...] += part


def _merge(h, o_fox, o_dsa, o_cross, x, w_gates, gate_row0, wbf, wbd, wbc, w_out, tm, tn):
    m, d = x.shape
    nj = d // tn
    assert gate_row0 % tn == 0
    g0 = gate_row0 // tn
    row = lambda i, j: (i, 0)
    in_specs = [pl.BlockSpec((tm, d), row),
                pl.BlockSpec((tm, o_fox.shape[1]), row),
                pl.BlockSpec((tm, o_dsa.shape[1]), row),
                pl.BlockSpec((tm, o_cross.shape[1]), row),
                pl.BlockSpec((tm, d), row),
                pl.BlockSpec((tn, d), lambda i, j: (g0 + j, 0)),
                pl.BlockSpec((tn, d), lambda i, j: (g0 + nj + j, 0)),
                pl.BlockSpec((tn, d), lambda i, j: (g0 + 2 * nj + j, 0)),
                pl.BlockSpec((wbf.shape[0], tn), lambda i, j: (0, j)),
                pl.BlockSpec((wbd.shape[0], tn), lambda i, j: (0, j)),
                pl.BlockSpec((wbc.shape[0], tn), lambda i, j: (0, j)),
                pl.BlockSpec((tn, d), lambda i, j: (j, 0))]
    return pl.pallas_call(
        _merge_body,
        grid=(m // tm, nj),
        in_specs=in_specs,
        out_specs=pl.BlockSpec((tm, d), row),
        out_shape=jax.ShapeDtypeStruct((m, d), F32),
        compiler_params=_cparams("parallel", "arbitrary"),
        name="gated_merge",
    )(h, o_fox, o_dsa, o_cross, x, w_gates, w_gates, w_gates, wbf, wbd, wbc, w_out)


def _router_body(x_ref, g_ref, w_ref, b_ref, h_ref, eid_ref, gate_ref):
    x = x_ref[...]
    r = lax.rsqrt(jnp.mean(x * x, axis=-1, keepdims=True) + EPS)
    h = (x * r) * g_ref[...]
    hb = h.astype(BF16)
    bits = lax.bitcast_convert_type(hb.astype(F32), jnp.int32)
    half = bits.shape[1] // 2
    h_ref[...] = lax.shift_right_logical(bits[:, :half], 16) | (bits[:, half:] & jnp.int32(-65536))
    logits = jnp.dot(hb, w_ref[...], preferred_element_type=F32) + b_ref[...]
    lane = lax.broadcasted_iota(jnp.int32, logits.shape, 1).astype(F32)

    def masked_max(mask):
        return jnp.max(jnp.where(mask, logits, -jnp.inf), axis=-1, keepdims=True)

    def first_lane(mask):
        return jnp.min(jnp.where(mask, lane, float(LANES)), axis=-1, keepdims=True)

    is_group = lane < N_GROUPS
    gmax = masked_max(is_group)
    grp = first_lane(is_group & (logits == gmax))
    denom = jnp.sum(jnp.where(is_group, jnp.exp(logits - gmax), 0.0), axis=-1, keepdims=True)
    p_sel = 1.0 / denom
    lo = N_GROUPS + grp * EXPERTS_PER_GROUP
    in_grp = (lane >= lo) & (lane < lo + EXPERTS_PER_GROUP)
    e1 = masked_max(in_grp)
    j1 = first_lane(in_grp & (logits == e1))
    rest = in_grp & (lane != j1)
    e2 = masked_max(rest)
    j2 = first_lane(rest & (logits == e2))
    t = jnp.exp(e2 - e1)
    g1 = p_sel * (1.0 / (1.0 + t))
    g2 = p_sel * (t / (1.0 + t))
    eid = jnp.where(lane == 0.0, j1 - N_GROUPS, jnp.where(lane == 1.0, j2 - N_GROUPS, 0.0))
    eid_ref[...] = eid.astype(jnp.int32)
    gate_ref[...] = jnp.where(lane == 0.0, g1, jnp.where(lane == 1.0, g2, 0.0))


def _router(x, g, w_router, b_router, tm):
    m, d = x.shape
    return pl.pallas_call(
        _router_body,
        grid=(m // tm,),
        in_specs=[pl.BlockSpec((tm, d), lambda i: (i, 0)),
                  pl.BlockSpec((1, d), lambda i: (0, 0)),
                  pl.BlockSpec((d, LANES), lambda i: (0, 0)),
                  pl.BlockSpec((1, LANES), lambda i: (0, 0))],
        out_specs=[pl.BlockSpec((tm, d // 2), lambda i: (i, 0)),
                   pl.BlockSpec((tm, LANES), lambda i: (i, 0)),
                   pl.BlockSpec((tm, LANES), lambda i: (i, 0))],
        out_shape=[jax.ShapeDtypeStruct((m, d // 2), jnp.int32),
                   jax.ShapeDtypeStruct((m, LANES), jnp.int32),
                   jax.ShapeDtypeStruct((m, LANES), F32)],
        compiler_params=_cparams("parallel"),
        name="moe_router",
    )(x, g.reshape(1, d), w_router, b_router.reshape(1, LANES))


def _expert_body(ge_ref, gb0_ref, gnb_ref, used_ref, tok_ref, h_hbm, wg_ref, wu_ref, wd_ref, ys_hbm,
                 xs_ref, xb_ref, acc_ref, wgb_ref, wub_ref, wdb_ref, sem_in, sem_out,
                 *, bm, n_f, n_blocks):
    g = pl.program_id(0)
    f = pl.program_id(1)
    nb = gnb_ref[g]
    blk0 = gb0_ref[g]

    @pl.when(jnp.logical_and(g == 0, f == 0))
    def _():
        acc_ref[pl.ds(0, bm)] = jnp.zeros((bm,) + acc_ref.shape[1:], F32)

        def tail_copy(i):
            return pltpu.make_async_copy(acc_ref.at[pl.ds(0, bm)], ys_hbm.at[pl.ds(i * bm, bm)],
                                         sem_out)

        def start_tail(i, _):
            tail_copy(i).start()
            return 0

        lax.fori_loop(used_ref[0], n_blocks, start_tail, 0)

        def wait_tail(i, _):
            tail_copy(i).wait()
            return 0

        lax.fori_loop(used_ref[0], n_blocks, wait_tail, 0)

    def gather(grp):
        slot = grp % 2
        base = gb0_ref[grp] * bm

        def issue(r0, _):
            for u in range(DMA_ISSUE_UNROLL):
                r = r0 * DMA_ISSUE_UNROLL + u
                pltpu.make_async_copy(h_hbm.at[pl.ds(tok_ref[base + r], 1)],
                                      xs_ref.at[slot, pl.ds(r, 1)], sem_in.at[slot]).start()
            return 0

        lax.fori_loop(0, gnb_ref[grp] * (bm // DMA_ISSUE_UNROLL), issue, 0)

    def block_copy_out(i):
        return pltpu.make_async_copy(acc_ref.at[pl.ds(i * bm, bm)],
                                     ys_hbm.at[pl.ds((blk0 + i) * bm, bm)], sem_out)

    @pl.when(jnp.logical_and(g == 0, f == 0))
    def _():
        gather(g)

    @pl.when(jnp.logical_and(nb > 0, f == 0))
    def _():
        slot = g % 2

        def land(i, _):
            rows = pl.ds(pl.multiple_of(i * bm, bm), bm)
            pltpu.make_async_copy(h_hbm.at[pl.ds(0, bm)], xs_ref.at[slot, rows], sem_in.at[slot]).wait()
            return 0

        lax.fori_loop(0, nb, land, 0)

        half = xb_ref.shape[1] // 2

        def unpack(i, _):
            rows = pl.ds(pl.multiple_of(i * bm, bm), bm)
            w = xs_ref[slot, rows, :]
            lo = lax.bitcast_convert_type(lax.shift_left(w, 16), F32)
            hi = lax.bitcast_convert_type(w & jnp.int32(-65536), F32)
            xb_ref[rows, :half] = lo.astype(BF16)
            xb_ref[rows, half:] = hi.astype(BF16)
            return 0

        lax.fori_loop(0, nb, unpack, 0)

    @pl.when(jnp.logical_and(f == 0, g + 1 < pl.num_programs(0)))
    def _():
        gather(g + 1)

    def wait_out(n):
        def body(i, _):
            pltpu.make_async_copy(acc_ref.at[pl.ds(0, bm)], ys_hbm.at[pl.ds(0, bm)], sem_out).wait()
            return 0

        lax.fori_loop(0, n, body, 0)

    @pl.when(jnp.logical_and(f == 0, g > 0))
    def _():
        wait_out(gnb_ref[jnp.maximum(g - 1, 0)])

    @pl.when(nb > 0)
    def _():
        wgb_ref[...] = wg_ref[...].astype(BF16)
        wub_ref[...] = wu_ref[...].astype(BF16)
        wdb_ref[...] = wd_ref[...].astype(BF16)

        def swiglu(rows):
            xb = xb_ref[rows, :]
            a = jnp.dot(xb, wgb_ref[...], preferred_element_type=F32)
            u = jnp.dot(xb, wub_ref[...], preferred_element_type=F32)
            hid = (a * _sigmoid(a)) * u
            y = jnp.dot(hid.astype(BF16), wdb_ref[...], preferred_element_type=F32)

            @pl.when(f == 0)
            def _():
                acc_ref[rows, :] = y

            @pl.when(f != 0)
            def _():
                acc_ref[rows, :] += y

        per = MOE_MM_ROWS // bm

        def full_tile(i, _):
            swiglu(pl.ds(pl.multiple_of(i * MOE_MM_ROWS, MOE_MM_ROWS), MOE_MM_ROWS))
            return 0

        lax.fori_loop(0, nb // per, full_tile, 0)
        piece = per // 2
        while piece >= 1:
            done = (nb // (2 * piece)) * (2 * piece)

            @pl.when((nb // piece) % 2 == 1)
            def _(done=done, piece=piece):
                swiglu(pl.ds(pl.multiple_of(done * bm, piece * bm), piece * bm))

            piece //= 2

    @pl.when(jnp.logical_and(nb > 0, f == n_f - 1))
    def _():
        def start_out(i, _):
            block_copy_out(i).start()
            return 0

        lax.fori_loop(0, nb, start_out, 0)

        @pl.when(g == pl.num_programs(0) - 1)
        def _():
            wait_out(nb)


def _experts(h2, slot_tok, grp_e, grp_blk0, grp_nb, n_used, w_gate, w_up, w_down, n_slots):
    d = w_gate.shape[1]
    ff = w_gate.shape[2]
    n_f = ff // MOE_FC
    n_groups = grp_e.shape[0]
    rmax = MOE_GROUP_BLOCKS * MOE_BM

    def f_idx(g, f, gnb):
        return jnp.where(gnb[g] > 0, f, n_f - 1)

    grid_spec = pltpu.PrefetchScalarGridSpec(
        num_scalar_prefetch=5,
        grid=(n_groups, n_f),
        in_specs=[
            pl.BlockSpec(memory_space=pl.ANY),
            pl.BlockSpec((None, d, MOE_FC), lambda g, f, ge, gb0, gnb, *_: (ge[g], 0, f_idx(g, f, gnb))),
            pl.BlockSpec((None, d, MOE_FC), lambda g, f, ge, gb0, gnb, *_: (ge[g], 0, f_idx(g, f, gnb))),
            pl.BlockSpec((None, MOE_FC, d), lambda g, f, ge, gb0, gnb, *_: (ge[g], f_idx(g, f, gnb), 0)),
        ],
        out_specs=pl.BlockSpec(memory_space=pl.ANY),
        scratch_shapes=[pltpu.VMEM((2, rmax, d // 2), jnp.int32),
                        pltpu.VMEM((rmax, d), BF16),
                        pltpu.VMEM((rmax, d), F32),
                        pltpu.VMEM((d, MOE_FC), BF16),
                        pltpu.VMEM((d, MOE_FC), BF16),
                        pltpu.VMEM((MOE_FC, d), BF16),
                        pltpu.SemaphoreType.DMA((2,)),
                        pltpu.SemaphoreType.DMA(())],
    )
    return pl.pallas_call(
        functools.partial(_expert_body, bm=MOE_BM, n_f=n_f, n_blocks=n_slots // MOE_BM),
        grid_spec=grid_spec,
        out_shape=jax.ShapeDtypeStruct((n_slots, d), F32),
        compiler_params=_cparams("arbitrary", "arbitrary"),
        name="moe_experts",
    )(grp_e, grp_blk0, grp_nb, n_used, slot_tok, h2, w_gate, w_up, w_down)


def _combine_body(pos_ref, x_ref, gate_ref, ys_hbm, o_ref, buf_ref, sem, *, tm):
    i = pl.program_id(0)
    n_steps = pl.num_programs(0)

    def issue(step):
        slot = step % 2

        def body(r0, _):
            for u in range(DMA_ISSUE_UNROLL):
                r = r0 * DMA_ISSUE_UNROLL + u
                for k in range(MOE_TOP_K):
                    src = pos_ref[(step * tm + r) * MOE_TOP_K + k]
                    pltpu.make_async_copy(ys_hbm.at[pl.ds(src, 1)], buf_ref.at[slot, k, pl.ds(r, 1)],
                                          sem.at[slot]).start()
            return 0

        lax.fori_loop(0, tm // DMA_ISSUE_UNROLL, body, 0)

    @pl.when(i == 0)
    def _():
        issue(i)

    @pl.when(i + 1 < n_steps)
    def _():
        issue(i + 1)

    slot = i % 2
    for k in range(MOE_TOP_K):
        pltpu.make_async_copy(ys_hbm.at[pl.ds(0, tm)], buf_ref.at[slot, k], sem.at[slot]).wait()
    o_ref[...] = x_ref[...] + (buf_ref[slot, 0] * gate_ref[:, 0:1] + buf_ref[slot, 1] * gate_ref[:, 1:2])


def _combine(x, gate, ys, pos, tm):
    m, d = x.shape
    grid_spec = pltpu.PrefetchScalarGridSpec(
        num_scalar_prefetch=1,
        grid=(m // tm,),
        in_specs=[pl.BlockSpec((tm, d), lambda i, pos: (i, 0)),
                  pl.BlockSpec((tm, LANES), lambda i, pos: (i, 0)),
                  pl.BlockSpec(memory_space=pl.ANY)],
        out_specs=pl.BlockSpec((tm, d), lambda i, pos: (i, 0)),
        scratch_shapes=[pltpu.VMEM((2, MOE_TOP_K, tm, d), F32), pltpu.SemaphoreType.DMA((2,))],
    )
    return pl.pallas_call(
        functools.partial(_combine_body, tm=tm),
        grid_spec=grid_spec,
        out_shape=jax.ShapeDtypeStruct((m, d), F32),
        compiler_params=_cparams("arbitrary"),
        name="moe_combine",
    )(pos, x, gate, ys)


def _moe_plan(eid, n_tokens):
    a = n_tokens * MOE_TOP_K
    eid = eid.reshape(a)
    n_blocks = (a + N_EXPERTS * (MOE_BM - 1)) // MOE_BM
    n_slots = n_blocks * MOE_BM
    n_groups = n_blocks // MOE_GROUP_BLOCKS + N_EXPERTS
    onehot = (eid[:, None] == jnp.arange(N_EXPERTS, dtype=jnp.int32)[None, :]).astype(jnp.int32)
    rank = jnp.take_along_axis(jnp.cumsum(onehot, axis=0), eid[:, None], axis=1)[:, 0] - 1
    counts = jnp.sum(onehot, axis=0)
    blocks_e = (counts + MOE_BM - 1) // MOE_BM
    blk_start = jnp.cumsum(blocks_e) - blocks_e
    pos = (blk_start[eid] * MOE_BM + rank).astype(jnp.int32)
    tok = jnp.arange(a, dtype=jnp.int32) // MOE_TOP_K
    slot_tok = jnp.zeros((n_slots,), jnp.int32).at[pos].set(tok, unique_indices=True)
    groups_e = (blocks_e + MOE_GROUP_BLOCKS - 1) // MOE_GROUP_BLOCKS
    grp_end = jnp.cumsum(groups_e)
    grp_start = grp_end - groups_e
    gidx = jnp.arange(n_groups, dtype=jnp.int32)
    ge = jnp.minimum(jnp.searchsorted(grp_end, gidx, side="right"), N_EXPERTS - 1).astype(jnp.int32)
    local = gidx - grp_start[ge]
    active = gidx < grp_end[-1]
    gb0 = jnp.where(active, blk_start[ge] + local * MOE_GROUP_BLOCKS, 0).astype(jnp.int32)
    gnb = jnp.where(active, jnp.minimum(blocks_e[ge] - local * MOE_GROUP_BLOCKS, MOE_GROUP_BLOCKS),
                    0).astype(jnp.int32)
    last_e = ge[jnp.maximum(grp_end[-1] - 1, 0)]
    ge = jnp.where(active, ge, last_e).astype(jnp.int32)
    n_used = jnp.sum(blocks_e).astype(jnp.int32).reshape(1)
    return pos, slot_tok, ge, gb0, gnb, n_used, n_slots


def _layer(x, mem, positions, attn_norm_g, mem_norm_g, w_in, fox_forget_b, fox_q_norm_g,
           fox_k_norm_g, dsa_q_norm_g, dsa_k_norm_g, idx_k_norm_g, cross_q_norm_g, cross_k_norm_g,
           w_mem_kv, w_branch_fox, w_branch_dsa, w_branch_cross, w_out, ffn_norm_g,
           router_group_w, router_group_b, router_expert_w, router_expert_b,
           expert_w_gate, expert_w_up, expert_w_down):
    b, s, d = x.shape
    n = b * s
    n_mem = mem.shape[1]
    fw, dw, cw = FOX_HEADS * HEAD_DIM, DSA_HEADS * HEAD_DIM, CROSS_HEADS * HEAD_DIM
    iqw = IDX_HEADS * IDX_DIM
    tm = min(512, n)
    x2d = x.reshape(n, d)

    sizes = (fw, fw, fw, FOX_HEADS, dw, dw, dw, iqw, IDX_HEADS, IDX_DIM, cw, 3 * d)
    offs = [0]
    for sz in sizes:
        offs.append(offs[-1] + sz)
    (c_fq, c_fk, c_fv, c_fl, c_dq, c_dk, c_dv, c_iq, c_iw, c_ik, c_cq, c_g) = range(12)

    w_t = jnp.transpose(w_in, (2, 0, 1)).reshape(offs[-1], d)
    order = (c_dq, c_dk, c_fv, c_dv, c_fq, c_fk, c_cq, c_iq, c_g)
    row0, r = {}, 0
    for c in order:
        row0[c] = r
        r += sizes[c]
    w_all = jnp.concatenate([w_t[offs[c]:offs[c + 1]] for c in order], axis=0).astype(BF16)
    small_rows = [w_t[offs[c]:offs[c + 1]] for c in (c_ik, c_iw, c_fl)]
    small_rows.append(jnp.zeros((LANES - IDX_DIM - IDX_HEADS - FOX_HEADS, d), F32))
    w_small = jnp.concatenate(small_rows, axis=0).astype(BF16)
    q_unit = (HEAD_DIM ** -0.5) * LOG2E
    g_qk1 = jnp.concatenate([jnp.tile(fox_q_norm_g * q_unit, FOX_HEADS),
                             jnp.tile(fox_k_norm_g, FOX_HEADS),
                             jnp.tile(cross_q_norm_g * q_unit, CROSS_HEADS)])
    g_qk2 = jnp.concatenate([jnp.tile(dsa_q_norm_g * q_unit, DSA_HEADS),
                             jnp.tile(dsa_k_norm_g, DSA_HEADS)])
    g_ik = jnp.concatenate([idx_k_norm_g, jnp.zeros((LANES - IDX_DIM,), F32)])
    fb_lane = jnp.zeros((LANES,), F32).at[SMALL_F:SMALL_F + FOX_HEADS].set(fox_forget_b)

    pos_col = positions.reshape(n, 1).astype(F32)
    cd, sd, ci, si = _rope_tables(pos_col, tm)

    h = _rmsnorm(x2d, attn_norm_g, BF16, tm)
    tmp = min(PROJ_TM, n)
    qk1 = _proj(h, w_all, "norm", BF16, tmp, 1024, gains=g_qk1, w_rows=(row0[c_fq], 2 * fw + cw))
    qk2 = _proj(h, w_all, "norm_rope", BF16, tm, 768, gains=g_qk2, tables=(cd, sd),
                w_rows=(row0[c_dq], 2 * dw))
    iq = _proj(h, w_all, "rope_idx", BF16, tmp, 1024, tables=(ci, si), w_rows=(row0[c_iq], iqw))
    small = _proj(h, w_small, "raw", F32, tm, LANES, w_rows=(0, LANES))
    ika, ikb, wl, lf_col = _small(small, g_ik, fb_lane, ci, si, tm)

    tq, tk = min(ATT_TQ, s), min(ATT_TK, s)
    assert tq == tk and s % tq == 0
    cum_col = _forget_cumsum(lf_col.reshape(b, s, LANES))
    qk1 = qk1.reshape(b, s, 2 * fw + cw)
    qk2 = qk2.reshape(b, s, 2 * dw)
    q_aug, k_aug = _fox_pack(qk1, cum_col, FOX_HEADS, min(512, s))
    vt = _proj_t(h, w_all, (row0[c_fv], fw + dw), tmp, 768, tk).reshape(b, s // tk, fw + dw, tk)
    o_fox = _attention(q_aug, 0, k_aug, 0, vt, 0, (), "fox", FOX_HEADS, AUG_DIM, tq, tk)

    topk = min(INDEX_TOPK, s // 4)
    wt = jnp.transpose(wl.reshape(b, s, LANES)[:, :, :IDX_HEADS], (0, 2, 1))
    bias = _select_bias(iq.reshape(b, s, iqw), ika.reshape(b, s, LANES), ikb.reshape(b, s, LANES),
                        wt, topk, tq, tk)
    o_dsa = _attention(qk2, 0, qk2, 1, vt, 1, (bias,), "dsa", DSA_HEADS, HEAD_DIM, tq, tk)

    tmm = min(512, b * n_mem)
    m_n = _rmsnorm(mem.reshape(b * n_mem, d), mem_norm_g, BF16, tmm)
    w_kv = w_mem_kv.astype(BF16)
    mk = _proj(m_n, w_kv[:, :cw], "norm", BF16, tmm, cw, gains=jnp.tile(cross_k_norm_g, CROSS_HEADS))
    mv = _proj(m_n, w_kv[:, cw:], "raw", BF16, tmm, cw)
    o_cross = _cross_attention(qk1, (2 * fw) // cw, mk.reshape(b, n_mem, cw), mv.reshape(b, n_mem, cw),
                               min(512, s))

    x_mid = _merge(h, o_fox.reshape(n, fw), o_dsa.reshape(n, dw), o_cross.reshape(n, cw), x2d,
                   w_all, row0[c_g], w_branch_fox.astype(BF16), w_branch_dsa.astype(BF16),
                   w_branch_cross.astype(BF16), w_out.astype(BF16), tm, MERGE_TN)

    w_router = jnp.concatenate(
        [router_group_w, router_expert_w, jnp.zeros((d, LANES - N_GROUPS - N_EXPERTS), F32)],
        axis=1).astype(BF16)
    b_router = jnp.concatenate(
        [router_group_b, router_expert_b, jnp.zeros((LANES - N_GROUPS - N_EXPERTS,), F32)])
    h2, eid, gate = _router(x_mid, ffn_norm_g, w_router, b_router, tm)
    pos, slot_tok, ge, gb0, gnb, n_used, n_slots = _moe_plan(eid[:, :MOE_TOP_K], n)
    ys = _experts(h2, slot_tok, ge, gb0, gnb, n_used, expert_w_gate, expert_w_up, expert_w_down,
                  n_slots)
    out = _combine(x_mid, gate, ys, pos, min(256, n))
    return out.reshape(b, s, d)


def kernel(x, mem, positions, attn_norm_g, mem_norm_g, w_in, fox_forget_b, fox_q_norm_g, fox_k_norm_g, dsa_q_norm_g, dsa_k_norm_g, idx_k_norm_g, cross_q_norm_g, cross_k_norm_g, w_mem_kv, w_branch_fox, w_branch_dsa, w_branch_cross, w_out, ffn_norm_g, router_group_w, router_group_b, router_expert_w, router_expert_b, expert_w_gate, expert_w_up, expert_w_down):
    depth = w_in.shape[0]
    for layer in range(depth):
        x = _layer(
            x, mem, positions, attn_norm_g[layer], mem_norm_g[layer], w_in[layer:layer + 1],
            fox_forget_b[layer], fox_q_norm_g[layer], fox_k_norm_g[layer],
            dsa_q_norm_g[layer], dsa_k_norm_g[layer], idx_k_norm_g[layer],
            cross_q_norm_g[layer], cross_k_norm_g[layer], w_mem_kv[layer],
            w_branch_fox[layer], w_branch_dsa[layer], w_branch_cross[layer], w_out[layer],
            ffn_norm_g[layer], router_group_w[layer], router_group_b[layer],
            router_expert_w[layer], router_expert_b[layer], expert_w_gate[layer],
            expert_w_up[layer], expert_w_down[layer])
    return x
```

```python
import functools

import jax
import jax.numpy as jnp
from jax import lax
from jax.experimental import pallas as pl
from jax.experimental.pallas import tpu as pltpu

F32 = jnp.float32
BF16 = jnp.bfloat16

LANES = 128
HEAD_DIM = 128
FOX_HEADS = 6
DSA_HEADS = 6
CROSS_HEADS = 4
IDX_HEADS = 16
IDX_DIM = 64
ROPE_THETA = 500000.0
ROT_DIM = HEAD_DIM // 4
IDX_ROT_DIM = IDX_DIM // 4
INDEX_TOPK = 256
N_GROUPS = 4
EXPERTS_PER_GROUP = 8
N_EXPERTS = N_GROUPS * EXPERTS_PER_GROUP
MOE_TOP_K = 2
EPS = 1e-6

NEG = -1e30
INT_MIN = -(2 ** 31)
VMEM_LIMIT = 56 * 1024 * 1024

ATT_TQ = 512
ATT_TK = 512
ATT_HEADS_PER_LOOP = 6
PROJ_TM = 1024
MERGE_TN = 512
MERGE_SUB = 256
MOE_BM = 128
MOE_GROUP_BLOCKS = 8
MOE_MM_ROWS = 256
MOE_FC = 512
DMA_ISSUE_UNROLL = 8


def _cparams(*sem):
    return pltpu.CompilerParams(dimension_semantics=sem, vmem_limit_bytes=VMEM_LIMIT)


def _rmsnorm_body(x_ref, g_ref, o_ref):
    x = x_ref[...]
    r = lax.rsqrt(jnp.mean(x * x, axis=-1, keepdims=True) + EPS)
    o_ref[...] = ((x * r) * g_ref[...]).astype(o_ref.dtype)


def _rmsnorm(x2d, g, out_dtype, tm):
    m, d = x2d.shape
    return pl.pallas_call(
        _rmsnorm_body,
        grid=(m // tm,),
        in_specs=[pl.BlockSpec((tm, d), lambda i: (i, 0)),
                  pl.BlockSpec((1, d), lambda i: (0, 0))],
        out_specs=pl.BlockSpec((tm, d), lambda i: (i, 0)),
        out_shape=jax.ShapeDtypeStruct((m, d), out_dtype),
        compiler_params=_cparams("parallel"),
        name="rmsnorm",
    )(x2d, g.reshape(1, d))


W_PREP_ROWS = 128


def _w_prep_body(src_ref, w_ref, o_ref):
    del src_ref
    o_ref[...] = w_ref[:, 0, :].astype(o_ref.dtype)


def _w_prep(w_t, src_rows):
    _, _, k = w_t.shape
    n_chunks = src_rows.shape[0]
    grid_spec = pltpu.PrefetchScalarGridSpec(
        num_scalar_prefetch=1,
        grid=(n_chunks,),
        in_specs=[pl.BlockSpec((pl.Element(W_PREP_ROWS), pl.Element(1), pl.Element(k)),
                               lambda c, src: (src[c], 0, 0))],
        out_specs=pl.BlockSpec((W_PREP_ROWS, k), lambda c, src: (c, 0)),
    )
    return pl.pallas_call(
        _w_prep_body,
        grid_spec=grid_spec,
        out_shape=jax.ShapeDtypeStruct((n_chunks * W_PREP_ROWS, k), BF16),
        compiler_params=_cparams("parallel"),
        name="w_prep",
    )(src_rows, w_t)


IDX_FREQ_LANE = 32


def _rope_table_body(pos_ref, f_ref, cd_ref, sd_ref, ci_ref, si_ref):
    ang = pos_ref[...] * f_ref[...]
    c = jnp.cos(ang)
    s = jnp.sin(ang)
    lane = lax.broadcasted_iota(jnp.int32, c.shape, 1)
    hd = ROT_DIM // 2
    cd_ref[...] = jnp.where(lane < hd, c, jnp.where(lane < ROT_DIM, pltpu.roll(c, hd, 1), 1.0))
    sd_ref[...] = jnp.where(lane < hd, -s, jnp.where(lane < ROT_DIM, pltpu.roll(s, hd, 1), 0.0))
    m = lane & (IDX_DIM - 1)
    hi = IDX_ROT_DIM // 2
    ci_src = pltpu.roll(c, LANES - IDX_FREQ_LANE, 1)
    si_src = pltpu.roll(s, LANES - IDX_FREQ_LANE, 1)
    ci_ref[...] = jnp.where(m < IDX_ROT_DIM, ci_src, 1.0)
    si_ref[...] = jnp.where(m < hi, -si_src, jnp.where(m < IDX_ROT_DIM, si_src, 0.0))


def _rope_tables(pos_col, tm):
    n = pos_col.shape[0]

    def inv_freq(rot_dim):
        half = rot_dim // 2
        return jnp.power(jnp.float32(ROPE_THETA), -jnp.arange(half, dtype=F32) * 2.0 / rot_dim)

    fi = jnp.tile(inv_freq(IDX_ROT_DIM), 2)
    freq = jnp.zeros((LANES,), F32).at[:ROT_DIM // 2].set(inv_freq(ROT_DIM))
    for base in (IDX_FREQ_LANE, IDX_FREQ_LANE + IDX_DIM):
        freq = freq.at[base:base + IDX_ROT_DIM].set(fi)
    tab = pl.BlockSpec((tm, LANES), lambda i: (i, 0))
    return pl.pallas_call(
        _rope_table_body,
        grid=(n // tm,),
        in_specs=[pl.BlockSpec((tm, 1), lambda i: (i, 0)), pl.BlockSpec((1, LANES), lambda i: (0, 0))],
        out_specs=[tab] * 4,
        out_shape=[jax.ShapeDtypeStruct((n, LANES), F32)] * 4,
        compiler_params=_cparams("parallel"),
        name="rope_tables",
    )(pos_col, freq.reshape(1, LANES))


def _apply_rope(a, c, s, half, period):
    lane = lax.broadcasted_iota(jnp.int32, a.shape, 1)
    first = (lane & (period - 1)) < half
    partner = jnp.where(first, pltpu.roll(a, LANES - half, 1), pltpu.roll(a, half, 1))
    return a * c + partner * s


def _proj_body(*refs, mode, n_lane_groups, w_rows):
    h_ref, w_ref = refs[0], refs[1]
    o_ref = refs[-1]

    if w_rows:
        acc = lax.dot_general(h_ref[...], w_ref[...], (((1,), (1,)), ((), ())),
                              preferred_element_type=F32)
    else:
        acc = jnp.dot(h_ref[...], w_ref[...], preferred_element_type=F32)
    if mode == "raw":
        o_ref[...] = acc.astype(o_ref.dtype)
        return
    for g in range(n_lane_groups):
        sl = slice(g * LANES, (g + 1) * LANES)
        a = acc[:, sl]
        if mode in ("norm", "norm_rope"):
            g_ref = refs[2]
            a = (a * lax.rsqrt(jnp.mean(a * a, axis=-1, keepdims=True) + EPS)) * g_ref[:, sl]
        if mode == "norm_rope":
            a = _apply_rope(a, refs[3][...], refs[4][...], ROT_DIM // 2, HEAD_DIM)
        if mode == "rope_idx":
            a = _apply_rope(a, refs[2][...], refs[3][...], IDX_ROT_DIM // 2, IDX_DIM)
        o_ref[:, sl] = a.astype(o_ref.dtype)


def _proj_t_body(h_ref, w_ref, o_ref, *, tk):
    acc = lax.dot_general(w_ref[...], h_ref[...], (((1,), (1,)), ((), ())),
                          preferred_element_type=F32)
    for q in range(o_ref.shape[0]):
        o_ref[q] = acc[:, q * tk:(q + 1) * tk].astype(o_ref.dtype)


def _proj_t(h, w, w_rows, tm, tn, tk):
    m, k = h.shape
    row0, n = w_rows
    assert row0 % tn == 0 and n % tn == 0 and tm % tk == 0
    return pl.pallas_call(
        functools.partial(_proj_t_body, tk=tk),
        grid=(n // tn, m // tm),
        in_specs=[pl.BlockSpec((tm, k), lambda j, i: (i, 0)),
                  pl.BlockSpec((tn, k), lambda j, i: (row0 // tn + j, 0))],
        out_specs=pl.BlockSpec((tm // tk, tn, tk), lambda j, i: (i, j, 0)),
        out_shape=jax.ShapeDtypeStruct((m // tk, n, tk), BF16),
        compiler_params=_cparams("parallel", "parallel"),
        name="proj_t",
    )(h, w)


def _proj(h, w, mode, out_dtype, tm, tn, gains=None, tables=None, w_rows=None):
    m, k = h.shape
    if w_rows is None:
        n = w.shape[1]
        w_spec = pl.BlockSpec((k, tn), lambda j, i: (0, j))
    else:
        row0, n = w_rows
        assert row0 % tn == 0 and n % tn == 0
        w_spec = pl.BlockSpec((tn, k), lambda j, i: (row0 // tn + j, 0))
    in_specs = [pl.BlockSpec((tm, k), lambda j, i: (i, 0)), w_spec]
    args = [h, w]
    if gains is not None:
        in_specs.append(pl.BlockSpec((1, tn), lambda j, i: (0, j)))
        args.append(gains.reshape(1, n))
    if tables is not None:
        for t in tables:
            in_specs.append(pl.BlockSpec((tm, LANES), lambda j, i: (i, 0)))
            args.append(t)
    return pl.pallas_call(
        functools.partial(_proj_body, mode=mode, n_lane_groups=tn // LANES,
                          w_rows=w_rows is not None),
        grid=(n // tn, m // tm),
        in_specs=in_specs,
        out_specs=pl.BlockSpec((tm, tn), lambda j, i: (i, j)),
        out_shape=jax.ShapeDtypeStruct((m, n), out_dtype),
        compiler_params=_cparams("parallel", "parallel"),
        name="proj_" + mode,
    )(*args)


SMALL_IK = 0
SMALL_IW = 64
SMALL_F = 80


def _small_body(x_ref, gk_ref, fb_ref, c_ref, s_ref, ika_ref, ikb_ref, wl_ref, lf_ref):
    x = x_ref[...]
    lane = lax.broadcasted_iota(jnp.int32, x.shape, 1)
    is_ik = lane < IDX_DIM
    ik = jnp.where(is_ik, x, 0.0)
    ms = jnp.sum(ik * ik, axis=-1, keepdims=True) * (1.0 / IDX_DIM)
    ik = (ik * lax.rsqrt(ms + EPS)) * gk_ref[...]
    ik = _apply_rope(ik, c_ref[...], s_ref[...], IDX_ROT_DIM // 2, IDX_DIM)
    ik = jnp.where(is_ik, ik, 0.0)
    ika_ref[...] = ik.astype(BF16)
    ikb_ref[...] = pltpu.roll(ik, IDX_DIM, 1).astype(BF16)
    wl_ref[...] = pltpu.roll(x, LANES - SMALL_IW, 1) * (2.0 ** -5)
    z = x + fb_ref[...]
    log_f = jnp.minimum(z, 0.0) - jnp.log1p(jnp.exp(-jnp.abs(z)))
    lf_ref[...] = pltpu.roll(log_f, LANES - SMALL_F, 1)


def _small(x, idx_k_gain_lane, forget_b_lane, ci, si, tm):
    n = x.shape[0]
    row = pl.BlockSpec((1, LANES), lambda i: (0, 0))
    tab = pl.BlockSpec((tm, LANES), lambda i: (i, 0))
    return pl.pallas_call(
        _small_body,
        grid=(n // tm,),
        in_specs=[tab, row, row, tab, tab],
        out_specs=[tab, tab, tab, tab],
        out_shape=[jax.ShapeDtypeStruct((n, LANES), BF16), jax.ShapeDtypeStruct((n, LANES), BF16),
                   jax.ShapeDtypeStruct((n, LANES), F32), jax.ShapeDtypeStruct((n, LANES), F32)],
        compiler_params=_cparams("parallel"),
        name="small_cols",
    )(x, idx_k_gain_lane.reshape(1, LANES), forget_b_lane.reshape(1, LANES), ci, si)


def _cumsum_body(col_ref, ccol_ref, *, seq):
    c = col_ref[...]
    ci = lax.broadcasted_iota(jnp.int32, c.shape, 0)
    sh = 1
    while sh < seq:
        c = c + jnp.where(ci >= sh, pltpu.roll(c, sh, 0), 0.0)
        sh *= 2
    ccol_ref[...] = c


def _forget_cumsum(lf_col):
    b, s, _ = lf_col.shape
    return pl.pallas_call(
        functools.partial(_cumsum_body, seq=s),
        grid=(b,),
        in_specs=[pl.BlockSpec((None, s, LANES), lambda i: (i, 0, 0))],
        out_specs=pl.BlockSpec((None, s, LANES), lambda i: (i, 0, 0)),
        out_shape=jax.ShapeDtypeStruct((b, s, LANES), F32),
        compiler_params=_cparams("parallel"),
        name="forget_cumsum",
    )(lf_col)


AUG_DIM = 2 * HEAD_DIM
LOG2E = 1.4426950408889634


def _split3(x):
    hi = x.astype(BF16).astype(F32)
    r1 = x - hi
    mid = r1.astype(BF16).astype(F32)
    lo = (r1 - mid).astype(BF16).astype(F32)
    return hi, mid, lo


def _fox_pack_body(q_ref, k_ref, cum_ref, qa_ref, ka_ref, *, n_heads):
    rows = q_ref.shape[0]
    lane = lax.broadcasted_iota(jnp.int32, (rows, LANES), 1)
    for h in range(n_heads):
        c = jnp.broadcast_to(cum_ref[:, h:h + 1], (rows, LANES)) * LOG2E
        qh, qm, ql = _split3(c)
        kh, km, kl = _split3(-c)
        q_aug = jnp.where(lane == 0, qh, jnp.where(lane == 1, qm, jnp.where(
            lane == 2, ql, jnp.where(lane < 6, 1.0, 0.0))))
        k_aug = jnp.where(lane < 3, 1.0, jnp.where(lane == 3, kh, jnp.where(
            lane == 4, km, jnp.where(lane == 5, kl, 0.0))))
        base = h * AUG_DIM
        qa_ref[:, base:base + HEAD_DIM] = q_ref[:, h * HEAD_DIM:(h + 1) * HEAD_DIM]
        ka_ref[:, base:base + HEAD_DIM] = k_ref[:, h * HEAD_DIM:(h + 1) * HEAD_DIM]
        qa_ref[:, base + HEAD_DIM:base + AUG_DIM] = q_aug.astype(BF16)
        ka_ref[:, base + HEAD_DIM:base + AUG_DIM] = k_aug.astype(BF16)


def _fox_pack(qk, cum_col, n_heads, ts):
    b, s, _ = qk.shape
    w = n_heads * HEAD_DIM
    wa = n_heads * AUG_DIM
    return pl.pallas_call(
        functools.partial(_fox_pack_body, n_heads=n_heads),
        grid=(b, s // ts),
        in_specs=[pl.BlockSpec((None, ts, w), lambda bi, i: (bi, i, 0)),
                  pl.BlockSpec((None, ts, w), lambda bi, i: (bi, i, 1)),
                  pl.BlockSpec((None, ts, LANES), lambda bi, i: (bi, i, 0))],
        out_specs=[pl.BlockSpec((None, ts, wa), lambda bi, i: (bi, i, 0)),
                   pl.BlockSpec((None, ts, wa), lambda bi, i: (bi, i, 0))],
        out_shape=[jax.ShapeDtypeStruct((b, s, wa), BF16)] * 2,
        compiler_params=_cparams("parallel", "parallel"),
        name="fox_pack",
    )(qk, qk, cum_col)


def _attn_body(q_ref, k_ref, vt_ref, *rest, tq, tk, n_heads, kdim, mode, heads_per_loop):
    o_ref = rest[-1]
    qi = pl.program_id(1)
    key0 = lax.broadcasted_iota(jnp.int32, (tk, tq), 0)
    qry = qi * tq + lax.broadcasted_iota(jnp.int32, (tk, tq), 1)
    def scores(h, j):
        off = pl.multiple_of(j * tk, tk)
        q = q_ref[:, h * kdim:(h + 1) * kdim]
        k = k_ref[pl.ds(off, tk), h * kdim:(h + 1) * kdim]
        return lax.dot_general(k, q, (((1,), (1,)), ((), ())), preferred_element_type=F32)

    def flush(h, j, p, alpha, acc):
        vt = vt_ref[j, h * HEAD_DIM:(h + 1) * HEAD_DIM, :]
        return alpha * acc + jnp.dot(vt, p, preferred_element_type=F32)

    def softmax_step(h, j, m, l, masked):
        s = scores(h, j)
        if mode == "dsa":
            s = s + rest[0][j]
        if masked:
            s = jnp.where(key0 + j * tk <= qry, s, NEG)
        m_new = jnp.maximum(m, jnp.max(s, axis=0, keepdims=True))
        alpha = jnp.exp2(m - m_new)
        p = jnp.exp2(s - m_new)
        l = alpha * l + jnp.sum(p, axis=0, keepdims=True)
        return p.astype(BF16), alpha, m_new, l

    last = qi
    for h0 in range(0, n_heads, heads_per_loop):
        heads = range(h0, h0 + heads_per_loop)

        def step(j, carries, masked=False, heads=heads):
            prev = jnp.maximum(j - 1, 0)
            out = []
            for h, (p, alpha, m, l, acc) in zip(heads, carries):
                acc = flush(h, prev, p, alpha, acc)
                p, alpha, m, l = softmax_step(h, j, m, l, masked)
                out.append((p, alpha, m, l, acc))
            return tuple(out)

        carries = tuple(
            (jnp.zeros((tk, tq), BF16), jnp.ones((1, tq), F32), jnp.full((1, tq), NEG, F32),
             jnp.zeros((1, tq), F32), jnp.zeros((HEAD_DIM, tq), F32)) for h in heads)
        n_loop = last if mode == "fox" else last + 1
        carries = lax.fori_loop(0, n_loop, step, carries)
        if mode == "fox":
            carries = step(last, carries, masked=True)
        for h, (p, alpha, m, l, acc) in zip(heads, carries):
            acc = flush(h, last, p, alpha, acc)
            o_ref[:, h * HEAD_DIM:(h + 1) * HEAD_DIM] = jnp.transpose(acc / l).astype(o_ref.dtype)


def _attention(q_arr, q_blk, k_arr, k_blk, vt_arr, v_blk, extras, mode, n_heads, kdim, tq, tk):
    b, s, _ = q_arr.shape
    w = n_heads * HEAD_DIM
    wk = n_heads * kdim
    in_specs = [pl.BlockSpec((None, tq, wk), lambda bi, qi: (bi, qi, q_blk)),
                pl.BlockSpec((None, s, wk), lambda bi, qi: (bi, 0, k_blk)),
                pl.BlockSpec((None, s // tk, w, tk), lambda bi, qi: (bi, 0, v_blk, 0))]
    if mode == "dsa":
        in_specs += [pl.BlockSpec((None, None, s // tk, tk, tq), lambda bi, qi: (bi, qi, 0, 0, 0))]
    return pl.pallas_call(
        functools.partial(_attn_body, tq=tq, tk=tk, n_heads=n_heads, kdim=kdim, mode=mode,
                          heads_per_loop=ATT_HEADS_PER_LOOP),
        grid=(b, s // tq),
        in_specs=in_specs,
        out_specs=pl.BlockSpec((None, tq, w), lambda bi, qi: (bi, qi, 0)),
        out_shape=jax.ShapeDtypeStruct((b, s, w), BF16),
        compiler_params=_cparams("parallel", "parallel"),
        name="attn_" + mode,
    )(q_arr, k_arr, vt_arr, *extras)


def _select_body(iq_ref, ika_ref, ikb_ref, wt_ref, o_ref, keys_ref, half_ref, *, tq, tk, n_chunks, topk):
    qi = pl.program_id(1)
    used = qi + 1
    key0 = lax.broadcasted_iota(jnp.int32, (tk, tq), 0)
    qry = qi * tq + lax.broadcasted_iota(jnp.int32, (tk, tq), 1)

    def score_chunk(c, _):
        off = pl.multiple_of(c * tk, tk)
        ka = ika_ref[pl.ds(off, tk), :]
        kb = ikb_ref[pl.ds(off, tk), :]
        acc = jnp.zeros((tk, tq), F32)
        for j in range(IDX_HEADS):
            pair = iq_ref[:, (j // 2) * LANES:(j // 2 + 1) * LANES]
            logit = lax.dot_general(ka if j % 2 == 0 else kb, pair, (((1,), (1,)), ((), ())),
                                    preferred_element_type=F32)
            acc = acc + wt_ref[j:j + 1, :] * jnp.maximum(logit, 0.0)
        bits = lax.bitcast_convert_type(acc, jnp.int32)
        key = jnp.where(bits < 0, bits ^ jnp.int32(0x7FFFFFFF), bits)
        key = jnp.where(key0 + off <= qry, key, jnp.int32(INT_MIN))
        keys_ref[c] = key
        half_ref[c] = lax.shift_right_arithmetic(key, 16).astype(jnp.int16)
        return 0

    lax.fori_loop(0, used, score_chunk, 0)

    one16 = jnp.int16(1)
    zero16 = jnp.int16(0)

    def count_ge(cand16):
        def count_chunk(c, cnt):
            partial = [None] * 4
            for i, r in enumerate(range(0, tk, 16)):
                ge = jnp.where(half_ref[c, r:r + 16, :] >= cand16, one16, zero16)
                partial[i % 4] = ge if partial[i % 4] is None else partial[i % 4] + ge
            return cnt + ((partial[0] + partial[1]) + (partial[2] + partial[3]))

        cnt = lax.fori_loop(0, used, count_chunk, jnp.zeros((16, tq), jnp.int16))
        return jnp.sum(cnt.astype(jnp.int32), axis=0, keepdims=True)

    def search16(need):
        def bit_step(i, carry):
            prefix, kept = carry
            cand_u = prefix | jnp.left_shift(jnp.int32(1), 15 - i)
            cnt = count_ge((cand_u - 32768).astype(jnp.int16))
            ok = cnt >= need
            return jnp.where(ok, cand_u, prefix), jnp.where(ok, cnt, kept)

        init = (jnp.zeros((1, tq), jnp.int32), jnp.full((1, tq), tk, jnp.int32) * used)
        prefix, kept = lax.fori_loop(0, 16, bit_step, init)
        return prefix - 32768, kept

    hi, _ = search16(topk)
    above = count_ge(jnp.minimum(hi + 1, 32767).astype(jnp.int16))
    above = jnp.where(hi >= 32767, 0, above)
    hi16 = hi.astype(jnp.int16)

    def low_chunk(c, _):
        low = (keys_ref[c] & jnp.int32(0xFFFF)) - 32768
        half_ref[c] = jnp.where(half_ref[c] == hi16, low.astype(jnp.int16), jnp.int16(-32768))
        return 0

    lax.fori_loop(0, used, low_chunk, 0)
    lo, at_or_above_lo = search16(topk - above)
    thr = lax.shift_left(hi, 16) | (lo + 32768)
    thr = jnp.maximum(thr, jnp.int32(INT_MIN + 1))
    tied = jnp.max(above + at_or_above_lo) > topk

    def count_keys(pred):
        def chunk(c, cnt):
            hit = jnp.where(pred(keys_ref[c]), 1.0, 0.0)
            return cnt + jnp.sum(hit.reshape(tk // 8, 8, tq), axis=0)

        cnt = lax.fori_loop(0, used, chunk, jnp.zeros((8, tq), F32))
        return jnp.sum(cnt, axis=0, keepdims=True)

    @pl.when(jnp.logical_not(tied))
    def _():
        def write_chunk(c, _):
            o_ref[c] = jnp.where(keys_ref[c] >= thr, 0.0, NEG)
            return 0

        lax.fori_loop(0, used, write_chunk, 0)

    @pl.when(tied)
    def _():
        need = topk - count_keys(lambda k: k > thr)
        earlier = jnp.where(lax.broadcasted_iota(jnp.int32, (tk, tk), 1)
                            < lax.broadcasted_iota(jnp.int32, (tk, tk), 0), 1.0, 0.0).astype(BF16)

        def write_chunk(c, seen):
            keys = keys_ref[c]
            tie = jnp.where(keys == thr, 1.0, 0.0)
            rank = jnp.dot(earlier, tie.astype(BF16), preferred_element_type=F32) + seen
            o_ref[c] = jnp.where(keys > thr, 0.0,
                                 jnp.where(keys == thr, jnp.where(rank < need, 0.0, NEG), NEG))
            return seen + jnp.sum(tie, axis=0, keepdims=True)

        lax.fori_loop(0, used, write_chunk, jnp.zeros((1, tq), F32))

    def fill_chunk(c, _):
        o_ref[c] = jnp.full((tk, tq), NEG, F32)
        return 0

    lax.fori_loop(used, n_chunks, fill_chunk, 0)


def _select_bias(iq, ika, ikb, wt, topk, tq, tk):
    b, s, _ = iq.shape
    n_chunks = s // tk
    return pl.pallas_call(
        functools.partial(_select_body, tq=tq, tk=tk, n_chunks=n_chunks, topk=topk),
        grid=(b, s // tq),
        in_specs=[pl.BlockSpec((None, tq, IDX_HEADS * IDX_DIM), lambda bi, qi: (bi, qi, 0)),
                  pl.BlockSpec((None, s, LANES), lambda bi, qi: (bi, 0, 0)),
                  pl.BlockSpec((None, s, LANES), lambda bi, qi: (bi, 0, 0)),
                  pl.BlockSpec((None, IDX_HEADS, tq), lambda bi, qi: (bi, 0, qi))],
        out_specs=pl.BlockSpec((None, None, n_chunks, tk, tq), lambda bi, qi: (bi, qi, 0, 0, 0)),
        out_shape=jax.ShapeDtypeStruct((b, s // tq, n_chunks, tk, tq), F32),
        scratch_shapes=[pltpu.VMEM((n_chunks, tk, tq), jnp.int32),
                        pltpu.VMEM((n_chunks, tk, tq), jnp.int16)],
        compiler_params=_cparams("parallel", "parallel"),
        name="index_select",
    )(iq, ika, ikb, wt)


def _cross_body(q_ref, k_ref, v_ref, o_ref, *, n_heads):
    for h in range(n_heads):
        hs = slice(h * HEAD_DIM, (h + 1) * HEAD_DIM)
        s = lax.dot_general(q_ref[:, hs], k_ref[:, hs], (((1,), (1,)), ((), ())),
                            preferred_element_type=F32)
        p = jnp.exp2(s - jnp.max(s, axis=-1, keepdims=True))
        l = jnp.sum(p, axis=-1, keepdims=True)
        o = jnp.dot(p.astype(BF16), v_ref[:, hs], preferred_element_type=F32)
        o_ref[:, hs] = (o / l).astype(o_ref.dtype)


def _cross_attention(q_arr, q_blk, mk, mv, tq):
    b, s, _ = q_arr.shape
    n_mem = mk.shape[1]
    w = CROSS_HEADS * HEAD_DIM
    return pl.pallas_call(
        functools.partial(_cross_body, n_heads=CROSS_HEADS),
        grid=(b, s // tq),
        in_specs=[pl.BlockSpec((None, tq, w), lambda bi, qi: (bi, qi, q_blk)),
                  pl.BlockSpec((None, n_mem, w), lambda bi, qi: (bi, 0, 0)),
                  pl.BlockSpec((None, n_mem, w), lambda bi, qi: (bi, 0, 0))],
        out_specs=pl.BlockSpec((None, tq, w), lambda bi, qi: (bi, qi, 0)),
        out_shape=jax.ShapeDtypeStruct((b, s, w), BF16),
        compiler_params=_cparams("parallel", "parallel"),
        name="cross_attn",
    )(q_arr, mk, mv)


def _sigmoid(x):
    return 1.0 / (1.0 + jnp.exp(-x))


def _merge_body(h_ref, of_ref, od_ref, oc_ref, x_ref, wgf_ref, wgd_ref, wgc_ref,
                wbf_ref, wbd_ref, wbc_ref, wo_ref, o_ref):
    j = pl.program_id(1)
    h = h_ref[...]
    tn = wo_ref.shape[0]
    sub = min(tn, MERGE_SUB)

    def gate_logits(wg_ref, cs):
        return lax.dot_general(h, wg_ref[cs, :], (((1,), (1,)), ((), ())), preferred_element_type=F32)

    part = None
    for c in range(tn // sub):
        cs = slice(c * sub, (c + 1) * sub)
        zf, zd, zc = gate_logits(wgf_ref, cs), gate_logits(wgd_ref, cs), gate_logits(wgc_ref, cs)
        bf = jnp.dot(of_ref[...], wbf_ref[:, cs], preferred_element_type=F32)
        bd = jnp.dot(od_ref[...], wbd_ref[:, cs], preferred_element_type=F32)
        bc = jnp.dot(oc_ref[...], wbc_ref[:, cs], preferred_element_type=F32)
        merged = (_sigmoid(zf) * bf + _sigmoid(zd) * bd) + _sigmoid(zc) * bc
        p = jnp.dot(merged.astype(BF16), wo_ref[cs, :], preferred_element_type=F32)
        part = p if part is None else part + p

    @pl.when(j == 0)
    def _():
        o_ref[...] = x_ref[...] + part

    @pl.when(j != 0)
    def _():
        o_ref[...] += part


def _merge(h, o_fox, o_dsa, o_cross, x, w_gates, gate_row0, wbf, wbd, wbc, w_out, tm, tn):
    m, d = x.shape
    nj = d // tn
    assert gate_row0 % tn == 0
    g0 = gate_row0 // tn
    row = lambda i, j: (i, 0)
    in_specs = [pl.BlockSpec((tm, d), row),
                pl.BlockSpec((tm, o_fox.shape[1]), row),
                pl.BlockSpec((tm, o_dsa.shape[1]), row),
                pl.BlockSpec((tm, o_cross.shape[1]), row),
                pl.BlockSpec((tm, d), row),
                pl.BlockSpec((tn, d), lambda i, j: (g0 + j, 0)),
                pl.BlockSpec((tn, d), lambda i, j: (g0 + nj + j, 0)),
                pl.BlockSpec((tn, d), lambda i, j: (g0 + 2 * nj + j, 0)),
                pl.BlockSpec((wbf.shape[0], tn), lambda i, j: (0, j)),
                pl.BlockSpec((wbd.shape[0], tn), lambda i, j: (0, j)),
                pl.BlockSpec((wbc.shape[0], tn), lambda i, j: (0, j)),
                pl.BlockSpec((tn, d), lambda i, j: (j, 0))]
    return pl.pallas_call(
        _merge_body,
        grid=(m // tm, nj),
        in_specs=in_specs,
        out_specs=pl.BlockSpec((tm, d), row),
        out_shape=jax.ShapeDtypeStruct((m, d), F32),
        compiler_params=_cparams("parallel", "arbitrary"),
        name="gated_merge",
    )(h, o_fox, o_dsa, o_cross, x, w_gates, w_gates, w_gates, wbf, wbd, wbc, w_out)


def _router_body(x_ref, g_ref, w_ref, b_ref, h_ref, eid_ref, gate_ref):
    x = x_ref[...]
    r = lax.rsqrt(jnp.mean(x * x, axis=-1, keepdims=True) + EPS)
    h = (x * r) * g_ref[...]
    hb = h.astype(BF16)
    bits = lax.bitcast_convert_type(hb.astype(F32), jnp.int32)
    half = bits.shape[1] // 2
    h_ref[...] = lax.shift_right_logical(bits[:, :half], 16) | (bits[:, half:] & jnp.int32(-65536))
    logits = jnp.dot(hb, w_ref[...], preferred_element_type=F32) + b_ref[...]
    lane = lax.broadcasted_iota(jnp.int32, logits.shape, 1).astype(F32)

    def masked_max(mask):
        return jnp.max(jnp.where(mask, logits, -jnp.inf), axis=-1, keepdims=True)

    def first_lane(mask):
        return jnp.min(jnp.where(mask, lane, float(LANES)), axis=-1, keepdims=True)

    is_group = lane < N_GROUPS
    gmax = masked_max(is_group)
    grp = first_lane(is_group & (logits == gmax))
    denom = jnp.sum(jnp.where(is_group, jnp.exp(logits - gmax), 0.0), axis=-1, keepdims=True)
    p_sel = 1.0 / denom
    lo = N_GROUPS + grp * EXPERTS_PER_GROUP
    in_grp = (lane >= lo) & (lane < lo + EXPERTS_PER_GROUP)
    e1 = masked_max(in_grp)
    j1 = first_lane(in_grp & (logits == e1))
    rest = in_grp & (lane != j1)
    e2 = masked_max(rest)
    j2 = first_lane(rest & (logits == e2))
    t = jnp.exp(e2 - e1)
    g1 = p_sel * (1.0 / (1.0 + t))
    g2 = p_sel * (t / (1.0 + t))
    eid = jnp.where(lane == 0.0, j1 - N_GROUPS, jnp.where(lane == 1.0, j2 - N_GROUPS, 0.0))
    eid_ref[...] = eid.astype(jnp.int32)
    gate_ref[...] = jnp.where(lane == 0.0, g1, jnp.where(lane == 1.0, g2, 0.0))


def _router(x, g, w_router, b_router, tm):
    m, d = x.shape
    return pl.pallas_call(
        _router_body,
        grid=(m // tm,),
        in_specs=[pl.BlockSpec((tm, d), lambda i: (i, 0)),
                  pl.BlockSpec((1, d), lambda i: (0, 0)),
                  pl.BlockSpec((d, LANES), lambda i: (0, 0)),
                  pl.BlockSpec((1, LANES), lambda i: (0, 0))],
        out_specs=[pl.BlockSpec((tm, d // 2), lambda i: (i, 0)),
                   pl.BlockSpec((tm, LANES), lambda i: (i, 0)),
                   pl.BlockSpec((tm, LANES), lambda i: (i, 0))],
        out_shape=[jax.ShapeDtypeStruct((m, d // 2), jnp.int32),
                   jax.ShapeDtypeStruct((m, LANES), jnp.int32),
                   jax.ShapeDtypeStruct((m, LANES), F32)],
        compiler_params=_cparams("parallel"),
        name="moe_router",
    )(x, g.reshape(1, d), w_router, b_router.reshape(1, LANES))


def _expert_body(ge_ref, gb0_ref, gnb_ref, used_ref, tok_ref, h_hbm, wg_ref, wu_ref, wd_ref, ys_hbm,
                 xs_ref, xb_ref, acc_ref, wgb_ref, wub_ref, wdb_ref, sem_in, sem_out,
                 *, bm, n_f, n_blocks):
    g = pl.program_id(0)
    f = pl.program_id(1)
    nb = gnb_ref[g]
    blk0 = gb0_ref[g]

    @pl.when(jnp.logical_and(g == 0, f == 0))
    def _():
        acc_ref[pl.ds(0, bm)] = jnp.zeros((bm,) + acc_ref.shape[1:], F32)

        def tail_copy(i):
            return pltpu.make_async_copy(acc_ref.at[pl.ds(0, bm)], ys_hbm.at[pl.ds(i * bm, bm)],
                                         sem_out)

        def start_tail(i, _):
            tail_copy(i).start()
            return 0

        lax.fori_loop(used_ref[0], n_blocks, start_tail, 0)

        def wait_tail(i, _):
            tail_copy(i).wait()
            return 0

        lax.fori_loop(used_ref[0], n_blocks, wait_tail, 0)

    def gather(grp):
        slot = grp % 2
        base = gb0_ref[grp] * bm

        def issue(r0, _):
            for u in range(DMA_ISSUE_UNROLL):
                r = r0 * DMA_ISSUE_UNROLL + u
                pltpu.make_async_copy(h_hbm.at[pl.ds(tok_ref[base + r], 1)],
                                      xs_ref.at[slot, pl.ds(r, 1)], sem_in.at[slot]).start()
            return 0

        lax.fori_loop(0, gnb_ref[grp] * (bm // DMA_ISSUE_UNROLL), issue, 0)

    def block_copy_out(i):
        return pltpu.make_async_copy(acc_ref.at[pl.ds(i * bm, bm)],
                                     ys_hbm.at[pl.ds((blk0 + i) * bm, bm)], sem_out)

    @pl.when(jnp.logical_and(g == 0, f == 0))
    def _():
        gather(g)

    @pl.when(jnp.logical_and(nb > 0, f == 0))
    def _():
        slot = g % 2

        def land(i, _):
            rows = pl.ds(pl.multiple_of(i * bm, bm), bm)
            pltpu.make_async_copy(h_hbm.at[pl.ds(0, bm)], xs_ref.at[slot, rows], sem_in.at[slot]).wait()
            return 0

        lax.fori_loop(0, nb, land, 0)

        half = xb_ref.shape[1] // 2

        def unpack(i, _):
            rows = pl.ds(pl.multiple_of(i * bm, bm), bm)
            w = xs_ref[slot, rows, :]
            lo = lax.bitcast_convert_type(lax.shift_left(w, 16), F32)
            hi = lax.bitcast_convert_type(w & jnp.int32(-65536), F32)
            xb_ref[rows, :half] = lo.astype(BF16)
            xb_ref[rows, half:] = hi.astype(BF16)
            return 0

        lax.fori_loop(0, nb, unpack, 0)

    @pl.when(jnp.logical_and(f == 0, g + 1 < pl.num_programs(0)))
    def _():
        gather(g + 1)

    def wait_out(n):
        def body(i, _):
            pltpu.make_async_copy(acc_ref.at[pl.ds(0, bm)], ys_hbm.at[pl.ds(0, bm)], sem_out).wait()
            return 0

        lax.fori_loop(0, n, body, 0)

    @pl.when(jnp.logical_and(f == 0, g > 0))
    def _():
        wait_out(gnb_ref[jnp.maximum(g - 1, 0)])

    @pl.when(nb > 0)
    def _():
        wgb_ref[...] = wg_ref[...].astype(BF16)
        wub_ref[...] = wu_ref[...].astype(BF16)
        wdb_ref[...] = wd_ref[...].astype(BF16)

        def swiglu(rows):
            xb = xb_ref[rows, :]
            a = jnp.dot(xb, wgb_ref[...], preferred_element_type=F32)
            u = jnp.dot(xb, wub_ref[...], preferred_element_type=F32)
            hid = (a * _sigmoid(a)) * u
            y = jnp.dot(hid.astype(BF16), wdb_ref[...], preferred_element_type=F32)

            @pl.when(f == 0)
            def _():
                acc_ref[rows, :] = y

            @pl.when(f != 0)
            def _():
                acc_ref[rows, :] += y

        per = MOE_MM_ROWS // bm

        def full_tile(i, _):
            swiglu(pl.ds(pl.multiple_of(i * MOE_MM_ROWS, MOE_MM_ROWS), MOE_MM_ROWS))
            return 0

        lax.fori_loop(0, nb // per, full_tile, 0)
        piece = per // 2
        while piece >= 1:
            done = (nb // (2 * piece)) * (2 * piece)

            @pl.when((nb // piece) % 2 == 1)
            def _(done=done, piece=piece):
                swiglu(pl.ds(pl.multiple_of(done * bm, piece * bm), piece * bm))

            piece //= 2

    @pl.when(jnp.logical_and(nb > 0, f == n_f - 1))
    def _():
        def start_out(i, _):
            block_copy_out(i).start()
            return 0

        lax.fori_loop(0, nb, start_out, 0)

        @pl.when(g == pl.num_programs(0) - 1)
        def _():
            wait_out(nb)


def _experts(h2, slot_tok, grp_e, grp_blk0, grp_nb, n_used, w_gate, w_up, w_down, n_slots):
    d = w_gate.shape[1]
    ff = w_gate.shape[2]
    n_f = ff // MOE_FC
    n_groups = grp_e.shape[0]
    rmax = MOE_GROUP_BLOCKS * MOE_BM

    def f_idx(g, f, gnb):
        return jnp.where(gnb[g] > 0, f, n_f - 1)

    grid_spec = pltpu.PrefetchScalarGridSpec(
        num_scalar_prefetch=5,
        grid=(n_groups, n_f),
        in_specs=[
            pl.BlockSpec(memory_space=pl.ANY),
            pl.BlockSpec((None, d, MOE_FC), lambda g, f, ge, gb0, gnb, *_: (ge[g], 0, f_idx(g, f, gnb))),
            pl.BlockSpec((None, d, MOE_FC), lambda g, f, ge, gb0, gnb, *_: (ge[g], 0, f_idx(g, f, gnb))),
            pl.BlockSpec((None, MOE_FC, d), lambda g, f, ge, gb0, gnb, *_: (ge[g], f_idx(g, f, gnb), 0)),
        ],
        out_specs=pl.BlockSpec(memory_space=pl.ANY),
        scratch_shapes=[pltpu.VMEM((2, rmax, d // 2), jnp.int32),
                        pltpu.VMEM((rmax, d), BF16),
                        pltpu.VMEM((rmax, d), F32),
                        pltpu.VMEM((d, MOE_FC), BF16),
                        pltpu.VMEM((d, MOE_FC), BF16),
                        pltpu.VMEM((MOE_FC, d), BF16),
                        pltpu.SemaphoreType.DMA((2,)),
                        pltpu.SemaphoreType.DMA(())],
    )
    return pl.pallas_call(
        functools.partial(_expert_body, bm=MOE_BM, n_f=n_f, n_blocks=n_slots // MOE_BM),
        grid_spec=grid_spec,
        out_shape=jax.ShapeDtypeStruct((n_slots, d), F32),
        compiler_params=_cparams("arbitrary", "arbitrary"),
        name="moe_experts",
    )(grp_e, grp_blk0, grp_nb, n_used, slot_tok, h2, w_gate, w_up, w_down)


def _combine_body(pos_ref, x_ref, gate_ref, ys_hbm, o_ref, buf_ref, sem, *, tm):
    i = pl.program_id(0)
    n_steps = pl.num_programs(0)

    def issue(step):
        slot = step % 2

        def body(r0, _):
            for u in range(DMA_ISSUE_UNROLL):
                r = r0 * DMA_ISSUE_UNROLL + u
                for k in range(MOE_TOP_K):
                    src = pos_ref[(step * tm + r) * MOE_TOP_K + k]
                    pltpu.make_async_copy(ys_hbm.at[pl.ds(src, 1)], buf_ref.at[slot, k, pl.ds(r, 1)],
                                          sem.at[slot]).start()
            return 0

        lax.fori_loop(0, tm // DMA_ISSUE_UNROLL, body, 0)

    @pl.when(i == 0)
    def _():
        issue(i)

    @pl.when(i + 1 < n_steps)
    def _():
        issue(i + 1)

    slot = i % 2
    for k in range(MOE_TOP_K):
        pltpu.make_async_copy(ys_hbm.at[pl.ds(0, tm)], buf_ref.at[slot, k], sem.at[slot]).wait()
    o_ref[...] = x_ref[...] + (buf_ref[slot, 0] * gate_ref[:, 0:1] + buf_ref[slot, 1] * gate_ref[:, 1:2])


def _combine(x, gate, ys, pos, tm):
    m, d = x.shape
    grid_spec = pltpu.PrefetchScalarGridSpec(
        num_scalar_prefetch=1,
        grid=(m // tm,),
        in_specs=[pl.BlockSpec((tm, d), lambda i, pos: (i, 0)),
                  pl.BlockSpec((tm, LANES), lambda i, pos: (i, 0)),
                  pl.BlockSpec(memory_space=pl.ANY)],
        out_specs=pl.BlockSpec((tm, d), lambda i, pos: (i, 0)),
        scratch_shapes=[pltpu.VMEM((2, MOE_TOP_K, tm, d), F32), pltpu.SemaphoreType.DMA((2,))],
    )
    return pl.pallas_call(
        functools.partial(_combine_body, tm=tm),
        grid_spec=grid_spec,
        out_shape=jax.ShapeDtypeStruct((m, d), F32),
        compiler_params=_cparams("arbitrary"),
        name="moe_combine",
    )(pos, x, gate, ys)


def _moe_plan(eid, n_tokens):
    a = n_tokens * MOE_TOP_K
    eid = eid.reshape(a)
    n_blocks = (a + N_EXPERTS * (MOE_BM - 1)) // MOE_BM
    n_slots = n_blocks * MOE_BM
    n_groups = n_blocks // MOE_GROUP_BLOCKS + N_EXPERTS
    onehot = (eid[:, None] == jnp.arange(N_EXPERTS, dtype=jnp.int32)[None, :]).astype(jnp.int32)
    rank = jnp.take_along_axis(jnp.cumsum(onehot, axis=0), eid[:, None], axis=1)[:, 0] - 1
    counts = jnp.sum(onehot, axis=0)
    blocks_e = (counts + MOE_BM - 1) // MOE_BM
    blk_start = jnp.cumsum(blocks_e) - blocks_e
    pos = (blk_start[eid] * MOE_BM + rank).astype(jnp.int32)
    tok = jnp.arange(a, dtype=jnp.int32) // MOE_TOP_K
    slot_tok = jnp.zeros((n_slots,), jnp.int32).at[pos].set(tok, unique_indices=True)
    groups_e = (blocks_e + MOE_GROUP_BLOCKS - 1) // MOE_GROUP_BLOCKS
    grp_end = jnp.cumsum(groups_e)
    grp_start = grp_end - groups_e
    gidx = jnp.arange(n_groups, dtype=jnp.int32)
    ge = jnp.minimum(jnp.searchsorted(grp_end, gidx, side="right"), N_EXPERTS - 1).astype(jnp.int32)
    local = gidx - grp_start[ge]
    active = gidx < grp_end[-1]
    gb0 = jnp.where(active, blk_start[ge] + local * MOE_GROUP_BLOCKS, 0).astype(jnp.int32)
    gnb = jnp.where(active, jnp.minimum(blocks_e[ge] - local * MOE_GROUP_BLOCKS, MOE_GROUP_BLOCKS),
                    0).astype(jnp.int32)
    last_e = ge[jnp.maximum(grp_end[-1] - 1, 0)]
    ge = jnp.where(active, ge, last_e).astype(jnp.int32)
    n_used = jnp.sum(blocks_e).astype(jnp.int32).reshape(1)
    return pos, slot_tok, ge, gb0, gnb, n_used, n_slots


def _layer(x, mem, positions, attn_norm_g, mem_norm_g, w_in, fox_forget_b, fox_q_norm_g,
           fox_k_norm_g, dsa_q_norm_g, dsa_k_norm_g, idx_k_norm_g, cross_q_norm_g, cross_k_norm_g,
           w_mem_kv, w_branch_fox, w_branch_dsa, w_branch_cross, w_out, ffn_norm_g,
           router_group_w, router_group_b, router_expert_w, router_expert_b,
           expert_w_gate, expert_w_up, expert_w_down):
    b, s, d = x.shape
    n = b * s
    n_mem = mem.shape[1]
    fw, dw, cw = FOX_HEADS * HEAD_DIM, DSA_HEADS * HEAD_DIM, CROSS_HEADS * HEAD_DIM
    iqw = IDX_HEADS * IDX_DIM
    tm = min(512, n)
    x2d = x.reshape(n, d)

    sizes = (fw, fw, fw, FOX_HEADS, dw, dw, dw, iqw, IDX_HEADS, IDX_DIM, cw, 3 * d)
    offs = [0]
    for sz in sizes:
        offs.append(offs[-1] + sz)
    (c_fq, c_fk, c_fv, c_fl, c_dq, c_dk, c_dv, c_iq, c_iw, c_ik, c_cq, c_g) = range(12)

    w_t = jnp.transpose(w_in, (2, 0, 1)).reshape(offs[-1], d)
    order = (c_dq, c_dk, c_fv, c_dv, c_fq, c_fk, c_cq, c_iq, c_g)
    row0, r = {}, 0
    for c in order:
        row0[c] = r
        r += sizes[c]
    w_all = jnp.concatenate([w_t[offs[c]:offs[c + 1]] for c in order], axis=0).astype(BF16)
    small_rows = [w_t[offs[c]:offs[c + 1]] for c in (c_ik, c_iw, c_fl)]
    small_rows.append(jnp.zeros((LANES - IDX_DIM - IDX_HEADS - FOX_HEADS, d), F32))
    w_small = jnp.concatenate(small_rows, axis=0).astype(BF16)
    q_unit = (HEAD_DIM ** -0.5) * LOG2E
    g_qk1 = jnp.concatenate([jnp.tile(fox_q_norm_g * q_unit, FOX_HEADS),
                             jnp.tile(fox_k_norm_g, FOX_HEADS),
                             jnp.tile(cross_q_norm_g * q_unit, CROSS_HEADS)])
    g_qk2 = jnp.concatenate([jnp.tile(dsa_q_norm_g * q_unit, DSA_HEADS),
                             jnp.tile(dsa_k_norm_g, DSA_HEADS)])
    g_ik = jnp.concatenate([idx_k_norm_g, jnp.zeros((LANES - IDX_DIM,), F32)])
    fb_lane = jnp.zeros((LANES,), F32).at[SMALL_F:SMALL_F + FOX_HEADS].set(fox_forget_b)

    pos_col = positions.reshape(n, 1).astype(F32)
    cd, sd, ci, si = _rope_tables(pos_col, tm)

    h = _rmsnorm(x2d, attn_norm_g, BF16, tm)
    tmp = min(PROJ_TM, n)
    qk1 = _proj(h, w_all, "norm", BF16, tmp, 1024, gains=g_qk1, w_rows=(row0[c_fq], 2 * fw + cw))
    qk2 = _proj(h, w_all, "norm_rope", BF16, tm, 768, gains=g_qk2, tables=(cd, sd),
                w_rows=(row0[c_dq], 2 * dw))
    iq = _proj(h, w_all, "rope_idx", BF16, tmp, 1024, tables=(ci, si), w_rows=(row0[c_iq], iqw))
    small = _proj(h, w_small, "raw", F32, tm, LANES, w_rows=(0, LANES))
    ika, ikb, wl, lf_col = _small(small, g_ik, fb_lane, ci, si, tm)

    tq, tk = min(ATT_TQ, s), min(ATT_TK, s)
    assert tq == tk and s % tq == 0
    cum_col = _forget_cumsum(lf_col.reshape(b, s, LANES))
    qk1 = qk1.reshape(b, s, 2 * fw + cw)
    qk2 = qk2.reshape(b, s, 2 * dw)
    q_aug, k_aug = _fox_pack(qk1, cum_col, FOX_HEADS, min(512, s))
    vt = _proj_t(h, w_all, (row0[c_fv], fw + dw), tmp, 768, tk).reshape(b, s // tk, fw + dw, tk)
    o_fox = _attention(q_aug, 0, k_aug, 0, vt, 0, (), "fox", FOX_HEADS, AUG_DIM, tq, tk)

    topk = min(INDEX_TOPK, s // 4)
    wt = jnp.transpose(wl.reshape(b, s, LANES)[:, :, :IDX_HEADS], (0, 2, 1))
    bias = _select_bias(iq.reshape(b, s, iqw), ika.reshape(b, s, LANES), ikb.reshape(b, s, LANES),
                        wt, topk, tq, tk)
    o_dsa = _attention(qk2, 0, qk2, 1, vt, 1, (bias,), "dsa", DSA_HEADS, HEAD_DIM, tq, tk)

    tmm = min(512, b * n_mem)
    m_n = _rmsnorm(mem.reshape(b * n_mem, d), mem_norm_g, BF16, tmm)
    w_kv = w_mem_kv.astype(BF16)
    mk = _proj(m_n, w_kv[:, :cw], "norm", BF16, tmm, cw, gains=jnp.tile(cross_k_norm_g, CROSS_HEADS))
    mv = _proj(m_n, w_kv[:, cw:], "raw", BF16, tmm, cw)
    o_cross = _cross_attention(qk1, (2 * fw) // cw, mk.reshape(b, n_mem, cw), mv.reshape(b, n_mem, cw),
                               min(512, s))

    x_mid = _merge(h, o_fox.reshape(n, fw), o_dsa.reshape(n, dw), o_cross.reshape(n, cw), x2d,
                   w_all, row0[c_g], w_branch_fox.astype(BF16), w_branch_dsa.astype(BF16),
                   w_branch_cross.astype(BF16), w_out.astype(BF16), tm, MERGE_TN)

    w_router = jnp.concatenate(
        [router_group_w, router_expert_w, jnp.zeros((d, LANES - N_GROUPS - N_EXPERTS), F32)],
        axis=1).astype(BF16)
    b_router = jnp.concatenate(
        [router_group_b, router_expert_b, jnp.zeros((LANES - N_GROUPS - N_EXPERTS,), F32)])
    h2, eid, gate = _router(x_mid, ffn_norm_g, w_router, b_router, tm)
    pos, slot_tok, ge, gb0, gnb, n_used, n_slots = _moe_plan(eid[:, :MOE_TOP_K], n)
    ys = _experts(h2, slot_tok, ge, gb0, gnb, n_used, expert_w_gate, expert_w_up, expert_w_down,
                  n_slots)
    out = _combine(x_mid, gate, ys, pos, min(256, n))
    return out.reshape(b, s, d)


def kernel(x, mem, positions, attn_norm_g, mem_norm_g, w_in, fox_forget_b, fox_q_norm_g, fox_k_norm_g, dsa_q_norm_g, dsa_k_norm_g, idx_k_norm_g, cross_q_norm_g, cross_k_norm_g, w_mem_kv, w_branch_fox, w_branch_dsa, w_branch_cross, w_out, ffn_norm_g, router_group_w, router_group_b, router_expert_w, router_expert_b, expert_w_gate, expert_w_up, expert_w_down):
    depth = w_in.shape[0]
    for layer in range(depth):
        x = _layer(
            x, mem, positions, attn_norm_g[layer], mem_norm_g[layer], w_in[layer:layer + 1],
            fox_forget_b[layer], fox_q_norm_g[layer], fox_k_norm_g[layer],
            dsa_q_norm_g[layer], dsa_k_norm_g[layer], idx_k_norm_g[layer],
            cross_q_norm_g[layer], cross_k_norm_g[layer], w_mem_kv[layer],
            w_branch_fox[layer], w_branch_dsa[layer], w_branch_cross[layer], w_out[layer],
            ffn_norm_g[layer], router_group_w[layer], router_group_b[layer],
            router_expert_w[layer], router_expert_b[layer], expert_w_gate[layer],
            expert_w_up[layer], expert_w_down[layer])
    return x
```

```python
import functools

import jax
import jax.numpy as jnp
from jax import lax
from jax.experimental import pallas as pl
from jax.experimental.pallas import tpu as pltpu

F32 = jnp.float32
BF16 = jnp.bfloat16

LANES = 128
HEAD_DIM = 128
FOX_HEADS = 6
DSA_HEADS = 6
CROSS_HEADS = 4
IDX_HEADS = 16
IDX_DIM = 64
ROPE_THETA = 500000.0
ROT_DIM = HEAD_DIM // 4
IDX_ROT_DIM = IDX_DIM // 4
INDEX_TOPK = 256
N_GROUPS = 4
EXPERTS_PER_GROUP = 8
N_EXPERTS = N_GROUPS * EXPERTS_PER_GROUP
MOE_TOP_K = 2
EPS = 1e-6

NEG = -1e30
INT_MIN = -(2 ** 31)
VMEM_LIMIT = 56 * 1024 * 1024

ATT_TQ = 512
ATT_TK = 512
ATT_HEADS_PER_LOOP = 6
PROJ_TM = 1024
PROJ_ROPE_TM = 256
MERGE_TN = 512
MERGE_SUB = 256
MOE_BM = 128
MOE_GROUP_BLOCKS = 8
MOE_MM_ROWS = 256
MOE_FC = 512
DMA_ISSUE_UNROLL = 8


def _cparams(*sem):
    return pltpu.CompilerParams(dimension_semantics=sem, vmem_limit_bytes=VMEM_LIMIT)


def _rmsnorm_body(x_ref, g_ref, o_ref):
    x = x_ref[...]
    r = lax.rsqrt(jnp.mean(x * x, axis=-1, keepdims=True) + EPS)
    o_ref[...] = ((x * r) * g_ref[...]).astype(o_ref.dtype)


def _rmsnorm(x2d, g, out_dtype, tm):
    m, d = x2d.shape
    return pl.pallas_call(
        _rmsnorm_body,
        grid=(m // tm,),
        in_specs=[pl.BlockSpec((tm, d), lambda i: (i, 0)),
                  pl.BlockSpec((1, d), lambda i: (0, 0))],
        out_specs=pl.BlockSpec((tm, d), lambda i: (i, 0)),
        out_shape=jax.ShapeDtypeStruct((m, d), out_dtype),
        compiler_params=_cparams("parallel"),
        name="rmsnorm",
    )(x2d, g.reshape(1, d))


W_PREP_ROWS = 128


def _w_prep_body(src_ref, w_ref, o_ref):
    del src_ref
    o_ref[...] = w_ref[:, 0, :].astype(o_ref.dtype)


def _w_prep(w_t, src_rows):
    _, _, k = w_t.shape
    n_chunks = src_rows.shape[0]
    grid_spec = pltpu.PrefetchScalarGridSpec(
        num_scalar_prefetch=1,
        grid=(n_chunks,),
        in_specs=[pl.BlockSpec((pl.Element(W_PREP_ROWS), pl.Element(1), pl.Element(k)),
                               lambda c, src: (src[c], 0, 0))],
        out_specs=pl.BlockSpec((W_PREP_ROWS, k), lambda c, src: (c, 0)),
    )
    return pl.pallas_call(
        _w_prep_body,
        grid_spec=grid_spec,
        out_shape=jax.ShapeDtypeStruct((n_chunks * W_PREP_ROWS, k), BF16),
        compiler_params=_cparams("parallel"),
        name="w_prep",
    )(src_rows, w_t)


IDX_FREQ_LANE = 32


def _rope_table_body(pos_ref, f_ref, cd_ref, sd_ref, ci_ref, si_ref):
    ang = pos_ref[...] * f_ref[...]
    c = jnp.cos(ang)
    s = jnp.sin(ang)
    lane = lax.broadcasted_iota(jnp.int32, c.shape, 1)
    hd = ROT_DIM // 2
    cd_ref[...] = jnp.where(lane < hd, c, jnp.where(lane < ROT_DIM, pltpu.roll(c, hd, 1), 1.0))
    sd_ref[...] = jnp.where(lane < hd, -s, jnp.where(lane < ROT_DIM, pltpu.roll(s, hd, 1), 0.0))
    m = lane & (IDX_DIM - 1)
    hi = IDX_ROT_DIM // 2
    ci_src = pltpu.roll(c, LANES - IDX_FREQ_LANE, 1)
    si_src = pltpu.roll(s, LANES - IDX_FREQ_LANE, 1)
    ci_ref[...] = jnp.where(m < IDX_ROT_DIM, ci_src, 1.0)
    si_ref[...] = jnp.where(m < hi, -si_src, jnp.where(m < IDX_ROT_DIM, si_src, 0.0))


def _rope_tables(pos_col, tm):
    n = pos_col.shape[0]

    def inv_freq(rot_dim):
        half = rot_dim // 2
        return jnp.power(jnp.float32(ROPE_THETA), -jnp.arange(half, dtype=F32) * 2.0 / rot_dim)

    fi = jnp.tile(inv_freq(IDX_ROT_DIM), 2)
    freq = jnp.zeros((LANES,), F32).at[:ROT_DIM // 2].set(inv_freq(ROT_DIM))
    for base in (IDX_FREQ_LANE, IDX_FREQ_LANE + IDX_DIM):
        freq = freq.at[base:base + IDX_ROT_DIM].set(fi)
    tab = pl.BlockSpec((tm, LANES), lambda i: (i, 0))
    return pl.pallas_call(
        _rope_table_body,
        grid=(n // tm,),
        in_specs=[pl.BlockSpec((tm, 1), lambda i: (i, 0)), pl.BlockSpec((1, LANES), lambda i: (0, 0))],
        out_specs=[tab] * 4,
        out_shape=[jax.ShapeDtypeStruct((n, LANES), F32)] * 4,
        compiler_params=_cparams("parallel"),
        name="rope_tables",
    )(pos_col, freq.reshape(1, LANES))


def _apply_rope(a, c, s, half, period):
    lane = lax.broadcasted_iota(jnp.int32, a.shape, 1)
    first = (lane & (period - 1)) < half
    partner = jnp.where(first, pltpu.roll(a, LANES - half, 1), pltpu.roll(a, half, 1))
    return a * c + partner * s


def _proj_body(*refs, mode, n_lane_groups, w_rows):
    h_ref, w_ref = refs[0], refs[1]
    o_ref = refs[-1]

    if w_rows:
        acc = lax.dot_general(h_ref[...], w_ref[...], (((1,), (1,)), ((), ())),
                              preferred_element_type=F32)
    else:
        acc = jnp.dot(h_ref[...], w_ref[...], preferred_element_type=F32)
    if mode == "raw":
        o_ref[...] = acc.astype(o_ref.dtype)
        return
    for g in range(n_lane_groups):
        sl = slice(g * LANES, (g + 1) * LANES)
        a = acc[:, sl]
        if mode in ("norm", "norm_rope"):
            g_ref = refs[2]
            a = (a * lax.rsqrt(jnp.mean(a * a, axis=-1, keepdims=True) + EPS)) * g_ref[:, sl]
        if mode == "norm_rope":
            a = _apply_rope(a, refs[3][...], refs[4][...], ROT_DIM // 2, HEAD_DIM)
        if mode == "rope_idx":
            a = _apply_rope(a, refs[2][...], refs[3][...], IDX_ROT_DIM // 2, IDX_DIM)
        o_ref[:, sl] = a.astype(o_ref.dtype)


def _proj_t_body(h_ref, w_ref, o_ref, *, tk):
    acc = lax.dot_general(w_ref[...], h_ref[...], (((1,), (1,)), ((), ())),
                          preferred_element_type=F32)
    for q in range(o_ref.shape[0]):
        o_ref[q] = acc[:, q * tk:(q + 1) * tk].astype(o_ref.dtype)


def _proj_t(h, w, w_rows, tm, tn, tk):
    m, k = h.shape
    row0, n = w_rows
    assert row0 % tn == 0 and n % tn == 0 and tm % tk == 0
    return pl.pallas_call(
        functools.partial(_proj_t_body, tk=tk),
        grid=(n // tn, m // tm),
        in_specs=[pl.BlockSpec((tm, k), lambda j, i: (i, 0)),
                  pl.BlockSpec((tn, k), lambda j, i: (row0 // tn + j, 0))],
        out_specs=pl.BlockSpec((tm // tk, tn, tk), lambda j, i: (i, j, 0)),
        out_shape=jax.ShapeDtypeStruct((m // tk, n, tk), BF16),
        compiler_params=_cparams("parallel", "parallel"),
        name="proj_t",
    )(h, w)


def _proj(h, w, mode, out_dtype, tm, tn, gains=None, tables=None, w_rows=None):
    m, k = h.shape
    if w_rows is None:
        n = w.shape[1]
        w_spec = pl.BlockSpec((k, tn), lambda j, i: (0, j))
    else:
        row0, n = w_rows
        assert row0 % tn == 0 and n % tn == 0
        w_spec = pl.BlockSpec((tn, k), lambda j, i: (row0 // tn + j, 0))
    in_specs = [pl.BlockSpec((tm, k), lambda j, i: (i, 0)), w_spec]
    args = [h, w]
    if gains is not None:
        in_specs.append(pl.BlockSpec((1, tn), lambda j, i: (0, j)))
        args.append(gains.reshape(1, n))
    if tables is not None:
        for t in tables:
            in_specs.append(pl.BlockSpec((tm, LANES), lambda j, i: (i, 0)))
            args.append(t)
    return pl.pallas_call(
        functools.partial(_proj_body, mode=mode, n_lane_groups=tn // LANES,
                          w_rows=w_rows is not None),
        grid=(n // tn, m // tm),
        in_specs=in_specs,
        out_specs=pl.BlockSpec((tm, tn), lambda j, i: (i, j)),
        out_shape=jax.ShapeDtypeStruct((m, n), out_dtype),
        compiler_params=_cparams("parallel", "parallel"),
        name="proj_" + mode,
    )(*args)


SMALL_IK = 0
SMALL_IW = 64
SMALL_F = 80


def _small_body(x_ref, gk_ref, fb_ref, c_ref, s_ref, ika_ref, ikb_ref, wl_ref, lf_ref):
    x = x_ref[...]
    lane = lax.broadcasted_iota(jnp.int32, x.shape, 1)
    is_ik = lane < IDX_DIM
    ik = jnp.where(is_ik, x, 0.0)
    ms = jnp.sum(ik * ik, axis=-1, keepdims=True) * (1.0 / IDX_DIM)
    ik = (ik * lax.rsqrt(ms + EPS)) * gk_ref[...]
    ik = _apply_rope(ik, c_ref[...], s_ref[...], IDX_ROT_DIM // 2, IDX_DIM)
    ik = jnp.where(is_ik, ik, 0.0)
    ika_ref[...] = ik.astype(BF16)
    ikb_ref[...] = pltpu.roll(ik, IDX_DIM, 1).astype(BF16)
    wl_ref[...] = pltpu.roll(x, LANES - SMALL_IW, 1) * (2.0 ** -5)
    z = x + fb_ref[...]
    log_f = jnp.minimum(z, 0.0) - jnp.log1p(jnp.exp(-jnp.abs(z)))
    lf_ref[...] = pltpu.roll(log_f, LANES - SMALL_F, 1)


def _small(x, idx_k_gain_lane, forget_b_lane, ci, si, tm):
    n = x.shape[0]
    row = pl.BlockSpec((1, LANES), lambda i: (0, 0))
    tab = pl.BlockSpec((tm, LANES), lambda i: (i, 0))
    return pl.pallas_call(
        _small_body,
        grid=(n // tm,),
        in_specs=[tab, row, row, tab, tab],
        out_specs=[tab, tab, tab, tab],
        out_shape=[jax.ShapeDtypeStruct((n, LANES), BF16), jax.ShapeDtypeStruct((n, LANES), BF16),
                   jax.ShapeDtypeStruct((n, LANES), F32), jax.ShapeDtypeStruct((n, LANES), F32)],
        compiler_params=_cparams("parallel"),
        name="small_cols",
    )(x, idx_k_gain_lane.reshape(1, LANES), forget_b_lane.reshape(1, LANES), ci, si)


def _cumsum_body(col_ref, ccol_ref, *, seq):
    c = col_ref[...]
    ci = lax.broadcasted_iota(jnp.int32, c.shape, 0)
    sh = 1
    while sh < seq:
        c = c + jnp.where(ci >= sh, pltpu.roll(c, sh, 0), 0.0)
        sh *= 2
    ccol_ref[...] = c


def _forget_cumsum(lf_col):
    b, s, _ = lf_col.shape
    return pl.pallas_call(
        functools.partial(_cumsum_body, seq=s),
        grid=(b,),
        in_specs=[pl.BlockSpec((None, s, LANES), lambda i: (i, 0, 0))],
        out_specs=pl.BlockSpec((None, s, LANES), lambda i: (i, 0, 0)),
        out_shape=jax.ShapeDtypeStruct((b, s, LANES), F32),
        compiler_params=_cparams("parallel"),
        name="forget_cumsum",
    )(lf_col)


AUG_DIM = 2 * HEAD_DIM
LOG2E = 1.4426950408889634


def _split3(x):
    hi = x.astype(BF16).astype(F32)
    r1 = x - hi
    mid = r1.astype(BF16).astype(F32)
    lo = (r1 - mid).astype(BF16).astype(F32)
    return hi, mid, lo


def _fox_pack_body(q_ref, k_ref, cum_ref, qa_ref, ka_ref, *, n_heads):
    rows = q_ref.shape[0]
    lane = lax.broadcasted_iota(jnp.int32, (rows, LANES), 1)
    for h in range(n_heads):
        c = jnp.broadcast_to(cum_ref[:, h:h + 1], (rows, LANES)) * LOG2E
        qh, qm, ql = _split3(c)
        kh, km, kl = _split3(-c)
        q_aug = jnp.where(lane == 0, qh, jnp.where(lane == 1, qm, jnp.where(
            lane == 2, ql, jnp.where(lane < 6, 1.0, 0.0))))
        k_aug = jnp.where(lane < 3, 1.0, jnp.where(lane == 3, kh, jnp.where(
            lane == 4, km, jnp.where(lane == 5, kl, 0.0))))
        base = h * AUG_DIM
        qa_ref[:, base:base + HEAD_DIM] = q_ref[:, h * HEAD_DIM:(h + 1) * HEAD_DIM]
        ka_ref[:, base:base + HEAD_DIM] = k_ref[:, h * HEAD_DIM:(h + 1) * HEAD_DIM]
        qa_ref[:, base + HEAD_DIM:base + AUG_DIM] = q_aug.astype(BF16)
        ka_ref[:, base + HEAD_DIM:base + AUG_DIM] = k_aug.astype(BF16)


def _fox_pack(qk, cum_col, n_heads, ts):
    b, s, _ = qk.shape
    w = n_heads * HEAD_DIM
    wa = n_heads * AUG_DIM
    return pl.pallas_call(
        functools.partial(_fox_pack_body, n_heads=n_heads),
        grid=(b, s // ts),
        in_specs=[pl.BlockSpec((None, ts, w), lambda bi, i: (bi, i, 0)),
                  pl.BlockSpec((None, ts, w), lambda bi, i: (bi, i, 1)),
                  pl.BlockSpec((None, ts, LANES), lambda bi, i: (bi, i, 0))],
        out_specs=[pl.BlockSpec((None, ts, wa), lambda bi, i: (bi, i, 0)),
                   pl.BlockSpec((None, ts, wa), lambda bi, i: (bi, i, 0))],
        out_shape=[jax.ShapeDtypeStruct((b, s, wa), BF16)] * 2,
        compiler_params=_cparams("parallel", "parallel"),
        name="fox_pack",
    )(qk, qk, cum_col)


def _attn_body(q_ref, k_ref, vt_ref, *rest, tq, tk, n_heads, kdim, mode, heads_per_loop):
    o_ref = rest[-1]
    qi = pl.program_id(1)
    key0 = lax.broadcasted_iota(jnp.int32, (tk, tq), 0)
    qry = qi * tq + lax.broadcasted_iota(jnp.int32, (tk, tq), 1)
    def scores(h, j):
        off = pl.multiple_of(j * tk, tk)
        q = q_ref[:, h * kdim:(h + 1) * kdim]
        k = k_ref[pl.ds(off, tk), h * kdim:(h + 1) * kdim]
        return lax.dot_general(k, q, (((1,), (1,)), ((), ())), preferred_element_type=F32)

    def flush(h, j, p, alpha, acc):
        vt = vt_ref[j, h * HEAD_DIM:(h + 1) * HEAD_DIM, :]
        return alpha * acc + jnp.dot(vt, p, preferred_element_type=F32)

    def softmax_step(h, j, m, l, masked):
        s = scores(h, j)
        if mode == "dsa":
            s = s + rest[0][j]
        if masked:
            s = jnp.where(key0 + j * tk <= qry, s, NEG)
        m_new = jnp.maximum(m, jnp.max(s, axis=0, keepdims=True))
        alpha = jnp.exp2(m - m_new)
        p = jnp.exp2(s - m_new)
        l = alpha * l + jnp.sum(p, axis=0, keepdims=True)
        return p.astype(BF16), alpha, m_new, l

    last = qi
    for h0 in range(0, n_heads, heads_per_loop):
        heads = range(h0, h0 + heads_per_loop)

        def step(j, carries, masked=False, heads=heads):
            prev = jnp.maximum(j - 1, 0)
            out = []
            for h, (p, alpha, m, l, acc) in zip(heads, carries):
                acc = flush(h, prev, p, alpha, acc)
                p, alpha, m, l = softmax_step(h, j, m, l, masked)
                out.append((p, alpha, m, l, acc))
            return tuple(out)

        carries = tuple(
            (jnp.zeros((tk, tq), BF16), jnp.ones((1, tq), F32), jnp.full((1, tq), NEG, F32),
             jnp.zeros((1, tq), F32), jnp.zeros((HEAD_DIM, tq), F32)) for h in heads)
        n_loop = last if mode == "fox" else last + 1
        carries = lax.fori_loop(0, n_loop, step, carries)
        if mode == "fox":
            carries = step(last, carries, masked=True)
        for h, (p, alpha, m, l, acc) in zip(heads, carries):
            acc = flush(h, last, p, alpha, acc)
            o_ref[:, h * HEAD_DIM:(h + 1) * HEAD_DIM] = jnp.transpose(acc / l).astype(o_ref.dtype)


def _attention(q_arr, q_blk, k_arr, k_blk, vt_arr, v_blk, extras, mode, n_heads, kdim, tq, tk):
    b, s, _ = q_arr.shape
    w = n_heads * HEAD_DIM
    wk = n_heads * kdim
    in_specs = [pl.BlockSpec((None, tq, wk), lambda bi, qi: (bi, qi, q_blk)),
                pl.BlockSpec((None, s, wk), lambda bi, qi: (bi, 0, k_blk)),
                pl.BlockSpec((None, s // tk, w, tk), lambda bi, qi: (bi, 0, v_blk, 0))]
    if mode == "dsa":
        in_specs += [pl.BlockSpec((None, None, s // tk, tk, tq), lambda bi, qi: (bi, qi, 0, 0, 0))]
    return pl.pallas_call(
        functools.partial(_attn_body, tq=tq, tk=tk, n_heads=n_heads, kdim=kdim, mode=mode,
                          heads_per_loop=ATT_HEADS_PER_LOOP),
        grid=(b, s // tq),
        in_specs=in_specs,
        out_specs=pl.BlockSpec((None, tq, w), lambda bi, qi: (bi, qi, 0)),
        out_shape=jax.ShapeDtypeStruct((b, s, w), BF16),
        compiler_params=_cparams("parallel", "parallel"),
        name="attn_" + mode,
    )(q_arr, k_arr, vt_arr, *extras)


def _select_body(iq_ref, ika_ref, ikb_ref, wt_ref, o_ref, keys_ref, half_ref, *, tq, tk, n_chunks, topk):
    qi = pl.program_id(1)
    used = qi + 1
    key0 = lax.broadcasted_iota(jnp.int32, (tk, tq), 0)
    qry = qi * tq + lax.broadcasted_iota(jnp.int32, (tk, tq), 1)

    def score_chunk(c, _):
        off = pl.multiple_of(c * tk, tk)
        ka = ika_ref[pl.ds(off, tk), :]
        kb = ikb_ref[pl.ds(off, tk), :]
        acc = jnp.zeros((tk, tq), F32)
        for j in range(IDX_HEADS):
            pair = iq_ref[:, (j // 2) * LANES:(j // 2 + 1) * LANES]
            logit = lax.dot_general(ka if j % 2 == 0 else kb, pair, (((1,), (1,)), ((), ())),
                                    preferred_element_type=F32)
            acc = acc + wt_ref[j:j + 1, :] * jnp.maximum(logit, 0.0)
        bits = lax.bitcast_convert_type(acc, jnp.int32)
        key = jnp.where(bits < 0, bits ^ jnp.int32(0x7FFFFFFF), bits)
        key = jnp.where(key0 + off <= qry, key, jnp.int32(INT_MIN))
        keys_ref[c] = key
        half_ref[c] = lax.shift_right_arithmetic(key, 16).astype(jnp.int16)
        return 0

    lax.fori_loop(0, used, score_chunk, 0)

    one16 = jnp.int16(1)
    zero16 = jnp.int16(0)

    def count_ge(cand16):
        def count_chunk(c, cnt):
            partial = [None] * 4
            for i, r in enumerate(range(0, tk, 16)):
                ge = jnp.where(half_ref[c, r:r + 16, :] >= cand16, one16, zero16)
                partial[i % 4] = ge if partial[i % 4] is None else partial[i % 4] + ge
            return cnt + ((partial[0] + partial[1]) + (partial[2] + partial[3]))

        cnt = lax.fori_loop(0, used, count_chunk, jnp.zeros((16, tq), jnp.int16))
        return jnp.sum(cnt.astype(jnp.int32), axis=0, keepdims=True)

    def search16(need):
        def bit_step(i, carry):
            prefix, kept = carry
            cand_u = prefix | jnp.left_shift(jnp.int32(1), 15 - i)
            cnt = count_ge((cand_u - 32768).astype(jnp.int16))
            ok = cnt >= need
            return jnp.where(ok, cand_u, prefix), jnp.where(ok, cnt, kept)

        init = (jnp.zeros((1, tq), jnp.int32), jnp.full((1, tq), tk, jnp.int32) * used)
        prefix, kept = lax.fori_loop(0, 16, bit_step, init)
        return prefix - 32768, kept

    hi, _ = search16(topk)
    above = count_ge(jnp.minimum(hi + 1, 32767).astype(jnp.int16))
    above = jnp.where(hi >= 32767, 0, above)
    hi16 = hi.astype(jnp.int16)

    def low_chunk(c, _):
        low = (keys_ref[c] & jnp.int32(0xFFFF)) - 32768
        half_ref[c] = jnp.where(half_ref[c] == hi16, low.astype(jnp.int16), jnp.int16(-32768))
        return 0

    lax.fori_loop(0, used, low_chunk, 0)
    lo, at_or_above_lo = search16(topk - above)
    thr = lax.shift_left(hi, 16) | (lo + 32768)
    thr = jnp.maximum(thr, jnp.int32(INT_MIN + 1))
    tied = jnp.max(above + at_or_above_lo) > topk

    def count_keys(pred):
        def chunk(c, cnt):
            hit = jnp.where(pred(keys_ref[c]), 1.0, 0.0)
            return cnt + jnp.sum(hit.reshape(tk // 8, 8, tq), axis=0)

        cnt = lax.fori_loop(0, used, chunk, jnp.zeros((8, tq), F32))
        return jnp.sum(cnt, axis=0, keepdims=True)

    @pl.when(jnp.logical_not(tied))
    def _():
        def write_chunk(c, _):
            o_ref[c] = jnp.where(keys_ref[c] >= thr, 0.0, NEG)
            return 0

        lax.fori_loop(0, used, write_chunk, 0)

    @pl.when(tied)
    def _():
        need = topk - count_keys(lambda k: k > thr)
        earlier = jnp.where(lax.broadcasted_iota(jnp.int32, (tk, tk), 1)
                            < lax.broadcasted_iota(jnp.int32, (tk, tk), 0), 1.0, 0.0).astype(BF16)

        def write_chunk(c, seen):
            keys = keys_ref[c]
            tie = jnp.where(keys == thr, 1.0, 0.0)
            rank = jnp.dot(earlier, tie.astype(BF16), preferred_element_type=F32) + seen
            o_ref[c] = jnp.where(keys > thr, 0.0,
                                 jnp.where(keys == thr, jnp.where(rank < need, 0.0, NEG), NEG))
            return seen + jnp.sum(tie, axis=0, keepdims=True)

        lax.fori_loop(0, used, write_chunk, jnp.zeros((1, tq), F32))

    def fill_chunk(c, _):
        o_ref[c] = jnp.full((tk, tq), NEG, F32)
        return 0

    lax.fori_loop(used, n_chunks, fill_chunk, 0)


def _select_bias(iq, ika, ikb, wt, topk, tq, tk):
    b, s, _ = iq.shape
    n_chunks = s // tk
    return pl.pallas_call(
        functools.partial(_select_body, tq=tq, tk=tk, n_chunks=n_chunks, topk=topk),
        grid=(b, s // tq),
        in_specs=[pl.BlockSpec((None, tq, IDX_HEADS * IDX_DIM), lambda bi, qi: (bi, qi, 0)),
                  pl.BlockSpec((None, s, LANES), lambda bi, qi: (bi, 0, 0)),
                  pl.BlockSpec((None, s, LANES), lambda bi, qi: (bi, 0, 0)),
                  pl.BlockSpec((None, IDX_HEADS, tq), lambda bi, qi: (bi, 0, qi))],
        out_specs=pl.BlockSpec((None, None, n_chunks, tk, tq), lambda bi, qi: (bi, qi, 0, 0, 0)),
        out_shape=jax.ShapeDtypeStruct((b, s // tq, n_chunks, tk, tq), F32),
        scratch_shapes=[pltpu.VMEM((n_chunks, tk, tq), jnp.int32),
                        pltpu.VMEM((n_chunks, tk, tq), jnp.int16)],
        compiler_params=_cparams("parallel", "parallel"),
        name="index_select",
    )(iq, ika, ikb, wt)


def _cross_body(q_ref, k_ref, v_ref, o_ref, *, n_heads):
    for h in range(n_heads):
        hs = slice(h * HEAD_DIM, (h + 1) * HEAD_DIM)
        s = lax.dot_general(q_ref[:, hs], k_ref[:, hs], (((1,), (1,)), ((), ())),
                            preferred_element_type=F32)
        p = jnp.exp2(s - jnp.max(s, axis=-1, keepdims=True))
        l = jnp.sum(p, axis=-1, keepdims=True)
        o = jnp.dot(p.astype(BF16), v_ref[:, hs], preferred_element_type=F32)
        o_ref[:, hs] = (o / l).astype(o_ref.dtype)


def _cross_attention(q_arr, q_blk, mk, mv, tq):
    b, s, _ = q_arr.shape
    n_mem = mk.shape[1]
    w = CROSS_HEADS * HEAD_DIM
    return pl.pallas_call(
        functools.partial(_cross_body, n_heads=CROSS_HEADS),
        grid=(b, s // tq),
        in_specs=[pl.BlockSpec((None, tq, w), lambda bi, qi: (bi, qi, q_blk)),
                  pl.BlockSpec((None, n_mem, w), lambda bi, qi: (bi, 0, 0)),
                  pl.BlockSpec((None, n_mem, w), lambda bi, qi: (bi, 0, 0))],
        out_specs=pl.BlockSpec((None, tq, w), lambda bi, qi: (bi, qi, 0)),
        out_shape=jax.ShapeDtypeStruct((b, s, w), BF16),
        compiler_params=_cparams("parallel", "parallel"),
        name="cross_attn",
    )(q_arr, mk, mv)


def _sigmoid(x):
    return 1.0 / (1.0 + jnp.exp(-x))


def _merge_body(h_ref, of_ref, od_ref, oc_ref, x_ref, wgf_ref, wgd_ref, wgc_ref,
                wbf_ref, wbd_ref, wbc_ref, wo_ref, o_ref):
    j = pl.program_id(1)
    h = h_ref[...]
    tn = wo_ref.shape[0]
    sub = min(tn, MERGE_SUB)

    def gate_logits(wg_ref, cs):
        return lax.dot_general(h, wg_ref[cs, :], (((1,), (1,)), ((), ())), preferred_element_type=F32)

    part = None
    for c in range(tn // sub):
        cs = slice(c * sub, (c + 1) * sub)
        zf, zd, zc = gate_logits(wgf_ref, cs), gate_logits(wgd_ref, cs), gate_logits(wgc_ref, cs)
        bf = jnp.dot(of_ref[...], wbf_ref[:, cs], preferred_element_type=F32)
        bd = jnp.dot(od_ref[...], wbd_ref[:, cs], preferred_element_type=F32)
        bc = jnp.dot(oc_ref[...], wbc_ref[:, cs], preferred_element_type=F32)
        merged = (_sigmoid(zf) * bf + _sigmoid(zd) * bd) + _sigmoid(zc) * bc
        p = jnp.dot(merged.astype(BF16), wo_ref[cs, :], preferred_element_type=F32)
        part = p if part is None else part + p

    @pl.when(j == 0)
    def _():
        o_ref[...] = x_ref[...] + part

    @pl.when(j != 0)
    def _():
        o_ref[...] += part


def _merge(h, o_fox, o_dsa, o_cross, x, w_gates, gate_row0, wbf, wbd, wbc, w_out, tm, tn):
    m, d = x.shape
    nj = d // tn
    assert gate_row0 % tn == 0
    g0 = gate_row0 // tn
    row = lambda i, j: (i, 0)
    in_specs = [pl.BlockSpec((tm, d), row),
                pl.BlockSpec((tm, o_fox.shape[1]), row),
                pl.BlockSpec((tm, o_dsa.shape[1]), row),
                pl.BlockSpec((tm, o_cross.shape[1]), row),
                pl.BlockSpec((tm, d), row),
                pl.BlockSpec((tn, d), lambda i, j: (g0 + j, 0)),
                pl.BlockSpec((tn, d), lambda i, j: (g0 + nj + j, 0)),
                pl.BlockSpec((tn, d), lambda i, j: (g0 + 2 * nj + j, 0)),
                pl.BlockSpec((wbf.shape[0], tn), lambda i, j: (0, j)),
                pl.BlockSpec((wbd.shape[0], tn), lambda i, j: (0, j)),
                pl.BlockSpec((wbc.shape[0], tn), lambda i, j: (0, j)),
                pl.BlockSpec((tn, d), lambda i, j: (j, 0))]
    return pl.pallas_call(
        _merge_body,
        grid=(m // tm, nj),
        in_specs=in_specs,
        out_specs=pl.BlockSpec((tm, d), row),
        out_shape=jax.ShapeDtypeStruct((m, d), F32),
        compiler_params=_cparams("parallel", "arbitrary"),
        name="gated_merge",
    )(h, o_fox, o_dsa, o_cross, x, w_gates, w_gates, w_gates, wbf, wbd, wbc, w_out)


def _router_body(x_ref, g_ref, w_ref, b_ref, h_ref, eid_ref, gate_ref):
    x = x_ref[...]
    r = lax.rsqrt(jnp.mean(x * x, axis=-1, keepdims=True) + EPS)
    h = (x * r) * g_ref[...]
    hb = h.astype(BF16)
    bits = lax.bitcast_convert_type(hb.astype(F32), jnp.int32)
    half = bits.shape[1] // 2
    h_ref[...] = lax.shift_right_logical(bits[:, :half], 16) | (bits[:, half:] & jnp.int32(-65536))
    logits = jnp.dot(hb, w_ref[...], preferred_element_type=F32) + b_ref[...]
    lane = lax.broadcasted_iota(jnp.int32, logits.shape, 1).astype(F32)

    def masked_max(mask):
        return jnp.max(jnp.where(mask, logits, -jnp.inf), axis=-1, keepdims=True)

    def first_lane(mask):
        return jnp.min(jnp.where(mask, lane, float(LANES)), axis=-1, keepdims=True)

    is_group = lane < N_GROUPS
    gmax = masked_max(is_group)
    grp = first_lane(is_group & (logits == gmax))
    denom = jnp.sum(jnp.where(is_group, jnp.exp(logits - gmax), 0.0), axis=-1, keepdims=True)
    p_sel = 1.0 / denom
    lo = N_GROUPS + grp * EXPERTS_PER_GROUP
    in_grp = (lane >= lo) & (lane < lo + EXPERTS_PER_GROUP)
    e1 = masked_max(in_grp)
    j1 = first_lane(in_grp & (logits == e1))
    rest = in_grp & (lane != j1)
    e2 = masked_max(rest)
    j2 = first_lane(rest & (logits == e2))
    t = jnp.exp(e2 - e1)
    g1 = p_sel * (1.0 / (1.0 + t))
    g2 = p_sel * (t / (1.0 + t))
    eid = jnp.where(lane == 0.0, j1 - N_GROUPS, jnp.where(lane == 1.0, j2 - N_GROUPS, 0.0))
    eid_ref[...] = eid.astype(jnp.int32)
    gate_ref[...] = jnp.where(lane == 0.0, g1, jnp.where(lane == 1.0, g2, 0.0))


def _router(x, g, w_router, b_router, tm):
    m, d = x.shape
    return pl.pallas_call(
        _router_body,
        grid=(m // tm,),
        in_specs=[pl.BlockSpec((tm, d), lambda i: (i, 0)),
                  pl.BlockSpec((1, d), lambda i: (0, 0)),
                  pl.BlockSpec((d, LANES), lambda i: (0, 0)),
                  pl.BlockSpec((1, LANES), lambda i: (0, 0))],
        out_specs=[pl.BlockSpec((tm, d // 2), lambda i: (i, 0)),
                   pl.BlockSpec((tm, LANES), lambda i: (i, 0)),
                   pl.BlockSpec((tm, LANES), lambda i: (i, 0))],
        out_shape=[jax.ShapeDtypeStruct((m, d // 2), jnp.int32),
                   jax.ShapeDtypeStruct((m, LANES), jnp.int32),
                   jax.ShapeDtypeStruct((m, LANES), F32)],
        compiler_params=_cparams("parallel"),
        name="moe_router",
    )(x, g.reshape(1, d), w_router, b_router.reshape(1, LANES))


def _expert_body(ge_ref, gb0_ref, gnb_ref, used_ref, tok_ref, h_hbm, wg_ref, wu_ref, wd_ref, ys_hbm,
                 xs_ref, xb_ref, acc_ref, wgb_ref, wub_ref, wdb_ref, sem_in, sem_out,
                 *, bm, n_f, n_blocks):
    g = pl.program_id(0)
    f = pl.program_id(1)
    nb = gnb_ref[g]
    blk0 = gb0_ref[g]

    @pl.when(jnp.logical_and(g == 0, f == 0))
    def _():
        acc_ref[pl.ds(0, bm)] = jnp.zeros((bm,) + acc_ref.shape[1:], F32)

        def tail_copy(i):
            return pltpu.make_async_copy(acc_ref.at[pl.ds(0, bm)], ys_hbm.at[pl.ds(i * bm, bm)],
                                         sem_out)

        def start_tail(i, _):
            tail_copy(i).start()
            return 0

        lax.fori_loop(used_ref[0], n_blocks, start_tail, 0)

        def wait_tail(i, _):
            tail_copy(i).wait()
            return 0

        lax.fori_loop(used_ref[0], n_blocks, wait_tail, 0)

    def gather(grp):
        slot = grp % 2
        base = gb0_ref[grp] * bm

        def issue(r0, _):
            for u in range(DMA_ISSUE_UNROLL):
                r = r0 * DMA_ISSUE_UNROLL + u
                pltpu.make_async_copy(h_hbm.at[pl.ds(tok_ref[base + r], 1)],
                                      xs_ref.at[slot, pl.ds(r, 1)], sem_in.at[slot]).start()
            return 0

        lax.fori_loop(0, gnb_ref[grp] * (bm // DMA_ISSUE_UNROLL), issue, 0)

    def block_copy_out(i):
        return pltpu.make_async_copy(acc_ref.at[pl.ds(i * bm, bm)],
                                     ys_hbm.at[pl.ds((blk0 + i) * bm, bm)], sem_out)

    @pl.when(jnp.logical_and(g == 0, f == 0))
    def _():
        gather(g)

    @pl.when(jnp.logical_and(nb > 0, f == 0))
    def _():
        slot = g % 2

        def land(i, _):
            rows = pl.ds(pl.multiple_of(i * bm, bm), bm)
            pltpu.make_async_copy(h_hbm.at[pl.ds(0, bm)], xs_ref.at[slot, rows], sem_in.at[slot]).wait()
            return 0

        lax.fori_loop(0, nb, land, 0)

        half = xb_ref.shape[1] // 2

        def unpack(i, _):
            rows = pl.ds(pl.multiple_of(i * bm, bm), bm)
            w = xs_ref[slot, rows, :]
            lo = lax.bitcast_convert_type(lax.shift_left(w, 16), F32)
            hi = lax.bitcast_convert_type(w & jnp.int32(-65536), F32)
            xb_ref[rows, :half] = lo.astype(BF16)
            xb_ref[rows, half:] = hi.astype(BF16)
            return 0

        lax.fori_loop(0, nb, unpack, 0)

    @pl.when(jnp.logical_and(f == 0, g + 1 < pl.num_programs(0)))
    def _():
        gather(g + 1)

    def wait_out(n):
        def body(i, _):
            pltpu.make_async_copy(acc_ref.at[pl.ds(0, bm)], ys_hbm.at[pl.ds(0, bm)], sem_out).wait()
            return 0

        lax.fori_loop(0, n, body, 0)

    @pl.when(jnp.logical_and(f == 0, g > 0))
    def _():
        wait_out(gnb_ref[jnp.maximum(g - 1, 0)])

    @pl.when(nb > 0)
    def _():
        wgb_ref[...] = wg_ref[...].astype(BF16)
        wub_ref[...] = wu_ref[...].astype(BF16)
        wdb_ref[...] = wd_ref[...].astype(BF16)

        def swiglu(rows):
            xb = xb_ref[rows, :]
            a = jnp.dot(xb, wgb_ref[...], preferred_element_type=F32)
            u = jnp.dot(xb, wub_ref[...], preferred_element_type=F32)
            hid = (a * _sigmoid(a)) * u
            y = jnp.dot(hid.astype(BF16), wdb_ref[...], preferred_element_type=F32)

            @pl.when(f == 0)
            def _():
                acc_ref[rows, :] = y

            @pl.when(f != 0)
            def _():
                acc_ref[rows, :] += y

        per = MOE_MM_ROWS // bm

        def full_tile(i, _):
            swiglu(pl.ds(pl.multiple_of(i * MOE_MM_ROWS, MOE_MM_ROWS), MOE_MM_ROWS))
            return 0

        lax.fori_loop(0, nb // per, full_tile, 0)
        piece = per // 2
        while piece >= 1:
            done = (nb // (2 * piece)) * (2 * piece)

            @pl.when((nb // piece) % 2 == 1)
            def _(done=done, piece=piece):
                swiglu(pl.ds(pl.multiple_of(done * bm, piece * bm), piece * bm))

            piece //= 2

    @pl.when(jnp.logical_and(nb > 0, f == n_f - 1))
    def _():
        def start_out(i, _):
            block_copy_out(i).start()
            return 0

        lax.fori_loop(0, nb, start_out, 0)

        @pl.when(g == pl.num_programs(0) - 1)
        def _():
            wait_out(nb)


def _experts(h2, slot_tok, grp_e, grp_blk0, grp_nb, n_used, w_gate, w_up, w_down, n_slots):
    d = w_gate.shape[1]
    ff = w_gate.shape[2]
    n_f = ff // MOE_FC
    n_groups = grp_e.shape[0]
    rmax = MOE_GROUP_BLOCKS * MOE_BM

    def f_idx(g, f, gnb):
        return jnp.where(gnb[g] > 0, f, n_f - 1)

    grid_spec = pltpu.PrefetchScalarGridSpec(
        num_scalar_prefetch=5,
        grid=(n_groups, n_f),
        in_specs=[
            pl.BlockSpec(memory_space=pl.ANY),
            pl.BlockSpec((None, d, MOE_FC), lambda g, f, ge, gb0, gnb, *_: (ge[g], 0, f_idx(g, f, gnb))),
            pl.BlockSpec((None, d, MOE_FC), lambda g, f, ge, gb0, gnb, *_: (ge[g], 0, f_idx(g, f, gnb))),
            pl.BlockSpec((None, MOE_FC, d), lambda g, f, ge, gb0, gnb, *_: (ge[g], f_idx(g, f, gnb), 0)),
        ],
        out_specs=pl.BlockSpec(memory_space=pl.ANY),
        scratch_shapes=[pltpu.VMEM((2, rmax, d // 2), jnp.int32),
                        pltpu.VMEM((rmax, d), BF16),
                        pltpu.VMEM((rmax, d), F32),
                        pltpu.VMEM((d, MOE_FC), BF16),
                        pltpu.VMEM((d, MOE_FC), BF16),
                        pltpu.VMEM((MOE_FC, d), BF16),
                        pltpu.SemaphoreType.DMA((2,)),
                        pltpu.SemaphoreType.DMA(())],
    )
    return pl.pallas_call(
        functools.partial(_expert_body, bm=MOE_BM, n_f=n_f, n_blocks=n_slots // MOE_BM),
        grid_spec=grid_spec,
        out_shape=jax.ShapeDtypeStruct((n_slots, d), F32),
        compiler_params=_cparams("arbitrary", "arbitrary"),
        name="moe_experts",
    )(grp_e, grp_blk0, grp_nb, n_used, slot_tok, h2, w_gate, w_up, w_down)


def _combine_body(pos_ref, x_ref, gate_ref, ys_hbm, o_ref, buf_ref, sem, *, tm):
    i = pl.program_id(0)
    n_steps = pl.num_programs(0)

    def issue(step):
        slot = step % 2

        def body(r0, _):
            for u in range(DMA_ISSUE_UNROLL):
                r = r0 * DMA_ISSUE_UNROLL + u
                for k in range(MOE_TOP_K):
                    src = pos_ref[(step * tm + r) * MOE_TOP_K + k]
                    pltpu.make_async_copy(ys_hbm.at[pl.ds(src, 1)], buf_ref.at[slot, k, pl.ds(r, 1)],
                                          sem.at[slot]).start()
            return 0

        lax.fori_loop(0, tm // DMA_ISSUE_UNROLL, body, 0)

    @pl.when(i == 0)
    def _():
        issue(i)

    @pl.when(i + 1 < n_steps)
    def _():
        issue(i + 1)

    slot = i % 2
    for k in range(MOE_TOP_K):
        pltpu.make_async_copy(ys_hbm.at[pl.ds(0, tm)], buf_ref.at[slot, k], sem.at[slot]).wait()
    o_ref[...] = x_ref[...] + (buf_ref[slot, 0] * gate_ref[:, 0:1] + buf_ref[slot, 1] * gate_ref[:, 1:2])


def _combine(x, gate, ys, pos, tm):
    m, d = x.shape
    grid_spec = pltpu.PrefetchScalarGridSpec(
        num_scalar_prefetch=1,
        grid=(m // tm,),
        in_specs=[pl.BlockSpec((tm, d), lambda i, pos: (i, 0)),
                  pl.BlockSpec((tm, LANES), lambda i, pos: (i, 0)),
                  pl.BlockSpec(memory_space=pl.ANY)],
        out_specs=pl.BlockSpec((tm, d), lambda i, pos: (i, 0)),
        scratch_shapes=[pltpu.VMEM((2, MOE_TOP_K, tm, d), F32), pltpu.SemaphoreType.DMA((2,))],
    )
    return pl.pallas_call(
        functools.partial(_combine_body, tm=tm),
        grid_spec=grid_spec,
        out_shape=jax.ShapeDtypeStruct((m, d), F32),
        compiler_params=_cparams("arbitrary"),
        name="moe_combine",
    )(pos, x, gate, ys)


def _moe_plan(eid, n_tokens):
    a = n_tokens * MOE_TOP_K
    eid = eid.reshape(a)
    n_blocks = (a + N_EXPERTS * (MOE_BM - 1)) // MOE_BM
    n_slots = n_blocks * MOE_BM
    n_groups = n_blocks // MOE_GROUP_BLOCKS + N_EXPERTS
    onehot = (eid[:, None] == jnp.arange(N_EXPERTS, dtype=jnp.int32)[None, :]).astype(jnp.int32)
    rank = jnp.take_along_axis(jnp.cumsum(onehot, axis=0), eid[:, None], axis=1)[:, 0] - 1
    counts = jnp.sum(onehot, axis=0)
    blocks_e = (counts + MOE_BM - 1) // MOE_BM
    blk_start = jnp.cumsum(blocks_e) - blocks_e
    pos = (blk_start[eid] * MOE_BM + rank).astype(jnp.int32)
    tok = jnp.arange(a, dtype=jnp.int32) // MOE_TOP_K
    slot_tok = jnp.zeros((n_slots,), jnp.int32).at[pos].set(tok, unique_indices=True)
    groups_e = (blocks_e + MOE_GROUP_BLOCKS - 1) // MOE_GROUP_BLOCKS
    grp_end = jnp.cumsum(groups_e)
    grp_start = grp_end - groups_e
    gidx = jnp.arange(n_groups, dtype=jnp.int32)
    ge = jnp.minimum(jnp.searchsorted(grp_end, gidx, side="right"), N_EXPERTS - 1).astype(jnp.int32)
    local = gidx - grp_start[ge]
    active = gidx < grp_end[-1]
    gb0 = jnp.where(active, blk_start[ge] + local * MOE_GROUP_BLOCKS, 0).astype(jnp.int32)
    gnb = jnp.where(active, jnp.minimum(blocks_e[ge] - local * MOE_GROUP_BLOCKS, MOE_GROUP_BLOCKS),
                    0).astype(jnp.int32)
    last_e = ge[jnp.maximum(grp_end[-1] - 1, 0)]
    ge = jnp.where(active, ge, last_e).astype(jnp.int32)
    n_used = jnp.sum(blocks_e).astype(jnp.int32).reshape(1)
    return pos, slot_tok, ge, gb0, gnb, n_used, n_slots


def _layer(x, mem, positions, attn_norm_g, mem_norm_g, w_in, fox_forget_b, fox_q_norm_g,
           fox_k_norm_g, dsa_q_norm_g, dsa_k_norm_g, idx_k_norm_g, cross_q_norm_g, cross_k_norm_g,
           w_mem_kv, w_branch_fox, w_branch_dsa, w_branch_cross, w_out, ffn_norm_g,
           router_group_w, router_group_b, router_expert_w, router_expert_b,
           expert_w_gate, expert_w_up, expert_w_down):
    b, s, d = x.shape
    n = b * s
    n_mem = mem.shape[1]
    fw, dw, cw = FOX_HEADS * HEAD_DIM, DSA_HEADS * HEAD_DIM, CROSS_HEADS * HEAD_DIM
    iqw = IDX_HEADS * IDX_DIM
    tm = min(512, n)
    x2d = x.reshape(n, d)

    sizes = (fw, fw, fw, FOX_HEADS, dw, dw, dw, iqw, IDX_HEADS, IDX_DIM, cw, 3 * d)
    offs = [0]
    for sz in sizes:
        offs.append(offs[-1] + sz)
    (c_fq, c_fk, c_fv, c_fl, c_dq, c_dk, c_dv, c_iq, c_iw, c_ik, c_cq, c_g) = range(12)

    w_t = jnp.transpose(w_in, (2, 0, 1)).reshape(offs[-1], d)
    order = (c_dq, c_dk, c_fv, c_dv, c_iq, c_fq, c_fk, c_cq, c_g)
    row0, r = {}, 0
    for c in order:
        row0[c] = r
        r += sizes[c]
    w_all = jnp.concatenate([w_t[offs[c]:offs[c + 1]] for c in order], axis=0).astype(BF16)
    small_rows = [w_t[offs[c]:offs[c + 1]] for c in (c_ik, c_iw, c_fl)]
    small_rows.append(jnp.zeros((LANES - IDX_DIM - IDX_HEADS - FOX_HEADS, d), F32))
    w_small = jnp.concatenate(small_rows, axis=0).astype(BF16)
    q_unit = (HEAD_DIM ** -0.5) * LOG2E
    g_qk1 = jnp.concatenate([jnp.tile(fox_q_norm_g * q_unit, FOX_HEADS),
                             jnp.tile(fox_k_norm_g, FOX_HEADS),
                             jnp.tile(cross_q_norm_g * q_unit, CROSS_HEADS)])
    g_qk2 = jnp.concatenate([jnp.tile(dsa_q_norm_g * q_unit, DSA_HEADS),
                             jnp.tile(dsa_k_norm_g, DSA_HEADS)])
    g_ik = jnp.concatenate([idx_k_norm_g, jnp.zeros((LANES - IDX_DIM,), F32)])
    fb_lane = jnp.zeros((LANES,), F32).at[SMALL_F:SMALL_F + FOX_HEADS].set(fox_forget_b)

    pos_col = positions.reshape(n, 1).astype(F32)
    cd, sd, ci, si = _rope_tables(pos_col, tm)

    h = _rmsnorm(x2d, attn_norm_g, BF16, tm)
    tmp = min(PROJ_TM, n)
    qk1 = _proj(h, w_all, "norm", BF16, tm, 2 * fw + cw, gains=g_qk1, w_rows=(row0[c_fq], 2 * fw + cw))
    qk2 = _proj(h, w_all, "norm_rope", BF16, min(PROJ_ROPE_TM, n), 2 * dw, gains=g_qk2, tables=(cd, sd),
                w_rows=(row0[c_dq], 2 * dw))
    iq = _proj(h, w_all, "rope_idx", BF16, tmp, 1024, tables=(ci, si), w_rows=(row0[c_iq], iqw))
    small = _proj(h, w_small, "raw", F32, tm, LANES, w_rows=(0, LANES))
    ika, ikb, wl, lf_col = _small(small, g_ik, fb_lane, ci, si, tm)

    tq, tk = min(ATT_TQ, s), min(ATT_TK, s)
    assert tq == tk and s % tq == 0
    cum_col = _forget_cumsum(lf_col.reshape(b, s, LANES))
    qk1 = qk1.reshape(b, s, 2 * fw + cw)
    qk2 = qk2.reshape(b, s, 2 * dw)
    q_aug, k_aug = _fox_pack(qk1, cum_col, FOX_HEADS, min(512, s))
    vt = _proj_t(h, w_all, (row0[c_fv], fw + dw), tmp, 768, tk).reshape(b, s // tk, fw + dw, tk)
    o_fox = _attention(q_aug, 0, k_aug, 0, vt, 0, (), "fox", FOX_HEADS, AUG_DIM, tq, tk)

    topk = min(INDEX_TOPK, s // 4)
    wt = jnp.transpose(wl.reshape(b, s, LANES)[:, :, :IDX_HEADS], (0, 2, 1))
    bias = _select_bias(iq.reshape(b, s, iqw), ika.reshape(b, s, LANES), ikb.reshape(b, s, LANES),
                        wt, topk, tq, tk)
    o_dsa = _attention(qk2, 0, qk2, 1, vt, 1, (bias,), "dsa", DSA_HEADS, HEAD_DIM, tq, tk)

    tmm = min(512, b * n_mem)
    m_n = _rmsnorm(mem.reshape(b * n_mem, d), mem_norm_g, BF16, tmm)
    w_kv = w_mem_kv.astype(BF16)
    mk = _proj(m_n, w_kv[:, :cw], "norm", BF16, tmm, cw, gains=jnp.tile(cross_k_norm_g, CROSS_HEADS))
    mv = _proj(m_n, w_kv[:, cw:], "raw", BF16, tmm, cw)
    o_cross = _cross_attention(qk1, (2 * fw) // cw, mk.reshape(b, n_mem, cw), mv.reshape(b, n_mem, cw),
                               min(512, s))

    x_mid = _merge(h, o_fox.reshape(n, fw), o_dsa.reshape(n, dw), o_cross.reshape(n, cw), x2d,
                   w_all, row0[c_g], w_branch_fox.astype(BF16), w_branch_dsa.astype(BF16),
                   w_branch_cross.astype(BF16), w_out.astype(BF16), tm, MERGE_TN)

    w_router = jnp.concatenate(
        [router_group_w, router_expert_w, jnp.zeros((d, LANES - N_GROUPS - N_EXPERTS), F32)],
        axis=1).astype(BF16)
    b_router = jnp.concatenate(
        [router_group_b, router_expert_b, jnp.zeros((LANES - N_GROUPS - N_EXPERTS,), F32)])
    h2, eid, gate = _router(x_mid, ffn_norm_g, w_router, b_router, tm)
    pos, slot_tok, ge, gb0, gnb, n_used, n_slots = _moe_plan(eid[:, :MOE_TOP_K], n)
    ys = _experts(h2, slot_tok, ge, gb0, gnb, n_used, expert_w_gate, expert_w_up, expert_w_down,
                  n_slots)
    out = _combine(x_mid, gate, ys, pos, min(256, n))
    return out.reshape(b, s, d)


def kernel(x, mem, positions, attn_norm_g, mem_norm_g, w_in, fox_forget_b, fox_q_norm_g, fox_k_norm_g, dsa_q_norm_g, dsa_k_norm_g, idx_k_norm_g, cross_q_norm_g, cross_k_norm_g, w_mem_kv, w_branch_fox, w_branch_dsa, w_branch_cross, w_out, ffn_norm_g, router_group_w, router_group_b, router_expert_w, router_expert_b, expert_w_gate, expert_w_up, expert_w_down):
    depth = w_in.shape[0]
    for layer in range(depth):
        x = _layer(
            x, mem, positions, attn_norm_g[layer], mem_norm_g[layer], w_in[layer:layer + 1],
            fox_forget_b[layer], fox_q_norm_g[layer], fox_k_norm_g[layer],
            dsa_q_norm_g[layer], dsa_k_norm_g[layer], idx_k_norm_g[layer],
            cross_q_norm_g[layer], cross_k_norm_g[layer], w_mem_kv[layer],
            w_branch_fox[layer], w_branch_dsa[layer], w_branch_cross[layer], w_out[layer],
            ffn_norm_g[layer], router_group_w[layer], router_group_b[layer],
            router_expert_w[layer], router_expert_b[layer], expert_w_gate[layer],
            expert_w_up[layer], expert_w_down[layer])
    return x
```

```python
import functools

import jax
import jax.numpy as jnp
from jax import lax
from jax.experimental import pallas as pl
from jax.experimental.pallas import tpu as pltpu

F32 = jnp.float32
BF16 = jnp.bfloat16

LANES = 128
HEAD_DIM = 128
FOX_HEADS = 6
DSA_HEADS = 6
CROSS_HEADS = 4
IDX_HEADS = 16
IDX_DIM = 64
ROPE_THETA = 500000.0
ROT_DIM = HEAD_DIM // 4
IDX_ROT_DIM = IDX_DIM // 4
INDEX_TOPK = 256
N_GROUPS = 4
EXPERTS_PER_GROUP = 8
N_EXPERTS = N_GROUPS * EXPERTS_PER_GROUP
MOE_TOP_K = 2
EPS = 1e-6

NEG = -1e30
INT_MIN = -(2 ** 31)
VMEM_LIMIT = 56 * 1024 * 1024

ATT_TQ = 512
ATT_TK = 512
ATT_HEADS_PER_LOOP = 6
PROJ_TM = 1024
PROJ_ROPE_TM = 256
MERGE_TN = 512
MERGE_SUB = 256
MOE_BM = 128
MOE_GROUP_BLOCKS = 8
MOE_MM_ROWS = 256
MOE_FC = 512
DMA_ISSUE_UNROLL = 8


def _cparams(*sem):
    return pltpu.CompilerParams(dimension_semantics=sem, vmem_limit_bytes=VMEM_LIMIT)


def _rmsnorm_body(x_ref, g_ref, o_ref):
    x = x_ref[...]
    r = lax.rsqrt(jnp.mean(x * x, axis=-1, keepdims=True) + EPS)
    o_ref[...] = ((x * r) * g_ref[...]).astype(o_ref.dtype)


def _rmsnorm(x2d, g, out_dtype, tm):
    m, d = x2d.shape
    return pl.pallas_call(
        _rmsnorm_body,
        grid=(m // tm,),
        in_specs=[pl.BlockSpec((tm, d), lambda i: (i, 0)),
                  pl.BlockSpec((1, d), lambda i: (0, 0))],
        out_specs=pl.BlockSpec((tm, d), lambda i: (i, 0)),
        out_shape=jax.ShapeDtypeStruct((m, d), out_dtype),
        compiler_params=_cparams("parallel"),
        name="rmsnorm",
    )(x2d, g.reshape(1, d))


W_PREP_ROWS = 128


def _w_prep_body(src_ref, w_ref, o_ref):
    del src_ref
    o_ref[...] = w_ref[:, 0, :].astype(o_ref.dtype)


def _w_prep(w_t, src_rows):
    _, _, k = w_t.shape
    n_chunks = src_rows.shape[0]
    grid_spec = pltpu.PrefetchScalarGridSpec(
        num_scalar_prefetch=1,
        grid=(n_chunks,),
        in_specs=[pl.BlockSpec((pl.Element(W_PREP_ROWS), pl.Element(1), pl.Element(k)),
                               lambda c, src: (src[c], 0, 0))],
        out_specs=pl.BlockSpec((W_PREP_ROWS, k), lambda c, src: (c, 0)),
    )
    return pl.pallas_call(
        _w_prep_body,
        grid_spec=grid_spec,
        out_shape=jax.ShapeDtypeStruct((n_chunks * W_PREP_ROWS, k), BF16),
        compiler_params=_cparams("parallel"),
        name="w_prep",
    )(src_rows, w_t)


IDX_FREQ_LANE = 32


def _rope_table_body(pos_ref, f_ref, cd_ref, sd_ref, ci_ref, si_ref):
    ang = pos_ref[...] * f_ref[...]
    c = jnp.cos(ang)
    s = jnp.sin(ang)
    lane = lax.broadcasted_iota(jnp.int32, c.shape, 1)
    hd = ROT_DIM // 2
    cd_ref[...] = jnp.where(lane < hd, c, jnp.where(lane < ROT_DIM, pltpu.roll(c, hd, 1), 1.0))
    sd_ref[...] = jnp.where(lane < hd, -s, jnp.where(lane < ROT_DIM, pltpu.roll(s, hd, 1), 0.0))
    m = lane & (IDX_DIM - 1)
    hi = IDX_ROT_DIM // 2
    ci_src = pltpu.roll(c, LANES - IDX_FREQ_LANE, 1)
    si_src = pltpu.roll(s, LANES - IDX_FREQ_LANE, 1)
    ci_ref[...] = jnp.where(m < IDX_ROT_DIM, ci_src, 1.0)
    si_ref[...] = jnp.where(m < hi, -si_src, jnp.where(m < IDX_ROT_DIM, si_src, 0.0))


def _rope_tables(pos_col, tm):
    n = pos_col.shape[0]

    def inv_freq(rot_dim):
        half = rot_dim // 2
        return jnp.power(jnp.float32(ROPE_THETA), -jnp.arange(half, dtype=F32) * 2.0 / rot_dim)

    fi = jnp.tile(inv_freq(IDX_ROT_DIM), 2)
    freq = jnp.zeros((LANES,), F32).at[:ROT_DIM // 2].set(inv_freq(ROT_DIM))
    for base in (IDX_FREQ_LANE, IDX_FREQ_LANE + IDX_DIM):
        freq = freq.at[base:base + IDX_ROT_DIM].set(fi)
    tab = pl.BlockSpec((tm, LANES), lambda i: (i, 0))
    return pl.pallas_call(
        _rope_table_body,
        grid=(n // tm,),
        in_specs=[pl.BlockSpec((tm, 1), lambda i: (i, 0)), pl.BlockSpec((1, LANES), lambda i: (0, 0))],
        out_specs=[tab] * 4,
        out_shape=[jax.ShapeDtypeStruct((n, LANES), F32)] * 4,
        compiler_params=_cparams("parallel"),
        name="rope_tables",
    )(pos_col, freq.reshape(1, LANES))


def _apply_rope(a, c, s, half, period):
    lane = lax.broadcasted_iota(jnp.int32, a.shape, 1)
    first = (lane & (period - 1)) < half
    partner = jnp.where(first, pltpu.roll(a, LANES - half, 1), pltpu.roll(a, half, 1))
    return a * c + partner * s


def _proj_body(*refs, mode, n_lane_groups, w_rows):
    h_ref, w_ref = refs[0], refs[1]
    o_ref = refs[-1]

    if w_rows:
        acc = lax.dot_general(h_ref[...], w_ref[...], (((1,), (1,)), ((), ())),
                              preferred_element_type=F32)
    else:
        acc = jnp.dot(h_ref[...], w_ref[...], preferred_element_type=F32)
    if mode == "raw":
        o_ref[...] = acc.astype(o_ref.dtype)
        return
    for g in range(n_lane_groups):
        sl = slice(g * LANES, (g + 1) * LANES)
        a = acc[:, sl]
        if mode in ("norm", "norm_rope"):
            g_ref = refs[2]
            a = (a * lax.rsqrt(jnp.mean(a * a, axis=-1, keepdims=True) + EPS)) * g_ref[:, sl]
        if mode == "norm_rope":
            a = _apply_rope(a, refs[3][...], refs[4][...], ROT_DIM // 2, HEAD_DIM)
        if mode == "rope_idx":
            a = _apply_rope(a, refs[2][...], refs[3][...], IDX_ROT_DIM // 2, IDX_DIM)
        o_ref[:, sl] = a.astype(o_ref.dtype)


def _proj_t_body(h_ref, w_ref, o_ref, *, tk):
    acc = lax.dot_general(w_ref[...], h_ref[...], (((1,), (1,)), ((), ())),
                          preferred_element_type=F32)
    for q in range(o_ref.shape[0]):
        o_ref[q] = acc[:, q * tk:(q + 1) * tk].astype(o_ref.dtype)


def _proj_t(h, w, w_rows, tm, tn, tk):
    m, k = h.shape
    row0, n = w_rows
    assert row0 % tn == 0 and n % tn == 0 and tm % tk == 0
    return pl.pallas_call(
        functools.partial(_proj_t_body, tk=tk),
        grid=(n // tn, m // tm),
        in_specs=[pl.BlockSpec((tm, k), lambda j, i: (i, 0)),
                  pl.BlockSpec((tn, k), lambda j, i: (row0 // tn + j, 0))],
        out_specs=pl.BlockSpec((tm // tk, tn, tk), lambda j, i: (i, j, 0)),
        out_shape=jax.ShapeDtypeStruct((m // tk, n, tk), BF16),
        compiler_params=_cparams("parallel", "parallel"),
        name="proj_t",
    )(h, w)


def _proj(h, w, mode, out_dtype, tm, tn, gains=None, tables=None, w_rows=None):
    m, k = h.shape
    if w_rows is None:
        n = w.shape[1]
        w_spec = pl.BlockSpec((k, tn), lambda j, i: (0, j))
    else:
        row0, n = w_rows
        assert row0 % tn == 0 and n % tn == 0
        w_spec = pl.BlockSpec((tn, k), lambda j, i: (row0 // tn + j, 0))
    in_specs = [pl.BlockSpec((tm, k), lambda j, i: (i, 0)), w_spec]
    args = [h, w]
    if gains is not None:
        in_specs.append(pl.BlockSpec((1, tn), lambda j, i: (0, j)))
        args.append(gains.reshape(1, n))
    if tables is not None:
        for t in tables:
            in_specs.append(pl.BlockSpec((tm, LANES), lambda j, i: (i, 0)))
            args.append(t)
    return pl.pallas_call(
        functools.partial(_proj_body, mode=mode, n_lane_groups=tn // LANES,
                          w_rows=w_rows is not None),
        grid=(n // tn, m // tm),
        in_specs=in_specs,
        out_specs=pl.BlockSpec((tm, tn), lambda j, i: (i, j)),
        out_shape=jax.ShapeDtypeStruct((m, n), out_dtype),
        compiler_params=_cparams("parallel", "parallel"),
        name="proj_" + mode,
    )(*args)


SMALL_IK = 0
SMALL_IW = 64
SMALL_F = 80


def _small_body(x_ref, gk_ref, fb_ref, c_ref, s_ref, ika_ref, ikb_ref, wl_ref, lf_ref):
    x = x_ref[...]
    lane = lax.broadcasted_iota(jnp.int32, x.shape, 1)
    is_ik = lane < IDX_DIM
    ik = jnp.where(is_ik, x, 0.0)
    ms = jnp.sum(ik * ik, axis=-1, keepdims=True) * (1.0 / IDX_DIM)
    ik = (ik * lax.rsqrt(ms + EPS)) * gk_ref[...]
    ik = _apply_rope(ik, c_ref[...], s_ref[...], IDX_ROT_DIM // 2, IDX_DIM)
    ik = jnp.where(is_ik, ik, 0.0)
    ika_ref[...] = ik.astype(BF16)
    ikb_ref[...] = pltpu.roll(ik, IDX_DIM, 1).astype(BF16)
    wl_ref[...] = pltpu.roll(x, LANES - SMALL_IW, 1) * (2.0 ** -5)
    z = x + fb_ref[...]
    log_f = jnp.minimum(z, 0.0) - jnp.log1p(jnp.exp(-jnp.abs(z)))
    lf_ref[...] = pltpu.roll(log_f, LANES - SMALL_F, 1)


def _small(x, idx_k_gain_lane, forget_b_lane, ci, si, tm):
    n = x.shape[0]
    row = pl.BlockSpec((1, LANES), lambda i: (0, 0))
    tab = pl.BlockSpec((tm, LANES), lambda i: (i, 0))
    return pl.pallas_call(
        _small_body,
        grid=(n // tm,),
        in_specs=[tab, row, row, tab, tab],
        out_specs=[tab, tab, tab, tab],
        out_shape=[jax.ShapeDtypeStruct((n, LANES), BF16), jax.ShapeDtypeStruct((n, LANES), BF16),
                   jax.ShapeDtypeStruct((n, LANES), F32), jax.ShapeDtypeStruct((n, LANES), F32)],
        compiler_params=_cparams("parallel"),
        name="small_cols",
    )(x, idx_k_gain_lane.reshape(1, LANES), forget_b_lane.reshape(1, LANES), ci, si)


def _cumsum_body(col_ref, ccol_ref, *, seq):
    c = col_ref[...]
    ci = lax.broadcasted_iota(jnp.int32, c.shape, 0)
    sh = 1
    while sh < seq:
        c = c + jnp.where(ci >= sh, pltpu.roll(c, sh, 0), 0.0)
        sh *= 2
    ccol_ref[...] = c


def _forget_cumsum(lf_col):
    b, s, _ = lf_col.shape
    return pl.pallas_call(
        functools.partial(_cumsum_body, seq=s),
        grid=(b,),
        in_specs=[pl.BlockSpec((None, s, LANES), lambda i: (i, 0, 0))],
        out_specs=pl.BlockSpec((None, s, LANES), lambda i: (i, 0, 0)),
        out_shape=jax.ShapeDtypeStruct((b, s, LANES), F32),
        compiler_params=_cparams("parallel"),
        name="forget_cumsum",
    )(lf_col)


AUG_DIM = 2 * HEAD_DIM
LOG2E = 1.4426950408889634


def _split3(x):
    hi = x.astype(BF16).astype(F32)
    r1 = x - hi
    mid = r1.astype(BF16).astype(F32)
    lo = (r1 - mid).astype(BF16).astype(F32)
    return hi, mid, lo


def _fox_pack_body(q_ref, k_ref, cum_ref, qa_ref, ka_ref, *, n_heads):
    rows = q_ref.shape[0]
    lane = lax.broadcasted_iota(jnp.int32, (rows, LANES), 1)
    for h in range(n_heads):
        c = jnp.broadcast_to(cum_ref[:, h:h + 1], (rows, LANES)) * LOG2E
        qh, qm, ql = _split3(c)
        kh, km, kl = _split3(-c)
        q_aug = jnp.where(lane == 0, qh, jnp.where(lane == 1, qm, jnp.where(
            lane == 2, ql, jnp.where(lane < 6, 1.0, 0.0))))
        k_aug = jnp.where(lane < 3, 1.0, jnp.where(lane == 3, kh, jnp.where(
            lane == 4, km, jnp.where(lane == 5, kl, 0.0))))
        base = h * AUG_DIM
        qa_ref[:, base:base + HEAD_DIM] = q_ref[:, h * HEAD_DIM:(h + 1) * HEAD_DIM]
        ka_ref[:, base:base + HEAD_DIM] = k_ref[:, h * HEAD_DIM:(h + 1) * HEAD_DIM]
        qa_ref[:, base + HEAD_DIM:base + AUG_DIM] = q_aug.astype(BF16)
        ka_ref[:, base + HEAD_DIM:base + AUG_DIM] = k_aug.astype(BF16)


def _fox_pack(qk, cum_col, n_heads, ts):
    b, s, _ = qk.shape
    w = n_heads * HEAD_DIM
    wa = n_heads * AUG_DIM
    return pl.pallas_call(
        functools.partial(_fox_pack_body, n_heads=n_heads),
        grid=(b, s // ts),
        in_specs=[pl.BlockSpec((None, ts, w), lambda bi, i: (bi, i, 0)),
                  pl.BlockSpec((None, ts, w), lambda bi, i: (bi, i, 1)),
                  pl.BlockSpec((None, ts, LANES), lambda bi, i: (bi, i, 0))],
        out_specs=[pl.BlockSpec((None, ts, wa), lambda bi, i: (bi, i, 0)),
                   pl.BlockSpec((None, ts, wa), lambda bi, i: (bi, i, 0))],
        out_shape=[jax.ShapeDtypeStruct((b, s, wa), BF16)] * 2,
        compiler_params=_cparams("parallel", "parallel"),
        name="fox_pack",
    )(qk, qk, cum_col)


def _attn_body(q_ref, k_ref, vt_ref, *rest, tq, tk, n_heads, kdim, mode, heads_per_loop):
    o_ref = rest[-1]
    qi = pl.program_id(1)
    key0 = lax.broadcasted_iota(jnp.int32, (tk, tq), 0)
    qry = qi * tq + lax.broadcasted_iota(jnp.int32, (tk, tq), 1)
    def scores(h, j):
        off = pl.multiple_of(j * tk, tk)
        q = q_ref[:, h * kdim:(h + 1) * kdim]
        k = k_ref[pl.ds(off, tk), h * kdim:(h + 1) * kdim]
        return lax.dot_general(k, q, (((1,), (1,)), ((), ())), preferred_element_type=F32)

    def flush(h, j, p, alpha, acc):
        vt = vt_ref[j, h * HEAD_DIM:(h + 1) * HEAD_DIM, :]
        return alpha * acc + jnp.dot(vt, p, preferred_element_type=F32)

    def softmax_step(h, j, m, l, masked):
        s = scores(h, j)
        if mode == "dsa":
            s = s + rest[0][j]
        if masked:
            s = jnp.where(key0 + j * tk <= qry, s, NEG)
        m_new = jnp.maximum(m, jnp.max(s, axis=0, keepdims=True))
        alpha = jnp.exp2(m - m_new)
        p = jnp.exp2(s - m_new)
        l = alpha * l + jnp.sum(p, axis=0, keepdims=True)
        return p.astype(BF16), alpha, m_new, l

    last = qi
    for h0 in range(0, n_heads, heads_per_loop):
        heads = range(h0, h0 + heads_per_loop)

        def step(j, carries, masked=False, heads=heads):
            prev = jnp.maximum(j - 1, 0)
            out = []
            for h, (p, alpha, m, l, acc) in zip(heads, carries):
                acc = flush(h, prev, p, alpha, acc)
                p, alpha, m, l = softmax_step(h, j, m, l, masked)
                out.append((p, alpha, m, l, acc))
            return tuple(out)

        carries = tuple(
            (jnp.zeros((tk, tq), BF16), jnp.ones((1, tq), F32), jnp.full((1, tq), NEG, F32),
             jnp.zeros((1, tq), F32), jnp.zeros((HEAD_DIM, tq), F32)) for h in heads)
        n_loop = last if mode == "fox" else last + 1
        carries = lax.fori_loop(0, n_loop, step, carries)
        if mode == "fox":
            carries = step(last, carries, masked=True)
        for h, (p, alpha, m, l, acc) in zip(heads, carries):
            acc = flush(h, last, p, alpha, acc)
            o_ref[:, h * HEAD_DIM:(h + 1) * HEAD_DIM] = jnp.transpose(acc / l).astype(o_ref.dtype)


def _attention(q_arr, q_blk, k_arr, k_blk, vt_arr, v_blk, extras, mode, n_heads, kdim, tq, tk):
    b, s, _ = q_arr.shape
    w = n_heads * HEAD_DIM
    wk = n_heads * kdim
    in_specs = [pl.BlockSpec((None, tq, wk), lambda bi, qi: (bi, qi, q_blk)),
                pl.BlockSpec((None, s, wk), lambda bi, qi: (bi, 0, k_blk)),
                pl.BlockSpec((None, s // tk, w, tk), lambda bi, qi: (bi, 0, v_blk, 0))]
    if mode == "dsa":
        in_specs += [pl.BlockSpec((None, None, s // tk, tk, tq), lambda bi, qi: (bi, qi, 0, 0, 0))]
    return pl.pallas_call(
        functools.partial(_attn_body, tq=tq, tk=tk, n_heads=n_heads, kdim=kdim, mode=mode,
                          heads_per_loop=ATT_HEADS_PER_LOOP),
        grid=(b, s // tq),
        in_specs=in_specs,
        out_specs=pl.BlockSpec((None, tq, w), lambda bi, qi: (bi, qi, 0)),
        out_shape=jax.ShapeDtypeStruct((b, s, w), BF16),
        compiler_params=_cparams("parallel", "parallel"),
        name="attn_" + mode,
    )(q_arr, k_arr, vt_arr, *extras)


def _select_body(iq_ref, ika_ref, ikb_ref, wt_ref, o_ref, keys_ref, half_ref, *, tq, tk, n_chunks, topk):
    qi = pl.program_id(1)
    used = qi + 1
    key0 = lax.broadcasted_iota(jnp.int32, (tk, tq), 0)
    qry = qi * tq + lax.broadcasted_iota(jnp.int32, (tk, tq), 1)

    def score_chunk(c, _):
        off = pl.multiple_of(c * tk, tk)
        ka = ika_ref[pl.ds(off, tk), :]
        kb = ikb_ref[pl.ds(off, tk), :]
        acc = jnp.zeros((tk, tq), F32)
        for j in range(IDX_HEADS):
            pair = iq_ref[:, (j // 2) * LANES:(j // 2 + 1) * LANES]
            logit = lax.dot_general(ka if j % 2 == 0 else kb, pair, (((1,), (1,)), ((), ())),
                                    preferred_element_type=F32)
            acc = acc + wt_ref[j:j + 1, :] * jnp.maximum(logit, 0.0)
        bits = lax.bitcast_convert_type(acc, jnp.int32)
        key = jnp.where(bits < 0, bits ^ jnp.int32(0x7FFFFFFF), bits)
        key = jnp.where(key0 + off <= qry, key, jnp.int32(INT_MIN))
        keys_ref[c] = key
        half_ref[c] = lax.shift_right_arithmetic(key, 16).astype(jnp.int16)
        return 0

    lax.fori_loop(0, used, score_chunk, 0)

    one16 = jnp.int16(1)
    zero16 = jnp.int16(0)

    def count_ge(cand16):
        def count_chunk(c, cnt):
            partial = [None] * 4
            for i, r in enumerate(range(0, tk, 16)):
                ge = jnp.where(half_ref[c, r:r + 16, :] >= cand16, one16, zero16)
                partial[i % 4] = ge if partial[i % 4] is None else partial[i % 4] + ge
            return cnt + ((partial[0] + partial[1]) + (partial[2] + partial[3]))

        cnt = lax.fori_loop(0, used, count_chunk, jnp.zeros((16, tq), jnp.int16))
        return jnp.sum(cnt.astype(jnp.int32), axis=0, keepdims=True)

    def search16(need):
        def bit_step(i, carry):
            prefix, kept = carry
            cand_u = prefix | jnp.left_shift(jnp.int32(1), 15 - i)
            cnt = count_ge((cand_u - 32768).astype(jnp.int16))
            ok = cnt >= need
            return jnp.where(ok, cand_u, prefix), jnp.where(ok, cnt, kept)

        init = (jnp.zeros((1, tq), jnp.int32), jnp.full((1, tq), tk, jnp.int32) * used)
        prefix, kept = lax.fori_loop(0, 16, bit_step, init)
        return prefix - 32768, kept

    hi, _ = search16(topk)
    above = count_ge(jnp.minimum(hi + 1, 32767).astype(jnp.int16))
    above = jnp.where(hi >= 32767, 0, above)
    hi16 = hi.astype(jnp.int16)

    def low_chunk(c, _):
        low = (keys_ref[c] & jnp.int32(0xFFFF)) - 32768
        half_ref[c] = jnp.where(half_ref[c] == hi16, low.astype(jnp.int16), jnp.int16(-32768))
        return 0

    lax.fori_loop(0, used, low_chunk, 0)
    lo, at_or_above_lo = search16(topk - above)
    thr = lax.shift_left(hi, 16) | (lo + 32768)
    thr = jnp.maximum(thr, jnp.int32(INT_MIN + 1))
    tied = jnp.max(above + at_or_above_lo) > topk

    def count_keys(pred):
        def chunk(c, cnt):
            hit = jnp.where(pred(keys_ref[c]), 1.0, 0.0)
            return cnt + jnp.sum(hit.reshape(tk // 8, 8, tq), axis=0)

        cnt = lax.fori_loop(0, used, chunk, jnp.zeros((8, tq), F32))
        return jnp.sum(cnt, axis=0, keepdims=True)

    @pl.when(jnp.logical_not(tied))
    def _():
        def write_chunk(c, _):
            o_ref[c] = jnp.where(keys_ref[c] >= thr, 0.0, NEG)
            return 0

        lax.fori_loop(0, used, write_chunk, 0)

    @pl.when(tied)
    def _():
        need = topk - count_keys(lambda k: k > thr)
        earlier = jnp.where(lax.broadcasted_iota(jnp.int32, (tk, tk), 1)
                            < lax.broadcasted_iota(jnp.int32, (tk, tk), 0), 1.0, 0.0).astype(BF16)

        def write_chunk(c, seen):
            keys = keys_ref[c]
            tie = jnp.where(keys == thr, 1.0, 0.0)
            rank = jnp.dot(earlier, tie.astype(BF16), preferred_element_type=F32) + seen
            o_ref[c] = jnp.where(keys > thr, 0.0,
                                 jnp.where(keys == thr, jnp.where(rank < need, 0.0, NEG), NEG))
            return seen + jnp.sum(tie, axis=0, keepdims=True)

        lax.fori_loop(0, used, write_chunk, jnp.zeros((1, tq), F32))

    def fill_chunk(c, _):
        o_ref[c] = jnp.full((tk, tq), NEG, F32)
        return 0

    lax.fori_loop(used, n_chunks, fill_chunk, 0)


def _select_bias(iq, ika, ikb, wt, topk, tq, tk):
    b, s, _ = iq.shape
    n_chunks = s // tk
    return pl.pallas_call(
        functools.partial(_select_body, tq=tq, tk=tk, n_chunks=n_chunks, topk=topk),
        grid=(b, s // tq),
        in_specs=[pl.BlockSpec((None, tq, IDX_HEADS * IDX_DIM), lambda bi, qi: (bi, qi, 0)),
                  pl.BlockSpec((None, s, LANES), lambda bi, qi: (bi, 0, 0)),
                  pl.BlockSpec((None, s, LANES), lambda bi, qi: (bi, 0, 0)),
                  pl.BlockSpec((None, IDX_HEADS, tq), lambda bi, qi: (bi, 0, qi))],
        out_specs=pl.BlockSpec((None, None, n_chunks, tk, tq), lambda bi, qi: (bi, qi, 0, 0, 0)),
        out_shape=jax.ShapeDtypeStruct((b, s // tq, n_chunks, tk, tq), F32),
        scratch_shapes=[pltpu.VMEM((n_chunks, tk, tq), jnp.int32),
                        pltpu.VMEM((n_chunks, tk, tq), jnp.int16)],
        compiler_params=_cparams("parallel", "parallel"),
        name="index_select",
    )(iq, ika, ikb, wt)


def _cross_body(q_ref, k_ref, v_ref, o_ref, *, n_heads):
    for h in range(n_heads):
        hs = slice(h * HEAD_DIM, (h + 1) * HEAD_DIM)
        s = lax.dot_general(q_ref[:, hs], k_ref[:, hs], (((1,), (1,)), ((), ())),
                            preferred_element_type=F32)
        p = jnp.exp2(s - jnp.max(s, axis=-1, keepdims=True))
        l = jnp.sum(p, axis=-1, keepdims=True)
        o = jnp.dot(p.astype(BF16), v_ref[:, hs], preferred_element_type=F32)
        o_ref[:, hs] = (o / l).astype(o_ref.dtype)


def _cross_attention(q_arr, q_blk, mk, mv, tq):
    b, s, _ = q_arr.shape
    n_mem = mk.shape[1]
    w = CROSS_HEADS * HEAD_DIM
    return pl.pallas_call(
        functools.partial(_cross_body, n_heads=CROSS_HEADS),
        grid=(b, s // tq),
        in_specs=[pl.BlockSpec((None, tq, w), lambda bi, qi: (bi, qi, q_blk)),
                  pl.BlockSpec((None, n_mem, w), lambda bi, qi: (bi, 0, 0)),
                  pl.BlockSpec((None, n_mem, w), lambda bi, qi: (bi, 0, 0))],
        out_specs=pl.BlockSpec((None, tq, w), lambda bi, qi: (bi, qi, 0)),
        out_shape=jax.ShapeDtypeStruct((b, s, w), BF16),
        compiler_params=_cparams("parallel", "parallel"),
        name="cross_attn",
    )(q_arr, mk, mv)


def _sigmoid(x):
    return 1.0 / (1.0 + jnp.exp(-x))


def _merge_body(h_ref, of_ref, od_ref, oc_ref, x_ref, wgf_ref, wgd_ref, wgc_ref,
                wbf_ref, wbd_ref, wbc_ref, wo_ref, o_ref):
    j = pl.program_id(1)
    h = h_ref[...]
    tn = wo_ref.shape[0]
    sub = min(tn, MERGE_SUB)

    def gate_logits(wg_ref, cs):
        return lax.dot_general(h, wg_ref[cs, :], (((1,), (1,)), ((), ())), preferred_element_type=F32)

    part = None
    for c in range(tn // sub):
        cs = slice(c * sub, (c + 1) * sub)
        zf, zd, zc = gate_logits(wgf_ref, cs), gate_logits(wgd_ref, cs), gate_logits(wgc_ref, cs)
        bf = jnp.dot(of_ref[...], wbf_ref[:, cs], preferred_element_type=F32)
        bd = jnp.dot(od_ref[...], wbd_ref[:, cs], preferred_element_type=F32)
        bc = jnp.dot(oc_ref[...], wbc_ref[:, cs], preferred_element_type=F32)
        merged = (_sigmoid(zf) * bf + _sigmoid(zd) * bd) + _sigmoid(zc) * bc
        p = jnp.dot(merged.astype(BF16), wo_ref[cs, :], preferred_element_type=F32)
        part = p if part is None else part + p

    @pl.when(j == 0)
    def _():
        o_ref[...] = x_ref[...] + part

    @pl.when(j != 0)
    def _():
        o_ref[...] += part


def _merge(h, o_fox, o_dsa, o_cross, x, w_gates, gate_row0, wbf, wbd, wbc, w_out, tm, tn):
    m, d = x.shape
    nj = d // tn
    assert gate_row0 % tn == 0
    g0 = gate_row0 // tn
    row = lambda i, j: (i, 0)
    in_specs = [pl.BlockSpec((tm, d), row),
                pl.BlockSpec((tm, o_fox.shape[1]), row),
                pl.BlockSpec((tm, o_dsa.shape[1]), row),
                pl.BlockSpec((tm, o_cross.shape[1]), row),
                pl.BlockSpec((tm, d), row),
                pl.BlockSpec((tn, d), lambda i, j: (g0 + j, 0)),
                pl.BlockSpec((tn, d), lambda i, j: (g0 + nj + j, 0)),
                pl.BlockSpec((tn, d), lambda i, j: (g0 + 2 * nj + j, 0)),
                pl.BlockSpec((wbf.shape[0], tn), lambda i, j: (0, j)),
                pl.BlockSpec((wbd.shape[0], tn), lambda i, j: (0, j)),
                pl.BlockSpec((wbc.shape[0], tn), lambda i, j: (0, j)),
                pl.BlockSpec((tn, d), lambda i, j: (j, 0))]
    return pl.pallas_call(
        _merge_body,
        grid=(m // tm, nj),
        in_specs=in_specs,
        out_specs=pl.BlockSpec((tm, d), row),
        out_shape=jax.ShapeDtypeStruct((m, d), F32),
        compiler_params=_cparams("parallel", "arbitrary"),
        name="gated_merge",
    )(h, o_fox, o_dsa, o_cross, x, w_gates, w_gates, w_gates, wbf, wbd, wbc, w_out)


def _router_body(x_ref, g_ref, w_ref, b_ref, h_ref, eid_ref, gate_ref):
    x = x_ref[...]
    r = lax.rsqrt(jnp.mean(x * x, axis=-1, keepdims=True) + EPS)
    h = (x * r) * g_ref[...]
    hb = h.astype(BF16)
    bits = lax.bitcast_convert_type(hb.astype(F32), jnp.int32)
    half = bits.shape[1] // 2
    h_ref[...] = lax.shift_right_logical(bits[:, :half], 16) | (bits[:, half:] & jnp.int32(-65536))
    logits = jnp.dot(hb, w_ref[...], preferred_element_type=F32) + b_ref[...]
    lane = lax.broadcasted_iota(jnp.int32, logits.shape, 1).astype(F32)

    def masked_max(mask):
        return jnp.max(jnp.where(mask, logits, -jnp.inf), axis=-1, keepdims=True)

    def first_lane(mask):
        return jnp.min(jnp.where(mask, lane, float(LANES)), axis=-1, keepdims=True)

    is_group = lane < N_GROUPS
    gmax = masked_max(is_group)
    grp = first_lane(is_group & (logits == gmax))
    denom = jnp.sum(jnp.where(is_group, jnp.exp(logits - gmax), 0.0), axis=-1, keepdims=True)
    p_sel = 1.0 / denom
    lo = N_GROUPS + grp * EXPERTS_PER_GROUP
    in_grp = (lane >= lo) & (lane < lo + EXPERTS_PER_GROUP)
    e1 = masked_max(in_grp)
    j1 = first_lane(in_grp & (logits == e1))
    rest = in_grp & (lane != j1)
    e2 = masked_max(rest)
    j2 = first_lane(rest & (logits == e2))
    t = jnp.exp(e2 - e1)
    g1 = p_sel * (1.0 / (1.0 + t))
    g2 = p_sel * (t / (1.0 + t))
    eid = jnp.where(lane == 0.0, j1 - N_GROUPS, jnp.where(lane == 1.0, j2 - N_GROUPS, 0.0))
    eid_ref[...] = eid.astype(jnp.int32)
    gate_ref[...] = jnp.where(lane == 0.0, g1, jnp.where(lane == 1.0, g2, 0.0))


def _router(x, g, w_router, b_router, tm):
    m, d = x.shape
    return pl.pallas_call(
        _router_body,
        grid=(m // tm,),
        in_specs=[pl.BlockSpec((tm, d), lambda i: (i, 0)),
                  pl.BlockSpec((1, d), lambda i: (0, 0)),
                  pl.BlockSpec((d, LANES), lambda i: (0, 0)),
                  pl.BlockSpec((1, LANES), lambda i: (0, 0))],
        out_specs=[pl.BlockSpec((tm, d // 2), lambda i: (i, 0)),
                   pl.BlockSpec((tm, LANES), lambda i: (i, 0)),
                   pl.BlockSpec((tm, LANES), lambda i: (i, 0))],
        out_shape=[jax.ShapeDtypeStruct((m, d // 2), jnp.int32),
                   jax.ShapeDtypeStruct((m, LANES), jnp.int32),
                   jax.ShapeDtypeStruct((m, LANES), F32)],
        compiler_params=_cparams("parallel"),
        name="moe_router",
    )(x, g.reshape(1, d), w_router, b_router.reshape(1, LANES))


def _expert_body(ge_ref, gb0_ref, gnb_ref, used_ref, tok_ref, h_hbm, wg_ref, wu_ref, wd_ref, ys_hbm,
                 xs_ref, xb_ref, acc_ref, wgb_ref, wub_ref, wdb_ref, sem_in, sem_out,
                 *, bm, n_f, n_blocks):
    g = pl.program_id(0)
    f = pl.program_id(1)
    nb = gnb_ref[g]
    blk0 = gb0_ref[g]

    @pl.when(jnp.logical_and(g == 0, f == 0))
    def _():
        acc_ref[pl.ds(0, bm)] = jnp.zeros((bm,) + acc_ref.shape[1:], F32)

        def tail_copy(i):
            return pltpu.make_async_copy(acc_ref.at[pl.ds(0, bm)], ys_hbm.at[pl.ds(i * bm, bm)],
                                         sem_out)

        def start_tail(i, _):
            tail_copy(i).start()
            return 0

        lax.fori_loop(used_ref[0], n_blocks, start_tail, 0)

        def wait_tail(i, _):
            tail_copy(i).wait()
            return 0

        lax.fori_loop(used_ref[0], n_blocks, wait_tail, 0)

    def gather(grp):
        slot = grp % 2
        base = gb0_ref[grp] * bm

        def issue(r0, _):
            for u in range(DMA_ISSUE_UNROLL):
                r = r0 * DMA_ISSUE_UNROLL + u
                pltpu.make_async_copy(h_hbm.at[pl.ds(tok_ref[base + r], 1)],
                                      xs_ref.at[slot, pl.ds(r, 1)], sem_in.at[slot]).start()
            return 0

        lax.fori_loop(0, gnb_ref[grp] * (bm // DMA_ISSUE_UNROLL), issue, 0)

    def block_copy_out(i):
        return pltpu.make_async_copy(acc_ref.at[pl.ds(i * bm, bm)],
                                     ys_hbm.at[pl.ds((blk0 + i) * bm, bm)], sem_out)

    @pl.when(jnp.logical_and(g == 0, f == 0))
    def _():
        gather(g)

    @pl.when(jnp.logical_and(nb > 0, f == 0))
    def _():
        slot = g % 2

        def land(i, _):
            rows = pl.ds(pl.multiple_of(i * bm, bm), bm)
            pltpu.make_async_copy(h_hbm.at[pl.ds(0, bm)], xs_ref.at[slot, rows], sem_in.at[slot]).wait()
            return 0

        lax.fori_loop(0, nb, land, 0)

        half = xb_ref.shape[1] // 2

        def unpack(i, _):
            rows = pl.ds(pl.multiple_of(i * bm, bm), bm)
            w = xs_ref[slot, rows, :]
            lo = lax.bitcast_convert_type(lax.shift_left(w, 16), F32)
            hi = lax.bitcast_convert_type(w & jnp.int32(-65536), F32)
            xb_ref[rows, :half] = lo.astype(BF16)
            xb_ref[rows, half:] = hi.astype(BF16)
            return 0

        lax.fori_loop(0, nb, unpack, 0)

    @pl.when(jnp.logical_and(f == 0, g + 1 < pl.num_programs(0)))
    def _():
        gather(g + 1)

    def wait_out(n):
        def body(i, _):
            pltpu.make_async_copy(acc_ref.at[pl.ds(0, bm)], ys_hbm.at[pl.ds(0, bm)], sem_out).wait()
            return 0

        lax.fori_loop(0, n, body, 0)

    @pl.when(jnp.logical_and(f == 0, g > 0))
    def _():
        wait_out(gnb_ref[jnp.maximum(g - 1, 0)])

    @pl.when(nb > 0)
    def _():
        wgb_ref[...] = wg_ref[...].astype(BF16)
        wub_ref[...] = wu_ref[...].astype(BF16)
        wdb_ref[...] = wd_ref[...].astype(BF16)

        def swiglu(rows):
            xb = xb_ref[rows, :]
            a = jnp.dot(xb, wgb_ref[...], preferred_element_type=F32)
            u = jnp.dot(xb, wub_ref[...], preferred_element_type=F32)
            hid = (a * _sigmoid(a)) * u
            y = jnp.dot(hid.astype(BF16), wdb_ref[...], preferred_element_type=F32)

            @pl.when(f == 0)
            def _():
                acc_ref[rows, :] = y

            @pl.when(f != 0)
            def _():
                acc_ref[rows, :] += y

        per = MOE_MM_ROWS // bm

        def full_tile(i, _):
            swiglu(pl.ds(pl.multiple_of(i * MOE_MM_ROWS, MOE_MM_ROWS), MOE_MM_ROWS))
            return 0

        lax.fori_loop(0, nb // per, full_tile, 0)
        piece = per // 2
        while piece >= 1:
            done = (nb // (2 * piece)) * (2 * piece)

            @pl.when((nb // piece) % 2 == 1)
            def _(done=done, piece=piece):
                swiglu(pl.ds(pl.multiple_of(done * bm, piece * bm), piece * bm))

            piece //= 2

    @pl.when(jnp.logical_and(nb > 0, f == n_f - 1))
    def _():
        def start_out(i, _):
            block_copy_out(i).start()
            return 0

        lax.fori_loop(0, nb, start_out, 0)

        @pl.when(g == pl.num_programs(0) - 1)
        def _():
            wait_out(nb)


def _experts(h2, slot_tok, grp_e, grp_blk0, grp_nb, n_used, w_gate, w_up, w_down, n_slots):
    d = w_gate.shape[1]
    ff = w_gate.shape[2]
    n_f = ff // MOE_FC
    n_groups = grp_e.shape[0]
    rmax = MOE_GROUP_BLOCKS * MOE_BM

    def f_idx(g, f, gnb):
        return jnp.where(gnb[g] > 0, f, n_f - 1)

    grid_spec = pltpu.PrefetchScalarGridSpec(
        num_scalar_prefetch=5,
        grid=(n_groups, n_f),
        in_specs=[
            pl.BlockSpec(memory_space=pl.ANY),
            pl.BlockSpec((None, d, MOE_FC), lambda g, f, ge, gb0, gnb, *_: (ge[g], 0, f_idx(g, f, gnb))),
            pl.BlockSpec((None, d, MOE_FC), lambda g, f, ge, gb0, gnb, *_: (ge[g], 0, f_idx(g, f, gnb))),
            pl.BlockSpec((None, MOE_FC, d), lambda g, f, ge, gb0, gnb, *_: (ge[g], f_idx(g, f, gnb), 0)),
        ],
        out_specs=pl.BlockSpec(memory_space=pl.ANY),
        scratch_shapes=[pltpu.VMEM((2, rmax, d // 2), jnp.int32),
                        pltpu.VMEM((rmax, d), BF16),
                        pltpu.VMEM((rmax, d), F32),
                        pltpu.VMEM((d, MOE_FC), BF16),
                        pltpu.VMEM((d, MOE_FC), BF16),
                        pltpu.VMEM((MOE_FC, d), BF16),
                        pltpu.SemaphoreType.DMA((2,)),
                        pltpu.SemaphoreType.DMA(())],
    )
    return pl.pallas_call(
        functools.partial(_expert_body, bm=MOE_BM, n_f=n_f, n_blocks=n_slots // MOE_BM),
        grid_spec=grid_spec,
        out_shape=jax.ShapeDtypeStruct((n_slots, d), F32),
        compiler_params=_cparams("arbitrary", "arbitrary"),
        name="moe_experts",
    )(grp_e, grp_blk0, grp_nb, n_used, slot_tok, h2, w_gate, w_up, w_down)


def _combine_body(pos_ref, x_ref, gate_ref, ys_hbm, o_ref, buf_ref, sem, *, tm):
    i = pl.program_id(0)
    n_steps = pl.num_programs(0)

    def issue(step):
        slot = step % 2

        def body(r0, _):
            for u in range(DMA_ISSUE_UNROLL):
                r = r0 * DMA_ISSUE_UNROLL + u
                for k in range(MOE_TOP_K):
                    src = pos_ref[(step * tm + r) * MOE_TOP_K + k]
                    pltpu.make_async_copy(ys_hbm.at[pl.ds(src, 1)], buf_ref.at[slot, k, pl.ds(r, 1)],
                                          sem.at[slot]).start()
            return 0

        lax.fori_loop(0, tm // DMA_ISSUE_UNROLL, body, 0)

    @pl.when(i == 0)
    def _():
        issue(i)

    @pl.when(i + 1 < n_steps)
    def _():
        issue(i + 1)

    slot = i % 2
    for k in range(MOE_TOP_K):
        pltpu.make_async_copy(ys_hbm.at[pl.ds(0, tm)], buf_ref.at[slot, k], sem.at[slot]).wait()
    o_ref[...] = x_ref[...] + (buf_ref[slot, 0] * gate_ref[:, 0:1] + buf_ref[slot, 1] * gate_ref[:, 1:2])


def _combine(x, gate, ys, pos, tm):
    m, d = x.shape
    grid_spec = pltpu.PrefetchScalarGridSpec(
        num_scalar_prefetch=1,
        grid=(m // tm,),
        in_specs=[pl.BlockSpec((tm, d), lambda i, pos: (i, 0)),
                  pl.BlockSpec((tm, LANES), lambda i, pos: (i, 0)),
                  pl.BlockSpec(memory_space=pl.ANY)],
        out_specs=pl.BlockSpec((tm, d), lambda i, pos: (i, 0)),
        scratch_shapes=[pltpu.VMEM((2, MOE_TOP_K, tm, d), F32), pltpu.SemaphoreType.DMA((2,))],
    )
    return pl.pallas_call(
        functools.partial(_combine_body, tm=tm),
        grid_spec=grid_spec,
        out_shape=jax.ShapeDtypeStruct((m, d), F32),
        compiler_params=_cparams("arbitrary"),
        name="moe_combine",
    )(pos, x, gate, ys)


PLAN_CHUNK = 256


def _positions_body(eid_ref, pos_ref, cnt_ref, prefix_ref, *, n_tokens):
    ch = PLAN_CHUNK
    lane = lax.broadcasted_iota(jnp.int32, (ch, LANES), 1)
    earlier = jnp.where(lax.broadcasted_iota(jnp.int32, (ch, ch), 1)
                        < lax.broadcasted_iota(jnp.int32, (ch, ch), 0), 1.0, 0.0).astype(BF16)

    def chosen(e, k):
        return jnp.where(lane == e[:, k:k + 1], 1.0, 0.0)

    def count_step(c, seen):
        rows = pl.ds(pl.multiple_of(c * ch, ch), ch)
        e = eid_ref[rows, :]
        both = chosen(e, 0) + chosen(e, 1)
        prefix_ref[rows, :] = jnp.dot(earlier, both.astype(BF16), preferred_element_type=F32) + seen
        return seen + jnp.sum(both, axis=0, keepdims=True)

    counts = lax.fori_loop(0, n_tokens // ch, count_step, jnp.zeros((1, LANES), F32))
    blocks = jnp.floor((counts + (MOE_BM - 1)) * (1.0 / MOE_BM))
    lower = jnp.where(lax.broadcasted_iota(jnp.int32, (LANES, LANES), 0)
                      < lax.broadcasted_iota(jnp.int32, (LANES, LANES), 1), 1.0, 0.0).astype(BF16)
    first_block = jnp.dot(jnp.broadcast_to(blocks, (8, LANES)).astype(BF16), lower,
                          preferred_element_type=F32)[0:1, :]
    first_row = first_block * MOE_BM

    def place_step(c, _):
        rows = pl.ds(pl.multiple_of(c * ch, ch), ch)
        e = eid_ref[rows, :]
        slot = prefix_ref[rows, :] + first_row
        p0 = jnp.sum(chosen(e, 0) * slot, axis=1, keepdims=True)
        p1 = jnp.sum(chosen(e, 1) * slot, axis=1, keepdims=True)
        pos_ref[rows, :] = jnp.where(lane == 0, p0, jnp.where(lane == 1, p1, 0.0)).astype(jnp.int32)
        return 0

    lax.fori_loop(0, n_tokens // ch, place_step, 0)
    cnt_ref[...] = jnp.broadcast_to(counts, cnt_ref.shape).astype(jnp.int32)


def _positions(eid):
    n = eid.shape[0]
    assert n % PLAN_CHUNK == 0 and MOE_BM & (MOE_BM - 1) == 0
    pos, cnt = pl.pallas_call(
        functools.partial(_positions_body, n_tokens=n),
        out_shape=[jax.ShapeDtypeStruct((n, LANES), jnp.int32),
                   jax.ShapeDtypeStruct((8, LANES), jnp.int32)],
        scratch_shapes=[pltpu.VMEM((n, LANES), F32)],
        compiler_params=pltpu.CompilerParams(vmem_limit_bytes=VMEM_LIMIT),
        name="moe_positions",
    )(eid)
    return pos, cnt[0, :N_EXPERTS]


def _moe_plan(eid, n_tokens):
    a = n_tokens * MOE_TOP_K
    n_blocks = (a + N_EXPERTS * (MOE_BM - 1)) // MOE_BM
    n_slots = n_blocks * MOE_BM
    n_groups = n_blocks // MOE_GROUP_BLOCKS + N_EXPERTS
    pos, counts = _positions(eid)
    pos = pos[:, :MOE_TOP_K].reshape(a)
    blocks_e = (counts + MOE_BM - 1) // MOE_BM
    blk_start = jnp.cumsum(blocks_e) - blocks_e
    tok = jnp.arange(a, dtype=jnp.int32) // MOE_TOP_K
    slot_tok = jnp.zeros((n_slots,), jnp.int32).at[pos].set(tok, unique_indices=True)
    groups_e = (blocks_e + MOE_GROUP_BLOCKS - 1) // MOE_GROUP_BLOCKS
    grp_end = jnp.cumsum(groups_e)
    grp_start = grp_end - groups_e
    gidx = jnp.arange(n_groups, dtype=jnp.int32)
    ge = jnp.minimum(jnp.searchsorted(grp_end, gidx, side="right"), N_EXPERTS - 1).astype(jnp.int32)
    local = gidx - grp_start[ge]
    active = gidx < grp_end[-1]
    gb0 = jnp.where(active, blk_start[ge] + local * MOE_GROUP_BLOCKS, 0).astype(jnp.int32)
    gnb = jnp.where(active, jnp.minimum(blocks_e[ge] - local * MOE_GROUP_BLOCKS, MOE_GROUP_BLOCKS),
                    0).astype(jnp.int32)
    last_e = ge[jnp.maximum(grp_end[-1] - 1, 0)]
    ge = jnp.where(active, ge, last_e).astype(jnp.int32)
    n_used = jnp.sum(blocks_e).astype(jnp.int32).reshape(1)
    return pos, slot_tok, ge, gb0, gnb, n_used, n_slots


def _layer(x, mem, positions, attn_norm_g, mem_norm_g, w_in, fox_forget_b, fox_q_norm_g,
           fox_k_norm_g, dsa_q_norm_g, dsa_k_norm_g, idx_k_norm_g, cross_q_norm_g, cross_k_norm_g,
           w_mem_kv, w_branch_fox, w_branch_dsa, w_branch_cross, w_out, ffn_norm_g,
           router_group_w, router_group_b, router_expert_w, router_expert_b,
           expert_w_gate, expert_w_up, expert_w_down):
    b, s, d = x.shape
    n = b * s
    n_mem = mem.shape[1]
    fw, dw, cw = FOX_HEADS * HEAD_DIM, DSA_HEADS * HEAD_DIM, CROSS_HEADS * HEAD_DIM
    iqw = IDX_HEADS * IDX_DIM
    tm = min(512, n)
    x2d = x.reshape(n, d)

    sizes = (fw, fw, fw, FOX_HEADS, dw, dw, dw, iqw, IDX_HEADS, IDX_DIM, cw, 3 * d)
    offs = [0]
    for sz in sizes:
        offs.append(offs[-1] + sz)
    (c_fq, c_fk, c_fv, c_fl, c_dq, c_dk, c_dv, c_iq, c_iw, c_ik, c_cq, c_g) = range(12)

    w_t = jnp.transpose(w_in, (2, 0, 1)).reshape(offs[-1], d)
    order = (c_dq, c_dk, c_fv, c_dv, c_iq, c_fq, c_fk, c_cq, c_g)
    row0, r = {}, 0
    for c in order:
        row0[c] = r
        r += sizes[c]
    w_all = jnp.concatenate([w_t[offs[c]:offs[c + 1]] for c in order], axis=0).astype(BF16)
    small_rows = [w_t[offs[c]:offs[c + 1]] for c in (c_ik, c_iw, c_fl)]
    small_rows.append(jnp.zeros((LANES - IDX_DIM - IDX_HEADS - FOX_HEADS, d), F32))
    w_small = jnp.concatenate(small_rows, axis=0).astype(BF16)
    q_unit = (HEAD_DIM ** -0.5) * LOG2E
    g_qk1 = jnp.concatenate([jnp.tile(fox_q_norm_g * q_unit, FOX_HEADS),
                             jnp.tile(fox_k_norm_g, FOX_HEADS),
                             jnp.tile(cross_q_norm_g * q_unit, CROSS_HEADS)])
    g_qk2 = jnp.concatenate([jnp.tile(dsa_q_norm_g * q_unit, DSA_HEADS),
                             jnp.tile(dsa_k_norm_g, DSA_HEADS)])
    g_ik = jnp.concatenate([idx_k_norm_g, jnp.zeros((LANES - IDX_DIM,), F32)])
    fb_lane = jnp.zeros((LANES,), F32).at[SMALL_F:SMALL_F + FOX_HEADS].set(fox_forget_b)

    pos_col = positions.reshape(n, 1).astype(F32)
    cd, sd, ci, si = _rope_tables(pos_col, tm)

    h = _rmsnorm(x2d, attn_norm_g, BF16, tm)
    tmp = min(PROJ_TM, n)
    qk1 = _proj(h, w_all, "norm", BF16, tm, 2 * fw + cw, gains=g_qk1, w_rows=(row0[c_fq], 2 * fw + cw))
    qk2 = _proj(h, w_all, "norm_rope", BF16, min(PROJ_ROPE_TM, n), 2 * dw, gains=g_qk2, tables=(cd, sd),
                w_rows=(row0[c_dq], 2 * dw))
    iq = _proj(h, w_all, "rope_idx", BF16, tmp, 1024, tables=(ci, si), w_rows=(row0[c_iq], iqw))
    small = _proj(h, w_small, "raw", F32, tm, LANES, w_rows=(0, LANES))
    ika, ikb, wl, lf_col = _small(small, g_ik, fb_lane, ci, si, tm)

    tq, tk = min(ATT_TQ, s), min(ATT_TK, s)
    assert tq == tk and s % tq == 0
    cum_col = _forget_cumsum(lf_col.reshape(b, s, LANES))
    qk1 = qk1.reshape(b, s, 2 * fw + cw)
    qk2 = qk2.reshape(b, s, 2 * dw)
    q_aug, k_aug = _fox_pack(qk1, cum_col, FOX_HEADS, min(512, s))
    vt = _proj_t(h, w_all, (row0[c_fv], fw + dw), tmp, 768, tk).reshape(b, s // tk, fw + dw, tk)
    o_fox = _attention(q_aug, 0, k_aug, 0, vt, 0, (), "fox", FOX_HEADS, AUG_DIM, tq, tk)

    topk = min(INDEX_TOPK, s // 4)
    wt = jnp.transpose(wl.reshape(b, s, LANES)[:, :, :IDX_HEADS], (0, 2, 1))
    bias = _select_bias(iq.reshape(b, s, iqw), ika.reshape(b, s, LANES), ikb.reshape(b, s, LANES),
                        wt, topk, tq, tk)
    o_dsa = _attention(qk2, 0, qk2, 1, vt, 1, (bias,), "dsa", DSA_HEADS, HEAD_DIM, tq, tk)

    tmm = min(512, b * n_mem)
    m_n = _rmsnorm(mem.reshape(b * n_mem, d), mem_norm_g, BF16, tmm)
    w_kv = w_mem_kv.astype(BF16)
    mk = _proj(m_n, w_kv[:, :cw], "norm", BF16, tmm, cw, gains=jnp.tile(cross_k_norm_g, CROSS_HEADS))
    mv = _proj(m_n, w_kv[:, cw:], "raw", BF16, tmm, cw)
    o_cross = _cross_attention(qk1, (2 * fw) // cw, mk.reshape(b, n_mem, cw), mv.reshape(b, n_mem, cw),
                               min(512, s))

    x_mid = _merge(h, o_fox.reshape(n, fw), o_dsa.reshape(n, dw), o_cross.reshape(n, cw), x2d,
                   w_all, row0[c_g], w_branch_fox.astype(BF16), w_branch_dsa.astype(BF16),
                   w_branch_cross.astype(BF16), w_out.astype(BF16), tm, MERGE_TN)

    w_router = jnp.concatenate(
        [router_group_w, router_expert_w, jnp.zeros((d, LANES - N_GROUPS - N_EXPERTS), F32)],
        axis=1).astype(BF16)
    b_router = jnp.concatenate(
        [router_group_b, router_expert_b, jnp.zeros((LANES - N_GROUPS - N_EXPERTS,), F32)])
    h2, eid, gate = _router(x_mid, ffn_norm_g, w_router, b_router, tm)
    pos, slot_tok, ge, gb0, gnb, n_used, n_slots = _moe_plan(eid, n)
    ys = _experts(h2, slot_tok, ge, gb0, gnb, n_used, expert_w_gate, expert_w_up, expert_w_down,
                  n_slots)
    out = _combine(x_mid, gate, ys, pos, min(256, n))
    return out.reshape(b, s, d)


def kernel(x, mem, positions, attn_norm_g, mem_norm_g, w_in, fox_forget_b, fox_q_norm_g, fox_k_norm_g, dsa_q_norm_g, dsa_k_norm_g, idx_k_norm_g, cross_q_norm_g, cross_k_norm_g, w_mem_kv, w_branch_fox, w_branch_dsa, w_branch_cross, w_out, ffn_norm_g, router_group_w, router_group_b, router_expert_w, router_expert_b, expert_w_gate, expert_w_up, expert_w_down):
    depth = w_in.shape[0]
    for layer in range(depth):
        x = _layer(
            x, mem, positions, attn_norm_g[layer], mem_norm_g[layer], w_in[layer:layer + 1],
            fox_forget_b[layer], fox_q_norm_g[layer], fox_k_norm_g[layer],
            dsa_q_norm_g[layer], dsa_k_norm_g[layer], idx_k_norm_g[layer],
            cross_q_norm_g[layer], cross_k_norm_g[layer], w_mem_kv[layer],
            w_branch_fox[layer], w_branch_dsa[layer], w_branch_cross[layer], w_out[layer],
            ffn_norm_g[layer], router_group_w[layer], router_group_b[layer],
            router_expert_w[layer], router_expert_b[layer], expert_w_gate[layer],
            expert_w_up[layer], expert_w_down[layer])
    return x
```

```python
import functools

import jax
import jax.numpy as jnp
from jax import lax
from jax.experimental import pallas as pl
from jax.experimental.pallas import tpu as pltpu

F32 = jnp.float32
BF16 = jnp.bfloat16

LANES = 128
HEAD_DIM = 128
FOX_HEADS = 6
DSA_HEADS = 6
CROSS_HEADS = 4
IDX_HEADS = 16
IDX_DIM = 64
ROPE_THETA = 500000.0
ROT_DIM = HEAD_DIM // 4
IDX_ROT_DIM = IDX_DIM // 4
INDEX_TOPK = 256
N_GROUPS = 4
EXPERTS_PER_GROUP = 8
N_EXPERTS = N_GROUPS * EXPERTS_PER_GROUP
MOE_TOP_K = 2
EPS = 1e-6

NEG = -1e30
INT_MIN = -(2 ** 31)
VMEM_LIMIT = 56 * 1024 * 1024

ATT_TQ = 512
ATT_TK = 512
ATT_HEADS_PER_LOOP = 6
PROJ_TM = 1024
PROJ_ROPE_TM = 256
MERGE_TN = 512
MERGE_SUB = 256
MOE_BM = 128
MOE_GROUP_BLOCKS = 8
MOE_MM_ROWS = 256
MOE_FC = 512
DMA_ISSUE_UNROLL = 8


def _cparams(*sem):
    return pltpu.CompilerParams(dimension_semantics=sem, vmem_limit_bytes=VMEM_LIMIT)


def _rmsnorm_body(x_ref, g_ref, o_ref):
    x = x_ref[...]
    r = lax.rsqrt(jnp.mean(x * x, axis=-1, keepdims=True) + EPS)
    o_ref[...] = ((x * r) * g_ref[...]).astype(o_ref.dtype)


def _rmsnorm(x2d, g, out_dtype, tm):
    m, d = x2d.shape
    return pl.pallas_call(
        _rmsnorm_body,
        grid=(m // tm,),
        in_specs=[pl.BlockSpec((tm, d), lambda i: (i, 0)),
                  pl.BlockSpec((1, d), lambda i: (0, 0))],
        out_specs=pl.BlockSpec((tm, d), lambda i: (i, 0)),
        out_shape=jax.ShapeDtypeStruct((m, d), out_dtype),
        compiler_params=_cparams("parallel"),
        name="rmsnorm",
    )(x2d, g.reshape(1, d))


W_PREP_ROWS = 128


def _w_prep_body(src_ref, w_ref, o_ref):
    del src_ref
    o_ref[...] = w_ref[:, 0, :].astype(o_ref.dtype)


def _w_prep(w_t, src_rows):
    _, _, k = w_t.shape
    n_chunks = src_rows.shape[0]
    grid_spec = pltpu.PrefetchScalarGridSpec(
        num_scalar_prefetch=1,
        grid=(n_chunks,),
        in_specs=[pl.BlockSpec((pl.Element(W_PREP_ROWS), pl.Element(1), pl.Element(k)),
                               lambda c, src: (src[c], 0, 0))],
        out_specs=pl.BlockSpec((W_PREP_ROWS, k), lambda c, src: (c, 0)),
    )
    return pl.pallas_call(
        _w_prep_body,
        grid_spec=grid_spec,
        out_shape=jax.ShapeDtypeStruct((n_chunks * W_PREP_ROWS, k), BF16),
        compiler_params=_cparams("parallel"),
        name="w_prep",
    )(src_rows, w_t)


IDX_FREQ_LANE = 32


def _rope_table_body(pos_ref, f_ref, cd_ref, sd_ref, ci_ref, si_ref):
    ang = pos_ref[...] * f_ref[...]
    c = jnp.cos(ang)
    s = jnp.sin(ang)
    lane = lax.broadcasted_iota(jnp.int32, c.shape, 1)
    hd = ROT_DIM // 2
    cd_ref[...] = jnp.where(lane < hd, c, jnp.where(lane < ROT_DIM, pltpu.roll(c, hd, 1), 1.0))
    sd_ref[...] = jnp.where(lane < hd, -s, jnp.where(lane < ROT_DIM, pltpu.roll(s, hd, 1), 0.0))
    m = lane & (IDX_DIM - 1)
    hi = IDX_ROT_DIM // 2
    ci_src = pltpu.roll(c, LANES - IDX_FREQ_LANE, 1)
    si_src = pltpu.roll(s, LANES - IDX_FREQ_LANE, 1)
    ci_ref[...] = jnp.where(m < IDX_ROT_DIM, ci_src, 1.0)
    si_ref[...] = jnp.where(m < hi, -si_src, jnp.where(m < IDX_ROT_DIM, si_src, 0.0))


def _rope_tables(pos_col, tm):
    n = pos_col.shape[0]

    def inv_freq(rot_dim):
        half = rot_dim // 2
        return jnp.power(jnp.float32(ROPE_THETA), -jnp.arange(half, dtype=F32) * 2.0 / rot_dim)

    fi = jnp.tile(inv_freq(IDX_ROT_DIM), 2)
    freq = jnp.zeros((LANES,), F32).at[:ROT_DIM // 2].set(inv_freq(ROT_DIM))
    for base in (IDX_FREQ_LANE, IDX_FREQ_LANE + IDX_DIM):
        freq = freq.at[base:base + IDX_ROT_DIM].set(fi)
    tab = pl.BlockSpec((tm, LANES), lambda i: (i, 0))
    return pl.pallas_call(
        _rope_table_body,
        grid=(n // tm,),
        in_specs=[pl.BlockSpec((tm, 1), lambda i: (i, 0)), pl.BlockSpec((1, LANES), lambda i: (0, 0))],
        out_specs=[tab] * 4,
        out_shape=[jax.ShapeDtypeStruct((n, LANES), F32)] * 4,
        compiler_params=_cparams("parallel"),
        name="rope_tables",
    )(pos_col, freq.reshape(1, LANES))


def _apply_rope(a, c, s, half, period):
    lane = lax.broadcasted_iota(jnp.int32, a.shape, 1)
    first = (lane & (period - 1)) < half
    partner = jnp.where(first, pltpu.roll(a, LANES - half, 1), pltpu.roll(a, half, 1))
    return a * c + partner * s


def _proj_body(*refs, mode, n_lane_groups, w_rows):
    h_ref, w_ref = refs[0], refs[1]
    o_ref = refs[-1]

    if w_rows:
        acc = lax.dot_general(h_ref[...], w_ref[...], (((1,), (1,)), ((), ())),
                              preferred_element_type=F32)
    else:
        acc = jnp.dot(h_ref[...], w_ref[...], preferred_element_type=F32)
    if mode == "raw":
        o_ref[...] = acc.astype(o_ref.dtype)
        return
    for g in range(n_lane_groups):
        sl = slice(g * LANES, (g + 1) * LANES)
        a = acc[:, sl]
        if mode in ("norm", "norm_rope"):
            g_ref = refs[2]
            a = (a * lax.rsqrt(jnp.mean(a * a, axis=-1, keepdims=True) + EPS)) * g_ref[:, sl]
        if mode == "norm_rope":
            a = _apply_rope(a, refs[3][...], refs[4][...], ROT_DIM // 2, HEAD_DIM)
        if mode == "rope_idx":
            a = _apply_rope(a, refs[2][...], refs[3][...], IDX_ROT_DIM // 2, IDX_DIM)
        o_ref[:, sl] = a.astype(o_ref.dtype)


def _proj_t_body(h_ref, w_ref, o_ref, *, tk):
    acc = lax.dot_general(w_ref[...], h_ref[...], (((1,), (1,)), ((), ())),
                          preferred_element_type=F32)
    for q in range(o_ref.shape[0]):
        o_ref[q] = acc[:, q * tk:(q + 1) * tk].astype(o_ref.dtype)


def _proj_t(h, w, w_rows, tm, tn, tk):
    m, k = h.shape
    row0, n = w_rows
    assert row0 % tn == 0 and n % tn == 0 and tm % tk == 0
    return pl.pallas_call(
        functools.partial(_proj_t_body, tk=tk),
        grid=(n // tn, m // tm),
        in_specs=[pl.BlockSpec((tm, k), lambda j, i: (i, 0)),
                  pl.BlockSpec((tn, k), lambda j, i: (row0 // tn + j, 0))],
        out_specs=pl.BlockSpec((tm // tk, tn, tk), lambda j, i: (i, j, 0)),
        out_shape=jax.ShapeDtypeStruct((m // tk, n, tk), BF16),
        compiler_params=_cparams("parallel", "parallel"),
        name="proj_t",
    )(h, w)


def _proj(h, w, mode, out_dtype, tm, tn, gains=None, tables=None, w_rows=None):
    m, k = h.shape
    if w_rows is None:
        n = w.shape[1]
        w_spec = pl.BlockSpec((k, tn), lambda j, i: (0, j))
    else:
        row0, n = w_rows
        assert row0 % tn == 0 and n % tn == 0
        w_spec = pl.BlockSpec((tn, k), lambda j, i: (row0 // tn + j, 0))
    in_specs = [pl.BlockSpec((tm, k), lambda j, i: (i, 0)), w_spec]
    args = [h, w]
    if gains is not None:
        in_specs.append(pl.BlockSpec((1, tn), lambda j, i: (0, j)))
        args.append(gains.reshape(1, n))
    if tables is not None:
        for t in tables:
            in_specs.append(pl.BlockSpec((tm, LANES), lambda j, i: (i, 0)))
            args.append(t)
    return pl.pallas_call(
        functools.partial(_proj_body, mode=mode, n_lane_groups=tn // LANES,
                          w_rows=w_rows is not None),
        grid=(n // tn, m // tm),
        in_specs=in_specs,
        out_specs=pl.BlockSpec((tm, tn), lambda j, i: (i, j)),
        out_shape=jax.ShapeDtypeStruct((m, n), out_dtype),
        compiler_params=_cparams("parallel", "parallel"),
        name="proj_" + mode,
    )(*args)


SMALL_IK = 0
SMALL_IW = 64
SMALL_F = 80


def _small_body(x_ref, gk_ref, fb_ref, c_ref, s_ref, ika_ref, ikb_ref, wl_ref, lf_ref):
    x = x_ref[...]
    lane = lax.broadcasted_iota(jnp.int32, x.shape, 1)
    is_ik = lane < IDX_DIM
    ik = jnp.where(is_ik, x, 0.0)
    ms = jnp.sum(ik * ik, axis=-1, keepdims=True) * (1.0 / IDX_DIM)
    ik = (ik * lax.rsqrt(ms + EPS)) * gk_ref[...]
    ik = _apply_rope(ik, c_ref[...], s_ref[...], IDX_ROT_DIM // 2, IDX_DIM)
    ik = jnp.where(is_ik, ik, 0.0)
    ika_ref[...] = ik.astype(BF16)
    ikb_ref[...] = pltpu.roll(ik, IDX_DIM, 1).astype(BF16)
    wl_ref[...] = pltpu.roll(x, LANES - SMALL_IW, 1) * (2.0 ** -5)
    z = x + fb_ref[...]
    log_f = jnp.minimum(z, 0.0) - jnp.log1p(jnp.exp(-jnp.abs(z)))
    lf_ref[...] = pltpu.roll(log_f, LANES - SMALL_F, 1)


def _small(x, idx_k_gain_lane, forget_b_lane, ci, si, tm):
    n = x.shape[0]
    row = pl.BlockSpec((1, LANES), lambda i: (0, 0))
    tab = pl.BlockSpec((tm, LANES), lambda i: (i, 0))
    return pl.pallas_call(
        _small_body,
        grid=(n // tm,),
        in_specs=[tab, row, row, tab, tab],
        out_specs=[tab, tab, tab, tab],
        out_shape=[jax.ShapeDtypeStruct((n, LANES), BF16), jax.ShapeDtypeStruct((n, LANES), BF16),
                   jax.ShapeDtypeStruct((n, LANES), F32), jax.ShapeDtypeStruct((n, LANES), F32)],
        compiler_params=_cparams("parallel"),
        name="small_cols",
    )(x, idx_k_gain_lane.reshape(1, LANES), forget_b_lane.reshape(1, LANES), ci, si)


def _cumsum_body(col_ref, ccol_ref, *, seq):
    c = col_ref[...]
    ci = lax.broadcasted_iota(jnp.int32, c.shape, 0)
    sh = 1
    while sh < seq:
        c = c + jnp.where(ci >= sh, pltpu.roll(c, sh, 0), 0.0)
        sh *= 2
    ccol_ref[...] = c


def _forget_cumsum(lf_col):
    b, s, _ = lf_col.shape
    return pl.pallas_call(
        functools.partial(_cumsum_body, seq=s),
        grid=(b,),
        in_specs=[pl.BlockSpec((None, s, LANES), lambda i: (i, 0, 0))],
        out_specs=pl.BlockSpec((None, s, LANES), lambda i: (i, 0, 0)),
        out_shape=jax.ShapeDtypeStruct((b, s, LANES), F32),
        compiler_params=_cparams("parallel"),
        name="forget_cumsum",
    )(lf_col)


AUG_DIM = 2 * HEAD_DIM
LOG2E = 1.4426950408889634


def _split3(x):
    hi = x.astype(BF16).astype(F32)
    r1 = x - hi
    mid = r1.astype(BF16).astype(F32)
    lo = (r1 - mid).astype(BF16).astype(F32)
    return hi, mid, lo


def _fox_pack_body(q_ref, k_ref, cum_ref, qa_ref, ka_ref, *, n_heads):
    rows = q_ref.shape[0]
    lane = lax.broadcasted_iota(jnp.int32, (rows, LANES), 1)
    for h in range(n_heads):
        c = jnp.broadcast_to(cum_ref[:, h:h + 1], (rows, LANES)) * LOG2E
        qh, qm, ql = _split3(c)
        kh, km, kl = _split3(-c)
        q_aug = jnp.where(lane == 0, qh, jnp.where(lane == 1, qm, jnp.where(
            lane == 2, ql, jnp.where(lane < 6, 1.0, 0.0))))
        k_aug = jnp.where(lane < 3, 1.0, jnp.where(lane == 3, kh, jnp.where(
            lane == 4, km, jnp.where(lane == 5, kl, 0.0))))
        base = h * AUG_DIM
        qa_ref[:, base:base + HEAD_DIM] = q_ref[:, h * HEAD_DIM:(h + 1) * HEAD_DIM]
        ka_ref[:, base:base + HEAD_DIM] = k_ref[:, h * HEAD_DIM:(h + 1) * HEAD_DIM]
        qa_ref[:, base + HEAD_DIM:base + AUG_DIM] = q_aug.astype(BF16)
        ka_ref[:, base + HEAD_DIM:base + AUG_DIM] = k_aug.astype(BF16)


def _fox_pack(qk, cum_col, n_heads, ts):
    b, s, _ = qk.shape
    w = n_heads * HEAD_DIM
    wa = n_heads * AUG_DIM
    return pl.pallas_call(
        functools.partial(_fox_pack_body, n_heads=n_heads),
        grid=(b, s // ts),
        in_specs=[pl.BlockSpec((None, ts, w), lambda bi, i: (bi, i, 0)),
                  pl.BlockSpec((None, ts, w), lambda bi, i: (bi, i, 1)),
                  pl.BlockSpec((None, ts, LANES), lambda bi, i: (bi, i, 0))],
        out_specs=[pl.BlockSpec((None, ts, wa), lambda bi, i: (bi, i, 0)),
                   pl.BlockSpec((None, ts, wa), lambda bi, i: (bi, i, 0))],
        out_shape=[jax.ShapeDtypeStruct((b, s, wa), BF16)] * 2,
        compiler_params=_cparams("parallel", "parallel"),
        name="fox_pack",
    )(qk, qk, cum_col)


def _attn_body(q_ref, k_ref, vt_ref, *rest, tq, tk, n_heads, kdim, mode, heads_per_loop):
    o_ref = rest[-1]
    qi = pl.program_id(1)
    key0 = lax.broadcasted_iota(jnp.int32, (tk, tq), 0)
    qry = qi * tq + lax.broadcasted_iota(jnp.int32, (tk, tq), 1)
    def scores(h, j):
        off = pl.multiple_of(j * tk, tk)
        q = q_ref[:, h * kdim:(h + 1) * kdim]
        k = k_ref[pl.ds(off, tk), h * kdim:(h + 1) * kdim]
        return lax.dot_general(k, q, (((1,), (1,)), ((), ())), preferred_element_type=F32)

    def flush(h, j, p, alpha, acc):
        vt = vt_ref[j, h * HEAD_DIM:(h + 1) * HEAD_DIM, :]
        return alpha * acc + jnp.dot(vt, p, preferred_element_type=F32)

    def softmax_step(h, j, m, l, masked):
        s = scores(h, j)
        if mode == "dsa":
            s = s + rest[0][j]
        if masked:
            s = jnp.where(key0 + j * tk <= qry, s, NEG)
        m_new = jnp.maximum(m, jnp.max(s, axis=0, keepdims=True))
        alpha = jnp.exp2(m - m_new)
        p = jnp.exp2(s - m_new)
        l = alpha * l + jnp.sum(p, axis=0, keepdims=True)
        return p.astype(BF16), alpha, m_new, l

    last = qi
    for h0 in range(0, n_heads, heads_per_loop):
        heads = range(h0, h0 + heads_per_loop)

        def step(j, carries, masked=False, heads=heads):
            prev = jnp.maximum(j - 1, 0)
            out = []
            for h, (p, alpha, m, l, acc) in zip(heads, carries):
                acc = flush(h, prev, p, alpha, acc)
                p, alpha, m, l = softmax_step(h, j, m, l, masked)
                out.append((p, alpha, m, l, acc))
            return tuple(out)

        carries = tuple(
            (jnp.zeros((tk, tq), BF16), jnp.ones((1, tq), F32), jnp.full((1, tq), NEG, F32),
             jnp.zeros((1, tq), F32), jnp.zeros((HEAD_DIM, tq), F32)) for h in heads)
        n_loop = last if mode == "fox" else last + 1
        carries = lax.fori_loop(0, n_loop, step, carries)
        if mode == "fox":
            carries = step(last, carries, masked=True)
        for h, (p, alpha, m, l, acc) in zip(heads, carries):
            acc = flush(h, last, p, alpha, acc)
            o_ref[:, h * HEAD_DIM:(h + 1) * HEAD_DIM] = jnp.transpose(acc / l).astype(o_ref.dtype)


def _attention(q_arr, q_blk, k_arr, k_blk, vt_arr, v_blk, extras, mode, n_heads, kdim, tq, tk):
    b, s, _ = q_arr.shape
    w = n_heads * HEAD_DIM
    wk = n_heads * kdim
    in_specs = [pl.BlockSpec((None, tq, wk), lambda bi, qi: (bi, qi, q_blk)),
                pl.BlockSpec((None, s, wk), lambda bi, qi: (bi, 0, k_blk)),
                pl.BlockSpec((None, s // tk, w, tk), lambda bi, qi: (bi, 0, v_blk, 0))]
    if mode == "dsa":
        in_specs += [pl.BlockSpec((None, None, s // tk, tk, tq), lambda bi, qi: (bi, qi, 0, 0, 0))]
    return pl.pallas_call(
        functools.partial(_attn_body, tq=tq, tk=tk, n_heads=n_heads, kdim=kdim, mode=mode,
                          heads_per_loop=ATT_HEADS_PER_LOOP),
        grid=(b, s // tq),
        in_specs=in_specs,
        out_specs=pl.BlockSpec((None, tq, w), lambda bi, qi: (bi, qi, 0)),
        out_shape=jax.ShapeDtypeStruct((b, s, w), BF16),
        compiler_params=_cparams("parallel", "parallel"),
        name="attn_" + mode,
    )(q_arr, k_arr, vt_arr, *extras)


def _select_body(iq_ref, ika_ref, ikb_ref, wt_ref, o_ref, keys_ref, half_ref, *, tq, tk, n_chunks, topk):
    qi = pl.program_id(1)
    used = qi + 1
    key0 = lax.broadcasted_iota(jnp.int32, (tk, tq), 0)
    qry = qi * tq + lax.broadcasted_iota(jnp.int32, (tk, tq), 1)

    def score_chunk(c, _):
        off = pl.multiple_of(c * tk, tk)
        ka = ika_ref[pl.ds(off, tk), :]
        kb = ikb_ref[pl.ds(off, tk), :]
        acc = jnp.zeros((tk, tq), F32)
        for j in range(IDX_HEADS):
            pair = iq_ref[:, (j // 2) * LANES:(j // 2 + 1) * LANES]
            logit = lax.dot_general(ka if j % 2 == 0 else kb, pair, (((1,), (1,)), ((), ())),
                                    preferred_element_type=F32)
            acc = acc + wt_ref[j:j + 1, :] * jnp.maximum(logit, 0.0)
        bits = lax.bitcast_convert_type(acc, jnp.int32)
        key = jnp.where(bits < 0, bits ^ jnp.int32(0x7FFFFFFF), bits)
        key = jnp.where(key0 + off <= qry, key, jnp.int32(INT_MIN))
        keys_ref[c] = key
        half_ref[c] = lax.shift_right_arithmetic(key, 16).astype(jnp.int16)
        return 0

    lax.fori_loop(0, used, score_chunk, 0)

    one16 = jnp.int16(1)
    zero16 = jnp.int16(0)

    def count_ge(cand16):
        def count_chunk(c, cnt):
            partial = [None] * 4
            for i, r in enumerate(range(0, tk, 16)):
                ge = jnp.where(half_ref[c, r:r + 16, :] >= cand16, one16, zero16)
                partial[i % 4] = ge if partial[i % 4] is None else partial[i % 4] + ge
            return cnt + ((partial[0] + partial[1]) + (partial[2] + partial[3]))

        cnt = lax.fori_loop(0, used, count_chunk, jnp.zeros((16, tq), jnp.int16))
        return jnp.sum(cnt.astype(jnp.int32), axis=0, keepdims=True)

    def search16(need):
        def bit_step(i, carry):
            prefix, kept = carry
            cand_u = prefix | jnp.left_shift(jnp.int32(1), 15 - i)
            cnt = count_ge((cand_u - 32768).astype(jnp.int16))
            ok = cnt >= need
            return jnp.where(ok, cand_u, prefix), jnp.where(ok, cnt, kept)

        init = (jnp.zeros((1, tq), jnp.int32), jnp.full((1, tq), tk, jnp.int32) * used)
        prefix, kept = lax.fori_loop(0, 16, bit_step, init)
        return prefix - 32768, kept

    hi, _ = search16(topk)
    above = count_ge(jnp.minimum(hi + 1, 32767).astype(jnp.int16))
    above = jnp.where(hi >= 32767, 0, above)
    hi16 = hi.astype(jnp.int16)

    def low_chunk(c, _):
        low = (keys_ref[c] & jnp.int32(0xFFFF)) - 32768
        half_ref[c] = jnp.where(half_ref[c] == hi16, low.astype(jnp.int16), jnp.int16(-32768))
        return 0

    lax.fori_loop(0, used, low_chunk, 0)
    lo, at_or_above_lo = search16(topk - above)
    thr = lax.shift_left(hi, 16) | (lo + 32768)
    thr = jnp.maximum(thr, jnp.int32(INT_MIN + 1))
    tied = jnp.max(above + at_or_above_lo) > topk

    def count_keys(pred):
        def chunk(c, cnt):
            hit = jnp.where(pred(keys_ref[c]), 1.0, 0.0)
            return cnt + jnp.sum(hit.reshape(tk // 8, 8, tq), axis=0)

        cnt = lax.fori_loop(0, used, chunk, jnp.zeros((8, tq), F32))
        return jnp.sum(cnt, axis=0, keepdims=True)

    @pl.when(jnp.logical_not(tied))
    def _():
        def write_chunk(c, _):
            o_ref[c] = jnp.where(keys_ref[c] >= thr, 0.0, NEG)
            return 0

        lax.fori_loop(0, used, write_chunk, 0)

    @pl.when(tied)
    def _():
        need = topk - count_keys(lambda k: k > thr)
        earlier = jnp.where(lax.broadcasted_iota(jnp.int32, (tk, tk), 1)
                            < lax.broadcasted_iota(jnp.int32, (tk, tk), 0), 1.0, 0.0).astype(BF16)

        def write_chunk(c, seen):
            keys = keys_ref[c]
            tie = jnp.where(keys == thr, 1.0, 0.0)
            rank = jnp.dot(earlier, tie.astype(BF16), preferred_element_type=F32) + seen
            o_ref[c] = jnp.where(keys > thr, 0.0,
                                 jnp.where(keys == thr, jnp.where(rank < need, 0.0, NEG), NEG))
            return seen + jnp.sum(tie, axis=0, keepdims=True)

        lax.fori_loop(0, used, write_chunk, jnp.zeros((1, tq), F32))

    def fill_chunk(c, _):
        o_ref[c] = jnp.full((tk, tq), NEG, F32)
        return 0

    lax.fori_loop(used, n_chunks, fill_chunk, 0)


def _select_bias(iq, ika, ikb, wt, topk, tq, tk):
    b, s, _ = iq.shape
    n_chunks = s // tk
    return pl.pallas_call(
        functools.partial(_select_body, tq=tq, tk=tk, n_chunks=n_chunks, topk=topk),
        grid=(b, s // tq),
        in_specs=[pl.BlockSpec((None, tq, IDX_HEADS * IDX_DIM), lambda bi, qi: (bi, qi, 0)),
                  pl.BlockSpec((None, s, LANES), lambda bi, qi: (bi, 0, 0)),
                  pl.BlockSpec((None, s, LANES), lambda bi, qi: (bi, 0, 0)),
                  pl.BlockSpec((None, IDX_HEADS, tq), lambda bi, qi: (bi, 0, qi))],
        out_specs=pl.BlockSpec((None, None, n_chunks, tk, tq), lambda bi, qi: (bi, qi, 0, 0, 0)),
        out_shape=jax.ShapeDtypeStruct((b, s // tq, n_chunks, tk, tq), F32),
        scratch_shapes=[pltpu.VMEM((n_chunks, tk, tq), jnp.int32),
                        pltpu.VMEM((n_chunks, tk, tq), jnp.int16)],
        compiler_params=_cparams("parallel", "parallel"),
        name="index_select",
    )(iq, ika, ikb, wt)


def _cross_body(q_ref, k_ref, v_ref, o_ref, *, n_heads):
    for h in range(n_heads):
        hs = slice(h * HEAD_DIM, (h + 1) * HEAD_DIM)
        s = lax.dot_general(q_ref[:, hs], k_ref[:, hs], (((1,), (1,)), ((), ())),
                            preferred_element_type=F32)
        p = jnp.exp2(s - jnp.max(s, axis=-1, keepdims=True))
        l = jnp.sum(p, axis=-1, keepdims=True)
        o = jnp.dot(p.astype(BF16), v_ref[:, hs], preferred_element_type=F32)
        o_ref[:, hs] = (o / l).astype(o_ref.dtype)


def _cross_attention(q_arr, q_blk, mk, mv, tq):
    b, s, _ = q_arr.shape
    n_mem = mk.shape[1]
    w = CROSS_HEADS * HEAD_DIM
    return pl.pallas_call(
        functools.partial(_cross_body, n_heads=CROSS_HEADS),
        grid=(b, s // tq),
        in_specs=[pl.BlockSpec((None, tq, w), lambda bi, qi: (bi, qi, q_blk)),
                  pl.BlockSpec((None, n_mem, w), lambda bi, qi: (bi, 0, 0)),
                  pl.BlockSpec((None, n_mem, w), lambda bi, qi: (bi, 0, 0))],
        out_specs=pl.BlockSpec((None, tq, w), lambda bi, qi: (bi, qi, 0)),
        out_shape=jax.ShapeDtypeStruct((b, s, w), BF16),
        compiler_params=_cparams("parallel", "parallel"),
        name="cross_attn",
    )(q_arr, mk, mv)


def _sigmoid(x):
    return 1.0 / (1.0 + jnp.exp(-x))


def _merge_body(h_ref, of_ref, od_ref, oc_ref, x_ref, wgf_ref, wgd_ref, wgc_ref,
                wbf_ref, wbd_ref, wbc_ref, wo_ref, o_ref):
    j = pl.program_id(1)
    h = h_ref[...]
    tn = wo_ref.shape[0]
    sub = min(tn, MERGE_SUB)

    def gate_logits(wg_ref, cs):
        return lax.dot_general(h, wg_ref[cs, :], (((1,), (1,)), ((), ())), preferred_element_type=F32)

    part = None
    for c in range(tn // sub):
        cs = slice(c * sub, (c + 1) * sub)
        zf, zd, zc = gate_logits(wgf_ref, cs), gate_logits(wgd_ref, cs), gate_logits(wgc_ref, cs)
        bf = jnp.dot(of_ref[...], wbf_ref[:, cs], preferred_element_type=F32)
        bd = jnp.dot(od_ref[...], wbd_ref[:, cs], preferred_element_type=F32)
        bc = jnp.dot(oc_ref[...], wbc_ref[:, cs], preferred_element_type=F32)
        merged = (_sigmoid(zf) * bf + _sigmoid(zd) * bd) + _sigmoid(zc) * bc
        p = jnp.dot(merged.astype(BF16), wo_ref[cs, :], preferred_element_type=F32)
        part = p if part is None else part + p

    @pl.when(j == 0)
    def _():
        o_ref[...] = x_ref[...] + part

    @pl.when(j != 0)
    def _():
        o_ref[...] += part


def _merge(h, o_fox, o_dsa, o_cross, x, w_gates, gate_row0, wbf, wbd, wbc, w_out, tm, tn):
    m, d = x.shape
    nj = d // tn
    assert gate_row0 % tn == 0
    g0 = gate_row0 // tn
    row = lambda i, j: (i, 0)
    in_specs = [pl.BlockSpec((tm, d), row),
                pl.BlockSpec((tm, o_fox.shape[1]), row),
                pl.BlockSpec((tm, o_dsa.shape[1]), row),
                pl.BlockSpec((tm, o_cross.shape[1]), row),
                pl.BlockSpec((tm, d), row),
                pl.BlockSpec((tn, d), lambda i, j: (g0 + j, 0)),
                pl.BlockSpec((tn, d), lambda i, j: (g0 + nj + j, 0)),
                pl.BlockSpec((tn, d), lambda i, j: (g0 + 2 * nj + j, 0)),
                pl.BlockSpec((wbf.shape[0], tn), lambda i, j: (0, j)),
                pl.BlockSpec((wbd.shape[0], tn), lambda i, j: (0, j)),
                pl.BlockSpec((wbc.shape[0], tn), lambda i, j: (0, j)),
                pl.BlockSpec((tn, d), lambda i, j: (j, 0))]
    return pl.pallas_call(
        _merge_body,
        grid=(m // tm, nj),
        in_specs=in_specs,
        out_specs=pl.BlockSpec((tm, d), row),
        out_shape=jax.ShapeDtypeStruct((m, d), F32),
        compiler_params=_cparams("parallel", "arbitrary"),
        name="gated_merge",
    )(h, o_fox, o_dsa, o_cross, x, w_gates, w_gates, w_gates, wbf, wbd, wbc, w_out)


def _router_body(x_ref, g_ref, w_ref, b_ref, h_ref, eid_ref, gate_ref):
    x = x_ref[...]
    r = lax.rsqrt(jnp.mean(x * x, axis=-1, keepdims=True) + EPS)
    h = (x * r) * g_ref[...]
    hb = h.astype(BF16)
    bits = lax.bitcast_convert_type(hb.astype(F32), jnp.int32)
    half = bits.shape[1] // 2
    h_ref[...] = lax.shift_right_logical(bits[:, :half], 16) | (bits[:, half:] & jnp.int32(-65536))
    logits = jnp.dot(hb, w_ref[...], preferred_element_type=F32) + b_ref[...]
    lane = lax.broadcasted_iota(jnp.int32, logits.shape, 1).astype(F32)

    def masked_max(mask):
        return jnp.max(jnp.where(mask, logits, -jnp.inf), axis=-1, keepdims=True)

    def first_lane(mask):
        return jnp.min(jnp.where(mask, lane, float(LANES)), axis=-1, keepdims=True)

    is_group = lane < N_GROUPS
    gmax = masked_max(is_group)
    grp = first_lane(is_group & (logits == gmax))
    denom = jnp.sum(jnp.where(is_group, jnp.exp(logits - gmax), 0.0), axis=-1, keepdims=True)
    p_sel = 1.0 / denom
    lo = N_GROUPS + grp * EXPERTS_PER_GROUP
    in_grp = (lane >= lo) & (lane < lo + EXPERTS_PER_GROUP)
    e1 = masked_max(in_grp)
    j1 = first_lane(in_grp & (logits == e1))
    rest = in_grp & (lane != j1)
    e2 = masked_max(rest)
    j2 = first_lane(rest & (logits == e2))
    t = jnp.exp(e2 - e1)
    g1 = p_sel * (1.0 / (1.0 + t))
    g2 = p_sel * (t / (1.0 + t))
    eid = jnp.where(lane == 0.0, j1 - N_GROUPS, jnp.where(lane == 1.0, j2 - N_GROUPS, 0.0))
    eid_ref[...] = eid.astype(jnp.int32)
    gate_ref[...] = jnp.where(lane == 0.0, g1, jnp.where(lane == 1.0, g2, 0.0))


def _router(x, g, w_router, b_router, tm):
    m, d = x.shape
    return pl.pallas_call(
        _router_body,
        grid=(m // tm,),
        in_specs=[pl.BlockSpec((tm, d), lambda i: (i, 0)),
                  pl.BlockSpec((1, d), lambda i: (0, 0)),
                  pl.BlockSpec((d, LANES), lambda i: (0, 0)),
                  pl.BlockSpec((1, LANES), lambda i: (0, 0))],
        out_specs=[pl.BlockSpec((tm, d // 2), lambda i: (i, 0)),
                   pl.BlockSpec((tm, LANES), lambda i: (i, 0)),
                   pl.BlockSpec((tm, LANES), lambda i: (i, 0))],
        out_shape=[jax.ShapeDtypeStruct((m, d // 2), jnp.int32),
                   jax.ShapeDtypeStruct((m, LANES), jnp.int32),
                   jax.ShapeDtypeStruct((m, LANES), F32)],
        compiler_params=_cparams("parallel"),
        name="moe_router",
    )(x, g.reshape(1, d), w_router, b_router.reshape(1, LANES))


def _expert_body(ge_ref, gb0_ref, gnb_ref, used_ref, tok_ref, h_hbm, wg_ref, wu_ref, wd_ref, ys_hbm,
                 xs_ref, xb_ref, acc_ref, wgb_ref, wub_ref, wdb_ref, sem_in, sem_out,
                 *, bm, n_f, n_blocks):
    g = pl.program_id(0)
    f = pl.program_id(1)
    nb = gnb_ref[g]
    blk0 = gb0_ref[g]

    @pl.when(jnp.logical_and(g == 0, f == 0))
    def _():
        acc_ref[pl.ds(0, bm)] = jnp.zeros((bm,) + acc_ref.shape[1:], F32)

        def tail_copy(i):
            return pltpu.make_async_copy(acc_ref.at[pl.ds(0, bm)], ys_hbm.at[pl.ds(i * bm, bm)],
                                         sem_out)

        def start_tail(i, _):
            tail_copy(i).start()
            return 0

        lax.fori_loop(used_ref[0], n_blocks, start_tail, 0)

        def wait_tail(i, _):
            tail_copy(i).wait()
            return 0

        lax.fori_loop(used_ref[0], n_blocks, wait_tail, 0)

    def gather(grp):
        slot = grp % 2
        base = gb0_ref[grp] * bm

        def issue(r0, _):
            for u in range(DMA_ISSUE_UNROLL):
                r = r0 * DMA_ISSUE_UNROLL + u
                pltpu.make_async_copy(h_hbm.at[pl.ds(tok_ref[base + r], 1)],
                                      xs_ref.at[slot, pl.ds(r, 1)], sem_in.at[slot]).start()
            return 0

        lax.fori_loop(0, gnb_ref[grp] * (bm // DMA_ISSUE_UNROLL), issue, 0)

    def block_copy_out(i):
        return pltpu.make_async_copy(acc_ref.at[pl.ds(i * bm, bm)],
                                     ys_hbm.at[pl.ds((blk0 + i) * bm, bm)], sem_out)

    @pl.when(jnp.logical_and(g == 0, f == 0))
    def _():
        gather(g)

    @pl.when(jnp.logical_and(nb > 0, f == 0))
    def _():
        slot = g % 2

        def land(i, _):
            rows = pl.ds(pl.multiple_of(i * bm, bm), bm)
            pltpu.make_async_copy(h_hbm.at[pl.ds(0, bm)], xs_ref.at[slot, rows], sem_in.at[slot]).wait()
            return 0

        lax.fori_loop(0, nb, land, 0)

        half = xb_ref.shape[1] // 2

        def unpack(i, _):
            rows = pl.ds(pl.multiple_of(i * bm, bm), bm)
            w = xs_ref[slot, rows, :]
            lo = lax.bitcast_convert_type(lax.shift_left(w, 16), F32)
            hi = lax.bitcast_convert_type(w & jnp.int32(-65536), F32)
            xb_ref[rows, :half] = lo.astype(BF16)
            xb_ref[rows, half:] = hi.astype(BF16)
            return 0

        lax.fori_loop(0, nb, unpack, 0)

    @pl.when(jnp.logical_and(f == 0, g + 1 < pl.num_programs(0)))
    def _():
        gather(g + 1)

    def wait_out(n):
        def body(i, _):
            pltpu.make_async_copy(acc_ref.at[pl.ds(0, bm)], ys_hbm.at[pl.ds(0, bm)], sem_out).wait()
            return 0

        lax.fori_loop(0, n, body, 0)

    @pl.when(jnp.logical_and(f == 0, g > 0))
    def _():
        wait_out(gnb_ref[jnp.maximum(g - 1, 0)])

    @pl.when(nb > 0)
    def _():
        wgb_ref[...] = wg_ref[...].astype(BF16)
        wub_ref[...] = wu_ref[...].astype(BF16)
        wdb_ref[...] = wd_ref[...].astype(BF16)

        def swiglu(rows):
            xb = xb_ref[rows, :]
            a = jnp.dot(xb, wgb_ref[...], preferred_element_type=F32)
            u = jnp.dot(xb, wub_ref[...], preferred_element_type=F32)
            hid = (a * _sigmoid(a)) * u
            y = jnp.dot(hid.astype(BF16), wdb_ref[...], preferred_element_type=F32)

            @pl.when(f == 0)
            def _():
                acc_ref[rows, :] = y

            @pl.when(f != 0)
            def _():
                acc_ref[rows, :] += y

        per = MOE_MM_ROWS // bm

        def full_tile(i, _):
            swiglu(pl.ds(pl.multiple_of(i * MOE_MM_ROWS, MOE_MM_ROWS), MOE_MM_ROWS))
            return 0

        lax.fori_loop(0, nb // per, full_tile, 0)
        piece = per // 2
        while piece >= 1:
            done = (nb // (2 * piece)) * (2 * piece)

            @pl.when((nb // piece) % 2 == 1)
            def _(done=done, piece=piece):
                swiglu(pl.ds(pl.multiple_of(done * bm, piece * bm), piece * bm))

            piece //= 2

    @pl.when(jnp.logical_and(nb > 0, f == n_f - 1))
    def _():
        def start_out(i, _):
            block_copy_out(i).start()
            return 0

        lax.fori_loop(0, nb, start_out, 0)

        @pl.when(g == pl.num_programs(0) - 1)
        def _():
            wait_out(nb)


def _experts(h2, slot_tok, grp_e, grp_blk0, grp_nb, n_used, w_gate, w_up, w_down, n_slots):
    d = w_gate.shape[1]
    ff = w_gate.shape[2]
    n_f = ff // MOE_FC
    n_groups = grp_e.shape[0]
    rmax = MOE_GROUP_BLOCKS * MOE_BM

    def f_idx(g, f, gnb):
        return jnp.where(gnb[g] > 0, f, n_f - 1)

    grid_spec = pltpu.PrefetchScalarGridSpec(
        num_scalar_prefetch=5,
        grid=(n_groups, n_f),
        in_specs=[
            pl.BlockSpec(memory_space=pl.ANY),
            pl.BlockSpec((None, d, MOE_FC), lambda g, f, ge, gb0, gnb, *_: (ge[g], 0, f_idx(g, f, gnb))),
            pl.BlockSpec((None, d, MOE_FC), lambda g, f, ge, gb0, gnb, *_: (ge[g], 0, f_idx(g, f, gnb))),
            pl.BlockSpec((None, MOE_FC, d), lambda g, f, ge, gb0, gnb, *_: (ge[g], f_idx(g, f, gnb), 0)),
        ],
        out_specs=pl.BlockSpec(memory_space=pl.ANY),
        scratch_shapes=[pltpu.VMEM((2, rmax, d // 2), jnp.int32),
                        pltpu.VMEM((rmax, d), BF16),
                        pltpu.VMEM((rmax, d), F32),
                        pltpu.VMEM((d, MOE_FC), BF16),
                        pltpu.VMEM((d, MOE_FC), BF16),
                        pltpu.VMEM((MOE_FC, d), BF16),
                        pltpu.SemaphoreType.DMA((2,)),
                        pltpu.SemaphoreType.DMA(())],
    )
    return pl.pallas_call(
        functools.partial(_expert_body, bm=MOE_BM, n_f=n_f, n_blocks=n_slots // MOE_BM),
        grid_spec=grid_spec,
        out_shape=jax.ShapeDtypeStruct((n_slots, d), F32),
        compiler_params=_cparams("arbitrary", "arbitrary"),
        name="moe_experts",
    )(grp_e, grp_blk0, grp_nb, n_used, slot_tok, h2, w_gate, w_up, w_down)


def _combine_body(pos_ref, x_ref, gate_ref, ys_hbm, o_ref, buf_ref, sem, *, tm):
    i = pl.program_id(0)
    n_steps = pl.num_programs(0)

    def issue(step):
        slot = step % 2

        def body(r0, _):
            for u in range(DMA_ISSUE_UNROLL):
                r = r0 * DMA_ISSUE_UNROLL + u
                for k in range(MOE_TOP_K):
                    src = pos_ref[(step * tm + r) * MOE_TOP_K + k]
                    pltpu.make_async_copy(ys_hbm.at[pl.ds(src, 1)], buf_ref.at[slot, k, pl.ds(r, 1)],
                                          sem.at[slot]).start()
            return 0

        lax.fori_loop(0, tm // DMA_ISSUE_UNROLL, body, 0)

    @pl.when(i == 0)
    def _():
        issue(i)

    @pl.when(i + 1 < n_steps)
    def _():
        issue(i + 1)

    slot = i % 2
    for k in range(MOE_TOP_K):
        pltpu.make_async_copy(ys_hbm.at[pl.ds(0, tm)], buf_ref.at[slot, k], sem.at[slot]).wait()
    o_ref[...] = x_ref[...] + (buf_ref[slot, 0] * gate_ref[:, 0:1] + buf_ref[slot, 1] * gate_ref[:, 1:2])


def _combine(x, gate, ys, pos, tm):
    m, d = x.shape
    grid_spec = pltpu.PrefetchScalarGridSpec(
        num_scalar_prefetch=1,
        grid=(m // tm,),
        in_specs=[pl.BlockSpec((tm, d), lambda i, pos: (i, 0)),
                  pl.BlockSpec((tm, LANES), lambda i, pos: (i, 0)),
                  pl.BlockSpec(memory_space=pl.ANY)],
        out_specs=pl.BlockSpec((tm, d), lambda i, pos: (i, 0)),
        scratch_shapes=[pltpu.VMEM((2, MOE_TOP_K, tm, d), F32), pltpu.SemaphoreType.DMA((2,))],
    )
    return pl.pallas_call(
        functools.partial(_combine_body, tm=tm),
        grid_spec=grid_spec,
        out_shape=jax.ShapeDtypeStruct((m, d), F32),
        compiler_params=_cparams("arbitrary"),
        name="moe_combine",
    )(pos, x, gate, ys)


PLAN_CHUNK = 256


def _positions_body(eid_ref, pos_ref, tok_ref, cnt_ref, prefix_ref, *, n_tokens):
    ch = PLAN_CHUNK
    lane = lax.broadcasted_iota(jnp.int32, (ch, LANES), 1)
    earlier = jnp.where(lax.broadcasted_iota(jnp.int32, (ch, ch), 1)
                        < lax.broadcasted_iota(jnp.int32, (ch, ch), 0), 1.0, 0.0).astype(BF16)

    def chosen(e, k):
        return jnp.where(lane == e[:, k:k + 1], 1.0, 0.0)

    def count_step(c, seen):
        rows = pl.ds(pl.multiple_of(c * ch, ch), ch)
        e = eid_ref[rows, :]
        both = chosen(e, 0) + chosen(e, 1)
        prefix_ref[rows, :] = jnp.dot(earlier, both.astype(BF16), preferred_element_type=F32) + seen
        return seen + jnp.sum(both, axis=0, keepdims=True)

    counts = lax.fori_loop(0, n_tokens // ch, count_step, jnp.zeros((1, LANES), F32))
    blocks = jnp.floor((counts + (MOE_BM - 1)) * (1.0 / MOE_BM))
    lower = jnp.where(lax.broadcasted_iota(jnp.int32, (LANES, LANES), 0)
                      < lax.broadcasted_iota(jnp.int32, (LANES, LANES), 1), 1.0, 0.0).astype(BF16)
    first_block = jnp.dot(jnp.broadcast_to(blocks, (8, LANES)).astype(BF16), lower,
                          preferred_element_type=F32)[0:1, :]
    first_row = first_block * MOE_BM

    slot_rows = tok_ref.shape[0]
    row_id = lax.broadcasted_iota(jnp.int32, (slot_rows, ch), 0).astype(F32)
    tok_row = lax.broadcasted_iota(jnp.int32, (ch, 1), 0)

    def place_step(c, acc):
        acc_hi, acc_lo = acc
        rows = pl.ds(pl.multiple_of(c * ch, ch), ch)
        e = eid_ref[rows, :]
        slot = prefix_ref[rows, :] + first_row
        tok = c * ch + tok_row
        tok_hi = lax.shift_right_logical(tok, 7).astype(F32)
        tok_lo = (tok & (LANES - 1)).astype(F32)
        p = []
        for k in range(MOE_TOP_K):
            pk = jnp.sum(chosen(e, k) * slot, axis=1, keepdims=True)
            p.append(pk)
            srow = jnp.floor(pk * (1.0 / LANES))
            scol = pk - srow * LANES
            srow_t = jnp.transpose(jnp.broadcast_to(srow, (ch, LANES)))[0:1, :]
            onehot_rows = jnp.where(row_id == srow_t, 1.0, 0.0).astype(BF16)
            in_lane = lane.astype(F32) == scol
            acc_hi = acc_hi + jnp.dot(onehot_rows, jnp.where(in_lane, tok_hi, 0.0).astype(BF16),
                                      preferred_element_type=F32)
            acc_lo = acc_lo + jnp.dot(onehot_rows, jnp.where(in_lane, tok_lo, 0.0).astype(BF16),
                                      preferred_element_type=F32)
        pos_ref[rows, :] = jnp.where(lane == 0, p[0], jnp.where(lane == 1, p[1], 0.0)).astype(jnp.int32)
        return acc_hi, acc_lo

    zero = jnp.zeros((slot_rows, LANES), F32)
    acc_hi, acc_lo = lax.fori_loop(0, n_tokens // ch, place_step, (zero, zero))
    tok_ref[...] = (acc_hi * LANES + acc_lo).astype(jnp.int32)
    cnt_ref[...] = jnp.broadcast_to(counts, cnt_ref.shape).astype(jnp.int32)


def _positions(eid, n_slots):
    n = eid.shape[0]
    assert n % PLAN_CHUNK == 0 and MOE_BM & (MOE_BM - 1) == 0
    assert n_slots % LANES == 0 and n <= LANES * LANES
    slot_rows = -(-(n_slots // LANES) // LANES) * LANES
    pos, tok, cnt = pl.pallas_call(
        functools.partial(_positions_body, n_tokens=n),
        out_shape=[jax.ShapeDtypeStruct((n, LANES), jnp.int32),
                   jax.ShapeDtypeStruct((slot_rows, LANES), jnp.int32),
                   jax.ShapeDtypeStruct((8, LANES), jnp.int32)],
        scratch_shapes=[pltpu.VMEM((n, LANES), F32)],
        compiler_params=pltpu.CompilerParams(vmem_limit_bytes=VMEM_LIMIT),
        name="moe_positions",
    )(eid)
    return pos, tok.reshape(-1)[:n_slots], cnt[0, :N_EXPERTS]


def _moe_plan(eid, n_tokens):
    a = n_tokens * MOE_TOP_K
    n_blocks = (a + N_EXPERTS * (MOE_BM - 1)) // MOE_BM
    n_slots = n_blocks * MOE_BM
    n_groups = n_blocks // MOE_GROUP_BLOCKS + N_EXPERTS
    pos, slot_tok, counts = _positions(eid, n_slots)
    pos = pos[:, :MOE_TOP_K].reshape(a)
    blocks_e = (counts + MOE_BM - 1) // MOE_BM
    blk_start = jnp.cumsum(blocks_e) - blocks_e
    groups_e = (blocks_e + MOE_GROUP_BLOCKS - 1) // MOE_GROUP_BLOCKS
    grp_end = jnp.cumsum(groups_e)
    grp_start = grp_end - groups_e
    gidx = jnp.arange(n_groups, dtype=jnp.int32)
    ge = jnp.minimum(jnp.searchsorted(grp_end, gidx, side="right"), N_EXPERTS - 1).astype(jnp.int32)
    local = gidx - grp_start[ge]
    active = gidx < grp_end[-1]
    gb0 = jnp.where(active, blk_start[ge] + local * MOE_GROUP_BLOCKS, 0).astype(jnp.int32)
    gnb = jnp.where(active, jnp.minimum(blocks_e[ge] - local * MOE_GROUP_BLOCKS, MOE_GROUP_BLOCKS),
                    0).astype(jnp.int32)
    last_e = ge[jnp.maximum(grp_end[-1] - 1, 0)]
    ge = jnp.where(active, ge, last_e).astype(jnp.int32)
    n_used = jnp.sum(blocks_e).astype(jnp.int32).reshape(1)
    return pos, slot_tok, ge, gb0, gnb, n_used, n_slots


def _layer(x, mem, positions, attn_norm_g, mem_norm_g, w_in, fox_forget_b, fox_q_norm_g,
           fox_k_norm_g, dsa_q_norm_g, dsa_k_norm_g, idx_k_norm_g, cross_q_norm_g, cross_k_norm_g,
           w_mem_kv, w_branch_fox, w_branch_dsa, w_branch_cross, w_out, ffn_norm_g,
           router_group_w, router_group_b, router_expert_w, router_expert_b,
           expert_w_gate, expert_w_up, expert_w_down):
    b, s, d = x.shape
    n = b * s
    n_mem = mem.shape[1]
    fw, dw, cw = FOX_HEADS * HEAD_DIM, DSA_HEADS * HEAD_DIM, CROSS_HEADS * HEAD_DIM
    iqw = IDX_HEADS * IDX_DIM
    tm = min(512, n)
    x2d = x.reshape(n, d)

    sizes = (fw, fw, fw, FOX_HEADS, dw, dw, dw, iqw, IDX_HEADS, IDX_DIM, cw, 3 * d)
    offs = [0]
    for sz in sizes:
        offs.append(offs[-1] + sz)
    (c_fq, c_fk, c_fv, c_fl, c_dq, c_dk, c_dv, c_iq, c_iw, c_ik, c_cq, c_g) = range(12)

    w_t = jnp.transpose(w_in, (2, 0, 1)).reshape(offs[-1], d)
    order = (c_dq, c_dk, c_fv, c_dv, c_iq, c_fq, c_fk, c_cq, c_g)
    row0, r = {}, 0
    for c in order:
        row0[c] = r
        r += sizes[c]
    w_all = jnp.concatenate([w_t[offs[c]:offs[c + 1]] for c in order], axis=0).astype(BF16)
    small_rows = [w_t[offs[c]:offs[c + 1]] for c in (c_ik, c_iw, c_fl)]
    small_rows.append(jnp.zeros((LANES - IDX_DIM - IDX_HEADS - FOX_HEADS, d), F32))
    w_small = jnp.concatenate(small_rows, axis=0).astype(BF16)
    q_unit = (HEAD_DIM ** -0.5) * LOG2E
    g_qk1 = jnp.concatenate([jnp.tile(fox_q_norm_g * q_unit, FOX_HEADS),
                             jnp.tile(fox_k_norm_g, FOX_HEADS),
                             jnp.tile(cross_q_norm_g * q_unit, CROSS_HEADS)])
    g_qk2 = jnp.concatenate([jnp.tile(dsa_q_norm_g * q_unit, DSA_HEADS),
                             jnp.tile(dsa_k_norm_g, DSA_HEADS)])
    g_ik = jnp.concatenate([idx_k_norm_g, jnp.zeros((LANES - IDX_DIM,), F32)])
    fb_lane = jnp.zeros((LANES,), F32).at[SMALL_F:SMALL_F + FOX_HEADS].set(fox_forget_b)

    pos_col = positions.reshape(n, 1).astype(F32)
    cd, sd, ci, si = _rope_tables(pos_col, tm)

    h = _rmsnorm(x2d, attn_norm_g, BF16, tm)
    tmp = min(PROJ_TM, n)
    qk1 = _proj(h, w_all, "norm", BF16, tm, 2 * fw + cw, gains=g_qk1, w_rows=(row0[c_fq], 2 * fw + cw))
    qk2 = _proj(h, w_all, "norm_rope", BF16, min(PROJ_ROPE_TM, n), 2 * dw, gains=g_qk2, tables=(cd, sd),
                w_rows=(row0[c_dq], 2 * dw))
    iq = _proj(h, w_all, "rope_idx", BF16, tmp, 1024, tables=(ci, si), w_rows=(row0[c_iq], iqw))
    small = _proj(h, w_small, "raw", F32, tm, LANES, w_rows=(0, LANES))
    ika, ikb, wl, lf_col = _small(small, g_ik, fb_lane, ci, si, tm)

    tq, tk = min(ATT_TQ, s), min(ATT_TK, s)
    assert tq == tk and s % tq == 0
    cum_col = _forget_cumsum(lf_col.reshape(b, s, LANES))
    qk1 = qk1.reshape(b, s, 2 * fw + cw)
    qk2 = qk2.reshape(b, s, 2 * dw)
    q_aug, k_aug = _fox_pack(qk1, cum_col, FOX_HEADS, min(512, s))
    vt = _proj_t(h, w_all, (row0[c_fv], fw + dw), tmp, 768, tk).reshape(b, s // tk, fw + dw, tk)
    o_fox = _attention(q_aug, 0, k_aug, 0, vt, 0, (), "fox", FOX_HEADS, AUG_DIM, tq, tk)

    topk = min(INDEX_TOPK, s // 4)
    wt = jnp.transpose(wl.reshape(b, s, LANES)[:, :, :IDX_HEADS], (0, 2, 1))
    bias = _select_bias(iq.reshape(b, s, iqw), ika.reshape(b, s, LANES), ikb.reshape(b, s, LANES),
                        wt, topk, tq, tk)
    o_dsa = _attention(qk2, 0, qk2, 1, vt, 1, (bias,), "dsa", DSA_HEADS, HEAD_DIM, tq, tk)

    tmm = min(512, b * n_mem)
    m_n = _rmsnorm(mem.reshape(b * n_mem, d), mem_norm_g, BF16, tmm)
    w_kv = w_mem_kv.astype(BF16)
    mk = _proj(m_n, w_kv[:, :cw], "norm", BF16, tmm, cw, gains=jnp.tile(cross_k_norm_g, CROSS_HEADS))
    mv = _proj(m_n, w_kv[:, cw:], "raw", BF16, tmm, cw)
    o_cross = _cross_attention(qk1, (2 * fw) // cw, mk.reshape(b, n_mem, cw), mv.reshape(b, n_mem, cw),
                               min(512, s))

    x_mid = _merge(h, o_fox.reshape(n, fw), o_dsa.reshape(n, dw), o_cross.reshape(n, cw), x2d,
                   w_all, row0[c_g], w_branch_fox.astype(BF16), w_branch_dsa.astype(BF16),
                   w_branch_cross.astype(BF16), w_out.astype(BF16), tm, MERGE_TN)

    w_router = jnp.concatenate(
        [router_group_w, router_expert_w, jnp.zeros((d, LANES - N_GROUPS - N_EXPERTS), F32)],
        axis=1).astype(BF16)
    b_router = jnp.concatenate(
        [router_group_b, router_expert_b, jnp.zeros((LANES - N_GROUPS - N_EXPERTS,), F32)])
    h2, eid, gate = _router(x_mid, ffn_norm_g, w_router, b_router, tm)
    pos, slot_tok, ge, gb0, gnb, n_used, n_slots = _moe_plan(eid, n)
    ys = _experts(h2, slot_tok, ge, gb0, gnb, n_used, expert_w_gate, expert_w_up, expert_w_down,
                  n_slots)
    out = _combine(x_mid, gate, ys, pos, min(256, n))
    return out.reshape(b, s, d)


def kernel(x, mem, positions, attn_norm_g, mem_norm_g, w_in, fox_forget_b, fox_q_norm_g, fox_k_norm_g, dsa_q_norm_g, dsa_k_norm_g, idx_k_norm_g, cross_q_norm_g, cross_k_norm_g, w_mem_kv, w_branch_fox, w_branch_dsa, w_branch_cross, w_out, ffn_norm_g, router_group_w, router_group_b, router_expert_w, router_expert_b, expert_w_gate, expert_w_up, expert_w_down):
    depth = w_in.shape[0]
    for layer in range(depth):
        x = _layer(
            x, mem, positions, attn_norm_g[layer], mem_norm_g[layer], w_in[layer:layer + 1],
            fox_forget_b[layer], fox_q_norm_g[layer], fox_k_norm_g[layer],
            dsa_q_norm_g[layer], dsa_k_norm_g[layer], idx_k_norm_g[layer],
            cross_q_norm_g[layer], cross_k_norm_g[layer], w_mem_kv[layer],
            w_branch_fox[layer], w_branch_dsa[layer], w_branch_cross[layer], w_out[layer],
            ffn_norm_g[layer], router_group_w[layer], router_group_b[layer],
            router_expert_w[layer], router_expert_b[layer], expert_w_gate[layer],
            expert_w_up[layer], expert_w_down[layer])
    return x
```

```python
import functools

import jax
import jax.numpy as jnp
from jax import lax
from jax.experimental import pallas as pl
from jax.experimental.pallas import tpu as pltpu

F32 = jnp.float32
BF16 = jnp.bfloat16

LANES = 128
HEAD_DIM = 128
FOX_HEADS = 6
DSA_HEADS = 6
CROSS_HEADS = 4
IDX_HEADS = 16
IDX_DIM = 64
ROPE_THETA = 500000.0
ROT_DIM = HEAD_DIM // 4
IDX_ROT_DIM = IDX_DIM // 4
INDEX_TOPK = 256
N_GROUPS = 4
EXPERTS_PER_GROUP = 8
N_EXPERTS = N_GROUPS * EXPERTS_PER_GROUP
MOE_TOP_K = 2
EPS = 1e-6

NEG = -1e30
INT_MIN = -(2 ** 31)
VMEM_LIMIT = 56 * 1024 * 1024

ATT_TQ = 512
ATT_TK = 512
ATT_HEADS_PER_LOOP = 6
PROJ_TM = 1024
PROJ_ROPE_TM = 256
MERGE_TN = 512
MERGE_SUB = 256
MOE_BM = 128
MOE_GROUP_BLOCKS = 8
MOE_MM_ROWS = 256
MOE_FC = 512
DMA_ISSUE_UNROLL = 8


def _cparams(*sem):
    return pltpu.CompilerParams(dimension_semantics=sem, vmem_limit_bytes=VMEM_LIMIT)


def _rmsnorm_body(x_ref, g_ref, o_ref):
    x = x_ref[...]
    r = lax.rsqrt(jnp.mean(x * x, axis=-1, keepdims=True) + EPS)
    o_ref[...] = ((x * r) * g_ref[...]).astype(o_ref.dtype)


def _rmsnorm(x2d, g, out_dtype, tm):
    m, d = x2d.shape
    return pl.pallas_call(
        _rmsnorm_body,
        grid=(m // tm,),
        in_specs=[pl.BlockSpec((tm, d), lambda i: (i, 0)),
                  pl.BlockSpec((1, d), lambda i: (0, 0))],
        out_specs=pl.BlockSpec((tm, d), lambda i: (i, 0)),
        out_shape=jax.ShapeDtypeStruct((m, d), out_dtype),
        compiler_params=_cparams("parallel"),
        name="rmsnorm",
    )(x2d, g.reshape(1, d))


IDX_FREQ_LANE = 32


def _rope_table_body(pos_ref, f_ref, cd_ref, sd_ref, ci_ref, si_ref):
    ang = pos_ref[...] * f_ref[...]
    c = jnp.cos(ang)
    s = jnp.sin(ang)
    lane = lax.broadcasted_iota(jnp.int32, c.shape, 1)
    hd = ROT_DIM // 2
    cd_ref[...] = jnp.where(lane < hd, c, jnp.where(lane < ROT_DIM, pltpu.roll(c, hd, 1), 1.0))
    sd_ref[...] = jnp.where(lane < hd, -s, jnp.where(lane < ROT_DIM, pltpu.roll(s, hd, 1), 0.0))
    m = lane & (IDX_DIM - 1)
    hi = IDX_ROT_DIM // 2
    ci_src = pltpu.roll(c, LANES - IDX_FREQ_LANE, 1)
    si_src = pltpu.roll(s, LANES - IDX_FREQ_LANE, 1)
    ci_ref[...] = jnp.where(m < IDX_ROT_DIM, ci_src, 1.0)
    si_ref[...] = jnp.where(m < hi, -si_src, jnp.where(m < IDX_ROT_DIM, si_src, 0.0))


def _rope_tables(pos_col, tm):
    n = pos_col.shape[0]

    def inv_freq(rot_dim):
        half = rot_dim // 2
        return jnp.power(jnp.float32(ROPE_THETA), -jnp.arange(half, dtype=F32) * 2.0 / rot_dim)

    fi = jnp.tile(inv_freq(IDX_ROT_DIM), 2)
    freq = jnp.zeros((LANES,), F32).at[:ROT_DIM // 2].set(inv_freq(ROT_DIM))
    for base in (IDX_FREQ_LANE, IDX_FREQ_LANE + IDX_DIM):
        freq = freq.at[base:base + IDX_ROT_DIM].set(fi)
    tab = pl.BlockSpec((tm, LANES), lambda i: (i, 0))
    return pl.pallas_call(
        _rope_table_body,
        grid=(n // tm,),
        in_specs=[pl.BlockSpec((tm, 1), lambda i: (i, 0)), pl.BlockSpec((1, LANES), lambda i: (0, 0))],
        out_specs=[tab] * 4,
        out_shape=[jax.ShapeDtypeStruct((n, LANES), F32)] * 4,
        compiler_params=_cparams("parallel"),
        name="rope_tables",
    )(pos_col, freq.reshape(1, LANES))


def _apply_rope(a, c, s, half, period):
    lane = lax.broadcasted_iota(jnp.int32, a.shape, 1)
    first = (lane & (period - 1)) < half
    partner = jnp.where(first, pltpu.roll(a, LANES - half, 1), pltpu.roll(a, half, 1))
    return a * c + partner * s


def _proj_body(*refs, mode, n_lane_groups, w_rows):
    h_ref, w_ref = refs[0], refs[1]
    o_ref = refs[-1]

    if w_rows:
        acc = lax.dot_general(h_ref[...], w_ref[...], (((1,), (1,)), ((), ())),
                              preferred_element_type=F32)
    else:
        acc = jnp.dot(h_ref[...], w_ref[...], preferred_element_type=F32)
    if mode == "raw":
        o_ref[...] = acc.astype(o_ref.dtype)
        return
    for g in range(n_lane_groups):
        sl = slice(g * LANES, (g + 1) * LANES)
        a = acc[:, sl]
        if mode in ("norm", "norm_rope"):
            g_ref = refs[2]
            a = (a * lax.rsqrt(jnp.mean(a * a, axis=-1, keepdims=True) + EPS)) * g_ref[:, sl]
        if mode == "norm_rope":
            a = _apply_rope(a, refs[3][...], refs[4][...], ROT_DIM // 2, HEAD_DIM)
        if mode == "rope_idx":
            a = _apply_rope(a, refs[2][...], refs[3][...], IDX_ROT_DIM // 2, IDX_DIM)
        o_ref[:, sl] = a.astype(o_ref.dtype)


def _proj_t_body(h_ref, w_ref, o_ref, *, tk):
    acc = lax.dot_general(w_ref[...], h_ref[...], (((1,), (1,)), ((), ())),
                          preferred_element_type=F32)
    for q in range(o_ref.shape[0]):
        o_ref[q] = acc[:, q * tk:(q + 1) * tk].astype(o_ref.dtype)


def _proj_t(h, w, w_rows, tm, tn, tk):
    m, k = h.shape
    row0, n = w_rows
    assert row0 % tn == 0 and n % tn == 0 and tm % tk == 0
    return pl.pallas_call(
        functools.partial(_proj_t_body, tk=tk),
        grid=(n // tn, m // tm),
        in_specs=[pl.BlockSpec((tm, k), lambda j, i: (i, 0)),
                  pl.BlockSpec((tn, k), lambda j, i: (row0 // tn + j, 0))],
        out_specs=pl.BlockSpec((tm // tk, tn, tk), lambda j, i: (i, j, 0)),
        out_shape=jax.ShapeDtypeStruct((m // tk, n, tk), BF16),
        compiler_params=_cparams("parallel", "parallel"),
        name="proj_t",
    )(h, w)


def _proj(h, w, mode, out_dtype, tm, tn, gains=None, tables=None, w_rows=None):
    m, k = h.shape
    if w_rows is None:
        n = w.shape[1]
        w_spec = pl.BlockSpec((k, tn), lambda j, i: (0, j))
    else:
        row0, n = w_rows
        assert row0 % tn == 0 and n % tn == 0
        w_spec = pl.BlockSpec((tn, k), lambda j, i: (row0 // tn + j, 0))
    in_specs = [pl.BlockSpec((tm, k), lambda j, i: (i, 0)), w_spec]
    args = [h, w]
    if gains is not None:
        in_specs.append(pl.BlockSpec((1, tn), lambda j, i: (0, j)))
        args.append(gains.reshape(1, n))
    if tables is not None:
        for t in tables:
            in_specs.append(pl.BlockSpec((tm, LANES), lambda j, i: (i, 0)))
            args.append(t)
    return pl.pallas_call(
        functools.partial(_proj_body, mode=mode, n_lane_groups=tn // LANES,
                          w_rows=w_rows is not None),
        grid=(n // tn, m // tm),
        in_specs=in_specs,
        out_specs=pl.BlockSpec((tm, tn), lambda j, i: (i, j)),
        out_shape=jax.ShapeDtypeStruct((m, n), out_dtype),
        compiler_params=_cparams("parallel", "parallel"),
        name="proj_" + mode,
    )(*args)


SMALL_IK = 0
SMALL_IW = 64
SMALL_F = 80


def _small_body(x_ref, gk_ref, fb_ref, c_ref, s_ref, ika_ref, ikb_ref, wl_ref, lf_ref):
    x = x_ref[...]
    lane = lax.broadcasted_iota(jnp.int32, x.shape, 1)
    is_ik = lane < IDX_DIM
    ik = jnp.where(is_ik, x, 0.0)
    ms = jnp.sum(ik * ik, axis=-1, keepdims=True) * (1.0 / IDX_DIM)
    ik = (ik * lax.rsqrt(ms + EPS)) * gk_ref[...]
    ik = _apply_rope(ik, c_ref[...], s_ref[...], IDX_ROT_DIM // 2, IDX_DIM)
    ik = jnp.where(is_ik, ik, 0.0)
    ika_ref[...] = ik.astype(BF16)
    ikb_ref[...] = pltpu.roll(ik, IDX_DIM, 1).astype(BF16)
    wl_ref[...] = pltpu.roll(x, LANES - SMALL_IW, 1) * (2.0 ** -5)
    z = x + fb_ref[...]
    log_f = jnp.minimum(z, 0.0) - jnp.log1p(jnp.exp(-jnp.abs(z)))
    lf_ref[...] = pltpu.roll(log_f, LANES - SMALL_F, 1)


def _small(x, idx_k_gain_lane, forget_b_lane, ci, si, tm):
    n = x.shape[0]
    row = pl.BlockSpec((1, LANES), lambda i: (0, 0))
    tab = pl.BlockSpec((tm, LANES), lambda i: (i, 0))
    return pl.pallas_call(
        _small_body,
        grid=(n // tm,),
        in_specs=[tab, row, row, tab, tab],
        out_specs=[tab, tab, tab, tab],
        out_shape=[jax.ShapeDtypeStruct((n, LANES), BF16), jax.ShapeDtypeStruct((n, LANES), BF16),
                   jax.ShapeDtypeStruct((n, LANES), F32), jax.ShapeDtypeStruct((n, LANES), F32)],
        compiler_params=_cparams("parallel"),
        name="small_cols",
    )(x, idx_k_gain_lane.reshape(1, LANES), forget_b_lane.reshape(1, LANES), ci, si)


def _cumsum_body(col_ref, ccol_ref, *, seq):
    c = col_ref[...]
    ci = lax.broadcasted_iota(jnp.int32, c.shape, 0)
    sh = 1
    while sh < seq:
        c = c + jnp.where(ci >= sh, pltpu.roll(c, sh, 0), 0.0)
        sh *= 2
    ccol_ref[...] = c


def _forget_cumsum(lf_col):
    b, s, _ = lf_col.shape
    return pl.pallas_call(
        functools.partial(_cumsum_body, seq=s),
        grid=(b,),
        in_specs=[pl.BlockSpec((None, s, LANES), lambda i: (i, 0, 0))],
        out_specs=pl.BlockSpec((None, s, LANES), lambda i: (i, 0, 0)),
        out_shape=jax.ShapeDtypeStruct((b, s, LANES), F32),
        compiler_params=_cparams("parallel"),
        name="forget_cumsum",
    )(lf_col)


AUG_DIM = 2 * HEAD_DIM
LOG2E = 1.4426950408889634


def _split3(x):
    hi = x.astype(BF16).astype(F32)
    r1 = x - hi
    mid = r1.astype(BF16).astype(F32)
    lo = (r1 - mid).astype(BF16).astype(F32)
    return hi, mid, lo


def _fox_pack_body(q_ref, k_ref, cum_ref, qa_ref, ka_ref, *, n_heads):
    rows = q_ref.shape[0]
    lane = lax.broadcasted_iota(jnp.int32, (rows, LANES), 1)
    for h in range(n_heads):
        c = jnp.broadcast_to(cum_ref[:, h:h + 1], (rows, LANES)) * LOG2E
        qh, qm, ql = _split3(c)
        kh, km, kl = _split3(-c)
        q_aug = jnp.where(lane == 0, qh, jnp.where(lane == 1, qm, jnp.where(
            lane == 2, ql, jnp.where(lane < 6, 1.0, 0.0))))
        k_aug = jnp.where(lane < 3, 1.0, jnp.where(lane == 3, kh, jnp.where(
            lane == 4, km, jnp.where(lane == 5, kl, 0.0))))
        base = h * AUG_DIM
        qa_ref[:, base:base + HEAD_DIM] = q_ref[:, h * HEAD_DIM:(h + 1) * HEAD_DIM]
        ka_ref[:, base:base + HEAD_DIM] = k_ref[:, h * HEAD_DIM:(h + 1) * HEAD_DIM]
        qa_ref[:, base + HEAD_DIM:base + AUG_DIM] = q_aug.astype(BF16)
        ka_ref[:, base + HEAD_DIM:base + AUG_DIM] = k_aug.astype(BF16)


def _fox_pack(qk, cum_col, n_heads, ts):
    b, s, _ = qk.shape
    w = n_heads * HEAD_DIM
    wa = n_heads * AUG_DIM
    return pl.pallas_call(
        functools.partial(_fox_pack_body, n_heads=n_heads),
        grid=(b, s // ts),
        in_specs=[pl.BlockSpec((None, ts, w), lambda bi, i: (bi, i, 0)),
                  pl.BlockSpec((None, ts, w), lambda bi, i: (bi, i, 1)),
                  pl.BlockSpec((None, ts, LANES), lambda bi, i: (bi, i, 0))],
        out_specs=[pl.BlockSpec((None, ts, wa), lambda bi, i: (bi, i, 0)),
                   pl.BlockSpec((None, ts, wa), lambda bi, i: (bi, i, 0))],
        out_shape=[jax.ShapeDtypeStruct((b, s, wa), BF16)] * 2,
        compiler_params=_cparams("parallel", "parallel"),
        name="fox_pack",
    )(qk, qk, cum_col)


def _attn_body(q_ref, k_ref, vt_ref, *rest, tq, tk, n_heads, kdim, mode, heads_per_loop):
    o_ref = rest[-1]
    qi = pl.program_id(1)
    key0 = lax.broadcasted_iota(jnp.int32, (tk, tq), 0)
    qry = qi * tq + lax.broadcasted_iota(jnp.int32, (tk, tq), 1)
    def scores(h, j):
        off = pl.multiple_of(j * tk, tk)
        q = q_ref[:, h * kdim:(h + 1) * kdim]
        k = k_ref[pl.ds(off, tk), h * kdim:(h + 1) * kdim]
        return lax.dot_general(k, q, (((1,), (1,)), ((), ())), preferred_element_type=F32)

    def flush(h, j, p, alpha, acc):
        vt = vt_ref[j, h * HEAD_DIM:(h + 1) * HEAD_DIM, :]
        return alpha * acc + jnp.dot(vt, p, preferred_element_type=F32)

    def softmax_step(h, j, m, l, masked):
        s = scores(h, j)
        if mode == "dsa":
            s = s + rest[0][j]
        if masked:
            s = jnp.where(key0 + j * tk <= qry, s, NEG)
        m_new = jnp.maximum(m, jnp.max(s, axis=0, keepdims=True))
        alpha = jnp.exp2(m - m_new)
        p = jnp.exp2(s - m_new)
        l = alpha * l + jnp.sum(p, axis=0, keepdims=True)
        return p.astype(BF16), alpha, m_new, l

    last = qi
    for h0 in range(0, n_heads, heads_per_loop):
        heads = range(h0, h0 + heads_per_loop)

        def step(j, carries, masked=False, heads=heads):
            prev = jnp.maximum(j - 1, 0)
            out = []
            for h, (p, alpha, m, l, acc) in zip(heads, carries):
                acc = flush(h, prev, p, alpha, acc)
                p, alpha, m, l = softmax_step(h, j, m, l, masked)
                out.append((p, alpha, m, l, acc))
            return tuple(out)

        carries = tuple(
            (jnp.zeros((tk, tq), BF16), jnp.ones((1, tq), F32), jnp.full((1, tq), NEG, F32),
             jnp.zeros((1, tq), F32), jnp.zeros((HEAD_DIM, tq), F32)) for h in heads)
        n_loop = last if mode == "fox" else last + 1
        carries = lax.fori_loop(0, n_loop, step, carries)
        if mode == "fox":
            carries = step(last, carries, masked=True)
        for h, (p, alpha, m, l, acc) in zip(heads, carries):
            acc = flush(h, last, p, alpha, acc)
            o_ref[:, h * HEAD_DIM:(h + 1) * HEAD_DIM] = jnp.transpose(acc / l).astype(o_ref.dtype)


def _attention(q_arr, q_blk, k_arr, k_blk, vt_arr, v_blk, extras, mode, n_heads, kdim, tq, tk):
    b, s, _ = q_arr.shape
    w = n_heads * HEAD_DIM
    wk = n_heads * kdim
    in_specs = [pl.BlockSpec((None, tq, wk), lambda bi, qi: (bi, qi, q_blk)),
                pl.BlockSpec((None, s, wk), lambda bi, qi: (bi, 0, k_blk)),
                pl.BlockSpec((None, s // tk, w, tk), lambda bi, qi: (bi, 0, v_blk, 0))]
    if mode == "dsa":
        in_specs += [pl.BlockSpec((None, None, s // tk, tk, tq), lambda bi, qi: (bi, qi, 0, 0, 0))]
    return pl.pallas_call(
        functools.partial(_attn_body, tq=tq, tk=tk, n_heads=n_heads, kdim=kdim, mode=mode,
                          heads_per_loop=ATT_HEADS_PER_LOOP),
        grid=(b, s // tq),
        in_specs=in_specs,
        out_specs=pl.BlockSpec((None, tq, w), lambda bi, qi: (bi, qi, 0)),
        out_shape=jax.ShapeDtypeStruct((b, s, w), BF16),
        compiler_params=_cparams("parallel", "parallel"),
        name="attn_" + mode,
    )(q_arr, k_arr, vt_arr, *extras)


def _select_body(iq_ref, ika_ref, ikb_ref, wt_ref, o_ref, keys_ref, half_ref, *, tq, tk, n_chunks, topk):
    qi = pl.program_id(1)
    used = qi + 1
    key0 = lax.broadcasted_iota(jnp.int32, (tk, tq), 0)
    qry = qi * tq + lax.broadcasted_iota(jnp.int32, (tk, tq), 1)

    def score_chunk(c, _):
        off = pl.multiple_of(c * tk, tk)
        ka = ika_ref[pl.ds(off, tk), :]
        kb = ikb_ref[pl.ds(off, tk), :]
        acc = jnp.zeros((tk, tq), F32)
        for j in range(IDX_HEADS):
            pair = iq_ref[:, (j // 2) * LANES:(j // 2 + 1) * LANES]
            logit = lax.dot_general(ka if j % 2 == 0 else kb, pair, (((1,), (1,)), ((), ())),
                                    preferred_element_type=F32)
            acc = acc + wt_ref[j:j + 1, :] * jnp.maximum(logit, 0.0)
        bits = lax.bitcast_convert_type(acc, jnp.int32)
        key = jnp.where(bits < 0, bits ^ jnp.int32(0x7FFFFFFF), bits)
        key = jnp.where(key0 + off <= qry, key, jnp.int32(INT_MIN))
        keys_ref[c] = key
        half_ref[c] = lax.shift_right_arithmetic(key, 16).astype(jnp.int16)
        return 0

    lax.fori_loop(0, used, score_chunk, 0)

    one16 = jnp.int16(1)
    zero16 = jnp.int16(0)

    def count_ge(cand16):
        def count_chunk(c, cnt):
            partial = [None] * 4
            for i, r in enumerate(range(0, tk, 16)):
                ge = jnp.where(half_ref[c, r:r + 16, :] >= cand16, one16, zero16)
                partial[i % 4] = ge if partial[i % 4] is None else partial[i % 4] + ge
            return cnt + ((partial[0] + partial[1]) + (partial[2] + partial[3]))

        cnt = lax.fori_loop(0, used, count_chunk, jnp.zeros((16, tq), jnp.int16))
        return jnp.sum(cnt.astype(jnp.int32), axis=0, keepdims=True)

    def search16(need):
        def bit_step(i, carry):
            prefix, kept = carry
            cand_u = prefix | jnp.left_shift(jnp.int32(1), 15 - i)
            cnt = count_ge((cand_u - 32768).astype(jnp.int16))
            ok = cnt >= need
            return jnp.where(ok, cand_u, prefix), jnp.where(ok, cnt, kept)

        init = (jnp.zeros((1, tq), jnp.int32), jnp.full((1, tq), tk, jnp.int32) * used)
        prefix, kept = lax.fori_loop(0, 16, bit_step, init)
        return prefix - 32768, kept

    hi, _ = search16(topk)
    above = count_ge(jnp.minimum(hi + 1, 32767).astype(jnp.int16))
    above = jnp.where(hi >= 32767, 0, above)
    hi16 = hi.astype(jnp.int16)

    def low_chunk(c, _):
        low = (keys_ref[c] & jnp.int32(0xFFFF)) - 32768
        half_ref[c] = jnp.where(half_ref[c] == hi16, low.astype(jnp.int16), jnp.int16(-32768))
        return 0

    lax.fori_loop(0, used, low_chunk, 0)
    lo, at_or_above_lo = search16(topk - above)
    thr = lax.shift_left(hi, 16) | (lo + 32768)
    thr = jnp.maximum(thr, jnp.int32(INT_MIN + 1))
    tied = jnp.max(above + at_or_above_lo) > topk

    def count_keys(pred):
        def chunk(c, cnt):
            hit = jnp.where(pred(keys_ref[c]), 1.0, 0.0)
            return cnt + jnp.sum(hit.reshape(tk // 8, 8, tq), axis=0)

        cnt = lax.fori_loop(0, used, chunk, jnp.zeros((8, tq), F32))
        return jnp.sum(cnt, axis=0, keepdims=True)

    @pl.when(jnp.logical_not(tied))
    def _():
        def write_chunk(c, _):
            o_ref[c] = jnp.where(keys_ref[c] >= thr, 0.0, NEG)
            return 0

        lax.fori_loop(0, used, write_chunk, 0)

    @pl.when(tied)
    def _():
        need = topk - count_keys(lambda k: k > thr)
        earlier = jnp.where(lax.broadcasted_iota(jnp.int32, (tk, tk), 1)
                            < lax.broadcasted_iota(jnp.int32, (tk, tk), 0), 1.0, 0.0).astype(BF16)

        def write_chunk(c, seen):
            keys = keys_ref[c]
            tie = jnp.where(keys == thr, 1.0, 0.0)
            rank = jnp.dot(earlier, tie.astype(BF16), preferred_element_type=F32) + seen
            o_ref[c] = jnp.where(keys > thr, 0.0,
                                 jnp.where(keys == thr, jnp.where(rank < need, 0.0, NEG), NEG))
            return seen + jnp.sum(tie, axis=0, keepdims=True)

        lax.fori_loop(0, used, write_chunk, jnp.zeros((1, tq), F32))

    def fill_chunk(c, _):
        o_ref[c] = jnp.full((tk, tq), NEG, F32)
        return 0

    lax.fori_loop(used, n_chunks, fill_chunk, 0)


def _select_bias(iq, ika, ikb, wt, topk, tq, tk):
    b, s, _ = iq.shape
    n_chunks = s // tk
    return pl.pallas_call(
        functools.partial(_select_body, tq=tq, tk=tk, n_chunks=n_chunks, topk=topk),
        grid=(b, s // tq),
        in_specs=[pl.BlockSpec((None, tq, IDX_HEADS * IDX_DIM), lambda bi, qi: (bi, qi, 0)),
                  pl.BlockSpec((None, s, LANES), lambda bi, qi: (bi, 0, 0)),
                  pl.BlockSpec((None, s, LANES), lambda bi, qi: (bi, 0, 0)),
                  pl.BlockSpec((None, IDX_HEADS, tq), lambda bi, qi: (bi, 0, qi))],
        out_specs=pl.BlockSpec((None, None, n_chunks, tk, tq), lambda bi, qi: (bi, qi, 0, 0, 0)),
        out_shape=jax.ShapeDtypeStruct((b, s // tq, n_chunks, tk, tq), F32),
        scratch_shapes=[pltpu.VMEM((n_chunks, tk, tq), jnp.int32),
                        pltpu.VMEM((n_chunks, tk, tq), jnp.int16)],
        compiler_params=_cparams("parallel", "parallel"),
        name="index_select",
    )(iq, ika, ikb, wt)


def _cross_body(q_ref, k_ref, v_ref, o_ref, *, n_heads):
    for h in range(n_heads):
        hs = slice(h * HEAD_DIM, (h + 1) * HEAD_DIM)
        s = lax.dot_general(q_ref[:, hs], k_ref[:, hs], (((1,), (1,)), ((), ())),
                            preferred_element_type=F32)
        p = jnp.exp2(s - jnp.max(s, axis=-1, keepdims=True))
        l = jnp.sum(p, axis=-1, keepdims=True)
        o = jnp.dot(p.astype(BF16), v_ref[:, hs], preferred_element_type=F32)
        o_ref[:, hs] = (o / l).astype(o_ref.dtype)


def _cross_attention(q_arr, q_blk, mk, mv, tq):
    b, s, _ = q_arr.shape
    n_mem = mk.shape[1]
    w = CROSS_HEADS * HEAD_DIM
    return pl.pallas_call(
        functools.partial(_cross_body, n_heads=CROSS_HEADS),
        grid=(b, s // tq),
        in_specs=[pl.BlockSpec((None, tq, w), lambda bi, qi: (bi, qi, q_blk)),
                  pl.BlockSpec((None, n_mem, w), lambda bi, qi: (bi, 0, 0)),
                  pl.BlockSpec((None, n_mem, w), lambda bi, qi: (bi, 0, 0))],
        out_specs=pl.BlockSpec((None, tq, w), lambda bi, qi: (bi, qi, 0)),
        out_shape=jax.ShapeDtypeStruct((b, s, w), BF16),
        compiler_params=_cparams("parallel", "parallel"),
        name="cross_attn",
    )(q_arr, mk, mv)


def _sigmoid(x):
    return 1.0 / (1.0 + jnp.exp(-x))


def _merge_body(h_ref, of_ref, od_ref, oc_ref, x_ref, wgf_ref, wgd_ref, wgc_ref,
                wbf_ref, wbd_ref, wbc_ref, wo_ref, o_ref):
    j = pl.program_id(1)
    h = h_ref[...]
    tn = wo_ref.shape[0]
    sub = min(tn, MERGE_SUB)

    def gate_logits(wg_ref, cs):
        return lax.dot_general(h, wg_ref[cs, :], (((1,), (1,)), ((), ())), preferred_element_type=F32)

    part = None
    for c in range(tn // sub):
        cs = slice(c * sub, (c + 1) * sub)
        zf, zd, zc = gate_logits(wgf_ref, cs), gate_logits(wgd_ref, cs), gate_logits(wgc_ref, cs)
        bf = jnp.dot(of_ref[...], wbf_ref[:, cs], preferred_element_type=F32)
        bd = jnp.dot(od_ref[...], wbd_ref[:, cs], preferred_element_type=F32)
        bc = jnp.dot(oc_ref[...], wbc_ref[:, cs], preferred_element_type=F32)
        merged = (_sigmoid(zf) * bf + _sigmoid(zd) * bd) + _sigmoid(zc) * bc
        p = jnp.dot(merged.astype(BF16), wo_ref[cs, :], preferred_element_type=F32)
        part = p if part is None else part + p

    @pl.when(j == 0)
    def _():
        o_ref[...] = x_ref[...] + part

    @pl.when(j != 0)
    def _():
        o_ref[...] += part


def _merge(h, o_fox, o_dsa, o_cross, x, w_gates, gate_row0, wbf, wbd, wbc, w_out, tm, tn):
    m, d = x.shape
    nj = d // tn
    assert gate_row0 % tn == 0
    g0 = gate_row0 // tn
    row = lambda i, j: (i, 0)
    in_specs = [pl.BlockSpec((tm, d), row),
                pl.BlockSpec((tm, o_fox.shape[1]), row),
                pl.BlockSpec((tm, o_dsa.shape[1]), row),
                pl.BlockSpec((tm, o_cross.shape[1]), row),
                pl.BlockSpec((tm, d), row),
                pl.BlockSpec((tn, d), lambda i, j: (g0 + j, 0)),
                pl.BlockSpec((tn, d), lambda i, j: (g0 + nj + j, 0)),
                pl.BlockSpec((tn, d), lambda i, j: (g0 + 2 * nj + j, 0)),
                pl.BlockSpec((wbf.shape[0], tn), lambda i, j: (0, j)),
                pl.BlockSpec((wbd.shape[0], tn), lambda i, j: (0, j)),
                pl.BlockSpec((wbc.shape[0], tn), lambda i, j: (0, j)),
                pl.BlockSpec((tn, d), lambda i, j: (j, 0))]
    return pl.pallas_call(
        _merge_body,
        grid=(m // tm, nj),
        in_specs=in_specs,
        out_specs=pl.BlockSpec((tm, d), row),
        out_shape=jax.ShapeDtypeStruct((m, d), F32),
        compiler_params=_cparams("parallel", "arbitrary"),
        name="gated_merge",
    )(h, o_fox, o_dsa, o_cross, x, w_gates, w_gates, w_gates, wbf, wbd, wbc, w_out)


def _router_body(x_ref, g_ref, w_ref, b_ref, h_ref, eid_ref, gate_ref):
    x = x_ref[...]
    r = lax.rsqrt(jnp.mean(x * x, axis=-1, keepdims=True) + EPS)
    h = (x * r) * g_ref[...]
    hb = h.astype(BF16)
    bits = lax.bitcast_convert_type(hb.astype(F32), jnp.int32)
    half = bits.shape[1] // 2
    h_ref[...] = lax.shift_right_logical(bits[:, :half], 16) | (bits[:, half:] & jnp.int32(-65536))
    logits = jnp.dot(hb, w_ref[...], preferred_element_type=F32) + b_ref[...]
    lane = lax.broadcasted_iota(jnp.int32, logits.shape, 1).astype(F32)

    def masked_max(mask):
        return jnp.max(jnp.where(mask, logits, -jnp.inf), axis=-1, keepdims=True)

    def first_lane(mask):
        return jnp.min(jnp.where(mask, lane, float(LANES)), axis=-1, keepdims=True)

    is_group = lane < N_GROUPS
    gmax = masked_max(is_group)
    grp = first_lane(is_group & (logits == gmax))
    denom = jnp.sum(jnp.where(is_group, jnp.exp(logits - gmax), 0.0), axis=-1, keepdims=True)
    p_sel = 1.0 / denom
    lo = N_GROUPS + grp * EXPERTS_PER_GROUP
    in_grp = (lane >= lo) & (lane < lo + EXPERTS_PER_GROUP)
    e1 = masked_max(in_grp)
    j1 = first_lane(in_grp & (logits == e1))
    rest = in_grp & (lane != j1)
    e2 = masked_max(rest)
    j2 = first_lane(rest & (logits == e2))
    t = jnp.exp(e2 - e1)
    g1 = p_sel * (1.0 / (1.0 + t))
    g2 = p_sel * (t / (1.0 + t))
    eid = jnp.where(lane == 0.0, j1 - N_GROUPS, jnp.where(lane == 1.0, j2 - N_GROUPS, 0.0))
    eid_ref[...] = eid.astype(jnp.int32)
    gate_ref[...] = jnp.where(lane == 0.0, g1, jnp.where(lane == 1.0, g2, 0.0))


def _router(x, g, w_router, b_router, tm):
    m, d = x.shape
    return pl.pallas_call(
        _router_body,
        grid=(m // tm,),
        in_specs=[pl.BlockSpec((tm, d), lambda i: (i, 0)),
                  pl.BlockSpec((1, d), lambda i: (0, 0)),
                  pl.BlockSpec((d, LANES), lambda i: (0, 0)),
                  pl.BlockSpec((1, LANES), lambda i: (0, 0))],
        out_specs=[pl.BlockSpec((tm, d // 2), lambda i: (i, 0)),
                   pl.BlockSpec((tm, LANES), lambda i: (i, 0)),
                   pl.BlockSpec((tm, LANES), lambda i: (i, 0))],
        out_shape=[jax.ShapeDtypeStruct((m, d // 2), jnp.int32),
                   jax.ShapeDtypeStruct((m, LANES), jnp.int32),
                   jax.ShapeDtypeStruct((m, LANES), F32)],
        compiler_params=_cparams("parallel"),
        name="moe_router",
    )(x, g.reshape(1, d), w_router, b_router.reshape(1, LANES))


def _expert_body(ge_ref, gb0_ref, gnb_ref, used_ref, tok_ref, h_hbm, wg_ref, wu_ref, wd_ref, ys_hbm,
                 xs_ref, xb_ref, acc_ref, wgb_ref, wub_ref, wdb_ref, sem_in, sem_out,
                 *, bm, n_f, n_blocks):
    g = pl.program_id(0)
    f = pl.program_id(1)
    nb = gnb_ref[g]
    blk0 = gb0_ref[g]

    @pl.when(jnp.logical_and(g == 0, f == 0))
    def _():
        acc_ref[pl.ds(0, bm)] = jnp.zeros((bm,) + acc_ref.shape[1:], F32)

        def tail_copy(i):
            return pltpu.make_async_copy(acc_ref.at[pl.ds(0, bm)], ys_hbm.at[pl.ds(i * bm, bm)],
                                         sem_out)

        def start_tail(i, _):
            tail_copy(i).start()
            return 0

        lax.fori_loop(used_ref[0], n_blocks, start_tail, 0)

        def wait_tail(i, _):
            tail_copy(i).wait()
            return 0

        lax.fori_loop(used_ref[0], n_blocks, wait_tail, 0)

    def gather(grp):
        slot = grp % 2
        base = gb0_ref[grp] * bm

        def issue(r0, _):
            for u in range(DMA_ISSUE_UNROLL):
                r = r0 * DMA_ISSUE_UNROLL + u
                pltpu.make_async_copy(h_hbm.at[pl.ds(tok_ref[base + r], 1)],
                                      xs_ref.at[slot, pl.ds(r, 1)], sem_in.at[slot]).start()
            return 0

        lax.fori_loop(0, gnb_ref[grp] * (bm // DMA_ISSUE_UNROLL), issue, 0)

    def block_copy_out(i):
        return pltpu.make_async_copy(acc_ref.at[pl.ds(i * bm, bm)],
                                     ys_hbm.at[pl.ds((blk0 + i) * bm, bm)], sem_out)

    @pl.when(jnp.logical_and(g == 0, f == 0))
    def _():
        gather(g)

    @pl.when(jnp.logical_and(nb > 0, f == 0))
    def _():
        slot = g % 2

        def land(i, _):
            rows = pl.ds(pl.multiple_of(i * bm, bm), bm)
            pltpu.make_async_copy(h_hbm.at[pl.ds(0, bm)], xs_ref.at[slot, rows], sem_in.at[slot]).wait()
            return 0

        lax.fori_loop(0, nb, land, 0)

        half = xb_ref.shape[1] // 2

        def unpack(i, _):
            rows = pl.ds(pl.multiple_of(i * bm, bm), bm)
            w = xs_ref[slot, rows, :]
            lo = lax.bitcast_convert_type(lax.shift_left(w, 16), F32)
            hi = lax.bitcast_convert_type(w & jnp.int32(-65536), F32)
            xb_ref[rows, :half] = lo.astype(BF16)
            xb_ref[rows, half:] = hi.astype(BF16)
            return 0

        lax.fori_loop(0, nb, unpack, 0)

    @pl.when(jnp.logical_and(f == 0, g + 1 < pl.num_programs(0)))
    def _():
        gather(g + 1)

    def wait_out(n):
        def body(i, _):
            pltpu.make_async_copy(acc_ref.at[pl.ds(0, bm)], ys_hbm.at[pl.ds(0, bm)], sem_out).wait()
            return 0

        lax.fori_loop(0, n, body, 0)

    @pl.when(jnp.logical_and(f == 0, g > 0))
    def _():
        wait_out(gnb_ref[jnp.maximum(g - 1, 0)])

    @pl.when(nb > 0)
    def _():
        wgb_ref[...] = wg_ref[...].astype(BF16)
        wub_ref[...] = wu_ref[...].astype(BF16)
        wdb_ref[...] = wd_ref[...].astype(BF16)

        def swiglu(rows):
            xb = xb_ref[rows, :]
            a = jnp.dot(xb, wgb_ref[...], preferred_element_type=F32)
            u = jnp.dot(xb, wub_ref[...], preferred_element_type=F32)
            hid = (a * _sigmoid(a)) * u
            y = jnp.dot(hid.astype(BF16), wdb_ref[...], preferred_element_type=F32)

            @pl.when(f == 0)
            def _():
                acc_ref[rows, :] = y

            @pl.when(f != 0)
            def _():
                acc_ref[rows, :] += y

        per = MOE_MM_ROWS // bm

        def full_tile(i, _):
            swiglu(pl.ds(pl.multiple_of(i * MOE_MM_ROWS, MOE_MM_ROWS), MOE_MM_ROWS))
            return 0

        lax.fori_loop(0, nb // per, full_tile, 0)
        piece = per // 2
        while piece >= 1:
            done = (nb // (2 * piece)) * (2 * piece)

            @pl.when((nb // piece) % 2 == 1)
            def _(done=done, piece=piece):
                swiglu(pl.ds(pl.multiple_of(done * bm, piece * bm), piece * bm))

            piece //= 2

    @pl.when(jnp.logical_and(nb > 0, f == n_f - 1))
    def _():
        def start_out(i, _):
            block_copy_out(i).start()
            return 0

        lax.fori_loop(0, nb, start_out, 0)

        @pl.when(g == pl.num_programs(0) - 1)
        def _():
            wait_out(nb)


def _experts(h2, slot_tok, grp_e, grp_blk0, grp_nb, n_used, w_gate, w_up, w_down, n_slots):
    d = w_gate.shape[1]
    ff = w_gate.shape[2]
    n_f = ff // MOE_FC
    n_groups = grp_e.shape[0]
    rmax = MOE_GROUP_BLOCKS * MOE_BM

    def f_idx(g, f, gnb):
        return jnp.where(gnb[g] > 0, f, n_f - 1)

    grid_spec = pltpu.PrefetchScalarGridSpec(
        num_scalar_prefetch=5,
        grid=(n_groups, n_f),
        in_specs=[
            pl.BlockSpec(memory_space=pl.ANY),
            pl.BlockSpec((None, d, MOE_FC), lambda g, f, ge, gb0, gnb, *_: (ge[g], 0, f_idx(g, f, gnb))),
            pl.BlockSpec((None, d, MOE_FC), lambda g, f, ge, gb0, gnb, *_: (ge[g], 0, f_idx(g, f, gnb))),
            pl.BlockSpec((None, MOE_FC, d), lambda g, f, ge, gb0, gnb, *_: (ge[g], f_idx(g, f, gnb), 0)),
        ],
        out_specs=pl.BlockSpec(memory_space=pl.ANY),
        scratch_shapes=[pltpu.VMEM((2, rmax, d // 2), jnp.int32),
                        pltpu.VMEM((rmax, d), BF16),
                        pltpu.VMEM((rmax, d), F32),
                        pltpu.VMEM((d, MOE_FC), BF16),
                        pltpu.VMEM((d, MOE_FC), BF16),
                        pltpu.VMEM((MOE_FC, d), BF16),
                        pltpu.SemaphoreType.DMA((2,)),
                        pltpu.SemaphoreType.DMA(())],
    )
    return pl.pallas_call(
        functools.partial(_expert_body, bm=MOE_BM, n_f=n_f, n_blocks=n_slots // MOE_BM),
        grid_spec=grid_spec,
        out_shape=jax.ShapeDtypeStruct((n_slots, d), F32),
        compiler_params=_cparams("arbitrary", "arbitrary"),
        name="moe_experts",
    )(grp_e, grp_blk0, grp_nb, n_used, slot_tok, h2, w_gate, w_up, w_down)


def _combine_body(pos_ref, x_ref, gate_ref, ys_hbm, o_ref, buf_ref, sem, *, tm):
    i = pl.program_id(0)
    n_steps = pl.num_programs(0)

    def issue(step):
        slot = step % 2

        def body(r0, _):
            for u in range(DMA_ISSUE_UNROLL):
                r = r0 * DMA_ISSUE_UNROLL + u
                for k in range(MOE_TOP_K):
                    src = pos_ref[(step * tm + r) * MOE_TOP_K + k]
                    pltpu.make_async_copy(ys_hbm.at[pl.ds(src, 1)], buf_ref.at[slot, k, pl.ds(r, 1)],
                                          sem.at[slot]).start()
            return 0

        lax.fori_loop(0, tm // DMA_ISSUE_UNROLL, body, 0)

    @pl.when(i == 0)
    def _():
        issue(i)

    @pl.when(i + 1 < n_steps)
    def _():
        issue(i + 1)

    slot = i % 2
    for k in range(MOE_TOP_K):
        pltpu.make_async_copy(ys_hbm.at[pl.ds(0, tm)], buf_ref.at[slot, k], sem.at[slot]).wait()
    o_ref[...] = x_ref[...] + (buf_ref[slot, 0] * gate_ref[:, 0:1] + buf_ref[slot, 1] * gate_ref[:, 1:2])


def _combine(x, gate, ys, pos, tm):
    m, d = x.shape
    grid_spec = pltpu.PrefetchScalarGridSpec(
        num_scalar_prefetch=1,
        grid=(m // tm,),
        in_specs=[pl.BlockSpec((tm, d), lambda i, pos: (i, 0)),
                  pl.BlockSpec((tm, LANES), lambda i, pos: (i, 0)),
                  pl.BlockSpec(memory_space=pl.ANY)],
        out_specs=pl.BlockSpec((tm, d), lambda i, pos: (i, 0)),
        scratch_shapes=[pltpu.VMEM((2, MOE_TOP_K, tm, d), F32), pltpu.SemaphoreType.DMA((2,))],
    )
    return pl.pallas_call(
        functools.partial(_combine_body, tm=tm),
        grid_spec=grid_spec,
        out_shape=jax.ShapeDtypeStruct((m, d), F32),
        compiler_params=_cparams("arbitrary"),
        name="moe_combine",
    )(pos, x, gate, ys)


PLAN_CHUNK = 256


def _positions_body(eid_ref, pos_ref, tok_ref, cnt_ref, prefix_ref, *, n_tokens):
    ch = PLAN_CHUNK
    lane = lax.broadcasted_iota(jnp.int32, (ch, LANES), 1)
    earlier = jnp.where(lax.broadcasted_iota(jnp.int32, (ch, ch), 1)
                        < lax.broadcasted_iota(jnp.int32, (ch, ch), 0), 1.0, 0.0).astype(BF16)

    def chosen(e, k):
        return jnp.where(lane == e[:, k:k + 1], 1.0, 0.0)

    def count_step(c, seen):
        rows = pl.ds(pl.multiple_of(c * ch, ch), ch)
        e = eid_ref[rows, :]
        both = chosen(e, 0) + chosen(e, 1)
        prefix_ref[rows, :] = jnp.dot(earlier, both.astype(BF16), preferred_element_type=F32) + seen
        return seen + jnp.sum(both, axis=0, keepdims=True)

    counts = lax.fori_loop(0, n_tokens // ch, count_step, jnp.zeros((1, LANES), F32))
    blocks = jnp.floor((counts + (MOE_BM - 1)) * (1.0 / MOE_BM))
    lower = jnp.where(lax.broadcasted_iota(jnp.int32, (LANES, LANES), 0)
                      < lax.broadcasted_iota(jnp.int32, (LANES, LANES), 1), 1.0, 0.0).astype(BF16)
    first_block = jnp.dot(jnp.broadcast_to(blocks, (8, LANES)).astype(BF16), lower,
                          preferred_element_type=F32)[0:1, :]
    first_row = first_block * MOE_BM

    slot_rows = tok_ref.shape[0]
    row_id = lax.broadcasted_iota(jnp.int32, (slot_rows, ch), 0).astype(F32)
    tok_row = lax.broadcasted_iota(jnp.int32, (ch, 1), 0)

    def place_step(c, acc):
        acc_hi, acc_lo = acc
        rows = pl.ds(pl.multiple_of(c * ch, ch), ch)
        e = eid_ref[rows, :]
        slot = prefix_ref[rows, :] + first_row
        tok = c * ch + tok_row
        tok_hi = lax.shift_right_logical(tok, 7).astype(F32)
        tok_lo = (tok & (LANES - 1)).astype(F32)
        p = []
        for k in range(MOE_TOP_K):
            pk = jnp.sum(chosen(e, k) * slot, axis=1, keepdims=True)
            p.append(pk)
            srow = jnp.floor(pk * (1.0 / LANES))
            scol = pk - srow * LANES
            srow_t = jnp.transpose(jnp.broadcast_to(srow, (ch, LANES)))[0:1, :]
            onehot_rows = jnp.where(row_id == srow_t, 1.0, 0.0).astype(BF16)
            in_lane = lane.astype(F32) == scol
            acc_hi = acc_hi + jnp.dot(onehot_rows, jnp.where(in_lane, tok_hi, 0.0).astype(BF16),
                                      preferred_element_type=F32)
            acc_lo = acc_lo + jnp.dot(onehot_rows, jnp.where(in_lane, tok_lo, 0.0).astype(BF16),
                                      preferred_element_type=F32)
        pos_ref[rows, :] = jnp.where(lane == 0, p[0], jnp.where(lane == 1, p[1], 0.0)).astype(jnp.int32)
        return acc_hi, acc_lo

    zero = jnp.zeros((slot_rows, LANES), F32)
    acc_hi, acc_lo = lax.fori_loop(0, n_tokens // ch, place_step, (zero, zero))
    tok_ref[...] = (acc_hi * LANES + acc_lo).astype(jnp.int32)
    cnt_ref[...] = jnp.broadcast_to(counts, cnt_ref.shape).astype(jnp.int32)


def _positions(eid, n_slots):
    n = eid.shape[0]
    assert n % PLAN_CHUNK == 0 and MOE_BM & (MOE_BM - 1) == 0
    assert n_slots % LANES == 0 and n <= LANES * LANES
    slot_rows = -(-(n_slots // LANES) // LANES) * LANES
    pos, tok, cnt = pl.pallas_call(
        functools.partial(_positions_body, n_tokens=n),
        out_shape=[jax.ShapeDtypeStruct((n, LANES), jnp.int32),
                   jax.ShapeDtypeStruct((slot_rows, LANES), jnp.int32),
                   jax.ShapeDtypeStruct((8, LANES), jnp.int32)],
        scratch_shapes=[pltpu.VMEM((n, LANES), F32)],
        compiler_params=pltpu.CompilerParams(vmem_limit_bytes=VMEM_LIMIT),
        name="moe_positions",
    )(eid)
    return pos, tok.reshape(-1)[:n_slots], cnt[0, :N_EXPERTS]


def _moe_plan(eid, n_tokens):
    a = n_tokens * MOE_TOP_K
    n_blocks = (a + N_EXPERTS * (MOE_BM - 1)) // MOE_BM
    n_slots = n_blocks * MOE_BM
    n_groups = n_blocks // MOE_GROUP_BLOCKS + N_EXPERTS
    pos, slot_tok, counts = _positions(eid, n_slots)
    pos = pos[:, :MOE_TOP_K].reshape(a)
    blocks_e = (counts + MOE_BM - 1) // MOE_BM
    blk_start = jnp.cumsum(blocks_e) - blocks_e
    groups_e = (blocks_e + MOE_GROUP_BLOCKS - 1) // MOE_GROUP_BLOCKS
    grp_end = jnp.cumsum(groups_e)
    grp_start = grp_end - groups_e
    gidx = jnp.arange(n_groups, dtype=jnp.int32)
    ge = jnp.minimum(jnp.searchsorted(grp_end, gidx, side="right"), N_EXPERTS - 1).astype(jnp.int32)
    local = gidx - grp_start[ge]
    active = gidx < grp_end[-1]
    gb0 = jnp.where(active, blk_start[ge] + local * MOE_GROUP_BLOCKS, 0).astype(jnp.int32)
    gnb = jnp.where(active, jnp.minimum(blocks_e[ge] - local * MOE_GROUP_BLOCKS, MOE_GROUP_BLOCKS),
                    0).astype(jnp.int32)
    last_e = ge[jnp.maximum(grp_end[-1] - 1, 0)]
    ge = jnp.where(active, ge, last_e).astype(jnp.int32)
    n_used = jnp.sum(blocks_e).astype(jnp.int32).reshape(1)
    return pos, slot_tok, ge, gb0, gnb, n_used, n_slots


def _layer(x, mem, positions, attn_norm_g, mem_norm_g, w_in, fox_forget_b, fox_q_norm_g,
           fox_k_norm_g, dsa_q_norm_g, dsa_k_norm_g, idx_k_norm_g, cross_q_norm_g, cross_k_norm_g,
           w_mem_kv, w_branch_fox, w_branch_dsa, w_branch_cross, w_out, ffn_norm_g,
           router_group_w, router_group_b, router_expert_w, router_expert_b,
           expert_w_gate, expert_w_up, expert_w_down):
    b, s, d = x.shape
    n = b * s
    n_mem = mem.shape[1]
    fw, dw, cw = FOX_HEADS * HEAD_DIM, DSA_HEADS * HEAD_DIM, CROSS_HEADS * HEAD_DIM
    iqw = IDX_HEADS * IDX_DIM
    tm = min(512, n)
    x2d = x.reshape(n, d)

    sizes = (fw, fw, fw, FOX_HEADS, dw, dw, dw, iqw, IDX_HEADS, IDX_DIM, cw, 3 * d)
    offs = [0]
    for sz in sizes:
        offs.append(offs[-1] + sz)
    (c_fq, c_fk, c_fv, c_fl, c_dq, c_dk, c_dv, c_iq, c_iw, c_ik, c_cq, c_g) = range(12)

    w_t = jnp.transpose(w_in, (2, 0, 1)).reshape(offs[-1], d)
    order = (c_dq, c_dk, c_fv, c_dv, c_iq, c_fq, c_fk, c_cq, c_g)
    row0, r = {}, 0
    for c in order:
        row0[c] = r
        r += sizes[c]
    w_all = jnp.concatenate([w_t[offs[c]:offs[c + 1]] for c in order], axis=0).astype(BF16)
    small_rows = [w_t[offs[c]:offs[c + 1]] for c in (c_ik, c_iw, c_fl)]
    small_rows.append(jnp.zeros((LANES - IDX_DIM - IDX_HEADS - FOX_HEADS, d), F32))
    w_small = jnp.concatenate(small_rows, axis=0).astype(BF16)
    q_unit = (HEAD_DIM ** -0.5) * LOG2E
    g_qk1 = jnp.concatenate([jnp.tile(fox_q_norm_g * q_unit, FOX_HEADS),
                             jnp.tile(fox_k_norm_g, FOX_HEADS),
                             jnp.tile(cross_q_norm_g * q_unit, CROSS_HEADS)])
    g_qk2 = jnp.concatenate([jnp.tile(dsa_q_norm_g * q_unit, DSA_HEADS),
                             jnp.tile(dsa_k_norm_g, DSA_HEADS)])
    g_ik = jnp.concatenate([idx_k_norm_g, jnp.zeros((LANES - IDX_DIM,), F32)])
    fb_lane = jnp.zeros((LANES,), F32).at[SMALL_F:SMALL_F + FOX_HEADS].set(fox_forget_b)

    pos_col = positions.reshape(n, 1).astype(F32)
    cd, sd, ci, si = _rope_tables(pos_col, tm)

    h = _rmsnorm(x2d, attn_norm_g, BF16, tm)
    tmp = min(PROJ_TM, n)
    qk1 = _proj(h, w_all, "norm", BF16, tm, 2 * fw + cw, gains=g_qk1, w_rows=(row0[c_fq], 2 * fw + cw))
    qk2 = _proj(h, w_all, "norm_rope", BF16, min(PROJ_ROPE_TM, n), 2 * dw, gains=g_qk2, tables=(cd, sd),
                w_rows=(row0[c_dq], 2 * dw))
    iq = _proj(h, w_all, "rope_idx", BF16, tmp, 1024, tables=(ci, si), w_rows=(row0[c_iq], iqw))
    small = _proj(h, w_small, "raw", F32, tm, LANES, w_rows=(0, LANES))
    ika, ikb, wl, lf_col = _small(small, g_ik, fb_lane, ci, si, tm)

    tq, tk = min(ATT_TQ, s), min(ATT_TK, s)
    assert tq == tk and s % tq == 0
    cum_col = _forget_cumsum(lf_col.reshape(b, s, LANES))
    qk1 = qk1.reshape(b, s, 2 * fw + cw)
    qk2 = qk2.reshape(b, s, 2 * dw)
    q_aug, k_aug = _fox_pack(qk1, cum_col, FOX_HEADS, min(512, s))
    vt = _proj_t(h, w_all, (row0[c_fv], fw + dw), tmp, 768, tk).reshape(b, s // tk, fw + dw, tk)
    o_fox = _attention(q_aug, 0, k_aug, 0, vt, 0, (), "fox", FOX_HEADS, AUG_DIM, tq, tk)

    topk = min(INDEX_TOPK, s // 4)
    wt = jnp.transpose(wl.reshape(b, s, LANES)[:, :, :IDX_HEADS], (0, 2, 1))
    bias = _select_bias(iq.reshape(b, s, iqw), ika.reshape(b, s, LANES), ikb.reshape(b, s, LANES),
                        wt, topk, tq, tk)
    o_dsa = _attention(qk2, 0, qk2, 1, vt, 1, (bias,), "dsa", DSA_HEADS, HEAD_DIM, tq, tk)

    tmm = min(512, b * n_mem)
    m_n = _rmsnorm(mem.reshape(b * n_mem, d), mem_norm_g, BF16, tmm)
    w_kv = w_mem_kv.astype(BF16)
    mk = _proj(m_n, w_kv[:, :cw], "norm", BF16, tmm, cw, gains=jnp.tile(cross_k_norm_g, CROSS_HEADS))
    mv = _proj(m_n, w_kv[:, cw:], "raw", BF16, tmm, cw)
    o_cross = _cross_attention(qk1, (2 * fw) // cw, mk.reshape(b, n_mem, cw), mv.reshape(b, n_mem, cw),
                               min(512, s))

    x_mid = _merge(h, o_fox.reshape(n, fw), o_dsa.reshape(n, dw), o_cross.reshape(n, cw), x2d,
                   w_all, row0[c_g], w_branch_fox.astype(BF16), w_branch_dsa.astype(BF16),
                   w_branch_cross.astype(BF16), w_out.astype(BF16), tm, MERGE_TN)

    w_router = jnp.concatenate(
        [router_group_w, router_expert_w, jnp.zeros((d, LANES - N_GROUPS - N_EXPERTS), F32)],
        axis=1).astype(BF16)
    b_router = jnp.concatenate(
        [router_group_b, router_expert_b, jnp.zeros((LANES - N_GROUPS - N_EXPERTS,), F32)])
    h2, eid, gate = _router(x_mid, ffn_norm_g, w_router, b_router, tm)
    pos, slot_tok, ge, gb0, gnb, n_used, n_slots = _moe_plan(eid, n)
    ys = _experts(h2, slot_tok, ge, gb0, gnb, n_used, expert_w_gate, expert_w_up, expert_w_down,
                  n_slots)
    out = _combine(x_mid, gate, ys, pos, min(256, n))
    return out.reshape(b, s, d)


def kernel(x, mem, positions, attn_norm_g, mem_norm_g, w_in, fox_forget_b, fox_q_norm_g, fox_k_norm_g, dsa_q_norm_g, dsa_k_norm_g, idx_k_norm_g, cross_q_norm_g, cross_k_norm_g, w_mem_kv, w_branch_fox, w_branch_dsa, w_branch_cross, w_out, ffn_norm_g, router_group_w, router_group_b, router_expert_w, router_expert_b, expert_w_gate, expert_w_up, expert_w_down):
    depth = w_in.shape[0]
    for layer in range(depth):
        x = _layer(
            x, mem, positions, attn_norm_g[layer], mem_norm_g[layer], w_in[layer:layer + 1],
            fox_forget_b[layer], fox_q_norm_g[layer], fox_k_norm_g[layer],
            dsa_q_norm_g[layer], dsa_k_norm_g[layer], idx_k_norm_g[layer],
            cross_q_norm_g[layer], cross_k_norm_g[layer], w_mem_kv[layer],
            w_branch_fox[layer], w_branch_dsa[layer], w_branch_cross[layer], w_out[layer],
            ffn_norm_g[layer], router_group_w[layer], router_group_b[layer],
            router_expert_w[layer], router_expert_b[layer], expert_w_gate[layer],
            expert_w_up[layer], expert_w_down[layer])
    return x
```

```python
import functools

import jax
import jax.numpy as jnp
from jax import lax
from jax.experimental import pallas as pl
from jax.experimental.pallas import tpu as pltpu

F32 = jnp.float32
BF16 = jnp.bfloat16

LANES = 128
HEAD_DIM = 128
FOX_HEADS = 6
DSA_HEADS = 6
CROSS_HEADS = 4
IDX_HEADS = 16
IDX_DIM = 64
ROPE_THETA = 500000.0
ROT_DIM = HEAD_DIM // 4
IDX_ROT_DIM = IDX_DIM // 4
INDEX_TOPK = 256
N_GROUPS = 4
EXPERTS_PER_GROUP = 8
N_EXPERTS = N_GROUPS * EXPERTS_PER_GROUP
MOE_TOP_K = 2
EPS = 1e-6

NEG = -1e30
INT_MIN = -(2 ** 31)
VMEM_LIMIT = 56 * 1024 * 1024

ATT_TQ = 512
ATT_TK = 512
ATT_HEADS_PER_LOOP = 6
PROJ_TM = 1024
PROJ_ROPE_TM = 256
MERGE_TN = 512
MERGE_SUB = 256
MOE_BM = 128
MOE_GROUP_BLOCKS = 8
MOE_MM_ROWS = 256
MOE_FC = 512
DMA_ISSUE_UNROLL = 8


def _cparams(*sem):
    return pltpu.CompilerParams(dimension_semantics=sem, vmem_limit_bytes=VMEM_LIMIT)


def _rmsnorm_body(x_ref, g_ref, o_ref):
    x = x_ref[...]
    r = lax.rsqrt(jnp.mean(x * x, axis=-1, keepdims=True) + EPS)
    o_ref[...] = ((x * r) * g_ref[...]).astype(o_ref.dtype)


def _rmsnorm(x2d, g, out_dtype, tm):
    m, d = x2d.shape
    return pl.pallas_call(
        _rmsnorm_body,
        grid=(m // tm,),
        in_specs=[pl.BlockSpec((tm, d), lambda i: (i, 0)),
                  pl.BlockSpec((1, d), lambda i: (0, 0))],
        out_specs=pl.BlockSpec((tm, d), lambda i: (i, 0)),
        out_shape=jax.ShapeDtypeStruct((m, d), out_dtype),
        compiler_params=_cparams("parallel"),
        name="rmsnorm",
    )(x2d, g.reshape(1, d))


IDX_FREQ_LANE = 32


def _rope_table_body(pos_ref, f_ref, cd_ref, sd_ref, ci_ref, si_ref):
    ang = pos_ref[...] * f_ref[...]
    c = jnp.cos(ang)
    s = jnp.sin(ang)
    lane = lax.broadcasted_iota(jnp.int32, c.shape, 1)
    hd = ROT_DIM // 2
    cd_ref[...] = jnp.where(lane < hd, c, jnp.where(lane < ROT_DIM, pltpu.roll(c, hd, 1), 1.0))
    sd_ref[...] = jnp.where(lane < hd, -s, jnp.where(lane < ROT_DIM, pltpu.roll(s, hd, 1), 0.0))
    m = lane & (IDX_DIM - 1)
    hi = IDX_ROT_DIM // 2
    ci_src = pltpu.roll(c, LANES - IDX_FREQ_LANE, 1)
    si_src = pltpu.roll(s, LANES - IDX_FREQ_LANE, 1)
    ci_ref[...] = jnp.where(m < IDX_ROT_DIM, ci_src, 1.0)
    si_ref[...] = jnp.where(m < hi, -si_src, jnp.where(m < IDX_ROT_DIM, si_src, 0.0))


def _rope_tables(pos_col, tm):
    n = pos_col.shape[0]

    def inv_freq(rot_dim):
        half = rot_dim // 2
        return jnp.power(jnp.float32(ROPE_THETA), -jnp.arange(half, dtype=F32) * 2.0 / rot_dim)

    fi = jnp.tile(inv_freq(IDX_ROT_DIM), 2)
    freq = jnp.zeros((LANES,), F32).at[:ROT_DIM // 2].set(inv_freq(ROT_DIM))
    for base in (IDX_FREQ_LANE, IDX_FREQ_LANE + IDX_DIM):
        freq = freq.at[base:base + IDX_ROT_DIM].set(fi)
    tab = pl.BlockSpec((tm, LANES), lambda i: (i, 0))
    return pl.pallas_call(
        _rope_table_body,
        grid=(n // tm,),
        in_specs=[pl.BlockSpec((tm, 1), lambda i: (i, 0)), pl.BlockSpec((1, LANES), lambda i: (0, 0))],
        out_specs=[tab] * 4,
        out_shape=[jax.ShapeDtypeStruct((n, LANES), F32)] * 4,
        compiler_params=_cparams("parallel"),
        name="rope_tables",
    )(pos_col, freq.reshape(1, LANES))


def _apply_rope(a, c, s, half, period):
    lane = lax.broadcasted_iota(jnp.int32, a.shape, 1)
    first = (lane & (period - 1)) < half
    partner = jnp.where(first, pltpu.roll(a, LANES - half, 1), pltpu.roll(a, half, 1))
    return a * c + partner * s


def _proj_body(*refs, mode, n_lane_groups, w_rows):
    h_ref, w_ref = refs[0], refs[1]
    o_ref = refs[-1]

    if w_rows:
        acc = lax.dot_general(h_ref[...], w_ref[...], (((1,), (1,)), ((), ())),
                              preferred_element_type=F32)
    else:
        acc = jnp.dot(h_ref[...], w_ref[...], preferred_element_type=F32)
    if mode == "raw":
        o_ref[...] = acc.astype(o_ref.dtype)
        return
    for g in range(n_lane_groups):
        sl = slice(g * LANES, (g + 1) * LANES)
        a = acc[:, sl]
        if mode in ("norm", "norm_rope"):
            g_ref = refs[2]
            a = (a * lax.rsqrt(jnp.mean(a * a, axis=-1, keepdims=True) + EPS)) * g_ref[:, sl]
        if mode == "norm_rope":
            a = _apply_rope(a, refs[3][...], refs[4][...], ROT_DIM // 2, HEAD_DIM)
        if mode == "rope_idx":
            a = _apply_rope(a, refs[2][...], refs[3][...], IDX_ROT_DIM // 2, IDX_DIM)
        o_ref[:, sl] = a.astype(o_ref.dtype)


def _proj_t_body(h_ref, w_ref, o_ref, *, tk):
    acc = lax.dot_general(w_ref[...], h_ref[...], (((1,), (1,)), ((), ())),
                          preferred_element_type=F32)
    for q in range(o_ref.shape[0]):
        o_ref[q] = acc[:, q * tk:(q + 1) * tk].astype(o_ref.dtype)


def _proj_t(h, w, w_rows, tm, tn, tk):
    m, k = h.shape
    row0, n = w_rows
    assert row0 % tn == 0 and n % tn == 0 and tm % tk == 0
    return pl.pallas_call(
        functools.partial(_proj_t_body, tk=tk),
        grid=(n // tn, m // tm),
        in_specs=[pl.BlockSpec((tm, k), lambda j, i: (i, 0)),
                  pl.BlockSpec((tn, k), lambda j, i: (row0 // tn + j, 0))],
        out_specs=pl.BlockSpec((tm // tk, tn, tk), lambda j, i: (i, j, 0)),
        out_shape=jax.ShapeDtypeStruct((m // tk, n, tk), BF16),
        compiler_params=_cparams("parallel", "parallel"),
        name="proj_t",
    )(h, w)


def _proj(h, w, mode, out_dtype, tm, tn, gains=None, tables=None, w_rows=None):
    m, k = h.shape
    if w_rows is None:
        n = w.shape[1]
        w_spec = pl.BlockSpec((k, tn), lambda j, i: (0, j))
    else:
        row0, n = w_rows
        assert row0 % tn == 0 and n % tn == 0
        w_spec = pl.BlockSpec((tn, k), lambda j, i: (row0 // tn + j, 0))
    in_specs = [pl.BlockSpec((tm, k), lambda j, i: (i, 0)), w_spec]
    args = [h, w]
    if gains is not None:
        in_specs.append(pl.BlockSpec((1, tn), lambda j, i: (0, j)))
        args.append(gains.reshape(1, n))
    if tables is not None:
        for t in tables:
            in_specs.append(pl.BlockSpec((tm, LANES), lambda j, i: (i, 0)))
            args.append(t)
    return pl.pallas_call(
        functools.partial(_proj_body, mode=mode, n_lane_groups=tn // LANES,
                          w_rows=w_rows is not None),
        grid=(n // tn, m // tm),
        in_specs=in_specs,
        out_specs=pl.BlockSpec((tm, tn), lambda j, i: (i, j)),
        out_shape=jax.ShapeDtypeStruct((m, n), out_dtype),
        compiler_params=_cparams("parallel", "parallel"),
        name="proj_" + mode,
    )(*args)


SMALL_IK = 0
SMALL_IW = 64
SMALL_F = 80


def _small_body(x_ref, gk_ref, fb_ref, c_ref, s_ref, ika_ref, ikb_ref, wl_ref, lf_ref):
    x = x_ref[...]
    lane = lax.broadcasted_iota(jnp.int32, x.shape, 1)
    is_ik = lane < IDX_DIM
    ik = jnp.where(is_ik, x, 0.0)
    ms = jnp.sum(ik * ik, axis=-1, keepdims=True) * (1.0 / IDX_DIM)
    ik = (ik * lax.rsqrt(ms + EPS)) * gk_ref[...]
    ik = _apply_rope(ik, c_ref[...], s_ref[...], IDX_ROT_DIM // 2, IDX_DIM)
    ik = jnp.where(is_ik, ik, 0.0)
    ika_ref[...] = ik.astype(BF16)
    ikb_ref[...] = pltpu.roll(ik, IDX_DIM, 1).astype(BF16)
    wl_ref[...] = pltpu.roll(x, LANES - SMALL_IW, 1) * (2.0 ** -5)
    z = x + fb_ref[...]
    log_f = jnp.minimum(z, 0.0) - jnp.log1p(jnp.exp(-jnp.abs(z)))
    lf_ref[...] = pltpu.roll(log_f, LANES - SMALL_F, 1)


def _small(x, idx_k_gain_lane, forget_b_lane, ci, si, tm):
    n = x.shape[0]
    row = pl.BlockSpec((1, LANES), lambda i: (0, 0))
    tab = pl.BlockSpec((tm, LANES), lambda i: (i, 0))
    return pl.pallas_call(
        _small_body,
        grid=(n // tm,),
        in_specs=[tab, row, row, tab, tab],
        out_specs=[tab, tab, tab, tab],
        out_shape=[jax.ShapeDtypeStruct((n, LANES), BF16), jax.ShapeDtypeStruct((n, LANES), BF16),
                   jax.ShapeDtypeStruct((n, LANES), F32), jax.ShapeDtypeStruct((n, LANES), F32)],
        compiler_params=_cparams("parallel"),
        name="small_cols",
    )(x, idx_k_gain_lane.reshape(1, LANES), forget_b_lane.reshape(1, LANES), ci, si)


def _cumsum_body(col_ref, ccol_ref, *, seq):
    c = col_ref[...]
    ci = lax.broadcasted_iota(jnp.int32, c.shape, 0)
    sh = 1
    while sh < seq:
        c = c + jnp.where(ci >= sh, pltpu.roll(c, sh, 0), 0.0)
        sh *= 2
    ccol_ref[...] = c


def _forget_cumsum(lf_col):
    b, s, _ = lf_col.shape
    return pl.pallas_call(
        functools.partial(_cumsum_body, seq=s),
        grid=(b,),
        in_specs=[pl.BlockSpec((None, s, LANES), lambda i: (i, 0, 0))],
        out_specs=pl.BlockSpec((None, s, LANES), lambda i: (i, 0, 0)),
        out_shape=jax.ShapeDtypeStruct((b, s, LANES), F32),
        compiler_params=_cparams("parallel"),
        name="forget_cumsum",
    )(lf_col)


AUG_DIM = 2 * HEAD_DIM
LOG2E = 1.4426950408889634


def _split3(x):
    hi = x.astype(BF16).astype(F32)
    r1 = x - hi
    mid = r1.astype(BF16).astype(F32)
    lo = (r1 - mid).astype(BF16).astype(F32)
    return hi, mid, lo


def _fox_pack_body(q_ref, k_ref, cum_ref, qa_ref, ka_ref, *, n_heads):
    rows = q_ref.shape[0]
    lane = lax.broadcasted_iota(jnp.int32, (rows, LANES), 1)
    for h in range(n_heads):
        c = jnp.broadcast_to(cum_ref[:, h:h + 1], (rows, LANES)) * LOG2E
        qh, qm, ql = _split3(c)
        kh, km, kl = _split3(-c)
        q_aug = jnp.where(lane == 0, qh, jnp.where(lane == 1, qm, jnp.where(
            lane == 2, ql, jnp.where(lane < 6, 1.0, 0.0))))
        k_aug = jnp.where(lane < 3, 1.0, jnp.where(lane == 3, kh, jnp.where(
            lane == 4, km, jnp.where(lane == 5, kl, 0.0))))
        base = h * AUG_DIM
        qa_ref[:, base:base + HEAD_DIM] = q_ref[:, h * HEAD_DIM:(h + 1) * HEAD_DIM]
        ka_ref[:, base:base + HEAD_DIM] = k_ref[:, h * HEAD_DIM:(h + 1) * HEAD_DIM]
        qa_ref[:, base + HEAD_DIM:base + AUG_DIM] = q_aug.astype(BF16)
        ka_ref[:, base + HEAD_DIM:base + AUG_DIM] = k_aug.astype(BF16)


def _fox_pack(qk, cum_col, n_heads, ts):
    b, s, _ = qk.shape
    w = n_heads * HEAD_DIM
    wa = n_heads * AUG_DIM
    return pl.pallas_call(
        functools.partial(_fox_pack_body, n_heads=n_heads),
        grid=(b, s // ts),
        in_specs=[pl.BlockSpec((None, ts, w), lambda bi, i: (bi, i, 0)),
                  pl.BlockSpec((None, ts, w), lambda bi, i: (bi, i, 1)),
                  pl.BlockSpec((None, ts, LANES), lambda bi, i: (bi, i, 0))],
        out_specs=[pl.BlockSpec((None, ts, wa), lambda bi, i: (bi, i, 0)),
                   pl.BlockSpec((None, ts, wa), lambda bi, i: (bi, i, 0))],
        out_shape=[jax.ShapeDtypeStruct((b, s, wa), BF16)] * 2,
        compiler_params=_cparams("parallel", "parallel"),
        name="fox_pack",
    )(qk, qk, cum_col)


def _attn_body(q_ref, k_ref, vt_ref, *rest, tq, tk, n_heads, kdim, mode, heads_per_loop):
    o_ref = rest[-1]
    qi = pl.program_id(1)
    key0 = lax.broadcasted_iota(jnp.int32, (tk, tq), 0)
    qry = qi * tq + lax.broadcasted_iota(jnp.int32, (tk, tq), 1)
    def scores(h, j):
        off = pl.multiple_of(j * tk, tk)
        q = q_ref[:, h * kdim:(h + 1) * kdim]
        k = k_ref[pl.ds(off, tk), h * kdim:(h + 1) * kdim]
        return lax.dot_general(k, q, (((1,), (1,)), ((), ())), preferred_element_type=F32)

    def flush(h, j, p, alpha, acc):
        vt = vt_ref[j, h * HEAD_DIM:(h + 1) * HEAD_DIM, :]
        return alpha * acc + jnp.dot(vt, p, preferred_element_type=F32)

    def softmax_step(h, j, m, l, masked):
        s = scores(h, j)
        if mode == "dsa":
            s = s + rest[0][j]
        if masked:
            s = jnp.where(key0 + j * tk <= qry, s, NEG)
        m_new = jnp.maximum(m, jnp.max(s, axis=0, keepdims=True))
        alpha = jnp.exp2(m - m_new)
        p = jnp.exp2(s - m_new)
        l = alpha * l + jnp.sum(p, axis=0, keepdims=True)
        return p.astype(BF16), alpha, m_new, l

    last = qi
    for h0 in range(0, n_heads, heads_per_loop):
        heads = range(h0, h0 + heads_per_loop)

        def step(j, carries, masked=False, heads=heads):
            prev = jnp.maximum(j - 1, 0)
            out = []
            for h, (p, alpha, m, l, acc) in zip(heads, carries):
                acc = flush(h, prev, p, alpha, acc)
                p, alpha, m, l = softmax_step(h, j, m, l, masked)
                out.append((p, alpha, m, l, acc))
            return tuple(out)

        carries = tuple(
            (jnp.zeros((tk, tq), BF16), jnp.ones((1, tq), F32), jnp.full((1, tq), NEG, F32),
             jnp.zeros((1, tq), F32), jnp.zeros((HEAD_DIM, tq), F32)) for h in heads)
        n_loop = last if mode == "fox" else last + 1
        carries = lax.fori_loop(0, n_loop, step, carries)
        if mode == "fox":
            carries = step(last, carries, masked=True)
        for h, (p, alpha, m, l, acc) in zip(heads, carries):
            acc = flush(h, last, p, alpha, acc)
            o_ref[:, h * HEAD_DIM:(h + 1) * HEAD_DIM] = jnp.transpose(acc / l).astype(o_ref.dtype)


def _attention(q_arr, q_blk, k_arr, k_blk, vt_arr, v_blk, extras, mode, n_heads, kdim, tq, tk):
    b, s, _ = q_arr.shape
    w = n_heads * HEAD_DIM
    wk = n_heads * kdim
    in_specs = [pl.BlockSpec((None, tq, wk), lambda bi, qi: (bi, qi, q_blk)),
                pl.BlockSpec((None, s, wk), lambda bi, qi: (bi, 0, k_blk)),
                pl.BlockSpec((None, s // tk, w, tk), lambda bi, qi: (bi, 0, v_blk, 0))]
    if mode == "dsa":
        in_specs += [pl.BlockSpec((None, None, s // tk, tk, tq), lambda bi, qi: (bi, qi, 0, 0, 0))]
    return pl.pallas_call(
        functools.partial(_attn_body, tq=tq, tk=tk, n_heads=n_heads, kdim=kdim, mode=mode,
                          heads_per_loop=ATT_HEADS_PER_LOOP),
        grid=(b, s // tq),
        in_specs=in_specs,
        out_specs=pl.BlockSpec((None, tq, w), lambda bi, qi: (bi, qi, 0)),
        out_shape=jax.ShapeDtypeStruct((b, s, w), BF16),
        compiler_params=_cparams("parallel", "parallel"),
        name="attn_" + mode,
    )(q_arr, k_arr, vt_arr, *extras)


def _select_body(iq_ref, ika_ref, ikb_ref, wt_ref, o_ref, keys_ref, half_ref, *, tq, tk, n_chunks, topk):
    qi = pl.program_id(1)
    used = qi + 1
    key0 = lax.broadcasted_iota(jnp.int32, (tk, tq), 0)
    qry = qi * tq + lax.broadcasted_iota(jnp.int32, (tk, tq), 1)

    def score_chunk(c, _):
        off = pl.multiple_of(c * tk, tk)
        ka = ika_ref[pl.ds(off, tk), :]
        kb = ikb_ref[pl.ds(off, tk), :]
        acc = jnp.zeros((tk, tq), F32)
        for j in range(IDX_HEADS):
            pair = iq_ref[:, (j // 2) * LANES:(j // 2 + 1) * LANES]
            logit = lax.dot_general(ka if j % 2 == 0 else kb, pair, (((1,), (1,)), ((), ())),
                                    preferred_element_type=F32)
            acc = acc + wt_ref[j:j + 1, :] * jnp.maximum(logit, 0.0)
        bits = lax.bitcast_convert_type(acc, jnp.int32)
        key = jnp.where(bits < 0, bits ^ jnp.int32(0x7FFFFFFF), bits)
        key = jnp.where(key0 + off <= qry, key, jnp.int32(INT_MIN))
        keys_ref[c] = key
        half_ref[c] = lax.shift_right_arithmetic(key, 16).astype(jnp.int16)
        return 0

    lax.fori_loop(0, used, score_chunk, 0)

    one16 = jnp.int16(1)
    zero16 = jnp.int16(0)

    def count_ge(cand16):
        def count_chunk(c, cnt):
            partial = [None] * 4
            for i, r in enumerate(range(0, tk, 16)):
                ge = jnp.where(half_ref[c, r:r + 16, :] >= cand16, one16, zero16)
                partial[i % 4] = ge if partial[i % 4] is None else partial[i % 4] + ge
            return cnt + ((partial[0] + partial[1]) + (partial[2] + partial[3]))

        cnt = lax.fori_loop(0, used, count_chunk, jnp.zeros((16, tq), jnp.int16))
        return jnp.sum(cnt.astype(jnp.int32), axis=0, keepdims=True)

    def search16(need):
        def bit_step(i, carry):
            prefix, kept = carry
            cand_u = prefix | jnp.left_shift(jnp.int32(1), 15 - i)
            cnt = count_ge((cand_u - 32768).astype(jnp.int16))
            ok = cnt >= need
            return jnp.where(ok, cand_u, prefix), jnp.where(ok, cnt, kept)

        init = (jnp.zeros((1, tq), jnp.int32), jnp.full((1, tq), tk, jnp.int32) * used)
        prefix, kept = lax.fori_loop(0, 16, bit_step, init)
        return prefix - 32768, kept

    hi, _ = search16(topk)
    above = count_ge(jnp.minimum(hi + 1, 32767).astype(jnp.int16))
    above = jnp.where(hi >= 32767, 0, above)
    hi16 = hi.astype(jnp.int16)

    def low_chunk(c, _):
        low = (keys_ref[c] & jnp.int32(0xFFFF)) - 32768
        half_ref[c] = jnp.where(half_ref[c] == hi16, low.astype(jnp.int16), jnp.int16(-32768))
        return 0

    lax.fori_loop(0, used, low_chunk, 0)
    lo, at_or_above_lo = search16(topk - above)
    thr = lax.shift_left(hi, 16) | (lo + 32768)
    thr = jnp.maximum(thr, jnp.int32(INT_MIN + 1))
    tied = jnp.max(above + at_or_above_lo) > topk

    def count_keys(pred):
        def chunk(c, cnt):
            hit = jnp.where(pred(keys_ref[c]), 1.0, 0.0)
            return cnt + jnp.sum(hit.reshape(tk // 8, 8, tq), axis=0)

        cnt = lax.fori_loop(0, used, chunk, jnp.zeros((8, tq), F32))
        return jnp.sum(cnt, axis=0, keepdims=True)

    @pl.when(jnp.logical_not(tied))
    def _():
        def write_chunk(c, _):
            o_ref[c] = jnp.where(keys_ref[c] >= thr, 0.0, NEG)
            return 0

        lax.fori_loop(0, used, write_chunk, 0)

    @pl.when(tied)
    def _():
        need = topk - count_keys(lambda k: k > thr)
        earlier = jnp.where(lax.broadcasted_iota(jnp.int32, (tk, tk), 1)
                            < lax.broadcasted_iota(jnp.int32, (tk, tk), 0), 1.0, 0.0).astype(BF16)

        def write_chunk(c, seen):
            keys = keys_ref[c]
            tie = jnp.where(keys == thr, 1.0, 0.0)
            rank = jnp.dot(earlier, tie.astype(BF16), preferred_element_type=F32) + seen
            o_ref[c] = jnp.where(keys > thr, 0.0,
                                 jnp.where(keys == thr, jnp.where(rank < need, 0.0, NEG), NEG))
            return seen + jnp.sum(tie, axis=0, keepdims=True)

        lax.fori_loop(0, used, write_chunk, jnp.zeros((1, tq), F32))

    def fill_chunk(c, _):
        o_ref[c] = jnp.full((tk, tq), NEG, F32)
        return 0

    lax.fori_loop(used, n_chunks, fill_chunk, 0)


def _select_bias(iq, ika, ikb, wt, topk, tq, tk):
    b, s, _ = iq.shape
    n_chunks = s // tk
    return pl.pallas_call(
        functools.partial(_select_body, tq=tq, tk=tk, n_chunks=n_chunks, topk=topk),
        grid=(b, s // tq),
        in_specs=[pl.BlockSpec((None, tq, IDX_HEADS * IDX_DIM), lambda bi, qi: (bi, qi, 0)),
                  pl.BlockSpec((None, s, LANES), lambda bi, qi: (bi, 0, 0)),
                  pl.BlockSpec((None, s, LANES), lambda bi, qi: (bi, 0, 0)),
                  pl.BlockSpec((None, IDX_HEADS, tq), lambda bi, qi: (bi, 0, qi))],
        out_specs=pl.BlockSpec((None, None, n_chunks, tk, tq), lambda bi, qi: (bi, qi, 0, 0, 0)),
        out_shape=jax.ShapeDtypeStruct((b, s // tq, n_chunks, tk, tq), F32),
        scratch_shapes=[pltpu.VMEM((n_chunks, tk, tq), jnp.int32),
                        pltpu.VMEM((n_chunks, tk, tq), jnp.int16)],
        compiler_params=_cparams("parallel", "parallel"),
        name="index_select",
    )(iq, ika, ikb, wt)


def _cross_body(q_ref, k_ref, v_ref, o_ref, *, n_heads):
    for h in range(n_heads):
        hs = slice(h * HEAD_DIM, (h + 1) * HEAD_DIM)
        s = lax.dot_general(q_ref[:, hs], k_ref[:, hs], (((1,), (1,)), ((), ())),
                            preferred_element_type=F32)
        p = jnp.exp2(s - jnp.max(s, axis=-1, keepdims=True))
        l = jnp.sum(p, axis=-1, keepdims=True)
        o = jnp.dot(p.astype(BF16), v_ref[:, hs], preferred_element_type=F32)
        o_ref[:, hs] = (o / l).astype(o_ref.dtype)


def _cross_attention(q_arr, q_blk, mk, mv, tq):
    b, s, _ = q_arr.shape
    n_mem = mk.shape[1]
    w = CROSS_HEADS * HEAD_DIM
    return pl.pallas_call(
        functools.partial(_cross_body, n_heads=CROSS_HEADS),
        grid=(b, s // tq),
        in_specs=[pl.BlockSpec((None, tq, w), lambda bi, qi: (bi, qi, q_blk)),
                  pl.BlockSpec((None, n_mem, w), lambda bi, qi: (bi, 0, 0)),
                  pl.BlockSpec((None, n_mem, w), lambda bi, qi: (bi, 0, 0))],
        out_specs=pl.BlockSpec((None, tq, w), lambda bi, qi: (bi, qi, 0)),
        out_shape=jax.ShapeDtypeStruct((b, s, w), BF16),
        compiler_params=_cparams("parallel", "parallel"),
        name="cross_attn",
    )(q_arr, mk, mv)


def _sigmoid(x):
    return 1.0 / (1.0 + jnp.exp(-x))


def _merge_body(h_ref, of_ref, od_ref, oc_ref, x_ref, wgf_ref, wgd_ref, wgc_ref,
                wbf_ref, wbd_ref, wbc_ref, wo_ref, o_ref):
    j = pl.program_id(1)
    h = h_ref[...]
    tn = wo_ref.shape[0]
    sub = min(tn, MERGE_SUB)

    def gate_logits(wg_ref, cs):
        return lax.dot_general(h, wg_ref[cs, :], (((1,), (1,)), ((), ())), preferred_element_type=F32)

    part = None
    for c in range(tn // sub):
        cs = slice(c * sub, (c + 1) * sub)
        zf, zd, zc = gate_logits(wgf_ref, cs), gate_logits(wgd_ref, cs), gate_logits(wgc_ref, cs)
        bf = jnp.dot(of_ref[...], wbf_ref[:, cs], preferred_element_type=F32)
        bd = jnp.dot(od_ref[...], wbd_ref[:, cs], preferred_element_type=F32)
        bc = jnp.dot(oc_ref[...], wbc_ref[:, cs], preferred_element_type=F32)
        merged = (_sigmoid(zf) * bf + _sigmoid(zd) * bd) + _sigmoid(zc) * bc
        p = jnp.dot(merged.astype(BF16), wo_ref[cs, :], preferred_element_type=F32)
        part = p if part is None else part + p

    @pl.when(j == 0)
    def _():
        o_ref[...] = x_ref[...] + part

    @pl.when(j != 0)
    def _():
        o_ref[...] += part


def _merge(h, o_fox, o_dsa, o_cross, x, w_gates, gate_row0, wbf, wbd, wbc, w_out, tm, tn):
    m, d = x.shape
    nj = d // tn
    assert gate_row0 % tn == 0
    g0 = gate_row0 // tn
    row = lambda i, j: (i, 0)
    in_specs = [pl.BlockSpec((tm, d), row),
                pl.BlockSpec((tm, o_fox.shape[1]), row),
                pl.BlockSpec((tm, o_dsa.shape[1]), row),
                pl.BlockSpec((tm, o_cross.shape[1]), row),
                pl.BlockSpec((tm, d), row),
                pl.BlockSpec((tn, d), lambda i, j: (g0 + j, 0)),
                pl.BlockSpec((tn, d), lambda i, j: (g0 + nj + j, 0)),
                pl.BlockSpec((tn, d), lambda i, j: (g0 + 2 * nj + j, 0)),
                pl.BlockSpec((wbf.shape[0], tn), lambda i, j: (0, j)),
                pl.BlockSpec((wbd.shape[0], tn), lambda i, j: (0, j)),
                pl.BlockSpec((wbc.shape[0], tn), lambda i, j: (0, j)),
                pl.BlockSpec((tn, d), lambda i, j: (j, 0))]
    return pl.pallas_call(
        _merge_body,
        grid=(m // tm, nj),
        in_specs=in_specs,
        out_specs=pl.BlockSpec((tm, d), row),
        out_shape=jax.ShapeDtypeStruct((m, d), F32),
        compiler_params=_cparams("parallel", "arbitrary"),
        name="gated_merge",
    )(h, o_fox, o_dsa, o_cross, x, w_gates, w_gates, w_gates, wbf, wbd, wbc, w_out)


def _router_body(x_ref, g_ref, w_ref, b_ref, h_ref, eid_ref, gate_ref):
    x = x_ref[...]
    r = lax.rsqrt(jnp.mean(x * x, axis=-1, keepdims=True) + EPS)
    h = (x * r) * g_ref[...]
    hb = h.astype(BF16)
    bits = lax.bitcast_convert_type(hb.astype(F32), jnp.int32)
    half = bits.shape[1] // 2
    h_ref[...] = lax.shift_right_logical(bits[:, :half], 16) | (bits[:, half:] & jnp.int32(-65536))
    logits = jnp.dot(hb, w_ref[...], preferred_element_type=F32) + b_ref[...]
    lane = lax.broadcasted_iota(jnp.int32, logits.shape, 1).astype(F32)

    def masked_max(mask):
        return jnp.max(jnp.where(mask, logits, -jnp.inf), axis=-1, keepdims=True)

    def first_lane(mask):
        return jnp.min(jnp.where(mask, lane, float(LANES)), axis=-1, keepdims=True)

    is_group = lane < N_GROUPS
    gmax = masked_max(is_group)
    grp = first_lane(is_group & (logits == gmax))
    denom = jnp.sum(jnp.where(is_group, jnp.exp(logits - gmax), 0.0), axis=-1, keepdims=True)
    p_sel = 1.0 / denom
    lo = N_GROUPS + grp * EXPERTS_PER_GROUP
    in_grp = (lane >= lo) & (lane < lo + EXPERTS_PER_GROUP)
    e1 = masked_max(in_grp)
    j1 = first_lane(in_grp & (logits == e1))
    rest = in_grp & (lane != j1)
    e2 = masked_max(rest)
    j2 = first_lane(rest & (logits == e2))
    t = jnp.exp(e2 - e1)
    g1 = p_sel * (1.0 / (1.0 + t))
    g2 = p_sel * (t / (1.0 + t))
    eid = jnp.where(lane == 0.0, j1 - N_GROUPS, jnp.where(lane == 1.0, j2 - N_GROUPS, 0.0))
    eid_ref[...] = eid.astype(jnp.int32)
    gate_ref[...] = jnp.where(lane == 0.0, g1, jnp.where(lane == 1.0, g2, 0.0))


def _router(x, g, w_router, b_router, tm):
    m, d = x.shape
    return pl.pallas_call(
        _router_body,
        grid=(m // tm,),
        in_specs=[pl.BlockSpec((tm, d), lambda i: (i, 0)),
                  pl.BlockSpec((1, d), lambda i: (0, 0)),
                  pl.BlockSpec((d, LANES), lambda i: (0, 0)),
                  pl.BlockSpec((1, LANES), lambda i: (0, 0))],
        out_specs=[pl.BlockSpec((tm, d // 2), lambda i: (i, 0)),
                   pl.BlockSpec((tm, LANES), lambda i: (i, 0)),
                   pl.BlockSpec((tm, LANES), lambda i: (i, 0))],
        out_shape=[jax.ShapeDtypeStruct((m, d // 2), jnp.int32),
                   jax.ShapeDtypeStruct((m, LANES), jnp.int32),
                   jax.ShapeDtypeStruct((m, LANES), F32)],
        compiler_params=_cparams("parallel"),
        name="moe_router",
    )(x, g.reshape(1, d), w_router, b_router.reshape(1, LANES))


def _expert_body(ge_ref, gb0_ref, gnb_ref, used_ref, tok_ref, h_hbm, wg_ref, wu_ref, wd_ref, ys_hbm,
                 xs_ref, xb_ref, acc_ref, wgb_ref, wub_ref, wdb_ref, sem_in, sem_out,
                 *, bm, n_f, n_blocks):
    g = pl.program_id(0)
    f = pl.program_id(1)
    nb = gnb_ref[g]
    blk0 = gb0_ref[g]

    @pl.when(jnp.logical_and(g == 0, f == 0))
    def _():
        acc_ref[pl.ds(0, bm)] = jnp.zeros((bm,) + acc_ref.shape[1:], F32)

        def tail_copy(i):
            return pltpu.make_async_copy(acc_ref.at[pl.ds(0, bm)], ys_hbm.at[pl.ds(i * bm, bm)],
                                         sem_out)

        def start_tail(i, _):
            tail_copy(i).start()
            return 0

        lax.fori_loop(used_ref[0], n_blocks, start_tail, 0)

        def wait_tail(i, _):
            tail_copy(i).wait()
            return 0

        lax.fori_loop(used_ref[0], n_blocks, wait_tail, 0)

    def gather(grp):
        slot = grp % 2
        base = gb0_ref[grp] * bm

        def issue(r0, _):
            for u in range(DMA_ISSUE_UNROLL):
                r = r0 * DMA_ISSUE_UNROLL + u
                pltpu.make_async_copy(h_hbm.at[pl.ds(tok_ref[base + r], 1)],
                                      xs_ref.at[slot, pl.ds(r, 1)], sem_in.at[slot]).start()
            return 0

        lax.fori_loop(0, gnb_ref[grp] * (bm // DMA_ISSUE_UNROLL), issue, 0)

    def block_copy_out(i):
        return pltpu.make_async_copy(acc_ref.at[pl.ds(i * bm, bm)],
                                     ys_hbm.at[pl.ds((blk0 + i) * bm, bm)], sem_out)

    @pl.when(jnp.logical_and(g == 0, f == 0))
    def _():
        gather(g)

    @pl.when(jnp.logical_and(nb > 0, f == 0))
    def _():
        slot = g % 2

        def land(i, _):
            rows = pl.ds(pl.multiple_of(i * bm, bm), bm)
            pltpu.make_async_copy(h_hbm.at[pl.ds(0, bm)], xs_ref.at[slot, rows], sem_in.at[slot]).wait()
            return 0

        lax.fori_loop(0, nb, land, 0)

        half = xb_ref.shape[1] // 2

        def unpack(i, _):
            rows = pl.ds(pl.multiple_of(i * bm, bm), bm)
            w = xs_ref[slot, rows, :]
            lo = lax.bitcast_convert_type(lax.shift_left(w, 16), F32)
            hi = lax.bitcast_convert_type(w & jnp.int32(-65536), F32)
            xb_ref[rows, :half] = lo.astype(BF16)
            xb_ref[rows, half:] = hi.astype(BF16)
            return 0

        lax.fori_loop(0, nb, unpack, 0)

    @pl.when(jnp.logical_and(f == 0, g + 1 < pl.num_programs(0)))
    def _():
        gather(g + 1)

    def wait_out(n):
        def body(i, _):
            pltpu.make_async_copy(acc_ref.at[pl.ds(0, bm)], ys_hbm.at[pl.ds(0, bm)], sem_out).wait()
            return 0

        lax.fori_loop(0, n, body, 0)

    @pl.when(jnp.logical_and(f == 0, g > 0))
    def _():
        wait_out(gnb_ref[jnp.maximum(g - 1, 0)])

    @pl.when(nb > 0)
    def _():
        wgb_ref[...] = wg_ref[...].astype(BF16)
        wub_ref[...] = wu_ref[...].astype(BF16)
        wdb_ref[...] = wd_ref[...].astype(BF16)

        def swiglu(rows):
            xb = xb_ref[rows, :]
            a = jnp.dot(xb, wgb_ref[...], preferred_element_type=F32)
            u = jnp.dot(xb, wub_ref[...], preferred_element_type=F32)
            hid = (a * _sigmoid(a)) * u
            y = jnp.dot(hid.astype(BF16), wdb_ref[...], preferred_element_type=F32)

            @pl.when(f == 0)
            def _():
                acc_ref[rows, :] = y

            @pl.when(f != 0)
            def _():
                acc_ref[rows, :] += y

        per = MOE_MM_ROWS // bm

        def full_tile(i, _):
            swiglu(pl.ds(pl.multiple_of(i * MOE_MM_ROWS, MOE_MM_ROWS), MOE_MM_ROWS))
            return 0

        lax.fori_loop(0, nb // per, full_tile, 0)
        piece = per // 2
        while piece >= 1:
            done = (nb // (2 * piece)) * (2 * piece)

            @pl.when((nb // piece) % 2 == 1)
            def _(done=done, piece=piece):
                swiglu(pl.ds(pl.multiple_of(done * bm, piece * bm), piece * bm))

            piece //= 2

    @pl.when(jnp.logical_and(nb > 0, f == n_f - 1))
    def _():
        def start_out(i, _):
            block_copy_out(i).start()
            return 0

        lax.fori_loop(0, nb, start_out, 0)

        @pl.when(g == pl.num_programs(0) - 1)
        def _():
            wait_out(nb)


def _experts(h2, slot_tok, grp_e, grp_blk0, grp_nb, n_used, w_gate, w_up, w_down, n_slots):
    d = w_gate.shape[1]
    ff = w_gate.shape[2]
    n_f = ff // MOE_FC
    n_groups = grp_e.shape[0]
    rmax = MOE_GROUP_BLOCKS * MOE_BM

    def f_idx(g, f, gnb):
        return jnp.where(gnb[g] > 0, f, n_f - 1)

    grid_spec = pltpu.PrefetchScalarGridSpec(
        num_scalar_prefetch=5,
        grid=(n_groups, n_f),
        in_specs=[
            pl.BlockSpec(memory_space=pl.ANY),
            pl.BlockSpec((None, d, MOE_FC), lambda g, f, ge, gb0, gnb, *_: (ge[g], 0, f_idx(g, f, gnb))),
            pl.BlockSpec((None, d, MOE_FC), lambda g, f, ge, gb0, gnb, *_: (ge[g], 0, f_idx(g, f, gnb))),
            pl.BlockSpec((None, MOE_FC, d), lambda g, f, ge, gb0, gnb, *_: (ge[g], f_idx(g, f, gnb), 0)),
        ],
        out_specs=pl.BlockSpec(memory_space=pl.ANY),
        scratch_shapes=[pltpu.VMEM((2, rmax, d // 2), jnp.int32),
                        pltpu.VMEM((rmax, d), BF16),
                        pltpu.VMEM((rmax, d), F32),
                        pltpu.VMEM((d, MOE_FC), BF16),
                        pltpu.VMEM((d, MOE_FC), BF16),
                        pltpu.VMEM((MOE_FC, d), BF16),
                        pltpu.SemaphoreType.DMA((2,)),
                        pltpu.SemaphoreType.DMA(())],
    )
    return pl.pallas_call(
        functools.partial(_expert_body, bm=MOE_BM, n_f=n_f, n_blocks=n_slots // MOE_BM),
        grid_spec=grid_spec,
        out_shape=jax.ShapeDtypeStruct((n_slots, d), F32),
        compiler_params=_cparams("arbitrary", "arbitrary"),
        name="moe_experts",
    )(grp_e, grp_blk0, grp_nb, n_used, slot_tok, h2, w_gate, w_up, w_down)


def _combine_body(pos_ref, x_ref, gate_ref, ys_hbm, o_ref, buf_ref, sem, *, tm):
    i = pl.program_id(0)
    n_steps = pl.num_programs(0)

    def issue(step):
        slot = step % 2

        def body(r0, _):
            for u in range(DMA_ISSUE_UNROLL):
                r = r0 * DMA_ISSUE_UNROLL + u
                for k in range(MOE_TOP_K):
                    src = pos_ref[(step * tm + r) * MOE_TOP_K + k]
                    pltpu.make_async_copy(ys_hbm.at[pl.ds(src, 1)], buf_ref.at[slot, k, pl.ds(r, 1)],
                                          sem.at[slot]).start(priority=k)
            return 0

        lax.fori_loop(0, tm // DMA_ISSUE_UNROLL, body, 0)

    @pl.when(i == 0)
    def _():
        issue(i)

    @pl.when(i + 1 < n_steps)
    def _():
        issue(i + 1)

    slot = i % 2
    for k in range(MOE_TOP_K):
        pltpu.make_async_copy(ys_hbm.at[pl.ds(0, tm)], buf_ref.at[slot, k], sem.at[slot]).wait()
    o_ref[...] = x_ref[...] + (buf_ref[slot, 0] * gate_ref[:, 0:1] + buf_ref[slot, 1] * gate_ref[:, 1:2])


def _combine(x, gate, ys, pos, tm):
    m, d = x.shape
    grid_spec = pltpu.PrefetchScalarGridSpec(
        num_scalar_prefetch=1,
        grid=(m // tm,),
        in_specs=[pl.BlockSpec((tm, d), lambda i, pos: (i, 0)),
                  pl.BlockSpec((tm, LANES), lambda i, pos: (i, 0)),
                  pl.BlockSpec(memory_space=pl.ANY)],
        out_specs=pl.BlockSpec((tm, d), lambda i, pos: (i, 0)),
        scratch_shapes=[pltpu.VMEM((2, MOE_TOP_K, tm, d), F32), pltpu.SemaphoreType.DMA((2,))],
    )
    return pl.pallas_call(
        functools.partial(_combine_body, tm=tm),
        grid_spec=grid_spec,
        out_shape=jax.ShapeDtypeStruct((m, d), F32),
        compiler_params=_cparams("arbitrary"),
        name="moe_combine",
    )(pos, x, gate, ys)


PLAN_CHUNK = 256


def _positions_body(eid_ref, pos_ref, tok_ref, cnt_ref, prefix_ref, *, n_tokens):
    ch = PLAN_CHUNK
    lane = lax.broadcasted_iota(jnp.int32, (ch, LANES), 1)
    earlier = jnp.where(lax.broadcasted_iota(jnp.int32, (ch, ch), 1)
                        < lax.broadcasted_iota(jnp.int32, (ch, ch), 0), 1.0, 0.0).astype(BF16)

    def chosen(e, k):
        return jnp.where(lane == e[:, k:k + 1], 1.0, 0.0)

    def count_step(c, seen):
        rows = pl.ds(pl.multiple_of(c * ch, ch), ch)
        e = eid_ref[rows, :]
        both = chosen(e, 0) + chosen(e, 1)
        prefix_ref[rows, :] = jnp.dot(earlier, both.astype(BF16), preferred_element_type=F32) + seen
        return seen + jnp.sum(both, axis=0, keepdims=True)

    counts = lax.fori_loop(0, n_tokens // ch, count_step, jnp.zeros((1, LANES), F32))
    blocks = jnp.floor((counts + (MOE_BM - 1)) * (1.0 / MOE_BM))
    lower = jnp.where(lax.broadcasted_iota(jnp.int32, (LANES, LANES), 0)
                      < lax.broadcasted_iota(jnp.int32, (LANES, LANES), 1), 1.0, 0.0).astype(BF16)
    first_block = jnp.dot(jnp.broadcast_to(blocks, (8, LANES)).astype(BF16), lower,
                          preferred_element_type=F32)[0:1, :]
    first_row = first_block * MOE_BM

    slot_rows = tok_ref.shape[0]
    row_id = lax.broadcasted_iota(jnp.int32, (slot_rows, ch), 0).astype(F32)
    tok_row = lax.broadcasted_iota(jnp.int32, (ch, 1), 0)

    def place_step(c, acc):
        acc_hi, acc_lo = acc
        rows = pl.ds(pl.multiple_of(c * ch, ch), ch)
        e = eid_ref[rows, :]
        slot = prefix_ref[rows, :] + first_row
        tok = c * ch + tok_row
        tok_hi = lax.shift_right_logical(tok, 7).astype(F32)
        tok_lo = (tok & (LANES - 1)).astype(F32)
        p = []
        for k in range(MOE_TOP_K):
            pk = jnp.sum(chosen(e, k) * slot, axis=1, keepdims=True)
            p.append(pk)
            srow = jnp.floor(pk * (1.0 / LANES))
            scol = pk - srow * LANES
            srow_t = jnp.transpose(jnp.broadcast_to(srow, (ch, LANES)))[0:1, :]
            onehot_rows = jnp.where(row_id == srow_t, 1.0, 0.0).astype(BF16)
            in_lane = lane.astype(F32) == scol
            acc_hi = acc_hi + jnp.dot(onehot_rows, jnp.where(in_lane, tok_hi, 0.0).astype(BF16),
                                      preferred_element_type=F32)
            acc_lo = acc_lo + jnp.dot(onehot_rows, jnp.where(in_lane, tok_lo, 0.0).astype(BF16),
                                      preferred_element_type=F32)
        pos_ref[rows, :] = jnp.where(lane == 0, p[0], jnp.where(lane == 1, p[1], 0.0)).astype(jnp.int32)
        return acc_hi, acc_lo

    zero = jnp.zeros((slot_rows, LANES), F32)
    acc_hi, acc_lo = lax.fori_loop(0, n_tokens // ch, place_step, (zero, zero))
    tok_ref[...] = (acc_hi * LANES + acc_lo).astype(jnp.int32)
    cnt_ref[...] = jnp.broadcast_to(counts, cnt_ref.shape).astype(jnp.int32)


def _positions(eid, n_slots):
    n = eid.shape[0]
    assert n % PLAN_CHUNK == 0 and MOE_BM & (MOE_BM - 1) == 0
    assert n_slots % LANES == 0 and n <= LANES * LANES
    slot_rows = -(-(n_slots // LANES) // LANES) * LANES
    pos, tok, cnt = pl.pallas_call(
        functools.partial(_positions_body, n_tokens=n),
        out_shape=[jax.ShapeDtypeStruct((n, LANES), jnp.int32),
                   jax.ShapeDtypeStruct((slot_rows, LANES), jnp.int32),
                   jax.ShapeDtypeStruct((8, LANES), jnp.int32)],
        scratch_shapes=[pltpu.VMEM((n, LANES), F32)],
        compiler_params=pltpu.CompilerParams(vmem_limit_bytes=VMEM_LIMIT),
        name="moe_positions",
    )(eid)
    return pos, tok.reshape(-1)[:n_slots], cnt[0, :N_EXPERTS]


def _moe_plan(eid, n_tokens):
    a = n_tokens * MOE_TOP_K
    n_blocks = (a + N_EXPERTS * (MOE_BM - 1)) // MOE_BM
    n_slots = n_blocks * MOE_BM
    n_groups = n_blocks // MOE_GROUP_BLOCKS + N_EXPERTS
    pos, slot_tok, counts = _positions(eid, n_slots)
    pos = pos[:, :MOE_TOP_K].reshape(a)
    blocks_e = (counts + MOE_BM - 1) // MOE_BM
    blk_start = jnp.cumsum(blocks_e) - blocks_e
    groups_e = (blocks_e + MOE_GROUP_BLOCKS - 1) // MOE_GROUP_BLOCKS
    grp_end = jnp.cumsum(groups_e)
    grp_start = grp_end - groups_e
    gidx = jnp.arange(n_groups, dtype=jnp.int32)
    ge = jnp.minimum(jnp.searchsorted(grp_end, gidx, side="right"), N_EXPERTS - 1).astype(jnp.int32)
    local = gidx - grp_start[ge]
    active = gidx < grp_end[-1]
    gb0 = jnp.where(active, blk_start[ge] + local * MOE_GROUP_BLOCKS, 0).astype(jnp.int32)
    gnb = jnp.where(active, jnp.minimum(blocks_e[ge] - local * MOE_GROUP_BLOCKS, MOE_GROUP_BLOCKS),
                    0).astype(jnp.int32)
    last_e = ge[jnp.maximum(grp_end[-1] - 1, 0)]
    ge = jnp.where(active, ge, last_e).astype(jnp.int32)
    n_used = jnp.sum(blocks_e).astype(jnp.int32).reshape(1)
    return pos, slot_tok, ge, gb0, gnb, n_used, n_slots


def _layer(x, mem, positions, attn_norm_g, mem_norm_g, w_in, fox_forget_b, fox_q_norm_g,
           fox_k_norm_g, dsa_q_norm_g, dsa_k_norm_g, idx_k_norm_g, cross_q_norm_g, cross_k_norm_g,
           w_mem_kv, w_branch_fox, w_branch_dsa, w_branch_cross, w_out, ffn_norm_g,
           router_group_w, router_group_b, router_expert_w, router_expert_b,
           expert_w_gate, expert_w_up, expert_w_down):
    b, s, d = x.shape
    n = b * s
    n_mem = mem.shape[1]
    fw, dw, cw = FOX_HEADS * HEAD_DIM, DSA_HEADS * HEAD_DIM, CROSS_HEADS * HEAD_DIM
    iqw = IDX_HEADS * IDX_DIM
    tm = min(512, n)
    x2d = x.reshape(n, d)

    sizes = (fw, fw, fw, FOX_HEADS, dw, dw, dw, iqw, IDX_HEADS, IDX_DIM, cw, 3 * d)
    offs = [0]
    for sz in sizes:
        offs.append(offs[-1] + sz)
    (c_fq, c_fk, c_fv, c_fl, c_dq, c_dk, c_dv, c_iq, c_iw, c_ik, c_cq, c_g) = range(12)

    w_t = jnp.transpose(w_in, (2, 0, 1)).reshape(offs[-1], d)
    order = (c_dq, c_dk, c_fv, c_dv, c_iq, c_fq, c_fk, c_cq, c_g)
    row0, r = {}, 0
    for c in order:
        row0[c] = r
        r += sizes[c]
    w_all = jnp.concatenate([w_t[offs[c]:offs[c + 1]] for c in order], axis=0).astype(BF16)
    small_rows = [w_t[offs[c]:offs[c + 1]] for c in (c_ik, c_iw, c_fl)]
    small_rows.append(jnp.zeros((LANES - IDX_DIM - IDX_HEADS - FOX_HEADS, d), F32))
    w_small = jnp.concatenate(small_rows, axis=0).astype(BF16)
    q_unit = (HEAD_DIM ** -0.5) * LOG2E
    g_qk1 = jnp.concatenate([jnp.tile(fox_q_norm_g * q_unit, FOX_HEADS),
                             jnp.tile(fox_k_norm_g, FOX_HEADS),
                             jnp.tile(cross_q_norm_g * q_unit, CROSS_HEADS)])
    g_qk2 = jnp.concatenate([jnp.tile(dsa_q_norm_g * q_unit, DSA_HEADS),
                             jnp.tile(dsa_k_norm_g, DSA_HEADS)])
    g_ik = jnp.concatenate([idx_k_norm_g, jnp.zeros((LANES - IDX_DIM,), F32)])
    fb_lane = jnp.zeros((LANES,), F32).at[SMALL_F:SMALL_F + FOX_HEADS].set(fox_forget_b)

    pos_col = positions.reshape(n, 1).astype(F32)
    cd, sd, ci, si = _rope_tables(pos_col, tm)

    h = _rmsnorm(x2d, attn_norm_g, BF16, tm)
    tmp = min(PROJ_TM, n)
    qk1 = _proj(h, w_all, "norm", BF16, tm, 2 * fw + cw, gains=g_qk1, w_rows=(row0[c_fq], 2 * fw + cw))
    qk2 = _proj(h, w_all, "norm_rope", BF16, min(PROJ_ROPE_TM, n), 2 * dw, gains=g_qk2, tables=(cd, sd),
                w_rows=(row0[c_dq], 2 * dw))
    iq = _proj(h, w_all, "rope_idx", BF16, tmp, 1024, tables=(ci, si), w_rows=(row0[c_iq], iqw))
    small = _proj(h, w_small, "raw", F32, tm, LANES, w_rows=(0, LANES))
    ika, ikb, wl, lf_col = _small(small, g_ik, fb_lane, ci, si, tm)

    tq, tk = min(ATT_TQ, s), min(ATT_TK, s)
    assert tq == tk and s % tq == 0
    cum_col = _forget_cumsum(lf_col.reshape(b, s, LANES))
    qk1 = qk1.reshape(b, s, 2 * fw + cw)
    qk2 = qk2.reshape(b, s, 2 * dw)
    q_aug, k_aug = _fox_pack(qk1, cum_col, FOX_HEADS, min(512, s))
    vt = _proj_t(h, w_all, (row0[c_fv], fw + dw), tmp, 768, tk).reshape(b, s // tk, fw + dw, tk)
    o_fox = _attention(q_aug, 0, k_aug, 0, vt, 0, (), "fox", FOX_HEADS, AUG_DIM, tq, tk)

    topk = min(INDEX_TOPK, s // 4)
    wt = jnp.transpose(wl.reshape(b, s, LANES)[:, :, :IDX_HEADS], (0, 2, 1))
    bias = _select_bias(iq.reshape(b, s, iqw), ika.reshape(b, s, LANES), ikb.reshape(b, s, LANES),
                        wt, topk, tq, tk)
    o_dsa = _attention(qk2, 0, qk2, 1, vt, 1, (bias,), "dsa", DSA_HEADS, HEAD_DIM, tq, tk)

    tmm = min(512, b * n_mem)
    m_n = _rmsnorm(mem.reshape(b * n_mem, d), mem_norm_g, BF16, tmm)
    w_kv = w_mem_kv.astype(BF16)
    mk = _proj(m_n, w_kv[:, :cw], "norm", BF16, tmm, cw, gains=jnp.tile(cross_k_norm_g, CROSS_HEADS))
    mv = _proj(m_n, w_kv[:, cw:], "raw", BF16, tmm, cw)
    o_cross = _cross_attention(qk1, (2 * fw) // cw, mk.reshape(b, n_mem, cw), mv.reshape(b, n_mem, cw),
                               min(512, s))

    x_mid = _merge(h, o_fox.reshape(n, fw), o_dsa.reshape(n, dw), o_cross.reshape(n, cw), x2d,
                   w_all, row0[c_g], w_branch_fox.astype(BF16), w_branch_dsa.astype(BF16),
                   w_branch_cross.astype(BF16), w_out.astype(BF16), tm, MERGE_TN)

    w_router = jnp.concatenate(
        [router_group_w, router_expert_w, jnp.zeros((d, LANES - N_GROUPS - N_EXPERTS), F32)],
        axis=1).astype(BF16)
    b_router = jnp.concatenate(
        [router_group_b, router_expert_b, jnp.zeros((LANES - N_GROUPS - N_EXPERTS,), F32)])
    h2, eid, gate = _router(x_mid, ffn_norm_g, w_router, b_router, tm)
    pos, slot_tok, ge, gb0, gnb, n_used, n_slots = _moe_plan(eid, n)
    ys = _experts(h2, slot_tok, ge, gb0, gnb, n_used, expert_w_gate, expert_w_up, expert_w_down,
                  n_slots)
    out = _combine(x_mid, gate, ys, pos, min(256, n))
    return out.reshape(b, s, d)


def kernel(x, mem, positions, attn_norm_g, mem_norm_g, w_in, fox_forget_b, fox_q_norm_g, fox_k_norm_g, dsa_q_norm_g, dsa_k_norm_g, idx_k_norm_g, cross_q_norm_g, cross_k_norm_g, w_mem_kv, w_branch_fox, w_branch_dsa, w_branch_cross, w_out, ffn_norm_g, router_group_w, router_group_b, router_expert_w, router_expert_b, expert_w_gate, expert_w_up, expert_w_down):
    depth = w_in.shape[0]
    for layer in range(depth):
        x = _layer(
            x, mem, positions, attn_norm_g[layer], mem_norm_g[layer], w_in[layer:layer + 1],
            fox_forget_b[layer], fox_q_norm_g[layer], fox_k_norm_g[layer],
            dsa_q_norm_g[layer], dsa_k_norm_g[layer], idx_k_norm_g[layer],
            cross_q_norm_g[layer], cross_k_norm_g[layer], w_mem_kv[layer],
            w_branch_fox[layer], w_branch_dsa[layer], w_branch_cross[layer], w_out[layer],
            ffn_norm_g[layer], router_group_w[layer], router_group_b[layer],
            router_expert_w[layer], router_expert_b[layer], expert_w_gate[layer],
            expert_w_up[layer], expert_w_down[layer])
    return x
```

```python
import functools

import jax
import jax.numpy as jnp
from jax import lax
from jax.experimental import pallas as pl
from jax.experimental.pallas import tpu as pltpu

F32 = jnp.float32
BF16 = jnp.bfloat16

LANES = 128
HEAD_DIM = 128
FOX_HEADS = 6
DSA_HEADS = 6
CROSS_HEADS = 4
IDX_HEADS = 16
IDX_DIM = 64
ROPE_THETA = 500000.0
ROT_DIM = HEAD_DIM // 4
IDX_ROT_DIM = IDX_DIM // 4
INDEX_TOPK = 256
N_GROUPS = 4
EXPERTS_PER_GROUP = 8
N_EXPERTS = N_GROUPS * EXPERTS_PER_GROUP
MOE_TOP_K = 2
EPS = 1e-6

NEG = -1e30
INT_MIN = -(2 ** 31)
VMEM_LIMIT = 56 * 1024 * 1024

ATT_TQ = 512
ATT_TK = 512
ATT_HEADS_PER_LOOP = 6
PROJ_TM = 1024
PROJ_ROPE_TM = 256
MERGE_TN = 512
MERGE_SUB = 256
MOE_BM = 128
MOE_GROUP_BLOCKS = 8
MOE_MM_ROWS = 256
MOE_FC = 512
DMA_ISSUE_UNROLL = 8


def _cparams(*sem):
    return pltpu.CompilerParams(dimension_semantics=sem, vmem_limit_bytes=VMEM_LIMIT)


def _rmsnorm_body(x_ref, g_ref, o_ref):
    x = x_ref[...]
    r = lax.rsqrt(jnp.mean(x * x, axis=-1, keepdims=True) + EPS)
    o_ref[...] = ((x * r) * g_ref[...]).astype(o_ref.dtype)


def _rmsnorm(x2d, g, out_dtype, tm):
    m, d = x2d.shape
    return pl.pallas_call(
        _rmsnorm_body,
        grid=(m // tm,),
        in_specs=[pl.BlockSpec((tm, d), lambda i: (i, 0)),
                  pl.BlockSpec((1, d), lambda i: (0, 0))],
        out_specs=pl.BlockSpec((tm, d), lambda i: (i, 0)),
        out_shape=jax.ShapeDtypeStruct((m, d), out_dtype),
        compiler_params=_cparams("parallel"),
        name="rmsnorm",
    )(x2d, g.reshape(1, d))


IDX_FREQ_LANE = 32


def _rope_table_body(pos_ref, f_ref, cd_ref, sd_ref, ci_ref, si_ref):
    ang = pos_ref[...] * f_ref[...]
    c = jnp.cos(ang)
    s = jnp.sin(ang)
    lane = lax.broadcasted_iota(jnp.int32, c.shape, 1)
    hd = ROT_DIM // 2
    cd_ref[...] = jnp.where(lane < hd, c, jnp.where(lane < ROT_DIM, pltpu.roll(c, hd, 1), 1.0))
    sd_ref[...] = jnp.where(lane < hd, -s, jnp.where(lane < ROT_DIM, pltpu.roll(s, hd, 1), 0.0))
    m = lane & (IDX_DIM - 1)
    hi = IDX_ROT_DIM // 2
    ci_src = pltpu.roll(c, LANES - IDX_FREQ_LANE, 1)
    si_src = pltpu.roll(s, LANES - IDX_FREQ_LANE, 1)
    ci_ref[...] = jnp.where(m < IDX_ROT_DIM, ci_src, 1.0)
    si_ref[...] = jnp.where(m < hi, -si_src, jnp.where(m < IDX_ROT_DIM, si_src, 0.0))


def _rope_tables(pos_col, tm):
    n = pos_col.shape[0]

    def inv_freq(rot_dim):
        half = rot_dim // 2
        return jnp.power(jnp.float32(ROPE_THETA), -jnp.arange(half, dtype=F32) * 2.0 / rot_dim)

    fi = jnp.tile(inv_freq(IDX_ROT_DIM), 2)
    freq = jnp.zeros((LANES,), F32).at[:ROT_DIM // 2].set(inv_freq(ROT_DIM))
    for base in (IDX_FREQ_LANE, IDX_FREQ_LANE + IDX_DIM):
        freq = freq.at[base:base + IDX_ROT_DIM].set(fi)
    tab = pl.BlockSpec((tm, LANES), lambda i: (i, 0))
    return pl.pallas_call(
        _rope_table_body,
        grid=(n // tm,),
        in_specs=[pl.BlockSpec((tm, 1), lambda i: (i, 0)), pl.BlockSpec((1, LANES), lambda i: (0, 0))],
        out_specs=[tab] * 4,
        out_shape=[jax.ShapeDtypeStruct((n, LANES), F32)] * 4,
        compiler_params=_cparams("parallel"),
        name="rope_tables",
    )(pos_col, freq.reshape(1, LANES))


def _apply_rope(a, c, s, half, period):
    lane = lax.broadcasted_iota(jnp.int32, a.shape, 1)
    first = (lane & (period - 1)) < half
    partner = jnp.where(first, pltpu.roll(a, LANES - half, 1), pltpu.roll(a, half, 1))
    return a * c + partner * s


def _proj_body(*refs, mode, n_lane_groups, w_rows):
    h_ref, w_ref = refs[0], refs[1]
    o_ref = refs[-1]

    if w_rows:
        acc = lax.dot_general(h_ref[...], w_ref[...], (((1,), (1,)), ((), ())),
                              preferred_element_type=F32)
    else:
        acc = jnp.dot(h_ref[...], w_ref[...], preferred_element_type=F32)
    if mode == "raw":
        o_ref[...] = acc.astype(o_ref.dtype)
        return
    for g in range(n_lane_groups):
        sl = slice(g * LANES, (g + 1) * LANES)
        a = acc[:, sl]
        if mode in ("norm", "norm_rope"):
            g_ref = refs[2]
            a = (a * lax.rsqrt(jnp.mean(a * a, axis=-1, keepdims=True) + EPS)) * g_ref[:, sl]
        if mode == "norm_rope":
            a = _apply_rope(a, refs[3][...], refs[4][...], ROT_DIM // 2, HEAD_DIM)
        if mode == "rope_idx":
            a = _apply_rope(a, refs[2][...], refs[3][...], IDX_ROT_DIM // 2, IDX_DIM)
        o_ref[:, sl] = a.astype(o_ref.dtype)


def _proj_t_body(h_ref, w_ref, o_ref, *, tk):
    acc = lax.dot_general(w_ref[...], h_ref[...], (((1,), (1,)), ((), ())),
                          preferred_element_type=F32)
    for q in range(o_ref.shape[0]):
        o_ref[q] = acc[:, q * tk:(q + 1) * tk].astype(o_ref.dtype)


def _proj_t(h, w, w_rows, tm, tn, tk):
    m, k = h.shape
    row0, n = w_rows
    assert row0 % tn == 0 and n % tn == 0 and tm % tk == 0
    return pl.pallas_call(
        functools.partial(_proj_t_body, tk=tk),
        grid=(n // tn, m // tm),
        in_specs=[pl.BlockSpec((tm, k), lambda j, i: (i, 0)),
                  pl.BlockSpec((tn, k), lambda j, i: (row0 // tn + j, 0))],
        out_specs=pl.BlockSpec((tm // tk, tn, tk), lambda j, i: (i, j, 0)),
        out_shape=jax.ShapeDtypeStruct((m // tk, n, tk), BF16),
        compiler_params=_cparams("parallel", "parallel"),
        name="proj_t",
    )(h, w)


def _proj(h, w, mode, out_dtype, tm, tn, gains=None, tables=None, w_rows=None):
    m, k = h.shape
    if w_rows is None:
        n = w.shape[1]
        w_spec = pl.BlockSpec((k, tn), lambda j, i: (0, j))
    else:
        row0, n = w_rows
        assert row0 % tn == 0 and n % tn == 0
        w_spec = pl.BlockSpec((tn, k), lambda j, i: (row0 // tn + j, 0))
    in_specs = [pl.BlockSpec((tm, k), lambda j, i: (i, 0)), w_spec]
    args = [h, w]
    if gains is not None:
        in_specs.append(pl.BlockSpec((1, tn), lambda j, i: (0, j)))
        args.append(gains.reshape(1, n))
    if tables is not None:
        for t in tables:
            in_specs.append(pl.BlockSpec((tm, LANES), lambda j, i: (i, 0)))
            args.append(t)
    return pl.pallas_call(
        functools.partial(_proj_body, mode=mode, n_lane_groups=tn // LANES,
                          w_rows=w_rows is not None),
        grid=(n // tn, m // tm),
        in_specs=in_specs,
        out_specs=pl.BlockSpec((tm, tn), lambda j, i: (i, j)),
        out_shape=jax.ShapeDtypeStruct((m, n), out_dtype),
        compiler_params=_cparams("parallel", "parallel"),
        name="proj_" + mode,
    )(*args)


SMALL_IK = 0
SMALL_IW = 64
SMALL_F = 80


def _small_body(x_ref, gk_ref, fb_ref, c_ref, s_ref, ika_ref, ikb_ref, wl_ref, lf_ref):
    x = x_ref[...]
    lane = lax.broadcasted_iota(jnp.int32, x.shape, 1)
    is_ik = lane < IDX_DIM
    ik = jnp.where(is_ik, x, 0.0)
    ms = jnp.sum(ik * ik, axis=-1, keepdims=True) * (1.0 / IDX_DIM)
    ik = (ik * lax.rsqrt(ms + EPS)) * gk_ref[...]
    ik = _apply_rope(ik, c_ref[...], s_ref[...], IDX_ROT_DIM // 2, IDX_DIM)
    ik = jnp.where(is_ik, ik, 0.0)
    ika_ref[...] = ik.astype(BF16)
    ikb_ref[...] = pltpu.roll(ik, IDX_DIM, 1).astype(BF16)
    wl_ref[...] = pltpu.roll(x, LANES - SMALL_IW, 1) * (2.0 ** -5)
    z = x + fb_ref[...]
    log_f = jnp.minimum(z, 0.0) - jnp.log1p(jnp.exp(-jnp.abs(z)))
    lf_ref[...] = pltpu.roll(log_f, LANES - SMALL_F, 1)


def _small(x, idx_k_gain_lane, forget_b_lane, ci, si, tm):
    n = x.shape[0]
    row = pl.BlockSpec((1, LANES), lambda i: (0, 0))
    tab = pl.BlockSpec((tm, LANES), lambda i: (i, 0))
    return pl.pallas_call(
        _small_body,
        grid=(n // tm,),
        in_specs=[tab, row, row, tab, tab],
        out_specs=[tab, tab, tab, tab],
        out_shape=[jax.ShapeDtypeStruct((n, LANES), BF16), jax.ShapeDtypeStruct((n, LANES), BF16),
                   jax.ShapeDtypeStruct((n, LANES), F32), jax.ShapeDtypeStruct((n, LANES), F32)],
        compiler_params=_cparams("parallel"),
        name="small_cols",
    )(x, idx_k_gain_lane.reshape(1, LANES), forget_b_lane.reshape(1, LANES), ci, si)


def _cumsum_body(col_ref, ccol_ref, *, seq):
    c = col_ref[...]
    ci = lax.broadcasted_iota(jnp.int32, c.shape, 0)
    sh = 1
    while sh < seq:
        c = c + jnp.where(ci >= sh, pltpu.roll(c, sh, 0), 0.0)
        sh *= 2
    ccol_ref[...] = c


def _forget_cumsum(lf_col):
    b, s, _ = lf_col.shape
    return pl.pallas_call(
        functools.partial(_cumsum_body, seq=s),
        grid=(b,),
        in_specs=[pl.BlockSpec((None, s, LANES), lambda i: (i, 0, 0))],
        out_specs=pl.BlockSpec((None, s, LANES), lambda i: (i, 0, 0)),
        out_shape=jax.ShapeDtypeStruct((b, s, LANES), F32),
        compiler_params=_cparams("parallel"),
        name="forget_cumsum",
    )(lf_col)


AUG_DIM = 2 * HEAD_DIM
LOG2E = 1.4426950408889634


def _split3(x):
    hi = x.astype(BF16).astype(F32)
    r1 = x - hi
    mid = r1.astype(BF16).astype(F32)
    lo = (r1 - mid).astype(BF16).astype(F32)
    return hi, mid, lo


def _fox_pack_body(q_ref, k_ref, cum_ref, qa_ref, ka_ref, *, n_heads):
    rows = q_ref.shape[0]
    lane = lax.broadcasted_iota(jnp.int32, (rows, LANES), 1)
    for h in range(n_heads):
        c = jnp.broadcast_to(cum_ref[:, h:h + 1], (rows, LANES)) * LOG2E
        qh, qm, ql = _split3(c)
        kh, km, kl = _split3(-c)
        q_aug = jnp.where(lane == 0, qh, jnp.where(lane == 1, qm, jnp.where(
            lane == 2, ql, jnp.where(lane < 6, 1.0, 0.0))))
        k_aug = jnp.where(lane < 3, 1.0, jnp.where(lane == 3, kh, jnp.where(
            lane == 4, km, jnp.where(lane == 5, kl, 0.0))))
        base = h * AUG_DIM
        qa_ref[:, base:base + HEAD_DIM] = q_ref[:, h * HEAD_DIM:(h + 1) * HEAD_DIM]
        ka_ref[:, base:base + HEAD_DIM] = k_ref[:, h * HEAD_DIM:(h + 1) * HEAD_DIM]
        qa_ref[:, base + HEAD_DIM:base + AUG_DIM] = q_aug.astype(BF16)
        ka_ref[:, base + HEAD_DIM:base + AUG_DIM] = k_aug.astype(BF16)


def _fox_pack(qk, cum_col, n_heads, ts):
    b, s, _ = qk.shape
    w = n_heads * HEAD_DIM
    wa = n_heads * AUG_DIM
    return pl.pallas_call(
        functools.partial(_fox_pack_body, n_heads=n_heads),
        grid=(b, s // ts),
        in_specs=[pl.BlockSpec((None, ts, w), lambda bi, i: (bi, i, 0)),
                  pl.BlockSpec((None, ts, w), lambda bi, i: (bi, i, 1)),
                  pl.BlockSpec((None, ts, LANES), lambda bi, i: (bi, i, 0))],
        out_specs=[pl.BlockSpec((None, ts, wa), lambda bi, i: (bi, i, 0)),
                   pl.BlockSpec((None, ts, wa), lambda bi, i: (bi, i, 0))],
        out_shape=[jax.ShapeDtypeStruct((b, s, wa), BF16)] * 2,
        compiler_params=_cparams("parallel", "parallel"),
        name="fox_pack",
    )(qk, qk, cum_col)


def _attn_body(q_ref, k_ref, vt_ref, *rest, tq, tk, n_heads, kdim, mode, heads_per_loop):
    o_ref = rest[-1]
    qi = pl.program_id(1)
    key0 = lax.broadcasted_iota(jnp.int32, (tk, tq), 0)
    qry = qi * tq + lax.broadcasted_iota(jnp.int32, (tk, tq), 1)
    def scores(h, j):
        off = pl.multiple_of(j * tk, tk)
        q = q_ref[:, h * kdim:(h + 1) * kdim]
        k = k_ref[pl.ds(off, tk), h * kdim:(h + 1) * kdim]
        return lax.dot_general(k, q, (((1,), (1,)), ((), ())), preferred_element_type=F32)

    def flush(h, j, p, alpha, acc):
        vt = vt_ref[j, h * HEAD_DIM:(h + 1) * HEAD_DIM, :]
        return alpha * acc + jnp.dot(vt, p, preferred_element_type=F32)

    def softmax_step(h, j, m, l, masked):
        s = scores(h, j)
        if mode == "dsa":
            s = s + rest[0][j]
        if masked:
            s = jnp.where(key0 + j * tk <= qry, s, NEG)
        m_new = jnp.maximum(m, jnp.max(s, axis=0, keepdims=True))
        alpha = jnp.exp2(m - m_new)
        p = jnp.exp2(s - m_new)
        l = alpha * l + jnp.sum(p, axis=0, keepdims=True)
        return p.astype(BF16), alpha, m_new, l

    last = qi
    for h0 in range(0, n_heads, heads_per_loop):
        heads = range(h0, h0 + heads_per_loop)

        def step(j, carries, masked=False, heads=heads):
            prev = jnp.maximum(j - 1, 0)
            out = []
            for h, (p, alpha, m, l, acc) in zip(heads, carries):
                acc = flush(h, prev, p, alpha, acc)
                p, alpha, m, l = softmax_step(h, j, m, l, masked)
                out.append((p, alpha, m, l, acc))
            return tuple(out)

        carries = tuple(
            (jnp.zeros((tk, tq), BF16), jnp.ones((1, tq), F32), jnp.full((1, tq), NEG, F32),
             jnp.zeros((1, tq), F32), jnp.zeros((HEAD_DIM, tq), F32)) for h in heads)
        n_loop = last if mode == "fox" else last + 1
        carries = lax.fori_loop(0, n_loop, step, carries)
        if mode == "fox":
            carries = step(last, carries, masked=True)
        for h, (p, alpha, m, l, acc) in zip(heads, carries):
            acc = flush(h, last, p, alpha, acc)
            o_ref[:, h * HEAD_DIM:(h + 1) * HEAD_DIM] = jnp.transpose(acc / l).astype(o_ref.dtype)


def _attention(q_arr, q_blk, k_arr, k_blk, vt_arr, v_blk, extras, mode, n_heads, kdim, tq, tk):
    b, s, _ = q_arr.shape
    w = n_heads * HEAD_DIM
    wk = n_heads * kdim
    in_specs = [pl.BlockSpec((None, tq, wk), lambda bi, qi: (bi, qi, q_blk)),
                pl.BlockSpec((None, s, wk), lambda bi, qi: (bi, 0, k_blk)),
                pl.BlockSpec((None, s // tk, w, tk), lambda bi, qi: (bi, 0, v_blk, 0))]
    if mode == "dsa":
        in_specs += [pl.BlockSpec((None, None, s // tk, tk, tq), lambda bi, qi: (bi, qi, 0, 0, 0))]
    return pl.pallas_call(
        functools.partial(_attn_body, tq=tq, tk=tk, n_heads=n_heads, kdim=kdim, mode=mode,
                          heads_per_loop=ATT_HEADS_PER_LOOP),
        grid=(b, s // tq),
        in_specs=in_specs,
        out_specs=pl.BlockSpec((None, tq, w), lambda bi, qi: (bi, qi, 0)),
        out_shape=jax.ShapeDtypeStruct((b, s, w), BF16),
        compiler_params=_cparams("parallel", "parallel"),
        name="attn_" + mode,
    )(q_arr, k_arr, vt_arr, *extras)


def _select_body(iq_ref, ika_ref, ikb_ref, wt_ref, o_ref, keys_ref, half_ref, *, tq, tk, n_chunks, topk):
    qi = pl.program_id(1)
    used = qi + 1
    key0 = lax.broadcasted_iota(jnp.int32, (tk, tq), 0)
    qry = qi * tq + lax.broadcasted_iota(jnp.int32, (tk, tq), 1)

    def score_chunk(c, _):
        off = pl.multiple_of(c * tk, tk)
        ka = ika_ref[pl.ds(off, tk), :]
        kb = ikb_ref[pl.ds(off, tk), :]
        acc = jnp.zeros((tk, tq), F32)
        for j in range(IDX_HEADS):
            pair = iq_ref[:, (j // 2) * LANES:(j // 2 + 1) * LANES]
            logit = lax.dot_general(ka if j % 2 == 0 else kb, pair, (((1,), (1,)), ((), ())),
                                    preferred_element_type=F32)
            acc = acc + wt_ref[j:j + 1, :] * jnp.maximum(logit, 0.0)
        bits = lax.bitcast_convert_type(acc, jnp.int32)
        key = jnp.where(bits < 0, bits ^ jnp.int32(0x7FFFFFFF), bits)
        key = jnp.where(key0 + off <= qry, key, jnp.int32(INT_MIN))
        keys_ref[c] = key
        half_ref[c] = lax.shift_right_arithmetic(key, 16).astype(jnp.int16)
        return 0

    lax.fori_loop(0, used, score_chunk, 0)

    one16 = jnp.int16(1)
    zero16 = jnp.int16(0)

    def count_ge(cand16):
        def count_chunk(c, cnt):
            partial = [None] * 4
            for i, r in enumerate(range(0, tk, 16)):
                ge = jnp.where(half_ref[c, r:r + 16, :] >= cand16, one16, zero16)
                partial[i % 4] = ge if partial[i % 4] is None else partial[i % 4] + ge
            return cnt + ((partial[0] + partial[1]) + (partial[2] + partial[3]))

        cnt = lax.fori_loop(0, used, count_chunk, jnp.zeros((16, tq), jnp.int16))
        return jnp.sum(cnt.astype(jnp.int32), axis=0, keepdims=True)

    def search16(need):
        def bit_step(i, carry):
            prefix, kept = carry
            cand_u = prefix | jnp.left_shift(jnp.int32(1), 15 - i)
            cnt = count_ge((cand_u - 32768).astype(jnp.int16))
            ok = cnt >= need
            return jnp.where(ok, cand_u, prefix), jnp.where(ok, cnt, kept)

        init = (jnp.zeros((1, tq), jnp.int32), jnp.full((1, tq), tk, jnp.int32) * used)
        prefix, kept = lax.fori_loop(0, 16, bit_step, init)
        return prefix - 32768, kept

    hi, _ = search16(topk)
    above = count_ge(jnp.minimum(hi + 1, 32767).astype(jnp.int16))
    above = jnp.where(hi >= 32767, 0, above)
    hi16 = hi.astype(jnp.int16)

    def low_chunk(c, _):
        low = (keys_ref[c] & jnp.int32(0xFFFF)) - 32768
        half_ref[c] = jnp.where(half_ref[c] == hi16, low.astype(jnp.int16), jnp.int16(-32768))
        return 0

    lax.fori_loop(0, used, low_chunk, 0)
    lo, at_or_above_lo = search16(topk - above)
    thr = lax.shift_left(hi, 16) | (lo + 32768)
    thr = jnp.maximum(thr, jnp.int32(INT_MIN + 1))
    tied = jnp.max(above + at_or_above_lo) > topk

    def count_keys(pred):
        def chunk(c, cnt):
            hit = jnp.where(pred(keys_ref[c]), 1.0, 0.0)
            return cnt + jnp.sum(hit.reshape(tk // 8, 8, tq), axis=0)

        cnt = lax.fori_loop(0, used, chunk, jnp.zeros((8, tq), F32))
        return jnp.sum(cnt, axis=0, keepdims=True)

    @pl.when(jnp.logical_not(tied))
    def _():
        def write_chunk(c, _):
            o_ref[c] = jnp.where(keys_ref[c] >= thr, 0.0, NEG)
            return 0

        lax.fori_loop(0, used, write_chunk, 0)

    @pl.when(tied)
    def _():
        need = topk - count_keys(lambda k: k > thr)
        earlier = jnp.where(lax.broadcasted_iota(jnp.int32, (tk, tk), 1)
                            < lax.broadcasted_iota(jnp.int32, (tk, tk), 0), 1.0, 0.0).astype(BF16)

        def write_chunk(c, seen):
            keys = keys_ref[c]
            tie = jnp.where(keys == thr, 1.0, 0.0)
            rank = jnp.dot(earlier, tie.astype(BF16), preferred_element_type=F32) + seen
            o_ref[c] = jnp.where(keys > thr, 0.0,
                                 jnp.where(keys == thr, jnp.where(rank < need, 0.0, NEG), NEG))
            return seen + jnp.sum(tie, axis=0, keepdims=True)

        lax.fori_loop(0, used, write_chunk, jnp.zeros((1, tq), F32))

    def fill_chunk(c, _):
        o_ref[c] = jnp.full((tk, tq), NEG, F32)
        return 0

    lax.fori_loop(used, n_chunks, fill_chunk, 0)


def _select_bias(iq, ika, ikb, wt, topk, tq, tk):
    b, s, _ = iq.shape
    n_chunks = s // tk
    return pl.pallas_call(
        functools.partial(_select_body, tq=tq, tk=tk, n_chunks=n_chunks, topk=topk),
        grid=(b, s // tq),
        in_specs=[pl.BlockSpec((None, tq, IDX_HEADS * IDX_DIM), lambda bi, qi: (bi, qi, 0)),
                  pl.BlockSpec((None, s, LANES), lambda bi, qi: (bi, 0, 0)),
                  pl.BlockSpec((None, s, LANES), lambda bi, qi: (bi, 0, 0)),
                  pl.BlockSpec((None, IDX_HEADS, tq), lambda bi, qi: (bi, 0, qi))],
        out_specs=pl.BlockSpec((None, None, n_chunks, tk, tq), lambda bi, qi: (bi, qi, 0, 0, 0)),
        out_shape=jax.ShapeDtypeStruct((b, s // tq, n_chunks, tk, tq), F32),
        scratch_shapes=[pltpu.VMEM((n_chunks, tk, tq), jnp.int32),
                        pltpu.VMEM((n_chunks, tk, tq), jnp.int16)],
        compiler_params=_cparams("parallel", "parallel"),
        name="index_select",
    )(iq, ika, ikb, wt)


def _cross_body(q_ref, k_ref, v_ref, o_ref, *, n_heads):
    for h in range(n_heads):
        hs = slice(h * HEAD_DIM, (h + 1) * HEAD_DIM)
        s = lax.dot_general(q_ref[:, hs], k_ref[:, hs], (((1,), (1,)), ((), ())),
                            preferred_element_type=F32)
        p = jnp.exp2(s - jnp.max(s, axis=-1, keepdims=True))
        l = jnp.sum(p, axis=-1, keepdims=True)
        o = jnp.dot(p.astype(BF16), v_ref[:, hs], preferred_element_type=F32)
        o_ref[:, hs] = (o / l).astype(o_ref.dtype)


def _cross_attention(q_arr, q_blk, mk, mv, tq):
    b, s, _ = q_arr.shape
    n_mem = mk.shape[1]
    w = CROSS_HEADS * HEAD_DIM
    return pl.pallas_call(
        functools.partial(_cross_body, n_heads=CROSS_HEADS),
        grid=(b, s // tq),
        in_specs=[pl.BlockSpec((None, tq, w), lambda bi, qi: (bi, qi, q_blk)),
                  pl.BlockSpec((None, n_mem, w), lambda bi, qi: (bi, 0, 0)),
                  pl.BlockSpec((None, n_mem, w), lambda bi, qi: (bi, 0, 0))],
        out_specs=pl.BlockSpec((None, tq, w), lambda bi, qi: (bi, qi, 0)),
        out_shape=jax.ShapeDtypeStruct((b, s, w), BF16),
        compiler_params=_cparams("parallel", "parallel"),
        name="cross_attn",
    )(q_arr, mk, mv)


def _sigmoid(x):
    return 1.0 / (1.0 + jnp.exp(-x))


def _merge_body(h_ref, of_ref, od_ref, oc_ref, x_ref, wgf_ref, wgd_ref, wgc_ref,
                wbf_ref, wbd_ref, wbc_ref, wo_ref, o_ref):
    j = pl.program_id(1)
    h = h_ref[...]
    tn = wo_ref.shape[0]
    sub = min(tn, MERGE_SUB)

    def gate_logits(wg_ref, cs):
        return lax.dot_general(h, wg_ref[cs, :], (((1,), (1,)), ((), ())), preferred_element_type=F32)

    part = None
    for c in range(tn // sub):
        cs = slice(c * sub, (c + 1) * sub)
        zf, zd, zc = gate_logits(wgf_ref, cs), gate_logits(wgd_ref, cs), gate_logits(wgc_ref, cs)
        bf = jnp.dot(of_ref[...], wbf_ref[:, cs], preferred_element_type=F32)
        bd = jnp.dot(od_ref[...], wbd_ref[:, cs], preferred_element_type=F32)
        bc = jnp.dot(oc_ref[...], wbc_ref[:, cs], preferred_element_type=F32)
        merged = (_sigmoid(zf) * bf + _sigmoid(zd) * bd) + _sigmoid(zc) * bc
        p = jnp.dot(merged.astype(BF16), wo_ref[cs, :], preferred_element_type=F32)
        part = p if part is None else part + p

    @pl.when(j == 0)
    def _():
        o_ref[...] = x_ref[...] + part

    @pl.when(j != 0)
    def _():
        o_ref[...] += part


def _merge(h, o_fox, o_dsa, o_cross, x, w_gates, gate_row0, wbf, wbd, wbc, w_out, tm, tn):
    m, d = x.shape
    nj = d // tn
    assert gate_row0 % tn == 0
    g0 = gate_row0 // tn
    row = lambda i, j: (i, 0)
    in_specs = [pl.BlockSpec((tm, d), row),
                pl.BlockSpec((tm, o_fox.shape[1]), row),
                pl.BlockSpec((tm, o_dsa.shape[1]), row),
                pl.BlockSpec((tm, o_cross.shape[1]), row),
                pl.BlockSpec((tm, d), row),
                pl.BlockSpec((tn, d), lambda i, j: (g0 + j, 0)),
                pl.BlockSpec((tn, d), lambda i, j: (g0 + nj + j, 0)),
                pl.BlockSpec((tn, d), lambda i, j: (g0 + 2 * nj + j, 0)),
                pl.BlockSpec((wbf.shape[0], tn), lambda i, j: (0, j)),
                pl.BlockSpec((wbd.shape[0], tn), lambda i, j: (0, j)),
                pl.BlockSpec((wbc.shape[0], tn), lambda i, j: (0, j)),
                pl.BlockSpec((tn, d), lambda i, j: (j, 0))]
    return pl.pallas_call(
        _merge_body,
        grid=(m // tm, nj),
        in_specs=in_specs,
        out_specs=pl.BlockSpec((tm, d), row),
        out_shape=jax.ShapeDtypeStruct((m, d), F32),
        compiler_params=_cparams("parallel", "arbitrary"),
        name="gated_merge",
    )(h, o_fox, o_dsa, o_cross, x, w_gates, w_gates, w_gates, wbf, wbd, wbc, w_out)


def _router_body(x_ref, g_ref, w_ref, b_ref, h_ref, eid_ref, gate_ref):
    x = x_ref[...]
    r = lax.rsqrt(jnp.mean(x * x, axis=-1, keepdims=True) + EPS)
    h = (x * r) * g_ref[...]
    hb = h.astype(BF16)
    bits = lax.bitcast_convert_type(hb.astype(F32), jnp.int32)
    half = bits.shape[1] // 2
    h_ref[...] = lax.shift_right_logical(bits[:, :half], 16) | (bits[:, half:] & jnp.int32(-65536))
    logits = jnp.dot(hb, w_ref[...], preferred_element_type=F32) + b_ref[...]
    lane = lax.broadcasted_iota(jnp.int32, logits.shape, 1).astype(F32)

    def masked_max(mask):
        return jnp.max(jnp.where(mask, logits, -jnp.inf), axis=-1, keepdims=True)

    def first_lane(mask):
        return jnp.min(jnp.where(mask, lane, float(LANES)), axis=-1, keepdims=True)

    is_group = lane < N_GROUPS
    gmax = masked_max(is_group)
    grp = first_lane(is_group & (logits == gmax))
    denom = jnp.sum(jnp.where(is_group, jnp.exp(logits - gmax), 0.0), axis=-1, keepdims=True)
    p_sel = 1.0 / denom
    lo = N_GROUPS + grp * EXPERTS_PER_GROUP
    in_grp = (lane >= lo) & (lane < lo + EXPERTS_PER_GROUP)
    e1 = masked_max(in_grp)
    j1 = first_lane(in_grp & (logits == e1))
    rest = in_grp & (lane != j1)
    e2 = masked_max(rest)
    j2 = first_lane(rest & (logits == e2))
    t = jnp.exp(e2 - e1)
    g1 = p_sel * (1.0 / (1.0 + t))
    g2 = p_sel * (t / (1.0 + t))
    eid = jnp.where(lane == 0.0, j1 - N_GROUPS, jnp.where(lane == 1.0, j2 - N_GROUPS, 0.0))
    eid_ref[...] = eid.astype(jnp.int32)
    gate_ref[...] = jnp.where(lane == 0.0, g1, jnp.where(lane == 1.0, g2, 0.0))


def _router(x, g, w_router, b_router, tm):
    m, d = x.shape
    return pl.pallas_call(
        _router_body,
        grid=(m // tm,),
        in_specs=[pl.BlockSpec((tm, d), lambda i: (i, 0)),
                  pl.BlockSpec((1, d), lambda i: (0, 0)),
                  pl.BlockSpec((d, LANES), lambda i: (0, 0)),
                  pl.BlockSpec((1, LANES), lambda i: (0, 0))],
        out_specs=[pl.BlockSpec((tm, d // 2), lambda i: (i, 0)),
                   pl.BlockSpec((tm, LANES), lambda i: (i, 0)),
                   pl.BlockSpec((tm, LANES), lambda i: (i, 0))],
        out_shape=[jax.ShapeDtypeStruct((m, d // 2), jnp.int32),
                   jax.ShapeDtypeStruct((m, LANES), jnp.int32),
                   jax.ShapeDtypeStruct((m, LANES), F32)],
        compiler_params=_cparams("parallel"),
        name="moe_router",
    )(x, g.reshape(1, d), w_router, b_router.reshape(1, LANES))


def _expert_body(ge_ref, gb0_ref, gnb_ref, used_ref, tok_ref, h_hbm, wg_ref, wu_ref, wd_ref, ys_hbm,
                 xs_ref, xb_ref, acc_ref, wgb_ref, wub_ref, wdb_ref, sem_in, sem_out,
                 *, bm, n_f, n_blocks):
    g = pl.program_id(0)
    f = pl.program_id(1)
    nb = gnb_ref[g]
    blk0 = gb0_ref[g]

    @pl.when(jnp.logical_and(g == 0, f == 0))
    def _():
        acc_ref[pl.ds(0, bm)] = jnp.zeros((bm,) + acc_ref.shape[1:], F32)

        def tail_copy(i):
            return pltpu.make_async_copy(acc_ref.at[pl.ds(0, bm)], ys_hbm.at[pl.ds(i * bm, bm)],
                                         sem_out)

        def start_tail(i, _):
            tail_copy(i).start()
            return 0

        lax.fori_loop(used_ref[0], n_blocks, start_tail, 0)

        def wait_tail(i, _):
            tail_copy(i).wait()
            return 0

        lax.fori_loop(used_ref[0], n_blocks, wait_tail, 0)

    def gather(grp):
        slot = grp % 2
        base = gb0_ref[grp] * bm

        def issue(r0, _):
            for u in range(DMA_ISSUE_UNROLL):
                r = r0 * DMA_ISSUE_UNROLL + u
                pltpu.make_async_copy(h_hbm.at[pl.ds(tok_ref[base + r], 1)],
                                      xs_ref.at[slot, pl.ds(r, 1)], sem_in.at[slot]).start(priority=1)
            return 0

        lax.fori_loop(0, gnb_ref[grp] * (bm // DMA_ISSUE_UNROLL), issue, 0)

    def block_copy_out(i):
        return pltpu.make_async_copy(acc_ref.at[pl.ds(i * bm, bm)],
                                     ys_hbm.at[pl.ds((blk0 + i) * bm, bm)], sem_out)

    @pl.when(jnp.logical_and(g == 0, f == 0))
    def _():
        gather(g)

    @pl.when(jnp.logical_and(nb > 0, f == 0))
    def _():
        slot = g % 2

        def land(i, _):
            rows = pl.ds(pl.multiple_of(i * bm, bm), bm)
            pltpu.make_async_copy(h_hbm.at[pl.ds(0, bm)], xs_ref.at[slot, rows], sem_in.at[slot]).wait()
            return 0

        lax.fori_loop(0, nb, land, 0)

        half = xb_ref.shape[1] // 2

        def unpack(i, _):
            rows = pl.ds(pl.multiple_of(i * bm, bm), bm)
            w = xs_ref[slot, rows, :]
            lo = lax.bitcast_convert_type(lax.shift_left(w, 16), F32)
            hi = lax.bitcast_convert_type(w & jnp.int32(-65536), F32)
            xb_ref[rows, :half] = lo.astype(BF16)
            xb_ref[rows, half:] = hi.astype(BF16)
            return 0

        lax.fori_loop(0, nb, unpack, 0)

    @pl.when(jnp.logical_and(f == 0, g + 1 < pl.num_programs(0)))
    def _():
        gather(g + 1)

    def wait_out(n):
        def body(i, _):
            pltpu.make_async_copy(acc_ref.at[pl.ds(0, bm)], ys_hbm.at[pl.ds(0, bm)], sem_out).wait()
            return 0

        lax.fori_loop(0, n, body, 0)

    @pl.when(jnp.logical_and(f == 0, g > 0))
    def _():
        wait_out(gnb_ref[jnp.maximum(g - 1, 0)])

    @pl.when(nb > 0)
    def _():
        wgb_ref[...] = wg_ref[...].astype(BF16)
        wub_ref[...] = wu_ref[...].astype(BF16)
        wdb_ref[...] = wd_ref[...].astype(BF16)

        def swiglu(rows):
            xb = xb_ref[rows, :]
            a = jnp.dot(xb, wgb_ref[...], preferred_element_type=F32)
            u = jnp.dot(xb, wub_ref[...], preferred_element_type=F32)
            hid = (a * _sigmoid(a)) * u
            y = jnp.dot(hid.astype(BF16), wdb_ref[...], preferred_element_type=F32)

            @pl.when(f == 0)
            def _():
                acc_ref[rows, :] = y

            @pl.when(f != 0)
            def _():
                acc_ref[rows, :] += y

        per = MOE_MM_ROWS // bm

        def full_tile(i, _):
            swiglu(pl.ds(pl.multiple_of(i * MOE_MM_ROWS, MOE_MM_ROWS), MOE_MM_ROWS))
            return 0

        lax.fori_loop(0, nb // per, full_tile, 0)
        piece = per // 2
        while piece >= 1:
            done = (nb // (2 * piece)) * (2 * piece)

            @pl.when((nb // piece) % 2 == 1)
            def _(done=done, piece=piece):
                swiglu(pl.ds(pl.multiple_of(done * bm, piece * bm), piece * bm))

            piece //= 2

    @pl.when(jnp.logical_and(nb > 0, f == n_f - 1))
    def _():
        def start_out(i, _):
            block_copy_out(i).start()
            return 0

        lax.fori_loop(0, nb, start_out, 0)

        @pl.when(g == pl.num_programs(0) - 1)
        def _():
            wait_out(nb)


def _experts(h2, slot_tok, grp_e, grp_blk0, grp_nb, n_used, w_gate, w_up, w_down, n_slots):
    d = w_gate.shape[1]
    ff = w_gate.shape[2]
    n_f = ff // MOE_FC
    n_groups = grp_e.shape[0]
    rmax = MOE_GROUP_BLOCKS * MOE_BM

    def f_idx(g, f, gnb):
        return jnp.where(gnb[g] > 0, f, n_f - 1)

    grid_spec = pltpu.PrefetchScalarGridSpec(
        num_scalar_prefetch=5,
        grid=(n_groups, n_f),
        in_specs=[
            pl.BlockSpec(memory_space=pl.ANY),
            pl.BlockSpec((None, d, MOE_FC), lambda g, f, ge, gb0, gnb, *_: (ge[g], 0, f_idx(g, f, gnb))),
            pl.BlockSpec((None, d, MOE_FC), lambda g, f, ge, gb0, gnb, *_: (ge[g], 0, f_idx(g, f, gnb))),
            pl.BlockSpec((None, MOE_FC, d), lambda g, f, ge, gb0, gnb, *_: (ge[g], f_idx(g, f, gnb), 0)),
        ],
        out_specs=pl.BlockSpec(memory_space=pl.ANY),
        scratch_shapes=[pltpu.VMEM((2, rmax, d // 2), jnp.int32),
                        pltpu.VMEM((rmax, d), BF16),
                        pltpu.VMEM((rmax, d), F32),
                        pltpu.VMEM((d, MOE_FC), BF16),
                        pltpu.VMEM((d, MOE_FC), BF16),
                        pltpu.VMEM((MOE_FC, d), BF16),
                        pltpu.SemaphoreType.DMA((2,)),
                        pltpu.SemaphoreType.DMA(())],
    )
    return pl.pallas_call(
        functools.partial(_expert_body, bm=MOE_BM, n_f=n_f, n_blocks=n_slots // MOE_BM),
        grid_spec=grid_spec,
        out_shape=jax.ShapeDtypeStruct((n_slots, d), F32),
        compiler_params=_cparams("arbitrary", "arbitrary"),
        name="moe_experts",
    )(grp_e, grp_blk0, grp_nb, n_used, slot_tok, h2, w_gate, w_up, w_down)


def _combine_body(pos_ref, x_ref, gate_ref, ys_hbm, o_ref, buf_ref, sem, *, tm):
    i = pl.program_id(0)
    n_steps = pl.num_programs(0)

    def issue(step):
        slot = step % 2

        def body(r0, _):
            for u in range(DMA_ISSUE_UNROLL):
                r = r0 * DMA_ISSUE_UNROLL + u
                for k in range(MOE_TOP_K):
                    src = pos_ref[(step * tm + r) * MOE_TOP_K + k]
                    pltpu.make_async_copy(ys_hbm.at[pl.ds(src, 1)], buf_ref.at[slot, k, pl.ds(r, 1)],
                                          sem.at[slot]).start()
            return 0

        lax.fori_loop(0, tm // DMA_ISSUE_UNROLL, body, 0)

    @pl.when(i == 0)
    def _():
        issue(i)

    @pl.when(i + 1 < n_steps)
    def _():
        issue(i + 1)

    slot = i % 2
    for k in range(MOE_TOP_K):
        pltpu.make_async_copy(ys_hbm.at[pl.ds(0, tm)], buf_ref.at[slot, k], sem.at[slot]).wait()
    o_ref[...] = x_ref[...] + (buf_ref[slot, 0] * gate_ref[:, 0:1] + buf_ref[slot, 1] * gate_ref[:, 1:2])


def _combine(x, gate, ys, pos, tm):
    m, d = x.shape
    grid_spec = pltpu.PrefetchScalarGridSpec(
        num_scalar_prefetch=1,
        grid=(m // tm,),
        in_specs=[pl.BlockSpec((tm, d), lambda i, pos: (i, 0)),
                  pl.BlockSpec((tm, LANES), lambda i, pos: (i, 0)),
                  pl.BlockSpec(memory_space=pl.ANY)],
        out_specs=pl.BlockSpec((tm, d), lambda i, pos: (i, 0)),
        scratch_shapes=[pltpu.VMEM((2, MOE_TOP_K, tm, d), F32), pltpu.SemaphoreType.DMA((2,))],
    )
    return pl.pallas_call(
        functools.partial(_combine_body, tm=tm),
        grid_spec=grid_spec,
        out_shape=jax.ShapeDtypeStruct((m, d), F32),
        compiler_params=_cparams("arbitrary"),
        name="moe_combine",
    )(pos, x, gate, ys)


PLAN_CHUNK = 256


def _positions_body(eid_ref, pos_ref, tok_ref, cnt_ref, prefix_ref, *, n_tokens):
    ch = PLAN_CHUNK
    lane = lax.broadcasted_iota(jnp.int32, (ch, LANES), 1)
    earlier = jnp.where(lax.broadcasted_iota(jnp.int32, (ch, ch), 1)
                        < lax.broadcasted_iota(jnp.int32, (ch, ch), 0), 1.0, 0.0).astype(BF16)

    def chosen(e, k):
        return jnp.where(lane == e[:, k:k + 1], 1.0, 0.0)

    def count_step(c, seen):
        rows = pl.ds(pl.multiple_of(c * ch, ch), ch)
        e = eid_ref[rows, :]
        both = chosen(e, 0) + chosen(e, 1)
        prefix_ref[rows, :] = jnp.dot(earlier, both.astype(BF16), preferred_element_type=F32) + seen
        return seen + jnp.sum(both, axis=0, keepdims=True)

    counts = lax.fori_loop(0, n_tokens // ch, count_step, jnp.zeros((1, LANES), F32))
    blocks = jnp.floor((counts + (MOE_BM - 1)) * (1.0 / MOE_BM))
    lower = jnp.where(lax.broadcasted_iota(jnp.int32, (LANES, LANES), 0)
                      < lax.broadcasted_iota(jnp.int32, (LANES, LANES), 1), 1.0, 0.0).astype(BF16)
    first_block = jnp.dot(jnp.broadcast_to(blocks, (8, LANES)).astype(BF16), lower,
                          preferred_element_type=F32)[0:1, :]
    first_row = first_block * MOE_BM

    slot_rows = tok_ref.shape[0]
    row_id = lax.broadcasted_iota(jnp.int32, (slot_rows, ch), 0).astype(F32)
    tok_row = lax.broadcasted_iota(jnp.int32, (ch, 1), 0)

    def place_step(c, acc):
        acc_hi, acc_lo = acc
        rows = pl.ds(pl.multiple_of(c * ch, ch), ch)
        e = eid_ref[rows, :]
        slot = prefix_ref[rows, :] + first_row
        tok = c * ch + tok_row
        tok_hi = lax.shift_right_logical(tok, 7).astype(F32)
        tok_lo = (tok & (LANES - 1)).astype(F32)
        p = []
        for k in range(MOE_TOP_K):
            pk = jnp.sum(chosen(e, k) * slot, axis=1, keepdims=True)
            p.append(pk)
            srow = jnp.floor(pk * (1.0 / LANES))
            scol = pk - srow * LANES
            srow_t = jnp.transpose(jnp.broadcast_to(srow, (ch, LANES)))[0:1, :]
            onehot_rows = jnp.where(row_id == srow_t, 1.0, 0.0).astype(BF16)
            in_lane = lane.astype(F32) == scol
            acc_hi = acc_hi + jnp.dot(onehot_rows, jnp.where(in_lane, tok_hi, 0.0).astype(BF16),
                                      preferred_element_type=F32)
            acc_lo = acc_lo + jnp.dot(onehot_rows, jnp.where(in_lane, tok_lo, 0.0).astype(BF16),
                                      preferred_element_type=F32)
        pos_ref[rows, :] = jnp.where(lane == 0, p[0], jnp.where(lane == 1, p[1], 0.0)).astype(jnp.int32)
        return acc_hi, acc_lo

    zero = jnp.zeros((slot_rows, LANES), F32)
    acc_hi, acc_lo = lax.fori_loop(0, n_tokens // ch, place_step, (zero, zero))
    tok_ref[...] = (acc_hi * LANES + acc_lo).astype(jnp.int32)
    cnt_ref[...] = jnp.broadcast_to(counts, cnt_ref.shape).astype(jnp.int32)


def _positions(eid, n_slots):
    n = eid.shape[0]
    assert n % PLAN_CHUNK == 0 and MOE_BM & (MOE_BM - 1) == 0
    assert n_slots % LANES == 0 and n <= LANES * LANES
    slot_rows = -(-(n_slots // LANES) // LANES) * LANES
    pos, tok, cnt = pl.pallas_call(
        functools.partial(_positions_body, n_tokens=n),
        out_shape=[jax.ShapeDtypeStruct((n, LANES), jnp.int32),
                   jax.ShapeDtypeStruct((slot_rows, LANES), jnp.int32),
                   jax.ShapeDtypeStruct((8, LANES), jnp.int32)],
        scratch_shapes=[pltpu.VMEM((n, LANES), F32)],
        compiler_params=pltpu.CompilerParams(vmem_limit_bytes=VMEM_LIMIT),
        name="moe_positions",
    )(eid)
    return pos, tok.reshape(-1)[:n_slots], cnt[0, :N_EXPERTS]


def _moe_plan(eid, n_tokens):
    a = n_tokens * MOE_TOP_K
    n_blocks = (a + N_EXPERTS * (MOE_BM - 1)) // MOE_BM
    n_slots = n_blocks * MOE_BM
    n_groups = n_blocks // MOE_GROUP_BLOCKS + N_EXPERTS
    pos, slot_tok, counts = _positions(eid, n_slots)
    pos = pos[:, :MOE_TOP_K].reshape(a)
    blocks_e = (counts + MOE_BM - 1) // MOE_BM
    blk_start = jnp.cumsum(blocks_e) - blocks_e
    groups_e = (blocks_e + MOE_GROUP_BLOCKS - 1) // MOE_GROUP_BLOCKS
    grp_end = jnp.cumsum(groups_e)
    grp_start = grp_end - groups_e
    gidx = jnp.arange(n_groups, dtype=jnp.int32)
    ge = jnp.minimum(jnp.searchsorted(grp_end, gidx, side="right"), N_EXPERTS - 1).astype(jnp.int32)
    local = gidx - grp_start[ge]
    active = gidx < grp_end[-1]
    gb0 = jnp.where(active, blk_start[ge] + local * MOE_GROUP_BLOCKS, 0).astype(jnp.int32)
    gnb = jnp.where(active, jnp.minimum(blocks_e[ge] - local * MOE_GROUP_BLOCKS, MOE_GROUP_BLOCKS),
                    0).astype(jnp.int32)
    last_e = ge[jnp.maximum(grp_end[-1] - 1, 0)]
    ge = jnp.where(active, ge, last_e).astype(jnp.int32)
    n_used = jnp.sum(blocks_e).astype(jnp.int32).reshape(1)
    return pos, slot_tok, ge, gb0, gnb, n_used, n_slots


def _layer(x, mem, positions, attn_norm_g, mem_norm_g, w_in, fox_forget_b, fox_q_norm_g,
           fox_k_norm_g, dsa_q_norm_g, dsa_k_norm_g, idx_k_norm_g, cross_q_norm_g, cross_k_norm_g,
           w_mem_kv, w_branch_fox, w_branch_dsa, w_branch_cross, w_out, ffn_norm_g,
           router_group_w, router_group_b, router_expert_w, router_expert_b,
           expert_w_gate, expert_w_up, expert_w_down):
    b, s, d = x.shape
    n = b * s
    n_mem = mem.shape[1]
    fw, dw, cw = FOX_HEADS * HEAD_DIM, DSA_HEADS * HEAD_DIM, CROSS_HEADS * HEAD_DIM
    iqw = IDX_HEADS * IDX_DIM
    tm = min(512, n)
    x2d = x.reshape(n, d)

    sizes = (fw, fw, fw, FOX_HEADS, dw, dw, dw, iqw, IDX_HEADS, IDX_DIM, cw, 3 * d)
    offs = [0]
    for sz in sizes:
        offs.append(offs[-1] + sz)
    (c_fq, c_fk, c_fv, c_fl, c_dq, c_dk, c_dv, c_iq, c_iw, c_ik, c_cq, c_g) = range(12)

    w_t = jnp.transpose(w_in, (2, 0, 1)).reshape(offs[-1], d)
    order = (c_dq, c_dk, c_fv, c_dv, c_iq, c_fq, c_fk, c_cq, c_g)
    row0, r = {}, 0
    for c in order:
        row0[c] = r
        r += sizes[c]
    w_all = jnp.concatenate([w_t[offs[c]:offs[c + 1]] for c in order], axis=0).astype(BF16)
    small_rows = [w_t[offs[c]:offs[c + 1]] for c in (c_ik, c_iw, c_fl)]
    small_rows.append(jnp.zeros((LANES - IDX_DIM - IDX_HEADS - FOX_HEADS, d), F32))
    w_small = jnp.concatenate(small_rows, axis=0).astype(BF16)
    q_unit = (HEAD_DIM ** -0.5) * LOG2E
    g_qk1 = jnp.concatenate([jnp.tile(fox_q_norm_g * q_unit, FOX_HEADS),
                             jnp.tile(fox_k_norm_g, FOX_HEADS),
                             jnp.tile(cross_q_norm_g * q_unit, CROSS_HEADS)])
    g_qk2 = jnp.concatenate([jnp.tile(dsa_q_norm_g * q_unit, DSA_HEADS),
                             jnp.tile(dsa_k_norm_g, DSA_HEADS)])
    g_ik = jnp.concatenate([idx_k_norm_g, jnp.zeros((LANES - IDX_DIM,), F32)])
    fb_lane = jnp.zeros((LANES,), F32).at[SMALL_F:SMALL_F + FOX_HEADS].set(fox_forget_b)

    pos_col = positions.reshape(n, 1).astype(F32)
    cd, sd, ci, si = _rope_tables(pos_col, tm)

    h = _rmsnorm(x2d, attn_norm_g, BF16, tm)
    tmp = min(PROJ_TM, n)
    qk1 = _proj(h, w_all, "norm", BF16, tm, 2 * fw + cw, gains=g_qk1, w_rows=(row0[c_fq], 2 * fw + cw))
    qk2 = _proj(h, w_all, "norm_rope", BF16, min(PROJ_ROPE_TM, n), 2 * dw, gains=g_qk2, tables=(cd, sd),
                w_rows=(row0[c_dq], 2 * dw))
    iq = _proj(h, w_all, "rope_idx", BF16, tmp, 1024, tables=(ci, si), w_rows=(row0[c_iq], iqw))
    small = _proj(h, w_small, "raw", F32, tm, LANES, w_rows=(0, LANES))
    ika, ikb, wl, lf_col = _small(small, g_ik, fb_lane, ci, si, tm)

    tq, tk = min(ATT_TQ, s), min(ATT_TK, s)
    assert tq == tk and s % tq == 0
    cum_col = _forget_cumsum(lf_col.reshape(b, s, LANES))
    qk1 = qk1.reshape(b, s, 2 * fw + cw)
    qk2 = qk2.reshape(b, s, 2 * dw)
    q_aug, k_aug = _fox_pack(qk1, cum_col, FOX_HEADS, min(512, s))
    vt = _proj_t(h, w_all, (row0[c_fv], fw + dw), tmp, 768, tk).reshape(b, s // tk, fw + dw, tk)
    o_fox = _attention(q_aug, 0, k_aug, 0, vt, 0, (), "fox", FOX_HEADS, AUG_DIM, tq, tk)

    topk = min(INDEX_TOPK, s // 4)
    wt = jnp.transpose(wl.reshape(b, s, LANES)[:, :, :IDX_HEADS], (0, 2, 1))
    bias = _select_bias(iq.reshape(b, s, iqw), ika.reshape(b, s, LANES), ikb.reshape(b, s, LANES),
                        wt, topk, tq, tk)
    o_dsa = _attention(qk2, 0, qk2, 1, vt, 1, (bias,), "dsa", DSA_HEADS, HEAD_DIM, tq, tk)

    tmm = min(512, b * n_mem)
    m_n = _rmsnorm(mem.reshape(b * n_mem, d), mem_norm_g, BF16, tmm)
    w_kv = w_mem_kv.astype(BF16)
    mk = _proj(m_n, w_kv[:, :cw], "norm", BF16, tmm, cw, gains=jnp.tile(cross_k_norm_g, CROSS_HEADS))
    mv = _proj(m_n, w_kv[:, cw:], "raw", BF16, tmm, cw)
    o_cross = _cross_attention(qk1, (2 * fw) // cw, mk.reshape(b, n_mem, cw), mv.reshape(b, n_mem, cw),
                               min(512, s))

    x_mid = _merge(h, o_fox.reshape(n, fw), o_dsa.reshape(n, dw), o_cross.reshape(n, cw), x2d,
                   w_all, row0[c_g], w_branch_fox.astype(BF16), w_branch_dsa.astype(BF16),
                   w_branch_cross.astype(BF16), w_out.astype(BF16), tm, MERGE_TN)

    w_router = jnp.concatenate(
        [router_group_w, router_expert_w, jnp.zeros((d, LANES - N_GROUPS - N_EXPERTS), F32)],
        axis=1).astype(BF16)
    b_router = jnp.concatenate(
        [router_group_b, router_expert_b, jnp.zeros((LANES - N_GROUPS - N_EXPERTS,), F32)])
    h2, eid, gate = _router(x_mid, ffn_norm_g, w_router, b_router, tm)
    pos, slot_tok, ge, gb0, gnb, n_used, n_slots = _moe_plan(eid, n)
    ys = _experts(h2, slot_tok, ge, gb0, gnb, n_used, expert_w_gate, expert_w_up, expert_w_down,
                  n_slots)
    out = _combine(x_mid, gate, ys, pos, min(256, n))
    return out.reshape(b, s, d)


def kernel(x, mem, positions, attn_norm_g, mem_norm_g, w_in, fox_forget_b, fox_q_norm_g, fox_k_norm_g, dsa_q_norm_g, dsa_k_norm_g, idx_k_norm_g, cross_q_norm_g, cross_k_norm_g, w_mem_kv, w_branch_fox, w_branch_dsa, w_branch_cross, w_out, ffn_norm_g, router_group_w, router_group_b, router_expert_w, router_expert_b, expert_w_gate, expert_w_up, expert_w_down):
    depth = w_in.shape[0]
    for layer in range(depth):
        x = _layer(
            x, mem, positions, attn_norm_g[layer], mem_norm_g[layer], w_in[layer:layer + 1],
            fox_forget_b[layer], fox_q_norm_g[layer], fox_k_norm_g[layer],
            dsa_q_norm_g[layer], dsa_k_norm_g[layer], idx_k_norm_g[layer],
            cross_q_norm_g[layer], cross_k_norm_g[layer], w_mem_kv[layer],
            w_branch_fox[layer], w_branch_dsa[layer], w_branch_cross[layer], w_out[layer],
            ffn_norm_g[layer], router_group_w[layer], router_group_b[layer],
            router_expert_w[layer], router_expert_b[layer], expert_w_gate[layer],
            expert_w_up[layer], expert_w_down[layer])
    return x
```
